```python
import jax, jax.numpy as jnp
from jax import lax
import numpy as np

D_MODEL = 1024
BATCH = 8
SEQ = 2048
DEPTH = 2

N_MIXERS = 2
N_LAYERS_A = (DEPTH + 1) // 2
N_LAYERS_B = DEPTH // 2
PLE_DIM = 256
CONV_A_WIDTH = 3
D_RNN = D_MODEL
LRU_BLOCK = 256
N_LRU_BLOCKS = D_RNN // LRU_BLOCK
CONV_B_WIDTH = 4
RG_C = 8.0
N_GROUPS = 4
EXPERTS_PER_GROUP = 8
N_EXPERTS = N_GROUPS * EXPERTS_PER_GROUP
TOP_K = 2
D_EXPERT = 512
ROUTE_BLOCK = 128
EPS = 1e-6

kernel_name = 'hybrid_shortconv_rglru_hmoe_encoder'


def rmsnorm(x, g):
    x32 = x.astype(jnp.float32)
    y = x32 * lax.rsqrt(jnp.mean(x32 * x32, axis=-1, keepdims=True) + EPS)
    return (y * g.astype(jnp.float32)).astype(x.dtype)


def depthwise_conv(x, w, pad):
    return lax.conv_general_dilated(
        x, w[:, None, :], window_strides=(1,), padding=[pad],
        dimension_numbers=('NWC', 'WIO', 'NWC'), feature_group_count=x.shape[-1])


def short_conv_mixer(xn, w_in, conv_w, w_out):
    bgate, cgate, h = jnp.split(xn @ w_in, 3, axis=-1)
    u = depthwise_conv(cgate * h, conv_w, (1, 1))
    return (bgate * u) @ w_out


def _linear_combine(earlier, later):
    a_e, b_e = earlier
    a_l, b_l = later
    return a_l * a_e, a_l * b_e + b_l


def rglru_direction(uh, w_r, b_r, w_i, b_i, lam, reverse):
    bsz, s, nh, dh = uh.shape
    u = uh.reshape(bsz, s, nh * dh)
    r = jax.nn.sigmoid((jnp.einsum('bshd,hde->bshe', uh, w_r).reshape(bsz, s, -1) + b_r).astype(jnp.float32))
    i = jax.nn.sigmoid((jnp.einsum('bshd,hde->bshe', uh, w_i).reshape(bsz, s, -1) + b_i).astype(jnp.float32))
    log_a = -RG_C * r * jax.nn.softplus(-lam.astype(jnp.float32))
    a = jnp.exp(log_a)
    gated = jnp.sqrt(-jnp.expm1(2.0 * log_a)) * (i * u.astype(jnp.float32))
    _, h = lax.associative_scan(_linear_combine, (a, gated), axis=1, reverse=reverse)
    return h.astype(uh.dtype)


def rglru_mixer(xn, w_in, conv_w, conv_bias, w_rg, b_rg, w_ig, b_ig, lam, w_out):
    y_branch, u = jnp.split(xn @ w_in, 2, axis=-1)
    y_branch = jax.nn.gelu(y_branch)
    u = depthwise_conv(u, conv_w, (2, 1)) + conv_bias
    bsz, s, _ = u.shape
    uh = u.reshape(bsz, s, N_LRU_BLOCKS, LRU_BLOCK)
    h = (rglru_direction(uh, w_rg[0], b_rg[0], w_ig[0], b_ig[0], lam[0], False)
         + rglru_direction(uh, w_rg[1], b_rg[1], w_ig[1], b_ig[1], lam[1], True))
    return (y_branch * h) @ w_out


def hier_moe(x, w_rg, b_rg, w_re, b_re, w_gate, w_up, w_down):
    bsz, s, d = x.shape
    n = bsz * s
    xf = x.reshape(n, d)
    pg = jax.nn.softmax((xf @ w_rg).astype(jnp.float32) + b_rg.astype(jnp.float32), axis=-1)
    pg_top, g_top = lax.top_k(pg, 1)
    le = ((xf @ w_re).astype(jnp.float32) + b_re.astype(jnp.float32)).reshape(n, N_GROUPS, EXPERTS_PER_GROUP)
    le_sel = jnp.take_along_axis(le, g_top[:, :, None], axis=1)[:, 0]
    pe = jax.nn.softmax(le_sel, axis=-1)
    top_p, top_i = lax.top_k(pe, TOP_K)
    top_p = top_p / jnp.sum(top_p, axis=-1, keepdims=True)
    expert = g_top * EXPERTS_PER_GROUP + top_i
    weight = pg_top * top_p

    n_slots = n * TOP_K
    e_flat = expert.reshape(-1)
    tok_flat = jnp.arange(n_slots, dtype=jnp.int32) // TOP_K
    w_flat = weight.reshape(-1)
    order = jnp.argsort(e_flat)
    e_sorted = e_flat[order]
    counts = jnp.bincount(e_flat, length=N_EXPERTS)
    start = jnp.cumsum(counts) - counts
    padded = ((counts + ROUTE_BLOCK - 1) // ROUTE_BLOCK) * ROUTE_BLOCK
    pend = jnp.cumsum(padded)
    pstart = pend - padded
    dest = pstart[e_sorted] + (jnp.arange(n_slots) - start[e_sorted])
    n_blocks = -(-n_slots // ROUTE_BLOCK) + N_EXPERTS
    slot_of_pos = jnp.full((n_blocks * ROUTE_BLOCK,), n_slots, jnp.int32).at[dest].set(order.astype(jnp.int32))
    tok_pad = jnp.concatenate([tok_flat, jnp.array([n], jnp.int32)])[slot_of_pos]
    w_pad = jnp.concatenate([w_flat, jnp.zeros((1,), jnp.float32)])[slot_of_pos]
    block_expert = jnp.clip(jnp.searchsorted(pend, jnp.arange(n_blocks) * ROUTE_BLOCK, side='right'), 0, N_EXPERTS - 1)
    x_pad = jnp.concatenate([xf, jnp.zeros((1, d), xf.dtype)], axis=0)

    def expert_block(args):
        idx, e = args
        xb = x_pad[idx]
        hb = jax.nn.silu(xb @ w_gate[e]) * (xb @ w_up[e])
        return hb @ w_down[e]

    y_blocks = lax.map(expert_block, (tok_pad.reshape(n_blocks, ROUTE_BLOCK), block_expert))
    contrib = y_blocks.reshape(-1, d) * w_pad[:, None].astype(x.dtype)
    y = jnp.zeros((n + 1, d), x.dtype).at[tok_pad].add(contrib)
    return y[:n].reshape(bsz, s, d)


def _normal(key, shape, fan_in):
    return jax.random.normal(key, shape, jnp.float32) * (fan_in ** -0.5)


def setup_inputs(seed: int = 0) -> dict:
    key = jax.random.key(seed)
    ks = jax.random.split(key, 28)
    f32 = jnp.float32
    D, R, H, Bk = D_MODEL, D_RNN, N_LRU_BLOCKS, LRU_BLOCK
    x = jax.random.normal(ks[0], (BATCH, SEQ, D), f32)
    p = jax.random.normal(ks[1], (DEPTH, BATCH, SEQ, PLE_DIM), f32)
    norm_mix = 1.0 + 0.05 * jax.random.normal(ks[2], (DEPTH, D), f32)
    w_in_a = _normal(ks[3], (N_LAYERS_A, D, 3 * D), D)
    conv_a = _normal(ks[4], (N_LAYERS_A, CONV_A_WIDTH, D), CONV_A_WIDTH)
    w_out_a = _normal(ks[5], (N_LAYERS_A, D, D), D)
    w_in_b = _normal(ks[6], (N_LAYERS_B, D, 2 * R), D)
    conv_b = _normal(ks[7], (N_LAYERS_B, CONV_B_WIDTH, R), CONV_B_WIDTH)
    conv_bias_b = 0.01 * jax.random.normal(ks[8], (N_LAYERS_B, R), f32)
    w_rgate_b = _normal(ks[9], (N_LAYERS_B, 2, H, Bk, Bk), Bk)
    b_rgate_b = 0.01 * jax.random.normal(ks[10], (N_LAYERS_B, 2, R), f32)
    w_igate_b = _normal(ks[11], (N_LAYERS_B, 2, H, Bk, Bk), Bk)
    b_igate_b = 0.01 * jax.random.normal(ks[12], (N_LAYERS_B, 2, R), f32)
    a0 = jax.random.uniform(ks[13], (N_LAYERS_B, 2, R), f32, minval=0.9, maxval=0.999)
    sig = a0 ** (1.0 / RG_C)
    lam_b = jnp.log(sig) - jnp.log1p(-sig)
    w_out_b = _normal(ks[14], (N_LAYERS_B, R, D), R)
    norm_ffn = 1.0 + 0.05 * jax.random.normal(ks[15], (DEPTH, D), f32)
    w_router_group = _normal(ks[16], (DEPTH, D, N_GROUPS), D)
    b_router_group = 0.01 * jax.random.normal(ks[17], (DEPTH, N_GROUPS), f32)
    w_router_expert = _normal(ks[18], (DEPTH, D, N_EXPERTS), D)
    b_router_expert = 0.01 * jax.random.normal(ks[19], (DEPTH, N_EXPERTS), f32)
    w_exp_gate = _normal(ks[20], (DEPTH, N_EXPERTS, D, D_EXPERT), D)
    w_exp_up = _normal(ks[21], (DEPTH, N_EXPERTS, D, D_EXPERT), D)
    w_exp_down = _normal(ks[22], (DEPTH, N_EXPERTS, D_EXPERT, D), D_EXPERT)
    norm_ple = 1.0 + 0.05 * jax.random.normal(ks[23], (DEPTH, D), f32)
    w_ple = _normal(ks[24], (DEPTH, PLE_DIM, D), PLE_DIM)
    w_ple_gate = _normal(ks[25], (DEPTH, D, D), D)
    b_ple_gate = 0.01 * jax.random.normal(ks[26], (DEPTH, D), f32)
    norm_final = 1.0 + 0.05 * jax.random.normal(ks[27], (D,), f32)
    return {'x': x, 'p': p, 'norm_mix': norm_mix,
            'w_in_a': w_in_a, 'conv_a': conv_a, 'w_out_a': w_out_a,
            'w_in_b': w_in_b, 'conv_b': conv_b, 'conv_bias_b': conv_bias_b,
            'w_rgate_b': w_rgate_b, 'b_rgate_b': b_rgate_b,
            'w_igate_b': w_igate_b, 'b_igate_b': b_igate_b,
            'lam_b': lam_b, 'w_out_b': w_out_b, 'norm_ffn': norm_ffn,
            'w_router_group': w_router_group, 'b_router_group': b_router_group,
            'w_router_expert': w_router_expert, 'b_router_expert': b_router_expert,
            'w_exp_gate': w_exp_gate, 'w_exp_up': w_exp_up, 'w_exp_down': w_exp_down,
            'norm_ple': norm_ple, 'w_ple': w_ple, 'w_ple_gate': w_ple_gate,
            'b_ple_gate': b_ple_gate, 'norm_final': norm_final}


def reference(x, p, norm_mix, w_in_a, conv_a, w_out_a, w_in_b, conv_b, conv_bias_b,
              w_rgate_b, b_rgate_b, w_igate_b, b_igate_b, lam_b, w_out_b, norm_ffn,
              w_router_group, b_router_group, w_router_expert, b_router_expert,
              w_exp_gate, w_exp_up, w_exp_down, norm_ple, w_ple, w_ple_gate,
              b_ple_gate, norm_final):
    h = x
    for i in range(DEPTH):
        xn = rmsnorm(h, norm_mix[i])
        j = i // N_MIXERS
        if i % N_MIXERS == 0:
            mix = short_conv_mixer(xn, w_in_a[j], conv_a[j], w_out_a[j])
        else:
            mix = rglru_mixer(xn, w_in_b[j], conv_b[j], conv_bias_b[j], w_rgate_b[j], b_rgate_b[j],
                              w_igate_b[j], b_igate_b[j], lam_b[j], w_out_b[j])
        h = h + mix
        h = h + hier_moe(rmsnorm(h, norm_ffn[i]), w_router_group[i], b_router_group[i],
                         w_router_expert[i], b_router_expert[i],
                         w_exp_gate[i], w_exp_up[i], w_exp_down[i])
        gate = jax.nn.sigmoid(rmsnorm(h, norm_ple[i]) @ w_ple_gate[i] + b_ple_gate[i])
        h = h + gate * (p[i] @ w_ple[i])
    return rmsnorm(h, norm_final)
```

```python
import functools

import jax
import jax.numpy as jnp
from jax import lax
from jax.experimental import pallas as pl
from jax.experimental.pallas import tpu as pltpu

F32 = jnp.float32
BF16 = jnp.bfloat16
I32 = jnp.int32

EPS = 1e-6
N_GROUPS = 4
EXPERTS_PER_GROUP = 8
N_EXPERTS = N_GROUPS * EXPERTS_PER_GROUP
RG_C = 8.0

LANES = 128
SUBLANES = 8
BF16_ROWS = 16
EXPERT_LANE0 = N_GROUPS
ROUTE_ROWS = 256
M_E0, M_E1, M_R0, M_R1, M_W0, M_W1 = 0, 1, 2, 3, 4, 5
VMEM_LIMIT = 48 * 1024 * 1024


def _cparams(*sem):
    return pltpu.CompilerParams(dimension_semantics=sem, vmem_limit_bytes=VMEM_LIMIT)


def _rms(x, g):
    return x * lax.rsqrt(jnp.mean(x * x, axis=-1, keepdims=True) + EPS) * g


def _dot(a, b):
    return jnp.dot(a, b, preferred_element_type=F32)


def _tile(n, want):
    t = min(n, want)
    assert n % t == 0, (n, want)
    return t


def _mix_a_in_kernel(h_ref, g_ref, w_ref, bg_ref, v_ref):
    d = h_ref.shape[1]
    xn = _rms(h_ref[...], g_ref[...]).astype(BF16)
    bg = _dot(xn, w_ref[:, 0:d])
    cg = _dot(xn, w_ref[:, d:2 * d])
    hh = _dot(xn, w_ref[:, 2 * d:3 * d])
    bg_ref[...] = bg.astype(BF16)
    v_ref[...] = (cg * hh).astype(BF16)


def _mix_a_in(h, g, w_in):
    n, d = h.shape
    tm = _tile(n, 512)
    return pl.pallas_call(
        _mix_a_in_kernel,
        grid=(n // tm,),
        in_specs=[pl.BlockSpec((tm, d), lambda i: (i, 0)),
                  pl.BlockSpec((1, d), lambda i: (0, 0)),
                  pl.BlockSpec((d, 3 * d), lambda i: (0, 0))],
        out_specs=[pl.BlockSpec((tm, d), lambda i: (i, 0)),
                   pl.BlockSpec((tm, d), lambda i: (i, 0))],
        out_shape=[jax.ShapeDtypeStruct((n, d), BF16), jax.ShapeDtypeStruct((n, d), BF16)],
        compiler_params=_cparams("parallel"),
        name="mix_a_in",
    )(h, g, w_in)


def _mix_a_conv_kernel(v_ref, vp_ref, vn_ref, bg_ref, cw_ref, z_ref, *, tiles_per_seq):
    i = pl.program_id(0)
    tm = v_ref.shape[0]
    v = v_ref[...].astype(F32)
    at_start = (i % tiles_per_seq) == 0
    at_end = (i % tiles_per_seq) == tiles_per_seq - 1
    prev_row = vp_ref[...].astype(F32)[BF16_ROWS - 1:BF16_ROWS, :]
    next_row = vn_ref[...].astype(F32)[0:1, :]
    prev_row = jnp.where(at_start, 0.0, prev_row)
    next_row = jnp.where(at_end, 0.0, next_row)
    row = lax.broadcasted_iota(I32, (tm, 1), 0)
    v_dn = jnp.where(row == 0, prev_row, pltpu.roll(v, 1, 0))
    v_up = jnp.where(row == tm - 1, next_row, pltpu.roll(v, tm - 1, 0))
    cw = cw_ref[...]
    u = cw[0:1, :] * v_dn + cw[1:2, :] * v + cw[2:3, :] * v_up
    z_ref[...] = (bg_ref[...].astype(F32) * u).astype(BF16)


def _mix_a_conv(v, bg, conv_w, seq):
    n, d = v.shape
    tm = _tile(seq, 512)
    hb = tm // BF16_ROWS
    nhalo = n // BF16_ROWS
    return pl.pallas_call(
        functools.partial(_mix_a_conv_kernel, tiles_per_seq=seq // tm),
        grid=(n // tm,),
        in_specs=[pl.BlockSpec((tm, d), lambda i: (i, 0)),
                  pl.BlockSpec((BF16_ROWS, d), lambda i: (jnp.maximum(i * hb - 1, 0), 0)),
                  pl.BlockSpec((BF16_ROWS, d), lambda i: (jnp.minimum((i + 1) * hb, nhalo - 1), 0)),
                  pl.BlockSpec((tm, d), lambda i: (i, 0)),
                  pl.BlockSpec(conv_w.shape, lambda i: (0, 0))],
        out_specs=pl.BlockSpec((tm, d), lambda i: (i, 0)),
        out_shape=jax.ShapeDtypeStruct((n, d), BF16),
        compiler_params=_cparams("parallel"),
        name="mix_a_conv",
    )(v, v, v, bg, conv_w)


def _mix_b_in_kernel(h_ref, g_ref, w_ref, yb_ref, u_ref):
    r = yb_ref.shape[1]
    xn = _rms(h_ref[...], g_ref[...]).astype(BF16)
    yb = _dot(xn, w_ref[:, 0:r])
    yb_ref[...] = jax.nn.gelu(yb).astype(BF16)
    u_ref[...] = _dot(xn, w_ref[:, r:2 * r])


def _mix_b_in(h, g, w_in):
    n, d = h.shape
    r = w_in.shape[1] // 2
    tm = _tile(n, 512)
    return pl.pallas_call(
        _mix_b_in_kernel,
        grid=(n // tm,),
        in_specs=[pl.BlockSpec((tm, d), lambda i: (i, 0)),
                  pl.BlockSpec((1, d), lambda i: (0, 0)),
                  pl.BlockSpec((d, 2 * r), lambda i: (0, 0))],
        out_specs=[pl.BlockSpec((tm, r), lambda i: (i, 0)),
                   pl.BlockSpec((tm, r), lambda i: (i, 0))],
        out_shape=[jax.ShapeDtypeStruct((n, r), BF16), jax.ShapeDtypeStruct((n, r), F32)],
        compiler_params=_cparams("parallel"),
        name="mix_b_in",
    )(h, g, w_in)


def _rglru_kernel(u_ref, yb_ref, cw_ref, cb_ref, wr_ref, br_ref, wi_ref, bi_ref, lam_ref,
                  z_ref, a_s, g_s, c_s, h_s):
    s, c = u_ref.shape
    nchunk = s // SUBLANES
    x = u_ref[...]
    row = lax.broadcasted_iota(I32, (s, 1), 0)
    sub = row & (SUBLANES - 1)
    x_m1 = jnp.where(row >= 1, pltpu.roll(x, 1, 0), 0.0)
    x_m2 = jnp.where(row >= 2, pltpu.roll(x, 2, 0), 0.0)
    x_p1 = jnp.where(row <= s - 2, pltpu.roll(x, s - 1, 0), 0.0)
    cw = cw_ref[...]
    u = cw[0:1, :] * x_m2 + cw[1:2, :] * x_m1 + cw[2:3, :] * x + cw[3:4, :] * x_p1 + cb_ref[...]
    ub = u.astype(BF16)

    for d in range(2):
        reverse = d == 1
        r = jax.nn.sigmoid(_dot(ub, wr_ref[d]) + br_ref[d:d + 1, :])
        ig = jax.nn.sigmoid(_dot(ub, wi_ref[d]) + bi_ref[d:d + 1, :])
        log_a = -RG_C * r * jax.nn.softplus(-lam_ref[d:d + 1, :])
        a = jnp.exp(log_a)
        th = jnp.tanh(log_a)
        g = jnp.sqrt(-2.0 * th / (1.0 - th)) * (ig * u)
        for st in (1, 2, 4):
            if not reverse:
                keep = sub >= st
                a_sh = pltpu.roll(a, st, 0)
                g_sh = pltpu.roll(g, st, 0)
            else:
                keep = sub < SUBLANES - st
                a_sh = pltpu.roll(a, s - st, 0)
                g_sh = pltpu.roll(g, s - st, 0)
            g = g + a * jnp.where(keep, g_sh, 0.0)
            a = a * jnp.where(keep, a_sh, 1.0)
        a_s[...] = a
        g_s[...] = g
        edge = 0 if reverse else SUBLANES - 1

        def chain(k, hp, reverse=reverse, edge=edge):
            kk = nchunk - 1 - k if reverse else k
            c_s[pl.ds(kk, 1), :] = hp
            base = pl.multiple_of(kk * SUBLANES, SUBLANES)
            return a_s[pl.ds(base + edge, 1), :] * hp + g_s[pl.ds(base + edge, 1), :]

        lax.fori_loop(0, nchunk, chain, jnp.zeros((1, c), F32))

        def expand(k, carry, reverse=reverse):
            base = pl.multiple_of(k * SUBLANES, SUBLANES)
            hk = a_s[pl.ds(base, SUBLANES), :] * c_s[pl.ds(k, 1), :] + g_s[pl.ds(base, SUBLANES), :]
            if reverse:
                h_s[pl.ds(base, SUBLANES), :] = h_s[pl.ds(base, SUBLANES), :] + hk
            else:
                h_s[pl.ds(base, SUBLANES), :] = hk
            return carry

        lax.fori_loop(0, nchunk, expand, 0)

    z_ref[...] = (yb_ref[...].astype(F32) * h_s[...]).astype(BF16)


def _rglru(u_raw, yb, conv_w, conv_b, w_r, b_r, w_i, b_i, lam, batch, seq):
    n, r = u_raw.shape
    nh, bk = w_r.shape[1], w_r.shape[2]
    kw = conv_w.shape[0]
    return pl.pallas_call(
        _rglru_kernel,
        grid=(batch, nh),
        in_specs=[pl.BlockSpec((seq, bk), lambda b, h: (b, h)),
                  pl.BlockSpec((seq, bk), lambda b, h: (b, h)),
                  pl.BlockSpec((kw, bk), lambda b, h: (0, h)),
                  pl.BlockSpec((1, bk), lambda b, h: (0, h)),
                  pl.BlockSpec((2, None, bk, bk), lambda b, h: (0, h, 0, 0)),
                  pl.BlockSpec((2, bk), lambda b, h: (0, h)),
                  pl.BlockSpec((2, None, bk, bk), lambda b, h: (0, h, 0, 0)),
                  pl.BlockSpec((2, bk), lambda b, h: (0, h)),
                  pl.BlockSpec((2, bk), lambda b, h: (0, h))],
        out_specs=pl.BlockSpec((seq, bk), lambda b, h: (b, h)),
        out_shape=jax.ShapeDtypeStruct((n, r), BF16),
        scratch_shapes=[pltpu.VMEM((seq, bk), F32), pltpu.VMEM((seq, bk), F32),
                        pltpu.VMEM((seq // SUBLANES, bk), F32), pltpu.VMEM((seq, bk), F32)],
        compiler_params=_cparams("parallel", "parallel"),
        name="rglru",
    )(u_raw, yb, conv_w, conv_b, w_r, b_r, w_i, b_i, lam)


def _post_mix_kernel(z_ref, wo_ref, h_ref, g_ref, wr_ref, br_ref,
                     h1_ref, xn_ref, meta_ref, meta_t_ref, cnt_ref, carry_ref):
    i = pl.program_id(0)
    tm = z_ref.shape[0]

    @pl.when(i == 0)
    def _():
        carry_ref[...] = jnp.zeros_like(carry_ref)

    h1 = h_ref[...] + _dot(z_ref[...], wo_ref[...])
    h1_ref[...] = h1
    xn = _rms(h1, g_ref[...])
    xn_ref[...] = xn

    x_hi = xn.astype(BF16)
    x_lo = (xn - x_hi.astype(F32)).astype(BF16)
    w = wr_ref[...]
    w_hi = w.astype(BF16)
    w_lo = (w - w_hi.astype(F32)).astype(BF16)
    logits = _dot(x_hi, w_hi) + _dot(x_hi, w_lo) + _dot(x_lo, w_hi) + br_ref[...]

    lane = lax.broadcasted_iota(I32, (tm, LANES), 1)
    neg = jnp.float32(-jnp.inf)
    gmask = lane < N_GROUPS
    gmax = jnp.max(jnp.where(gmask, logits, neg), axis=1, keepdims=True)
    eg = jnp.where(gmask, jnp.exp(logits - gmax), 0.0)
    pg = eg / jnp.sum(eg, axis=1, keepdims=True)
    pg_top = jnp.max(pg, axis=1, keepdims=True)
    g_top = jnp.min(jnp.where(gmask & (pg == pg_top), lane, LANES), axis=1, keepdims=True)
    lo = EXPERT_LANE0 + g_top * EXPERTS_PER_GROUP
    emask = (lane >= lo) & (lane < lo + EXPERTS_PER_GROUP)
    emax = jnp.max(jnp.where(emask, logits, neg), axis=1, keepdims=True)
    ee = jnp.where(emask, jnp.exp(logits - emax), 0.0)
    pe = jnp.where(emask, ee / jnp.sum(ee, axis=1, keepdims=True), -1.0)
    p1 = jnp.max(pe, axis=1, keepdims=True)
    i1 = jnp.min(jnp.where(pe == p1, lane, LANES), axis=1, keepdims=True)
    pe2 = jnp.where(lane == i1, -1.0, pe)
    p2 = jnp.max(pe2, axis=1, keepdims=True)
    i2 = jnp.min(jnp.where(pe2 == p2, lane, LANES), axis=1, keepdims=True)
    psum = p1 + p2
    w0 = pg_top * (p1 / psum)
    w1 = pg_top * (p2 / psum)

    oh0 = lane == i1
    oh1 = lane == i2
    both = (oh0 | oh1).astype(BF16)
    ri = lax.broadcasted_iota(I32, (tm, tm), 0)
    ci = lax.broadcasted_iota(I32, (tm, tm), 1)
    before = (ri > ci).astype(BF16)
    cnt_before = _dot(before, both) + carry_ref[...]
    rank0 = jnp.sum(jnp.where(oh0, cnt_before, 0.0), axis=1, keepdims=True)
    rank1 = jnp.sum(jnp.where(oh1, cnt_before, 0.0), axis=1, keepdims=True)
    carry = carry_ref[...] + jnp.sum(both.astype(F32), axis=0, keepdims=True)
    carry_ref[...] = carry
    cnt_ref[...] = carry

    e0 = (i1 - EXPERT_LANE0).astype(F32)
    e1 = (i2 - EXPERT_LANE0).astype(F32)
    meta = jnp.zeros((tm, LANES), F32)
    for ln, val in ((M_E0, e0), (M_E1, e1), (M_R0, rank0), (M_R1, rank1), (M_W0, w0), (M_W1, w1)):
        meta = jnp.where(lane == ln, val, meta)
    meta_ref[...] = meta
    meta_t_ref[...] = meta.T[0:SUBLANES, :]


def _post_mix(z, w_out, h, g, w_router, b_router):
    n, d = h.shape
    k = z.shape[1]
    tm = _tile(n, 512)
    return pl.pallas_call(
        _post_mix_kernel,
        grid=(n // tm,),
        in_specs=[pl.BlockSpec((tm, k), lambda i: (i, 0)),
                  pl.BlockSpec((k, d), lambda i: (0, 0)),
                  pl.BlockSpec((tm, d), lambda i: (i, 0)),
                  pl.BlockSpec((1, d), lambda i: (0, 0)),
                  pl.BlockSpec((d, LANES), lambda i: (0, 0)),
                  pl.BlockSpec((1, LANES), lambda i: (0, 0))],
        out_specs=[pl.BlockSpec((tm, d), lambda i: (i, 0)),
                   pl.BlockSpec((tm, d), lambda i: (i, 0)),
                   pl.BlockSpec((tm, LANES), lambda i: (i, 0)),
                   pl.BlockSpec((SUBLANES, tm), lambda i: (0, i)),
                   pl.BlockSpec((1, LANES), lambda i: (0, 0))],
        out_shape=[jax.ShapeDtypeStruct((n, d), F32), jax.ShapeDtypeStruct((n, d), F32),
                   jax.ShapeDtypeStruct((n, LANES), F32), jax.ShapeDtypeStruct((SUBLANES, n), F32),
                   jax.ShapeDtypeStruct((1, LANES), F32)],
        scratch_shapes=[pltpu.VMEM((1, LANES), F32)],
        compiler_params=_cparams("arbitrary"),
        name="post_mix_router",
    )(z, w_out, h, g, w_router, b_router)


def _plan_kernel(cnt_ref, meta_t_ref, dest_ref, blk_e_ref, nblk_ref, ztail_ref, pstart_ref):
    nb_max = blk_e_ref.shape[0]

    def per_expert(e, acc):
        nb_e = (cnt_ref[e] + (ROUTE_ROWS - 1)) // ROUTE_ROWS
        pstart_ref[e] = acc * ROUTE_ROWS

        def fill(j, c):
            blk_e_ref[acc + j] = e
            return c

        lax.fori_loop(0, nb_e, fill, 0)
        ztail_ref[e] = jnp.where(nb_e > 0, (acc + nb_e - 1) * ROUTE_ROWS, -1)
        return acc + nb_e

    used = lax.fori_loop(0, N_EXPERTS, per_expert, 0)
    nblk_ref[0] = used

    def fill_rest(b, c):
        blk_e_ref[b] = N_EXPERTS - 1
        return c

    lax.fori_loop(used, nb_max, fill_rest, 0)

    e0 = meta_t_ref[M_E0:M_E0 + 1, :]
    e1 = meta_t_ref[M_E1:M_E1 + 1, :]
    d0 = meta_t_ref[M_R0:M_R0 + 1, :]
    d1 = meta_t_ref[M_R1:M_R1 + 1, :]
    for e in range(N_EXPERTS):
        ps = pstart_ref[e].astype(F32)
        d0 = d0 + jnp.where(e0 == e, ps, 0.0)
        d1 = d1 + jnp.where(e1 == e, ps, 0.0)
    dest_ref[...] = jnp.zeros_like(dest_ref)
    dest_ref[0:1, :] = d0.astype(I32)
    dest_ref[1:2, :] = d1.astype(I32)


def _plan(cnt_i32, meta_t, nb_max):
    n = meta_t.shape[1]
    smem = pl.BlockSpec(memory_space=pltpu.SMEM)
    return pl.pallas_call(
        _plan_kernel,
        in_specs=[smem, pl.BlockSpec((SUBLANES, n), lambda: (0, 0))],
        out_specs=[pl.BlockSpec((SUBLANES, n), lambda: (0, 0)), smem, smem, smem],
        out_shape=[jax.ShapeDtypeStruct((SUBLANES, n), I32),
                   jax.ShapeDtypeStruct((nb_max,), I32),
                   jax.ShapeDtypeStruct((1,), I32),
                   jax.ShapeDtypeStruct((N_EXPERTS,), I32)],
        scratch_shapes=[pltpu.SMEM((N_EXPERTS,), I32)],
        name="dispatch_plan",
    )(cnt_i32, meta_t)


def _dispatch_kernel(ztail_ref, nblk_ref, d0_ref, d1_ref, x_ref, xs_ref, zero_s, sem):
    i = pl.program_id(0)
    tmd = x_ref.shape[0]
    nb_max = xs_ref.shape[0] // ROUTE_ROWS

    def zero_block(start):
        start = pl.multiple_of(start, ROUTE_ROWS)
        return pltpu.make_async_copy(zero_s, xs_ref.at[pl.ds(start, ROUTE_ROWS)], sem.at[2])

    @pl.when(i == 0)
    def _():
        zero_s[...] = jnp.zeros_like(zero_s)

        def start_tail(e, c):
            @pl.when(ztail_ref[e] >= 0)
            def _():
                zero_block(ztail_ref[e]).start()
            return c

        def wait_tail(e, c):
            @pl.when(ztail_ref[e] >= 0)
            def _():
                zero_block(ztail_ref[e]).wait()
            return c

        def start_unused(b, c):
            zero_block(b * ROUTE_ROWS).start()
            return c

        def wait_unused(b, c):
            zero_block(b * ROUTE_ROWS).wait()
            return c

        lax.fori_loop(0, N_EXPERTS, start_tail, 0)
        lax.fori_loop(nblk_ref[0], nb_max, start_unused, 0)
        lax.fori_loop(0, N_EXPERTS, wait_tail, 0)
        lax.fori_loop(nblk_ref[0], nb_max, wait_unused, 0)

    def row_copy(t, dref, k):
        return pltpu.make_async_copy(x_ref.at[pl.ds(t, 1)], xs_ref.at[pl.ds(dref[0, t], 1)], sem.at[k])

    def issue(t, c):
        row_copy(t, d0_ref, 0).start()
        row_copy(t, d1_ref, 1).start()
        return c

    def drain(t, c):
        row_copy(t, d0_ref, 0).wait()
        row_copy(t, d1_ref, 1).wait()
        return c

    lax.fori_loop(0, tmd, issue, 0)
    lax.fori_loop(0, tmd, drain, 0)


def _dispatch(ztail, nblk, dest, xn, nb_max):
    n, d = xn.shape
    tmd = _tile(n, 256)
    nt = n // tmd
    d0 = dest[0].reshape(nt, 1, tmd)
    d1 = dest[1].reshape(nt, 1, tmd)
    idx_spec = pl.BlockSpec((None, 1, tmd), lambda i: (i, 0, 0), memory_space=pltpu.SMEM)
    smem = pl.BlockSpec(memory_space=pltpu.SMEM)
    return pl.pallas_call(
        _dispatch_kernel,
        grid=(nt,),
        in_specs=[smem, smem, idx_spec, idx_spec,
                  pl.BlockSpec((tmd, d), lambda i: (i, 0))],
        out_specs=pl.BlockSpec(memory_space=pl.ANY),
        out_shape=jax.ShapeDtypeStruct((nb_max * ROUTE_ROWS, d), F32),
        scratch_shapes=[pltpu.VMEM((ROUTE_ROWS, d), F32), pltpu.SemaphoreType.DMA((3,))],
        compiler_params=_cparams("arbitrary"),
        name="dispatch_scatter",
    )(ztail, nblk, d0, d1, xn)


def _expert_kernel(be_ref, nb_ref, xs_ref, wg_ref, wu_ref, wd_ref, ys_ref, wg_s, wu_s, wd_s):
    b = pl.program_id(0)

    @pl.when(b < nb_ref[0])
    def _():
        changed = (b == 0) | (be_ref[b] != be_ref[jnp.maximum(b - 1, 0)])

        @pl.when(changed)
        def _():
            wg_s[...] = wg_ref[...].astype(BF16)
            wu_s[...] = wu_ref[...].astype(BF16)
            wd_s[...] = wd_ref[...].astype(BF16)

        x = xs_ref[...].astype(BF16)
        g = _dot(x, wg_s[...])
        u = _dot(x, wu_s[...])
        hb = (jax.nn.silu(g) * u).astype(BF16)
        ys_ref[...] = _dot(hb, wd_s[...])

    @pl.when(b >= nb_ref[0])
    def _():
        ys_ref[...] = jnp.zeros_like(ys_ref)


def _experts(blk_e, nblk, xs, w_gate, w_up, w_down):
    p, d = xs.shape
    f = w_gate.shape[2]
    nb_max = p // ROUTE_ROWS

    def row_map(b, be, nb):
        return (jnp.minimum(b, nb[0] - 1), 0)

    def w_map(b, be, nb):
        return (be[jnp.minimum(b, nb[0] - 1)], 0, 0)

    grid_spec = pltpu.PrefetchScalarGridSpec(
        num_scalar_prefetch=2,
        grid=(nb_max,),
        in_specs=[pl.BlockSpec((ROUTE_ROWS, d), row_map),
                  pl.BlockSpec((None, d, f), w_map),
                  pl.BlockSpec((None, d, f), w_map),
                  pl.BlockSpec((None, f, d), w_map)],
        out_specs=pl.BlockSpec((ROUTE_ROWS, d), lambda b, be, nb: (b, 0)),
        scratch_shapes=[pltpu.VMEM((d, f), BF16), pltpu.VMEM((d, f), BF16), pltpu.VMEM((f, d), BF16)],
    )
    return pl.pallas_call(
        _expert_kernel,
        grid_spec=grid_spec,
        out_shape=jax.ShapeDtypeStruct((p, d), F32),
        compiler_params=_cparams("arbitrary"),
        name="expert_mlp",
    )(blk_e, nblk, xs, w_gate, w_up, w_down)


def _combine_kernel(d0_ref, d1_ref, ys_ref, h1_ref, meta_ref, p_ref, gp_ref, wpg_ref, bpg_ref,
                    wp_ref, gf_ref, out_ref, ya_s, yb_s, sem, *, final):
    tmc = h1_ref.shape[0]

    def row_copy(t, dref, dst, k):
        return pltpu.make_async_copy(ys_ref.at[pl.ds(dref[0, t], 1)], dst.at[pl.ds(t, 1)], sem.at[k])

    def issue(t, c):
        row_copy(t, d0_ref, ya_s, 0).start()
        row_copy(t, d1_ref, yb_s, 1).start()
        return c

    def drain(t, c):
        row_copy(t, d0_ref, ya_s, 0).wait()
        row_copy(t, d1_ref, yb_s, 1).wait()
        return c

    lax.fori_loop(0, tmc, issue, 0)
    lax.fori_loop(0, tmc, drain, 0)

    meta = meta_ref[...]
    w0 = meta[:, M_W0:M_W0 + 1]
    w1 = meta[:, M_W1:M_W1 + 1]
    h2 = h1_ref[...] + (ya_s[...] * w0 + yb_s[...] * w1)
    xn = _rms(h2, gp_ref[...]).astype(BF16)
    gate = jax.nn.sigmoid(_dot(xn, wpg_ref[...]) + bpg_ref[...])
    pp = _dot(p_ref[...].astype(BF16), wp_ref[...])
    h3 = h2 + gate * pp
    if final:
        h3 = _rms(h3, gf_ref[...])
    out_ref[...] = h3


def _combine(dest, ys, h1, meta, p, g_ple, w_ple_gate, b_ple_gate, w_ple, g_final, final):
    n, d = h1.shape
    pd = p.shape[1]
    tmc = _tile(n, 256)
    nt = n // tmc
    d0 = dest[0].reshape(nt, 1, tmc)
    d1 = dest[1].reshape(nt, 1, tmc)
    idx_spec = pl.BlockSpec((None, 1, tmc), lambda i: (i, 0, 0), memory_space=pltpu.SMEM)
    vec = pl.BlockSpec((1, d), lambda i: (0, 0))
    return pl.pallas_call(
        functools.partial(_combine_kernel, final=final),
        grid=(nt,),
        in_specs=[idx_spec, idx_spec,
                  pl.BlockSpec(memory_space=pl.ANY),
                  pl.BlockSpec((tmc, d), lambda i: (i, 0)),
                  pl.BlockSpec((tmc, LANES), lambda i: (i, 0)),
                  pl.BlockSpec((tmc, pd), lambda i: (i, 0)),
                  vec,
                  pl.BlockSpec((d, d), lambda i: (0, 0)),
                  vec,
                  pl.BlockSpec((pd, d), lambda i: (0, 0)),
                  vec],
        out_specs=pl.BlockSpec((tmc, d), lambda i: (i, 0)),
        out_shape=jax.ShapeDtypeStruct((n, d), F32),
        scratch_shapes=[pltpu.VMEM((tmc, d), F32), pltpu.VMEM((tmc, d), F32),
                        pltpu.SemaphoreType.DMA((2,))],
        compiler_params=_cparams("arbitrary"),
        name="combine_ple",
    )(d0, d1, ys, h1, meta, p, g_ple, w_ple_gate, b_ple_gate, w_ple, g_final)


def _row(v):
    return v.reshape(1, -1)


def _moe_and_ple(z, w_out, h, i, final, norm_ffn, w_router_group, b_router_group, w_router_expert,
                 b_router_expert, w_exp_gate, w_exp_up, w_exp_down, norm_ple, w_ple, w_ple_gate,
                 b_ple_gate, norm_final, p):
    n, d = h.shape
    pad = LANES - N_GROUPS - N_EXPERTS
    w_router = jnp.concatenate(
        [w_router_group[i], w_router_expert[i], jnp.zeros((d, pad), F32)], axis=1)
    b_router = jnp.concatenate(
        [b_router_group[i], b_router_expert[i], jnp.zeros((pad,), F32)]).reshape(1, LANES)
    h1, xn, meta, meta_t, cnt = _post_mix(z, w_out, h, _row(norm_ffn[i]), w_router, b_router)
    cnt_i32 = cnt[0, EXPERT_LANE0:EXPERT_LANE0 + N_EXPERTS].astype(I32)
    nb_max = (2 * n) // ROUTE_ROWS + N_EXPERTS
    dest, blk_e, nblk, ztail = _plan(cnt_i32, meta_t, nb_max)
    xs = _dispatch(ztail, nblk, dest, xn, nb_max)
    ys = _experts(blk_e, nblk, xs, w_exp_gate[i], w_exp_up[i], w_exp_down[i])
    return _combine(dest, ys, h1, meta, p[i].reshape(n, -1), _row(norm_ple[i]),
                    w_ple_gate[i].astype(BF16), _row(b_ple_gate[i]), w_ple[i].astype(BF16),
                    _row(norm_final), final)


def kernel(x, p, norm_mix, w_in_a, conv_a, w_out_a, w_in_b, conv_b, conv_bias_b, w_rgate_b, b_rgate_b, w_igate_b, b_igate_b, lam_b, w_out_b, norm_ffn, w_router_group, b_router_group, w_router_expert, b_router_expert, w_exp_gate, w_exp_up, w_exp_down, norm_ple, w_ple, w_ple_gate, b_ple_gate, norm_final):
    batch, seq, d = x.shape
    depth = p.shape[0]
    n = batch * seq
    h = x.reshape(n, d)
    for i in range(depth):
        j = i // 2
        if i % 2 == 0:
            bg, v = _mix_a_in(h, _row(norm_mix[i]), w_in_a[j].astype(BF16))
            z = _mix_a_conv(v, bg, conv_a[j], seq)
            w_out = w_out_a[j]
        else:
            yb, u_raw = _mix_b_in(h, _row(norm_mix[i]), w_in_b[j].astype(BF16))
            z = _rglru(u_raw, yb, conv_b[j], _row(conv_bias_b[j]),
                       w_rgate_b[j].astype(BF16), b_rgate_b[j], w_igate_b[j].astype(BF16),
                       b_igate_b[j], lam_b[j], batch, seq)
            w_out = w_out_b[j]
        h = _moe_and_ple(z, w_out.astype(BF16), h, i, i == depth - 1, norm_ffn, w_router_group,
                         b_router_group, w_router_expert, b_router_expert, w_exp_gate, w_exp_up,
                         w_exp_down, norm_ple, w_ple, w_ple_gate, b_ple_gate, norm_final, p)
    return h.reshape(batch, seq, d)
```

```python
import functools

import jax
import jax.numpy as jnp
from jax import lax
from jax.experimental import pallas as pl
from jax.experimental.pallas import tpu as pltpu

F32 = jnp.float32
BF16 = jnp.bfloat16
I32 = jnp.int32

EPS = 1e-6
N_GROUPS = 4
EXPERTS_PER_GROUP = 8
N_EXPERTS = N_GROUPS * EXPERTS_PER_GROUP
RG_C = 8.0

LANES = 128
SUBLANES = 8
BF16_ROWS = 16
EXPERT_LANE0 = N_GROUPS
ROUTE_ROWS = 256
M_E0, M_E1, M_R0, M_R1, M_W0, M_W1 = 0, 1, 2, 3, 4, 5
VMEM_LIMIT = 48 * 1024 * 1024


def _cparams(*sem):
    return pltpu.CompilerParams(dimension_semantics=sem, vmem_limit_bytes=VMEM_LIMIT)


def _rms(x, g):
    return x * lax.rsqrt(jnp.mean(x * x, axis=-1, keepdims=True) + EPS) * g


def _dot(a, b):
    return jnp.dot(a, b, preferred_element_type=F32)


def _tile(n, want):
    t = min(n, want)
    assert n % t == 0, (n, want)
    return t


def _mix_a_in_kernel(h_ref, g_ref, w_ref, bg_ref, v_ref):
    d = h_ref.shape[1]
    xn = _rms(h_ref[...], g_ref[...]).astype(BF16)
    bg = _dot(xn, w_ref[:, 0:d])
    cg = _dot(xn, w_ref[:, d:2 * d])
    hh = _dot(xn, w_ref[:, 2 * d:3 * d])
    bg_ref[...] = bg.astype(BF16)
    v_ref[...] = (cg * hh).astype(BF16)


def _mix_a_in(h, g, w_in):
    n, d = h.shape
    tm = _tile(n, 512)
    return pl.pallas_call(
        _mix_a_in_kernel,
        grid=(n // tm,),
        in_specs=[pl.BlockSpec((tm, d), lambda i: (i, 0)),
                  pl.BlockSpec((1, d), lambda i: (0, 0)),
                  pl.BlockSpec((d, 3 * d), lambda i: (0, 0))],
        out_specs=[pl.BlockSpec((tm, d), lambda i: (i, 0)),
                   pl.BlockSpec((tm, d), lambda i: (i, 0))],
        out_shape=[jax.ShapeDtypeStruct((n, d), BF16), jax.ShapeDtypeStruct((n, d), BF16)],
        compiler_params=_cparams("parallel"),
        name="mix_a_in",
    )(h, g, w_in)


def _mix_a_conv_kernel(v_ref, vp_ref, vn_ref, bg_ref, cw_ref, z_ref, *, tiles_per_seq):
    i = pl.program_id(0)
    tm = v_ref.shape[0]
    v = v_ref[...].astype(F32)
    at_start = (i % tiles_per_seq) == 0
    at_end = (i % tiles_per_seq) == tiles_per_seq - 1
    prev_row = vp_ref[...].astype(F32)[BF16_ROWS - 1:BF16_ROWS, :]
    next_row = vn_ref[...].astype(F32)[0:1, :]
    prev_row = jnp.where(at_start, 0.0, prev_row)
    next_row = jnp.where(at_end, 0.0, next_row)
    row = lax.broadcasted_iota(I32, (tm, 1), 0)
    v_dn = jnp.where(row == 0, prev_row, pltpu.roll(v, 1, 0))
    v_up = jnp.where(row == tm - 1, next_row, pltpu.roll(v, tm - 1, 0))
    cw = cw_ref[...]
    u = cw[0:1, :] * v_dn + cw[1:2, :] * v + cw[2:3, :] * v_up
    z_ref[...] = (bg_ref[...].astype(F32) * u).astype(BF16)


def _mix_a_conv(v, bg, conv_w, seq):
    n, d = v.shape
    tm = _tile(seq, 512)
    hb = tm // BF16_ROWS
    nhalo = n // BF16_ROWS
    return pl.pallas_call(
        functools.partial(_mix_a_conv_kernel, tiles_per_seq=seq // tm),
        grid=(n // tm,),
        in_specs=[pl.BlockSpec((tm, d), lambda i: (i, 0)),
                  pl.BlockSpec((BF16_ROWS, d), lambda i: (jnp.maximum(i * hb - 1, 0), 0)),
                  pl.BlockSpec((BF16_ROWS, d), lambda i: (jnp.minimum((i + 1) * hb, nhalo - 1), 0)),
                  pl.BlockSpec((tm, d), lambda i: (i, 0)),
                  pl.BlockSpec(conv_w.shape, lambda i: (0, 0))],
        out_specs=pl.BlockSpec((tm, d), lambda i: (i, 0)),
        out_shape=jax.ShapeDtypeStruct((n, d), BF16),
        compiler_params=_cparams("parallel"),
        name="mix_a_conv",
    )(v, v, v, bg, conv_w)


def _mix_b_in_kernel(h_ref, hp_ref, hn_ref, g_ref, w_ref, yb_ref, u_ref, *, steps_per_seq, tt):
    i = pl.program_id(0)
    r = yb_ref.shape[1]
    tiles = h_ref.shape[0] // tt
    g = g_ref[...]
    xn = _rms(h_ref[...], g).astype(BF16)
    yb_ref[...] = jax.nn.gelu(_dot(xn, w_ref[:, 0:r])).astype(BF16)
    u = _dot(xn, w_ref[:, r:2 * r])
    xh = _rms(jnp.concatenate([hp_ref[...], hn_ref[...]], axis=0), g).astype(BF16)
    uh = _dot(xh, w_ref[:, r:2 * r])
    at_start = (i % steps_per_seq) == 0
    at_end = (i % steps_per_seq) == steps_per_seq - 1
    prev2 = jnp.where(at_start, 0.0, uh[SUBLANES - 2:SUBLANES, :])
    next1 = jnp.where(at_end, 0.0, uh[SUBLANES:SUBLANES + 1, :])
    row = lax.broadcasted_iota(I32, (SUBLANES, 1), 0)
    for k in range(tiles):
        p2 = prev2 if k == 0 else u[k * tt - 2:k * tt]
        n1 = next1 if k == tiles - 1 else u[(k + 1) * tt:(k + 1) * tt + 1]
        pad = jnp.where(row == 0, p2[0:1], jnp.where(row == 1, p2[1:2], jnp.where(row == 2, n1, 0.0)))
        for s in range(r // LANES):
            lanes = slice(s * LANES, (s + 1) * LANES)
            u_ref[k, s, 0:tt, :] = u[k * tt:(k + 1) * tt, lanes]
            u_ref[k, s, tt:tt + SUBLANES, :] = pad[:, lanes]


def _mix_b_in(h, g, w_in, batch, seq, tt):
    n, d = h.shape
    r = w_in.shape[1] // 2
    tm = _tile(seq, 512)
    steps_per_seq = seq // tm
    tiles = tm // tt
    hb = tm // SUBLANES
    nhalo = n // SUBLANES
    return pl.pallas_call(
        functools.partial(_mix_b_in_kernel, steps_per_seq=steps_per_seq, tt=tt),
        grid=(n // tm,),
        in_specs=[pl.BlockSpec((tm, d), lambda i: (i, 0)),
                  pl.BlockSpec((SUBLANES, d), lambda i: (jnp.maximum(i * hb - 1, 0), 0)),
                  pl.BlockSpec((SUBLANES, d), lambda i: (jnp.minimum((i + 1) * hb, nhalo - 1), 0)),
                  pl.BlockSpec((1, d), lambda i: (0, 0)),
                  pl.BlockSpec((d, 2 * r), lambda i: (0, 0))],
        out_specs=[pl.BlockSpec((tm, r), lambda i: (i, 0)),
                   pl.BlockSpec((tiles, r // LANES, None, tt + SUBLANES, LANES),
                                lambda i: (i % steps_per_seq, 0, i // steps_per_seq, 0, 0))],
        out_shape=[jax.ShapeDtypeStruct((n, r), BF16),
                   jax.ShapeDtypeStruct((seq // tt, r // LANES, batch, tt + SUBLANES, LANES), F32)],
        compiler_params=_cparams("parallel"),
        name="mix_b_in",
    )(h, h, h, g, w_in)


def _rglru_pass_kernel(*refs, reverse, tt):
    if reverse:
        (u_ref, cw_ref, cb_ref, wr_ref, br_ref, wi_ref, bi_ref, lam_ref, hf_ref, yb_ref,
         out_ref, x_s, a_s, g_s, carry_s, nat_s) = refs
    else:
        (u_ref, cw_ref, cb_ref, wr_ref, br_ref, wi_ref, bi_ref, lam_ref,
         out_ref, x_s, a_s, g_s, carry_s) = refs
    nb = SUBLANES
    ttp = tt + SUBLANES
    rows = tt * nb
    slabs = u_ref.shape[0]

    @pl.when(pl.program_id(1) == 0)
    def _():
        carry_s[...] = jnp.zeros_like(carry_s)

    def gather_t(tl, dst):
        for s in range(slabs):
            x_s[pl.ds(dst, nb), s * LANES:(s + 1) * LANES] = u_ref[s, pl.ds(tl, nb, stride=ttp), :]

    def load_t(tl, c):
        gather_t(tl, pl.multiple_of((tl + 2) * nb, nb))
        return c

    lax.fori_loop(0, tt, load_t, 0, unroll=8)
    gather_t(tt, 0)
    gather_t(tt + 1, nb)
    gather_t(tt + 2, (tt + 2) * nb)

    cw = cw_ref[...]
    u = (cw[0:1, :] * x_s[0:rows, :] + cw[1:2, :] * x_s[nb:nb + rows, :]
         + cw[2:3, :] * x_s[2 * nb:2 * nb + rows, :] + cw[3:4, :] * x_s[3 * nb:3 * nb + rows, :]
         + cb_ref[...])
    ub = u.astype(BF16)
    r = 0.5 * (1.0 + jnp.tanh(0.5 * (_dot(ub, wr_ref[...]) + br_ref[...])))
    ig = 0.5 * (1.0 + jnp.tanh(0.5 * (_dot(ub, wi_ref[...]) + bi_ref[...])))
    log_a = -RG_C * r * jax.nn.softplus(-lam_ref[...])
    th = jnp.tanh(0.5 * log_a)
    q = 1.0 / (1.0 - th)
    a_s[...] = (1.0 + th) * q
    m2 = -4.0 * th * q * q
    g_s[...] = jnp.where(m2 > 0.0, m2 * lax.rsqrt(m2), 0.0) * (ig * u)

    def step(k, h):
        tl = tt - 1 - k if reverse else k
        base = pl.multiple_of(tl * nb, nb)
        h = a_s[pl.ds(base, nb), :] * h + g_s[pl.ds(base, nb), :]
        if reverse:
            hs = h + hf_ref[pl.ds(base, nb), :]
            for s in range(slabs):
                nat_s[s, pl.ds(tl, nb, stride=ttp), :] = hs[:, s * LANES:(s + 1) * LANES]
        else:
            out_ref[pl.ds(base, nb), :] = h
        return h

    carry_s[...] = lax.fori_loop(0, tt, step, carry_s[...], unroll=8)
    if reverse:
        for b in range(nb):
            for s in range(slabs):
                lanes = slice(s * LANES, (s + 1) * LANES)
                hb = nat_s[s, b * ttp:b * ttp + tt, :]
                out_ref[b, :, lanes] = (yb_ref[b, :, lanes].astype(F32) * hb).astype(BF16)


def _rglru_pass(u_tiles, conv_w, conv_b, w_r, b_r, w_i, b_i, lam, hf, yb, seq, tt, reverse):
    nt, nslab, batch, ttp, _ = u_tiles.shape
    r = nslab * LANES
    assert batch == SUBLANES, "time-major tile layout puts the batch on the 8 sublanes"
    nh, bk = w_r.shape[1], w_r.shape[2]
    slabs = bk // LANES
    kw = conv_w.shape[0]
    d = 1 if reverse else 0
    rows = tt * batch

    def tile(jj):
        return nt - 1 - jj if reverse else jj

    vec = pl.BlockSpec((None, 1, bk), lambda h, jj: (d, 0, h))
    mat = pl.BlockSpec((None, None, bk, bk), lambda h, jj: (d, h, 0, 0))
    in_specs = [pl.BlockSpec((None, slabs, batch * ttp, LANES), lambda h, jj: (tile(jj), h, 0, 0)),
                pl.BlockSpec((kw, bk), lambda h, jj: (0, h)),
                pl.BlockSpec((1, bk), lambda h, jj: (0, h)),
                mat, vec, mat, vec, vec]
    args = [u_tiles.reshape(nt, nslab, batch * ttp, LANES), conv_w, conv_b, w_r, b_r.reshape(2, 1, r),
            w_i, b_i.reshape(2, 1, r), lam.reshape(2, 1, r)]
    scratch = [pltpu.VMEM(((tt + 3) * batch, bk), F32), pltpu.VMEM((rows, bk), F32),
               pltpu.VMEM((rows, bk), F32), pltpu.VMEM((batch, bk), F32)]
    if reverse:
        in_specs += [pl.BlockSpec((rows, bk), lambda h, jj: (tile(jj), h)),
                     pl.BlockSpec((batch, tt, bk), lambda h, jj: (0, tile(jj), h))]
        args += [hf, yb.reshape(batch, seq, r)]
        scratch += [pltpu.VMEM((slabs, batch * ttp, LANES), F32)]
        out_spec = pl.BlockSpec((batch, tt, bk), lambda h, jj: (0, tile(jj), h))
        out_shape = jax.ShapeDtypeStruct((batch, seq, r), BF16)
    else:
        out_spec = pl.BlockSpec((rows, bk), lambda h, jj: (tile(jj), h))
        out_shape = jax.ShapeDtypeStruct((seq * batch, r), F32)
    return pl.pallas_call(
        functools.partial(_rglru_pass_kernel, reverse=reverse, tt=tt),
        grid=(nh, nt),
        in_specs=in_specs,
        out_specs=out_spec,
        out_shape=out_shape,
        scratch_shapes=scratch,
        compiler_params=_cparams("parallel", "arbitrary"),
        name="rglru_rev" if reverse else "rglru_fwd",
    )(*args)


def _rglru(u_tiles, yb, conv_w, conv_b, w_r, b_r, w_i, b_i, lam, seq, tt):
    hf = _rglru_pass(u_tiles, conv_w, conv_b, w_r, b_r, w_i, b_i, lam, None, None, seq, tt, False)
    z = _rglru_pass(u_tiles, conv_w, conv_b, w_r, b_r, w_i, b_i, lam, hf, yb, seq, tt, True)
    return z.reshape(-1, z.shape[2])


def _post_mix_kernel(z_ref, wo_ref, h_ref, g_ref, wr_ref, br_ref,
                     h1_ref, xn_ref, meta_ref, meta_t_ref, cnt_ref, carry_ref):
    i = pl.program_id(0)
    tm = z_ref.shape[0]

    @pl.when(i == 0)
    def _():
        carry_ref[...] = jnp.zeros_like(carry_ref)

    h1 = h_ref[...] + _dot(z_ref[...], wo_ref[...])
    h1_ref[...] = h1
    xn = _rms(h1, g_ref[...])
    xn_ref[...] = xn

    x_hi = xn.astype(BF16)
    x_lo = (xn - x_hi.astype(F32)).astype(BF16)
    w = wr_ref[...]
    w_hi = w.astype(BF16)
    w_lo = (w - w_hi.astype(F32)).astype(BF16)
    logits = _dot(x_hi, w_hi) + _dot(x_hi, w_lo) + _dot(x_lo, w_hi) + br_ref[...]

    lane = lax.broadcasted_iota(I32, (tm, LANES), 1)
    neg = jnp.float32(-jnp.inf)
    gmask = lane < N_GROUPS
    gmax = jnp.max(jnp.where(gmask, logits, neg), axis=1, keepdims=True)
    eg = jnp.where(gmask, jnp.exp(logits - gmax), 0.0)
    pg = eg / jnp.sum(eg, axis=1, keepdims=True)
    pg_top = jnp.max(pg, axis=1, keepdims=True)
    g_top = jnp.min(jnp.where(gmask & (pg == pg_top), lane, LANES), axis=1, keepdims=True)
    lo = EXPERT_LANE0 + g_top * EXPERTS_PER_GROUP
    emask = (lane >= lo) & (lane < lo + EXPERTS_PER_GROUP)
    emax = jnp.max(jnp.where(emask, logits, neg), axis=1, keepdims=True)
    ee = jnp.where(emask, jnp.exp(logits - emax), 0.0)
    pe = jnp.where(emask, ee / jnp.sum(ee, axis=1, keepdims=True), -1.0)
    p1 = jnp.max(pe, axis=1, keepdims=True)
    i1 = jnp.min(jnp.where(pe == p1, lane, LANES), axis=1, keepdims=True)
    pe2 = jnp.where(lane == i1, -1.0, pe)
    p2 = jnp.max(pe2, axis=1, keepdims=True)
    i2 = jnp.min(jnp.where(pe2 == p2, lane, LANES), axis=1, keepdims=True)
    psum = p1 + p2
    w0 = pg_top * (p1 / psum)
    w1 = pg_top * (p2 / psum)

    oh0 = lane == i1
    oh1 = lane == i2
    both = (oh0 | oh1).astype(BF16)
    ri = lax.broadcasted_iota(I32, (tm, tm), 0)
    ci = lax.broadcasted_iota(I32, (tm, tm), 1)
    before = (ri > ci).astype(BF16)
    cnt_before = _dot(before, both) + carry_ref[...]
    rank0 = jnp.sum(jnp.where(oh0, cnt_before, 0.0), axis=1, keepdims=True)
    rank1 = jnp.sum(jnp.where(oh1, cnt_before, 0.0), axis=1, keepdims=True)
    carry = carry_ref[...] + jnp.sum(both.astype(F32), axis=0, keepdims=True)
    carry_ref[...] = carry
    cnt_ref[...] = carry

    e0 = (i1 - EXPERT_LANE0).astype(F32)
    e1 = (i2 - EXPERT_LANE0).astype(F32)
    meta = jnp.zeros((tm, LANES), F32)
    for ln, val in ((M_E0, e0), (M_E1, e1), (M_R0, rank0), (M_R1, rank1), (M_W0, w0), (M_W1, w1)):
        meta = jnp.where(lane == ln, val, meta)
    meta_ref[...] = meta
    meta_t_ref[...] = meta.T[0:SUBLANES, :]


def _post_mix(z, w_out, h, g, w_router, b_router):
    n, d = h.shape
    k = z.shape[1]
    tm = _tile(n, 512)
    return pl.pallas_call(
        _post_mix_kernel,
        grid=(n // tm,),
        in_specs=[pl.BlockSpec((tm, k), lambda i: (i, 0)),
                  pl.BlockSpec((k, d), lambda i: (0, 0)),
                  pl.BlockSpec((tm, d), lambda i: (i, 0)),
                  pl.BlockSpec((1, d), lambda i: (0, 0)),
                  pl.BlockSpec((d, LANES), lambda i: (0, 0)),
                  pl.BlockSpec((1, LANES), lambda i: (0, 0))],
        out_specs=[pl.BlockSpec((tm, d), lambda i: (i, 0)),
                   pl.BlockSpec((tm, d), lambda i: (i, 0)),
                   pl.BlockSpec((tm, LANES), lambda i: (i, 0)),
                   pl.BlockSpec((SUBLANES, tm), lambda i: (0, i)),
                   pl.BlockSpec((1, LANES), lambda i: (0, 0))],
        out_shape=[jax.ShapeDtypeStruct((n, d), F32), jax.ShapeDtypeStruct((n, d), F32),
                   jax.ShapeDtypeStruct((n, LANES), F32), jax.ShapeDtypeStruct((SUBLANES, n), F32),
                   jax.ShapeDtypeStruct((1, LANES), F32)],
        scratch_shapes=[pltpu.VMEM((1, LANES), F32)],
        compiler_params=_cparams("arbitrary"),
        name="post_mix_router",
    )(z, w_out, h, g, w_router, b_router)


def _plan_kernel(cnt_ref, meta_t_ref, dest_ref, blk_e_ref, nblk_ref, ztail_ref, pstart_ref):
    nb_max = blk_e_ref.shape[0]

    def per_expert(e, acc):
        nb_e = (cnt_ref[e] + (ROUTE_ROWS - 1)) // ROUTE_ROWS
        pstart_ref[e] = acc * ROUTE_ROWS

        def fill(j, c):
            blk_e_ref[acc + j] = e
            return c

        lax.fori_loop(0, nb_e, fill, 0)
        ztail_ref[e] = jnp.where(nb_e > 0, (acc + nb_e - 1) * ROUTE_ROWS, -1)
        return acc + nb_e

    used = lax.fori_loop(0, N_EXPERTS, per_expert, 0)
    nblk_ref[0] = used

    def fill_rest(b, c):
        blk_e_ref[b] = N_EXPERTS - 1
        return c

    lax.fori_loop(used, nb_max, fill_rest, 0)

    e0 = meta_t_ref[M_E0:M_E0 + 1, :]
    e1 = meta_t_ref[M_E1:M_E1 + 1, :]
    d0 = meta_t_ref[M_R0:M_R0 + 1, :]
    d1 = meta_t_ref[M_R1:M_R1 + 1, :]
    for e in range(N_EXPERTS):
        ps = pstart_ref[e].astype(F32)
        d0 = d0 + jnp.where(e0 == e, ps, 0.0)
        d1 = d1 + jnp.where(e1 == e, ps, 0.0)
    dest_ref[...] = jnp.zeros_like(dest_ref)
    dest_ref[0:1, :] = d0.astype(I32)
    dest_ref[1:2, :] = d1.astype(I32)


def _plan(cnt_i32, meta_t, nb_max):
    n = meta_t.shape[1]
    smem = pl.BlockSpec(memory_space=pltpu.SMEM)
    return pl.pallas_call(
        _plan_kernel,
        in_specs=[smem, pl.BlockSpec((SUBLANES, n), lambda: (0, 0))],
        out_specs=[pl.BlockSpec((SUBLANES, n), lambda: (0, 0)), smem, smem, smem],
        out_shape=[jax.ShapeDtypeStruct((SUBLANES, n), I32),
                   jax.ShapeDtypeStruct((nb_max,), I32),
                   jax.ShapeDtypeStruct((1,), I32),
                   jax.ShapeDtypeStruct((N_EXPERTS,), I32)],
        scratch_shapes=[pltpu.SMEM((N_EXPERTS,), I32)],
        name="dispatch_plan",
    )(cnt_i32, meta_t)


def _dispatch_kernel(ztail_ref, nblk_ref, d0_ref, d1_ref, x_ref, xs_ref, zero_s, sem):
    i = pl.program_id(0)
    nb_max = xs_ref.shape[0] // ROUTE_ROWS

    def zero_block(start):
        start = pl.multiple_of(start, ROUTE_ROWS)
        return pltpu.make_async_copy(zero_s, xs_ref.at[pl.ds(start, ROUTE_ROWS)], sem.at[2])

    @pl.when(i == 0)
    def _():
        zero_s[...] = jnp.zeros_like(zero_s)

        def start_tail(e, c):
            @pl.when(ztail_ref[e] >= 0)
            def _():
                zero_block(ztail_ref[e]).start()
            return c

        def wait_tail(e, c):
            @pl.when(ztail_ref[e] >= 0)
            def _():
                zero_block(ztail_ref[e]).wait()
            return c

        def start_unused(b, c):
            zero_block(b * ROUTE_ROWS).start()
            return c

        def wait_unused(b, c):
            zero_block(b * ROUTE_ROWS).wait()
            return c

        lax.fori_loop(0, N_EXPERTS, start_tail, 0)
        lax.fori_loop(nblk_ref[0], nb_max, start_unused, 0)
        lax.fori_loop(0, N_EXPERTS, wait_tail, 0)
        lax.fori_loop(nblk_ref[0], nb_max, wait_unused, 0)

    def row_copy(o, j, dref, k):
        dst = dref[0, o * SUBLANES + j]
        return pltpu.make_async_copy(x_ref.at[o, pl.ds(j, 1)], xs_ref.at[pl.ds(dst, 1)], sem.at[k])

    def issue(o, c):
        for j in range(SUBLANES):
            row_copy(o, j, d0_ref, 0).start(priority=0)
            row_copy(o, j, d1_ref, 1).start(priority=1)
        return c

    def drain(o, c):
        for j in range(SUBLANES):
            row_copy(o, j, d0_ref, 0).wait()
            row_copy(o, j, d1_ref, 1).wait()
        return c

    lax.fori_loop(0, x_ref.shape[0], issue, 0)
    lax.fori_loop(0, x_ref.shape[0], drain, 0)


def _dispatch(ztail, nblk, dest, xn, nb_max):
    n, d = xn.shape
    tmd = _tile(n, 256)
    nt = n // tmd
    d0 = dest[0].reshape(nt, 1, tmd)
    d1 = dest[1].reshape(nt, 1, tmd)
    idx_spec = pl.BlockSpec((None, 1, tmd), lambda i: (i, 0, 0), memory_space=pltpu.SMEM)
    smem = pl.BlockSpec(memory_space=pltpu.SMEM)
    return pl.pallas_call(
        _dispatch_kernel,
        grid=(nt,),
        in_specs=[smem, smem, idx_spec, idx_spec,
                  pl.BlockSpec((tmd // SUBLANES, SUBLANES, d), lambda i: (i, 0, 0))],
        out_specs=pl.BlockSpec(memory_space=pl.ANY),
        out_shape=jax.ShapeDtypeStruct((nb_max * ROUTE_ROWS, d), F32),
        scratch_shapes=[pltpu.VMEM((ROUTE_ROWS, d), F32), pltpu.SemaphoreType.DMA((3,))],
        compiler_params=_cparams("arbitrary"),
        name="dispatch_scatter",
    )(ztail, nblk, d0, d1, xn.reshape(n // SUBLANES, SUBLANES, d))


def _expert_kernel(be_ref, nb_ref, xs_ref, wg_ref, wu_ref, wd_ref, ys_ref, wg_s, wu_s, wd_s):
    b = pl.program_id(0)

    @pl.when(b < nb_ref[0])
    def _():
        changed = (b == 0) | (be_ref[b] != be_ref[jnp.maximum(b - 1, 0)])

        @pl.when(changed)
        def _():
            wg_s[...] = wg_ref[...].astype(BF16)
            wu_s[...] = wu_ref[...].astype(BF16)
            wd_s[...] = wd_ref[...].astype(BF16)

        x = xs_ref[...].astype(BF16)
        g = _dot(x, wg_s[...])
        u = _dot(x, wu_s[...])
        hb = (jax.nn.silu(g) * u).astype(BF16)
        ys_ref[...] = _dot(hb, wd_s[...])

    @pl.when(b >= nb_ref[0])
    def _():
        ys_ref[...] = jnp.zeros_like(ys_ref)


def _experts(blk_e, nblk, xs, w_gate, w_up, w_down, layer):
    p, d = xs.shape
    f = w_gate.shape[3]
    nb_max = p // ROUTE_ROWS

    def row_map(b, be, nb):
        return (jnp.minimum(b, nb[0] - 1), 0)

    def w_map(b, be, nb):
        return (layer, be[jnp.minimum(b, nb[0] - 1)], 0, 0)

    grid_spec = pltpu.PrefetchScalarGridSpec(
        num_scalar_prefetch=2,
        grid=(nb_max,),
        in_specs=[pl.BlockSpec((ROUTE_ROWS, d), row_map),
                  pl.BlockSpec((None, None, d, f), w_map),
                  pl.BlockSpec((None, None, d, f), w_map),
                  pl.BlockSpec((None, None, f, d), w_map)],
        out_specs=pl.BlockSpec((ROUTE_ROWS, d), lambda b, be, nb: (b, 0)),
        scratch_shapes=[pltpu.VMEM((d, f), BF16), pltpu.VMEM((d, f), BF16), pltpu.VMEM((f, d), BF16)],
    )
    return pl.pallas_call(
        _expert_kernel,
        grid_spec=grid_spec,
        out_shape=jax.ShapeDtypeStruct((p, d), F32),
        compiler_params=_cparams("arbitrary"),
        name="expert_mlp",
    )(blk_e, nblk, xs, w_gate, w_up, w_down)


def _combine_kernel(d0_ref, d1_ref, ys_ref, h1_ref, meta_ref, p_ref, gp_ref, wpg_ref, bpg_ref,
                    wp_ref, gf_ref, out_ref, ya_s, yb_s, sem, *, final):
    tmc = h1_ref.shape[0]

    def row_copy(o, j, dref, dst, k):
        src = dref[0, o * SUBLANES + j]
        return pltpu.make_async_copy(ys_ref.at[pl.ds(src, 1)], dst.at[o, pl.ds(j, 1)], sem.at[k])

    def issue(o, c):
        for j in range(SUBLANES):
            row_copy(o, j, d0_ref, ya_s, 0).start(priority=0)
            row_copy(o, j, d1_ref, yb_s, 1).start(priority=1)
        return c

    def drain(o, c):
        for j in range(SUBLANES):
            row_copy(o, j, d0_ref, ya_s, 0).wait()
            row_copy(o, j, d1_ref, yb_s, 1).wait()
        return c

    lax.fori_loop(0, tmc // SUBLANES, issue, 0)
    lax.fori_loop(0, tmc // SUBLANES, drain, 0)

    meta = meta_ref[...]
    w0 = meta[:, M_W0:M_W0 + 1]
    w1 = meta[:, M_W1:M_W1 + 1]
    ya = ya_s[...].reshape(tmc, -1)
    yb = yb_s[...].reshape(tmc, -1)
    h2 = h1_ref[...] + (ya * w0 + yb * w1)
    xn = _rms(h2, gp_ref[...]).astype(BF16)
    gate = jax.nn.sigmoid(_dot(xn, wpg_ref[...]) + bpg_ref[...])
    pp = _dot(p_ref[...].astype(BF16), wp_ref[...])
    h3 = h2 + gate * pp
    if final:
        h3 = _rms(h3, gf_ref[...])
    out_ref[...] = h3


def _combine(dest, ys, h1, meta, p, layer, g_ple, w_ple_gate, b_ple_gate, w_ple, g_final, final):
    n, d = h1.shape
    pd = p.shape[2]
    tmc = _tile(n, 256)
    nt = n // tmc
    d0 = dest[0].reshape(nt, 1, tmc)
    d1 = dest[1].reshape(nt, 1, tmc)
    idx_spec = pl.BlockSpec((None, 1, tmc), lambda i: (i, 0, 0), memory_space=pltpu.SMEM)
    vec = pl.BlockSpec((1, d), lambda i: (0, 0))
    return pl.pallas_call(
        functools.partial(_combine_kernel, final=final),
        grid=(nt,),
        in_specs=[idx_spec, idx_spec,
                  pl.BlockSpec(memory_space=pl.ANY),
                  pl.BlockSpec((tmc, d), lambda i: (i, 0)),
                  pl.BlockSpec((tmc, LANES), lambda i: (i, 0)),
                  pl.BlockSpec((None, tmc, pd), lambda i: (layer, i, 0)),
                  vec,
                  pl.BlockSpec((d, d), lambda i: (0, 0)),
                  vec,
                  pl.BlockSpec((pd, d), lambda i: (0, 0)),
                  vec],
        out_specs=pl.BlockSpec((tmc, d), lambda i: (i, 0)),
        out_shape=jax.ShapeDtypeStruct((n, d), F32),
        scratch_shapes=[pltpu.VMEM((tmc // SUBLANES, SUBLANES, d), F32),
                        pltpu.VMEM((tmc // SUBLANES, SUBLANES, d), F32),
                        pltpu.SemaphoreType.DMA((2,))],
        compiler_params=_cparams("arbitrary"),
        name="combine_ple",
    )(d0, d1, ys, h1, meta, p, g_ple, w_ple_gate, b_ple_gate, w_ple, g_final)


def _row(v):
    return v.reshape(1, -1)


def _moe_and_ple(z, w_out, h, i, final, norm_ffn, w_router_group, b_router_group, w_router_expert,
                 b_router_expert, w_exp_gate, w_exp_up, w_exp_down, norm_ple, w_ple, w_ple_gate,
                 b_ple_gate, norm_final, p):
    n, d = h.shape
    pad = LANES - N_GROUPS - N_EXPERTS
    w_router = jnp.concatenate(
        [w_router_group[i], w_router_expert[i], jnp.zeros((d, pad), F32)], axis=1)
    b_router = jnp.concatenate(
        [b_router_group[i], b_router_expert[i], jnp.zeros((pad,), F32)]).reshape(1, LANES)
    h1, xn, meta, meta_t, cnt = _post_mix(z, w_out, h, _row(norm_ffn[i]), w_router, b_router)
    cnt_i32 = cnt[0, EXPERT_LANE0:EXPERT_LANE0 + N_EXPERTS].astype(I32)
    nb_max = (2 * n) // ROUTE_ROWS + N_EXPERTS
    dest, blk_e, nblk, ztail = _plan(cnt_i32, meta_t, nb_max)
    xs = _dispatch(ztail, nblk, dest, xn, nb_max)
    ys = _experts(blk_e, nblk, xs, w_exp_gate, w_exp_up, w_exp_down, i)
    return _combine(dest, ys, h1, meta, p.reshape(p.shape[0], n, -1), i, _row(norm_ple[i]),
                    w_ple_gate[i].astype(BF16), _row(b_ple_gate[i]), w_ple[i].astype(BF16),
                    _row(norm_final), final)


def kernel(x, p, norm_mix, w_in_a, conv_a, w_out_a, w_in_b, conv_b, conv_bias_b, w_rgate_b, b_rgate_b, w_igate_b, b_igate_b, lam_b, w_out_b, norm_ffn, w_router_group, b_router_group, w_router_expert, b_router_expert, w_exp_gate, w_exp_up, w_exp_down, norm_ple, w_ple, w_ple_gate, b_ple_gate, norm_final):
    batch, seq, d = x.shape
    depth = p.shape[0]
    n = batch * seq
    h = x.reshape(n, d)
    for i in range(depth):
        j = i // 2
        if i % 2 == 0:
            bg, v = _mix_a_in(h, _row(norm_mix[i]), w_in_a[j].astype(BF16))
            z = _mix_a_conv(v, bg, conv_a[j], seq)
            w_out = w_out_a[j]
        else:
            tt = _tile(seq, 256)
            yb, u_tiles = _mix_b_in(h, _row(norm_mix[i]), w_in_b[j].astype(BF16), batch, seq, tt)
            z = _rglru(u_tiles, yb, conv_b[j], _row(conv_bias_b[j]),
                       w_rgate_b[j].astype(BF16), b_rgate_b[j], w_igate_b[j].astype(BF16),
                       b_igate_b[j], lam_b[j], seq, tt)
            w_out = w_out_b[j]
        h = _moe_and_ple(z, w_out.astype(BF16), h, i, i == depth - 1, norm_ffn, w_router_group,
                         b_router_group, w_router_expert, b_router_expert, w_exp_gate, w_exp_up,
                         w_exp_down, norm_ple, w_ple, w_ple_gate, b_ple_gate, norm_final, p)
    return h.reshape(batch, seq, d)
```

```python
import functools

import jax
import jax.numpy as jnp
from jax import lax
from jax.experimental import pallas as pl
from jax.experimental.pallas import tpu as pltpu

F32 = jnp.float32
BF16 = jnp.bfloat16
I32 = jnp.int32
U32 = jnp.uint32

EPS = 1e-6
N_GROUPS = 4
EXPERTS_PER_GROUP = 8
N_EXPERTS = N_GROUPS * EXPERTS_PER_GROUP
RG_C = 8.0

LANES = 128
SUBLANES = 8
BF16_ROWS = 16
EXPERT_LANE0 = N_GROUPS
ROUTE_ROWS = 256
M_E0, M_E1, M_R0, M_R1, M_W0, M_W1 = 0, 1, 2, 3, 4, 5
VMEM_LIMIT = 48 * 1024 * 1024


def _cparams(*sem):
    return pltpu.CompilerParams(dimension_semantics=sem, vmem_limit_bytes=VMEM_LIMIT)


def _rms(x, g):
    return x * lax.rsqrt(jnp.mean(x * x, axis=-1, keepdims=True) + EPS) * g


def _dot(a, b):
    return jnp.dot(a, b, preferred_element_type=F32)


def _pack_bf16_pairs(x):
    k = x.shape[1] // 2
    bits = lax.bitcast_convert_type(x, U32)
    bits = bits + (jnp.uint32(0x7FFF) + ((bits >> 16) & jnp.uint32(1)))
    return (bits[:, :k] & jnp.uint32(0xFFFF0000)) | (bits[:, k:] >> 16)


def _unpack_bf16_pairs(p):
    hi = lax.bitcast_convert_type(p & jnp.uint32(0xFFFF0000), F32)
    lo = lax.bitcast_convert_type(p << 16, F32)
    return hi, lo


def _tile(n, want):
    t = min(n, want)
    assert n % t == 0, (n, want)
    return t


def _mix_a_in_kernel(h_ref, g_ref, w_ref, bg_ref, v_ref):
    d = h_ref.shape[1]
    xn = _rms(h_ref[...], g_ref[...]).astype(BF16)
    bg = _dot(xn, w_ref[:, 0:d])
    cg = _dot(xn, w_ref[:, d:2 * d])
    hh = _dot(xn, w_ref[:, 2 * d:3 * d])
    bg_ref[...] = bg.astype(BF16)
    v_ref[...] = (cg * hh).astype(BF16)


def _mix_a_in(h, g, w_in):
    n, d = h.shape
    tm = _tile(n, 512)
    return pl.pallas_call(
        _mix_a_in_kernel,
        grid=(n // tm,),
        in_specs=[pl.BlockSpec((tm, d), lambda i: (i, 0)),
                  pl.BlockSpec((1, d), lambda i: (0, 0)),
                  pl.BlockSpec((d, 3 * d), lambda i: (0, 0))],
        out_specs=[pl.BlockSpec((tm, d), lambda i: (i, 0)),
                   pl.BlockSpec((tm, d), lambda i: (i, 0))],
        out_shape=[jax.ShapeDtypeStruct((n, d), BF16), jax.ShapeDtypeStruct((n, d), BF16)],
        compiler_params=_cparams("parallel"),
        name="mix_a_in",
    )(h, g, w_in)


def _mix_a_conv_kernel(v_ref, vp_ref, vn_ref, bg_ref, cw_ref, z_ref, *, tiles_per_seq):
    i = pl.program_id(0)
    tm = v_ref.shape[0]
    v = v_ref[...].astype(F32)
    at_start = (i % tiles_per_seq) == 0
    at_end = (i % tiles_per_seq) == tiles_per_seq - 1
    prev_row = vp_ref[...].astype(F32)[BF16_ROWS - 1:BF16_ROWS, :]
    next_row = vn_ref[...].astype(F32)[0:1, :]
    prev_row = jnp.where(at_start, 0.0, prev_row)
    next_row = jnp.where(at_end, 0.0, next_row)
    row = lax.broadcasted_iota(I32, (tm, 1), 0)
    v_dn = jnp.where(row == 0, prev_row, pltpu.roll(v, 1, 0))
    v_up = jnp.where(row == tm - 1, next_row, pltpu.roll(v, tm - 1, 0))
    cw = cw_ref[...]
    u = cw[0:1, :] * v_dn + cw[1:2, :] * v + cw[2:3, :] * v_up
    z_ref[...] = (bg_ref[...].astype(F32) * u).astype(BF16)


def _mix_a_conv(v, bg, conv_w, seq):
    n, d = v.shape
    tm = _tile(seq, 512)
    hb = tm // BF16_ROWS
    nhalo = n // BF16_ROWS
    return pl.pallas_call(
        functools.partial(_mix_a_conv_kernel, tiles_per_seq=seq // tm),
        grid=(n // tm,),
        in_specs=[pl.BlockSpec((tm, d), lambda i: (i, 0)),
                  pl.BlockSpec((BF16_ROWS, d), lambda i: (jnp.maximum(i * hb - 1, 0), 0)),
                  pl.BlockSpec((BF16_ROWS, d), lambda i: (jnp.minimum((i + 1) * hb, nhalo - 1), 0)),
                  pl.BlockSpec((tm, d), lambda i: (i, 0)),
                  pl.BlockSpec(conv_w.shape, lambda i: (0, 0))],
        out_specs=pl.BlockSpec((tm, d), lambda i: (i, 0)),
        out_shape=jax.ShapeDtypeStruct((n, d), BF16),
        compiler_params=_cparams("parallel"),
        name="mix_a_conv",
    )(v, v, v, bg, conv_w)


def _mix_b_in_kernel(h_ref, hp_ref, hn_ref, g_ref, w_ref, yb_ref, u_ref, *, steps_per_seq, tt):
    i = pl.program_id(0)
    r = yb_ref.shape[1]
    tiles = h_ref.shape[0] // tt
    g = g_ref[...]
    xn = _rms(h_ref[...], g).astype(BF16)
    yb_ref[...] = jax.nn.gelu(_dot(xn, w_ref[:, 0:r])).astype(BF16)
    u = _dot(xn, w_ref[:, r:2 * r])
    xh = _rms(jnp.concatenate([hp_ref[...], hn_ref[...]], axis=0), g).astype(BF16)
    uh = _dot(xh, w_ref[:, r:2 * r])
    at_start = (i % steps_per_seq) == 0
    at_end = (i % steps_per_seq) == steps_per_seq - 1
    prev2 = jnp.where(at_start, 0.0, uh[SUBLANES - 2:SUBLANES, :])
    next1 = jnp.where(at_end, 0.0, uh[SUBLANES:SUBLANES + 1, :])
    row = lax.broadcasted_iota(I32, (SUBLANES, 1), 0)
    for k in range(tiles):
        p2 = prev2 if k == 0 else u[k * tt - 2:k * tt]
        n1 = next1 if k == tiles - 1 else u[(k + 1) * tt:(k + 1) * tt + 1]
        pad = jnp.where(row == 0, p2[0:1], jnp.where(row == 1, p2[1:2], jnp.where(row == 2, n1, 0.0)))
        for s in range(r // LANES):
            lanes = slice(s * LANES, (s + 1) * LANES)
            u_ref[k, s, 0:tt, :] = u[k * tt:(k + 1) * tt, lanes]
            u_ref[k, s, tt:tt + SUBLANES, :] = pad[:, lanes]


def _mix_b_in(h, g, w_in, batch, seq, tt):
    n, d = h.shape
    r = w_in.shape[1] // 2
    tm = _tile(seq, 512)
    steps_per_seq = seq // tm
    tiles = tm // tt
    hb = tm // SUBLANES
    nhalo = n // SUBLANES
    return pl.pallas_call(
        functools.partial(_mix_b_in_kernel, steps_per_seq=steps_per_seq, tt=tt),
        grid=(n // tm,),
        in_specs=[pl.BlockSpec((tm, d), lambda i: (i, 0)),
                  pl.BlockSpec((SUBLANES, d), lambda i: (jnp.maximum(i * hb - 1, 0), 0)),
                  pl.BlockSpec((SUBLANES, d), lambda i: (jnp.minimum((i + 1) * hb, nhalo - 1), 0)),
                  pl.BlockSpec((1, d), lambda i: (0, 0)),
                  pl.BlockSpec((d, 2 * r), lambda i: (0, 0))],
        out_specs=[pl.BlockSpec((tm, r), lambda i: (i, 0)),
                   pl.BlockSpec((tiles, r // LANES, None, tt + SUBLANES, LANES),
                                lambda i: (i % steps_per_seq, 0, i // steps_per_seq, 0, 0))],
        out_shape=[jax.ShapeDtypeStruct((n, r), BF16),
                   jax.ShapeDtypeStruct((seq // tt, r // LANES, batch, tt + SUBLANES, LANES), F32)],
        compiler_params=_cparams("parallel"),
        name="mix_b_in",
    )(h, h, h, g, w_in)


def _rglru_pass_kernel(*refs, reverse, tt):
    if reverse:
        (u_ref, cw_ref, cb_ref, wr_ref, br_ref, wi_ref, bi_ref, lam_ref, hf_ref, yb_ref,
         out_ref, x_s, a_s, g_s, carry_s, nat_s) = refs
    else:
        (u_ref, cw_ref, cb_ref, wr_ref, br_ref, wi_ref, bi_ref, lam_ref,
         out_ref, x_s, a_s, g_s, carry_s) = refs
    nb = SUBLANES
    ttp = tt + SUBLANES
    rows = tt * nb
    slabs = u_ref.shape[0]

    @pl.when(pl.program_id(1) == 0)
    def _():
        carry_s[...] = jnp.zeros_like(carry_s)

    def gather_t(tl, dst):
        for s in range(slabs):
            x_s[pl.ds(dst, nb), s * LANES:(s + 1) * LANES] = u_ref[s, pl.ds(tl, nb, stride=ttp), :]

    def load_t(tl, c):
        gather_t(tl, pl.multiple_of((tl + 2) * nb, nb))
        return c

    lax.fori_loop(0, tt, load_t, 0, unroll=8)
    gather_t(tt, 0)
    gather_t(tt + 1, nb)
    gather_t(tt + 2, (tt + 2) * nb)

    cw = cw_ref[...]
    u = (cw[0:1, :] * x_s[0:rows, :] + cw[1:2, :] * x_s[nb:nb + rows, :]
         + cw[2:3, :] * x_s[2 * nb:2 * nb + rows, :] + cw[3:4, :] * x_s[3 * nb:3 * nb + rows, :]
         + cb_ref[...])
    ub = u.astype(BF16)
    r = 0.5 * (1.0 + jnp.tanh(0.5 * (_dot(ub, wr_ref[...]) + br_ref[...])))
    ig = 0.5 * (1.0 + jnp.tanh(0.5 * (_dot(ub, wi_ref[...]) + bi_ref[...])))
    log_a = -RG_C * r * jax.nn.softplus(-lam_ref[...])
    th = jnp.tanh(0.5 * log_a)
    q = 1.0 / (1.0 - th)
    a_s[...] = (1.0 + th) * q
    m2 = -4.0 * th * q * q
    g_s[...] = jnp.where(m2 > 0.0, m2 * lax.rsqrt(m2), 0.0) * (ig * u)

    def step(k, h):
        tl = tt - 1 - k if reverse else k
        base = pl.multiple_of(tl * nb, nb)
        h = a_s[pl.ds(base, nb), :] * h + g_s[pl.ds(base, nb), :]
        if reverse:
            hs = h + hf_ref[pl.ds(base, nb), :]
            for s in range(slabs):
                nat_s[s, pl.ds(tl, nb, stride=ttp), :] = hs[:, s * LANES:(s + 1) * LANES]
        else:
            out_ref[pl.ds(base, nb), :] = h
        return h

    carry_s[...] = lax.fori_loop(0, tt, step, carry_s[...], unroll=8)
    if reverse:
        for b in range(nb):
            for s in range(slabs):
                lanes = slice(s * LANES, (s + 1) * LANES)
                hb = nat_s[s, b * ttp:b * ttp + tt, :]
                out_ref[b, :, lanes] = (yb_ref[b, :, lanes].astype(F32) * hb).astype(BF16)


def _rglru_pass(u_tiles, conv_w, conv_b, w_r, b_r, w_i, b_i, lam, hf, yb, seq, tt, reverse):
    nt, nslab, batch, ttp, _ = u_tiles.shape
    r = nslab * LANES
    assert batch == SUBLANES, "time-major tile layout puts the batch on the 8 sublanes"
    nh, bk = w_r.shape[1], w_r.shape[2]
    slabs = bk // LANES
    kw = conv_w.shape[0]
    d = 1 if reverse else 0
    rows = tt * batch

    def tile(jj):
        return nt - 1 - jj if reverse else jj

    vec = pl.BlockSpec((None, 1, bk), lambda h, jj: (d, 0, h))
    mat = pl.BlockSpec((None, None, bk, bk), lambda h, jj: (d, h, 0, 0))
    in_specs = [pl.BlockSpec((None, slabs, batch * ttp, LANES), lambda h, jj: (tile(jj), h, 0, 0)),
                pl.BlockSpec((kw, bk), lambda h, jj: (0, h)),
                pl.BlockSpec((1, bk), lambda h, jj: (0, h)),
                mat, vec, mat, vec, vec]
    args = [u_tiles.reshape(nt, nslab, batch * ttp, LANES), conv_w, conv_b, w_r, b_r.reshape(2, 1, r),
            w_i, b_i.reshape(2, 1, r), lam.reshape(2, 1, r)]
    scratch = [pltpu.VMEM(((tt + 3) * batch, bk), F32), pltpu.VMEM((rows, bk), F32),
               pltpu.VMEM((rows, bk), F32), pltpu.VMEM((batch, bk), F32)]
    if reverse:
        in_specs += [pl.BlockSpec((rows, bk), lambda h, jj: (tile(jj), h)),
                     pl.BlockSpec((batch, tt, bk), lambda h, jj: (0, tile(jj), h))]
        args += [hf, yb.reshape(batch, seq, r)]
        scratch += [pltpu.VMEM((slabs, batch * ttp, LANES), F32)]
        out_spec = pl.BlockSpec((batch, tt, bk), lambda h, jj: (0, tile(jj), h))
        out_shape = jax.ShapeDtypeStruct((batch, seq, r), BF16)
    else:
        out_spec = pl.BlockSpec((rows, bk), lambda h, jj: (tile(jj), h))
        out_shape = jax.ShapeDtypeStruct((seq * batch, r), F32)
    return pl.pallas_call(
        functools.partial(_rglru_pass_kernel, reverse=reverse, tt=tt),
        grid=(nh, nt),
        in_specs=in_specs,
        out_specs=out_spec,
        out_shape=out_shape,
        scratch_shapes=scratch,
        compiler_params=_cparams("parallel", "arbitrary"),
        name="rglru_rev" if reverse else "rglru_fwd",
    )(*args)


def _rglru(u_tiles, yb, conv_w, conv_b, w_r, b_r, w_i, b_i, lam, seq, tt):
    hf = _rglru_pass(u_tiles, conv_w, conv_b, w_r, b_r, w_i, b_i, lam, None, None, seq, tt, False)
    z = _rglru_pass(u_tiles, conv_w, conv_b, w_r, b_r, w_i, b_i, lam, hf, yb, seq, tt, True)
    return z.reshape(-1, z.shape[2])


def _post_mix_kernel(z_ref, wo_ref, h_ref, g_ref, wr_ref, br_ref,
                     h1_ref, xn_ref, meta_ref, meta_t_ref, cnt_ref, carry_ref):
    i = pl.program_id(0)
    tm = z_ref.shape[0]

    @pl.when(i == 0)
    def _():
        carry_ref[...] = jnp.zeros_like(carry_ref)

    h1 = h_ref[...] + _dot(z_ref[...], wo_ref[...])
    h1_ref[...] = h1
    xn = _rms(h1, g_ref[...])
    xn_ref[...] = _pack_bf16_pairs(xn)

    x_hi = xn.astype(BF16)
    x_lo = (xn - x_hi.astype(F32)).astype(BF16)
    w = wr_ref[...]
    w_hi = w.astype(BF16)
    w_lo = (w - w_hi.astype(F32)).astype(BF16)
    logits = _dot(x_hi, w_hi) + _dot(x_hi, w_lo) + _dot(x_lo, w_hi) + br_ref[...]

    lane = lax.broadcasted_iota(I32, (tm, LANES), 1).astype(F32)
    neg = jnp.float32(-jnp.inf)
    nolane = jnp.float32(LANES)
    gmask = lane < N_GROUPS
    gmax = jnp.max(jnp.where(gmask, logits, neg), axis=1, keepdims=True)
    eg = jnp.where(gmask, jnp.exp(logits - gmax), 0.0)
    pg = eg / jnp.sum(eg, axis=1, keepdims=True)
    pg_top = jnp.max(pg, axis=1, keepdims=True)
    g_top = jnp.min(jnp.where(gmask & (pg == pg_top), lane, nolane), axis=1, keepdims=True)
    lo = EXPERT_LANE0 + g_top * EXPERTS_PER_GROUP
    emask = (lane >= lo) & (lane < lo + EXPERTS_PER_GROUP)
    emax = jnp.max(jnp.where(emask, logits, neg), axis=1, keepdims=True)
    ee = jnp.where(emask, jnp.exp(logits - emax), 0.0)
    pe = jnp.where(emask, ee / jnp.sum(ee, axis=1, keepdims=True), -1.0)
    p1 = jnp.max(pe, axis=1, keepdims=True)
    i1 = jnp.min(jnp.where(pe == p1, lane, nolane), axis=1, keepdims=True)
    pe2 = jnp.where(lane == i1, -1.0, pe)
    p2 = jnp.max(pe2, axis=1, keepdims=True)
    i2 = jnp.min(jnp.where(pe2 == p2, lane, nolane), axis=1, keepdims=True)
    psum = p1 + p2
    w0 = pg_top * (p1 / psum)
    w1 = pg_top * (p2 / psum)

    oh0 = lane == i1
    oh1 = lane == i2
    both = (oh0 | oh1).astype(BF16)
    ri = lax.broadcasted_iota(I32, (tm, tm), 0)
    ci = lax.broadcasted_iota(I32, (tm, tm), 1)
    before = (ri > ci).astype(BF16)
    cnt_before = _dot(before, both) + carry_ref[...]
    rank0 = jnp.sum(jnp.where(oh0, cnt_before, 0.0), axis=1, keepdims=True)
    rank1 = jnp.sum(jnp.where(oh1, cnt_before, 0.0), axis=1, keepdims=True)
    carry = carry_ref[...] + jnp.sum(both.astype(F32), axis=0, keepdims=True)
    carry_ref[...] = carry
    cnt_ref[...] = carry

    e0 = i1 - EXPERT_LANE0
    e1 = i2 - EXPERT_LANE0
    meta = jnp.zeros((tm, LANES), F32)
    for ln, val in ((M_E0, e0), (M_E1, e1), (M_R0, rank0), (M_R1, rank1), (M_W0, w0), (M_W1, w1)):
        meta = jnp.where(lane == ln, val, meta)
    meta_ref[...] = meta
    meta_t_ref[...] = meta.T[0:SUBLANES, :]


def _post_mix(z, w_out, h, g, w_router, b_router):
    n, d = h.shape
    k = z.shape[1]
    tm = _tile(n, 512)
    return pl.pallas_call(
        _post_mix_kernel,
        grid=(n // tm,),
        in_specs=[pl.BlockSpec((tm, k), lambda i: (i, 0)),
                  pl.BlockSpec((k, d), lambda i: (0, 0)),
                  pl.BlockSpec((tm, d), lambda i: (i, 0)),
                  pl.BlockSpec((1, d), lambda i: (0, 0)),
                  pl.BlockSpec((d, LANES), lambda i: (0, 0)),
                  pl.BlockSpec((1, LANES), lambda i: (0, 0))],
        out_specs=[pl.BlockSpec((tm, d), lambda i: (i, 0)),
                   pl.BlockSpec((tm, d // 2), lambda i: (i, 0)),
                   pl.BlockSpec((tm, LANES), lambda i: (i, 0)),
                   pl.BlockSpec((SUBLANES, tm), lambda i: (0, i)),
                   pl.BlockSpec((1, LANES), lambda i: (0, 0))],
        out_shape=[jax.ShapeDtypeStruct((n, d), F32), jax.ShapeDtypeStruct((n, d // 2), U32),
                   jax.ShapeDtypeStruct((n, LANES), F32), jax.ShapeDtypeStruct((SUBLANES, n), F32),
                   jax.ShapeDtypeStruct((1, LANES), F32)],
        scratch_shapes=[pltpu.VMEM((1, LANES), F32)],
        compiler_params=_cparams("arbitrary"),
        name="post_mix_router",
    )(z, w_out, h, g, w_router, b_router)


def _plan_kernel(cnt_ref, meta_t_ref, dest_ref, blk_e_ref, nblk_ref, ztail_ref, pstart_ref):
    nb_max = blk_e_ref.shape[0]

    def per_expert(e, acc):
        nb_e = (cnt_ref[e] + (ROUTE_ROWS - 1)) // ROUTE_ROWS
        pstart_ref[e] = acc * ROUTE_ROWS

        def fill(j, c):
            blk_e_ref[acc + j] = e
            return c

        lax.fori_loop(0, nb_e, fill, 0)
        ztail_ref[e] = jnp.where(nb_e > 0, (acc + nb_e - 1) * ROUTE_ROWS, -1)
        return acc + nb_e

    used = lax.fori_loop(0, N_EXPERTS, per_expert, 0)
    nblk_ref[0] = used

    def fill_rest(b, c):
        blk_e_ref[b] = N_EXPERTS - 1
        return c

    lax.fori_loop(used, nb_max, fill_rest, 0)

    e0 = meta_t_ref[M_E0:M_E0 + 1, :]
    e1 = meta_t_ref[M_E1:M_E1 + 1, :]
    d0 = meta_t_ref[M_R0:M_R0 + 1, :]
    d1 = meta_t_ref[M_R1:M_R1 + 1, :]
    for e in range(N_EXPERTS):
        ps = pstart_ref[e].astype(F32)
        d0 = d0 + jnp.where(e0 == e, ps, 0.0)
        d1 = d1 + jnp.where(e1 == e, ps, 0.0)
    dest_ref[...] = jnp.zeros_like(dest_ref)
    dest_ref[0:1, :] = d0.astype(I32)
    dest_ref[1:2, :] = d1.astype(I32)


def _plan(cnt_i32, meta_t, nb_max):
    n = meta_t.shape[1]
    smem = pl.BlockSpec(memory_space=pltpu.SMEM)
    return pl.pallas_call(
        _plan_kernel,
        in_specs=[smem, pl.BlockSpec((SUBLANES, n), lambda: (0, 0))],
        out_specs=[pl.BlockSpec((SUBLANES, n), lambda: (0, 0)), smem, smem, smem],
        out_shape=[jax.ShapeDtypeStruct((SUBLANES, n), I32),
                   jax.ShapeDtypeStruct((nb_max,), I32),
                   jax.ShapeDtypeStruct((1,), I32),
                   jax.ShapeDtypeStruct((N_EXPERTS,), I32)],
        scratch_shapes=[pltpu.SMEM((N_EXPERTS,), I32)],
        name="dispatch_plan",
    )(cnt_i32, meta_t)


def _dispatch_kernel(ztail_ref, nblk_ref, d0_ref, d1_ref, x_ref, xs_ref, zero_s, sem):
    i = pl.program_id(0)
    nb_max = xs_ref.shape[0] // ROUTE_ROWS

    def zero_block(start):
        start = pl.multiple_of(start, ROUTE_ROWS)
        return pltpu.make_async_copy(zero_s, xs_ref.at[pl.ds(start, ROUTE_ROWS)], sem.at[2])

    @pl.when(i == 0)
    def _():
        zero_s[...] = jnp.zeros_like(zero_s)

        def start_tail(e, c):
            @pl.when(ztail_ref[e] >= 0)
            def _():
                zero_block(ztail_ref[e]).start()
            return c

        def wait_tail(e, c):
            @pl.when(ztail_ref[e] >= 0)
            def _():
                zero_block(ztail_ref[e]).wait()
            return c

        def start_unused(b, c):
            zero_block(b * ROUTE_ROWS).start()
            return c

        def wait_unused(b, c):
            zero_block(b * ROUTE_ROWS).wait()
            return c

        lax.fori_loop(0, N_EXPERTS, start_tail, 0)
        lax.fori_loop(nblk_ref[0], nb_max, start_unused, 0)
        lax.fori_loop(0, N_EXPERTS, wait_tail, 0)
        lax.fori_loop(nblk_ref[0], nb_max, wait_unused, 0)

    def row_copy(o, j, dref, k):
        dst = dref[0, o * SUBLANES + j]
        return pltpu.make_async_copy(x_ref.at[o, pl.ds(j, 1)], xs_ref.at[pl.ds(dst, 1)], sem.at[k])

    def issue(o, c):
        for j in range(SUBLANES):
            row_copy(o, j, d0_ref, 0).start(priority=0)
            row_copy(o, j, d1_ref, 1).start(priority=1)
        return c

    def drain(o, c):
        for j in range(SUBLANES):
            row_copy(o, j, d0_ref, 0).wait()
            row_copy(o, j, d1_ref, 1).wait()
        return c

    lax.fori_loop(0, x_ref.shape[0], issue, 0)
    lax.fori_loop(0, x_ref.shape[0], drain, 0)


def _dispatch(ztail, nblk, dest, xn, nb_max):
    n, d = xn.shape
    tmd = _tile(n, 256)
    nt = n // tmd
    d0 = dest[0].reshape(nt, 1, tmd)
    d1 = dest[1].reshape(nt, 1, tmd)
    idx_spec = pl.BlockSpec((None, 1, tmd), lambda i: (i, 0, 0), memory_space=pltpu.SMEM)
    smem = pl.BlockSpec(memory_space=pltpu.SMEM)
    return pl.pallas_call(
        _dispatch_kernel,
        grid=(nt,),
        in_specs=[smem, smem, idx_spec, idx_spec,
                  pl.BlockSpec((tmd // SUBLANES, SUBLANES, d), lambda i: (i, 0, 0))],
        out_specs=pl.BlockSpec(memory_space=pl.ANY),
        out_shape=jax.ShapeDtypeStruct((nb_max * ROUTE_ROWS, d), xn.dtype),
        scratch_shapes=[pltpu.VMEM((ROUTE_ROWS, d), xn.dtype), pltpu.SemaphoreType.DMA((3,))],
        compiler_params=_cparams("arbitrary"),
        name="dispatch_scatter",
    )(ztail, nblk, d0, d1, xn.reshape(n // SUBLANES, SUBLANES, d))


def _expert_kernel(be_ref, nb_ref, xs_ref, wg_ref, wu_ref, wd_ref, ys_ref, wg_s, wu_s, wd_s):
    b = pl.program_id(0)

    @pl.when(b < nb_ref[0])
    def _():
        changed = (b == 0) | (be_ref[b] != be_ref[jnp.maximum(b - 1, 0)])

        @pl.when(changed)
        def _():
            wg_s[...] = wg_ref[...].astype(BF16)
            wu_s[...] = wu_ref[...].astype(BF16)
            wd_s[...] = wd_ref[...].astype(BF16)

        k = xs_ref.shape[1]
        x_hi, x_lo = _unpack_bf16_pairs(xs_ref[...])
        x_hi = x_hi.astype(BF16)
        x_lo = x_lo.astype(BF16)
        g = _dot(x_hi, wg_s[0:k, :]) + _dot(x_lo, wg_s[k:2 * k, :])
        u = _dot(x_hi, wu_s[0:k, :]) + _dot(x_lo, wu_s[k:2 * k, :])
        hb = (jax.nn.silu(g) * u).astype(BF16)
        ys_ref[...] = _pack_bf16_pairs(_dot(hb, wd_s[...]))

    @pl.when(b >= nb_ref[0])
    def _():
        ys_ref[...] = jnp.zeros_like(ys_ref)


def _experts(blk_e, nblk, xs, w_gate, w_up, w_down, layer):
    p, k = xs.shape
    d, f = w_gate.shape[2], w_gate.shape[3]
    nb_max = p // ROUTE_ROWS

    def row_map(b, be, nb):
        return (jnp.minimum(b, nb[0] - 1), 0)

    def w_map(b, be, nb):
        return (layer, be[jnp.minimum(b, nb[0] - 1)], 0, 0)

    grid_spec = pltpu.PrefetchScalarGridSpec(
        num_scalar_prefetch=2,
        grid=(nb_max,),
        in_specs=[pl.BlockSpec((ROUTE_ROWS, k), row_map),
                  pl.BlockSpec((None, None, d, f), w_map),
                  pl.BlockSpec((None, None, d, f), w_map),
                  pl.BlockSpec((None, None, f, d), w_map)],
        out_specs=pl.BlockSpec((ROUTE_ROWS, k), lambda b, be, nb: (b, 0)),
        scratch_shapes=[pltpu.VMEM((d, f), BF16), pltpu.VMEM((d, f), BF16), pltpu.VMEM((f, d), BF16)],
    )
    return pl.pallas_call(
        _expert_kernel,
        grid_spec=grid_spec,
        out_shape=jax.ShapeDtypeStruct((p, k), U32),
        compiler_params=_cparams("arbitrary"),
        name="expert_mlp",
    )(blk_e, nblk, xs, w_gate, w_up, w_down)


def _combine_kernel(d0_ref, d1_ref, d0n_ref, d1n_ref, ys_ref, h1_ref, meta_ref, p_ref, gp_ref,
                    wpg_ref, bpg_ref, wp_ref, gf_ref, out_ref, ya_s, yb_s, sem, *, final):
    i = pl.program_id(0)
    nt = pl.num_programs(0)
    tmc = h1_ref.shape[0]
    slot = i % 2

    def row_copy(o, j, dref, dst, s, k):
        src = dref[0, o * SUBLANES + j]
        return pltpu.make_async_copy(ys_ref.at[pl.ds(src, 1)], dst.at[s, o, pl.ds(j, 1)], sem.at[s, k])

    def gather(d0r, d1r, s, wait):
        def body(o, c):
            for j in range(SUBLANES):
                ca = row_copy(o, j, d0r, ya_s, s, 0)
                cb = row_copy(o, j, d1r, yb_s, s, 1)
                if wait:
                    ca.wait()
                    cb.wait()
                else:
                    ca.start(priority=0)
                    cb.start(priority=1)
            return c

        lax.fori_loop(0, tmc // SUBLANES, body, 0)

    @pl.when(i == 0)
    def _():
        gather(d0_ref, d1_ref, slot, False)

    @pl.when(i + 1 < nt)
    def _():
        gather(d0n_ref, d1n_ref, 1 - slot, False)

    gather(d0_ref, d1_ref, slot, True)

    meta = meta_ref[...]
    w0 = meta[:, M_W0:M_W0 + 1]
    w1 = meta[:, M_W1:M_W1 + 1]
    ya_hi, ya_lo = _unpack_bf16_pairs(ya_s[slot].reshape(tmc, -1))
    yb_hi, yb_lo = _unpack_bf16_pairs(yb_s[slot].reshape(tmc, -1))
    y = jnp.concatenate([ya_hi * w0 + yb_hi * w1, ya_lo * w0 + yb_lo * w1], axis=1)
    h2 = h1_ref[...] + y
    xn = _rms(h2, gp_ref[...]).astype(BF16)
    gate = jax.nn.sigmoid(_dot(xn, wpg_ref[...]) + bpg_ref[...])
    pp = _dot(p_ref[...].astype(BF16), wp_ref[...])
    h3 = h2 + gate * pp
    if final:
        h3 = _rms(h3, gf_ref[...])
    out_ref[...] = h3


def _combine(dest, ys, h1, meta, p, layer, g_ple, w_ple_gate, b_ple_gate, w_ple, g_final, final):
    n, d = h1.shape
    pd = p.shape[2]
    tmc = _tile(n, 256)
    nt = n // tmc
    d0 = dest[0].reshape(nt, 1, tmc)
    d1 = dest[1].reshape(nt, 1, tmc)
    idx_spec = pl.BlockSpec((None, 1, tmc), lambda i: (i, 0, 0), memory_space=pltpu.SMEM)
    nxt_spec = pl.BlockSpec((None, 1, tmc), lambda i: (jnp.minimum(i + 1, nt - 1), 0, 0),
                            memory_space=pltpu.SMEM)
    vec = pl.BlockSpec((1, d), lambda i: (0, 0))
    k = ys.shape[1]
    return pl.pallas_call(
        functools.partial(_combine_kernel, final=final),
        grid=(nt,),
        in_specs=[idx_spec, idx_spec, nxt_spec, nxt_spec,
                  pl.BlockSpec(memory_space=pl.ANY),
                  pl.BlockSpec((tmc, d), lambda i: (i, 0)),
                  pl.BlockSpec((tmc, LANES), lambda i: (i, 0)),
                  pl.BlockSpec((None, tmc, pd), lambda i: (layer, i, 0)),
                  vec,
                  pl.BlockSpec((d, d), lambda i: (0, 0)),
                  vec,
                  pl.BlockSpec((pd, d), lambda i: (0, 0)),
                  vec],
        out_specs=pl.BlockSpec((tmc, d), lambda i: (i, 0)),
        out_shape=jax.ShapeDtypeStruct((n, d), F32),
        scratch_shapes=[pltpu.VMEM((2, tmc // SUBLANES, SUBLANES, k), U32),
                        pltpu.VMEM((2, tmc // SUBLANES, SUBLANES, k), U32),
                        pltpu.SemaphoreType.DMA((2, 2))],
        compiler_params=_cparams("arbitrary"),
        name="combine_ple",
    )(d0, d1, d0, d1, ys, h1, meta, p, g_ple, w_ple_gate, b_ple_gate, w_ple, g_final)


def _row(v):
    return v.reshape(1, -1)


def _moe_and_ple(z, w_out, h, i, final, norm_ffn, w_router_group, b_router_group, w_router_expert,
                 b_router_expert, w_exp_gate, w_exp_up, w_exp_down, norm_ple, w_ple, w_ple_gate,
                 b_ple_gate, norm_final, p):
    n, d = h.shape
    pad = LANES - N_GROUPS - N_EXPERTS
    w_router = jnp.concatenate(
        [w_router_group[i], w_router_expert[i], jnp.zeros((d, pad), F32)], axis=1)
    b_router = jnp.concatenate(
        [b_router_group[i], b_router_expert[i], jnp.zeros((pad,), F32)]).reshape(1, LANES)
    h1, xn, meta, meta_t, cnt = _post_mix(z, w_out, h, _row(norm_ffn[i]), w_router, b_router)
    cnt_i32 = cnt[0, EXPERT_LANE0:EXPERT_LANE0 + N_EXPERTS].astype(I32)
    nb_max = (2 * n) // ROUTE_ROWS + N_EXPERTS
    dest, blk_e, nblk, ztail = _plan(cnt_i32, meta_t, nb_max)
    xs = _dispatch(ztail, nblk, dest, xn, nb_max)
    ys = _experts(blk_e, nblk, xs, w_exp_gate, w_exp_up, w_exp_down, i)
    return _combine(dest, ys, h1, meta, p.reshape(p.shape[0], n, -1), i, _row(norm_ple[i]),
                    w_ple_gate[i].astype(BF16), _row(b_ple_gate[i]), w_ple[i].astype(BF16),
                    _row(norm_final), final)


def kernel(x, p, norm_mix, w_in_a, conv_a, w_out_a, w_in_b, conv_b, conv_bias_b, w_rgate_b, b_rgate_b, w_igate_b, b_igate_b, lam_b, w_out_b, norm_ffn, w_router_group, b_router_group, w_router_expert, b_router_expert, w_exp_gate, w_exp_up, w_exp_down, norm_ple, w_ple, w_ple_gate, b_ple_gate, norm_final):
    batch, seq, d = x.shape
    depth = p.shape[0]
    n = batch * seq
    h = x.reshape(n, d)
    for i in range(depth):
        j = i // 2
        if i % 2 == 0:
            bg, v = _mix_a_in(h, _row(norm_mix[i]), w_in_a[j].astype(BF16))
            z = _mix_a_conv(v, bg, conv_a[j], seq)
            w_out = w_out_a[j]
        else:
            tt = _tile(seq, 256)
            yb, u_tiles = _mix_b_in(h, _row(norm_mix[i]), w_in_b[j].astype(BF16), batch, seq, tt)
            z = _rglru(u_tiles, yb, conv_b[j], _row(conv_bias_b[j]),
                       w_rgate_b[j].astype(BF16), b_rgate_b[j], w_igate_b[j].astype(BF16),
                       b_igate_b[j], lam_b[j], seq, tt)
            w_out = w_out_b[j]
        h = _moe_and_ple(z, w_out.astype(BF16), h, i, i == depth - 1, norm_ffn, w_router_group,
                         b_router_group, w_router_expert, b_router_expert, w_exp_gate, w_exp_up,
                         w_exp_down, norm_ple, w_ple, w_ple_gate, b_ple_gate, norm_final, p)
    return h.reshape(batch, seq, d)
```

```python
import functools

import jax
import jax.numpy as jnp
from jax import lax
from jax.experimental import pallas as pl
from jax.experimental.pallas import tpu as pltpu

F32 = jnp.float32
BF16 = jnp.bfloat16
I32 = jnp.int32
U32 = jnp.uint32

EPS = 1e-6
N_GROUPS = 4
EXPERTS_PER_GROUP = 8
N_EXPERTS = N_GROUPS * EXPERTS_PER_GROUP
RG_C = 8.0

LANES = 128
SUBLANES = 8
BF16_ROWS = 16
EXPERT_LANE0 = N_GROUPS
ROUTE_ROWS = 256
M_E0, M_E1, M_R0, M_R1, M_W0, M_W1 = 0, 1, 2, 3, 4, 5
VMEM_LIMIT = 48 * 1024 * 1024


def _cparams(*sem):
    return pltpu.CompilerParams(dimension_semantics=sem, vmem_limit_bytes=VMEM_LIMIT)


def _rms(x, g):
    return x * lax.rsqrt(jnp.mean(x * x, axis=-1, keepdims=True) + EPS) * g


def _dot(a, b):
    return jnp.dot(a, b, preferred_element_type=F32)


def _pack_bf16_pairs(x):
    k = x.shape[1] // 2
    bits = lax.bitcast_convert_type(x, U32)
    bits = bits + (jnp.uint32(0x7FFF) + ((bits >> 16) & jnp.uint32(1)))
    return (bits[:, :k] & jnp.uint32(0xFFFF0000)) | (bits[:, k:] >> 16)


def _unpack_bf16_pairs(p):
    hi = lax.bitcast_convert_type(p & jnp.uint32(0xFFFF0000), F32)
    lo = lax.bitcast_convert_type(p << 16, F32)
    return hi, lo


def _tile(n, want):
    t = min(n, want)
    assert n % t == 0, (n, want)
    return t


def _mix_a_in_kernel(h_ref, g_ref, w_ref, bg_ref, v_ref):
    d = h_ref.shape[1]
    xn = _rms(h_ref[...], g_ref[...]).astype(BF16)
    bg = _dot(xn, w_ref[:, 0:d])
    cg = _dot(xn, w_ref[:, d:2 * d])
    hh = _dot(xn, w_ref[:, 2 * d:3 * d])
    bg_ref[...] = bg.astype(BF16)
    v_ref[...] = (cg * hh).astype(BF16)


def _mix_a_in(h, g, w_in):
    n, d = h.shape
    tm = _tile(n, 512)
    return pl.pallas_call(
        _mix_a_in_kernel,
        grid=(n // tm,),
        in_specs=[pl.BlockSpec((tm, d), lambda i: (i, 0)),
                  pl.BlockSpec((1, d), lambda i: (0, 0)),
                  pl.BlockSpec((d, 3 * d), lambda i: (0, 0))],
        out_specs=[pl.BlockSpec((tm, d), lambda i: (i, 0)),
                   pl.BlockSpec((tm, d), lambda i: (i, 0))],
        out_shape=[jax.ShapeDtypeStruct((n, d), BF16), jax.ShapeDtypeStruct((n, d), BF16)],
        compiler_params=_cparams("parallel"),
        name="mix_a_in",
    )(h, g, w_in)


def _mix_a_conv_kernel(v_ref, vp_ref, vn_ref, bg_ref, cw_ref, z_ref, *, tiles_per_seq):
    i = pl.program_id(0)
    tm = v_ref.shape[0]
    v = v_ref[...].astype(F32)
    at_start = (i % tiles_per_seq) == 0
    at_end = (i % tiles_per_seq) == tiles_per_seq - 1
    prev_row = vp_ref[...].astype(F32)[BF16_ROWS - 1:BF16_ROWS, :]
    next_row = vn_ref[...].astype(F32)[0:1, :]
    prev_row = jnp.where(at_start, 0.0, prev_row)
    next_row = jnp.where(at_end, 0.0, next_row)
    row = lax.broadcasted_iota(I32, (tm, 1), 0)
    v_dn = jnp.where(row == 0, prev_row, pltpu.roll(v, 1, 0))
    v_up = jnp.where(row == tm - 1, next_row, pltpu.roll(v, tm - 1, 0))
    cw = cw_ref[...]
    u = cw[0:1, :] * v_dn + cw[1:2, :] * v + cw[2:3, :] * v_up
    z_ref[...] = (bg_ref[...].astype(F32) * u).astype(BF16)


def _mix_a_conv(v, bg, conv_w, seq):
    n, d = v.shape
    tm = _tile(seq, 512)
    hb = tm // BF16_ROWS
    nhalo = n // BF16_ROWS
    return pl.pallas_call(
        functools.partial(_mix_a_conv_kernel, tiles_per_seq=seq // tm),
        grid=(n // tm,),
        in_specs=[pl.BlockSpec((tm, d), lambda i: (i, 0)),
                  pl.BlockSpec((BF16_ROWS, d), lambda i: (jnp.maximum(i * hb - 1, 0), 0)),
                  pl.BlockSpec((BF16_ROWS, d), lambda i: (jnp.minimum((i + 1) * hb, nhalo - 1), 0)),
                  pl.BlockSpec((tm, d), lambda i: (i, 0)),
                  pl.BlockSpec(conv_w.shape, lambda i: (0, 0))],
        out_specs=pl.BlockSpec((tm, d), lambda i: (i, 0)),
        out_shape=jax.ShapeDtypeStruct((n, d), BF16),
        compiler_params=_cparams("parallel"),
        name="mix_a_conv",
    )(v, v, v, bg, conv_w)


def _mix_b_in_kernel(h_ref, hp_ref, hn_ref, g_ref, w_ref, yb_ref, u_ref, *, steps_per_seq, tt):
    i = pl.program_id(0)
    r = yb_ref.shape[1]
    tiles = h_ref.shape[0] // tt
    g = g_ref[...]
    xn = _rms(h_ref[...], g).astype(BF16)
    yb_ref[...] = jax.nn.gelu(_dot(xn, w_ref[:, 0:r])).astype(BF16)
    u = _dot(xn, w_ref[:, r:2 * r])
    xh = _rms(jnp.concatenate([hp_ref[...], hn_ref[...]], axis=0), g).astype(BF16)
    uh = _dot(xh, w_ref[:, r:2 * r])
    at_start = (i % steps_per_seq) == 0
    at_end = (i % steps_per_seq) == steps_per_seq - 1
    prev2 = jnp.where(at_start, 0.0, uh[SUBLANES - 2:SUBLANES, :])
    next1 = jnp.where(at_end, 0.0, uh[SUBLANES:SUBLANES + 1, :])
    row = lax.broadcasted_iota(I32, (SUBLANES, 1), 0)
    for k in range(tiles):
        p2 = prev2 if k == 0 else u[k * tt - 2:k * tt]
        n1 = next1 if k == tiles - 1 else u[(k + 1) * tt:(k + 1) * tt + 1]
        pad = jnp.where(row == 0, p2[0:1], jnp.where(row == 1, p2[1:2], jnp.where(row == 2, n1, 0.0)))
        for s in range(r // LANES):
            lanes = slice(s * LANES, (s + 1) * LANES)
            u_ref[k, s, 0:tt, :] = u[k * tt:(k + 1) * tt, lanes]
            u_ref[k, s, tt:tt + SUBLANES, :] = pad[:, lanes]


def _mix_b_in(h, g, w_in, batch, seq, tt):
    n, d = h.shape
    r = w_in.shape[1] // 2
    tm = _tile(seq, 512)
    steps_per_seq = seq // tm
    tiles = tm // tt
    hb = tm // SUBLANES
    nhalo = n // SUBLANES
    return pl.pallas_call(
        functools.partial(_mix_b_in_kernel, steps_per_seq=steps_per_seq, tt=tt),
        grid=(n // tm,),
        in_specs=[pl.BlockSpec((tm, d), lambda i: (i, 0)),
                  pl.BlockSpec((SUBLANES, d), lambda i: (jnp.maximum(i * hb - 1, 0), 0)),
                  pl.BlockSpec((SUBLANES, d), lambda i: (jnp.minimum((i + 1) * hb, nhalo - 1), 0)),
                  pl.BlockSpec((1, d), lambda i: (0, 0)),
                  pl.BlockSpec((d, 2 * r), lambda i: (0, 0))],
        out_specs=[pl.BlockSpec((tm, r), lambda i: (i, 0)),
                   pl.BlockSpec((tiles, r // LANES, None, tt + SUBLANES, LANES),
                                lambda i: (i % steps_per_seq, 0, i // steps_per_seq, 0, 0))],
        out_shape=[jax.ShapeDtypeStruct((n, r), BF16),
                   jax.ShapeDtypeStruct((seq // tt, r // LANES, batch, tt + SUBLANES, LANES), F32)],
        compiler_params=_cparams("parallel"),
        name="mix_b_in",
    )(h, h, h, g, w_in)


def _rglru_pass_kernel(*refs, reverse, tt):
    if reverse:
        (u_ref, cw_ref, cb_ref, wr_ref, br_ref, wi_ref, bi_ref, lam_ref, hf_ref, yb_ref,
         out_ref, x_s, a_s, g_s, carry_s, nat_s) = refs
    else:
        (u_ref, cw_ref, cb_ref, wr_ref, br_ref, wi_ref, bi_ref, lam_ref,
         out_ref, x_s, a_s, g_s, carry_s) = refs
    nb = SUBLANES
    ttp = tt + SUBLANES
    rows = tt * nb
    slabs = u_ref.shape[0]

    @pl.when(pl.program_id(1) == 0)
    def _():
        carry_s[...] = jnp.zeros_like(carry_s)

    def gather_t(tl, dst):
        for s in range(slabs):
            x_s[pl.ds(dst, nb), s * LANES:(s + 1) * LANES] = u_ref[s, pl.ds(tl, nb, stride=ttp), :]

    def load_t(tl, c):
        gather_t(tl, pl.multiple_of((tl + 2) * nb, nb))
        return c

    lax.fori_loop(0, tt, load_t, 0, unroll=8)
    gather_t(tt, 0)
    gather_t(tt + 1, nb)
    gather_t(tt + 2, (tt + 2) * nb)

    cw = cw_ref[...]
    u = (cw[0:1, :] * x_s[0:rows, :] + cw[1:2, :] * x_s[nb:nb + rows, :]
         + cw[2:3, :] * x_s[2 * nb:2 * nb + rows, :] + cw[3:4, :] * x_s[3 * nb:3 * nb + rows, :]
         + cb_ref[...])
    ub = u.astype(BF16)
    r = 0.5 * (1.0 + jnp.tanh(0.5 * (_dot(ub, wr_ref[...]) + br_ref[...])))
    ig = 0.5 * (1.0 + jnp.tanh(0.5 * (_dot(ub, wi_ref[...]) + bi_ref[...])))
    log_a = -RG_C * r * jax.nn.softplus(-lam_ref[...])
    th = jnp.tanh(0.5 * log_a)
    q = 1.0 / (1.0 - th)
    a_s[...] = (1.0 + th) * q
    m2 = -4.0 * th * q * q
    g_s[...] = jnp.where(m2 > 0.0, m2 * lax.rsqrt(m2), 0.0) * (ig * u)

    def step(k, h):
        tl = tt - 1 - k if reverse else k
        base = pl.multiple_of(tl * nb, nb)
        h = a_s[pl.ds(base, nb), :] * h + g_s[pl.ds(base, nb), :]
        if reverse:
            hs = h + hf_ref[pl.ds(base, nb), :]
            for s in range(slabs):
                nat_s[s, pl.ds(tl, nb, stride=ttp), :] = hs[:, s * LANES:(s + 1) * LANES]
        else:
            out_ref[pl.ds(base, nb), :] = h
        return h

    carry_s[...] = lax.fori_loop(0, tt, step, carry_s[...], unroll=8)
    if reverse:
        for b in range(nb):
            for s in range(slabs):
                lanes = slice(s * LANES, (s + 1) * LANES)
                hb = nat_s[s, b * ttp:b * ttp + tt, :]
                out_ref[b, :, lanes] = (yb_ref[b, :, lanes].astype(F32) * hb).astype(BF16)


def _rglru_pass(u_tiles, conv_w, conv_b, w_r, b_r, w_i, b_i, lam, hf, yb, seq, tt, reverse):
    nt, nslab, batch, ttp, _ = u_tiles.shape
    r = nslab * LANES
    assert batch == SUBLANES, "time-major tile layout puts the batch on the 8 sublanes"
    nh, bk = w_r.shape[1], w_r.shape[2]
    slabs = bk // LANES
    kw = conv_w.shape[0]
    d = 1 if reverse else 0
    rows = tt * batch

    def tile(jj):
        return nt - 1 - jj if reverse else jj

    vec = pl.BlockSpec((None, 1, bk), lambda h, jj: (d, 0, h))
    mat = pl.BlockSpec((None, None, bk, bk), lambda h, jj: (d, h, 0, 0))
    in_specs = [pl.BlockSpec((None, slabs, batch * ttp, LANES), lambda h, jj: (tile(jj), h, 0, 0)),
                pl.BlockSpec((kw, bk), lambda h, jj: (0, h)),
                pl.BlockSpec((1, bk), lambda h, jj: (0, h)),
                mat, vec, mat, vec, vec]
    args = [u_tiles.reshape(nt, nslab, batch * ttp, LANES), conv_w, conv_b, w_r, b_r.reshape(2, 1, r),
            w_i, b_i.reshape(2, 1, r), lam.reshape(2, 1, r)]
    scratch = [pltpu.VMEM(((tt + 3) * batch, bk), F32), pltpu.VMEM((rows, bk), F32),
               pltpu.VMEM((rows, bk), F32), pltpu.VMEM((batch, bk), F32)]
    if reverse:
        in_specs += [pl.BlockSpec((rows, bk), lambda h, jj: (tile(jj), h)),
                     pl.BlockSpec((batch, tt, bk), lambda h, jj: (0, tile(jj), h))]
        args += [hf, yb.reshape(batch, seq, r)]
        scratch += [pltpu.VMEM((slabs, batch * ttp, LANES), F32)]
        out_spec = pl.BlockSpec((batch, tt, bk), lambda h, jj: (0, tile(jj), h))
        out_shape = jax.ShapeDtypeStruct((batch, seq, r), BF16)
    else:
        out_spec = pl.BlockSpec((rows, bk), lambda h, jj: (tile(jj), h))
        out_shape = jax.ShapeDtypeStruct((seq * batch, r), F32)
    return pl.pallas_call(
        functools.partial(_rglru_pass_kernel, reverse=reverse, tt=tt),
        grid=(nh, nt),
        in_specs=in_specs,
        out_specs=out_spec,
        out_shape=out_shape,
        scratch_shapes=scratch,
        compiler_params=_cparams("parallel", "arbitrary"),
        name="rglru_rev" if reverse else "rglru_fwd",
    )(*args)


def _rglru(u_tiles, yb, conv_w, conv_b, w_r, b_r, w_i, b_i, lam, seq, tt):
    hf = _rglru_pass(u_tiles, conv_w, conv_b, w_r, b_r, w_i, b_i, lam, None, None, seq, tt, False)
    z = _rglru_pass(u_tiles, conv_w, conv_b, w_r, b_r, w_i, b_i, lam, hf, yb, seq, tt, True)
    return z.reshape(-1, z.shape[2])


def _post_mix_kernel(z_ref, wo_ref, h_ref, g_ref, wr_ref, br_ref,
                     h1_ref, xn_ref, meta_ref, meta_t_ref, cnt_ref, carry_ref):
    i = pl.program_id(0)
    tm = z_ref.shape[0]

    @pl.when(i == 0)
    def _():
        carry_ref[...] = jnp.zeros_like(carry_ref)

    h1 = h_ref[...] + _dot(z_ref[...], wo_ref[...])
    h1_ref[...] = h1
    xn = _rms(h1, g_ref[...])
    xn_ref[...] = _pack_bf16_pairs(xn)

    x_hi = xn.astype(BF16)
    x_lo = (xn - x_hi.astype(F32)).astype(BF16)
    w = wr_ref[...]
    w_hi = w.astype(BF16)
    w_lo = (w - w_hi.astype(F32)).astype(BF16)
    logits = _dot(x_hi, w_hi) + _dot(x_hi, w_lo) + _dot(x_lo, w_hi) + br_ref[...]

    lane = lax.broadcasted_iota(I32, (tm, LANES), 1).astype(F32)
    neg = jnp.float32(-jnp.inf)
    nolane = jnp.float32(LANES)
    gmask = lane < N_GROUPS
    gmax = jnp.max(jnp.where(gmask, logits, neg), axis=1, keepdims=True)
    eg = jnp.where(gmask, jnp.exp(logits - gmax), 0.0)
    pg = eg / jnp.sum(eg, axis=1, keepdims=True)
    pg_top = jnp.max(pg, axis=1, keepdims=True)
    g_top = jnp.min(jnp.where(gmask & (pg == pg_top), lane, nolane), axis=1, keepdims=True)
    lo = EXPERT_LANE0 + g_top * EXPERTS_PER_GROUP
    emask = (lane >= lo) & (lane < lo + EXPERTS_PER_GROUP)
    emax = jnp.max(jnp.where(emask, logits, neg), axis=1, keepdims=True)
    ee = jnp.where(emask, jnp.exp(logits - emax), 0.0)
    pe = jnp.where(emask, ee / jnp.sum(ee, axis=1, keepdims=True), -1.0)
    p1 = jnp.max(pe, axis=1, keepdims=True)
    i1 = jnp.min(jnp.where(pe == p1, lane, nolane), axis=1, keepdims=True)
    pe2 = jnp.where(lane == i1, -1.0, pe)
    p2 = jnp.max(pe2, axis=1, keepdims=True)
    i2 = jnp.min(jnp.where(pe2 == p2, lane, nolane), axis=1, keepdims=True)
    psum = p1 + p2
    w0 = pg_top * (p1 / psum)
    w1 = pg_top * (p2 / psum)

    oh0 = lane == i1
    oh1 = lane == i2
    both = (oh0 | oh1).astype(BF16)
    ri = lax.broadcasted_iota(I32, (tm, tm), 0)
    ci = lax.broadcasted_iota(I32, (tm, tm), 1)
    before = (ri > ci).astype(BF16)
    cnt_before = _dot(before, both) + carry_ref[...]
    rank0 = jnp.sum(jnp.where(oh0, cnt_before, 0.0), axis=1, keepdims=True)
    rank1 = jnp.sum(jnp.where(oh1, cnt_before, 0.0), axis=1, keepdims=True)
    carry = carry_ref[...] + jnp.sum(both.astype(F32), axis=0, keepdims=True)
    carry_ref[...] = carry
    cnt_ref[...] = carry

    e0 = i1 - EXPERT_LANE0
    e1 = i2 - EXPERT_LANE0
    meta = jnp.zeros((tm, LANES), F32)
    for ln, val in ((M_E0, e0), (M_E1, e1), (M_R0, rank0), (M_R1, rank1), (M_W0, w0), (M_W1, w1)):
        meta = jnp.where(lane == ln, val, meta)
    meta_ref[...] = meta
    meta_t_ref[...] = meta.T[0:SUBLANES, :]


def _post_mix(z, w_out, h, g, w_router, b_router):
    n, d = h.shape
    k = z.shape[1]
    tm = _tile(n, 512)
    return pl.pallas_call(
        _post_mix_kernel,
        grid=(n // tm,),
        in_specs=[pl.BlockSpec((tm, k), lambda i: (i, 0)),
                  pl.BlockSpec((k, d), lambda i: (0, 0)),
                  pl.BlockSpec((tm, d), lambda i: (i, 0)),
                  pl.BlockSpec((1, d), lambda i: (0, 0)),
                  pl.BlockSpec((d, LANES), lambda i: (0, 0)),
                  pl.BlockSpec((1, LANES), lambda i: (0, 0))],
        out_specs=[pl.BlockSpec((tm, d), lambda i: (i, 0)),
                   pl.BlockSpec((tm, d // 2), lambda i: (i, 0)),
                   pl.BlockSpec((tm, LANES), lambda i: (i, 0)),
                   pl.BlockSpec((SUBLANES, tm), lambda i: (0, i)),
                   pl.BlockSpec((1, LANES), lambda i: (0, 0))],
        out_shape=[jax.ShapeDtypeStruct((n, d), F32), jax.ShapeDtypeStruct((n, d // 2), U32),
                   jax.ShapeDtypeStruct((n, LANES), F32), jax.ShapeDtypeStruct((SUBLANES, n), F32),
                   jax.ShapeDtypeStruct((1, LANES), F32)],
        scratch_shapes=[pltpu.VMEM((1, LANES), F32)],
        compiler_params=_cparams("arbitrary"),
        name="post_mix_router",
    )(z, w_out, h, g, w_router, b_router)


def _plan_kernel(cnt_ref, meta_t_ref, dest_ref, estart_ref, ecount_ref, nblk_ref, ztail_ref,
                 pstart_ref):
    def per_expert(e, acc):
        nb_e = (cnt_ref[e] + (ROUTE_ROWS - 1)) // ROUTE_ROWS
        pstart_ref[e] = acc * ROUTE_ROWS
        estart_ref[e] = acc
        ecount_ref[e] = nb_e
        ztail_ref[e] = jnp.where(nb_e > 0, (acc + nb_e - 1) * ROUTE_ROWS, -1)
        return acc + nb_e

    nblk_ref[0] = lax.fori_loop(0, N_EXPERTS, per_expert, 0)

    e0 = meta_t_ref[M_E0:M_E0 + 1, :]
    e1 = meta_t_ref[M_E1:M_E1 + 1, :]
    d0 = meta_t_ref[M_R0:M_R0 + 1, :]
    d1 = meta_t_ref[M_R1:M_R1 + 1, :]
    for e in range(N_EXPERTS):
        ps = pstart_ref[e].astype(F32)
        d0 = d0 + jnp.where(e0 == e, ps, 0.0)
        d1 = d1 + jnp.where(e1 == e, ps, 0.0)
    dest_ref[...] = jnp.zeros_like(dest_ref)
    dest_ref[0:1, :] = d0.astype(I32)
    dest_ref[1:2, :] = d1.astype(I32)


def _plan(cnt_i32, meta_t):
    n = meta_t.shape[1]
    smem = pl.BlockSpec(memory_space=pltpu.SMEM)
    return pl.pallas_call(
        _plan_kernel,
        in_specs=[smem, pl.BlockSpec((SUBLANES, n), lambda: (0, 0))],
        out_specs=[pl.BlockSpec((SUBLANES, n), lambda: (0, 0)), smem, smem, smem, smem],
        out_shape=[jax.ShapeDtypeStruct((SUBLANES, n), I32),
                   jax.ShapeDtypeStruct((N_EXPERTS,), I32),
                   jax.ShapeDtypeStruct((N_EXPERTS,), I32),
                   jax.ShapeDtypeStruct((1,), I32),
                   jax.ShapeDtypeStruct((N_EXPERTS,), I32)],
        scratch_shapes=[pltpu.SMEM((N_EXPERTS,), I32)],
        name="dispatch_plan",
    )(cnt_i32, meta_t)


def _dispatch_kernel(ztail_ref, nblk_ref, d0_ref, d1_ref, x_ref, xs_ref, zero_s, sem):
    i = pl.program_id(0)
    nb_max = xs_ref.shape[0] // ROUTE_ROWS

    def zero_block(start):
        start = pl.multiple_of(start, ROUTE_ROWS)
        return pltpu.make_async_copy(zero_s, xs_ref.at[pl.ds(start, ROUTE_ROWS)], sem.at[2])

    @pl.when(i == 0)
    def _():
        zero_s[...] = jnp.zeros_like(zero_s)

        def start_tail(e, c):
            @pl.when(ztail_ref[e] >= 0)
            def _():
                zero_block(ztail_ref[e]).start()
            return c

        def wait_tail(e, c):
            @pl.when(ztail_ref[e] >= 0)
            def _():
                zero_block(ztail_ref[e]).wait()
            return c

        def start_unused(b, c):
            zero_block(b * ROUTE_ROWS).start()
            return c

        def wait_unused(b, c):
            zero_block(b * ROUTE_ROWS).wait()
            return c

        lax.fori_loop(0, N_EXPERTS, start_tail, 0)
        lax.fori_loop(nblk_ref[0], nb_max, start_unused, 0)
        lax.fori_loop(0, N_EXPERTS, wait_tail, 0)
        lax.fori_loop(nblk_ref[0], nb_max, wait_unused, 0)

    def row_copy(o, j, dref, k):
        dst = dref[0, o * SUBLANES + j]
        return pltpu.make_async_copy(x_ref.at[o, pl.ds(j, 1)], xs_ref.at[pl.ds(dst, 1)], sem.at[k])

    def issue(o, c):
        for j in range(SUBLANES):
            row_copy(o, j, d0_ref, 0).start(priority=0)
            row_copy(o, j, d1_ref, 1).start(priority=1)
        return c

    def drain(o, c):
        for j in range(SUBLANES):
            row_copy(o, j, d0_ref, 0).wait()
            row_copy(o, j, d1_ref, 1).wait()
        return c

    lax.fori_loop(0, x_ref.shape[0], issue, 0)
    lax.fori_loop(0, x_ref.shape[0], drain, 0)


def _dispatch(ztail, nblk, dest, xn, nb_max):
    n, d = xn.shape
    tmd = _tile(n, 256)
    nt = n // tmd
    d0 = dest[0].reshape(nt, 1, tmd)
    d1 = dest[1].reshape(nt, 1, tmd)
    idx_spec = pl.BlockSpec((None, 1, tmd), lambda i: (i, 0, 0), memory_space=pltpu.SMEM)
    smem = pl.BlockSpec(memory_space=pltpu.SMEM)
    return pl.pallas_call(
        _dispatch_kernel,
        grid=(nt,),
        in_specs=[smem, smem, idx_spec, idx_spec,
                  pl.BlockSpec((tmd // SUBLANES, SUBLANES, d), lambda i: (i, 0, 0))],
        out_specs=pl.BlockSpec(memory_space=pl.ANY),
        out_shape=jax.ShapeDtypeStruct((nb_max * ROUTE_ROWS, d), xn.dtype),
        scratch_shapes=[pltpu.VMEM((ROUTE_ROWS, d), xn.dtype), pltpu.SemaphoreType.DMA((3,))],
        compiler_params=_cparams("arbitrary"),
        name="dispatch_scatter",
    )(ztail, nblk, d0, d1, xn.reshape(n // SUBLANES, SUBLANES, d))


def _expert_kernel(es_ref, ec_ref, nblk_ref, xs_ref, wg_ref, wu_ref, wd_ref, ys_ref,
                   x_s, y_s, wg_s, wu_s, wd_s, sem):
    e = pl.program_id(0)
    nb = ec_ref[e]
    b0 = es_ref[e]
    k = x_s.shape[2]
    nb_max = ys_ref.shape[0] // ROUTE_ROWS

    def rows(b):
        return pl.ds(pl.multiple_of(b * ROUTE_ROWS, ROUTE_ROWS), ROUTE_ROWS)

    def copy_in(j, slot):
        return pltpu.make_async_copy(xs_ref.at[rows(b0 + j)], x_s.at[slot], sem.at[0, slot])

    def copy_out(j, slot):
        return pltpu.make_async_copy(y_s.at[slot], ys_ref.at[rows(b0 + j)], sem.at[1, slot])

    @pl.when(nb > 0)
    def _():
        copy_in(0, 0).start()
        wg_s[...] = wg_ref[...].astype(BF16)
        wu_s[...] = wu_ref[...].astype(BF16)
        wd_s[...] = wd_ref[...].astype(BF16)

        def block(j, c):
            slot = j % 2
            copy_in(j, slot).wait()

            @pl.when(j + 1 < nb)
            def _():
                copy_in(j + 1, 1 - slot).start()

            x_hi, x_lo = _unpack_bf16_pairs(x_s[slot])
            x_hi = x_hi.astype(BF16)
            x_lo = x_lo.astype(BF16)
            g = _dot(x_hi, wg_s[0:k, :]) + _dot(x_lo, wg_s[k:2 * k, :])
            u = _dot(x_hi, wu_s[0:k, :]) + _dot(x_lo, wu_s[k:2 * k, :])
            hb = (jax.nn.silu(g) * u).astype(BF16)
            y = _pack_bf16_pairs(_dot(hb, wd_s[...]))

            @pl.when(j >= 2)
            def _():
                copy_out(j - 2, slot).wait()

            y_s[slot] = y
            copy_out(j, slot).start()
            return c

        lax.fori_loop(0, nb, block, 0)

        @pl.when(nb >= 2)
        def _():
            copy_out(nb - 2, nb % 2).wait()

        copy_out(nb - 1, (nb - 1) % 2).wait()

    @pl.when(e == pl.num_programs(0) - 1)
    def _():
        y_s[0] = jnp.zeros_like(y_s[0])

        def zero(b):
            return pltpu.make_async_copy(y_s.at[0], ys_ref.at[rows(b)], sem.at[1, 0])

        def start_zero(b, c):
            zero(b).start()
            return c

        def wait_zero(b, c):
            zero(b).wait()
            return c

        lax.fori_loop(nblk_ref[0], nb_max, start_zero, 0)
        lax.fori_loop(nblk_ref[0], nb_max, wait_zero, 0)


def _experts(estart, ecount, nblk, xs, w_gate, w_up, w_down, layer):
    p, k = xs.shape
    d, f = w_gate.shape[2], w_gate.shape[3]

    def w_map(e, es, ec, nb):
        return (layer, e, 0, 0)

    grid_spec = pltpu.PrefetchScalarGridSpec(
        num_scalar_prefetch=3,
        grid=(N_EXPERTS,),
        in_specs=[pl.BlockSpec(memory_space=pl.ANY),
                  pl.BlockSpec((None, None, d, f), w_map),
                  pl.BlockSpec((None, None, d, f), w_map),
                  pl.BlockSpec((None, None, f, d), w_map)],
        out_specs=pl.BlockSpec(memory_space=pl.ANY),
        scratch_shapes=[pltpu.VMEM((2, ROUTE_ROWS, k), U32), pltpu.VMEM((2, ROUTE_ROWS, k), U32),
                        pltpu.VMEM((d, f), BF16), pltpu.VMEM((d, f), BF16), pltpu.VMEM((f, d), BF16),
                        pltpu.SemaphoreType.DMA((2, 2))],
    )
    return pl.pallas_call(
        _expert_kernel,
        grid_spec=grid_spec,
        out_shape=jax.ShapeDtypeStruct((p, k), U32),
        compiler_params=_cparams("arbitrary"),
        name="expert_mlp",
    )(estart, ecount, nblk, xs, w_gate, w_up, w_down)


def _combine_kernel(d0_ref, d1_ref, d0n_ref, d1n_ref, ys_ref, h1_ref, meta_ref, p_ref, gp_ref,
                    wpg_ref, bpg_ref, wp_ref, gf_ref, out_ref, ya_s, yb_s, sem, *, final):
    i = pl.program_id(0)
    nt = pl.num_programs(0)
    tmc = h1_ref.shape[0]
    slot = i % 2

    def row_copy(o, j, dref, dst, s, k):
        src = dref[0, o * SUBLANES + j]
        return pltpu.make_async_copy(ys_ref.at[pl.ds(src, 1)], dst.at[s, o, pl.ds(j, 1)], sem.at[s, k])

    def gather(d0r, d1r, s, wait):
        def body(o, c):
            for j in range(SUBLANES):
                ca = row_copy(o, j, d0r, ya_s, s, 0)
                cb = row_copy(o, j, d1r, yb_s, s, 1)
                if wait:
                    ca.wait()
                    cb.wait()
                else:
                    ca.start(priority=0)
                    cb.start(priority=1)
            return c

        lax.fori_loop(0, tmc // SUBLANES, body, 0)

    @pl.when(i == 0)
    def _():
        gather(d0_ref, d1_ref, slot, False)

    @pl.when(i + 1 < nt)
    def _():
        gather(d0n_ref, d1n_ref, 1 - slot, False)

    gather(d0_ref, d1_ref, slot, True)

    meta = meta_ref[...]
    w0 = meta[:, M_W0:M_W0 + 1]
    w1 = meta[:, M_W1:M_W1 + 1]
    ya_hi, ya_lo = _unpack_bf16_pairs(ya_s[slot].reshape(tmc, -1))
    yb_hi, yb_lo = _unpack_bf16_pairs(yb_s[slot].reshape(tmc, -1))
    y = jnp.concatenate([ya_hi * w0 + yb_hi * w1, ya_lo * w0 + yb_lo * w1], axis=1)
    h2 = h1_ref[...] + y
    xn = _rms(h2, gp_ref[...]).astype(BF16)
    gate = jax.nn.sigmoid(_dot(xn, wpg_ref[...]) + bpg_ref[...])
    pp = _dot(p_ref[...].astype(BF16), wp_ref[...])
    h3 = h2 + gate * pp
    if final:
        h3 = _rms(h3, gf_ref[...])
    out_ref[...] = h3


def _combine(dest, ys, h1, meta, p, layer, g_ple, w_ple_gate, b_ple_gate, w_ple, g_final, final):
    n, d = h1.shape
    pd = p.shape[2]
    tmc = _tile(n, 256)
    nt = n // tmc
    d0 = dest[0].reshape(nt, 1, tmc)
    d1 = dest[1].reshape(nt, 1, tmc)
    idx_spec = pl.BlockSpec((None, 1, tmc), lambda i: (i, 0, 0), memory_space=pltpu.SMEM)
    nxt_spec = pl.BlockSpec((None, 1, tmc), lambda i: (jnp.minimum(i + 1, nt - 1), 0, 0),
                            memory_space=pltpu.SMEM)
    vec = pl.BlockSpec((1, d), lambda i: (0, 0))
    k = ys.shape[1]
    return pl.pallas_call(
        functools.partial(_combine_kernel, final=final),
        grid=(nt,),
        in_specs=[idx_spec, idx_spec, nxt_spec, nxt_spec,
                  pl.BlockSpec(memory_space=pl.ANY),
                  pl.BlockSpec((tmc, d), lambda i: (i, 0)),
                  pl.BlockSpec((tmc, LANES), lambda i: (i, 0)),
                  pl.BlockSpec((None, tmc, pd), lambda i: (layer, i, 0)),
                  vec,
                  pl.BlockSpec((d, d), lambda i: (0, 0)),
                  vec,
                  pl.BlockSpec((pd, d), lambda i: (0, 0)),
                  vec],
        out_specs=pl.BlockSpec((tmc, d), lambda i: (i, 0)),
        out_shape=jax.ShapeDtypeStruct((n, d), F32),
        scratch_shapes=[pltpu.VMEM((2, tmc // SUBLANES, SUBLANES, k), U32),
                        pltpu.VMEM((2, tmc // SUBLANES, SUBLANES, k), U32),
                        pltpu.SemaphoreType.DMA((2, 2))],
        compiler_params=_cparams("arbitrary"),
        name="combine_ple",
    )(d0, d1, d0, d1, ys, h1, meta, p, g_ple, w_ple_gate, b_ple_gate, w_ple, g_final)


def _row(v):
    return v.reshape(1, -1)


def _moe_and_ple(z, w_out, h, i, final, norm_ffn, w_router_group, b_router_group, w_router_expert,
                 b_router_expert, w_exp_gate, w_exp_up, w_exp_down, norm_ple, w_ple, w_ple_gate,
                 b_ple_gate, norm_final, p):
    n, d = h.shape
    pad = LANES - N_GROUPS - N_EXPERTS
    w_router = jnp.concatenate(
        [w_router_group[i], w_router_expert[i], jnp.zeros((d, pad), F32)], axis=1)
    b_router = jnp.concatenate(
        [b_router_group[i], b_router_expert[i], jnp.zeros((pad,), F32)]).reshape(1, LANES)
    h1, xn, meta, meta_t, cnt = _post_mix(z, w_out, h, _row(norm_ffn[i]), w_router, b_router)
    cnt_i32 = cnt[0, EXPERT_LANE0:EXPERT_LANE0 + N_EXPERTS].astype(I32)
    nb_max = (2 * n) // ROUTE_ROWS + N_EXPERTS
    dest, estart, ecount, nblk, ztail = _plan(cnt_i32, meta_t)
    xs = _dispatch(ztail, nblk, dest, xn, nb_max)
    ys = _experts(estart, ecount, nblk, xs, w_exp_gate, w_exp_up, w_exp_down, i)
    return _combine(dest, ys, h1, meta, p.reshape(p.shape[0], n, -1), i, _row(norm_ple[i]),
                    w_ple_gate[i].astype(BF16), _row(b_ple_gate[i]), w_ple[i].astype(BF16),
                    _row(norm_final), final)


def kernel(x, p, norm_mix, w_in_a, conv_a, w_out_a, w_in_b, conv_b, conv_bias_b, w_rgate_b, b_rgate_b, w_igate_b, b_igate_b, lam_b, w_out_b, norm_ffn, w_router_group, b_router_group, w_router_expert, b_router_expert, w_exp_gate, w_exp_up, w_exp_down, norm_ple, w_ple, w_ple_gate, b_ple_gate, norm_final):
    batch, seq, d = x.shape
    depth = p.shape[0]
    n = batch * seq
    h = x.reshape(n, d)
    for i in range(depth):
        j = i // 2
        if i % 2 == 0:
            bg, v = _mix_a_in(h, _row(norm_mix[i]), w_in_a[j].astype(BF16))
            z = _mix_a_conv(v, bg, conv_a[j], seq)
            w_out = w_out_a[j]
        else:
            tt = _tile(seq, 256)
            yb, u_tiles = _mix_b_in(h, _row(norm_mix[i]), w_in_b[j].astype(BF16), batch, seq, tt)
            z = _rglru(u_tiles, yb, conv_b[j], _row(conv_bias_b[j]),
                       w_rgate_b[j].astype(BF16), b_rgate_b[j], w_igate_b[j].astype(BF16),
                       b_igate_b[j], lam_b[j], seq, tt)
            w_out = w_out_b[j]
        h = _moe_and_ple(z, w_out.astype(BF16), h, i, i == depth - 1, norm_ffn, w_router_group,
                         b_router_group, w_router_expert, b_router_expert, w_exp_gate, w_exp_up,
                         w_exp_down, norm_ple, w_ple, w_ple_gate, b_ple_gate, norm_final, p)
    return h.reshape(batch, seq, d)
```

```python
import functools

import jax
import jax.numpy as jnp
from jax import lax
from jax.experimental import pallas as pl
from jax.experimental.pallas import tpu as pltpu

F32 = jnp.float32
BF16 = jnp.bfloat16
I32 = jnp.int32
U32 = jnp.uint32

EPS = 1e-6
N_GROUPS = 4
EXPERTS_PER_GROUP = 8
N_EXPERTS = N_GROUPS * EXPERTS_PER_GROUP
RG_C = 8.0

LANES = 128
SUBLANES = 8
BF16_ROWS = 16
EXPERT_LANE0 = N_GROUPS
ROUTE_ROWS = 256
IN_AHEAD = 3
IN_SLOTS = IN_AHEAD + 1
OUT_SLOTS = 2
M_E0, M_E1, M_R0, M_R1, M_W0, M_W1 = 0, 1, 2, 3, 4, 5
VMEM_LIMIT = 48 * 1024 * 1024


def _cparams(*sem):
    return pltpu.CompilerParams(dimension_semantics=sem, vmem_limit_bytes=VMEM_LIMIT)


def _rms(x, g):
    return x * lax.rsqrt(jnp.mean(x * x, axis=-1, keepdims=True) + EPS) * g


def _dot(a, b):
    return jnp.dot(a, b, preferred_element_type=F32)


def _pack_bf16_pairs(x):
    k = x.shape[1] // 2
    bits = lax.bitcast_convert_type(x, U32)
    bits = bits + (jnp.uint32(0x7FFF) + ((bits >> 16) & jnp.uint32(1)))
    return (bits[:, :k] & jnp.uint32(0xFFFF0000)) | (bits[:, k:] >> 16)


def _unpack_bf16_pairs(p):
    hi = lax.bitcast_convert_type(p & jnp.uint32(0xFFFF0000), F32)
    lo = lax.bitcast_convert_type(p << 16, F32)
    return hi, lo


def _tile(n, want):
    t = min(n, want)
    assert n % t == 0, (n, want)
    return t


def _mix_a_in_kernel(h_ref, g_ref, w_ref, bg_ref, v_ref):
    d = h_ref.shape[1]
    xn = _rms(h_ref[...], g_ref[...]).astype(BF16)
    bg = _dot(xn, w_ref[:, 0:d])
    cg = _dot(xn, w_ref[:, d:2 * d])
    hh = _dot(xn, w_ref[:, 2 * d:3 * d])
    bg_ref[...] = bg.astype(BF16)
    v_ref[...] = (cg * hh).astype(BF16)


def _mix_a_in(h, g, w_in):
    n, d = h.shape
    tm = _tile(n, 512)
    return pl.pallas_call(
        _mix_a_in_kernel,
        grid=(n // tm,),
        in_specs=[pl.BlockSpec((tm, d), lambda i: (i, 0)),
                  pl.BlockSpec((1, d), lambda i: (0, 0)),
                  pl.BlockSpec((d, 3 * d), lambda i: (0, 0))],
        out_specs=[pl.BlockSpec((tm, d), lambda i: (i, 0)),
                   pl.BlockSpec((tm, d), lambda i: (i, 0))],
        out_shape=[jax.ShapeDtypeStruct((n, d), BF16), jax.ShapeDtypeStruct((n, d), BF16)],
        compiler_params=_cparams("parallel"),
        name="mix_a_in",
    )(h, g, w_in)


def _mix_a_conv_kernel(v_ref, vp_ref, vn_ref, bg_ref, cw_ref, z_ref, *, tiles_per_seq):
    i = pl.program_id(0)
    tm = v_ref.shape[0]
    v = v_ref[...].astype(F32)
    at_start = (i % tiles_per_seq) == 0
    at_end = (i % tiles_per_seq) == tiles_per_seq - 1
    prev_row = vp_ref[...].astype(F32)[BF16_ROWS - 1:BF16_ROWS, :]
    next_row = vn_ref[...].astype(F32)[0:1, :]
    prev_row = jnp.where(at_start, 0.0, prev_row)
    next_row = jnp.where(at_end, 0.0, next_row)
    row = lax.broadcasted_iota(I32, (tm, 1), 0)
    v_dn = jnp.where(row == 0, prev_row, pltpu.roll(v, 1, 0))
    v_up = jnp.where(row == tm - 1, next_row, pltpu.roll(v, tm - 1, 0))
    cw = cw_ref[...]
    u = cw[0:1, :] * v_dn + cw[1:2, :] * v + cw[2:3, :] * v_up
    z_ref[...] = (bg_ref[...].astype(F32) * u).astype(BF16)


def _mix_a_conv(v, bg, conv_w, seq):
    n, d = v.shape
    tm = _tile(seq, 512)
    hb = tm // BF16_ROWS
    nhalo = n // BF16_ROWS
    return pl.pallas_call(
        functools.partial(_mix_a_conv_kernel, tiles_per_seq=seq // tm),
        grid=(n // tm,),
        in_specs=[pl.BlockSpec((tm, d), lambda i: (i, 0)),
                  pl.BlockSpec((BF16_ROWS, d), lambda i: (jnp.maximum(i * hb - 1, 0), 0)),
                  pl.BlockSpec((BF16_ROWS, d), lambda i: (jnp.minimum((i + 1) * hb, nhalo - 1), 0)),
                  pl.BlockSpec((tm, d), lambda i: (i, 0)),
                  pl.BlockSpec(conv_w.shape, lambda i: (0, 0))],
        out_specs=pl.BlockSpec((tm, d), lambda i: (i, 0)),
        out_shape=jax.ShapeDtypeStruct((n, d), BF16),
        compiler_params=_cparams("parallel"),
        name="mix_a_conv",
    )(v, v, v, bg, conv_w)


def _mix_b_in_kernel(h_ref, hp_ref, hn_ref, g_ref, w_ref, yb_ref, u_ref, *, steps_per_seq, tt):
    i = pl.program_id(0)
    r = yb_ref.shape[1]
    tiles = h_ref.shape[0] // tt
    g = g_ref[...]
    xn = _rms(h_ref[...], g).astype(BF16)
    yb_ref[...] = jax.nn.gelu(_dot(xn, w_ref[:, 0:r])).astype(BF16)
    u = _dot(xn, w_ref[:, r:2 * r])
    xh = _rms(jnp.concatenate([hp_ref[...], hn_ref[...]], axis=0), g).astype(BF16)
    uh = _dot(xh, w_ref[:, r:2 * r])
    at_start = (i % steps_per_seq) == 0
    at_end = (i % steps_per_seq) == steps_per_seq - 1
    prev2 = jnp.where(at_start, 0.0, uh[SUBLANES - 2:SUBLANES, :])
    next1 = jnp.where(at_end, 0.0, uh[SUBLANES:SUBLANES + 1, :])
    row = lax.broadcasted_iota(I32, (SUBLANES, 1), 0)
    for k in range(tiles):
        p2 = prev2 if k == 0 else u[k * tt - 2:k * tt]
        n1 = next1 if k == tiles - 1 else u[(k + 1) * tt:(k + 1) * tt + 1]
        pad = jnp.where(row == 0, p2[0:1], jnp.where(row == 1, p2[1:2], jnp.where(row == 2, n1, 0.0)))
        for s in range(r // LANES):
            lanes = slice(s * LANES, (s + 1) * LANES)
            u_ref[k, s, 0:tt, :] = u[k * tt:(k + 1) * tt, lanes]
            u_ref[k, s, tt:tt + SUBLANES, :] = pad[:, lanes]


def _mix_b_in(h, g, w_in, batch, seq, tt):
    n, d = h.shape
    r = w_in.shape[1] // 2
    tm = _tile(seq, 512)
    steps_per_seq = seq // tm
    tiles = tm // tt
    hb = tm // SUBLANES
    nhalo = n // SUBLANES
    return pl.pallas_call(
        functools.partial(_mix_b_in_kernel, steps_per_seq=steps_per_seq, tt=tt),
        grid=(n // tm,),
        in_specs=[pl.BlockSpec((tm, d), lambda i: (i, 0)),
                  pl.BlockSpec((SUBLANES, d), lambda i: (jnp.maximum(i * hb - 1, 0), 0)),
                  pl.BlockSpec((SUBLANES, d), lambda i: (jnp.minimum((i + 1) * hb, nhalo - 1), 0)),
                  pl.BlockSpec((1, d), lambda i: (0, 0)),
                  pl.BlockSpec((d, 2 * r), lambda i: (0, 0))],
        out_specs=[pl.BlockSpec((tm, r), lambda i: (i, 0)),
                   pl.BlockSpec((tiles, r // LANES, None, tt + SUBLANES, LANES),
                                lambda i: (i % steps_per_seq, 0, i // steps_per_seq, 0, 0))],
        out_shape=[jax.ShapeDtypeStruct((n, r), BF16),
                   jax.ShapeDtypeStruct((seq // tt, r // LANES, batch, tt + SUBLANES, LANES), F32)],
        compiler_params=_cparams("parallel"),
        name="mix_b_in",
    )(h, h, h, g, w_in)


def _rglru_pass_kernel(*refs, reverse, tt):
    if reverse:
        (u_ref, cw_ref, cb_ref, wr_ref, br_ref, wi_ref, bi_ref, lam_ref, hf_ref, yb_ref,
         out_ref, x_s, a_s, g_s, carry_s, nat_s) = refs
    else:
        (u_ref, cw_ref, cb_ref, wr_ref, br_ref, wi_ref, bi_ref, lam_ref,
         out_ref, x_s, a_s, g_s, carry_s) = refs
    nb = SUBLANES
    ttp = tt + SUBLANES
    rows = tt * nb
    slabs = u_ref.shape[0]

    @pl.when(pl.program_id(1) == 0)
    def _():
        carry_s[...] = jnp.zeros_like(carry_s)

    def gather_t(tl, dst):
        for s in range(slabs):
            x_s[pl.ds(dst, nb), s * LANES:(s + 1) * LANES] = u_ref[s, pl.ds(tl, nb, stride=ttp), :]

    def load_t(tl, c):
        gather_t(tl, pl.multiple_of((tl + 2) * nb, nb))
        return c

    lax.fori_loop(0, tt, load_t, 0, unroll=8)
    gather_t(tt, 0)
    gather_t(tt + 1, nb)
    gather_t(tt + 2, (tt + 2) * nb)

    cw = cw_ref[...]
    u = (cw[0:1, :] * x_s[0:rows, :] + cw[1:2, :] * x_s[nb:nb + rows, :]
         + cw[2:3, :] * x_s[2 * nb:2 * nb + rows, :] + cw[3:4, :] * x_s[3 * nb:3 * nb + rows, :]
         + cb_ref[...])
    ub = u.astype(BF16)
    r = 0.5 * (1.0 + jnp.tanh(0.5 * (_dot(ub, wr_ref[...]) + br_ref[...])))
    ig = 0.5 * (1.0 + jnp.tanh(0.5 * (_dot(ub, wi_ref[...]) + bi_ref[...])))
    log_a = -RG_C * r * jax.nn.softplus(-lam_ref[...])
    th = jnp.tanh(0.5 * log_a)
    q = 1.0 / (1.0 - th)
    a_s[...] = (1.0 + th) * q
    m2 = -4.0 * th * q * q
    g_s[...] = jnp.where(m2 > 0.0, m2 * lax.rsqrt(m2), 0.0) * (ig * u)

    def step(k, h):
        tl = tt - 1 - k if reverse else k
        base = pl.multiple_of(tl * nb, nb)
        h = a_s[pl.ds(base, nb), :] * h + g_s[pl.ds(base, nb), :]
        if reverse:
            hs = h + hf_ref[pl.ds(base, nb), :]
            for s in range(slabs):
                nat_s[s, pl.ds(tl, nb, stride=ttp), :] = hs[:, s * LANES:(s + 1) * LANES]
        else:
            out_ref[pl.ds(base, nb), :] = h
        return h

    carry_s[...] = lax.fori_loop(0, tt, step, carry_s[...], unroll=8)
    if reverse:
        for b in range(nb):
            for s in range(slabs):
                lanes = slice(s * LANES, (s + 1) * LANES)
                hb = nat_s[s, b * ttp:b * ttp + tt, :]
                out_ref[b, :, lanes] = (yb_ref[b, :, lanes].astype(F32) * hb).astype(BF16)


def _rglru_pass(u_tiles, conv_w, conv_b, w_r, b_r, w_i, b_i, lam, hf, yb, seq, tt, reverse):
    nt, nslab, batch, ttp, _ = u_tiles.shape
    r = nslab * LANES
    assert batch == SUBLANES, "time-major tile layout puts the batch on the 8 sublanes"
    nh, bk = w_r.shape[1], w_r.shape[2]
    slabs = bk // LANES
    kw = conv_w.shape[0]
    d = 1 if reverse else 0
    rows = tt * batch

    def tile(jj):
        return nt - 1 - jj if reverse else jj

    vec = pl.BlockSpec((None, 1, bk), lambda h, jj: (d, 0, h))
    mat = pl.BlockSpec((None, None, bk, bk), lambda h, jj: (d, h, 0, 0))
    in_specs = [pl.BlockSpec((None, slabs, batch * ttp, LANES), lambda h, jj: (tile(jj), h, 0, 0)),
                pl.BlockSpec((kw, bk), lambda h, jj: (0, h)),
                pl.BlockSpec((1, bk), lambda h, jj: (0, h)),
                mat, vec, mat, vec, vec]
    args = [u_tiles.reshape(nt, nslab, batch * ttp, LANES), conv_w, conv_b, w_r, b_r.reshape(2, 1, r),
            w_i, b_i.reshape(2, 1, r), lam.reshape(2, 1, r)]
    scratch = [pltpu.VMEM(((tt + 3) * batch, bk), F32), pltpu.VMEM((rows, bk), F32),
               pltpu.VMEM((rows, bk), F32), pltpu.VMEM((batch, bk), F32)]
    if reverse:
        in_specs += [pl.BlockSpec((rows, bk), lambda h, jj: (tile(jj), h)),
                     pl.BlockSpec((batch, tt, bk), lambda h, jj: (0, tile(jj), h))]
        args += [hf, yb.reshape(batch, seq, r)]
        scratch += [pltpu.VMEM((slabs, batch * ttp, LANES), F32)]
        out_spec = pl.BlockSpec((batch, tt, bk), lambda h, jj: (0, tile(jj), h))
        out_shape = jax.ShapeDtypeStruct((batch, seq, r), BF16)
    else:
        out_spec = pl.BlockSpec((rows, bk), lambda h, jj: (tile(jj), h))
        out_shape = jax.ShapeDtypeStruct((seq * batch, r), F32)
    return pl.pallas_call(
        functools.partial(_rglru_pass_kernel, reverse=reverse, tt=tt),
        grid=(nh, nt),
        in_specs=in_specs,
        out_specs=out_spec,
        out_shape=out_shape,
        scratch_shapes=scratch,
        compiler_params=_cparams("parallel", "arbitrary"),
        name="rglru_rev" if reverse else "rglru_fwd",
    )(*args)


def _rglru(u_tiles, yb, conv_w, conv_b, w_r, b_r, w_i, b_i, lam, seq, tt):
    hf = _rglru_pass(u_tiles, conv_w, conv_b, w_r, b_r, w_i, b_i, lam, None, None, seq, tt, False)
    z = _rglru_pass(u_tiles, conv_w, conv_b, w_r, b_r, w_i, b_i, lam, hf, yb, seq, tt, True)
    return z.reshape(-1, z.shape[2])


def _post_mix_kernel(z_ref, wo_ref, h_ref, g_ref, wr_ref, br_ref,
                     h1_ref, xn_ref, meta_ref, meta_t_ref, cnt_ref, carry_ref):
    i = pl.program_id(0)
    tm = z_ref.shape[0]

    @pl.when(i == 0)
    def _():
        carry_ref[...] = jnp.zeros_like(carry_ref)

    h1 = h_ref[...] + _dot(z_ref[...], wo_ref[...])
    h1_ref[...] = h1
    xn = _rms(h1, g_ref[...])
    xn_ref[...] = _pack_bf16_pairs(xn)

    x_hi = xn.astype(BF16)
    x_lo = (xn - x_hi.astype(F32)).astype(BF16)
    w = wr_ref[...]
    w_hi = w.astype(BF16)
    w_lo = (w - w_hi.astype(F32)).astype(BF16)
    logits = _dot(x_hi, w_hi) + _dot(x_hi, w_lo) + _dot(x_lo, w_hi) + br_ref[...]

    lane = lax.broadcasted_iota(I32, (tm, LANES), 1).astype(F32)
    neg = jnp.float32(-jnp.inf)
    nolane = jnp.float32(LANES)
    gmask = lane < N_GROUPS
    gmax = jnp.max(jnp.where(gmask, logits, neg), axis=1, keepdims=True)
    eg = jnp.where(gmask, jnp.exp(logits - gmax), 0.0)
    pg = eg / jnp.sum(eg, axis=1, keepdims=True)
    pg_top = jnp.max(pg, axis=1, keepdims=True)
    g_top = jnp.min(jnp.where(gmask & (pg == pg_top), lane, nolane), axis=1, keepdims=True)
    lo = EXPERT_LANE0 + g_top * EXPERTS_PER_GROUP
    emask = (lane >= lo) & (lane < lo + EXPERTS_PER_GROUP)
    emax = jnp.max(jnp.where(emask, logits, neg), axis=1, keepdims=True)
    ee = jnp.where(emask, jnp.exp(logits - emax), 0.0)
    pe = jnp.where(emask, ee / jnp.sum(ee, axis=1, keepdims=True), -1.0)
    p1 = jnp.max(pe, axis=1, keepdims=True)
    i1 = jnp.min(jnp.where(pe == p1, lane, nolane), axis=1, keepdims=True)
    pe2 = jnp.where(lane == i1, -1.0, pe)
    p2 = jnp.max(pe2, axis=1, keepdims=True)
    i2 = jnp.min(jnp.where(pe2 == p2, lane, nolane), axis=1, keepdims=True)
    psum = p1 + p2
    w0 = pg_top * (p1 / psum)
    w1 = pg_top * (p2 / psum)

    oh0 = lane == i1
    oh1 = lane == i2
    both = (oh0 | oh1).astype(BF16)
    ri = lax.broadcasted_iota(I32, (tm, tm), 0)
    ci = lax.broadcasted_iota(I32, (tm, tm), 1)
    before = (ri > ci).astype(BF16)
    cnt_before = _dot(before, both) + carry_ref[...]
    rank0 = jnp.sum(jnp.where(oh0, cnt_before, 0.0), axis=1, keepdims=True)
    rank1 = jnp.sum(jnp.where(oh1, cnt_before, 0.0), axis=1, keepdims=True)
    carry = carry_ref[...] + jnp.sum(both.astype(F32), axis=0, keepdims=True)
    carry_ref[...] = carry
    cnt_ref[...] = carry

    e0 = i1 - EXPERT_LANE0
    e1 = i2 - EXPERT_LANE0
    meta = jnp.zeros((tm, LANES), F32)
    for ln, val in ((M_E0, e0), (M_E1, e1), (M_R0, rank0), (M_R1, rank1), (M_W0, w0), (M_W1, w1)):
        meta = jnp.where(lane == ln, val, meta)
    meta_ref[...] = meta
    meta_t_ref[...] = meta.T[0:SUBLANES, :]


def _post_mix(z, w_out, h, g, w_router, b_router):
    n, d = h.shape
    k = z.shape[1]
    tm = _tile(n, 512)
    return pl.pallas_call(
        _post_mix_kernel,
        grid=(n // tm,),
        in_specs=[pl.BlockSpec((tm, k), lambda i: (i, 0)),
                  pl.BlockSpec((k, d), lambda i: (0, 0)),
                  pl.BlockSpec((tm, d), lambda i: (i, 0)),
                  pl.BlockSpec((1, d), lambda i: (0, 0)),
                  pl.BlockSpec((d, LANES), lambda i: (0, 0)),
                  pl.BlockSpec((1, LANES), lambda i: (0, 0))],
        out_specs=[pl.BlockSpec((tm, d), lambda i: (i, 0)),
                   pl.BlockSpec((tm, d // 2), lambda i: (i, 0)),
                   pl.BlockSpec((tm, LANES), lambda i: (i, 0)),
                   pl.BlockSpec((SUBLANES, tm), lambda i: (0, i)),
                   pl.BlockSpec((1, LANES), lambda i: (0, 0))],
        out_shape=[jax.ShapeDtypeStruct((n, d), F32), jax.ShapeDtypeStruct((n, d // 2), U32),
                   jax.ShapeDtypeStruct((n, LANES), F32), jax.ShapeDtypeStruct((SUBLANES, n), F32),
                   jax.ShapeDtypeStruct((1, LANES), F32)],
        scratch_shapes=[pltpu.VMEM((1, LANES), F32)],
        compiler_params=_cparams("arbitrary"),
        name="post_mix_router",
    )(z, w_out, h, g, w_router, b_router)


def _plan_kernel(cnt_ref, meta_t_ref, dest_ref, estart_ref, ecount_ref, nblk_ref, ztail_ref,
                 pstart_ref):
    def per_expert(e, acc):
        nb_e = (cnt_ref[e] + (ROUTE_ROWS - 1)) // ROUTE_ROWS
        pstart_ref[e] = acc * ROUTE_ROWS
        estart_ref[e] = acc
        ecount_ref[e] = nb_e
        ztail_ref[e] = jnp.where(nb_e > 0, (acc + nb_e - 1) * ROUTE_ROWS, -1)
        return acc + nb_e

    nblk_ref[0] = lax.fori_loop(0, N_EXPERTS, per_expert, 0)

    e0 = meta_t_ref[M_E0:M_E0 + 1, :]
    e1 = meta_t_ref[M_E1:M_E1 + 1, :]
    d0 = meta_t_ref[M_R0:M_R0 + 1, :]
    d1 = meta_t_ref[M_R1:M_R1 + 1, :]
    for e in range(N_EXPERTS):
        ps = pstart_ref[e].astype(F32)
        d0 = d0 + jnp.where(e0 == e, ps, 0.0)
        d1 = d1 + jnp.where(e1 == e, ps, 0.0)
    dest_ref[...] = jnp.zeros_like(dest_ref)
    dest_ref[0:1, :] = d0.astype(I32)
    dest_ref[1:2, :] = d1.astype(I32)


def _plan(cnt_i32, meta_t):
    n = meta_t.shape[1]
    smem = pl.BlockSpec(memory_space=pltpu.SMEM)
    return pl.pallas_call(
        _plan_kernel,
        in_specs=[smem, pl.BlockSpec((SUBLANES, n), lambda: (0, 0))],
        out_specs=[pl.BlockSpec((SUBLANES, n), lambda: (0, 0)), smem, smem, smem, smem],
        out_shape=[jax.ShapeDtypeStruct((SUBLANES, n), I32),
                   jax.ShapeDtypeStruct((N_EXPERTS,), I32),
                   jax.ShapeDtypeStruct((N_EXPERTS,), I32),
                   jax.ShapeDtypeStruct((1,), I32),
                   jax.ShapeDtypeStruct((N_EXPERTS,), I32)],
        scratch_shapes=[pltpu.SMEM((N_EXPERTS,), I32)],
        name="dispatch_plan",
    )(cnt_i32, meta_t)


def _dispatch_kernel(ztail_ref, nblk_ref, d0_ref, d1_ref, x_ref, xs_ref, zero_s, sem):
    i = pl.program_id(0)
    nb_max = xs_ref.shape[0] // ROUTE_ROWS

    def zero_block(start):
        start = pl.multiple_of(start, ROUTE_ROWS)
        return pltpu.make_async_copy(zero_s, xs_ref.at[pl.ds(start, ROUTE_ROWS)], sem.at[2])

    @pl.when(i == 0)
    def _():
        zero_s[...] = jnp.zeros_like(zero_s)

        def start_tail(e, c):
            @pl.when(ztail_ref[e] >= 0)
            def _():
                zero_block(ztail_ref[e]).start()
            return c

        def wait_tail(e, c):
            @pl.when(ztail_ref[e] >= 0)
            def _():
                zero_block(ztail_ref[e]).wait()
            return c

        def start_unused(b, c):
            zero_block(b * ROUTE_ROWS).start()
            return c

        def wait_unused(b, c):
            zero_block(b * ROUTE_ROWS).wait()
            return c

        lax.fori_loop(0, N_EXPERTS, start_tail, 0)
        lax.fori_loop(nblk_ref[0], nb_max, start_unused, 0)
        lax.fori_loop(0, N_EXPERTS, wait_tail, 0)
        lax.fori_loop(nblk_ref[0], nb_max, wait_unused, 0)

    def row_copy(o, j, dref, k):
        dst = dref[0, o * SUBLANES + j]
        return pltpu.make_async_copy(x_ref.at[o, pl.ds(j, 1)], xs_ref.at[pl.ds(dst, 1)], sem.at[k])

    def issue(o, c):
        for j in range(SUBLANES):
            row_copy(o, j, d0_ref, 0).start(priority=0)
            row_copy(o, j, d1_ref, 1).start(priority=1)
        return c

    def drain(o, c):
        for j in range(SUBLANES):
            row_copy(o, j, d0_ref, 0).wait()
            row_copy(o, j, d1_ref, 1).wait()
        return c

    lax.fori_loop(0, x_ref.shape[0], issue, 0)
    lax.fori_loop(0, x_ref.shape[0], drain, 0)


def _dispatch(ztail, nblk, dest, xn, nb_max):
    n, d = xn.shape
    tmd = _tile(n, 256)
    nt = n // tmd
    d0 = dest[0].reshape(nt, 1, tmd)
    d1 = dest[1].reshape(nt, 1, tmd)
    idx_spec = pl.BlockSpec((None, 1, tmd), lambda i: (i, 0, 0), memory_space=pltpu.SMEM)
    smem = pl.BlockSpec(memory_space=pltpu.SMEM)
    return pl.pallas_call(
        _dispatch_kernel,
        grid=(nt,),
        in_specs=[smem, smem, idx_spec, idx_spec,
                  pl.BlockSpec((tmd // SUBLANES, SUBLANES, d), lambda i: (i, 0, 0))],
        out_specs=pl.BlockSpec(memory_space=pl.ANY),
        out_shape=jax.ShapeDtypeStruct((nb_max * ROUTE_ROWS, d), xn.dtype),
        scratch_shapes=[pltpu.VMEM((ROUTE_ROWS, d), xn.dtype), pltpu.SemaphoreType.DMA((3,))],
        compiler_params=_cparams("arbitrary"),
        name="dispatch_scatter",
    )(ztail, nblk, d0, d1, xn.reshape(n // SUBLANES, SUBLANES, d))


def _expert_kernel(es_ref, ec_ref, nblk_ref, xs_ref, wg_ref, wu_ref, wd_ref, ys_ref,
                   x_s, y_s, wg_s, wu_s, wd_s, sem):
    e = pl.program_id(0)
    nb = ec_ref[e]
    b0 = es_ref[e]
    k = x_s.shape[2]
    nb_max = ys_ref.shape[0] // ROUTE_ROWS

    def rows(b):
        return pl.ds(pl.multiple_of(b * ROUTE_ROWS, ROUTE_ROWS), ROUTE_ROWS)

    nblk = nblk_ref[0]

    def copy_in(b):
        slot = lax.rem(b, IN_SLOTS)
        return pltpu.make_async_copy(xs_ref.at[rows(b)], x_s.at[slot], sem.at[0, slot])

    def copy_out(b):
        slot = lax.rem(b, OUT_SLOTS)
        return pltpu.make_async_copy(y_s.at[slot], ys_ref.at[rows(b)], sem.at[1, slot])

    @pl.when(e == 0)
    def _():
        for b in range(IN_AHEAD):
            @pl.when(b < nblk)
            def _(b=b):
                copy_in(b).start()

    @pl.when(nb > 0)
    def _():
        wg_s[...] = wg_ref[...].astype(BF16)
        wu_s[...] = wu_ref[...].astype(BF16)
        wd_s[...] = wd_ref[...].astype(BF16)

        def block(j, c):
            b = b0 + j
            copy_in(b).wait()

            @pl.when(b + IN_AHEAD < nblk)
            def _():
                copy_in(b + IN_AHEAD).start()

            x_hi, x_lo = _unpack_bf16_pairs(x_s[lax.rem(b, IN_SLOTS)])
            x_hi = x_hi.astype(BF16)
            x_lo = x_lo.astype(BF16)
            g = _dot(x_hi, wg_s[0:k, :]) + _dot(x_lo, wg_s[k:2 * k, :])
            u = _dot(x_hi, wu_s[0:k, :]) + _dot(x_lo, wu_s[k:2 * k, :])
            hb = (jax.nn.silu(g) * u).astype(BF16)
            y = _pack_bf16_pairs(_dot(hb, wd_s[...]))

            @pl.when(b >= OUT_SLOTS)
            def _():
                copy_out(b - OUT_SLOTS).wait()

            y_s[lax.rem(b, OUT_SLOTS)] = y
            copy_out(b).start()
            return c

        lax.fori_loop(0, nb, block, 0)

    @pl.when(e == pl.num_programs(0) - 1)
    def _():
        for back in range(OUT_SLOTS, 0, -1):
            @pl.when(nblk >= back)
            def _(back=back):
                copy_out(nblk - back).wait()

    @pl.when(e == pl.num_programs(0) - 1)
    def _():
        y_s[0] = jnp.zeros_like(y_s[0])

        def zero(b):
            return pltpu.make_async_copy(y_s.at[0], ys_ref.at[rows(b)], sem.at[1, 0])

        def start_zero(b, c):
            zero(b).start()
            return c

        def wait_zero(b, c):
            zero(b).wait()
            return c

        lax.fori_loop(nblk_ref[0], nb_max, start_zero, 0)
        lax.fori_loop(nblk_ref[0], nb_max, wait_zero, 0)


def _experts(estart, ecount, nblk, xs, w_gate, w_up, w_down, layer):
    p, k = xs.shape
    d, f = w_gate.shape[2], w_gate.shape[3]

    def w_map(e, es, ec, nb):
        return (layer, e, 0, 0)

    grid_spec = pltpu.PrefetchScalarGridSpec(
        num_scalar_prefetch=3,
        grid=(N_EXPERTS,),
        in_specs=[pl.BlockSpec(memory_space=pl.ANY),
                  pl.BlockSpec((None, None, d, f), w_map),
                  pl.BlockSpec((None, None, d, f), w_map),
                  pl.BlockSpec((None, None, f, d), w_map)],
        out_specs=pl.BlockSpec(memory_space=pl.ANY),
        scratch_shapes=[pltpu.VMEM((IN_SLOTS, ROUTE_ROWS, k), U32),
                        pltpu.VMEM((OUT_SLOTS, ROUTE_ROWS, k), U32),
                        pltpu.VMEM((d, f), BF16), pltpu.VMEM((d, f), BF16), pltpu.VMEM((f, d), BF16),
                        pltpu.SemaphoreType.DMA((2, max(IN_SLOTS, OUT_SLOTS)))],
    )
    return pl.pallas_call(
        _expert_kernel,
        grid_spec=grid_spec,
        out_shape=jax.ShapeDtypeStruct((p, k), U32),
        compiler_params=_cparams("arbitrary"),
        name="expert_mlp",
    )(estart, ecount, nblk, xs, w_gate, w_up, w_down)


def _combine_kernel(d0_ref, d1_ref, d0n_ref, d1n_ref, ys_ref, h1_ref, meta_ref, p_ref, gp_ref,
                    wpg_ref, bpg_ref, wp_ref, gf_ref, out_ref, ya_s, yb_s, sem, *, final):
    i = pl.program_id(0)
    nt = pl.num_programs(0)
    tmc = h1_ref.shape[0]
    slot = i % 2

    def row_copy(o, j, dref, dst, s, k):
        src = dref[0, o * SUBLANES + j]
        return pltpu.make_async_copy(ys_ref.at[pl.ds(src, 1)], dst.at[s, o, pl.ds(j, 1)], sem.at[s, k])

    def gather(d0r, d1r, s, wait):
        def body(o, c):
            for j in range(SUBLANES):
                ca = row_copy(o, j, d0r, ya_s, s, 0)
                cb = row_copy(o, j, d1r, yb_s, s, 1)
                if wait:
                    ca.wait()
                    cb.wait()
                else:
                    ca.start(priority=0)
                    cb.start(priority=1)
            return c

        lax.fori_loop(0, tmc // SUBLANES, body, 0)

    @pl.when(i == 0)
    def _():
        gather(d0_ref, d1_ref, slot, False)

    @pl.when(i + 1 < nt)
    def _():
        gather(d0n_ref, d1n_ref, 1 - slot, False)

    gather(d0_ref, d1_ref, slot, True)

    meta = meta_ref[...]
    w0 = meta[:, M_W0:M_W0 + 1]
    w1 = meta[:, M_W1:M_W1 + 1]
    ya_hi, ya_lo = _unpack_bf16_pairs(ya_s[slot].reshape(tmc, -1))
    yb_hi, yb_lo = _unpack_bf16_pairs(yb_s[slot].reshape(tmc, -1))
    y = jnp.concatenate([ya_hi * w0 + yb_hi * w1, ya_lo * w0 + yb_lo * w1], axis=1)
    h2 = h1_ref[...] + y
    xn = _rms(h2, gp_ref[...]).astype(BF16)
    gate = jax.nn.sigmoid(_dot(xn, wpg_ref[...]) + bpg_ref[...])
    pp = _dot(p_ref[...].astype(BF16), wp_ref[...])
    h3 = h2 + gate * pp
    if final:
        h3 = _rms(h3, gf_ref[...])
    out_ref[...] = h3


def _combine(dest, ys, h1, meta, p, layer, g_ple, w_ple_gate, b_ple_gate, w_ple, g_final, final):
    n, d = h1.shape
    pd = p.shape[2]
    tmc = _tile(n, 256)
    nt = n // tmc
    d0 = dest[0].reshape(nt, 1, tmc)
    d1 = dest[1].reshape(nt, 1, tmc)
    idx_spec = pl.BlockSpec((None, 1, tmc), lambda i: (i, 0, 0), memory_space=pltpu.SMEM)
    nxt_spec = pl.BlockSpec((None, 1, tmc), lambda i: (jnp.minimum(i + 1, nt - 1), 0, 0),
                            memory_space=pltpu.SMEM)
    vec = pl.BlockSpec((1, d), lambda i: (0, 0))
    k = ys.shape[1]
    return pl.pallas_call(
        functools.partial(_combine_kernel, final=final),
        grid=(nt,),
        in_specs=[idx_spec, idx_spec, nxt_spec, nxt_spec,
                  pl.BlockSpec(memory_space=pl.ANY),
                  pl.BlockSpec((tmc, d), lambda i: (i, 0)),
                  pl.BlockSpec((tmc, LANES), lambda i: (i, 0)),
                  pl.BlockSpec((None, tmc, pd), lambda i: (layer, i, 0)),
                  vec,
                  pl.BlockSpec((d, d), lambda i: (0, 0)),
                  vec,
                  pl.BlockSpec((pd, d), lambda i: (0, 0)),
                  vec],
        out_specs=pl.BlockSpec((tmc, d), lambda i: (i, 0)),
        out_shape=jax.ShapeDtypeStruct((n, d), F32),
        scratch_shapes=[pltpu.VMEM((2, tmc // SUBLANES, SUBLANES, k), U32),
                        pltpu.VMEM((2, tmc // SUBLANES, SUBLANES, k), U32),
                        pltpu.SemaphoreType.DMA((2, 2))],
        compiler_params=_cparams("arbitrary"),
        name="combine_ple",
    )(d0, d1, d0, d1, ys, h1, meta, p, g_ple, w_ple_gate, b_ple_gate, w_ple, g_final)


def _row(v):
    return v.reshape(1, -1)


def _moe_and_ple(z, w_out, h, i, final, norm_ffn, w_router_group, b_router_group, w_router_expert,
                 b_router_expert, w_exp_gate, w_exp_up, w_exp_down, norm_ple, w_ple, w_ple_gate,
                 b_ple_gate, norm_final, p):
    n, d = h.shape
    pad = LANES - N_GROUPS - N_EXPERTS
    w_router = jnp.concatenate(
        [w_router_group[i], w_router_expert[i], jnp.zeros((d, pad), F32)], axis=1)
    b_router = jnp.concatenate(
        [b_router_group[i], b_router_expert[i], jnp.zeros((pad,), F32)]).reshape(1, LANES)
    h1, xn, meta, meta_t, cnt = _post_mix(z, w_out, h, _row(norm_ffn[i]), w_router, b_router)
    cnt_i32 = cnt[0, EXPERT_LANE0:EXPERT_LANE0 + N_EXPERTS].astype(I32)
    nb_max = (2 * n) // ROUTE_ROWS + N_EXPERTS
    dest, estart, ecount, nblk, ztail = _plan(cnt_i32, meta_t)
    xs = _dispatch(ztail, nblk, dest, xn, nb_max)
    ys = _experts(estart, ecount, nblk, xs, w_exp_gate, w_exp_up, w_exp_down, i)
    return _combine(dest, ys, h1, meta, p.reshape(p.shape[0], n, -1), i, _row(norm_ple[i]),
                    w_ple_gate[i].astype(BF16), _row(b_ple_gate[i]), w_ple[i].astype(BF16),
                    _row(norm_final), final)


def kernel(x, p, norm_mix, w_in_a, conv_a, w_out_a, w_in_b, conv_b, conv_bias_b, w_rgate_b, b_rgate_b, w_igate_b, b_igate_b, lam_b, w_out_b, norm_ffn, w_router_group, b_router_group, w_router_expert, b_router_expert, w_exp_gate, w_exp_up, w_exp_down, norm_ple, w_ple, w_ple_gate, b_ple_gate, norm_final):
    batch, seq, d = x.shape
    depth = p.shape[0]
    n = batch * seq
    h = x.reshape(n, d)
    for i in range(depth):
        j = i // 2
        if i % 2 == 0:
            bg, v = _mix_a_in(h, _row(norm_mix[i]), w_in_a[j].astype(BF16))
            z = _mix_a_conv(v, bg, conv_a[j], seq)
            w_out = w_out_a[j]
        else:
            tt = _tile(seq, 256)
            yb, u_tiles = _mix_b_in(h, _row(norm_mix[i]), w_in_b[j].astype(BF16), batch, seq, tt)
            z = _rglru(u_tiles, yb, conv_b[j], _row(conv_bias_b[j]),
                       w_rgate_b[j].astype(BF16), b_rgate_b[j], w_igate_b[j].astype(BF16),
                       b_igate_b[j], lam_b[j], seq, tt)
            w_out = w_out_b[j]
        h = _moe_and_ple(z, w_out.astype(BF16), h, i, i == depth - 1, norm_ffn, w_router_group,
                         b_router_group, w_router_expert, b_router_expert, w_exp_gate, w_exp_up,
                         w_exp_down, norm_ple, w_ple, w_ple_gate, b_ple_gate, norm_final, p)
    return h.reshape(batch, seq, d)
```

```python
import functools

import jax
import jax.numpy as jnp
from jax import lax
from jax.experimental import pallas as pl
from jax.experimental.pallas import tpu as pltpu
from jax.experimental.pallas import tpu_sc as plsc

F32 = jnp.float32
BF16 = jnp.bfloat16
I32 = jnp.int32
U32 = jnp.uint32

EPS = 1e-6
N_GROUPS = 4
EXPERTS_PER_GROUP = 8
N_EXPERTS = N_GROUPS * EXPERTS_PER_GROUP
RG_C = 8.0

LANES = 128
SUBLANES = 8
BF16_ROWS = 16
EXPERT_LANE0 = N_GROUPS
ROUTE_ROWS = 256
IN_AHEAD = 3
IN_SLOTS = IN_AHEAD + 1
OUT_SLOTS = 2
SC_CORES = 2
SC_SUBCORES = 16
SC_CHUNK = 64
M_E0, M_E1, M_R0, M_R1, M_W0, M_W1 = 0, 1, 2, 3, 4, 5
VMEM_LIMIT = 48 * 1024 * 1024


def _cparams(*sem):
    return pltpu.CompilerParams(dimension_semantics=sem, vmem_limit_bytes=VMEM_LIMIT)


def _rms(x, g):
    return x * lax.rsqrt(jnp.mean(x * x, axis=-1, keepdims=True) + EPS) * g


def _dot(a, b):
    return jnp.dot(a, b, preferred_element_type=F32)


def _pack_bf16_pairs(x):
    k = x.shape[1] // 2
    bits = lax.bitcast_convert_type(x, U32)
    bits = bits + (jnp.uint32(0x7FFF) + ((bits >> 16) & jnp.uint32(1)))
    return (bits[:, :k] & jnp.uint32(0xFFFF0000)) | (bits[:, k:] >> 16)


def _unpack_bf16_pairs(p):
    hi = lax.bitcast_convert_type(p & jnp.uint32(0xFFFF0000), F32)
    lo = lax.bitcast_convert_type(p << 16, F32)
    return hi, lo


def _tile(n, want):
    t = min(n, want)
    assert n % t == 0, (n, want)
    return t


def _mix_a_in_kernel(h_ref, g_ref, w_ref, bg_ref, v_ref):
    d = h_ref.shape[1]
    xn = _rms(h_ref[...], g_ref[...]).astype(BF16)
    bg = _dot(xn, w_ref[:, 0:d])
    cg = _dot(xn, w_ref[:, d:2 * d])
    hh = _dot(xn, w_ref[:, 2 * d:3 * d])
    bg_ref[...] = bg.astype(BF16)
    v_ref[...] = (cg * hh).astype(BF16)


def _mix_a_in(h, g, w_in):
    n, d = h.shape
    tm = _tile(n, 512)
    return pl.pallas_call(
        _mix_a_in_kernel,
        grid=(n // tm,),
        in_specs=[pl.BlockSpec((tm, d), lambda i: (i, 0)),
                  pl.BlockSpec((1, d), lambda i: (0, 0)),
                  pl.BlockSpec((d, 3 * d), lambda i: (0, 0))],
        out_specs=[pl.BlockSpec((tm, d), lambda i: (i, 0)),
                   pl.BlockSpec((tm, d), lambda i: (i, 0))],
        out_shape=[jax.ShapeDtypeStruct((n, d), BF16), jax.ShapeDtypeStruct((n, d), BF16)],
        compiler_params=_cparams("parallel"),
        name="mix_a_in",
    )(h, g, w_in)


def _mix_a_conv_kernel(v_ref, vp_ref, vn_ref, bg_ref, cw_ref, z_ref, *, tiles_per_seq):
    i = pl.program_id(0)
    tm = v_ref.shape[0]
    v = v_ref[...].astype(F32)
    at_start = (i % tiles_per_seq) == 0
    at_end = (i % tiles_per_seq) == tiles_per_seq - 1
    prev_row = vp_ref[...].astype(F32)[BF16_ROWS - 1:BF16_ROWS, :]
    next_row = vn_ref[...].astype(F32)[0:1, :]
    prev_row = jnp.where(at_start, 0.0, prev_row)
    next_row = jnp.where(at_end, 0.0, next_row)
    row = lax.broadcasted_iota(I32, (tm, 1), 0)
    v_dn = jnp.where(row == 0, prev_row, pltpu.roll(v, 1, 0))
    v_up = jnp.where(row == tm - 1, next_row, pltpu.roll(v, tm - 1, 0))
    cw = cw_ref[...]
    u = cw[0:1, :] * v_dn + cw[1:2, :] * v + cw[2:3, :] * v_up
    z_ref[...] = (bg_ref[...].astype(F32) * u).astype(BF16)


def _mix_a_conv(v, bg, conv_w, seq):
    n, d = v.shape
    tm = _tile(seq, 512)
    hb = tm // BF16_ROWS
    nhalo = n // BF16_ROWS
    return pl.pallas_call(
        functools.partial(_mix_a_conv_kernel, tiles_per_seq=seq // tm),
        grid=(n // tm,),
        in_specs=[pl.BlockSpec((tm, d), lambda i: (i, 0)),
                  pl.BlockSpec((BF16_ROWS, d), lambda i: (jnp.maximum(i * hb - 1, 0), 0)),
                  pl.BlockSpec((BF16_ROWS, d), lambda i: (jnp.minimum((i + 1) * hb, nhalo - 1), 0)),
                  pl.BlockSpec((tm, d), lambda i: (i, 0)),
                  pl.BlockSpec(conv_w.shape, lambda i: (0, 0))],
        out_specs=pl.BlockSpec((tm, d), lambda i: (i, 0)),
        out_shape=jax.ShapeDtypeStruct((n, d), BF16),
        compiler_params=_cparams("parallel"),
        name="mix_a_conv",
    )(v, v, v, bg, conv_w)


def _mix_b_in_kernel(h_ref, hp_ref, hn_ref, g_ref, w_ref, yb_ref, u_ref, *, steps_per_seq, tt):
    i = pl.program_id(0)
    r = yb_ref.shape[1]
    tiles = h_ref.shape[0] // tt
    g = g_ref[...]
    xn = _rms(h_ref[...], g).astype(BF16)
    yb_ref[...] = jax.nn.gelu(_dot(xn, w_ref[:, 0:r])).astype(BF16)
    u = _dot(xn, w_ref[:, r:2 * r])
    xh = _rms(jnp.concatenate([hp_ref[...], hn_ref[...]], axis=0), g).astype(BF16)
    uh = _dot(xh, w_ref[:, r:2 * r])
    at_start = (i % steps_per_seq) == 0
    at_end = (i % steps_per_seq) == steps_per_seq - 1
    prev2 = jnp.where(at_start, 0.0, uh[SUBLANES - 2:SUBLANES, :])
    next1 = jnp.where(at_end, 0.0, uh[SUBLANES:SUBLANES + 1, :])
    row = lax.broadcasted_iota(I32, (SUBLANES, 1), 0)
    for k in range(tiles):
        p2 = prev2 if k == 0 else u[k * tt - 2:k * tt]
        n1 = next1 if k == tiles - 1 else u[(k + 1) * tt:(k + 1) * tt + 1]
        pad = jnp.where(row == 0, p2[0:1], jnp.where(row == 1, p2[1:2], jnp.where(row == 2, n1, 0.0)))
        for s in range(r // LANES):
            lanes = slice(s * LANES, (s + 1) * LANES)
            u_ref[k, s, 0:tt, :] = u[k * tt:(k + 1) * tt, lanes]
            u_ref[k, s, tt:tt + SUBLANES, :] = pad[:, lanes]


def _mix_b_in(h, g, w_in, batch, seq, tt):
    n, d = h.shape
    r = w_in.shape[1] // 2
    tm = _tile(seq, 512)
    steps_per_seq = seq // tm
    tiles = tm // tt
    hb = tm // SUBLANES
    nhalo = n // SUBLANES
    return pl.pallas_call(
        functools.partial(_mix_b_in_kernel, steps_per_seq=steps_per_seq, tt=tt),
        grid=(n // tm,),
        in_specs=[pl.BlockSpec((tm, d), lambda i: (i, 0)),
                  pl.BlockSpec((SUBLANES, d), lambda i: (jnp.maximum(i * hb - 1, 0), 0)),
                  pl.BlockSpec((SUBLANES, d), lambda i: (jnp.minimum((i + 1) * hb, nhalo - 1), 0)),
                  pl.BlockSpec((1, d), lambda i: (0, 0)),
                  pl.BlockSpec((d, 2 * r), lambda i: (0, 0))],
        out_specs=[pl.BlockSpec((tm, r), lambda i: (i, 0)),
                   pl.BlockSpec((tiles, r // LANES, None, tt + SUBLANES, LANES),
                                lambda i: (i % steps_per_seq, 0, i // steps_per_seq, 0, 0))],
        out_shape=[jax.ShapeDtypeStruct((n, r), BF16),
                   jax.ShapeDtypeStruct((seq // tt, r // LANES, batch, tt + SUBLANES, LANES), F32)],
        compiler_params=_cparams("parallel"),
        name="mix_b_in",
    )(h, h, h, g, w_in)


def _rglru_pass_kernel(*refs, reverse, tt):
    if reverse:
        (u_ref, cw_ref, cb_ref, wr_ref, br_ref, wi_ref, bi_ref, lam_ref, hf_ref, yb_ref,
         out_ref, x_s, a_s, g_s, carry_s, nat_s) = refs
    else:
        (u_ref, cw_ref, cb_ref, wr_ref, br_ref, wi_ref, bi_ref, lam_ref,
         out_ref, x_s, a_s, g_s, carry_s) = refs
    nb = SUBLANES
    ttp = tt + SUBLANES
    rows = tt * nb
    slabs = u_ref.shape[0]

    @pl.when(pl.program_id(1) == 0)
    def _():
        carry_s[...] = jnp.zeros_like(carry_s)

    def gather_t(tl, dst):
        for s in range(slabs):
            x_s[pl.ds(dst, nb), s * LANES:(s + 1) * LANES] = u_ref[s, pl.ds(tl, nb, stride=ttp), :]

    def load_t(tl, c):
        gather_t(tl, pl.multiple_of((tl + 2) * nb, nb))
        return c

    lax.fori_loop(0, tt, load_t, 0, unroll=8)
    gather_t(tt, 0)
    gather_t(tt + 1, nb)
    gather_t(tt + 2, (tt + 2) * nb)

    cw = cw_ref[...]
    u = (cw[0:1, :] * x_s[0:rows, :] + cw[1:2, :] * x_s[nb:nb + rows, :]
         + cw[2:3, :] * x_s[2 * nb:2 * nb + rows, :] + cw[3:4, :] * x_s[3 * nb:3 * nb + rows, :]
         + cb_ref[...])
    ub = u.astype(BF16)
    r = 0.5 * (1.0 + jnp.tanh(0.5 * (_dot(ub, wr_ref[...]) + br_ref[...])))
    ig = 0.5 * (1.0 + jnp.tanh(0.5 * (_dot(ub, wi_ref[...]) + bi_ref[...])))
    log_a = -RG_C * r * jax.nn.softplus(-lam_ref[...])
    th = jnp.tanh(0.5 * log_a)
    q = 1.0 / (1.0 - th)
    a_s[...] = (1.0 + th) * q
    m2 = -4.0 * th * q * q
    g_s[...] = jnp.where(m2 > 0.0, m2 * lax.rsqrt(m2), 0.0) * (ig * u)

    def step(k, h):
        tl = tt - 1 - k if reverse else k
        base = pl.multiple_of(tl * nb, nb)
        h = a_s[pl.ds(base, nb), :] * h + g_s[pl.ds(base, nb), :]
        if reverse:
            hs = h + hf_ref[pl.ds(base, nb), :]
            for s in range(slabs):
                nat_s[s, pl.ds(tl, nb, stride=ttp), :] = hs[:, s * LANES:(s + 1) * LANES]
        else:
            out_ref[pl.ds(base, nb), :] = h
        return h

    carry_s[...] = lax.fori_loop(0, tt, step, carry_s[...], unroll=8)
    if reverse:
        for b in range(nb):
            for s in range(slabs):
                lanes = slice(s * LANES, (s + 1) * LANES)
                hb = nat_s[s, b * ttp:b * ttp + tt, :]
                out_ref[b, :, lanes] = (yb_ref[b, :, lanes].astype(F32) * hb).astype(BF16)


def _rglru_pass(u_tiles, conv_w, conv_b, w_r, b_r, w_i, b_i, lam, hf, yb, seq, tt, reverse):
    nt, nslab, batch, ttp, _ = u_tiles.shape
    r = nslab * LANES
    assert batch == SUBLANES, "time-major tile layout puts the batch on the 8 sublanes"
    nh, bk = w_r.shape[1], w_r.shape[2]
    slabs = bk // LANES
    kw = conv_w.shape[0]
    d = 1 if reverse else 0
    rows = tt * batch

    def tile(jj):
        return nt - 1 - jj if reverse else jj

    vec = pl.BlockSpec((None, 1, bk), lambda h, jj: (d, 0, h))
    mat = pl.BlockSpec((None, None, bk, bk), lambda h, jj: (d, h, 0, 0))
    in_specs = [pl.BlockSpec((None, slabs, batch * ttp, LANES), lambda h, jj: (tile(jj), h, 0, 0)),
                pl.BlockSpec((kw, bk), lambda h, jj: (0, h)),
                pl.BlockSpec((1, bk), lambda h, jj: (0, h)),
                mat, vec, mat, vec, vec]
    args = [u_tiles.reshape(nt, nslab, batch * ttp, LANES), conv_w, conv_b, w_r, b_r.reshape(2, 1, r),
            w_i, b_i.reshape(2, 1, r), lam.reshape(2, 1, r)]
    scratch = [pltpu.VMEM(((tt + 3) * batch, bk), F32), pltpu.VMEM((rows, bk), F32),
               pltpu.VMEM((rows, bk), F32), pltpu.VMEM((batch, bk), F32)]
    if reverse:
        in_specs += [pl.BlockSpec((rows, bk), lambda h, jj: (tile(jj), h)),
                     pl.BlockSpec((batch, tt, bk), lambda h, jj: (0, tile(jj), h))]
        args += [hf, yb.reshape(batch, seq, r)]
        scratch += [pltpu.VMEM((slabs, batch * ttp, LANES), F32)]
        out_spec = pl.BlockSpec((batch, tt, bk), lambda h, jj: (0, tile(jj), h))
        out_shape = jax.ShapeDtypeStruct((batch, seq, r), BF16)
    else:
        out_spec = pl.BlockSpec((rows, bk), lambda h, jj: (tile(jj), h))
        out_shape = jax.ShapeDtypeStruct((seq * batch, r), F32)
    return pl.pallas_call(
        functools.partial(_rglru_pass_kernel, reverse=reverse, tt=tt),
        grid=(nh, nt),
        in_specs=in_specs,
        out_specs=out_spec,
        out_shape=out_shape,
        scratch_shapes=scratch,
        compiler_params=_cparams("parallel", "arbitrary"),
        name="rglru_rev" if reverse else "rglru_fwd",
    )(*args)


def _rglru(u_tiles, yb, conv_w, conv_b, w_r, b_r, w_i, b_i, lam, seq, tt):
    hf = _rglru_pass(u_tiles, conv_w, conv_b, w_r, b_r, w_i, b_i, lam, None, None, seq, tt, False)
    z = _rglru_pass(u_tiles, conv_w, conv_b, w_r, b_r, w_i, b_i, lam, hf, yb, seq, tt, True)
    return z.reshape(-1, z.shape[2])


def _post_mix_kernel(z_ref, wo_ref, h_ref, g_ref, wr_ref, br_ref,
                     h1_ref, xn_ref, meta_ref, meta_t_ref, cnt_ref, carry_ref):
    i = pl.program_id(0)
    tm = z_ref.shape[0]

    @pl.when(i == 0)
    def _():
        carry_ref[...] = jnp.zeros_like(carry_ref)

    h1 = h_ref[...] + _dot(z_ref[...], wo_ref[...])
    h1_ref[...] = h1
    xn = _rms(h1, g_ref[...])
    xn_ref[...] = _pack_bf16_pairs(xn)

    x_hi = xn.astype(BF16)
    x_lo = (xn - x_hi.astype(F32)).astype(BF16)
    w = wr_ref[...]
    w_hi = w.astype(BF16)
    w_lo = (w - w_hi.astype(F32)).astype(BF16)
    logits = _dot(x_hi, w_hi) + _dot(x_hi, w_lo) + _dot(x_lo, w_hi) + br_ref[...]

    lane = lax.broadcasted_iota(I32, (tm, LANES), 1).astype(F32)
    neg = jnp.float32(-jnp.inf)
    nolane = jnp.float32(LANES)
    gmask = lane < N_GROUPS
    gmax = jnp.max(jnp.where(gmask, logits, neg), axis=1, keepdims=True)
    eg = jnp.where(gmask, jnp.exp(logits - gmax), 0.0)
    pg = eg / jnp.sum(eg, axis=1, keepdims=True)
    pg_top = jnp.max(pg, axis=1, keepdims=True)
    g_top = jnp.min(jnp.where(gmask & (pg == pg_top), lane, nolane), axis=1, keepdims=True)
    lo = EXPERT_LANE0 + g_top * EXPERTS_PER_GROUP
    emask = (lane >= lo) & (lane < lo + EXPERTS_PER_GROUP)
    emax = jnp.max(jnp.where(emask, logits, neg), axis=1, keepdims=True)
    ee = jnp.where(emask, jnp.exp(logits - emax), 0.0)
    pe = jnp.where(emask, ee / jnp.sum(ee, axis=1, keepdims=True), -1.0)
    p1 = jnp.max(pe, axis=1, keepdims=True)
    i1 = jnp.min(jnp.where(pe == p1, lane, nolane), axis=1, keepdims=True)
    pe2 = jnp.where(lane == i1, -1.0, pe)
    p2 = jnp.max(pe2, axis=1, keepdims=True)
    i2 = jnp.min(jnp.where(pe2 == p2, lane, nolane), axis=1, keepdims=True)
    psum = p1 + p2
    w0 = pg_top * (p1 / psum)
    w1 = pg_top * (p2 / psum)

    oh0 = lane == i1
    oh1 = lane == i2
    both = (oh0 | oh1).astype(BF16)
    ri = lax.broadcasted_iota(I32, (tm, tm), 0)
    ci = lax.broadcasted_iota(I32, (tm, tm), 1)
    before = (ri > ci).astype(BF16)
    cnt_before = _dot(before, both) + carry_ref[...]
    rank0 = jnp.sum(jnp.where(oh0, cnt_before, 0.0), axis=1, keepdims=True)
    rank1 = jnp.sum(jnp.where(oh1, cnt_before, 0.0), axis=1, keepdims=True)
    carry = carry_ref[...] + jnp.sum(both.astype(F32), axis=0, keepdims=True)
    carry_ref[...] = carry
    cnt_ref[...] = carry

    e0 = i1 - EXPERT_LANE0
    e1 = i2 - EXPERT_LANE0
    meta = jnp.zeros((tm, LANES), F32)
    for ln, val in ((M_E0, e0), (M_E1, e1), (M_R0, rank0), (M_R1, rank1), (M_W0, w0), (M_W1, w1)):
        meta = jnp.where(lane == ln, val, meta)
    meta_ref[...] = meta
    meta_t_ref[...] = meta.T[0:SUBLANES, :]


def _post_mix(z, w_out, h, g, w_router, b_router):
    n, d = h.shape
    k = z.shape[1]
    tm = _tile(n, 512)
    return pl.pallas_call(
        _post_mix_kernel,
        grid=(n // tm,),
        in_specs=[pl.BlockSpec((tm, k), lambda i: (i, 0)),
                  pl.BlockSpec((k, d), lambda i: (0, 0)),
                  pl.BlockSpec((tm, d), lambda i: (i, 0)),
                  pl.BlockSpec((1, d), lambda i: (0, 0)),
                  pl.BlockSpec((d, LANES), lambda i: (0, 0)),
                  pl.BlockSpec((1, LANES), lambda i: (0, 0))],
        out_specs=[pl.BlockSpec((tm, d), lambda i: (i, 0)),
                   pl.BlockSpec((tm, d // 2), lambda i: (i, 0)),
                   pl.BlockSpec((tm, LANES), lambda i: (i, 0)),
                   pl.BlockSpec((SUBLANES, tm), lambda i: (0, i)),
                   pl.BlockSpec((1, LANES), lambda i: (0, 0))],
        out_shape=[jax.ShapeDtypeStruct((n, d), F32), jax.ShapeDtypeStruct((n, d // 2), U32),
                   jax.ShapeDtypeStruct((n, LANES), F32), jax.ShapeDtypeStruct((SUBLANES, n), F32),
                   jax.ShapeDtypeStruct((1, LANES), F32)],
        scratch_shapes=[pltpu.VMEM((1, LANES), F32)],
        compiler_params=_cparams("arbitrary"),
        name="post_mix_router",
    )(z, w_out, h, g, w_router, b_router)


def _plan_kernel(cnt_ref, meta_t_ref, dest_ref, estart_ref, ecount_ref, nblk_ref, pstart_ref):
    def per_expert(e, acc):
        nb_e = (cnt_ref[e] + (ROUTE_ROWS - 1)) // ROUTE_ROWS
        pstart_ref[e] = acc * ROUTE_ROWS
        estart_ref[e] = acc
        ecount_ref[e] = nb_e
        return acc + nb_e

    nblk_ref[0] = lax.fori_loop(0, N_EXPERTS, per_expert, 0)

    e0 = meta_t_ref[M_E0:M_E0 + 1, :]
    e1 = meta_t_ref[M_E1:M_E1 + 1, :]
    d0 = meta_t_ref[M_R0:M_R0 + 1, :]
    d1 = meta_t_ref[M_R1:M_R1 + 1, :]
    for e in range(N_EXPERTS):
        ps = pstart_ref[e].astype(F32)
        d0 = d0 + jnp.where(e0 == e, ps, 0.0)
        d1 = d1 + jnp.where(e1 == e, ps, 0.0)
    dest_ref[...] = jnp.zeros_like(dest_ref)
    dest_ref[0:1, :] = d0.astype(I32)
    dest_ref[1:2, :] = d1.astype(I32)


def _plan(cnt_i32, meta_t):
    n = meta_t.shape[1]
    smem = pl.BlockSpec(memory_space=pltpu.SMEM)
    return pl.pallas_call(
        _plan_kernel,
        in_specs=[smem, pl.BlockSpec((SUBLANES, n), lambda: (0, 0))],
        out_specs=[pl.BlockSpec((SUBLANES, n), lambda: (0, 0)), smem, smem, smem],
        out_shape=[jax.ShapeDtypeStruct((SUBLANES, n), I32),
                   jax.ShapeDtypeStruct((N_EXPERTS,), I32),
                   jax.ShapeDtypeStruct((N_EXPERTS,), I32),
                   jax.ShapeDtypeStruct((1,), I32)],
        scratch_shapes=[pltpu.SMEM((N_EXPERTS,), I32)],
        name="dispatch_plan",
    )(cnt_i32, meta_t)


def _sc_mesh():
    return plsc.VectorSubcoreMesh(core_axis_name="c", subcore_axis_name="s",
                                  num_cores=SC_CORES, num_subcores=SC_SUBCORES)


def _sc_scatter_rows(x, idx0, idx1, p_rows):
    m, k = x.shape
    per_w = m // (SC_CORES * SC_SUBCORES)
    assert per_w % SC_CHUNK == 0 and per_w * SC_CORES * SC_SUBCORES == m

    @functools.partial(
        pl.kernel, mesh=_sc_mesh(), out_type=jax.ShapeDtypeStruct((p_rows, k), x.dtype),
        scratch_types=[pltpu.VMEM((SC_CHUNK,), I32), pltpu.VMEM((SC_CHUNK,), I32),
                       pltpu.VMEM((SC_CHUNK, k), x.dtype), pltpu.SemaphoreType.DMA],
        name="sc_scatter_rows")
    def scatter(x_hbm, i0_hbm, i1_hbm, out_hbm, i0_v, i1_v, rows_v, sem):
        base = (lax.axis_index("s") * SC_CORES + lax.axis_index("c")) * per_w

        @pl.loop(0, per_w // SC_CHUNK)
        def _(c):
            off = pl.multiple_of(base + c * SC_CHUNK, SC_CHUNK)
            pltpu.sync_copy(x_hbm.at[pl.ds(off, SC_CHUNK)], rows_v)
            pltpu.sync_copy(i0_hbm.at[pl.ds(off, SC_CHUNK)], i0_v)
            pltpu.sync_copy(i1_hbm.at[pl.ds(off, SC_CHUNK)], i1_v)
            pltpu.async_copy(rows_v, out_hbm.at[i0_v], sem).wait()
            pltpu.async_copy(rows_v, out_hbm.at[i1_v], sem).wait()

    return scatter(x, idx0, idx1)


def _expert_kernel(es_ref, ec_ref, nblk_ref, cnt_ref, xs_ref, wg_ref, wu_ref, wd_ref, ys_ref,
                   x_s, y_s, wg_s, wu_s, wd_s, sem):
    e = pl.program_id(0)
    nb = ec_ref[e]
    b0 = es_ref[e]
    k = x_s.shape[2]
    nb_max = ys_ref.shape[0] // ROUTE_ROWS

    def rows(b):
        return pl.ds(pl.multiple_of(b * ROUTE_ROWS, ROUTE_ROWS), ROUTE_ROWS)

    nblk = nblk_ref[0]

    def copy_in(b):
        slot = lax.rem(b, IN_SLOTS)
        return pltpu.make_async_copy(xs_ref.at[rows(b)], x_s.at[slot], sem.at[0, slot])

    def copy_out(b):
        slot = lax.rem(b, OUT_SLOTS)
        return pltpu.make_async_copy(y_s.at[slot], ys_ref.at[rows(b)], sem.at[1, slot])

    @pl.when(e == 0)
    def _():
        for b in range(IN_AHEAD):
            @pl.when(b < nblk)
            def _(b=b):
                copy_in(b).start()

    @pl.when(nb > 0)
    def _():
        wg_s[...] = wg_ref[...].astype(BF16)
        wu_s[...] = wu_ref[...].astype(BF16)
        wd_s[...] = wd_ref[...].astype(BF16)

        def block(j, c):
            b = b0 + j
            copy_in(b).wait()

            @pl.when(b + IN_AHEAD < nblk)
            def _():
                copy_in(b + IN_AHEAD).start()

            row = lax.broadcasted_iota(I32, (ROUTE_ROWS, 1), 0)
            live = row < cnt_ref[e] - j * ROUTE_ROWS
            xp = jnp.where(live, x_s[lax.rem(b, IN_SLOTS)], jnp.uint32(0))
            x_hi, x_lo = _unpack_bf16_pairs(xp)
            x_hi = x_hi.astype(BF16)
            x_lo = x_lo.astype(BF16)
            g = _dot(x_hi, wg_s[0:k, :]) + _dot(x_lo, wg_s[k:2 * k, :])
            u = _dot(x_hi, wu_s[0:k, :]) + _dot(x_lo, wu_s[k:2 * k, :])
            hb = (jax.nn.silu(g) * u).astype(BF16)
            y = _pack_bf16_pairs(_dot(hb, wd_s[...]))

            @pl.when(b >= OUT_SLOTS)
            def _():
                copy_out(b - OUT_SLOTS).wait()

            y_s[lax.rem(b, OUT_SLOTS)] = y
            copy_out(b).start()
            return c

        lax.fori_loop(0, nb, block, 0)

    @pl.when(e == pl.num_programs(0) - 1)
    def _():
        for back in range(OUT_SLOTS, 0, -1):
            @pl.when(nblk >= back)
            def _(back=back):
                copy_out(nblk - back).wait()

    @pl.when(e == pl.num_programs(0) - 1)
    def _():
        y_s[0] = jnp.zeros_like(y_s[0])

        def zero(b):
            return pltpu.make_async_copy(y_s.at[0], ys_ref.at[rows(b)], sem.at[1, 0])

        def start_zero(b, c):
            zero(b).start()
            return c

        def wait_zero(b, c):
            zero(b).wait()
            return c

        lax.fori_loop(nblk_ref[0], nb_max, start_zero, 0)
        lax.fori_loop(nblk_ref[0], nb_max, wait_zero, 0)


def _experts(estart, ecount, nblk, cnt, xs, w_gate, w_up, w_down, layer):
    p, k = xs.shape
    d, f = w_gate.shape[2], w_gate.shape[3]

    def w_map(e, es, ec, nb, cn):
        return (layer, e, 0, 0)

    grid_spec = pltpu.PrefetchScalarGridSpec(
        num_scalar_prefetch=4,
        grid=(N_EXPERTS,),
        in_specs=[pl.BlockSpec(memory_space=pl.ANY),
                  pl.BlockSpec((None, None, d, f), w_map),
                  pl.BlockSpec((None, None, d, f), w_map),
                  pl.BlockSpec((None, None, f, d), w_map)],
        out_specs=pl.BlockSpec(memory_space=pl.ANY),
        scratch_shapes=[pltpu.VMEM((IN_SLOTS, ROUTE_ROWS, k), U32),
                        pltpu.VMEM((OUT_SLOTS, ROUTE_ROWS, k), U32),
                        pltpu.VMEM((d, f), BF16), pltpu.VMEM((d, f), BF16), pltpu.VMEM((f, d), BF16),
                        pltpu.SemaphoreType.DMA((2, max(IN_SLOTS, OUT_SLOTS)))],
    )
    return pl.pallas_call(
        _expert_kernel,
        grid_spec=grid_spec,
        out_shape=jax.ShapeDtypeStruct((p, k), U32),
        compiler_params=_cparams("arbitrary"),
        name="expert_mlp",
    )(estart, ecount, nblk, cnt, xs, w_gate, w_up, w_down)


def _sc_gather_rows(table, idx):
    m = idx.shape[0]
    k = table.shape[1]
    per_w = m // (SC_CORES * SC_SUBCORES)
    assert per_w % SC_CHUNK == 0 and per_w * SC_CORES * SC_SUBCORES == m

    @functools.partial(
        pl.kernel, mesh=_sc_mesh(), out_type=jax.ShapeDtypeStruct((m, k), table.dtype),
        scratch_types=[pltpu.VMEM((SC_CHUNK,), I32), pltpu.VMEM((SC_CHUNK, k), table.dtype),
                       pltpu.SemaphoreType.DMA],
        name="sc_gather_rows")
    def gather(table_hbm, idx_hbm, out_hbm, idx_v, rows_v, sem):
        base = (lax.axis_index("s") * SC_CORES + lax.axis_index("c")) * per_w

        @pl.loop(0, per_w // SC_CHUNK)
        def _(c):
            off = pl.multiple_of(base + c * SC_CHUNK, SC_CHUNK)
            pltpu.sync_copy(idx_hbm.at[pl.ds(off, SC_CHUNK)], idx_v)
            pltpu.async_copy(table_hbm.at[idx_v], rows_v, sem).wait()
            pltpu.sync_copy(rows_v, out_hbm.at[pl.ds(off, SC_CHUNK)])

    return gather(table, idx)


def _combine_kernel(ya_ref, yb_ref, h1_ref, meta_ref, p_ref, gp_ref, wpg_ref, bpg_ref, wp_ref,
                    gf_ref, out_ref, *, final):
    meta = meta_ref[...]
    w0 = meta[:, M_W0:M_W0 + 1]
    w1 = meta[:, M_W1:M_W1 + 1]
    ya_hi, ya_lo = _unpack_bf16_pairs(ya_ref[...])
    yb_hi, yb_lo = _unpack_bf16_pairs(yb_ref[...])
    y = jnp.concatenate([ya_hi * w0 + yb_hi * w1, ya_lo * w0 + yb_lo * w1], axis=1)
    h2 = h1_ref[...] + y
    xn = _rms(h2, gp_ref[...]).astype(BF16)
    gate = jax.nn.sigmoid(_dot(xn, wpg_ref[...]) + bpg_ref[...])
    pp = _dot(p_ref[...].astype(BF16), wp_ref[...])
    h3 = h2 + gate * pp
    if final:
        h3 = _rms(h3, gf_ref[...])
    out_ref[...] = h3


def _combine(dest, ys, h1, meta, p, layer, g_ple, w_ple_gate, b_ple_gate, w_ple, g_final, final):
    n, d = h1.shape
    pd = p.shape[2]
    k = ys.shape[1]
    tmc = _tile(n, 512)
    ya = _sc_gather_rows(ys, dest[0])
    yb = _sc_gather_rows(ys, dest[1])
    vec = pl.BlockSpec((1, d), lambda i: (0, 0))
    return pl.pallas_call(
        functools.partial(_combine_kernel, final=final),
        grid=(n // tmc,),
        in_specs=[pl.BlockSpec((tmc, k), lambda i: (i, 0)),
                  pl.BlockSpec((tmc, k), lambda i: (i, 0)),
                  pl.BlockSpec((tmc, d), lambda i: (i, 0)),
                  pl.BlockSpec((tmc, LANES), lambda i: (i, 0)),
                  pl.BlockSpec((None, tmc, pd), lambda i: (layer, i, 0)),
                  vec,
                  pl.BlockSpec((d, d), lambda i: (0, 0)),
                  vec,
                  pl.BlockSpec((pd, d), lambda i: (0, 0)),
                  vec],
        out_specs=pl.BlockSpec((tmc, d), lambda i: (i, 0)),
        out_shape=jax.ShapeDtypeStruct((n, d), F32),
        compiler_params=_cparams("parallel"),
        name="combine_ple",
    )(ya, yb, h1, meta, p, g_ple, w_ple_gate, b_ple_gate, w_ple, g_final)


def _row(v):
    return v.reshape(1, -1)


def _moe_and_ple(z, w_out, h, i, final, norm_ffn, w_router_group, b_router_group, w_router_expert,
                 b_router_expert, w_exp_gate, w_exp_up, w_exp_down, norm_ple, w_ple, w_ple_gate,
                 b_ple_gate, norm_final, p):
    n, d = h.shape
    pad = LANES - N_GROUPS - N_EXPERTS
    w_router = jnp.concatenate(
        [w_router_group[i], w_router_expert[i], jnp.zeros((d, pad), F32)], axis=1)
    b_router = jnp.concatenate(
        [b_router_group[i], b_router_expert[i], jnp.zeros((pad,), F32)]).reshape(1, LANES)
    h1, xn, meta, meta_t, cnt = _post_mix(z, w_out, h, _row(norm_ffn[i]), w_router, b_router)
    cnt_i32 = cnt[0, EXPERT_LANE0:EXPERT_LANE0 + N_EXPERTS].astype(I32)
    nb_max = (2 * n) // ROUTE_ROWS + N_EXPERTS
    dest, estart, ecount, nblk = _plan(cnt_i32, meta_t)
    xs = _sc_scatter_rows(xn, dest[0], dest[1], nb_max * ROUTE_ROWS)
    ys = _experts(estart, ecount, nblk, cnt_i32, xs, w_exp_gate, w_exp_up, w_exp_down, i)
    return _combine(dest, ys, h1, meta, p.reshape(p.shape[0], n, -1), i, _row(norm_ple[i]),
                    w_ple_gate[i].astype(BF16), _row(b_ple_gate[i]), w_ple[i].astype(BF16),
                    _row(norm_final), final)


def kernel(x, p, norm_mix, w_in_a, conv_a, w_out_a, w_in_b, conv_b, conv_bias_b, w_rgate_b, b_rgate_b, w_igate_b, b_igate_b, lam_b, w_out_b, norm_ffn, w_router_group, b_router_group, w_router_expert, b_router_expert, w_exp_gate, w_exp_up, w_exp_down, norm_ple, w_ple, w_ple_gate, b_ple_gate, norm_final):
    batch, seq, d = x.shape
    depth = p.shape[0]
    n = batch * seq
    h = x.reshape(n, d)
    for i in range(depth):
        j = i // 2
        if i % 2 == 0:
            bg, v = _mix_a_in(h, _row(norm_mix[i]), w_in_a[j].astype(BF16))
            z = _mix_a_conv(v, bg, conv_a[j], seq)
            w_out = w_out_a[j]
        else:
            tt = _tile(seq, 256)
            yb, u_tiles = _mix_b_in(h, _row(norm_mix[i]), w_in_b[j].astype(BF16), batch, seq, tt)
            z = _rglru(u_tiles, yb, conv_b[j], _row(conv_bias_b[j]),
                       w_rgate_b[j].astype(BF16), b_rgate_b[j], w_igate_b[j].astype(BF16),
                       b_igate_b[j], lam_b[j], seq, tt)
            w_out = w_out_b[j]
        h = _moe_and_ple(z, w_out.astype(BF16), h, i, i == depth - 1, norm_ffn, w_router_group,
                         b_router_group, w_router_expert, b_router_expert, w_exp_gate, w_exp_up,
                         w_exp_down, norm_ple, w_ple, w_ple_gate, b_ple_gate, norm_final, p)
    return h.reshape(batch, seq, d)
```

```python
import functools

import jax
import jax.numpy as jnp
from jax import lax
from jax.experimental import pallas as pl
from jax.experimental.pallas import tpu as pltpu
from jax.experimental.pallas import tpu_sc as plsc

F32 = jnp.float32
BF16 = jnp.bfloat16
I32 = jnp.int32
U32 = jnp.uint32

EPS = 1e-6
N_GROUPS = 4
EXPERTS_PER_GROUP = 8
N_EXPERTS = N_GROUPS * EXPERTS_PER_GROUP
RG_C = 8.0

LANES = 128
SUBLANES = 8
BF16_ROWS = 16
EXPERT_LANE0 = N_GROUPS
ROUTE_ROWS = 256
IN_AHEAD = 3
IN_SLOTS = IN_AHEAD + 1
OUT_SLOTS = 2
SC_CORES = 2
SC_SUBCORES = 16
SC_CHUNK = 128
M_E0, M_E1, M_R0, M_R1, M_W0, M_W1 = 0, 1, 2, 3, 4, 5
VMEM_LIMIT = 48 * 1024 * 1024


def _cparams(*sem):
    return pltpu.CompilerParams(dimension_semantics=sem, vmem_limit_bytes=VMEM_LIMIT)


def _rms(x, g):
    return x * lax.rsqrt(jnp.mean(x * x, axis=-1, keepdims=True) + EPS) * g


def _dot(a, b):
    return jnp.dot(a, b, preferred_element_type=F32)


def _pack_bf16_pairs(x):
    k = x.shape[1] // 2
    hi = lax.bitcast_convert_type(x[:, :k].astype(BF16).astype(F32), U32)
    lo = lax.bitcast_convert_type(x[:, k:].astype(BF16).astype(F32), U32)
    return hi | (lo >> 16)


def _unpack_bf16_pairs(p):
    hi = lax.bitcast_convert_type(p & jnp.uint32(0xFFFF0000), F32)
    lo = lax.bitcast_convert_type(p << 16, F32)
    return hi, lo


def _tile(n, want):
    t = min(n, want)
    assert n % t == 0, (n, want)
    return t


def _mix_a_in_kernel(h_ref, g_ref, w_ref, bg_ref, v_ref):
    d = h_ref.shape[1]
    xn = _rms(h_ref[...], g_ref[...]).astype(BF16)
    bg = _dot(xn, w_ref[:, 0:d])
    cg = _dot(xn, w_ref[:, d:2 * d])
    hh = _dot(xn, w_ref[:, 2 * d:3 * d])
    bg_ref[...] = bg.astype(BF16)
    v_ref[...] = (cg * hh).astype(BF16)


def _mix_a_in(h, g, w_in):
    n, d = h.shape
    tm = _tile(n, 512)
    return pl.pallas_call(
        _mix_a_in_kernel,
        grid=(n // tm,),
        in_specs=[pl.BlockSpec((tm, d), lambda i: (i, 0)),
                  pl.BlockSpec((1, d), lambda i: (0, 0)),
                  pl.BlockSpec((d, 3 * d), lambda i: (0, 0))],
        out_specs=[pl.BlockSpec((tm, d), lambda i: (i, 0)),
                   pl.BlockSpec((tm, d), lambda i: (i, 0))],
        out_shape=[jax.ShapeDtypeStruct((n, d), BF16), jax.ShapeDtypeStruct((n, d), BF16)],
        compiler_params=_cparams("parallel"),
        name="mix_a_in",
    )(h, g, w_in)


def _mix_a_conv_kernel(v_ref, vp_ref, vn_ref, bg_ref, cw_ref, z_ref, *, tiles_per_seq):
    i = pl.program_id(0)
    tm = v_ref.shape[0]
    v = v_ref[...].astype(F32)
    at_start = (i % tiles_per_seq) == 0
    at_end = (i % tiles_per_seq) == tiles_per_seq - 1
    prev_row = vp_ref[...].astype(F32)[BF16_ROWS - 1:BF16_ROWS, :]
    next_row = vn_ref[...].astype(F32)[0:1, :]
    prev_row = jnp.where(at_start, 0.0, prev_row)
    next_row = jnp.where(at_end, 0.0, next_row)
    row = lax.broadcasted_iota(I32, (tm, 1), 0)
    v_dn = jnp.where(row == 0, prev_row, pltpu.roll(v, 1, 0))
    v_up = jnp.where(row == tm - 1, next_row, pltpu.roll(v, tm - 1, 0))
    cw = cw_ref[...]
    u = cw[0:1, :] * v_dn + cw[1:2, :] * v + cw[2:3, :] * v_up
    z_ref[...] = (bg_ref[...].astype(F32) * u).astype(BF16)


def _mix_a_conv(v, bg, conv_w, seq):
    n, d = v.shape
    tm = _tile(seq, 512)
    hb = tm // BF16_ROWS
    nhalo = n // BF16_ROWS
    return pl.pallas_call(
        functools.partial(_mix_a_conv_kernel, tiles_per_seq=seq // tm),
        grid=(n // tm,),
        in_specs=[pl.BlockSpec((tm, d), lambda i: (i, 0)),
                  pl.BlockSpec((BF16_ROWS, d), lambda i: (jnp.maximum(i * hb - 1, 0), 0)),
                  pl.BlockSpec((BF16_ROWS, d), lambda i: (jnp.minimum((i + 1) * hb, nhalo - 1), 0)),
                  pl.BlockSpec((tm, d), lambda i: (i, 0)),
                  pl.BlockSpec(conv_w.shape, lambda i: (0, 0))],
        out_specs=pl.BlockSpec((tm, d), lambda i: (i, 0)),
        out_shape=jax.ShapeDtypeStruct((n, d), BF16),
        compiler_params=_cparams("parallel"),
        name="mix_a_conv",
    )(v, v, v, bg, conv_w)


def _mix_b_in_kernel(h_ref, hp_ref, hn_ref, g_ref, w_ref, yb_ref, u_ref, *, steps_per_seq, tt):
    i = pl.program_id(0)
    r = yb_ref.shape[1]
    tiles = h_ref.shape[0] // tt
    g = g_ref[...]
    xn = _rms(h_ref[...], g).astype(BF16)
    yb_ref[...] = jax.nn.gelu(_dot(xn, w_ref[:, 0:r])).astype(BF16)
    u = _dot(xn, w_ref[:, r:2 * r])
    xh = _rms(jnp.concatenate([hp_ref[...], hn_ref[...]], axis=0), g).astype(BF16)
    uh = _dot(xh, w_ref[:, r:2 * r])
    at_start = (i % steps_per_seq) == 0
    at_end = (i % steps_per_seq) == steps_per_seq - 1
    prev2 = jnp.where(at_start, 0.0, uh[SUBLANES - 2:SUBLANES, :])
    next1 = jnp.where(at_end, 0.0, uh[SUBLANES:SUBLANES + 1, :])
    row = lax.broadcasted_iota(I32, (SUBLANES, 1), 0)
    for k in range(tiles):
        p2 = prev2 if k == 0 else u[k * tt - 2:k * tt]
        n1 = next1 if k == tiles - 1 else u[(k + 1) * tt:(k + 1) * tt + 1]
        pad = jnp.where(row == 0, p2[0:1], jnp.where(row == 1, p2[1:2], jnp.where(row == 2, n1, 0.0)))
        for s in range(r // LANES):
            lanes = slice(s * LANES, (s + 1) * LANES)
            u_ref[k, s, 0:tt, :] = u[k * tt:(k + 1) * tt, lanes]
            u_ref[k, s, tt:tt + SUBLANES, :] = pad[:, lanes]


def _mix_b_in(h, g, w_in, batch, seq, tt):
    n, d = h.shape
    r = w_in.shape[1] // 2
    tm = _tile(seq, 512)
    steps_per_seq = seq // tm
    tiles = tm // tt
    hb = tm // SUBLANES
    nhalo = n // SUBLANES
    return pl.pallas_call(
        functools.partial(_mix_b_in_kernel, steps_per_seq=steps_per_seq, tt=tt),
        grid=(n // tm,),
        in_specs=[pl.BlockSpec((tm, d), lambda i: (i, 0)),
                  pl.BlockSpec((SUBLANES, d), lambda i: (jnp.maximum(i * hb - 1, 0), 0)),
                  pl.BlockSpec((SUBLANES, d), lambda i: (jnp.minimum((i + 1) * hb, nhalo - 1), 0)),
                  pl.BlockSpec((1, d), lambda i: (0, 0)),
                  pl.BlockSpec((d, 2 * r), lambda i: (0, 0))],
        out_specs=[pl.BlockSpec((tm, r), lambda i: (i, 0)),
                   pl.BlockSpec((tiles, r // LANES, None, tt + SUBLANES, LANES),
                                lambda i: (i % steps_per_seq, 0, i // steps_per_seq, 0, 0))],
        out_shape=[jax.ShapeDtypeStruct((n, r), BF16),
                   jax.ShapeDtypeStruct((seq // tt, r // LANES, batch, tt + SUBLANES, LANES), F32)],
        compiler_params=_cparams("parallel"),
        name="mix_b_in",
    )(h, h, h, g, w_in)


def _rglru_pass_kernel(*refs, reverse, tt):
    if reverse:
        (u_ref, cw_ref, cb_ref, wr_ref, br_ref, wi_ref, bi_ref, lam_ref, hf_ref, yb_ref,
         out_ref, x_s, a_s, g_s, carry_s, nat_s) = refs
    else:
        (u_ref, cw_ref, cb_ref, wr_ref, br_ref, wi_ref, bi_ref, lam_ref,
         out_ref, x_s, a_s, g_s, carry_s) = refs
    nb = SUBLANES
    ttp = tt + SUBLANES
    rows = tt * nb
    slabs = u_ref.shape[0]

    @pl.when(pl.program_id(1) == 0)
    def _():
        carry_s[...] = jnp.zeros_like(carry_s)

    def gather_t(tl, dst):
        for s in range(slabs):
            x_s[pl.ds(dst, nb), s * LANES:(s + 1) * LANES] = u_ref[s, pl.ds(tl, nb, stride=ttp), :]

    def load_t(tl, c):
        gather_t(tl, pl.multiple_of((tl + 2) * nb, nb))
        return c

    lax.fori_loop(0, tt, load_t, 0, unroll=8)
    gather_t(tt, 0)
    gather_t(tt + 1, nb)
    gather_t(tt + 2, (tt + 2) * nb)

    cw = cw_ref[...]
    u = (cw[0:1, :] * x_s[0:rows, :] + cw[1:2, :] * x_s[nb:nb + rows, :]
         + cw[2:3, :] * x_s[2 * nb:2 * nb + rows, :] + cw[3:4, :] * x_s[3 * nb:3 * nb + rows, :]
         + cb_ref[...])
    ub = u.astype(BF16)
    r = 0.5 * (1.0 + jnp.tanh(0.5 * (_dot(ub, wr_ref[...]) + br_ref[...])))
    ig = 0.5 * (1.0 + jnp.tanh(0.5 * (_dot(ub, wi_ref[...]) + bi_ref[...])))
    log_a = -RG_C * r * jax.nn.softplus(-lam_ref[...])
    th = jnp.tanh(0.5 * log_a)
    q = 1.0 / (1.0 - th)
    a_s[...] = (1.0 + th) * q
    m2 = -4.0 * th * q * q
    g_s[...] = jnp.where(m2 > 0.0, m2 * lax.rsqrt(m2), 0.0) * (ig * u)

    def step(k, h):
        tl = tt - 1 - k if reverse else k
        base = pl.multiple_of(tl * nb, nb)
        h = a_s[pl.ds(base, nb), :] * h + g_s[pl.ds(base, nb), :]
        if reverse:
            hs = h + hf_ref[pl.ds(base, nb), :]
            for s in range(slabs):
                nat_s[s, pl.ds(tl, nb, stride=ttp), :] = hs[:, s * LANES:(s + 1) * LANES]
        else:
            out_ref[pl.ds(base, nb), :] = h
        return h

    carry_s[...] = lax.fori_loop(0, tt, step, carry_s[...], unroll=8)
    if reverse:
        for b in range(nb):
            for s in range(slabs):
                lanes = slice(s * LANES, (s + 1) * LANES)
                hb = nat_s[s, b * ttp:b * ttp + tt, :]
                out_ref[b, :, lanes] = (yb_ref[b, :, lanes].astype(F32) * hb).astype(BF16)


def _rglru_pass(u_tiles, conv_w, conv_b, w_r, b_r, w_i, b_i, lam, hf, yb, seq, tt, reverse):
    nt, nslab, batch, ttp, _ = u_tiles.shape
    r = nslab * LANES
    assert batch == SUBLANES, "time-major tile layout puts the batch on the 8 sublanes"
    nh, bk = w_r.shape[1], w_r.shape[2]
    slabs = bk // LANES
    kw = conv_w.shape[0]
    d = 1 if reverse else 0
    rows = tt * batch

    def tile(jj):
        return nt - 1 - jj if reverse else jj

    vec = pl.BlockSpec((None, 1, bk), lambda h, jj: (d, 0, h))
    mat = pl.BlockSpec((None, None, bk, bk), lambda h, jj: (d, h, 0, 0))
    in_specs = [pl.BlockSpec((None, slabs, batch * ttp, LANES), lambda h, jj: (tile(jj), h, 0, 0)),
                pl.BlockSpec((kw, bk), lambda h, jj: (0, h)),
                pl.BlockSpec((1, bk), lambda h, jj: (0, h)),
                mat, vec, mat, vec, vec]
    args = [u_tiles.reshape(nt, nslab, batch * ttp, LANES), conv_w, conv_b, w_r, b_r.reshape(2, 1, r),
            w_i, b_i.reshape(2, 1, r), lam.reshape(2, 1, r)]
    scratch = [pltpu.VMEM(((tt + 3) * batch, bk), F32), pltpu.VMEM((rows, bk), F32),
               pltpu.VMEM((rows, bk), F32), pltpu.VMEM((batch, bk), F32)]
    if reverse:
        in_specs += [pl.BlockSpec((rows, bk), lambda h, jj: (tile(jj), h)),
                     pl.BlockSpec((batch, tt, bk), lambda h, jj: (0, tile(jj), h))]
        args += [hf, yb.reshape(batch, seq, r)]
        scratch += [pltpu.VMEM((slabs, batch * ttp, LANES), F32)]
        out_spec = pl.BlockSpec((batch, tt, bk), lambda h, jj: (0, tile(jj), h))
        out_shape = jax.ShapeDtypeStruct((batch, seq, r), BF16)
    else:
        out_spec = pl.BlockSpec((rows, bk), lambda h, jj: (tile(jj), h))
        out_shape = jax.ShapeDtypeStruct((seq * batch, r), F32)
    return pl.pallas_call(
        functools.partial(_rglru_pass_kernel, reverse=reverse, tt=tt),
        grid=(nh, nt),
        in_specs=in_specs,
        out_specs=out_spec,
        out_shape=out_shape,
        scratch_shapes=scratch,
        compiler_params=_cparams("parallel", "arbitrary"),
        name="rglru_rev" if reverse else "rglru_fwd",
    )(*args)


def _rglru(u_tiles, yb, conv_w, conv_b, w_r, b_r, w_i, b_i, lam, seq, tt):
    hf = _rglru_pass(u_tiles, conv_w, conv_b, w_r, b_r, w_i, b_i, lam, None, None, seq, tt, False)
    z = _rglru_pass(u_tiles, conv_w, conv_b, w_r, b_r, w_i, b_i, lam, hf, yb, seq, tt, True)
    return z.reshape(-1, z.shape[2])


def _post_mix_kernel(z_ref, wo_ref, h_ref, g_ref, wr_ref, br_ref,
                     h1_ref, xn_ref, meta_ref, meta_t_ref, cnt_ref, carry_ref):
    i = pl.program_id(0)
    tm = z_ref.shape[0]

    @pl.when(i == 0)
    def _():
        carry_ref[...] = jnp.zeros_like(carry_ref)

    h1 = h_ref[...] + _dot(z_ref[...], wo_ref[...])
    h1_ref[...] = h1
    xn = _rms(h1, g_ref[...])
    xn_ref[...] = _pack_bf16_pairs(xn)

    x_hi = xn.astype(BF16)
    x_lo = (xn - x_hi.astype(F32)).astype(BF16)
    w = wr_ref[...]
    w_hi = w.astype(BF16)
    w_lo = (w - w_hi.astype(F32)).astype(BF16)
    logits = _dot(x_hi, w_hi) + _dot(x_hi, w_lo) + _dot(x_lo, w_hi) + br_ref[...]

    lane = lax.broadcasted_iota(I32, (tm, LANES), 1).astype(F32)
    neg = jnp.float32(-jnp.inf)
    nolane = jnp.float32(LANES)
    gmask = lane < N_GROUPS
    gmax = jnp.max(jnp.where(gmask, logits, neg), axis=1, keepdims=True)
    eg = jnp.where(gmask, jnp.exp(logits - gmax), 0.0)
    gsum = jnp.sum(eg, axis=1, keepdims=True)
    pg = eg / gsum
    pg_top = 1.0 / gsum
    g_top = jnp.min(jnp.where(gmask & (pg == pg_top), lane, nolane), axis=1, keepdims=True)
    lo = EXPERT_LANE0 + g_top * EXPERTS_PER_GROUP
    emask = (lane >= lo) & (lane < lo + EXPERTS_PER_GROUP)
    emax = jnp.max(jnp.where(emask, logits, neg), axis=1, keepdims=True)
    ee = jnp.where(emask, jnp.exp(logits - emax), 0.0)
    esum = jnp.sum(ee, axis=1, keepdims=True)
    pe = jnp.where(emask, ee / esum, -1.0)
    p1 = 1.0 / esum
    i1 = jnp.min(jnp.where(pe == p1, lane, nolane), axis=1, keepdims=True)
    pe2 = jnp.where(lane == i1, -1.0, pe)
    p2 = jnp.max(pe2, axis=1, keepdims=True)
    i2 = jnp.min(jnp.where(pe2 == p2, lane, nolane), axis=1, keepdims=True)
    psum = p1 + p2
    w0 = pg_top * (p1 / psum)
    w1 = pg_top * (p2 / psum)

    oh0 = lane == i1
    oh1 = lane == i2
    both = (oh0 | oh1).astype(BF16)
    ri = lax.broadcasted_iota(I32, (tm, tm), 0)
    ci = lax.broadcasted_iota(I32, (tm, tm), 1)
    before = (ri > ci).astype(BF16)
    cnt_before = _dot(before, both) + carry_ref[...]
    rank0 = jnp.sum(jnp.where(oh0, cnt_before, 0.0), axis=1, keepdims=True)
    rank1 = jnp.sum(jnp.where(oh1, cnt_before, 0.0), axis=1, keepdims=True)
    carry = carry_ref[...] + jnp.sum(both.astype(F32), axis=0, keepdims=True)
    carry_ref[...] = carry
    cnt_ref[...] = carry

    e0 = i1 - EXPERT_LANE0
    e1 = i2 - EXPERT_LANE0
    meta = jnp.zeros((tm, LANES), F32)
    for ln, val in ((M_E0, e0), (M_E1, e1), (M_R0, rank0), (M_R1, rank1), (M_W0, w0), (M_W1, w1)):
        meta = jnp.where(lane == ln, val, meta)
    meta_ref[...] = meta
    meta_t_ref[...] = meta.T[0:SUBLANES, :]


def _post_mix(z, w_out, h, g, w_router, b_router):
    n, d = h.shape
    k = z.shape[1]
    tm = _tile(n, 512)
    return pl.pallas_call(
        _post_mix_kernel,
        grid=(n // tm,),
        in_specs=[pl.BlockSpec((tm, k), lambda i: (i, 0)),
                  pl.BlockSpec((k, d), lambda i: (0, 0)),
                  pl.BlockSpec((tm, d), lambda i: (i, 0)),
                  pl.BlockSpec((1, d), lambda i: (0, 0)),
                  pl.BlockSpec((d, LANES), lambda i: (0, 0)),
                  pl.BlockSpec((1, LANES), lambda i: (0, 0))],
        out_specs=[pl.BlockSpec((tm, d), lambda i: (i, 0)),
                   pl.BlockSpec((tm, d // 2), lambda i: (i, 0)),
                   pl.BlockSpec((tm, LANES), lambda i: (i, 0)),
                   pl.BlockSpec((SUBLANES, tm), lambda i: (0, i)),
                   pl.BlockSpec((1, LANES), lambda i: (0, 0))],
        out_shape=[jax.ShapeDtypeStruct((n, d), F32), jax.ShapeDtypeStruct((n, d // 2), U32),
                   jax.ShapeDtypeStruct((n, LANES), F32), jax.ShapeDtypeStruct((SUBLANES, n), F32),
                   jax.ShapeDtypeStruct((1, LANES), F32)],
        scratch_shapes=[pltpu.VMEM((1, LANES), F32)],
        compiler_params=_cparams("arbitrary"),
        name="post_mix_router",
    )(z, w_out, h, g, w_router, b_router)


def _plan_kernel(cnt_ref, meta_t_ref, dest_ref, estart_ref, ecount_ref, nblk_ref, pstart_ref):
    def per_expert(e, acc):
        nb_e = (cnt_ref[e] + (ROUTE_ROWS - 1)) // ROUTE_ROWS
        pstart_ref[e] = acc * ROUTE_ROWS
        estart_ref[e] = acc
        ecount_ref[e] = nb_e
        return acc + nb_e

    nblk_ref[0] = lax.fori_loop(0, N_EXPERTS, per_expert, 0)

    e0 = meta_t_ref[M_E0:M_E0 + 1, :]
    e1 = meta_t_ref[M_E1:M_E1 + 1, :]
    d0 = meta_t_ref[M_R0:M_R0 + 1, :]
    d1 = meta_t_ref[M_R1:M_R1 + 1, :]
    for e in range(N_EXPERTS):
        ps = pstart_ref[e].astype(F32)
        d0 = d0 + jnp.where(e0 == e, ps, 0.0)
        d1 = d1 + jnp.where(e1 == e, ps, 0.0)
    dest_ref[...] = jnp.zeros_like(dest_ref)
    dest_ref[0:1, :] = d0.astype(I32)
    dest_ref[1:2, :] = d1.astype(I32)


def _plan(cnt_i32, meta_t):
    n = meta_t.shape[1]
    smem = pl.BlockSpec(memory_space=pltpu.SMEM)
    return pl.pallas_call(
        _plan_kernel,
        in_specs=[smem, pl.BlockSpec((SUBLANES, n), lambda: (0, 0))],
        out_specs=[pl.BlockSpec((SUBLANES, n), lambda: (0, 0)), smem, smem, smem],
        out_shape=[jax.ShapeDtypeStruct((SUBLANES, n), I32),
                   jax.ShapeDtypeStruct((N_EXPERTS,), I32),
                   jax.ShapeDtypeStruct((N_EXPERTS,), I32),
                   jax.ShapeDtypeStruct((1,), I32)],
        scratch_shapes=[pltpu.SMEM((N_EXPERTS,), I32)],
        name="dispatch_plan",
    )(cnt_i32, meta_t)


def _sc_mesh():
    return plsc.VectorSubcoreMesh(core_axis_name="c", subcore_axis_name="s",
                                  num_cores=SC_CORES, num_subcores=SC_SUBCORES)


def _sc_scatter_rows(x, idx0, idx1, p_rows):
    m, k = x.shape
    per_w = m // (SC_CORES * SC_SUBCORES)
    assert per_w % SC_CHUNK == 0 and per_w * SC_CORES * SC_SUBCORES == m

    @functools.partial(
        pl.kernel, mesh=_sc_mesh(), out_type=jax.ShapeDtypeStruct((p_rows, k), x.dtype),
        scratch_types=[pltpu.VMEM((SC_CHUNK,), I32), pltpu.VMEM((SC_CHUNK,), I32),
                       pltpu.VMEM((SC_CHUNK, k), x.dtype), pltpu.SemaphoreType.DMA],
        name="sc_scatter_rows")
    def scatter(x_hbm, i0_hbm, i1_hbm, out_hbm, i0_v, i1_v, rows_v, sem):
        base = (lax.axis_index("s") * SC_CORES + lax.axis_index("c")) * per_w

        @pl.loop(0, per_w // SC_CHUNK)
        def _(c):
            off = pl.multiple_of(base + c * SC_CHUNK, SC_CHUNK)
            pltpu.sync_copy(x_hbm.at[pl.ds(off, SC_CHUNK)], rows_v)
            pltpu.sync_copy(i0_hbm.at[pl.ds(off, SC_CHUNK)], i0_v)
            pltpu.sync_copy(i1_hbm.at[pl.ds(off, SC_CHUNK)], i1_v)
            pltpu.async_copy(rows_v, out_hbm.at[i0_v], sem).wait()
            pltpu.async_copy(rows_v, out_hbm.at[i1_v], sem).wait()

    return scatter(x, idx0, idx1)


def _expert_kernel(es_ref, ec_ref, nblk_ref, cnt_ref, xs_ref, wg_ref, wu_ref, wd_ref, ys_ref,
                   x_s, y_s, wg_s, wu_s, wd_s, sem):
    e = pl.program_id(0)
    nb = ec_ref[e]
    b0 = es_ref[e]
    k = x_s.shape[2]
    nb_max = ys_ref.shape[0] // ROUTE_ROWS

    def rows(b):
        return pl.ds(pl.multiple_of(b * ROUTE_ROWS, ROUTE_ROWS), ROUTE_ROWS)

    nblk = nblk_ref[0]

    def copy_in(b):
        slot = lax.rem(b, IN_SLOTS)
        return pltpu.make_async_copy(xs_ref.at[rows(b)], x_s.at[slot], sem.at[0, slot])

    def copy_out(b):
        slot = lax.rem(b, OUT_SLOTS)
        return pltpu.make_async_copy(y_s.at[slot], ys_ref.at[rows(b)], sem.at[1, slot])

    @pl.when(e == 0)
    def _():
        for b in range(IN_AHEAD):
            @pl.when(b < nblk)
            def _(b=b):
                copy_in(b).start()

    @pl.when(nb > 0)
    def _():
        wg_s[...] = wg_ref[...].astype(BF16)
        wu_s[...] = wu_ref[...].astype(BF16)
        wd_s[...] = wd_ref[...].astype(BF16)

        def block(j, c):
            b = b0 + j
            copy_in(b).wait()

            @pl.when(b + IN_AHEAD < nblk)
            def _():
                copy_in(b + IN_AHEAD).start()

            row = lax.broadcasted_iota(I32, (ROUTE_ROWS, 1), 0)
            live = row < cnt_ref[e] - j * ROUTE_ROWS
            xp = jnp.where(live, x_s[lax.rem(b, IN_SLOTS)], jnp.uint32(0))
            x_hi, x_lo = _unpack_bf16_pairs(xp)
            x_hi = x_hi.astype(BF16)
            x_lo = x_lo.astype(BF16)
            g = _dot(x_hi, wg_s[0:k, :]) + _dot(x_lo, wg_s[k:2 * k, :])
            u = _dot(x_hi, wu_s[0:k, :]) + _dot(x_lo, wu_s[k:2 * k, :])
            hb = (jax.nn.silu(g) * u).astype(BF16)
            y = _pack_bf16_pairs(_dot(hb, wd_s[...]))

            @pl.when(b >= OUT_SLOTS)
            def _():
                copy_out(b - OUT_SLOTS).wait()

            y_s[lax.rem(b, OUT_SLOTS)] = y
            copy_out(b).start()
            return c

        lax.fori_loop(0, nb, block, 0)

    @pl.when(e == pl.num_programs(0) - 1)
    def _():
        for back in range(OUT_SLOTS, 0, -1):
            @pl.when(nblk >= back)
            def _(back=back):
                copy_out(nblk - back).wait()

    @pl.when(e == pl.num_programs(0) - 1)
    def _():
        y_s[0] = jnp.zeros_like(y_s[0])

        def zero(b):
            return pltpu.make_async_copy(y_s.at[0], ys_ref.at[rows(b)], sem.at[1, 0])

        def start_zero(b, c):
            zero(b).start()
            return c

        def wait_zero(b, c):
            zero(b).wait()
            return c

        lax.fori_loop(nblk_ref[0], nb_max, start_zero, 0)
        lax.fori_loop(nblk_ref[0], nb_max, wait_zero, 0)


def _experts(estart, ecount, nblk, cnt, xs, w_gate, w_up, w_down, layer):
    p, k = xs.shape
    d, f = w_gate.shape[2], w_gate.shape[3]

    def w_map(e, es, ec, nb, cn):
        return (layer, e, 0, 0)

    grid_spec = pltpu.PrefetchScalarGridSpec(
        num_scalar_prefetch=4,
        grid=(N_EXPERTS,),
        in_specs=[pl.BlockSpec(memory_space=pl.ANY),
                  pl.BlockSpec((None, None, d, f), w_map),
                  pl.BlockSpec((None, None, d, f), w_map),
                  pl.BlockSpec((None, None, f, d), w_map)],
        out_specs=pl.BlockSpec(memory_space=pl.ANY),
        scratch_shapes=[pltpu.VMEM((IN_SLOTS, ROUTE_ROWS, k), U32),
                        pltpu.VMEM((OUT_SLOTS, ROUTE_ROWS, k), U32),
                        pltpu.VMEM((d, f), BF16), pltpu.VMEM((d, f), BF16), pltpu.VMEM((f, d), BF16),
                        pltpu.SemaphoreType.DMA((2, max(IN_SLOTS, OUT_SLOTS)))],
    )
    return pl.pallas_call(
        _expert_kernel,
        grid_spec=grid_spec,
        out_shape=jax.ShapeDtypeStruct((p, k), U32),
        compiler_params=_cparams("arbitrary"),
        name="expert_mlp",
    )(estart, ecount, nblk, cnt, xs, w_gate, w_up, w_down)


def _sc_gather_rows(table, idx):
    m = idx.shape[0]
    k = table.shape[1]
    per_w = m // (SC_CORES * SC_SUBCORES)
    assert per_w % SC_CHUNK == 0 and per_w * SC_CORES * SC_SUBCORES == m

    @functools.partial(
        pl.kernel, mesh=_sc_mesh(), out_type=jax.ShapeDtypeStruct((m, k), table.dtype),
        scratch_types=[pltpu.VMEM((SC_CHUNK,), I32), pltpu.VMEM((SC_CHUNK, k), table.dtype),
                       pltpu.SemaphoreType.DMA],
        name="sc_gather_rows")
    def gather(table_hbm, idx_hbm, out_hbm, idx_v, rows_v, sem):
        base = (lax.axis_index("s") * SC_CORES + lax.axis_index("c")) * per_w

        @pl.loop(0, per_w // SC_CHUNK)
        def _(c):
            off = pl.multiple_of(base + c * SC_CHUNK, SC_CHUNK)
            pltpu.sync_copy(idx_hbm.at[pl.ds(off, SC_CHUNK)], idx_v)
            pltpu.async_copy(table_hbm.at[idx_v], rows_v, sem).wait()
            pltpu.sync_copy(rows_v, out_hbm.at[pl.ds(off, SC_CHUNK)])

    return gather(table, idx)


def _combine_kernel(ya_ref, yb_ref, h1_ref, meta_ref, p_ref, gp_ref, wpg_ref, bpg_ref, wp_ref,
                    gf_ref, out_ref, *, final):
    meta = meta_ref[...]
    w0 = meta[:, M_W0:M_W0 + 1]
    w1 = meta[:, M_W1:M_W1 + 1]
    ya_hi, ya_lo = _unpack_bf16_pairs(ya_ref[...])
    yb_hi, yb_lo = _unpack_bf16_pairs(yb_ref[...])
    y = jnp.concatenate([ya_hi * w0 + yb_hi * w1, ya_lo * w0 + yb_lo * w1], axis=1)
    h2 = h1_ref[...] + y
    xn = _rms(h2, gp_ref[...]).astype(BF16)
    gate = jax.nn.sigmoid(_dot(xn, wpg_ref[...]) + bpg_ref[...])
    pp = _dot(p_ref[...].astype(BF16), wp_ref[...])
    h3 = h2 + gate * pp
    if final:
        h3 = _rms(h3, gf_ref[...])
    out_ref[...] = h3


def _combine(dest, ys, h1, meta, p, layer, g_ple, w_ple_gate, b_ple_gate, w_ple, g_final, final):
    n, d = h1.shape
    pd = p.shape[2]
    k = ys.shape[1]
    tmc = _tile(n, 512)
    ya = _sc_gather_rows(ys, dest[0])
    yb = _sc_gather_rows(ys, dest[1])
    vec = pl.BlockSpec((1, d), lambda i: (0, 0))
    return pl.pallas_call(
        functools.partial(_combine_kernel, final=final),
        grid=(n // tmc,),
        in_specs=[pl.BlockSpec((tmc, k), lambda i: (i, 0)),
                  pl.BlockSpec((tmc, k), lambda i: (i, 0)),
                  pl.BlockSpec((tmc, d), lambda i: (i, 0)),
                  pl.BlockSpec((tmc, LANES), lambda i: (i, 0)),
                  pl.BlockSpec((None, tmc, pd), lambda i: (layer, i, 0)),
                  vec,
                  pl.BlockSpec((d, d), lambda i: (0, 0)),
                  vec,
                  pl.BlockSpec((pd, d), lambda i: (0, 0)),
                  vec],
        out_specs=pl.BlockSpec((tmc, d), lambda i: (i, 0)),
        out_shape=jax.ShapeDtypeStruct((n, d), F32),
        compiler_params=_cparams("parallel"),
        name="combine_ple",
    )(ya, yb, h1, meta, p, g_ple, w_ple_gate, b_ple_gate, w_ple, g_final)


def _row(v):
    return v.reshape(1, -1)


def _moe_and_ple(z, w_out, h, i, final, norm_ffn, w_router_group, b_router_group, w_router_expert,
                 b_router_expert, w_exp_gate, w_exp_up, w_exp_down, norm_ple, w_ple, w_ple_gate,
                 b_ple_gate, norm_final, p):
    n, d = h.shape
    pad = LANES - N_GROUPS - N_EXPERTS
    w_router = jnp.concatenate(
        [w_router_group[i], w_router_expert[i], jnp.zeros((d, pad), F32)], axis=1)
    b_router = jnp.concatenate(
        [b_router_group[i], b_router_expert[i], jnp.zeros((pad,), F32)]).reshape(1, LANES)
    h1, xn, meta, meta_t, cnt = _post_mix(z, w_out, h, _row(norm_ffn[i]), w_router, b_router)
    cnt_i32 = cnt[0, EXPERT_LANE0:EXPERT_LANE0 + N_EXPERTS].astype(I32)
    nb_max = (2 * n) // ROUTE_ROWS + N_EXPERTS
    dest, estart, ecount, nblk = _plan(cnt_i32, meta_t)
    xs = _sc_scatter_rows(xn, dest[0], dest[1], nb_max * ROUTE_ROWS)
    ys = _experts(estart, ecount, nblk, cnt_i32, xs, w_exp_gate, w_exp_up, w_exp_down, i)
    return _combine(dest, ys, h1, meta, p.reshape(p.shape[0], n, -1), i, _row(norm_ple[i]),
                    w_ple_gate[i].astype(BF16), _row(b_ple_gate[i]), w_ple[i].astype(BF16),
                    _row(norm_final), final)


def kernel(x, p, norm_mix, w_in_a, conv_a, w_out_a, w_in_b, conv_b, conv_bias_b, w_rgate_b, b_rgate_b, w_igate_b, b_igate_b, lam_b, w_out_b, norm_ffn, w_router_group, b_router_group, w_router_expert, b_router_expert, w_exp_gate, w_exp_up, w_exp_down, norm_ple, w_ple, w_ple_gate, b_ple_gate, norm_final):
    batch, seq, d = x.shape
    depth = p.shape[0]
    n = batch * seq
    h = x.reshape(n, d)
    for i in range(depth):
        j = i // 2
        if i % 2 == 0:
            bg, v = _mix_a_in(h, _row(norm_mix[i]), w_in_a[j].astype(BF16))
            z = _mix_a_conv(v, bg, conv_a[j], seq)
            w_out = w_out_a[j]
        else:
            tt = _tile(seq, 256)
            yb, u_tiles = _mix_b_in(h, _row(norm_mix[i]), w_in_b[j].astype(BF16), batch, seq, tt)
            z = _rglru(u_tiles, yb, conv_b[j], _row(conv_bias_b[j]),
                       w_rgate_b[j].astype(BF16), b_rgate_b[j], w_igate_b[j].astype(BF16),
                       b_igate_b[j], lam_b[j], seq, tt)
            w_out = w_out_b[j]
        h = _moe_and_ple(z, w_out.astype(BF16), h, i, i == depth - 1, norm_ffn, w_router_group,
                         b_router_group, w_router_expert, b_router_expert, w_exp_gate, w_exp_up,
                         w_exp_down, norm_ple, w_ple, w_ple_gate, b_ple_gate, norm_final, p)
    return h.reshape(batch, seq, d)
```

```python
import functools

import jax
import jax.numpy as jnp
from jax import lax
from jax.experimental import pallas as pl
from jax.experimental.pallas import tpu as pltpu
from jax.experimental.pallas import tpu_sc as plsc

F32 = jnp.float32
BF16 = jnp.bfloat16
I32 = jnp.int32
U32 = jnp.uint32

EPS = 1e-6
N_GROUPS = 4
EXPERTS_PER_GROUP = 8
N_EXPERTS = N_GROUPS * EXPERTS_PER_GROUP
RG_C = 8.0

LANES = 128
SUBLANES = 8
BF16_ROWS = 16
EXPERT_LANE0 = N_GROUPS
ROUTE_ROWS = 256
IN_AHEAD = 4
IN_SLOTS = IN_AHEAD + 2
OUT_SLOTS = 4
SC_CORES = 2
SC_SUBCORES = 16
SC_CHUNK = 128
M_E0, M_E1, M_R0, M_R1, M_W0, M_W1 = 0, 1, 2, 3, 4, 5
VMEM_LIMIT = 48 * 1024 * 1024


def _cparams(*sem):
    return pltpu.CompilerParams(dimension_semantics=sem, vmem_limit_bytes=VMEM_LIMIT)


def _rms(x, g):
    return x * lax.rsqrt(jnp.mean(x * x, axis=-1, keepdims=True) + EPS) * g


def _dot(a, b):
    return jnp.dot(a, b, preferred_element_type=F32)


def _pack_bf16_pairs(x):
    k = x.shape[1] // 2
    hi = lax.bitcast_convert_type(x[:, :k].astype(BF16).astype(F32), U32)
    lo = lax.bitcast_convert_type(x[:, k:].astype(BF16).astype(F32), U32)
    return hi | (lo >> 16)


def _unpack_bf16_pairs(p):
    hi = lax.bitcast_convert_type(p & jnp.uint32(0xFFFF0000), F32)
    lo = lax.bitcast_convert_type(p << 16, F32)
    return hi, lo


def _tile(n, want):
    t = min(n, want)
    assert n % t == 0, (n, want)
    return t


def _mix_a_in_kernel(h_ref, g_ref, w_ref, bg_ref, v_ref):
    d = h_ref.shape[1]
    xn = _rms(h_ref[...], g_ref[...]).astype(BF16)
    bg = _dot(xn, w_ref[:, 0:d])
    cg = _dot(xn, w_ref[:, d:2 * d])
    hh = _dot(xn, w_ref[:, 2 * d:3 * d])
    bg_ref[...] = bg.astype(BF16)
    v_ref[...] = (cg * hh).astype(BF16)


def _mix_a_in(h, g, w_in):
    n, d = h.shape
    tm = _tile(n, 512)
    return pl.pallas_call(
        _mix_a_in_kernel,
        grid=(n // tm,),
        in_specs=[pl.BlockSpec((tm, d), lambda i: (i, 0)),
                  pl.BlockSpec((1, d), lambda i: (0, 0)),
                  pl.BlockSpec((d, 3 * d), lambda i: (0, 0))],
        out_specs=[pl.BlockSpec((tm, d), lambda i: (i, 0)),
                   pl.BlockSpec((tm, d), lambda i: (i, 0))],
        out_shape=[jax.ShapeDtypeStruct((n, d), BF16), jax.ShapeDtypeStruct((n, d), BF16)],
        compiler_params=_cparams("parallel"),
        name="mix_a_in",
    )(h, g, w_in)


def _mix_a_conv_kernel(v_ref, vp_ref, vn_ref, bg_ref, cw_ref, z_ref, *, tiles_per_seq):
    i = pl.program_id(0)
    tm = v_ref.shape[0]
    v = v_ref[...].astype(F32)
    at_start = (i % tiles_per_seq) == 0
    at_end = (i % tiles_per_seq) == tiles_per_seq - 1
    prev_row = vp_ref[...].astype(F32)[BF16_ROWS - 1:BF16_ROWS, :]
    next_row = vn_ref[...].astype(F32)[0:1, :]
    prev_row = jnp.where(at_start, 0.0, prev_row)
    next_row = jnp.where(at_end, 0.0, next_row)
    row = lax.broadcasted_iota(I32, (tm, 1), 0)
    v_dn = jnp.where(row == 0, prev_row, pltpu.roll(v, 1, 0))
    v_up = jnp.where(row == tm - 1, next_row, pltpu.roll(v, tm - 1, 0))
    cw = cw_ref[...]
    u = cw[0:1, :] * v_dn + cw[1:2, :] * v + cw[2:3, :] * v_up
    z_ref[...] = (bg_ref[...].astype(F32) * u).astype(BF16)


def _mix_a_conv(v, bg, conv_w, seq):
    n, d = v.shape
    tm = _tile(seq, 512)
    hb = tm // BF16_ROWS
    nhalo = n // BF16_ROWS
    return pl.pallas_call(
        functools.partial(_mix_a_conv_kernel, tiles_per_seq=seq // tm),
        grid=(n // tm,),
        in_specs=[pl.BlockSpec((tm, d), lambda i: (i, 0)),
                  pl.BlockSpec((BF16_ROWS, d), lambda i: (jnp.maximum(i * hb - 1, 0), 0)),
                  pl.BlockSpec((BF16_ROWS, d), lambda i: (jnp.minimum((i + 1) * hb, nhalo - 1), 0)),
                  pl.BlockSpec((tm, d), lambda i: (i, 0)),
                  pl.BlockSpec(conv_w.shape, lambda i: (0, 0))],
        out_specs=pl.BlockSpec((tm, d), lambda i: (i, 0)),
        out_shape=jax.ShapeDtypeStruct((n, d), BF16),
        compiler_params=_cparams("parallel"),
        name="mix_a_conv",
    )(v, v, v, bg, conv_w)


def _mix_b_in_kernel(h_ref, hp_ref, hn_ref, g_ref, w_ref, yb_ref, u_ref, *, steps_per_seq, tt):
    i = pl.program_id(0)
    r = yb_ref.shape[1]
    tiles = h_ref.shape[0] // tt
    g = g_ref[...]
    xn = _rms(h_ref[...], g).astype(BF16)
    yb_ref[...] = jax.nn.gelu(_dot(xn, w_ref[:, 0:r])).astype(BF16)
    u = _dot(xn, w_ref[:, r:2 * r])
    xh = _rms(jnp.concatenate([hp_ref[...], hn_ref[...]], axis=0), g).astype(BF16)
    uh = _dot(xh, w_ref[:, r:2 * r])
    at_start = (i % steps_per_seq) == 0
    at_end = (i % steps_per_seq) == steps_per_seq - 1
    prev2 = jnp.where(at_start, 0.0, uh[SUBLANES - 2:SUBLANES, :])
    next1 = jnp.where(at_end, 0.0, uh[SUBLANES:SUBLANES + 1, :])
    row = lax.broadcasted_iota(I32, (SUBLANES, 1), 0)
    for k in range(tiles):
        p2 = prev2 if k == 0 else u[k * tt - 2:k * tt]
        n1 = next1 if k == tiles - 1 else u[(k + 1) * tt:(k + 1) * tt + 1]
        pad = jnp.where(row == 0, p2[0:1], jnp.where(row == 1, p2[1:2], jnp.where(row == 2, n1, 0.0)))
        for s in range(r // LANES):
            lanes = slice(s * LANES, (s + 1) * LANES)
            u_ref[k, s, 0:tt, :] = u[k * tt:(k + 1) * tt, lanes]
            u_ref[k, s, tt:tt + SUBLANES, :] = pad[:, lanes]


def _mix_b_in(h, g, w_in, batch, seq, tt):
    n, d = h.shape
    r = w_in.shape[1] // 2
    tm = _tile(seq, 512)
    steps_per_seq = seq // tm
    tiles = tm // tt
    hb = tm // SUBLANES
    nhalo = n // SUBLANES
    return pl.pallas_call(
        functools.partial(_mix_b_in_kernel, steps_per_seq=steps_per_seq, tt=tt),
        grid=(n // tm,),
        in_specs=[pl.BlockSpec((tm, d), lambda i: (i, 0)),
                  pl.BlockSpec((SUBLANES, d), lambda i: (jnp.maximum(i * hb - 1, 0), 0)),
                  pl.BlockSpec((SUBLANES, d), lambda i: (jnp.minimum((i + 1) * hb, nhalo - 1), 0)),
                  pl.BlockSpec((1, d), lambda i: (0, 0)),
                  pl.BlockSpec((d, 2 * r), lambda i: (0, 0))],
        out_specs=[pl.BlockSpec((tm, r), lambda i: (i, 0)),
                   pl.BlockSpec((tiles, r // LANES, None, tt + SUBLANES, LANES),
                                lambda i: (i % steps_per_seq, 0, i // steps_per_seq, 0, 0))],
        out_shape=[jax.ShapeDtypeStruct((n, r), BF16),
                   jax.ShapeDtypeStruct((seq // tt, r // LANES, batch, tt + SUBLANES, LANES), F32)],
        compiler_params=_cparams("parallel"),
        name="mix_b_in",
    )(h, h, h, g, w_in)


def _rglru_gates(u, wr_ref, br_ref, wi_ref, bi_ref, lam_ref, a_s, g_s):
    ub = u.astype(BF16)
    r = 0.5 * (1.0 + jnp.tanh(0.5 * (_dot(ub, wr_ref[...]) + br_ref[...])))
    ig = 0.5 * (1.0 + jnp.tanh(0.5 * (_dot(ub, wi_ref[...]) + bi_ref[...])))
    log_a = -RG_C * r * jax.nn.softplus(-lam_ref[...])
    th = jnp.tanh(0.5 * log_a)
    q = 1.0 / (1.0 - th)
    a_s[...] = (1.0 + th) * q
    m2 = -4.0 * th * q * q
    g_s[...] = jnp.where(m2 > 0.0, m2 * lax.rsqrt(m2), 0.0) * (ig * u)


def _rglru_fwd_kernel(u_ref, cw_ref, cb_ref, wr_ref, br_ref, wi_ref, bi_ref, lam_ref,
                      hf_ref, uc_ref, x_s, a_s, g_s, carry_s, *, tt):
    nb = SUBLANES
    ttp = tt + SUBLANES
    rows = tt * nb
    slabs = u_ref.shape[0]

    @pl.when(pl.program_id(1) == 0)
    def _():
        carry_s[...] = jnp.zeros_like(carry_s)

    def gather_t(tl, dst):
        for s in range(slabs):
            x_s[pl.ds(dst, nb), s * LANES:(s + 1) * LANES] = u_ref[s, pl.ds(tl, nb, stride=ttp), :]

    def load_t(tl, c):
        gather_t(tl, pl.multiple_of((tl + 2) * nb, nb))
        return c

    lax.fori_loop(0, tt, load_t, 0, unroll=8)
    gather_t(tt, 0)
    gather_t(tt + 1, nb)
    gather_t(tt + 2, (tt + 2) * nb)

    cw = cw_ref[...]
    u = (cw[0:1, :] * x_s[0:rows, :] + cw[1:2, :] * x_s[nb:nb + rows, :]
         + cw[2:3, :] * x_s[2 * nb:2 * nb + rows, :] + cw[3:4, :] * x_s[3 * nb:3 * nb + rows, :]
         + cb_ref[...])
    uc_ref[...] = u
    _rglru_gates(u, wr_ref, br_ref, wi_ref, bi_ref, lam_ref, a_s, g_s)

    def step(tl, h):
        base = pl.multiple_of(tl * nb, nb)
        h = a_s[pl.ds(base, nb), :] * h + g_s[pl.ds(base, nb), :]
        hf_ref[pl.ds(base, nb), :] = h
        return h

    carry_s[...] = lax.fori_loop(0, tt, step, carry_s[...], unroll=8)


def _rglru_rev_kernel(uc_ref, wr_ref, br_ref, wi_ref, bi_ref, lam_ref, hf_ref, yb_ref,
                      z_ref, a_s, g_s, carry_s, nat_s, *, tt):
    nb = SUBLANES
    ttp = tt + SUBLANES
    slabs = nat_s.shape[0]

    @pl.when(pl.program_id(1) == 0)
    def _():
        carry_s[...] = jnp.zeros_like(carry_s)

    _rglru_gates(uc_ref[...], wr_ref, br_ref, wi_ref, bi_ref, lam_ref, a_s, g_s)

    def step(k, h):
        tl = tt - 1 - k
        base = pl.multiple_of(tl * nb, nb)
        h = a_s[pl.ds(base, nb), :] * h + g_s[pl.ds(base, nb), :]
        hs = h + hf_ref[pl.ds(base, nb), :]
        for s in range(slabs):
            nat_s[s, pl.ds(tl, nb, stride=ttp), :] = hs[:, s * LANES:(s + 1) * LANES]
        return h

    carry_s[...] = lax.fori_loop(0, tt, step, carry_s[...], unroll=8)
    for b in range(nb):
        for s in range(slabs):
            lanes = slice(s * LANES, (s + 1) * LANES)
            hb = nat_s[s, b * ttp:b * ttp + tt, :]
            z_ref[b, :, lanes] = (yb_ref[b, :, lanes].astype(F32) * hb).astype(BF16)


def _rglru(u_tiles, yb, conv_w, conv_b, w_r, b_r, w_i, b_i, lam, seq, tt):
    nt, nslab, batch, ttp, _ = u_tiles.shape
    r = nslab * LANES
    assert batch == SUBLANES, "time-major tile layout puts the batch on the 8 sublanes"
    nh, bk = w_r.shape[1], w_r.shape[2]
    slabs = bk // LANES
    kw = conv_w.shape[0]
    rows = tt * batch
    gates = [w_r, b_r.reshape(2, 1, r), w_i, b_i.reshape(2, 1, r), lam.reshape(2, 1, r)]

    def gate_specs(d):
        vec = pl.BlockSpec((None, 1, bk), lambda h, jj: (d, 0, h))
        mat = pl.BlockSpec((None, None, bk, bk), lambda h, jj: (d, h, 0, 0))
        return [mat, vec, mat, vec, vec]

    tm_spec = pl.BlockSpec((rows, bk), lambda h, jj: (jj, h))
    tm_rev = pl.BlockSpec((rows, bk), lambda h, jj: (nt - 1 - jj, h))
    tm_shape = jax.ShapeDtypeStruct((seq * batch, r), F32)
    state = [pltpu.VMEM((rows, bk), F32), pltpu.VMEM((rows, bk), F32), pltpu.VMEM((batch, bk), F32)]

    hf, uc = pl.pallas_call(
        functools.partial(_rglru_fwd_kernel, tt=tt),
        grid=(nh, nt),
        in_specs=[pl.BlockSpec((None, slabs, batch * ttp, LANES), lambda h, jj: (jj, h, 0, 0)),
                  pl.BlockSpec((kw, bk), lambda h, jj: (0, h)),
                  pl.BlockSpec((1, bk), lambda h, jj: (0, h))] + gate_specs(0),
        out_specs=[tm_spec, tm_spec],
        out_shape=[tm_shape, tm_shape],
        scratch_shapes=[pltpu.VMEM(((tt + 3) * batch, bk), F32)] + state,
        compiler_params=_cparams("parallel", "arbitrary"),
        name="rglru_fwd",
    )(u_tiles.reshape(nt, nslab, batch * ttp, LANES), conv_w, conv_b, *gates)

    nat_spec = pl.BlockSpec((batch, tt, bk), lambda h, jj: (0, nt - 1 - jj, h))
    z = pl.pallas_call(
        functools.partial(_rglru_rev_kernel, tt=tt),
        grid=(nh, nt),
        in_specs=[tm_rev] + gate_specs(1) + [tm_rev, nat_spec],
        out_specs=nat_spec,
        out_shape=jax.ShapeDtypeStruct((batch, seq, r), BF16),
        scratch_shapes=state + [pltpu.VMEM((slabs, batch * ttp, LANES), F32)],
        compiler_params=_cparams("parallel", "arbitrary"),
        name="rglru_rev",
    )(uc, *gates, hf, yb.reshape(batch, seq, r))
    return z.reshape(-1, r)


def _post_mix_kernel(z_ref, wo_ref, h_ref, g_ref, wr_ref, br_ref,
                     h1_ref, xn_ref, meta_ref, meta_t_ref, cnt_ref, carry_ref):
    i = pl.program_id(0)
    tm = z_ref.shape[0]

    @pl.when(i == 0)
    def _():
        carry_ref[...] = jnp.zeros_like(carry_ref)

    h1 = h_ref[...] + _dot(z_ref[...], wo_ref[...])
    h1_ref[...] = h1
    xn = _rms(h1, g_ref[...])
    xn_ref[...] = _pack_bf16_pairs(xn)

    logits = _dot(xn.astype(BF16), wr_ref[...].astype(BF16)) + br_ref[...]

    lane = lax.broadcasted_iota(I32, (tm, LANES), 1).astype(F32)
    neg = jnp.float32(-jnp.inf)
    nolane = jnp.float32(LANES)
    gmask = lane < N_GROUPS
    gmax = jnp.max(jnp.where(gmask, logits, neg), axis=1, keepdims=True)
    eg = jnp.where(gmask, jnp.exp(logits - gmax), 0.0)
    gsum = jnp.sum(eg, axis=1, keepdims=True)
    pg = eg / gsum
    pg_top = 1.0 / gsum
    g_top = jnp.min(jnp.where(gmask & (pg == pg_top), lane, nolane), axis=1, keepdims=True)
    lo = EXPERT_LANE0 + g_top * EXPERTS_PER_GROUP
    emask = (lane >= lo) & (lane < lo + EXPERTS_PER_GROUP)
    emax = jnp.max(jnp.where(emask, logits, neg), axis=1, keepdims=True)
    ee = jnp.where(emask, jnp.exp(logits - emax), 0.0)
    esum = jnp.sum(ee, axis=1, keepdims=True)
    pe = jnp.where(emask, ee / esum, -1.0)
    p1 = 1.0 / esum
    i1 = jnp.min(jnp.where(pe == p1, lane, nolane), axis=1, keepdims=True)
    pe2 = jnp.where(lane == i1, -1.0, pe)
    p2 = jnp.max(pe2, axis=1, keepdims=True)
    i2 = jnp.min(jnp.where(pe2 == p2, lane, nolane), axis=1, keepdims=True)
    psum = p1 + p2
    w0 = pg_top * (p1 / psum)
    w1 = pg_top * (p2 / psum)

    oh0 = lane == i1
    oh1 = lane == i2
    both = (oh0 | oh1).astype(BF16)
    ri = lax.broadcasted_iota(I32, (tm, tm), 0)
    ci = lax.broadcasted_iota(I32, (tm, tm), 1)
    before = (ri > ci).astype(BF16)
    cnt_before = _dot(before, both) + carry_ref[...]
    rank0 = jnp.sum(jnp.where(oh0, cnt_before, 0.0), axis=1, keepdims=True)
    rank1 = jnp.sum(jnp.where(oh1, cnt_before, 0.0), axis=1, keepdims=True)
    carry = carry_ref[...] + jnp.sum(both.astype(F32), axis=0, keepdims=True)
    carry_ref[...] = carry
    cnt_ref[...] = carry

    e0 = i1 - EXPERT_LANE0
    e1 = i2 - EXPERT_LANE0
    meta = jnp.zeros((tm, LANES), F32)
    for ln, val in ((M_E0, e0), (M_E1, e1), (M_R0, rank0), (M_R1, rank1), (M_W0, w0), (M_W1, w1)):
        meta = jnp.where(lane == ln, val, meta)
    meta_ref[...] = meta
    meta_t_ref[...] = meta.T[0:SUBLANES, :]


def _post_mix(z, w_out, h, g, w_router, b_router):
    n, d = h.shape
    k = z.shape[1]
    tm = _tile(n, 512)
    return pl.pallas_call(
        _post_mix_kernel,
        grid=(n // tm,),
        in_specs=[pl.BlockSpec((tm, k), lambda i: (i, 0)),
                  pl.BlockSpec((k, d), lambda i: (0, 0)),
                  pl.BlockSpec((tm, d), lambda i: (i, 0)),
                  pl.BlockSpec((1, d), lambda i: (0, 0)),
                  pl.BlockSpec((d, LANES), lambda i: (0, 0)),
                  pl.BlockSpec((1, LANES), lambda i: (0, 0))],
        out_specs=[pl.BlockSpec((tm, d), lambda i: (i, 0)),
                   pl.BlockSpec((tm, d // 2), lambda i: (i, 0)),
                   pl.BlockSpec((tm, LANES), lambda i: (i, 0)),
                   pl.BlockSpec((SUBLANES, tm), lambda i: (0, i)),
                   pl.BlockSpec((1, LANES), lambda i: (0, 0))],
        out_shape=[jax.ShapeDtypeStruct((n, d), F32), jax.ShapeDtypeStruct((n, d // 2), U32),
                   jax.ShapeDtypeStruct((n, LANES), F32), jax.ShapeDtypeStruct((SUBLANES, n), F32),
                   jax.ShapeDtypeStruct((1, LANES), F32)],
        scratch_shapes=[pltpu.VMEM((1, LANES), F32)],
        compiler_params=_cparams("arbitrary"),
        name="post_mix_router",
    )(z, w_out, h, g, w_router, b_router)


def _plan_kernel(cnt_ref, meta_t_ref, dest_ref, estart_ref, ecount_ref, nblk_ref, pstart_ref):
    def per_expert(e, acc):
        nb_e = (cnt_ref[e] + (ROUTE_ROWS - 1)) // ROUTE_ROWS
        pstart_ref[e] = acc * ROUTE_ROWS
        estart_ref[e] = acc
        ecount_ref[e] = nb_e
        return acc + nb_e

    nblk_ref[0] = lax.fori_loop(0, N_EXPERTS, per_expert, 0)

    e0 = meta_t_ref[M_E0:M_E0 + 1, :]
    e1 = meta_t_ref[M_E1:M_E1 + 1, :]
    d0 = meta_t_ref[M_R0:M_R0 + 1, :]
    d1 = meta_t_ref[M_R1:M_R1 + 1, :]
    for e in range(N_EXPERTS):
        ps = pstart_ref[e].astype(F32)
        d0 = d0 + jnp.where(e0 == e, ps, 0.0)
        d1 = d1 + jnp.where(e1 == e, ps, 0.0)
    dest_ref[...] = jnp.zeros_like(dest_ref)
    dest_ref[0:1, :] = d0.astype(I32)
    dest_ref[1:2, :] = d1.astype(I32)


def _plan(cnt_i32, meta_t):
    n = meta_t.shape[1]
    smem = pl.BlockSpec(memory_space=pltpu.SMEM)
    return pl.pallas_call(
        _plan_kernel,
        in_specs=[smem, pl.BlockSpec((SUBLANES, n), lambda: (0, 0))],
        out_specs=[pl.BlockSpec((SUBLANES, n), lambda: (0, 0)), smem, smem, smem],
        out_shape=[jax.ShapeDtypeStruct((SUBLANES, n), I32),
                   jax.ShapeDtypeStruct((N_EXPERTS,), I32),
                   jax.ShapeDtypeStruct((N_EXPERTS,), I32),
                   jax.ShapeDtypeStruct((1,), I32)],
        scratch_shapes=[pltpu.SMEM((N_EXPERTS,), I32)],
        name="dispatch_plan",
    )(cnt_i32, meta_t)


def _sc_mesh():
    return plsc.VectorSubcoreMesh(core_axis_name="c", subcore_axis_name="s",
                                  num_cores=SC_CORES, num_subcores=SC_SUBCORES)


def _sc_scatter_rows(x, idx0, idx1, p_rows):
    m, k = x.shape
    per_w = m // (SC_CORES * SC_SUBCORES)
    assert per_w % SC_CHUNK == 0 and per_w * SC_CORES * SC_SUBCORES == m

    @functools.partial(
        pl.kernel, mesh=_sc_mesh(), out_type=jax.ShapeDtypeStruct((p_rows, k), x.dtype),
        scratch_types=[pltpu.VMEM((SC_CHUNK,), I32), pltpu.VMEM((SC_CHUNK,), I32),
                       pltpu.VMEM((SC_CHUNK, k), x.dtype), pltpu.SemaphoreType.DMA],
        name="sc_scatter_rows")
    def scatter(x_hbm, i0_hbm, i1_hbm, out_hbm, i0_v, i1_v, rows_v, sem):
        base = (lax.axis_index("s") * SC_CORES + lax.axis_index("c")) * per_w

        @pl.loop(0, per_w // SC_CHUNK)
        def _(c):
            off = pl.multiple_of(base + c * SC_CHUNK, SC_CHUNK)
            pltpu.sync_copy(x_hbm.at[pl.ds(off, SC_CHUNK)], rows_v)
            pltpu.sync_copy(i0_hbm.at[pl.ds(off, SC_CHUNK)], i0_v)
            pltpu.sync_copy(i1_hbm.at[pl.ds(off, SC_CHUNK)], i1_v)
            pltpu.async_copy(rows_v, out_hbm.at[i0_v], sem).wait()
            pltpu.async_copy(rows_v, out_hbm.at[i1_v], sem).wait()

    return scatter(x, idx0, idx1)


def _expert_kernel(es_ref, ec_ref, nblk_ref, cnt_ref, xs_ref, wg_ref, wu_ref, wd_ref, ys_ref,
                   x_s, y_s, wg_s, wu_s, wd_s, sem):
    e = pl.program_id(0)
    nb = ec_ref[e]
    b0 = es_ref[e]
    k = x_s.shape[2]
    nb_max = ys_ref.shape[0] // ROUTE_ROWS

    def rows(b):
        return pl.ds(pl.multiple_of(b * ROUTE_ROWS, ROUTE_ROWS), ROUTE_ROWS)

    nblk = nblk_ref[0]

    def copy_in(b):
        slot = lax.rem(b, IN_SLOTS)
        return pltpu.make_async_copy(xs_ref.at[rows(b)], x_s.at[slot], sem.at[0, slot])

    def copy_out(b):
        slot = lax.rem(b, OUT_SLOTS)
        return pltpu.make_async_copy(y_s.at[slot], ys_ref.at[rows(b)], sem.at[1, slot])

    @pl.when(e == 0)
    def _():
        for b in range(IN_AHEAD):
            @pl.when(b < nblk)
            def _(b=b):
                copy_in(b).start()

    @pl.when(nb > 0)
    def _():
        wg_s[...] = wg_ref[...].astype(BF16)
        wu_s[...] = wu_ref[...].astype(BF16)
        wd_s[...] = wd_ref[...].astype(BF16)

        def mlp(b, j):
            row = lax.broadcasted_iota(I32, (ROUTE_ROWS, 1), 0)
            live = row < cnt_ref[e] - j * ROUTE_ROWS
            xp = jnp.where(live, x_s[lax.rem(b, IN_SLOTS)], jnp.uint32(0))
            x_hi, x_lo = _unpack_bf16_pairs(xp)
            x_hi = x_hi.astype(BF16)
            x_lo = x_lo.astype(BF16)
            g = _dot(x_hi, wg_s[0:k, :]) + _dot(x_lo, wg_s[k:2 * k, :])
            u = _dot(x_hi, wu_s[0:k, :]) + _dot(x_lo, wu_s[k:2 * k, :])
            hb = (jax.nn.silu(g) * u).astype(BF16)
            return _pack_bf16_pairs(_dot(hb, wd_s[...]))

        def unit(j, width):
            b = b0 + j
            for i in range(width):
                copy_in(b + i).wait()
            for i in range(width):
                @pl.when(b + i + IN_AHEAD < nblk)
                def _(i=i):
                    copy_in(b + i + IN_AHEAD).start()
            ys = [mlp(b + i, j + i) for i in range(width)]
            for i in range(width):
                @pl.when(b + i >= OUT_SLOTS)
                def _(i=i):
                    copy_out(b + i - OUT_SLOTS).wait()
            for i in range(width):
                y_s[lax.rem(b + i, OUT_SLOTS)] = ys[i]
                copy_out(b + i).start()

        def pair(jj, c):
            unit(2 * jj, 2)
            return c

        lax.fori_loop(0, nb // 2, pair, 0)

        @pl.when(nb % 2 == 1)
        def _():
            unit(nb - 1, 1)

    @pl.when(e == pl.num_programs(0) - 1)
    def _():
        for back in range(OUT_SLOTS, 0, -1):
            @pl.when(nblk >= back)
            def _(back=back):
                copy_out(nblk - back).wait()

    @pl.when(e == pl.num_programs(0) - 1)
    def _():
        y_s[0] = jnp.zeros_like(y_s[0])

        def zero(b):
            return pltpu.make_async_copy(y_s.at[0], ys_ref.at[rows(b)], sem.at[1, 0])

        def start_zero(b, c):
            zero(b).start()
            return c

        def wait_zero(b, c):
            zero(b).wait()
            return c

        lax.fori_loop(nblk_ref[0], nb_max, start_zero, 0)
        lax.fori_loop(nblk_ref[0], nb_max, wait_zero, 0)


def _experts(estart, ecount, nblk, cnt, xs, w_gate, w_up, w_down, layer):
    p, k = xs.shape
    d, f = w_gate.shape[2], w_gate.shape[3]

    def w_map(e, es, ec, nb, cn):
        return (layer, e, 0, 0)

    grid_spec = pltpu.PrefetchScalarGridSpec(
        num_scalar_prefetch=4,
        grid=(N_EXPERTS,),
        in_specs=[pl.BlockSpec(memory_space=pl.ANY),
                  pl.BlockSpec((None, None, d, f), w_map),
                  pl.BlockSpec((None, None, d, f), w_map),
                  pl.BlockSpec((None, None, f, d), w_map)],
        out_specs=pl.BlockSpec(memory_space=pl.ANY),
        scratch_shapes=[pltpu.VMEM((IN_SLOTS, ROUTE_ROWS, k), U32),
                        pltpu.VMEM((OUT_SLOTS, ROUTE_ROWS, k), U32),
                        pltpu.VMEM((d, f), BF16), pltpu.VMEM((d, f), BF16), pltpu.VMEM((f, d), BF16),
                        pltpu.SemaphoreType.DMA((2, max(IN_SLOTS, OUT_SLOTS)))],
    )
    return pl.pallas_call(
        _expert_kernel,
        grid_spec=grid_spec,
        out_shape=jax.ShapeDtypeStruct((p, k), U32),
        compiler_params=_cparams("arbitrary"),
        name="expert_mlp",
    )(estart, ecount, nblk, cnt, xs, w_gate, w_up, w_down)


def _sc_gather_rows(table, idx):
    m = idx.shape[0]
    k = table.shape[1]
    per_w = m // (SC_CORES * SC_SUBCORES)
    assert per_w % SC_CHUNK == 0 and per_w * SC_CORES * SC_SUBCORES == m

    @functools.partial(
        pl.kernel, mesh=_sc_mesh(), out_type=jax.ShapeDtypeStruct((m, k), table.dtype),
        scratch_types=[pltpu.VMEM((SC_CHUNK,), I32), pltpu.VMEM((SC_CHUNK, k), table.dtype),
                       pltpu.SemaphoreType.DMA],
        name="sc_gather_rows")
    def gather(table_hbm, idx_hbm, out_hbm, idx_v, rows_v, sem):
        base = (lax.axis_index("s") * SC_CORES + lax.axis_index("c")) * per_w

        @pl.loop(0, per_w // SC_CHUNK)
        def _(c):
            off = pl.multiple_of(base + c * SC_CHUNK, SC_CHUNK)
            pltpu.sync_copy(idx_hbm.at[pl.ds(off, SC_CHUNK)], idx_v)
            pltpu.async_copy(table_hbm.at[idx_v], rows_v, sem).wait()
            pltpu.sync_copy(rows_v, out_hbm.at[pl.ds(off, SC_CHUNK)])

    return gather(table, idx)


def _combine_kernel(ya_ref, yb_ref, h1_ref, meta_ref, p_ref, gp_ref, wpg_ref, bpg_ref, wp_ref,
                    gf_ref, out_ref, *, final):
    meta = meta_ref[...]
    w0 = meta[:, M_W0:M_W0 + 1]
    w1 = meta[:, M_W1:M_W1 + 1]
    ya_hi, ya_lo = _unpack_bf16_pairs(ya_ref[...])
    yb_hi, yb_lo = _unpack_bf16_pairs(yb_ref[...])
    y = jnp.concatenate([ya_hi * w0 + yb_hi * w1, ya_lo * w0 + yb_lo * w1], axis=1)
    h2 = h1_ref[...] + y
    xn = _rms(h2, gp_ref[...]).astype(BF16)
    gate = jax.nn.sigmoid(_dot(xn, wpg_ref[...]) + bpg_ref[...])
    pp = _dot(p_ref[...].astype(BF16), wp_ref[...])
    h3 = h2 + gate * pp
    if final:
        h3 = _rms(h3, gf_ref[...])
    out_ref[...] = h3


def _combine(dest, ys, h1, meta, p, layer, g_ple, w_ple_gate, b_ple_gate, w_ple, g_final, final):
    n, d = h1.shape
    pd = p.shape[2]
    k = ys.shape[1]
    tmc = _tile(n, 512)
    ya = _sc_gather_rows(ys, dest[0])
    yb = _sc_gather_rows(ys, dest[1])
    vec = pl.BlockSpec((1, d), lambda i: (0, 0))
    return pl.pallas_call(
        functools.partial(_combine_kernel, final=final),
        grid=(n // tmc,),
        in_specs=[pl.BlockSpec((tmc, k), lambda i: (i, 0)),
                  pl.BlockSpec((tmc, k), lambda i: (i, 0)),
                  pl.BlockSpec((tmc, d), lambda i: (i, 0)),
                  pl.BlockSpec((tmc, LANES), lambda i: (i, 0)),
                  pl.BlockSpec((None, tmc, pd), lambda i: (layer, i, 0)),
                  vec,
                  pl.BlockSpec((d, d), lambda i: (0, 0)),
                  vec,
                  pl.BlockSpec((pd, d), lambda i: (0, 0)),
                  vec],
        out_specs=pl.BlockSpec((tmc, d), lambda i: (i, 0)),
        out_shape=jax.ShapeDtypeStruct((n, d), F32),
        compiler_params=_cparams("parallel"),
        name="combine_ple",
    )(ya, yb, h1, meta, p, g_ple, w_ple_gate, b_ple_gate, w_ple, g_final)


def _row(v):
    return v.reshape(1, -1)


def _moe_and_ple(z, w_out, h, i, final, norm_ffn, w_router_group, b_router_group, w_router_expert,
                 b_router_expert, w_exp_gate, w_exp_up, w_exp_down, norm_ple, w_ple, w_ple_gate,
                 b_ple_gate, norm_final, p):
    n, d = h.shape
    pad = LANES - N_GROUPS - N_EXPERTS
    w_router = jnp.concatenate(
        [w_router_group[i], w_router_expert[i], jnp.zeros((d, pad), F32)], axis=1)
    b_router = jnp.concatenate(
        [b_router_group[i], b_router_expert[i], jnp.zeros((pad,), F32)]).reshape(1, LANES)
    h1, xn, meta, meta_t, cnt = _post_mix(z, w_out, h, _row(norm_ffn[i]), w_router, b_router)
    cnt_i32 = cnt[0, EXPERT_LANE0:EXPERT_LANE0 + N_EXPERTS].astype(I32)
    nb_max = (2 * n) // ROUTE_ROWS + N_EXPERTS
    dest, estart, ecount, nblk = _plan(cnt_i32, meta_t)
    xs = _sc_scatter_rows(xn, dest[0], dest[1], nb_max * ROUTE_ROWS)
    ys = _experts(estart, ecount, nblk, cnt_i32, xs, w_exp_gate, w_exp_up, w_exp_down, i)
    return _combine(dest, ys, h1, meta, p.reshape(p.shape[0], n, -1), i, _row(norm_ple[i]),
                    w_ple_gate[i].astype(BF16), _row(b_ple_gate[i]), w_ple[i].astype(BF16),
                    _row(norm_final), final)


def kernel(x, p, norm_mix, w_in_a, conv_a, w_out_a, w_in_b, conv_b, conv_bias_b, w_rgate_b, b_rgate_b, w_igate_b, b_igate_b, lam_b, w_out_b, norm_ffn, w_router_group, b_router_group, w_router_expert, b_router_expert, w_exp_gate, w_exp_up, w_exp_down, norm_ple, w_ple, w_ple_gate, b_ple_gate, norm_final):
    batch, seq, d = x.shape
    depth = p.shape[0]
    n = batch * seq
    h = x.reshape(n, d)
    for i in range(depth):
        j = i // 2
        if i % 2 == 0:
            bg, v = _mix_a_in(h, _row(norm_mix[i]), w_in_a[j].astype(BF16))
            z = _mix_a_conv(v, bg, conv_a[j], seq)
            w_out = w_out_a[j]
        else:
            tt = _tile(seq, 256)
            yb, u_tiles = _mix_b_in(h, _row(norm_mix[i]), w_in_b[j].astype(BF16), batch, seq, tt)
            z = _rglru(u_tiles, yb, conv_b[j], _row(conv_bias_b[j]),
                       w_rgate_b[j].astype(BF16), b_rgate_b[j], w_igate_b[j].astype(BF16),
                       b_igate_b[j], lam_b[j], seq, tt)
            w_out = w_out_b[j]
        h = _moe_and_ple(z, w_out.astype(BF16), h, i, i == depth - 1, norm_ffn, w_router_group,
                         b_router_group, w_router_expert, b_router_expert, w_exp_gate, w_exp_up,
                         w_exp_down, norm_ple, w_ple, w_ple_gate, b_ple_gate, norm_final, p)
    return h.reshape(batch, seq, d)
```

```python
import functools

import jax
import jax.numpy as jnp
from jax import lax
from jax.experimental import pallas as pl
from jax.experimental.pallas import tpu as pltpu
from jax.experimental.pallas import tpu_sc as plsc

F32 = jnp.float32
BF16 = jnp.bfloat16
I32 = jnp.int32
U32 = jnp.uint32

EPS = 1e-6
N_GROUPS = 4
EXPERTS_PER_GROUP = 8
N_EXPERTS = N_GROUPS * EXPERTS_PER_GROUP
RG_C = 8.0

LANES = 128
SUBLANES = 8
BF16_ROWS = 16
EXPERT_LANE0 = N_GROUPS
ROUTE_ROWS = 256
IN_AHEAD = 4
IN_SLOTS = IN_AHEAD + 2
OUT_SLOTS = 4
SC_CORES = 2
SC_SUBCORES = 16
SC_CHUNK = 64
COMBINE_PARTS = 4
M_E0, M_E1, M_R0, M_R1, M_W0, M_W1 = 0, 1, 2, 3, 4, 5
VMEM_LIMIT = 48 * 1024 * 1024


def _cparams(*sem):
    return pltpu.CompilerParams(dimension_semantics=sem, vmem_limit_bytes=VMEM_LIMIT)


def _rms(x, g):
    return x * lax.rsqrt(jnp.mean(x * x, axis=-1, keepdims=True) + EPS) * g


def _dot(a, b):
    return jnp.dot(a, b, preferred_element_type=F32)


def _pack_bf16_pairs(x):
    k = x.shape[1] // 2
    hi = lax.bitcast_convert_type(x[:, :k].astype(BF16).astype(F32), U32)
    lo = lax.bitcast_convert_type(x[:, k:].astype(BF16).astype(F32), U32)
    return hi | (lo >> 16)


def _unpack_bf16_pairs(p):
    hi = lax.bitcast_convert_type(p & jnp.uint32(0xFFFF0000), F32)
    lo = lax.bitcast_convert_type(p << 16, F32)
    return hi, lo


def _tile(n, want):
    t = min(n, want)
    assert n % t == 0, (n, want)
    return t


def _mix_a_in_kernel(h_ref, g_ref, w_ref, bg_ref, v_ref):
    d = h_ref.shape[1]
    xn = _rms(h_ref[...], g_ref[...]).astype(BF16)
    bg = _dot(xn, w_ref[:, 0:d])
    cg = _dot(xn, w_ref[:, d:2 * d])
    hh = _dot(xn, w_ref[:, 2 * d:3 * d])
    bg_ref[...] = bg.astype(BF16)
    v_ref[...] = (cg * hh).astype(BF16)


def _mix_a_in(h, g, w_in):
    n, d = h.shape
    tm = _tile(n, 512)
    return pl.pallas_call(
        _mix_a_in_kernel,
        grid=(n // tm,),
        in_specs=[pl.BlockSpec((tm, d), lambda i: (i, 0)),
                  pl.BlockSpec((1, d), lambda i: (0, 0)),
                  pl.BlockSpec((d, 3 * d), lambda i: (0, 0))],
        out_specs=[pl.BlockSpec((tm, d), lambda i: (i, 0)),
                   pl.BlockSpec((tm, d), lambda i: (i, 0))],
        out_shape=[jax.ShapeDtypeStruct((n, d), BF16), jax.ShapeDtypeStruct((n, d), BF16)],
        compiler_params=_cparams("parallel"),
        name="mix_a_in",
    )(h, g, w_in)


def _mix_a_conv_kernel(v_ref, vp_ref, vn_ref, bg_ref, cw_ref, z_ref, *, tiles_per_seq):
    i = pl.program_id(0)
    tm = v_ref.shape[0]
    v = v_ref[...].astype(F32)
    at_start = (i % tiles_per_seq) == 0
    at_end = (i % tiles_per_seq) == tiles_per_seq - 1
    prev_row = vp_ref[...].astype(F32)[BF16_ROWS - 1:BF16_ROWS, :]
    next_row = vn_ref[...].astype(F32)[0:1, :]
    prev_row = jnp.where(at_start, 0.0, prev_row)
    next_row = jnp.where(at_end, 0.0, next_row)
    row = lax.broadcasted_iota(I32, (tm, 1), 0)
    v_dn = jnp.where(row == 0, prev_row, pltpu.roll(v, 1, 0))
    v_up = jnp.where(row == tm - 1, next_row, pltpu.roll(v, tm - 1, 0))
    cw = cw_ref[...]
    u = cw[0:1, :] * v_dn + cw[1:2, :] * v + cw[2:3, :] * v_up
    z_ref[...] = (bg_ref[...].astype(F32) * u).astype(BF16)


def _mix_a_conv(v, bg, conv_w, seq):
    n, d = v.shape
    tm = _tile(seq, 512)
    hb = tm // BF16_ROWS
    nhalo = n // BF16_ROWS
    return pl.pallas_call(
        functools.partial(_mix_a_conv_kernel, tiles_per_seq=seq // tm),
        grid=(n // tm,),
        in_specs=[pl.BlockSpec((tm, d), lambda i: (i, 0)),
                  pl.BlockSpec((BF16_ROWS, d), lambda i: (jnp.maximum(i * hb - 1, 0), 0)),
                  pl.BlockSpec((BF16_ROWS, d), lambda i: (jnp.minimum((i + 1) * hb, nhalo - 1), 0)),
                  pl.BlockSpec((tm, d), lambda i: (i, 0)),
                  pl.BlockSpec(conv_w.shape, lambda i: (0, 0))],
        out_specs=pl.BlockSpec((tm, d), lambda i: (i, 0)),
        out_shape=jax.ShapeDtypeStruct((n, d), BF16),
        compiler_params=_cparams("parallel"),
        name="mix_a_conv",
    )(v, v, v, bg, conv_w)


def _mix_b_in_kernel(h_ref, hp_ref, hn_ref, g_ref, w_ref, yb_ref, u_ref, *, steps_per_seq, tt):
    i = pl.program_id(0)
    r = yb_ref.shape[1]
    tiles = h_ref.shape[0] // tt
    g = g_ref[...]
    xn = _rms(h_ref[...], g).astype(BF16)
    yb_ref[...] = jax.nn.gelu(_dot(xn, w_ref[:, 0:r])).astype(BF16)
    u = _dot(xn, w_ref[:, r:2 * r])
    xh = _rms(jnp.concatenate([hp_ref[...], hn_ref[...]], axis=0), g).astype(BF16)
    uh = _dot(xh, w_ref[:, r:2 * r])
    at_start = (i % steps_per_seq) == 0
    at_end = (i % steps_per_seq) == steps_per_seq - 1
    prev2 = jnp.where(at_start, 0.0, uh[SUBLANES - 2:SUBLANES, :])
    next1 = jnp.where(at_end, 0.0, uh[SUBLANES:SUBLANES + 1, :])
    row = lax.broadcasted_iota(I32, (SUBLANES, 1), 0)
    for k in range(tiles):
        p2 = prev2 if k == 0 else u[k * tt - 2:k * tt]
        n1 = next1 if k == tiles - 1 else u[(k + 1) * tt:(k + 1) * tt + 1]
        pad = jnp.where(row == 0, p2[0:1], jnp.where(row == 1, p2[1:2], jnp.where(row == 2, n1, 0.0)))
        for s in range(r // LANES):
            lanes = slice(s * LANES, (s + 1) * LANES)
            u_ref[k, s, 0:tt, :] = u[k * tt:(k + 1) * tt, lanes]
            u_ref[k, s, tt:tt + SUBLANES, :] = pad[:, lanes]


def _mix_b_in(h, g, w_in, batch, seq, tt):
    n, d = h.shape
    r = w_in.shape[1] // 2
    tm = _tile(seq, 512)
    steps_per_seq = seq // tm
    tiles = tm // tt
    hb = tm // SUBLANES
    nhalo = n // SUBLANES
    return pl.pallas_call(
        functools.partial(_mix_b_in_kernel, steps_per_seq=steps_per_seq, tt=tt),
        grid=(n // tm,),
        in_specs=[pl.BlockSpec((tm, d), lambda i: (i, 0)),
                  pl.BlockSpec((SUBLANES, d), lambda i: (jnp.maximum(i * hb - 1, 0), 0)),
                  pl.BlockSpec((SUBLANES, d), lambda i: (jnp.minimum((i + 1) * hb, nhalo - 1), 0)),
                  pl.BlockSpec((1, d), lambda i: (0, 0)),
                  pl.BlockSpec((d, 2 * r), lambda i: (0, 0))],
        out_specs=[pl.BlockSpec((tm, r), lambda i: (i, 0)),
                   pl.BlockSpec((tiles, r // LANES, None, tt + SUBLANES, LANES),
                                lambda i: (i % steps_per_seq, 0, i // steps_per_seq, 0, 0))],
        out_shape=[jax.ShapeDtypeStruct((n, r), BF16),
                   jax.ShapeDtypeStruct((seq // tt, r // LANES, batch, tt + SUBLANES, LANES), F32)],
        compiler_params=_cparams("parallel"),
        name="mix_b_in",
    )(h, h, h, g, w_in)


def _rglru_gates(u, wr_ref, br_ref, wi_ref, bi_ref, lam_ref, a_s, g_s):
    ub = u.astype(BF16)
    r = 0.5 * (1.0 + jnp.tanh(0.5 * (_dot(ub, wr_ref[...]) + br_ref[...])))
    ig = 0.5 * (1.0 + jnp.tanh(0.5 * (_dot(ub, wi_ref[...]) + bi_ref[...])))
    log_a = -RG_C * r * jax.nn.softplus(-lam_ref[...])
    th = jnp.tanh(0.5 * log_a)
    q = 1.0 / (1.0 - th)
    a_s[...] = (1.0 + th) * q
    m2 = -4.0 * th * q * q
    g_s[...] = jnp.where(m2 > 0.0, m2 * lax.rsqrt(m2), 0.0) * (ig * u)


def _rglru_fwd_kernel(u_ref, cw_ref, cb_ref, wr_ref, br_ref, wi_ref, bi_ref, lam_ref,
                      hf_ref, uc_ref, x_s, a_s, g_s, carry_s, *, tt):
    nb = SUBLANES
    ttp = tt + SUBLANES
    rows = tt * nb
    slabs = u_ref.shape[0]

    @pl.when(pl.program_id(1) == 0)
    def _():
        carry_s[...] = jnp.zeros_like(carry_s)

    def gather_t(tl, dst):
        for s in range(slabs):
            x_s[pl.ds(dst, nb), s * LANES:(s + 1) * LANES] = u_ref[s, pl.ds(tl, nb, stride=ttp), :]

    def load_t(tl, c):
        gather_t(tl, pl.multiple_of((tl + 2) * nb, nb))
        return c

    lax.fori_loop(0, tt, load_t, 0, unroll=8)
    gather_t(tt, 0)
    gather_t(tt + 1, nb)
    gather_t(tt + 2, (tt + 2) * nb)

    cw = cw_ref[...]
    u = (cw[0:1, :] * x_s[0:rows, :] + cw[1:2, :] * x_s[nb:nb + rows, :]
         + cw[2:3, :] * x_s[2 * nb:2 * nb + rows, :] + cw[3:4, :] * x_s[3 * nb:3 * nb + rows, :]
         + cb_ref[...])
    uc_ref[...] = u
    _rglru_gates(u, wr_ref, br_ref, wi_ref, bi_ref, lam_ref, a_s, g_s)

    def step(tl, h):
        base = pl.multiple_of(tl * nb, nb)
        h = a_s[pl.ds(base, nb), :] * h + g_s[pl.ds(base, nb), :]
        hf_ref[pl.ds(base, nb), :] = h
        return h

    carry_s[...] = lax.fori_loop(0, tt, step, carry_s[...], unroll=8)


def _rglru_rev_kernel(uc_ref, wr_ref, br_ref, wi_ref, bi_ref, lam_ref, hf_ref, yb_ref,
                      z_ref, a_s, g_s, carry_s, nat_s, *, tt):
    nb = SUBLANES
    ttp = tt + SUBLANES
    slabs = nat_s.shape[0]

    @pl.when(pl.program_id(1) == 0)
    def _():
        carry_s[...] = jnp.zeros_like(carry_s)

    _rglru_gates(uc_ref[...], wr_ref, br_ref, wi_ref, bi_ref, lam_ref, a_s, g_s)

    def step(k, h):
        tl = tt - 1 - k
        base = pl.multiple_of(tl * nb, nb)
        h = a_s[pl.ds(base, nb), :] * h + g_s[pl.ds(base, nb), :]
        hs = h + hf_ref[pl.ds(base, nb), :]
        for s in range(slabs):
            nat_s[s, pl.ds(tl, nb, stride=ttp), :] = hs[:, s * LANES:(s + 1) * LANES]
        return h

    carry_s[...] = lax.fori_loop(0, tt, step, carry_s[...], unroll=8)
    for b in range(nb):
        for s in range(slabs):
            lanes = slice(s * LANES, (s + 1) * LANES)
            hb = nat_s[s, b * ttp:b * ttp + tt, :]
            z_ref[b, :, lanes] = (yb_ref[b, :, lanes].astype(F32) * hb).astype(BF16)


def _rglru(u_tiles, yb, conv_w, conv_b, w_r, b_r, w_i, b_i, lam, seq, tt):
    nt, nslab, batch, ttp, _ = u_tiles.shape
    r = nslab * LANES
    assert batch == SUBLANES, "time-major tile layout puts the batch on the 8 sublanes"
    nh, bk = w_r.shape[1], w_r.shape[2]
    slabs = bk // LANES
    kw = conv_w.shape[0]
    rows = tt * batch
    gates = [w_r, b_r.reshape(2, 1, r), w_i, b_i.reshape(2, 1, r), lam.reshape(2, 1, r)]

    def gate_specs(d):
        vec = pl.BlockSpec((None, 1, bk), lambda h, jj: (d, 0, h))
        mat = pl.BlockSpec((None, None, bk, bk), lambda h, jj: (d, h, 0, 0))
        return [mat, vec, mat, vec, vec]

    tm_spec = pl.BlockSpec((rows, bk), lambda h, jj: (jj, h))
    tm_rev = pl.BlockSpec((rows, bk), lambda h, jj: (nt - 1 - jj, h))
    tm_shape = jax.ShapeDtypeStruct((seq * batch, r), F32)
    state = [pltpu.VMEM((rows, bk), F32), pltpu.VMEM((rows, bk), F32), pltpu.VMEM((batch, bk), F32)]

    hf, uc = pl.pallas_call(
        functools.partial(_rglru_fwd_kernel, tt=tt),
        grid=(nh, nt),
        in_specs=[pl.BlockSpec((None, slabs, batch * ttp, LANES), lambda h, jj: (jj, h, 0, 0)),
                  pl.BlockSpec((kw, bk), lambda h, jj: (0, h)),
                  pl.BlockSpec((1, bk), lambda h, jj: (0, h))] + gate_specs(0),
        out_specs=[tm_spec, tm_spec],
        out_shape=[tm_shape, tm_shape],
        scratch_shapes=[pltpu.VMEM(((tt + 3) * batch, bk), F32)] + state,
        compiler_params=_cparams("parallel", "arbitrary"),
        name="rglru_fwd",
    )(u_tiles.reshape(nt, nslab, batch * ttp, LANES), conv_w, conv_b, *gates)

    nat_spec = pl.BlockSpec((batch, tt, bk), lambda h, jj: (0, nt - 1 - jj, h))
    z = pl.pallas_call(
        functools.partial(_rglru_rev_kernel, tt=tt),
        grid=(nh, nt),
        in_specs=[tm_rev] + gate_specs(1) + [tm_rev, nat_spec],
        out_specs=nat_spec,
        out_shape=jax.ShapeDtypeStruct((batch, seq, r), BF16),
        scratch_shapes=state + [pltpu.VMEM((slabs, batch * ttp, LANES), F32)],
        compiler_params=_cparams("parallel", "arbitrary"),
        name="rglru_rev",
    )(uc, *gates, hf, yb.reshape(batch, seq, r))
    return z.reshape(-1, r)


def _post_mix_kernel(z_ref, wo_ref, h_ref, g_ref, wr_ref, br_ref,
                     h1_ref, xn_ref, meta_ref, meta_t_ref, cnt_ref, carry_ref):
    i = pl.program_id(0)
    tm = z_ref.shape[0]

    @pl.when(i == 0)
    def _():
        carry_ref[...] = jnp.zeros_like(carry_ref)

    h1 = h_ref[...] + _dot(z_ref[...], wo_ref[...])
    h1_ref[...] = h1
    xn = _rms(h1, g_ref[...])
    xn_ref[...] = _pack_bf16_pairs(xn)

    logits = _dot(xn.astype(BF16), wr_ref[...].astype(BF16)) + br_ref[...]

    lane = lax.broadcasted_iota(I32, (tm, LANES), 1).astype(F32)
    neg = jnp.float32(-jnp.inf)
    nolane = jnp.float32(LANES)
    gmask = lane < N_GROUPS
    gmax = jnp.max(jnp.where(gmask, logits, neg), axis=1, keepdims=True)
    eg = jnp.where(gmask, jnp.exp(logits - gmax), 0.0)
    gsum = jnp.sum(eg, axis=1, keepdims=True)
    pg = eg / gsum
    pg_top = 1.0 / gsum
    g_top = jnp.min(jnp.where(gmask & (pg == pg_top), lane, nolane), axis=1, keepdims=True)
    lo = EXPERT_LANE0 + g_top * EXPERTS_PER_GROUP
    emask = (lane >= lo) & (lane < lo + EXPERTS_PER_GROUP)
    emax = jnp.max(jnp.where(emask, logits, neg), axis=1, keepdims=True)
    ee = jnp.where(emask, jnp.exp(logits - emax), 0.0)
    esum = jnp.sum(ee, axis=1, keepdims=True)
    pe = jnp.where(emask, ee / esum, -1.0)
    p1 = 1.0 / esum
    i1 = jnp.min(jnp.where(pe == p1, lane, nolane), axis=1, keepdims=True)
    pe2 = jnp.where(lane == i1, -1.0, pe)
    p2 = jnp.max(pe2, axis=1, keepdims=True)
    i2 = jnp.min(jnp.where(pe2 == p2, lane, nolane), axis=1, keepdims=True)
    psum = p1 + p2
    w0 = pg_top * (p1 / psum)
    w1 = pg_top * (p2 / psum)

    oh0 = lane == i1
    oh1 = lane == i2
    both = (oh0 | oh1).astype(BF16)
    ri = lax.broadcasted_iota(I32, (tm, tm), 0)
    ci = lax.broadcasted_iota(I32, (tm, tm), 1)
    before = (ri > ci).astype(BF16)
    cnt_before = _dot(before, both) + carry_ref[...]
    rank0 = jnp.sum(jnp.where(oh0, cnt_before, 0.0), axis=1, keepdims=True)
    rank1 = jnp.sum(jnp.where(oh1, cnt_before, 0.0), axis=1, keepdims=True)
    carry = carry_ref[...] + jnp.sum(both.astype(F32), axis=0, keepdims=True)
    carry_ref[...] = carry
    cnt_ref[...] = carry

    e0 = i1 - EXPERT_LANE0
    e1 = i2 - EXPERT_LANE0
    meta = jnp.zeros((tm, LANES), F32)
    for ln, val in ((M_E0, e0), (M_E1, e1), (M_R0, rank0), (M_R1, rank1), (M_W0, w0), (M_W1, w1)):
        meta = jnp.where(lane == ln, val, meta)
    meta_ref[...] = meta
    meta_t_ref[...] = meta.T[0:SUBLANES, :]


def _post_mix(z, w_out, h, g, w_router, b_router):
    n, d = h.shape
    k = z.shape[1]
    tm = _tile(n, 512)
    return pl.pallas_call(
        _post_mix_kernel,
        grid=(n // tm,),
        in_specs=[pl.BlockSpec((tm, k), lambda i: (i, 0)),
                  pl.BlockSpec((k, d), lambda i: (0, 0)),
                  pl.BlockSpec((tm, d), lambda i: (i, 0)),
                  pl.BlockSpec((1, d), lambda i: (0, 0)),
                  pl.BlockSpec((d, LANES), lambda i: (0, 0)),
                  pl.BlockSpec((1, LANES), lambda i: (0, 0))],
        out_specs=[pl.BlockSpec((tm, d), lambda i: (i, 0)),
                   pl.BlockSpec((tm, d // 2), lambda i: (i, 0)),
                   pl.BlockSpec((tm, LANES), lambda i: (i, 0)),
                   pl.BlockSpec((SUBLANES, tm), lambda i: (0, i)),
                   pl.BlockSpec((1, LANES), lambda i: (0, 0))],
        out_shape=[jax.ShapeDtypeStruct((n, d), F32), jax.ShapeDtypeStruct((n, d // 2), U32),
                   jax.ShapeDtypeStruct((n, LANES), F32), jax.ShapeDtypeStruct((SUBLANES, n), F32),
                   jax.ShapeDtypeStruct((1, LANES), F32)],
        scratch_shapes=[pltpu.VMEM((1, LANES), F32)],
        compiler_params=_cparams("arbitrary"),
        name="post_mix_router",
    )(z, w_out, h, g, w_router, b_router)


def _plan_kernel(cnt_ref, meta_t_ref, dest_ref, estart_ref, ecount_ref, nblk_ref, pstart_ref):
    def per_expert(e, acc):
        nb_e = (cnt_ref[e] + (ROUTE_ROWS - 1)) // ROUTE_ROWS
        pstart_ref[e] = acc * ROUTE_ROWS
        estart_ref[e] = acc
        ecount_ref[e] = nb_e
        return acc + nb_e

    nblk_ref[0] = lax.fori_loop(0, N_EXPERTS, per_expert, 0)

    e0 = meta_t_ref[M_E0:M_E0 + 1, :]
    e1 = meta_t_ref[M_E1:M_E1 + 1, :]
    d0 = meta_t_ref[M_R0:M_R0 + 1, :]
    d1 = meta_t_ref[M_R1:M_R1 + 1, :]
    for e in range(N_EXPERTS):
        ps = pstart_ref[e].astype(F32)
        d0 = d0 + jnp.where(e0 == e, ps, 0.0)
        d1 = d1 + jnp.where(e1 == e, ps, 0.0)
    dest_ref[...] = jnp.zeros_like(dest_ref)
    dest_ref[0:1, :] = d0.astype(I32)
    dest_ref[1:2, :] = d1.astype(I32)


def _plan(cnt_i32, meta_t):
    n = meta_t.shape[1]
    smem = pl.BlockSpec(memory_space=pltpu.SMEM)
    return pl.pallas_call(
        _plan_kernel,
        in_specs=[smem, pl.BlockSpec((SUBLANES, n), lambda: (0, 0))],
        out_specs=[pl.BlockSpec((SUBLANES, n), lambda: (0, 0)), smem, smem, smem],
        out_shape=[jax.ShapeDtypeStruct((SUBLANES, n), I32),
                   jax.ShapeDtypeStruct((N_EXPERTS,), I32),
                   jax.ShapeDtypeStruct((N_EXPERTS,), I32),
                   jax.ShapeDtypeStruct((1,), I32)],
        scratch_shapes=[pltpu.SMEM((N_EXPERTS,), I32)],
        name="dispatch_plan",
    )(cnt_i32, meta_t)


def _sc_mesh():
    return plsc.VectorSubcoreMesh(core_axis_name="c", subcore_axis_name="s",
                                  num_cores=SC_CORES, num_subcores=SC_SUBCORES)


def _sc_scatter_rows(x, idx0, idx1, p_rows):
    m, k = x.shape
    per_w = m // (SC_CORES * SC_SUBCORES)
    assert per_w % SC_CHUNK == 0 and per_w * SC_CORES * SC_SUBCORES == m

    @functools.partial(
        pl.kernel, mesh=_sc_mesh(), out_type=jax.ShapeDtypeStruct((p_rows, k), x.dtype),
        scratch_types=[pltpu.VMEM((SC_CHUNK,), I32), pltpu.VMEM((SC_CHUNK,), I32),
                       pltpu.VMEM((SC_CHUNK, k), x.dtype),
                       pltpu.SemaphoreType.DMA, pltpu.SemaphoreType.DMA],
        name="sc_scatter_rows")
    def scatter(x_hbm, i0_hbm, i1_hbm, out_hbm, i0_v, i1_v, rows_v, sem0, sem1):
        base = (lax.axis_index("s") * SC_CORES + lax.axis_index("c")) * per_w

        @pl.loop(0, per_w // SC_CHUNK)
        def _(c):
            off = pl.multiple_of(base + c * SC_CHUNK, SC_CHUNK)
            pltpu.sync_copy(i0_hbm.at[pl.ds(off, SC_CHUNK)], i0_v)
            pltpu.sync_copy(i1_hbm.at[pl.ds(off, SC_CHUNK)], i1_v)
            pltpu.sync_copy(x_hbm.at[pl.ds(off, SC_CHUNK)], rows_v)
            put0 = pltpu.async_copy(rows_v, out_hbm.at[i0_v], sem0)
            put1 = pltpu.async_copy(rows_v, out_hbm.at[i1_v], sem1)
            put0.wait()
            put1.wait()

    return scatter(x, idx0, idx1)


def _expert_kernel(es_ref, ec_ref, nblk_ref, cnt_ref, xs_ref, wg_ref, wu_ref, wd_ref, ys_ref,
                   x_s, y_s, wg_s, wu_s, wd_s, sem):
    e = pl.program_id(0)
    nb = ec_ref[e]
    b0 = es_ref[e]
    k = x_s.shape[2]
    nb_max = ys_ref.shape[0] // ROUTE_ROWS

    def rows(b):
        return pl.ds(pl.multiple_of(b * ROUTE_ROWS, ROUTE_ROWS), ROUTE_ROWS)

    nblk = nblk_ref[0]

    def copy_in(b):
        slot = lax.rem(b, IN_SLOTS)
        return pltpu.make_async_copy(xs_ref.at[rows(b)], x_s.at[slot], sem.at[0, slot])

    def copy_out(b):
        slot = lax.rem(b, OUT_SLOTS)
        return pltpu.make_async_copy(y_s.at[slot], ys_ref.at[rows(b)], sem.at[1, slot])

    @pl.when(e == 0)
    def _():
        for b in range(IN_AHEAD):
            @pl.when(b < nblk)
            def _(b=b):
                copy_in(b).start()

    @pl.when(nb > 0)
    def _():
        wg_s[...] = wg_ref[...].astype(BF16)
        wu_s[...] = wu_ref[...].astype(BF16)
        wd_s[...] = wd_ref[...].astype(BF16)

        def mlp(b, j):
            row = lax.broadcasted_iota(I32, (ROUTE_ROWS, 1), 0)
            live = row < cnt_ref[e] - j * ROUTE_ROWS
            xp = jnp.where(live, x_s[lax.rem(b, IN_SLOTS)], jnp.uint32(0))
            x_hi, x_lo = _unpack_bf16_pairs(xp)
            x_hi = x_hi.astype(BF16)
            x_lo = x_lo.astype(BF16)
            g = _dot(x_hi, wg_s[0:k, :]) + _dot(x_lo, wg_s[k:2 * k, :])
            u = _dot(x_hi, wu_s[0:k, :]) + _dot(x_lo, wu_s[k:2 * k, :])
            hb = (jax.nn.silu(g) * u).astype(BF16)
            return _pack_bf16_pairs(_dot(hb, wd_s[...]))

        def unit(j, width):
            b = b0 + j
            for i in range(width):
                copy_in(b + i).wait()
            for i in range(width):
                @pl.when(b + i + IN_AHEAD < nblk)
                def _(i=i):
                    copy_in(b + i + IN_AHEAD).start()
            ys = [mlp(b + i, j + i) for i in range(width)]
            for i in range(width):
                @pl.when(b + i >= OUT_SLOTS)
                def _(i=i):
                    copy_out(b + i - OUT_SLOTS).wait()
            for i in range(width):
                y_s[lax.rem(b + i, OUT_SLOTS)] = ys[i]
                copy_out(b + i).start()

        def pair(jj, c):
            unit(2 * jj, 2)
            return c

        lax.fori_loop(0, nb // 2, pair, 0)

        @pl.when(nb % 2 == 1)
        def _():
            unit(nb - 1, 1)

    @pl.when(e == pl.num_programs(0) - 1)
    def _():
        for back in range(OUT_SLOTS, 0, -1):
            @pl.when(nblk >= back)
            def _(back=back):
                copy_out(nblk - back).wait()

    @pl.when(e == pl.num_programs(0) - 1)
    def _():
        y_s[0] = jnp.zeros_like(y_s[0])

        def zero(b):
            return pltpu.make_async_copy(y_s.at[0], ys_ref.at[rows(b)], sem.at[1, 0])

        def start_zero(b, c):
            zero(b).start()
            return c

        def wait_zero(b, c):
            zero(b).wait()
            return c

        lax.fori_loop(nblk_ref[0], nb_max, start_zero, 0)
        lax.fori_loop(nblk_ref[0], nb_max, wait_zero, 0)


def _experts(estart, ecount, nblk, cnt, xs, w_gate, w_up, w_down, layer):
    p, k = xs.shape
    d, f = w_gate.shape[2], w_gate.shape[3]

    def w_map(e, es, ec, nb, cn):
        return (layer, e, 0, 0)

    grid_spec = pltpu.PrefetchScalarGridSpec(
        num_scalar_prefetch=4,
        grid=(N_EXPERTS,),
        in_specs=[pl.BlockSpec(memory_space=pl.ANY),
                  pl.BlockSpec((None, None, d, f), w_map),
                  pl.BlockSpec((None, None, d, f), w_map),
                  pl.BlockSpec((None, None, f, d), w_map)],
        out_specs=pl.BlockSpec(memory_space=pl.ANY),
        scratch_shapes=[pltpu.VMEM((IN_SLOTS, ROUTE_ROWS, k), U32),
                        pltpu.VMEM((OUT_SLOTS, ROUTE_ROWS, k), U32),
                        pltpu.VMEM((d, f), BF16), pltpu.VMEM((d, f), BF16), pltpu.VMEM((f, d), BF16),
                        pltpu.SemaphoreType.DMA((2, max(IN_SLOTS, OUT_SLOTS)))],
    )
    return pl.pallas_call(
        _expert_kernel,
        grid_spec=grid_spec,
        out_shape=jax.ShapeDtypeStruct((p, k), U32),
        compiler_params=_cparams("arbitrary"),
        name="expert_mlp",
    )(estart, ecount, nblk, cnt, xs, w_gate, w_up, w_down)


def _sc_gather_pair(table, idx0, idx1, start, count):
    k = table.shape[1]
    per_w = count // (SC_CORES * SC_SUBCORES)
    assert per_w % SC_CHUNK == 0 and per_w * SC_CORES * SC_SUBCORES == count
    assert start % SC_CHUNK == 0
    rows = jax.ShapeDtypeStruct((count, k), table.dtype)

    @functools.partial(
        pl.kernel, mesh=_sc_mesh(), out_type=(rows, rows),
        scratch_types=[pltpu.VMEM((SC_CHUNK,), I32), pltpu.VMEM((SC_CHUNK,), I32),
                       pltpu.VMEM((SC_CHUNK, k), table.dtype), pltpu.VMEM((SC_CHUNK, k), table.dtype),
                       pltpu.SemaphoreType.DMA, pltpu.SemaphoreType.DMA,
                       pltpu.SemaphoreType.DMA, pltpu.SemaphoreType.DMA],
        name="sc_gather_pair")
    def gather(table_hbm, i0_hbm, i1_hbm, o0_hbm, o1_hbm, i0_v, i1_v, r0_v, r1_v,
               sem0, sem1, sem2, sem3):
        base = (lax.axis_index("s") * SC_CORES + lax.axis_index("c")) * per_w

        @pl.loop(0, per_w // SC_CHUNK)
        def _(c):
            dst = pl.multiple_of(base + c * SC_CHUNK, SC_CHUNK)
            src = pl.multiple_of(start + dst, SC_CHUNK)
            pltpu.sync_copy(i0_hbm.at[pl.ds(src, SC_CHUNK)], i0_v)
            pltpu.sync_copy(i1_hbm.at[pl.ds(src, SC_CHUNK)], i1_v)
            get0 = pltpu.async_copy(table_hbm.at[i0_v], r0_v, sem0)
            get1 = pltpu.async_copy(table_hbm.at[i1_v], r1_v, sem1)
            get0.wait()
            put0 = pltpu.async_copy(r0_v, o0_hbm.at[pl.ds(dst, SC_CHUNK)], sem2)
            get1.wait()
            put1 = pltpu.async_copy(r1_v, o1_hbm.at[pl.ds(dst, SC_CHUNK)], sem3)
            put0.wait()
            put1.wait()

    return gather(table, idx0, idx1)


def _combine_kernel(*refs, final):
    (ya_ref, yb_ref, h1_ref, meta_ref, p_ref, gp_ref, wpg_ref, bpg_ref, wp_ref,
     gf_ref) = refs[:10]
    out_ref = refs[-1]
    meta = meta_ref[...]
    w0 = meta[:, M_W0:M_W0 + 1]
    w1 = meta[:, M_W1:M_W1 + 1]
    ya_hi, ya_lo = _unpack_bf16_pairs(ya_ref[...])
    yb_hi, yb_lo = _unpack_bf16_pairs(yb_ref[...])
    y = jnp.concatenate([ya_hi * w0 + yb_hi * w1, ya_lo * w0 + yb_lo * w1], axis=1)
    h2 = h1_ref[...] + y
    xn = _rms(h2, gp_ref[...]).astype(BF16)
    gate = jax.nn.sigmoid(_dot(xn, wpg_ref[...]) + bpg_ref[...])
    pp = _dot(p_ref[...].astype(BF16), wp_ref[...])
    h3 = h2 + gate * pp
    if final:
        h3 = _rms(h3, gf_ref[...])
    out_ref[...] = h3


def _combine(dest, ys, h1, meta, p, layer, g_ple, w_ple_gate, b_ple_gate, w_ple, g_final, final):
    n, d = h1.shape
    pd = p.shape[2]
    k = ys.shape[1]
    n_part = n // COMBINE_PARTS
    tmc = _tile(n_part, 512)
    tiles = n_part // tmc
    vec = pl.BlockSpec((1, d), lambda i: (0, 0))
    out = None
    for q in range(COMBINE_PARTS):
        ya, yb = _sc_gather_pair(ys, dest[0], dest[1], q * n_part, n_part)

        def tok(i, q=q):
            return (q * tiles + i, 0)

        in_specs = [pl.BlockSpec((tmc, k), lambda i: (i, 0)),
                    pl.BlockSpec((tmc, k), lambda i: (i, 0)),
                    pl.BlockSpec((tmc, d), tok),
                    pl.BlockSpec((tmc, LANES), tok),
                    pl.BlockSpec((None, tmc, pd), lambda i, q=q: (layer, q * tiles + i, 0)),
                    vec,
                    pl.BlockSpec((d, d), lambda i: (0, 0)),
                    vec,
                    pl.BlockSpec((pd, d), lambda i: (0, 0)),
                    vec]
        args = [ya, yb, h1, meta, p, g_ple, w_ple_gate, b_ple_gate, w_ple, g_final]
        aliases = {}
        if out is not None:
            in_specs.append(pl.BlockSpec(memory_space=pl.ANY))
            args.append(out)
            aliases = {len(args) - 1: 0}
        out = pl.pallas_call(
            functools.partial(_combine_kernel, final=final),
            grid=(tiles,),
            in_specs=in_specs,
            out_specs=pl.BlockSpec((tmc, d), tok),
            out_shape=jax.ShapeDtypeStruct((n, d), F32),
            input_output_aliases=aliases,
            compiler_params=_cparams("parallel"),
            name="combine_ple",
        )(*args)
    return out


def _row(v):
    return v.reshape(1, -1)


def _moe_and_ple(z, w_out, h, i, final, norm_ffn, w_router_group, b_router_group, w_router_expert,
                 b_router_expert, w_exp_gate, w_exp_up, w_exp_down, norm_ple, w_ple, w_ple_gate,
                 b_ple_gate, norm_final, p):
    n, d = h.shape
    pad = LANES - N_GROUPS - N_EXPERTS
    w_router = jnp.concatenate(
        [w_router_group[i], w_router_expert[i], jnp.zeros((d, pad), F32)], axis=1)
    b_router = jnp.concatenate(
        [b_router_group[i], b_router_expert[i], jnp.zeros((pad,), F32)]).reshape(1, LANES)
    h1, xn, meta, meta_t, cnt = _post_mix(z, w_out, h, _row(norm_ffn[i]), w_router, b_router)
    cnt_i32 = cnt[0, EXPERT_LANE0:EXPERT_LANE0 + N_EXPERTS].astype(I32)
    nb_max = (2 * n) // ROUTE_ROWS + N_EXPERTS
    dest, estart, ecount, nblk = _plan(cnt_i32, meta_t)
    xs = _sc_scatter_rows(xn, dest[0], dest[1], nb_max * ROUTE_ROWS)
    ys = _experts(estart, ecount, nblk, cnt_i32, xs, w_exp_gate, w_exp_up, w_exp_down, i)
    return _combine(dest, ys, h1, meta, p.reshape(p.shape[0], n, -1), i, _row(norm_ple[i]),
                    w_ple_gate[i].astype(BF16), _row(b_ple_gate[i]), w_ple[i].astype(BF16),
                    _row(norm_final), final)


def kernel(x, p, norm_mix, w_in_a, conv_a, w_out_a, w_in_b, conv_b, conv_bias_b, w_rgate_b, b_rgate_b, w_igate_b, b_igate_b, lam_b, w_out_b, norm_ffn, w_router_group, b_router_group, w_router_expert, b_router_expert, w_exp_gate, w_exp_up, w_exp_down, norm_ple, w_ple, w_ple_gate, b_ple_gate, norm_final):
    batch, seq, d = x.shape
    depth = p.shape[0]
    n = batch * seq
    h = x.reshape(n, d)
    for i in range(depth):
        j = i // 2
        if i % 2 == 0:
            bg, v = _mix_a_in(h, _row(norm_mix[i]), w_in_a[j].astype(BF16))
            z = _mix_a_conv(v, bg, conv_a[j], seq)
            w_out = w_out_a[j]
        else:
            tt = _tile(seq, 256)
            yb, u_tiles = _mix_b_in(h, _row(norm_mix[i]), w_in_b[j].astype(BF16), batch, seq, tt)
            z = _rglru(u_tiles, yb, conv_b[j], _row(conv_bias_b[j]),
                       w_rgate_b[j].astype(BF16), b_rgate_b[j], w_igate_b[j].astype(BF16),
                       b_igate_b[j], lam_b[j], seq, tt)
            w_out = w_out_b[j]
        h = _moe_and_ple(z, w_out.astype(BF16), h, i, i == depth - 1, norm_ffn, w_router_group,
                         b_router_group, w_router_expert, b_router_expert, w_exp_gate, w_exp_up,
                         w_exp_down, norm_ple, w_ple, w_ple_gate, b_ple_gate, norm_final, p)
    return h.reshape(batch, seq, d)
```

```python
import functools

import jax
import jax.numpy as jnp
from jax import lax
from jax.experimental import pallas as pl
from jax.experimental.pallas import tpu as pltpu
from jax.experimental.pallas import tpu_sc as plsc

F32 = jnp.float32
BF16 = jnp.bfloat16
I32 = jnp.int32
U32 = jnp.uint32

EPS = 1e-6
N_GROUPS = 4
EXPERTS_PER_GROUP = 8
N_EXPERTS = N_GROUPS * EXPERTS_PER_GROUP
RG_C = 8.0

LANES = 128
SUBLANES = 8
BF16_ROWS = 16
EXPERT_LANE0 = N_GROUPS
ROUTE_ROWS = 256
IN_AHEAD = 4
IN_SLOTS = IN_AHEAD + 2
OUT_SLOTS = 4
SC_CORES = 2
SC_SUBCORES = 16
SC_CHUNK = 64
COMBINE_PARTS = 4
M_E0, M_E1, M_R0, M_R1, M_W0, M_W1 = 0, 1, 2, 3, 4, 5
VMEM_LIMIT = 48 * 1024 * 1024


def _cparams(*sem):
    return pltpu.CompilerParams(dimension_semantics=sem, vmem_limit_bytes=VMEM_LIMIT)


def _rms(x, g):
    return x * lax.rsqrt(jnp.mean(x * x, axis=-1, keepdims=True) + EPS) * g


def _dot(a, b):
    return jnp.dot(a, b, preferred_element_type=F32)


def _pack_bf16_pairs(x):
    k = x.shape[1] // 2
    hi = lax.bitcast_convert_type(x[:, :k].astype(BF16).astype(F32), U32)
    lo = lax.bitcast_convert_type(x[:, k:].astype(BF16).astype(F32), U32)
    return hi | (lo >> 16)


def _unpack_bf16_pairs(p):
    hi = lax.bitcast_convert_type(p & jnp.uint32(0xFFFF0000), F32)
    lo = lax.bitcast_convert_type(p << 16, F32)
    return hi, lo


def _tile(n, want):
    t = min(n, want)
    assert n % t == 0, (n, want)
    return t


def _mix_a_in_kernel(h_ref, g_ref, w_ref, bg_ref, v_ref):
    d = h_ref.shape[1]
    xn = _rms(h_ref[...], g_ref[...]).astype(BF16)
    bg = _dot(xn, w_ref[:, 0:d])
    cg = _dot(xn, w_ref[:, d:2 * d])
    hh = _dot(xn, w_ref[:, 2 * d:3 * d])
    bg_ref[...] = bg.astype(BF16)
    v_ref[...] = (cg * hh).astype(BF16)


def _mix_a_in(h, g, w_in):
    n, d = h.shape
    tm = _tile(n, 512)
    return pl.pallas_call(
        _mix_a_in_kernel,
        grid=(n // tm,),
        in_specs=[pl.BlockSpec((tm, d), lambda i: (i, 0)),
                  pl.BlockSpec((1, d), lambda i: (0, 0)),
                  pl.BlockSpec((d, 3 * d), lambda i: (0, 0))],
        out_specs=[pl.BlockSpec((tm, d), lambda i: (i, 0)),
                   pl.BlockSpec((tm, d), lambda i: (i, 0))],
        out_shape=[jax.ShapeDtypeStruct((n, d), BF16), jax.ShapeDtypeStruct((n, d), BF16)],
        compiler_params=_cparams("parallel"),
        name="mix_a_in",
    )(h, g, w_in)


def _mix_a_conv_kernel(v_ref, vp_ref, vn_ref, bg_ref, cw_ref, z_ref, *, tiles_per_seq):
    i = pl.program_id(0)
    tm = v_ref.shape[0]
    v = v_ref[...].astype(F32)
    at_start = (i % tiles_per_seq) == 0
    at_end = (i % tiles_per_seq) == tiles_per_seq - 1
    prev_row = vp_ref[...].astype(F32)[BF16_ROWS - 1:BF16_ROWS, :]
    next_row = vn_ref[...].astype(F32)[0:1, :]
    prev_row = jnp.where(at_start, 0.0, prev_row)
    next_row = jnp.where(at_end, 0.0, next_row)
    row = lax.broadcasted_iota(I32, (tm, 1), 0)
    v_dn = jnp.where(row == 0, prev_row, pltpu.roll(v, 1, 0))
    v_up = jnp.where(row == tm - 1, next_row, pltpu.roll(v, tm - 1, 0))
    cw = cw_ref[...]
    u = cw[0:1, :] * v_dn + cw[1:2, :] * v + cw[2:3, :] * v_up
    z_ref[...] = (bg_ref[...].astype(F32) * u).astype(BF16)


def _mix_a_conv(v, bg, conv_w, seq):
    n, d = v.shape
    tm = _tile(seq, 512)
    hb = tm // BF16_ROWS
    nhalo = n // BF16_ROWS
    return pl.pallas_call(
        functools.partial(_mix_a_conv_kernel, tiles_per_seq=seq // tm),
        grid=(n // tm,),
        in_specs=[pl.BlockSpec((tm, d), lambda i: (i, 0)),
                  pl.BlockSpec((BF16_ROWS, d), lambda i: (jnp.maximum(i * hb - 1, 0), 0)),
                  pl.BlockSpec((BF16_ROWS, d), lambda i: (jnp.minimum((i + 1) * hb, nhalo - 1), 0)),
                  pl.BlockSpec((tm, d), lambda i: (i, 0)),
                  pl.BlockSpec(conv_w.shape, lambda i: (0, 0))],
        out_specs=pl.BlockSpec((tm, d), lambda i: (i, 0)),
        out_shape=jax.ShapeDtypeStruct((n, d), BF16),
        compiler_params=_cparams("parallel"),
        name="mix_a_conv",
    )(v, v, v, bg, conv_w)


def _mix_b_in_kernel(h_ref, hp_ref, hn_ref, g_ref, w_ref, yb_ref, u_ref, *, steps_per_seq, tt):
    i = pl.program_id(0)
    r = yb_ref.shape[1]
    tiles = h_ref.shape[0] // tt
    g = g_ref[...]
    xn = _rms(h_ref[...], g).astype(BF16)
    yb_ref[...] = jax.nn.gelu(_dot(xn, w_ref[:, 0:r])).astype(BF16)
    u = _dot(xn, w_ref[:, r:2 * r])
    xh = _rms(jnp.concatenate([hp_ref[...], hn_ref[...]], axis=0), g).astype(BF16)
    uh = _dot(xh, w_ref[:, r:2 * r])
    at_start = (i % steps_per_seq) == 0
    at_end = (i % steps_per_seq) == steps_per_seq - 1
    prev2 = jnp.where(at_start, 0.0, uh[SUBLANES - 2:SUBLANES, :])
    next1 = jnp.where(at_end, 0.0, uh[SUBLANES:SUBLANES + 1, :])
    row = lax.broadcasted_iota(I32, (SUBLANES, 1), 0)
    for k in range(tiles):
        p2 = prev2 if k == 0 else u[k * tt - 2:k * tt]
        n1 = next1 if k == tiles - 1 else u[(k + 1) * tt:(k + 1) * tt + 1]
        pad = jnp.where(row == 0, p2[0:1], jnp.where(row == 1, p2[1:2], jnp.where(row == 2, n1, 0.0)))
        for s in range(r // LANES):
            lanes = slice(s * LANES, (s + 1) * LANES)
            u_ref[k, s, 0:tt, :] = u[k * tt:(k + 1) * tt, lanes]
            u_ref[k, s, tt:tt + SUBLANES, :] = pad[:, lanes]


def _mix_b_in(h, g, w_in, batch, seq, tt):
    n, d = h.shape
    r = w_in.shape[1] // 2
    tm = _tile(seq, 512)
    steps_per_seq = seq // tm
    tiles = tm // tt
    hb = tm // SUBLANES
    nhalo = n // SUBLANES
    return pl.pallas_call(
        functools.partial(_mix_b_in_kernel, steps_per_seq=steps_per_seq, tt=tt),
        grid=(n // tm,),
        in_specs=[pl.BlockSpec((tm, d), lambda i: (i, 0)),
                  pl.BlockSpec((SUBLANES, d), lambda i: (jnp.maximum(i * hb - 1, 0), 0)),
                  pl.BlockSpec((SUBLANES, d), lambda i: (jnp.minimum((i + 1) * hb, nhalo - 1), 0)),
                  pl.BlockSpec((1, d), lambda i: (0, 0)),
                  pl.BlockSpec((d, 2 * r), lambda i: (0, 0))],
        out_specs=[pl.BlockSpec((tm, r), lambda i: (i, 0)),
                   pl.BlockSpec((tiles, r // LANES, None, tt + SUBLANES, LANES),
                                lambda i: (i % steps_per_seq, 0, i // steps_per_seq, 0, 0))],
        out_shape=[jax.ShapeDtypeStruct((n, r), BF16),
                   jax.ShapeDtypeStruct((seq // tt, r // LANES, batch, tt + SUBLANES, LANES), F32)],
        compiler_params=_cparams("parallel"),
        name="mix_b_in",
    )(h, h, h, g, w_in)


def _rglru_gates(u, wr_ref, br_ref, wi_ref, bi_ref, lam_ref, a_s, g_s):
    ub = u.astype(BF16)
    r = 0.5 * (1.0 + jnp.tanh(0.5 * (_dot(ub, wr_ref[...]) + br_ref[...])))
    ig = 0.5 * (1.0 + jnp.tanh(0.5 * (_dot(ub, wi_ref[...]) + bi_ref[...])))
    log_a = -RG_C * r * jax.nn.softplus(-lam_ref[...])
    th = jnp.tanh(0.5 * log_a)
    q = 1.0 / (1.0 - th)
    a_s[...] = (1.0 + th) * q
    m2 = -4.0 * th * q * q
    g_s[...] = jnp.where(m2 > 0.0, m2 * lax.rsqrt(m2), 0.0) * (ig * u)


def _rglru_fwd_kernel(u_ref, cw_ref, cb_ref, wr_ref, br_ref, wi_ref, bi_ref, lam_ref,
                      hf_ref, uc_ref, x_s, a_s, g_s, carry_s, *, tt):
    nb = SUBLANES
    ttp = tt + SUBLANES
    rows = tt * nb
    slabs = u_ref.shape[0]

    @pl.when(pl.program_id(1) == 0)
    def _():
        carry_s[...] = jnp.zeros_like(carry_s)

    def gather_t(tl, dst):
        for s in range(slabs):
            x_s[pl.ds(dst, nb), s * LANES:(s + 1) * LANES] = u_ref[s, pl.ds(tl, nb, stride=ttp), :]

    def load_t(tl, c):
        gather_t(tl, pl.multiple_of((tl + 2) * nb, nb))
        return c

    lax.fori_loop(0, tt, load_t, 0, unroll=8)
    gather_t(tt, 0)
    gather_t(tt + 1, nb)
    gather_t(tt + 2, (tt + 2) * nb)

    cw = cw_ref[...]
    u = (cw[0:1, :] * x_s[0:rows, :] + cw[1:2, :] * x_s[nb:nb + rows, :]
         + cw[2:3, :] * x_s[2 * nb:2 * nb + rows, :] + cw[3:4, :] * x_s[3 * nb:3 * nb + rows, :]
         + cb_ref[...])
    uc_ref[...] = u
    _rglru_gates(u, wr_ref, br_ref, wi_ref, bi_ref, lam_ref, a_s, g_s)

    def step(tl, h):
        base = pl.multiple_of(tl * nb, nb)
        h = a_s[pl.ds(base, nb), :] * h + g_s[pl.ds(base, nb), :]
        hf_ref[pl.ds(base, nb), :] = h
        return h

    carry_s[...] = lax.fori_loop(0, tt, step, carry_s[...], unroll=8)


def _rglru_rev_kernel(uc_ref, wr_ref, br_ref, wi_ref, bi_ref, lam_ref, hf_ref, yb_ref,
                      z_ref, a_s, g_s, carry_s, nat_s, *, tt):
    nb = SUBLANES
    ttp = tt + SUBLANES
    slabs = nat_s.shape[0]

    @pl.when(pl.program_id(1) == 0)
    def _():
        carry_s[...] = jnp.zeros_like(carry_s)

    _rglru_gates(uc_ref[...], wr_ref, br_ref, wi_ref, bi_ref, lam_ref, a_s, g_s)

    def step(k, h):
        tl = tt - 1 - k
        base = pl.multiple_of(tl * nb, nb)
        h = a_s[pl.ds(base, nb), :] * h + g_s[pl.ds(base, nb), :]
        hs = h + hf_ref[pl.ds(base, nb), :]
        for s in range(slabs):
            nat_s[s, pl.ds(tl, nb, stride=ttp), :] = hs[:, s * LANES:(s + 1) * LANES]
        return h

    carry_s[...] = lax.fori_loop(0, tt, step, carry_s[...], unroll=8)
    for b in range(nb):
        for s in range(slabs):
            lanes = slice(s * LANES, (s + 1) * LANES)
            hb = nat_s[s, b * ttp:b * ttp + tt, :]
            z_ref[b, :, lanes] = (yb_ref[b, :, lanes].astype(F32) * hb).astype(BF16)


def _rglru(u_tiles, yb, conv_w, conv_b, w_r, b_r, w_i, b_i, lam, seq, tt):
    nt, nslab, batch, ttp, _ = u_tiles.shape
    r = nslab * LANES
    assert batch == SUBLANES, "time-major tile layout puts the batch on the 8 sublanes"
    nh, bk = w_r.shape[1], w_r.shape[2]
    slabs = bk // LANES
    kw = conv_w.shape[0]
    rows = tt * batch
    gates = [w_r, b_r.reshape(2, 1, r), w_i, b_i.reshape(2, 1, r), lam.reshape(2, 1, r)]

    def gate_specs(d):
        vec = pl.BlockSpec((None, 1, bk), lambda h, jj: (d, 0, h))
        mat = pl.BlockSpec((None, None, bk, bk), lambda h, jj: (d, h, 0, 0))
        return [mat, vec, mat, vec, vec]

    tm_spec = pl.BlockSpec((rows, bk), lambda h, jj: (jj, h))
    tm_rev = pl.BlockSpec((rows, bk), lambda h, jj: (nt - 1 - jj, h))
    tm_shape = jax.ShapeDtypeStruct((seq * batch, r), F32)
    state = [pltpu.VMEM((rows, bk), F32), pltpu.VMEM((rows, bk), F32), pltpu.VMEM((batch, bk), F32)]

    hf, uc = pl.pallas_call(
        functools.partial(_rglru_fwd_kernel, tt=tt),
        grid=(nh, nt),
        in_specs=[pl.BlockSpec((None, slabs, batch * ttp, LANES), lambda h, jj: (jj, h, 0, 0)),
                  pl.BlockSpec((kw, bk), lambda h, jj: (0, h)),
                  pl.BlockSpec((1, bk), lambda h, jj: (0, h))] + gate_specs(0),
        out_specs=[tm_spec, tm_spec],
        out_shape=[tm_shape, tm_shape],
        scratch_shapes=[pltpu.VMEM(((tt + 3) * batch, bk), F32)] + state,
        compiler_params=_cparams("parallel", "arbitrary"),
        name="rglru_fwd",
    )(u_tiles.reshape(nt, nslab, batch * ttp, LANES), conv_w, conv_b, *gates)

    nat_spec = pl.BlockSpec((batch, tt, bk), lambda h, jj: (0, nt - 1 - jj, h))
    z = pl.pallas_call(
        functools.partial(_rglru_rev_kernel, tt=tt),
        grid=(nh, nt),
        in_specs=[tm_rev] + gate_specs(1) + [tm_rev, nat_spec],
        out_specs=nat_spec,
        out_shape=jax.ShapeDtypeStruct((batch, seq, r), BF16),
        scratch_shapes=state + [pltpu.VMEM((slabs, batch * ttp, LANES), F32)],
        compiler_params=_cparams("parallel", "arbitrary"),
        name="rglru_rev",
    )(uc, *gates, hf, yb.reshape(batch, seq, r))
    return z.reshape(-1, r)


def _route(logits):
    lane = lax.broadcasted_iota(I32, logits.shape, 1).astype(F32)
    neg = jnp.float32(-jnp.inf)
    nolane = jnp.float32(LANES)
    gmask = lane < N_GROUPS
    gmax = jnp.max(jnp.where(gmask, logits, neg), axis=1, keepdims=True)
    eg = jnp.where(gmask, jnp.exp(logits - gmax), 0.0)
    gsum = jnp.sum(eg, axis=1, keepdims=True)
    pg = eg / gsum
    pg_top = 1.0 / gsum
    g_top = jnp.min(jnp.where(gmask & (pg == pg_top), lane, nolane), axis=1, keepdims=True)
    lo = EXPERT_LANE0 + g_top * EXPERTS_PER_GROUP
    emask = (lane >= lo) & (lane < lo + EXPERTS_PER_GROUP)
    emax = jnp.max(jnp.where(emask, logits, neg), axis=1, keepdims=True)
    ee = jnp.where(emask, jnp.exp(logits - emax), 0.0)
    esum = jnp.sum(ee, axis=1, keepdims=True)
    pe = jnp.where(emask, ee / esum, -1.0)
    p1 = 1.0 / esum
    i1 = jnp.min(jnp.where(pe == p1, lane, nolane), axis=1, keepdims=True)
    pe2 = jnp.where(lane == i1, -1.0, pe)
    p2 = jnp.max(pe2, axis=1, keepdims=True)
    i2 = jnp.min(jnp.where(pe2 == p2, lane, nolane), axis=1, keepdims=True)
    psum = p1 + p2
    return i1, i2, pg_top * (p1 / psum), pg_top * (p2 / psum)


def _post_mix_kernel(z_ref, wo_ref, h_ref, g_ref, wr_ref, br_ref,
                     h1_ref, xn_ref, meta_ref, meta_t_ref, cnt_ref, carry_ref, logits_s):
    i = pl.program_id(0)
    tm = z_ref.shape[0]

    @pl.when(i == 0)
    def _():
        logits_s[...] = jnp.zeros_like(logits_s)

    @pl.when(i <= 1)
    def _():
        carry_ref[...] = jnp.zeros_like(carry_ref)

    logits = logits_s[...]

    h1 = h_ref[...] + _dot(z_ref[...], wo_ref[...])
    h1_ref[...] = h1
    xn = _rms(h1, g_ref[...])
    xn_ref[...] = _pack_bf16_pairs(xn)
    logits_s[...] = _dot(xn.astype(BF16), wr_ref[...].astype(BF16)) + br_ref[...]

    i1, i2, w0, w1 = _route(logits)

    lane = lax.broadcasted_iota(I32, (tm, LANES), 1).astype(F32)
    oh0 = lane == i1
    oh1 = lane == i2
    both = (oh0 | oh1).astype(BF16)
    ri = lax.broadcasted_iota(I32, (tm, tm), 0)
    ci = lax.broadcasted_iota(I32, (tm, tm), 1)
    before = (ri > ci).astype(BF16)
    cnt_before = _dot(before, both) + carry_ref[...]
    rank0 = jnp.sum(jnp.where(oh0, cnt_before, 0.0), axis=1, keepdims=True)
    rank1 = jnp.sum(jnp.where(oh1, cnt_before, 0.0), axis=1, keepdims=True)
    carry = carry_ref[...] + jnp.sum(both.astype(F32), axis=0, keepdims=True)
    carry_ref[...] = carry
    cnt_ref[...] = carry

    e0 = i1 - EXPERT_LANE0
    e1 = i2 - EXPERT_LANE0
    meta = jnp.zeros((tm, LANES), F32)
    for ln, val in ((M_E0, e0), (M_E1, e1), (M_R0, rank0), (M_R1, rank1), (M_W0, w0), (M_W1, w1)):
        meta = jnp.where(lane == ln, val, meta)
    meta_ref[...] = meta
    meta_t_ref[...] = meta.T[0:SUBLANES, :]


def _post_mix(z, w_out, h, g, w_router, b_router):
    n, d = h.shape
    k = z.shape[1]
    tm = _tile(n, 512)
    nt = n // tm

    def proj(i):
        return (jnp.minimum(i, nt - 1), 0)

    def routed(i):
        return (jnp.maximum(i - 1, 0), 0)

    return pl.pallas_call(
        _post_mix_kernel,
        grid=(nt + 1,),
        in_specs=[pl.BlockSpec((tm, k), proj),
                  pl.BlockSpec((k, d), lambda i: (0, 0)),
                  pl.BlockSpec((tm, d), proj),
                  pl.BlockSpec((1, d), lambda i: (0, 0)),
                  pl.BlockSpec((d, LANES), lambda i: (0, 0)),
                  pl.BlockSpec((1, LANES), lambda i: (0, 0))],
        out_specs=[pl.BlockSpec((tm, d), proj),
                   pl.BlockSpec((tm, d // 2), proj),
                   pl.BlockSpec((tm, LANES), routed),
                   pl.BlockSpec((SUBLANES, tm), lambda i: (0, jnp.maximum(i - 1, 0))),
                   pl.BlockSpec((1, LANES), lambda i: (0, 0))],
        out_shape=[jax.ShapeDtypeStruct((n, d), F32), jax.ShapeDtypeStruct((n, d // 2), U32),
                   jax.ShapeDtypeStruct((n, LANES), F32), jax.ShapeDtypeStruct((SUBLANES, n), F32),
                   jax.ShapeDtypeStruct((1, LANES), F32)],
        scratch_shapes=[pltpu.VMEM((1, LANES), F32), pltpu.VMEM((tm, LANES), F32)],
        compiler_params=_cparams("arbitrary"),
        name="post_mix_router",
    )(z, w_out, h, g, w_router, b_router)


def _plan_kernel(cnt_ref, meta_t_ref, dest_ref, estart_ref, ecount_ref, nblk_ref, pstart_ref):
    def per_expert(e, acc):
        nb_e = (cnt_ref[e] + (ROUTE_ROWS - 1)) // ROUTE_ROWS
        pstart_ref[e] = acc * ROUTE_ROWS
        estart_ref[e] = acc
        ecount_ref[e] = nb_e
        return acc + nb_e

    nblk_ref[0] = lax.fori_loop(0, N_EXPERTS, per_expert, 0)

    e0 = meta_t_ref[M_E0:M_E0 + 1, :]
    e1 = meta_t_ref[M_E1:M_E1 + 1, :]
    d0 = meta_t_ref[M_R0:M_R0 + 1, :]
    d1 = meta_t_ref[M_R1:M_R1 + 1, :]
    for e in range(N_EXPERTS):
        ps = pstart_ref[e].astype(F32)
        d0 = d0 + jnp.where(e0 == e, ps, 0.0)
        d1 = d1 + jnp.where(e1 == e, ps, 0.0)
    dest_ref[...] = jnp.zeros_like(dest_ref)
    dest_ref[0:1, :] = d0.astype(I32)
    dest_ref[1:2, :] = d1.astype(I32)


def _plan(cnt_i32, meta_t):
    n = meta_t.shape[1]
    smem = pl.BlockSpec(memory_space=pltpu.SMEM)
    return pl.pallas_call(
        _plan_kernel,
        in_specs=[smem, pl.BlockSpec((SUBLANES, n), lambda: (0, 0))],
        out_specs=[pl.BlockSpec((SUBLANES, n), lambda: (0, 0)), smem, smem, smem],
        out_shape=[jax.ShapeDtypeStruct((SUBLANES, n), I32),
                   jax.ShapeDtypeStruct((N_EXPERTS,), I32),
                   jax.ShapeDtypeStruct((N_EXPERTS,), I32),
                   jax.ShapeDtypeStruct((1,), I32)],
        scratch_shapes=[pltpu.SMEM((N_EXPERTS,), I32)],
        name="dispatch_plan",
    )(cnt_i32, meta_t)


def _sc_mesh():
    return plsc.VectorSubcoreMesh(core_axis_name="c", subcore_axis_name="s",
                                  num_cores=SC_CORES, num_subcores=SC_SUBCORES)


def _sc_scatter_rows(x, idx0, idx1, p_rows):
    m, k = x.shape
    per_w = m // (SC_CORES * SC_SUBCORES)
    assert per_w % SC_CHUNK == 0 and per_w * SC_CORES * SC_SUBCORES == m

    @functools.partial(
        pl.kernel, mesh=_sc_mesh(), out_type=jax.ShapeDtypeStruct((p_rows, k), x.dtype),
        scratch_types=[pltpu.VMEM((SC_CHUNK,), I32), pltpu.VMEM((SC_CHUNK,), I32),
                       pltpu.VMEM((SC_CHUNK, k), x.dtype),
                       pltpu.SemaphoreType.DMA, pltpu.SemaphoreType.DMA],
        name="sc_scatter_rows")
    def scatter(x_hbm, i0_hbm, i1_hbm, out_hbm, i0_v, i1_v, rows_v, sem0, sem1):
        base = (lax.axis_index("s") * SC_CORES + lax.axis_index("c")) * per_w

        @pl.loop(0, per_w // SC_CHUNK)
        def _(c):
            off = pl.multiple_of(base + c * SC_CHUNK, SC_CHUNK)
            pltpu.sync_copy(i0_hbm.at[pl.ds(off, SC_CHUNK)], i0_v)
            pltpu.sync_copy(i1_hbm.at[pl.ds(off, SC_CHUNK)], i1_v)
            pltpu.sync_copy(x_hbm.at[pl.ds(off, SC_CHUNK)], rows_v)
            put0 = pltpu.async_copy(rows_v, out_hbm.at[i0_v], sem0)
            put1 = pltpu.async_copy(rows_v, out_hbm.at[i1_v], sem1)
            put0.wait()
            put1.wait()

    return scatter(x, idx0, idx1)


def _expert_kernel(es_ref, ec_ref, nblk_ref, cnt_ref, xs_ref, wg_ref, wu_ref, wd_ref, ys_ref,
                   x_s, y_s, wg_s, wu_s, wd_s, sem):
    e = pl.program_id(0)
    nb = ec_ref[e]
    b0 = es_ref[e]
    k = x_s.shape[2]
    nb_max = ys_ref.shape[0] // ROUTE_ROWS

    def rows(b):
        return pl.ds(pl.multiple_of(b * ROUTE_ROWS, ROUTE_ROWS), ROUTE_ROWS)

    nblk = nblk_ref[0]

    def copy_in(b):
        slot = lax.rem(b, IN_SLOTS)
        return pltpu.make_async_copy(xs_ref.at[rows(b)], x_s.at[slot], sem.at[0, slot])

    def copy_out(b):
        slot = lax.rem(b, OUT_SLOTS)
        return pltpu.make_async_copy(y_s.at[slot], ys_ref.at[rows(b)], sem.at[1, slot])

    @pl.when(e == 0)
    def _():
        for b in range(IN_AHEAD):
            @pl.when(b < nblk)
            def _(b=b):
                copy_in(b).start()

    @pl.when(nb > 0)
    def _():
        wg_s[...] = wg_ref[...].astype(BF16)
        wu_s[...] = wu_ref[...].astype(BF16)
        wd_s[...] = wd_ref[...].astype(BF16)

        def mlp(b, j):
            row = lax.broadcasted_iota(I32, (ROUTE_ROWS, 1), 0)
            live = row < cnt_ref[e] - j * ROUTE_ROWS
            xp = jnp.where(live, x_s[lax.rem(b, IN_SLOTS)], jnp.uint32(0))
            x_hi, x_lo = _unpack_bf16_pairs(xp)
            x_hi = x_hi.astype(BF16)
            x_lo = x_lo.astype(BF16)
            g = _dot(x_hi, wg_s[0:k, :]) + _dot(x_lo, wg_s[k:2 * k, :])
            u = _dot(x_hi, wu_s[0:k, :]) + _dot(x_lo, wu_s[k:2 * k, :])
            hb = (jax.nn.silu(g) * u).astype(BF16)
            return _pack_bf16_pairs(_dot(hb, wd_s[...]))

        def unit(j, width):
            b = b0 + j
            for i in range(width):
                copy_in(b + i).wait()
            for i in range(width):
                @pl.when(b + i + IN_AHEAD < nblk)
                def _(i=i):
                    copy_in(b + i + IN_AHEAD).start()
            ys = [mlp(b + i, j + i) for i in range(width)]
            for i in range(width):
                @pl.when(b + i >= OUT_SLOTS)
                def _(i=i):
                    copy_out(b + i - OUT_SLOTS).wait()
            for i in range(width):
                y_s[lax.rem(b + i, OUT_SLOTS)] = ys[i]
                copy_out(b + i).start()

        def pair(jj, c):
            unit(2 * jj, 2)
            return c

        lax.fori_loop(0, nb // 2, pair, 0)

        @pl.when(nb % 2 == 1)
        def _():
            unit(nb - 1, 1)

    @pl.when(e == pl.num_programs(0) - 1)
    def _():
        for back in range(OUT_SLOTS, 0, -1):
            @pl.when(nblk >= back)
            def _(back=back):
                copy_out(nblk - back).wait()

    @pl.when(e == pl.num_programs(0) - 1)
    def _():
        y_s[0] = jnp.zeros_like(y_s[0])

        def zero(b):
            return pltpu.make_async_copy(y_s.at[0], ys_ref.at[rows(b)], sem.at[1, 0])

        def start_zero(b, c):
            zero(b).start()
            return c

        def wait_zero(b, c):
            zero(b).wait()
            return c

        lax.fori_loop(nblk_ref[0], nb_max, start_zero, 0)
        lax.fori_loop(nblk_ref[0], nb_max, wait_zero, 0)


def _experts(estart, ecount, nblk, cnt, xs, w_gate, w_up, w_down, layer):
    p, k = xs.shape
    d, f = w_gate.shape[2], w_gate.shape[3]

    def w_map(e, es, ec, nb, cn):
        return (layer, e, 0, 0)

    grid_spec = pltpu.PrefetchScalarGridSpec(
        num_scalar_prefetch=4,
        grid=(N_EXPERTS,),
        in_specs=[pl.BlockSpec(memory_space=pl.ANY),
                  pl.BlockSpec((None, None, d, f), w_map),
                  pl.BlockSpec((None, None, d, f), w_map),
                  pl.BlockSpec((None, None, f, d), w_map)],
        out_specs=pl.BlockSpec(memory_space=pl.ANY),
        scratch_shapes=[pltpu.VMEM((IN_SLOTS, ROUTE_ROWS, k), U32),
                        pltpu.VMEM((OUT_SLOTS, ROUTE_ROWS, k), U32),
                        pltpu.VMEM((d, f), BF16), pltpu.VMEM((d, f), BF16), pltpu.VMEM((f, d), BF16),
                        pltpu.SemaphoreType.DMA((2, max(IN_SLOTS, OUT_SLOTS)))],
    )
    return pl.pallas_call(
        _expert_kernel,
        grid_spec=grid_spec,
        out_shape=jax.ShapeDtypeStruct((p, k), U32),
        compiler_params=_cparams("arbitrary"),
        name="expert_mlp",
    )(estart, ecount, nblk, cnt, xs, w_gate, w_up, w_down)


def _sc_gather_pair(table, idx0, idx1, start, count):
    k = table.shape[1]
    per_w = count // (SC_CORES * SC_SUBCORES)
    assert per_w % SC_CHUNK == 0 and per_w * SC_CORES * SC_SUBCORES == count
    assert start % SC_CHUNK == 0
    rows = jax.ShapeDtypeStruct((count, k), table.dtype)

    @functools.partial(
        pl.kernel, mesh=_sc_mesh(), out_type=(rows, rows),
        scratch_types=[pltpu.VMEM((SC_CHUNK,), I32), pltpu.VMEM((SC_CHUNK,), I32),
                       pltpu.VMEM((SC_CHUNK, k), table.dtype), pltpu.VMEM((SC_CHUNK, k), table.dtype),
                       pltpu.SemaphoreType.DMA, pltpu.SemaphoreType.DMA,
                       pltpu.SemaphoreType.DMA, pltpu.SemaphoreType.DMA],
        name="sc_gather_pair")
    def gather(table_hbm, i0_hbm, i1_hbm, o0_hbm, o1_hbm, i0_v, i1_v, r0_v, r1_v,
               sem0, sem1, sem2, sem3):
        base = (lax.axis_index("s") * SC_CORES + lax.axis_index("c")) * per_w

        @pl.loop(0, per_w // SC_CHUNK)
        def _(c):
            dst = pl.multiple_of(base + c * SC_CHUNK, SC_CHUNK)
            src = pl.multiple_of(start + dst, SC_CHUNK)
            pltpu.sync_copy(i0_hbm.at[pl.ds(src, SC_CHUNK)], i0_v)
            pltpu.sync_copy(i1_hbm.at[pl.ds(src, SC_CHUNK)], i1_v)
            get0 = pltpu.async_copy(table_hbm.at[i0_v], r0_v, sem0)
            get1 = pltpu.async_copy(table_hbm.at[i1_v], r1_v, sem1)
            get0.wait()
            put0 = pltpu.async_copy(r0_v, o0_hbm.at[pl.ds(dst, SC_CHUNK)], sem2)
            get1.wait()
            put1 = pltpu.async_copy(r1_v, o1_hbm.at[pl.ds(dst, SC_CHUNK)], sem3)
            put0.wait()
            put1.wait()

    return gather(table, idx0, idx1)


def _combine_kernel(*refs, final):
    (ya_ref, yb_ref, h1_ref, meta_ref, p_ref, gp_ref, wpg_ref, bpg_ref, wp_ref,
     gf_ref) = refs[:10]
    out_ref = refs[-1]
    meta = meta_ref[...]
    w0 = meta[:, M_W0:M_W0 + 1]
    w1 = meta[:, M_W1:M_W1 + 1]
    ya_hi, ya_lo = _unpack_bf16_pairs(ya_ref[...])
    yb_hi, yb_lo = _unpack_bf16_pairs(yb_ref[...])
    y = jnp.concatenate([ya_hi * w0 + yb_hi * w1, ya_lo * w0 + yb_lo * w1], axis=1)
    h2 = h1_ref[...] + y
    xn = _rms(h2, gp_ref[...]).astype(BF16)
    gate = jax.nn.sigmoid(_dot(xn, wpg_ref[...]) + bpg_ref[...])
    pp = _dot(p_ref[...].astype(BF16), wp_ref[...])
    h3 = h2 + gate * pp
    if final:
        h3 = _rms(h3, gf_ref[...])
    out_ref[...] = h3


def _combine(dest, ys, h1, meta, p, layer, g_ple, w_ple_gate, b_ple_gate, w_ple, g_final, final):
    n, d = h1.shape
    pd = p.shape[2]
    k = ys.shape[1]
    n_part = n // COMBINE_PARTS
    tmc = _tile(n_part, 512)
    tiles = n_part // tmc
    vec = pl.BlockSpec((1, d), lambda i: (0, 0))
    out = None
    for q in range(COMBINE_PARTS):
        ya, yb = _sc_gather_pair(ys, dest[0], dest[1], q * n_part, n_part)

        def tok(i, q=q):
            return (q * tiles + i, 0)

        in_specs = [pl.BlockSpec((tmc, k), lambda i: (i, 0)),
                    pl.BlockSpec((tmc, k), lambda i: (i, 0)),
                    pl.BlockSpec((tmc, d), tok),
                    pl.BlockSpec((tmc, LANES), tok),
                    pl.BlockSpec((None, tmc, pd), lambda i, q=q: (layer, q * tiles + i, 0)),
                    vec,
                    pl.BlockSpec((d, d), lambda i: (0, 0)),
                    vec,
                    pl.BlockSpec((pd, d), lambda i: (0, 0)),
                    vec]
        args = [ya, yb, h1, meta, p, g_ple, w_ple_gate, b_ple_gate, w_ple, g_final]
        aliases = {}
        if out is not None:
            in_specs.append(pl.BlockSpec(memory_space=pl.ANY))
            args.append(out)
            aliases = {len(args) - 1: 0}
        out = pl.pallas_call(
            functools.partial(_combine_kernel, final=final),
            grid=(tiles,),
            in_specs=in_specs,
            out_specs=pl.BlockSpec((tmc, d), tok),
            out_shape=jax.ShapeDtypeStruct((n, d), F32),
            input_output_aliases=aliases,
            compiler_params=_cparams("parallel"),
            name="combine_ple",
        )(*args)
    return out


def _row(v):
    return v.reshape(1, -1)


def _moe_and_ple(z, w_out, h, i, final, norm_ffn, w_router_group, b_router_group, w_router_expert,
                 b_router_expert, w_exp_gate, w_exp_up, w_exp_down, norm_ple, w_ple, w_ple_gate,
                 b_ple_gate, norm_final, p):
    n, d = h.shape
    pad = LANES - N_GROUPS - N_EXPERTS
    w_router = jnp.concatenate(
        [w_router_group[i], w_router_expert[i], jnp.zeros((d, pad), F32)], axis=1)
    b_router = jnp.concatenate(
        [b_router_group[i], b_router_expert[i], jnp.zeros((pad,), F32)]).reshape(1, LANES)
    h1, xn, meta, meta_t, cnt = _post_mix(z, w_out, h, _row(norm_ffn[i]), w_router, b_router)
    cnt_i32 = cnt[0, EXPERT_LANE0:EXPERT_LANE0 + N_EXPERTS].astype(I32)
    nb_max = (2 * n) // ROUTE_ROWS + N_EXPERTS
    dest, estart, ecount, nblk = _plan(cnt_i32, meta_t)
    xs = _sc_scatter_rows(xn, dest[0], dest[1], nb_max * ROUTE_ROWS)
    ys = _experts(estart, ecount, nblk, cnt_i32, xs, w_exp_gate, w_exp_up, w_exp_down, i)
    return _combine(dest, ys, h1, meta, p.reshape(p.shape[0], n, -1), i, _row(norm_ple[i]),
                    w_ple_gate[i].astype(BF16), _row(b_ple_gate[i]), w_ple[i].astype(BF16),
                    _row(norm_final), final)


def kernel(x, p, norm_mix, w_in_a, conv_a, w_out_a, w_in_b, conv_b, conv_bias_b, w_rgate_b, b_rgate_b, w_igate_b, b_igate_b, lam_b, w_out_b, norm_ffn, w_router_group, b_router_group, w_router_expert, b_router_expert, w_exp_gate, w_exp_up, w_exp_down, norm_ple, w_ple, w_ple_gate, b_ple_gate, norm_final):
    batch, seq, d = x.shape
    depth = p.shape[0]
    n = batch * seq
    h = x.reshape(n, d)
    for i in range(depth):
        j = i // 2
        if i % 2 == 0:
            bg, v = _mix_a_in(h, _row(norm_mix[i]), w_in_a[j].astype(BF16))
            z = _mix_a_conv(v, bg, conv_a[j], seq)
            w_out = w_out_a[j]
        else:
            tt = _tile(seq, 256)
            yb, u_tiles = _mix_b_in(h, _row(norm_mix[i]), w_in_b[j].astype(BF16), batch, seq, tt)
            z = _rglru(u_tiles, yb, conv_b[j], _row(conv_bias_b[j]),
                       w_rgate_b[j].astype(BF16), b_rgate_b[j], w_igate_b[j].astype(BF16),
                       b_igate_b[j], lam_b[j], seq, tt)
            w_out = w_out_b[j]
        h = _moe_and_ple(z, w_out.astype(BF16), h, i, i == depth - 1, norm_ffn, w_router_group,
                         b_router_group, w_router_expert, b_router_expert, w_exp_gate, w_exp_up,
                         w_exp_down, norm_ple, w_ple, w_ple_gate, b_ple_gate, norm_final, p)
    return h.reshape(batch, seq, d)
```

```python
import functools

import jax
import jax.numpy as jnp
from jax import lax
from jax.experimental import pallas as pl
from jax.experimental.pallas import tpu as pltpu
from jax.experimental.pallas import tpu_sc as plsc

F32 = jnp.float32
BF16 = jnp.bfloat16
I32 = jnp.int32
U32 = jnp.uint32

EPS = 1e-6
N_GROUPS = 4
EXPERTS_PER_GROUP = 8
N_EXPERTS = N_GROUPS * EXPERTS_PER_GROUP
RG_C = 8.0

LANES = 128
SUBLANES = 8
BF16_ROWS = 16
EXPERT_LANE0 = N_GROUPS
ROUTE_ROWS = 256
IN_AHEAD = 4
IN_SLOTS = IN_AHEAD + 2
OUT_SLOTS = 4
SC_CORES = 2
SC_SUBCORES = 16
SC_CHUNK = 64
COMBINE_PARTS = 4
M_E0, M_E1, M_R0, M_R1, M_W0, M_W1 = 0, 1, 2, 3, 4, 5
VMEM_LIMIT = 48 * 1024 * 1024


def _cparams(*sem):
    return pltpu.CompilerParams(dimension_semantics=sem, vmem_limit_bytes=VMEM_LIMIT)


def _rms(x, g):
    return x * lax.rsqrt(jnp.mean(x * x, axis=-1, keepdims=True) + EPS) * g


def _dot(a, b):
    return jnp.dot(a, b, preferred_element_type=F32)


def _pack_bf16_pairs(x):
    k = x.shape[1] // 2
    hi = lax.bitcast_convert_type(x[:, :k].astype(BF16).astype(F32), U32)
    lo = lax.bitcast_convert_type(x[:, k:].astype(BF16).astype(F32), U32)
    return hi | (lo >> 16)


def _unpack_bf16_pairs(p):
    hi = lax.bitcast_convert_type(p & jnp.uint32(0xFFFF0000), F32)
    lo = lax.bitcast_convert_type(p << 16, F32)
    return hi, lo


def _tile(n, want):
    t = min(n, want)
    assert n % t == 0, (n, want)
    return t


def _mix_a_in_kernel(h_ref, g_ref, w_ref, bg_ref, v_ref):
    d = h_ref.shape[1]
    xn = _rms(h_ref[...], g_ref[...]).astype(BF16)
    bg = _dot(xn, w_ref[:, 0:d])
    cg = _dot(xn, w_ref[:, d:2 * d])
    hh = _dot(xn, w_ref[:, 2 * d:3 * d])
    bg_ref[...] = bg.astype(BF16)
    v_ref[...] = (cg * hh).astype(BF16)


def _mix_a_in(h, g, w_in):
    n, d = h.shape
    tm = _tile(n, 512)
    return pl.pallas_call(
        _mix_a_in_kernel,
        grid=(n // tm,),
        in_specs=[pl.BlockSpec((tm, d), lambda i: (i, 0)),
                  pl.BlockSpec((1, d), lambda i: (0, 0)),
                  pl.BlockSpec((d, 3 * d), lambda i: (0, 0))],
        out_specs=[pl.BlockSpec((tm, d), lambda i: (i, 0)),
                   pl.BlockSpec((tm, d), lambda i: (i, 0))],
        out_shape=[jax.ShapeDtypeStruct((n, d), BF16), jax.ShapeDtypeStruct((n, d), BF16)],
        compiler_params=_cparams("parallel"),
        name="mix_a_in",
    )(h, g, w_in)


def _mix_a_conv_kernel(v_ref, vp_ref, vn_ref, bg_ref, cw_ref, z_ref, *, tiles_per_seq):
    i = pl.program_id(0)
    tm = v_ref.shape[0]
    v = v_ref[...].astype(F32)
    at_start = (i % tiles_per_seq) == 0
    at_end = (i % tiles_per_seq) == tiles_per_seq - 1
    prev_row = vp_ref[...].astype(F32)[BF16_ROWS - 1:BF16_ROWS, :]
    next_row = vn_ref[...].astype(F32)[0:1, :]
    prev_row = jnp.where(at_start, 0.0, prev_row)
    next_row = jnp.where(at_end, 0.0, next_row)
    row = lax.broadcasted_iota(I32, (tm, 1), 0)
    v_dn = jnp.where(row == 0, prev_row, pltpu.roll(v, 1, 0))
    v_up = jnp.where(row == tm - 1, next_row, pltpu.roll(v, tm - 1, 0))
    cw = cw_ref[...]
    u = cw[0:1, :] * v_dn + cw[1:2, :] * v + cw[2:3, :] * v_up
    z_ref[...] = (bg_ref[...].astype(F32) * u).astype(BF16)


def _mix_a_conv(v, bg, conv_w, seq):
    n, d = v.shape
    tm = _tile(seq, 512)
    hb = tm // BF16_ROWS
    nhalo = n // BF16_ROWS
    return pl.pallas_call(
        functools.partial(_mix_a_conv_kernel, tiles_per_seq=seq // tm),
        grid=(n // tm,),
        in_specs=[pl.BlockSpec((tm, d), lambda i: (i, 0)),
                  pl.BlockSpec((BF16_ROWS, d), lambda i: (jnp.maximum(i * hb - 1, 0), 0)),
                  pl.BlockSpec((BF16_ROWS, d), lambda i: (jnp.minimum((i + 1) * hb, nhalo - 1), 0)),
                  pl.BlockSpec((tm, d), lambda i: (i, 0)),
                  pl.BlockSpec(conv_w.shape, lambda i: (0, 0))],
        out_specs=pl.BlockSpec((tm, d), lambda i: (i, 0)),
        out_shape=jax.ShapeDtypeStruct((n, d), BF16),
        compiler_params=_cparams("parallel"),
        name="mix_a_conv",
    )(v, v, v, bg, conv_w)


def _mix_b_in_kernel(h_ref, hp_ref, hn_ref, g_ref, w_ref, yb_ref, u_ref, *, steps_per_seq, tt):
    i = pl.program_id(0)
    r = yb_ref.shape[1]
    tiles = h_ref.shape[0] // tt
    g = g_ref[...]
    xn = _rms(h_ref[...], g).astype(BF16)
    yb_ref[...] = jax.nn.gelu(_dot(xn, w_ref[:, 0:r])).astype(BF16)
    u = _dot(xn, w_ref[:, r:2 * r])
    xh = _rms(jnp.concatenate([hp_ref[...], hn_ref[...]], axis=0), g).astype(BF16)
    uh = _dot(xh, w_ref[:, r:2 * r])
    at_start = (i % steps_per_seq) == 0
    at_end = (i % steps_per_seq) == steps_per_seq - 1
    prev2 = jnp.where(at_start, 0.0, uh[SUBLANES - 2:SUBLANES, :])
    next1 = jnp.where(at_end, 0.0, uh[SUBLANES:SUBLANES + 1, :])
    row = lax.broadcasted_iota(I32, (SUBLANES, 1), 0)
    for k in range(tiles):
        p2 = prev2 if k == 0 else u[k * tt - 2:k * tt]
        n1 = next1 if k == tiles - 1 else u[(k + 1) * tt:(k + 1) * tt + 1]
        pad = jnp.where(row == 0, p2[0:1], jnp.where(row == 1, p2[1:2], jnp.where(row == 2, n1, 0.0)))
        for s in range(r // LANES):
            lanes = slice(s * LANES, (s + 1) * LANES)
            u_ref[k, s, 0:tt, :] = u[k * tt:(k + 1) * tt, lanes]
            u_ref[k, s, tt:tt + SUBLANES, :] = pad[:, lanes]


def _mix_b_in(h, g, w_in, batch, seq, tt):
    n, d = h.shape
    r = w_in.shape[1] // 2
    tm = _tile(seq, 512)
    steps_per_seq = seq // tm
    tiles = tm // tt
    hb = tm // SUBLANES
    nhalo = n // SUBLANES
    return pl.pallas_call(
        functools.partial(_mix_b_in_kernel, steps_per_seq=steps_per_seq, tt=tt),
        grid=(n // tm,),
        in_specs=[pl.BlockSpec((tm, d), lambda i: (i, 0)),
                  pl.BlockSpec((SUBLANES, d), lambda i: (jnp.maximum(i * hb - 1, 0), 0)),
                  pl.BlockSpec((SUBLANES, d), lambda i: (jnp.minimum((i + 1) * hb, nhalo - 1), 0)),
                  pl.BlockSpec((1, d), lambda i: (0, 0)),
                  pl.BlockSpec((d, 2 * r), lambda i: (0, 0))],
        out_specs=[pl.BlockSpec((tm, r), lambda i: (i, 0)),
                   pl.BlockSpec((tiles, r // LANES, None, tt + SUBLANES, LANES),
                                lambda i: (i % steps_per_seq, 0, i // steps_per_seq, 0, 0))],
        out_shape=[jax.ShapeDtypeStruct((n, r), BF16),
                   jax.ShapeDtypeStruct((seq // tt, r // LANES, batch, tt + SUBLANES, LANES), F32)],
        compiler_params=_cparams("parallel"),
        name="mix_b_in",
    )(h, h, h, g, w_in)


def _rglru_gates(u, wr_ref, br_ref, wi_ref, bi_ref, lam_ref, a_s, g_s):
    ub = u.astype(BF16)
    t_r = jnp.tanh(_dot(ub, wr_ref[...]) + 0.5 * br_ref[...])
    t_i = jnp.tanh(_dot(ub, wi_ref[...]) + 0.5 * bi_ref[...])
    c = (-0.25 * RG_C) * jax.nn.softplus(-lam_ref[...])
    th = jnp.tanh(c + c * t_r)
    q = 1.0 / (1.0 - th)
    a_s[...] = (1.0 + th) * q
    nth = -th
    root = jnp.where(nth > 0.0, nth * lax.rsqrt(nth), 0.0)
    g_s[...] = (q * root) * (u + u * t_i)


def _rglru_fwd_kernel(u_ref, cw_ref, cb_ref, wr_ref, br_ref, wi_ref, bi_ref, lam_ref,
                      hf_ref, uc_ref, x_s, a_s, g_s, carry_s, *, tt):
    nb = SUBLANES
    ttp = tt + SUBLANES
    rows = tt * nb
    slabs = u_ref.shape[0]

    @pl.when(pl.program_id(1) == 0)
    def _():
        carry_s[...] = jnp.zeros_like(carry_s)

    def gather_t(tl, dst):
        for s in range(slabs):
            x_s[pl.ds(dst, nb), s * LANES:(s + 1) * LANES] = u_ref[s, pl.ds(tl, nb, stride=ttp), :]

    def load_t(tl, c):
        gather_t(tl, pl.multiple_of((tl + 2) * nb, nb))
        return c

    lax.fori_loop(0, tt, load_t, 0, unroll=8)
    gather_t(tt, 0)
    gather_t(tt + 1, nb)
    gather_t(tt + 2, (tt + 2) * nb)

    cw = cw_ref[...]
    u = (cw[0:1, :] * x_s[0:rows, :] + cw[1:2, :] * x_s[nb:nb + rows, :]
         + cw[2:3, :] * x_s[2 * nb:2 * nb + rows, :] + cw[3:4, :] * x_s[3 * nb:3 * nb + rows, :]
         + cb_ref[...])
    uc_ref[...] = u
    _rglru_gates(u, wr_ref, br_ref, wi_ref, bi_ref, lam_ref, a_s, g_s)

    def step(tl, h):
        base = pl.multiple_of(tl * nb, nb)
        h = a_s[pl.ds(base, nb), :] * h + g_s[pl.ds(base, nb), :]
        hf_ref[pl.ds(base, nb), :] = h
        return h

    carry_s[...] = lax.fori_loop(0, tt, step, carry_s[...], unroll=8)


def _rglru_rev_kernel(uc_ref, wr_ref, br_ref, wi_ref, bi_ref, lam_ref, hf_ref, yb_ref,
                      z_ref, a_s, g_s, carry_s, nat_s, *, tt):
    nb = SUBLANES
    ttp = tt + SUBLANES
    slabs = nat_s.shape[0]

    @pl.when(pl.program_id(1) == 0)
    def _():
        carry_s[...] = jnp.zeros_like(carry_s)

    _rglru_gates(uc_ref[...], wr_ref, br_ref, wi_ref, bi_ref, lam_ref, a_s, g_s)

    def step(k, h):
        tl = tt - 1 - k
        base = pl.multiple_of(tl * nb, nb)
        h = a_s[pl.ds(base, nb), :] * h + g_s[pl.ds(base, nb), :]
        hs = h + hf_ref[pl.ds(base, nb), :]
        for s in range(slabs):
            nat_s[s, pl.ds(tl, nb, stride=ttp), :] = hs[:, s * LANES:(s + 1) * LANES]
        return h

    carry_s[...] = lax.fori_loop(0, tt, step, carry_s[...], unroll=8)
    for b in range(nb):
        for s in range(slabs):
            lanes = slice(s * LANES, (s + 1) * LANES)
            hb = nat_s[s, b * ttp:b * ttp + tt, :]
            z_ref[b, :, lanes] = (yb_ref[b, :, lanes].astype(F32) * hb).astype(BF16)


def _rglru(u_tiles, yb, conv_w, conv_b, w_r, b_r, w_i, b_i, lam, seq, tt):
    nt, nslab, batch, ttp, _ = u_tiles.shape
    r = nslab * LANES
    assert batch == SUBLANES, "time-major tile layout puts the batch on the 8 sublanes"
    nh, bk = w_r.shape[1], w_r.shape[2]
    slabs = bk // LANES
    kw = conv_w.shape[0]
    rows = tt * batch
    gates = [w_r, b_r.reshape(2, 1, r), w_i, b_i.reshape(2, 1, r), lam.reshape(2, 1, r)]

    def gate_specs(d):
        vec = pl.BlockSpec((None, 1, bk), lambda h, jj: (d, 0, h))
        mat = pl.BlockSpec((None, None, bk, bk), lambda h, jj: (d, h, 0, 0))
        return [mat, vec, mat, vec, vec]

    tm_spec = pl.BlockSpec((rows, bk), lambda h, jj: (jj, h))
    tm_rev = pl.BlockSpec((rows, bk), lambda h, jj: (nt - 1 - jj, h))
    tm_shape = jax.ShapeDtypeStruct((seq * batch, r), F32)
    state = [pltpu.VMEM((rows, bk), F32), pltpu.VMEM((rows, bk), F32), pltpu.VMEM((batch, bk), F32)]

    hf, uc = pl.pallas_call(
        functools.partial(_rglru_fwd_kernel, tt=tt),
        grid=(nh, nt),
        in_specs=[pl.BlockSpec((None, slabs, batch * ttp, LANES), lambda h, jj: (jj, h, 0, 0)),
                  pl.BlockSpec((kw, bk), lambda h, jj: (0, h)),
                  pl.BlockSpec((1, bk), lambda h, jj: (0, h))] + gate_specs(0),
        out_specs=[tm_spec, tm_spec],
        out_shape=[tm_shape, tm_shape],
        scratch_shapes=[pltpu.VMEM(((tt + 3) * batch, bk), F32)] + state,
        compiler_params=_cparams("parallel", "arbitrary"),
        name="rglru_fwd",
    )(u_tiles.reshape(nt, nslab, batch * ttp, LANES), conv_w, conv_b, *gates)

    nat_spec = pl.BlockSpec((batch, tt, bk), lambda h, jj: (0, nt - 1 - jj, h))
    z = pl.pallas_call(
        functools.partial(_rglru_rev_kernel, tt=tt),
        grid=(nh, nt),
        in_specs=[tm_rev] + gate_specs(1) + [tm_rev, nat_spec],
        out_specs=nat_spec,
        out_shape=jax.ShapeDtypeStruct((batch, seq, r), BF16),
        scratch_shapes=state + [pltpu.VMEM((slabs, batch * ttp, LANES), F32)],
        compiler_params=_cparams("parallel", "arbitrary"),
        name="rglru_rev",
    )(uc, *gates, hf, yb.reshape(batch, seq, r))
    return z.reshape(-1, r)


def _route(logits):
    lane = lax.broadcasted_iota(I32, logits.shape, 1).astype(F32)
    neg = jnp.float32(-jnp.inf)
    nolane = jnp.float32(LANES)
    gmask = lane < N_GROUPS
    gmax = jnp.max(jnp.where(gmask, logits, neg), axis=1, keepdims=True)
    eg = jnp.where(gmask, jnp.exp(logits - gmax), 0.0)
    gsum = jnp.sum(eg, axis=1, keepdims=True)
    pg = eg / gsum
    pg_top = 1.0 / gsum
    g_top = jnp.min(jnp.where(gmask & (pg == pg_top), lane, nolane), axis=1, keepdims=True)
    lo = EXPERT_LANE0 + g_top * EXPERTS_PER_GROUP
    emask = (lane >= lo) & (lane < lo + EXPERTS_PER_GROUP)
    emax = jnp.max(jnp.where(emask, logits, neg), axis=1, keepdims=True)
    ee = jnp.where(emask, jnp.exp(logits - emax), 0.0)
    esum = jnp.sum(ee, axis=1, keepdims=True)
    pe = jnp.where(emask, ee / esum, -1.0)
    p1 = 1.0 / esum
    i1 = jnp.min(jnp.where(pe == p1, lane, nolane), axis=1, keepdims=True)
    pe2 = jnp.where(lane == i1, -1.0, pe)
    p2 = jnp.max(pe2, axis=1, keepdims=True)
    i2 = jnp.min(jnp.where(pe2 == p2, lane, nolane), axis=1, keepdims=True)
    psum = p1 + p2
    return i1, i2, pg_top * (p1 / psum), pg_top * (p2 / psum)


def _post_mix_kernel(z_ref, wo_ref, h_ref, g_ref, wr_ref, br_ref,
                     h1_ref, xn_ref, meta_ref, meta_t_ref, cnt_ref, carry_ref, logits_s):
    i = pl.program_id(0)
    tm = z_ref.shape[0]

    @pl.when(i == 0)
    def _():
        logits_s[...] = jnp.zeros_like(logits_s)

    @pl.when(i <= 1)
    def _():
        carry_ref[...] = jnp.zeros_like(carry_ref)

    logits = logits_s[...]

    h1 = h_ref[...] + _dot(z_ref[...], wo_ref[...])
    h1_ref[...] = h1
    xn = _rms(h1, g_ref[...])
    xn_ref[...] = _pack_bf16_pairs(xn)
    logits_s[...] = _dot(xn.astype(BF16), wr_ref[...].astype(BF16)) + br_ref[...]

    i1, i2, w0, w1 = _route(logits)

    lane = lax.broadcasted_iota(I32, (tm, LANES), 1).astype(F32)
    oh0 = lane == i1
    oh1 = lane == i2
    both = (oh0 | oh1).astype(BF16)
    ri = lax.broadcasted_iota(I32, (tm, tm), 0)
    ci = lax.broadcasted_iota(I32, (tm, tm), 1)
    before = (ri > ci).astype(BF16)
    cnt_before = _dot(before, both) + carry_ref[...]
    rank0 = jnp.sum(jnp.where(oh0, cnt_before, 0.0), axis=1, keepdims=True)
    rank1 = jnp.sum(jnp.where(oh1, cnt_before, 0.0), axis=1, keepdims=True)
    carry = carry_ref[...] + jnp.sum(both.astype(F32), axis=0, keepdims=True)
    carry_ref[...] = carry
    cnt_ref[...] = carry

    e0 = i1 - EXPERT_LANE0
    e1 = i2 - EXPERT_LANE0
    meta = jnp.zeros((tm, LANES), F32)
    for ln, val in ((M_E0, e0), (M_E1, e1), (M_R0, rank0), (M_R1, rank1), (M_W0, w0), (M_W1, w1)):
        meta = jnp.where(lane == ln, val, meta)
    meta_ref[...] = meta
    meta_t_ref[...] = meta.T[0:SUBLANES, :]


def _post_mix(z, w_out, h, g, w_router, b_router):
    n, d = h.shape
    k = z.shape[1]
    tm = _tile(n, 512)
    nt = n // tm

    def proj(i):
        return (jnp.minimum(i, nt - 1), 0)

    def routed(i):
        return (jnp.maximum(i - 1, 0), 0)

    return pl.pallas_call(
        _post_mix_kernel,
        grid=(nt + 1,),
        in_specs=[pl.BlockSpec((tm, k), proj),
                  pl.BlockSpec((k, d), lambda i: (0, 0)),
                  pl.BlockSpec((tm, d), proj),
                  pl.BlockSpec((1, d), lambda i: (0, 0)),
                  pl.BlockSpec((d, LANES), lambda i: (0, 0)),
                  pl.BlockSpec((1, LANES), lambda i: (0, 0))],
        out_specs=[pl.BlockSpec((tm, d), proj),
                   pl.BlockSpec((tm, d // 2), proj),
                   pl.BlockSpec((tm, LANES), routed),
                   pl.BlockSpec((SUBLANES, tm), lambda i: (0, jnp.maximum(i - 1, 0))),
                   pl.BlockSpec((1, LANES), lambda i: (0, 0))],
        out_shape=[jax.ShapeDtypeStruct((n, d), F32), jax.ShapeDtypeStruct((n, d // 2), U32),
                   jax.ShapeDtypeStruct((n, LANES), F32), jax.ShapeDtypeStruct((SUBLANES, n), F32),
                   jax.ShapeDtypeStruct((1, LANES), F32)],
        scratch_shapes=[pltpu.VMEM((1, LANES), F32), pltpu.VMEM((tm, LANES), F32)],
        compiler_params=_cparams("arbitrary"),
        name="post_mix_router",
    )(z, w_out, h, g, w_router, b_router)


def _plan_kernel(cnt_ref, meta_t_ref, dest_ref, estart_ref, ecount_ref, nblk_ref, pstart_ref):
    def per_expert(e, acc):
        nb_e = (cnt_ref[e] + (ROUTE_ROWS - 1)) // ROUTE_ROWS
        pstart_ref[e] = acc * ROUTE_ROWS
        estart_ref[e] = acc
        ecount_ref[e] = nb_e
        return acc + nb_e

    nblk_ref[0] = lax.fori_loop(0, N_EXPERTS, per_expert, 0)

    e0 = meta_t_ref[M_E0:M_E0 + 1, :]
    e1 = meta_t_ref[M_E1:M_E1 + 1, :]
    d0 = meta_t_ref[M_R0:M_R0 + 1, :]
    d1 = meta_t_ref[M_R1:M_R1 + 1, :]
    for e in range(N_EXPERTS):
        ps = pstart_ref[e].astype(F32)
        d0 = d0 + jnp.where(e0 == e, ps, 0.0)
        d1 = d1 + jnp.where(e1 == e, ps, 0.0)
    dest_ref[...] = jnp.zeros_like(dest_ref)
    dest_ref[0:1, :] = d0.astype(I32)
    dest_ref[1:2, :] = d1.astype(I32)


def _plan(cnt_i32, meta_t):
    n = meta_t.shape[1]
    smem = pl.BlockSpec(memory_space=pltpu.SMEM)
    return pl.pallas_call(
        _plan_kernel,
        in_specs=[smem, pl.BlockSpec((SUBLANES, n), lambda: (0, 0))],
        out_specs=[pl.BlockSpec((SUBLANES, n), lambda: (0, 0)), smem, smem, smem],
        out_shape=[jax.ShapeDtypeStruct((SUBLANES, n), I32),
                   jax.ShapeDtypeStruct((N_EXPERTS,), I32),
                   jax.ShapeDtypeStruct((N_EXPERTS,), I32),
                   jax.ShapeDtypeStruct((1,), I32)],
        scratch_shapes=[pltpu.SMEM((N_EXPERTS,), I32)],
        name="dispatch_plan",
    )(cnt_i32, meta_t)


def _sc_mesh():
    return plsc.VectorSubcoreMesh(core_axis_name="c", subcore_axis_name="s",
                                  num_cores=SC_CORES, num_subcores=SC_SUBCORES)


def _sc_scatter_rows(x, idx0, idx1, p_rows):
    m, k = x.shape
    per_w = m // (SC_CORES * SC_SUBCORES)
    assert per_w % SC_CHUNK == 0 and per_w * SC_CORES * SC_SUBCORES == m

    @functools.partial(
        pl.kernel, mesh=_sc_mesh(), out_type=jax.ShapeDtypeStruct((p_rows, k), x.dtype),
        scratch_types=[pltpu.VMEM((SC_CHUNK,), I32), pltpu.VMEM((SC_CHUNK,), I32),
                       pltpu.VMEM((SC_CHUNK, k), x.dtype),
                       pltpu.SemaphoreType.DMA, pltpu.SemaphoreType.DMA],
        name="sc_scatter_rows")
    def scatter(x_hbm, i0_hbm, i1_hbm, out_hbm, i0_v, i1_v, rows_v, sem0, sem1):
        base = (lax.axis_index("s") * SC_CORES + lax.axis_index("c")) * per_w

        @pl.loop(0, per_w // SC_CHUNK)
        def _(c):
            off = pl.multiple_of(base + c * SC_CHUNK, SC_CHUNK)
            pltpu.sync_copy(i0_hbm.at[pl.ds(off, SC_CHUNK)], i0_v)
            pltpu.sync_copy(i1_hbm.at[pl.ds(off, SC_CHUNK)], i1_v)
            pltpu.sync_copy(x_hbm.at[pl.ds(off, SC_CHUNK)], rows_v)
            put0 = pltpu.async_copy(rows_v, out_hbm.at[i0_v], sem0)
            put1 = pltpu.async_copy(rows_v, out_hbm.at[i1_v], sem1)
            put0.wait()
            put1.wait()

    return scatter(x, idx0, idx1)


def _expert_kernel(es_ref, ec_ref, nblk_ref, cnt_ref, xs_ref, wg_ref, wu_ref, wd_ref, ys_ref,
                   x_s, y_s, wg_s, wu_s, wd_s, sem):
    e = pl.program_id(0)
    nb = ec_ref[e]
    b0 = es_ref[e]
    k = x_s.shape[2]
    nb_max = ys_ref.shape[0] // ROUTE_ROWS

    def rows(b):
        return pl.ds(pl.multiple_of(b * ROUTE_ROWS, ROUTE_ROWS), ROUTE_ROWS)

    nblk = nblk_ref[0]

    def copy_in(b):
        slot = lax.rem(b, IN_SLOTS)
        return pltpu.make_async_copy(xs_ref.at[rows(b)], x_s.at[slot], sem.at[0, slot])

    def copy_out(b):
        slot = lax.rem(b, OUT_SLOTS)
        return pltpu.make_async_copy(y_s.at[slot], ys_ref.at[rows(b)], sem.at[1, slot])

    @pl.when(e == 0)
    def _():
        for b in range(IN_AHEAD):
            @pl.when(b < nblk)
            def _(b=b):
                copy_in(b).start()

    @pl.when(nb > 0)
    def _():
        wg_s[...] = wg_ref[...].astype(BF16)
        wu_s[...] = wu_ref[...].astype(BF16)
        wd_s[...] = wd_ref[...].astype(BF16)

        def mlp(b, j):
            row = lax.broadcasted_iota(I32, (ROUTE_ROWS, 1), 0)
            live = row < cnt_ref[e] - j * ROUTE_ROWS
            xp = jnp.where(live, x_s[lax.rem(b, IN_SLOTS)], jnp.uint32(0))
            x_hi, x_lo = _unpack_bf16_pairs(xp)
            x_hi = x_hi.astype(BF16)
            x_lo = x_lo.astype(BF16)
            g = _dot(x_hi, wg_s[0:k, :]) + _dot(x_lo, wg_s[k:2 * k, :])
            u = _dot(x_hi, wu_s[0:k, :]) + _dot(x_lo, wu_s[k:2 * k, :])
            hb = (jax.nn.silu(g) * u).astype(BF16)
            return _pack_bf16_pairs(_dot(hb, wd_s[...]))

        def unit(j, width):
            b = b0 + j
            for i in range(width):
                copy_in(b + i).wait()
            for i in range(width):
                @pl.when(b + i + IN_AHEAD < nblk)
                def _(i=i):
                    copy_in(b + i + IN_AHEAD).start()
            ys = [mlp(b + i, j + i) for i in range(width)]
            for i in range(width):
                @pl.when(b + i >= OUT_SLOTS)
                def _(i=i):
                    copy_out(b + i - OUT_SLOTS).wait()
            for i in range(width):
                y_s[lax.rem(b + i, OUT_SLOTS)] = ys[i]
                copy_out(b + i).start()

        def pair(jj, c):
            unit(2 * jj, 2)
            return c

        lax.fori_loop(0, nb // 2, pair, 0)

        @pl.when(nb % 2 == 1)
        def _():
            unit(nb - 1, 1)

    @pl.when(e == pl.num_programs(0) - 1)
    def _():
        for back in range(OUT_SLOTS, 0, -1):
            @pl.when(nblk >= back)
            def _(back=back):
                copy_out(nblk - back).wait()

    @pl.when(e == pl.num_programs(0) - 1)
    def _():
        y_s[0] = jnp.zeros_like(y_s[0])

        def zero(b):
            return pltpu.make_async_copy(y_s.at[0], ys_ref.at[rows(b)], sem.at[1, 0])

        def start_zero(b, c):
            zero(b).start()
            return c

        def wait_zero(b, c):
            zero(b).wait()
            return c

        lax.fori_loop(nblk_ref[0], nb_max, start_zero, 0)
        lax.fori_loop(nblk_ref[0], nb_max, wait_zero, 0)


def _experts(estart, ecount, nblk, cnt, xs, w_gate, w_up, w_down, layer):
    p, k = xs.shape
    d, f = w_gate.shape[2], w_gate.shape[3]

    def w_map(e, es, ec, nb, cn):
        return (layer, e, 0, 0)

    grid_spec = pltpu.PrefetchScalarGridSpec(
        num_scalar_prefetch=4,
        grid=(N_EXPERTS,),
        in_specs=[pl.BlockSpec(memory_space=pl.ANY),
                  pl.BlockSpec((None, None, d, f), w_map),
                  pl.BlockSpec((None, None, d, f), w_map),
                  pl.BlockSpec((None, None, f, d), w_map)],
        out_specs=pl.BlockSpec(memory_space=pl.ANY),
        scratch_shapes=[pltpu.VMEM((IN_SLOTS, ROUTE_ROWS, k), U32),
                        pltpu.VMEM((OUT_SLOTS, ROUTE_ROWS, k), U32),
                        pltpu.VMEM((d, f), BF16), pltpu.VMEM((d, f), BF16), pltpu.VMEM((f, d), BF16),
                        pltpu.SemaphoreType.DMA((2, max(IN_SLOTS, OUT_SLOTS)))],
    )
    return pl.pallas_call(
        _expert_kernel,
        grid_spec=grid_spec,
        out_shape=jax.ShapeDtypeStruct((p, k), U32),
        compiler_params=_cparams("arbitrary"),
        name="expert_mlp",
    )(estart, ecount, nblk, cnt, xs, w_gate, w_up, w_down)


def _sc_gather_pair(table, idx0, idx1, start, count):
    k = table.shape[1]
    per_w = count // (SC_CORES * SC_SUBCORES)
    assert per_w % SC_CHUNK == 0 and per_w * SC_CORES * SC_SUBCORES == count
    assert start % SC_CHUNK == 0
    rows = jax.ShapeDtypeStruct((count, k), table.dtype)

    @functools.partial(
        pl.kernel, mesh=_sc_mesh(), out_type=(rows, rows),
        scratch_types=[pltpu.VMEM((SC_CHUNK,), I32), pltpu.VMEM((SC_CHUNK,), I32),
                       pltpu.VMEM((SC_CHUNK, k), table.dtype), pltpu.VMEM((SC_CHUNK, k), table.dtype),
                       pltpu.SemaphoreType.DMA, pltpu.SemaphoreType.DMA,
                       pltpu.SemaphoreType.DMA, pltpu.SemaphoreType.DMA],
        name="sc_gather_pair")
    def gather(table_hbm, i0_hbm, i1_hbm, o0_hbm, o1_hbm, i0_v, i1_v, r0_v, r1_v,
               sem0, sem1, sem2, sem3):
        base = (lax.axis_index("s") * SC_CORES + lax.axis_index("c")) * per_w

        @pl.loop(0, per_w // SC_CHUNK)
        def _(c):
            dst = pl.multiple_of(base + c * SC_CHUNK, SC_CHUNK)
            src = pl.multiple_of(start + dst, SC_CHUNK)
            pltpu.sync_copy(i0_hbm.at[pl.ds(src, SC_CHUNK)], i0_v)
            pltpu.sync_copy(i1_hbm.at[pl.ds(src, SC_CHUNK)], i1_v)
            get0 = pltpu.async_copy(table_hbm.at[i0_v], r0_v, sem0)
            get1 = pltpu.async_copy(table_hbm.at[i1_v], r1_v, sem1)
            get0.wait()
            put0 = pltpu.async_copy(r0_v, o0_hbm.at[pl.ds(dst, SC_CHUNK)], sem2)
            get1.wait()
            put1 = pltpu.async_copy(r1_v, o1_hbm.at[pl.ds(dst, SC_CHUNK)], sem3)
            put0.wait()
            put1.wait()

    return gather(table, idx0, idx1)


def _combine_kernel(*refs, final):
    (ya_ref, yb_ref, h1_ref, meta_ref, p_ref, gp_ref, wpg_ref, bpg_ref, wp_ref,
     gf_ref) = refs[:10]
    out_ref = refs[-1]
    meta = meta_ref[...]
    w0 = meta[:, M_W0:M_W0 + 1]
    w1 = meta[:, M_W1:M_W1 + 1]
    ya_hi, ya_lo = _unpack_bf16_pairs(ya_ref[...])
    yb_hi, yb_lo = _unpack_bf16_pairs(yb_ref[...])
    y = jnp.concatenate([ya_hi * w0 + yb_hi * w1, ya_lo * w0 + yb_lo * w1], axis=1)
    h2 = h1_ref[...] + y
    xn = _rms(h2, gp_ref[...]).astype(BF16)
    gate = jax.nn.sigmoid(_dot(xn, wpg_ref[...]) + bpg_ref[...])
    pp = _dot(p_ref[...].astype(BF16), wp_ref[...])
    h3 = h2 + gate * pp
    if final:
        h3 = _rms(h3, gf_ref[...])
    out_ref[...] = h3


def _combine(dest, ys, h1, meta, p, layer, g_ple, w_ple_gate, b_ple_gate, w_ple, g_final, final):
    n, d = h1.shape
    pd = p.shape[2]
    k = ys.shape[1]
    n_part = n // COMBINE_PARTS
    tmc = _tile(n_part, 512)
    tiles = n_part // tmc
    vec = pl.BlockSpec((1, d), lambda i: (0, 0))
    out = None
    for q in range(COMBINE_PARTS):
        ya, yb = _sc_gather_pair(ys, dest[0], dest[1], q * n_part, n_part)

        def tok(i, q=q):
            return (q * tiles + i, 0)

        in_specs = [pl.BlockSpec((tmc, k), lambda i: (i, 0)),
                    pl.BlockSpec((tmc, k), lambda i: (i, 0)),
                    pl.BlockSpec((tmc, d), tok),
                    pl.BlockSpec((tmc, LANES), tok),
                    pl.BlockSpec((None, tmc, pd), lambda i, q=q: (layer, q * tiles + i, 0)),
                    vec,
                    pl.BlockSpec((d, d), lambda i: (0, 0)),
                    vec,
                    pl.BlockSpec((pd, d), lambda i: (0, 0)),
                    vec]
        args = [ya, yb, h1, meta, p, g_ple, w_ple_gate, b_ple_gate, w_ple, g_final]
        aliases = {}
        if out is not None:
            in_specs.append(pl.BlockSpec(memory_space=pl.ANY))
            args.append(out)
            aliases = {len(args) - 1: 0}
        out = pl.pallas_call(
            functools.partial(_combine_kernel, final=final),
            grid=(tiles,),
            in_specs=in_specs,
            out_specs=pl.BlockSpec((tmc, d), tok),
            out_shape=jax.ShapeDtypeStruct((n, d), F32),
            input_output_aliases=aliases,
            compiler_params=_cparams("parallel"),
            name="combine_ple",
        )(*args)
    return out


def _row(v):
    return v.reshape(1, -1)


def _moe_and_ple(z, w_out, h, i, final, norm_ffn, w_router_group, b_router_group, w_router_expert,
                 b_router_expert, w_exp_gate, w_exp_up, w_exp_down, norm_ple, w_ple, w_ple_gate,
                 b_ple_gate, norm_final, p):
    n, d = h.shape
    pad = LANES - N_GROUPS - N_EXPERTS
    w_router = jnp.concatenate(
        [w_router_group[i], w_router_expert[i], jnp.zeros((d, pad), F32)], axis=1)
    b_router = jnp.concatenate(
        [b_router_group[i], b_router_expert[i], jnp.zeros((pad,), F32)]).reshape(1, LANES)
    h1, xn, meta, meta_t, cnt = _post_mix(z, w_out, h, _row(norm_ffn[i]), w_router, b_router)
    cnt_i32 = cnt[0, EXPERT_LANE0:EXPERT_LANE0 + N_EXPERTS].astype(I32)
    nb_max = (2 * n) // ROUTE_ROWS + N_EXPERTS
    dest, estart, ecount, nblk = _plan(cnt_i32, meta_t)
    xs = _sc_scatter_rows(xn, dest[0], dest[1], nb_max * ROUTE_ROWS)
    ys = _experts(estart, ecount, nblk, cnt_i32, xs, w_exp_gate, w_exp_up, w_exp_down, i)
    return _combine(dest, ys, h1, meta, p.reshape(p.shape[0], n, -1), i, _row(norm_ple[i]),
                    w_ple_gate[i].astype(BF16), _row(b_ple_gate[i]), w_ple[i].astype(BF16),
                    _row(norm_final), final)


def kernel(x, p, norm_mix, w_in_a, conv_a, w_out_a, w_in_b, conv_b, conv_bias_b, w_rgate_b, b_rgate_b, w_igate_b, b_igate_b, lam_b, w_out_b, norm_ffn, w_router_group, b_router_group, w_router_expert, b_router_expert, w_exp_gate, w_exp_up, w_exp_down, norm_ple, w_ple, w_ple_gate, b_ple_gate, norm_final):
    batch, seq, d = x.shape
    depth = p.shape[0]
    n = batch * seq
    h = x.reshape(n, d)
    for i in range(depth):
        j = i // 2
        if i % 2 == 0:
            bg, v = _mix_a_in(h, _row(norm_mix[i]), w_in_a[j].astype(BF16))
            z = _mix_a_conv(v, bg, conv_a[j], seq)
            w_out = w_out_a[j]
        else:
            tt = _tile(seq, 256)
            yb, u_tiles = _mix_b_in(h, _row(norm_mix[i]), w_in_b[j].astype(BF16), batch, seq, tt)
            z = _rglru(u_tiles, yb, conv_b[j], _row(conv_bias_b[j]),
                       (0.5 * w_rgate_b[j]).astype(BF16), b_rgate_b[j],
                       (0.5 * w_igate_b[j]).astype(BF16),
                       b_igate_b[j], lam_b[j], seq, tt)
            w_out = w_out_b[j]
        h = _moe_and_ple(z, w_out.astype(BF16), h, i, i == depth - 1, norm_ffn, w_router_group,
                         b_router_group, w_router_expert, b_router_expert, w_exp_gate, w_exp_up,
                         w_exp_down, norm_ple, w_ple, w_ple_gate, b_ple_gate, norm_final, p)
    return h.reshape(batch, seq, d)
```

```python
import functools

import jax
import jax.numpy as jnp
from jax import lax
from jax.experimental import pallas as pl
from jax.experimental.pallas import tpu as pltpu
from jax.experimental.pallas import tpu_sc as plsc

F32 = jnp.float32
BF16 = jnp.bfloat16
I32 = jnp.int32
U32 = jnp.uint32

EPS = 1e-6
N_GROUPS = 4
EXPERTS_PER_GROUP = 8
N_EXPERTS = N_GROUPS * EXPERTS_PER_GROUP
RG_C = 8.0

LANES = 128
SUBLANES = 8
BF16_ROWS = 16
EXPERT_LANE0 = N_GROUPS
ROUTE_ROWS = 256
IN_AHEAD = 4
IN_SLOTS = IN_AHEAD + 2
OUT_SLOTS = 4
SC_CORES = 2
SC_SUBCORES = 16
SC_CHUNK = 64
COMBINE_PARTS = 4
M_E0, M_E1, M_R0, M_R1, M_W0, M_W1 = 0, 1, 2, 3, 4, 5
VMEM_LIMIT = 48 * 1024 * 1024


def _cparams(*sem):
    return pltpu.CompilerParams(dimension_semantics=sem, vmem_limit_bytes=VMEM_LIMIT)


def _rms(x, g):
    return x * lax.rsqrt(jnp.mean(x * x, axis=-1, keepdims=True) + EPS) * g


def _dot(a, b):
    return jnp.dot(a, b, preferred_element_type=F32)


def _pack_bf16_pairs(x):
    k = x.shape[1] // 2
    hi = lax.bitcast_convert_type(x[:, :k].astype(BF16).astype(F32), U32)
    lo = lax.bitcast_convert_type(x[:, k:].astype(BF16).astype(F32), U32)
    return hi | (lo >> 16)


def _unpack_bf16_pairs(p):
    hi = lax.bitcast_convert_type(p & jnp.uint32(0xFFFF0000), F32)
    lo = lax.bitcast_convert_type(p << 16, F32)
    return hi, lo


def _tile(n, want):
    t = min(n, want)
    assert n % t == 0, (n, want)
    return t


def _mix_a_in_kernel(h_ref, g_ref, w_ref, bg_ref, v_ref):
    d = h_ref.shape[1]
    xn = _rms(h_ref[...], g_ref[...]).astype(BF16)
    bg = _dot(xn, w_ref[:, 0:d])
    cg = _dot(xn, w_ref[:, d:2 * d])
    hh = _dot(xn, w_ref[:, 2 * d:3 * d])
    bg_ref[...] = bg.astype(BF16)
    v_ref[...] = (cg * hh).astype(BF16)


def _mix_a_in(h, g, w_in):
    n, d = h.shape
    tm = _tile(n, 512)
    return pl.pallas_call(
        _mix_a_in_kernel,
        grid=(n // tm,),
        in_specs=[pl.BlockSpec((tm, d), lambda i: (i, 0)),
                  pl.BlockSpec((1, d), lambda i: (0, 0)),
                  pl.BlockSpec((d, 3 * d), lambda i: (0, 0))],
        out_specs=[pl.BlockSpec((tm, d), lambda i: (i, 0)),
                   pl.BlockSpec((tm, d), lambda i: (i, 0))],
        out_shape=[jax.ShapeDtypeStruct((n, d), BF16), jax.ShapeDtypeStruct((n, d), BF16)],
        compiler_params=_cparams("parallel"),
        name="mix_a_in",
    )(h, g, w_in)


def _mix_a_conv_kernel(v_ref, vp_ref, vn_ref, bg_ref, cw_ref, z_ref, *, tiles_per_seq):
    i = pl.program_id(0)
    tm = v_ref.shape[0]
    v = v_ref[...].astype(F32)
    at_start = (i % tiles_per_seq) == 0
    at_end = (i % tiles_per_seq) == tiles_per_seq - 1
    prev_row = vp_ref[...].astype(F32)[BF16_ROWS - 1:BF16_ROWS, :]
    next_row = vn_ref[...].astype(F32)[0:1, :]
    prev_row = jnp.where(at_start, 0.0, prev_row)
    next_row = jnp.where(at_end, 0.0, next_row)
    row = lax.broadcasted_iota(I32, (tm, 1), 0)
    v_dn = jnp.where(row == 0, prev_row, pltpu.roll(v, 1, 0))
    v_up = jnp.where(row == tm - 1, next_row, pltpu.roll(v, tm - 1, 0))
    cw = cw_ref[...]
    u = cw[0:1, :] * v_dn + cw[1:2, :] * v + cw[2:3, :] * v_up
    z_ref[...] = (bg_ref[...].astype(F32) * u).astype(BF16)


def _mix_a_conv(v, bg, conv_w, seq):
    n, d = v.shape
    tm = _tile(seq, 512)
    hb = tm // BF16_ROWS
    nhalo = n // BF16_ROWS
    return pl.pallas_call(
        functools.partial(_mix_a_conv_kernel, tiles_per_seq=seq // tm),
        grid=(n // tm,),
        in_specs=[pl.BlockSpec((tm, d), lambda i: (i, 0)),
                  pl.BlockSpec((BF16_ROWS, d), lambda i: (jnp.maximum(i * hb - 1, 0), 0)),
                  pl.BlockSpec((BF16_ROWS, d), lambda i: (jnp.minimum((i + 1) * hb, nhalo - 1), 0)),
                  pl.BlockSpec((tm, d), lambda i: (i, 0)),
                  pl.BlockSpec(conv_w.shape, lambda i: (0, 0))],
        out_specs=pl.BlockSpec((tm, d), lambda i: (i, 0)),
        out_shape=jax.ShapeDtypeStruct((n, d), BF16),
        compiler_params=_cparams("parallel"),
        name="mix_a_conv",
    )(v, v, v, bg, conv_w)


def _mix_b_in_kernel(h_ref, hp_ref, hn_ref, g_ref, w_ref, yb_ref, u_ref, *, steps_per_seq, tt):
    i = pl.program_id(0)
    r = yb_ref.shape[1]
    tiles = h_ref.shape[0] // tt
    g = g_ref[...]
    xn = _rms(h_ref[...], g).astype(BF16)
    yb_ref[...] = jax.nn.gelu(_dot(xn, w_ref[:, 0:r])).astype(BF16)
    u = _dot(xn, w_ref[:, r:2 * r])
    xh = _rms(jnp.concatenate([hp_ref[...], hn_ref[...]], axis=0), g).astype(BF16)
    uh = _dot(xh, w_ref[:, r:2 * r])
    at_start = (i % steps_per_seq) == 0
    at_end = (i % steps_per_seq) == steps_per_seq - 1
    prev2 = jnp.where(at_start, 0.0, uh[SUBLANES - 2:SUBLANES, :])
    next1 = jnp.where(at_end, 0.0, uh[SUBLANES:SUBLANES + 1, :])
    row = lax.broadcasted_iota(I32, (SUBLANES, 1), 0)
    for k in range(tiles):
        p2 = prev2 if k == 0 else u[k * tt - 2:k * tt]
        n1 = next1 if k == tiles - 1 else u[(k + 1) * tt:(k + 1) * tt + 1]
        pad = jnp.where(row == 0, p2[0:1], jnp.where(row == 1, p2[1:2], jnp.where(row == 2, n1, 0.0)))
        for s in range(r // LANES):
            lanes = slice(s * LANES, (s + 1) * LANES)
            u_ref[k, s, 0:tt, :] = u[k * tt:(k + 1) * tt, lanes]
            u_ref[k, s, tt:tt + SUBLANES, :] = pad[:, lanes]


def _mix_b_in(h, g, w_in, batch, seq, tt):
    n, d = h.shape
    r = w_in.shape[1] // 2
    tm = _tile(seq, 512)
    steps_per_seq = seq // tm
    tiles = tm // tt
    hb = tm // SUBLANES
    nhalo = n // SUBLANES
    return pl.pallas_call(
        functools.partial(_mix_b_in_kernel, steps_per_seq=steps_per_seq, tt=tt),
        grid=(n // tm,),
        in_specs=[pl.BlockSpec((tm, d), lambda i: (i, 0)),
                  pl.BlockSpec((SUBLANES, d), lambda i: (jnp.maximum(i * hb - 1, 0), 0)),
                  pl.BlockSpec((SUBLANES, d), lambda i: (jnp.minimum((i + 1) * hb, nhalo - 1), 0)),
                  pl.BlockSpec((1, d), lambda i: (0, 0)),
                  pl.BlockSpec((d, 2 * r), lambda i: (0, 0))],
        out_specs=[pl.BlockSpec((tm, r), lambda i: (i, 0)),
                   pl.BlockSpec((tiles, r // LANES, None, tt + SUBLANES, LANES),
                                lambda i: (i % steps_per_seq, 0, i // steps_per_seq, 0, 0))],
        out_shape=[jax.ShapeDtypeStruct((n, r), BF16),
                   jax.ShapeDtypeStruct((seq // tt, r // LANES, batch, tt + SUBLANES, LANES), F32)],
        compiler_params=_cparams("parallel"),
        name="mix_b_in",
    )(h, h, h, g, w_in)


def _rglru_gates(u, wr_ref, br_ref, wi_ref, bi_ref, lam_ref, a_s, g_s):
    ub = u.astype(BF16)
    t_r = jnp.tanh(_dot(ub, wr_ref[...]) + 0.5 * br_ref[...])
    t_i = jnp.tanh(_dot(ub, wi_ref[...]) + 0.5 * bi_ref[...])
    c = (-0.25 * RG_C) * jax.nn.softplus(-lam_ref[...])
    th = jnp.tanh(c + c * t_r)
    q = 1.0 / (1.0 - th)
    a_s[...] = (1.0 + th) * q
    nth = -th
    root = jnp.where(nth > 0.0, nth * lax.rsqrt(nth), 0.0)
    g_s[...] = (q * root) * (u + u * t_i)


def _rglru_fwd_kernel(u_ref, cw_ref, cb_ref, wr_ref, br_ref, wi_ref, bi_ref, lam_ref,
                      hf_ref, uc_ref, x_s, a_s, g_s, carry_s, *, tt):
    nb = SUBLANES
    ttp = tt + SUBLANES
    rows = tt * nb
    slabs = u_ref.shape[0]

    @pl.when(pl.program_id(1) == 0)
    def _():
        carry_s[...] = jnp.zeros_like(carry_s)

    def gather_t(tl, dst):
        for s in range(slabs):
            x_s[pl.ds(dst, nb), s * LANES:(s + 1) * LANES] = u_ref[s, pl.ds(tl, nb, stride=ttp), :]

    def load_t(tl, c):
        gather_t(tl, pl.multiple_of((tl + 2) * nb, nb))
        return c

    lax.fori_loop(0, tt, load_t, 0, unroll=8)
    gather_t(tt, 0)
    gather_t(tt + 1, nb)
    gather_t(tt + 2, (tt + 2) * nb)

    cw = cw_ref[...]
    u = (cw[0:1, :] * x_s[0:rows, :] + cw[1:2, :] * x_s[nb:nb + rows, :]
         + cw[2:3, :] * x_s[2 * nb:2 * nb + rows, :] + cw[3:4, :] * x_s[3 * nb:3 * nb + rows, :]
         + cb_ref[...])
    uc_ref[...] = u
    _rglru_gates(u, wr_ref, br_ref, wi_ref, bi_ref, lam_ref, a_s, g_s)

    def step(tl, h):
        base = pl.multiple_of(tl * nb, nb)
        h = a_s[pl.ds(base, nb), :] * h + g_s[pl.ds(base, nb), :]
        hf_ref[pl.ds(base, nb), :] = h
        return h

    carry_s[...] = lax.fori_loop(0, tt, step, carry_s[...], unroll=8)


def _rglru_rev_kernel(uc_ref, wr_ref, br_ref, wi_ref, bi_ref, lam_ref, hf_ref, yb_ref,
                      z_ref, a_s, g_s, carry_s, nat_s, *, tt):
    nb = SUBLANES
    ttp = tt + SUBLANES
    slabs = nat_s.shape[0]

    @pl.when(pl.program_id(1) == 0)
    def _():
        carry_s[...] = jnp.zeros_like(carry_s)

    _rglru_gates(uc_ref[...], wr_ref, br_ref, wi_ref, bi_ref, lam_ref, a_s, g_s)

    def step(k, h):
        tl = tt - 1 - k
        base = pl.multiple_of(tl * nb, nb)
        h = a_s[pl.ds(base, nb), :] * h + g_s[pl.ds(base, nb), :]
        hs = h + hf_ref[pl.ds(base, nb), :]
        for s in range(slabs):
            nat_s[s, pl.ds(tl, nb, stride=ttp), :] = hs[:, s * LANES:(s + 1) * LANES]
        return h

    carry_s[...] = lax.fori_loop(0, tt, step, carry_s[...], unroll=8)
    for b in range(nb):
        for s in range(slabs):
            lanes = slice(s * LANES, (s + 1) * LANES)
            hb = nat_s[s, b * ttp:b * ttp + tt, :]
            z_ref[b, :, lanes] = (yb_ref[b, :, lanes].astype(F32) * hb).astype(BF16)


def _rglru(u_tiles, yb, conv_w, conv_b, w_r, b_r, w_i, b_i, lam, seq, tt):
    nt, nslab, batch, ttp, _ = u_tiles.shape
    r = nslab * LANES
    assert batch == SUBLANES, "time-major tile layout puts the batch on the 8 sublanes"
    nh, bk = w_r.shape[1], w_r.shape[2]
    slabs = bk // LANES
    kw = conv_w.shape[0]
    rows = tt * batch
    gates = [w_r, b_r.reshape(2, 1, r), w_i, b_i.reshape(2, 1, r), lam.reshape(2, 1, r)]

    def gate_specs(d):
        vec = pl.BlockSpec((None, 1, bk), lambda h, jj: (d, 0, h))
        mat = pl.BlockSpec((None, None, bk, bk), lambda h, jj: (d, h, 0, 0))
        return [mat, vec, mat, vec, vec]

    tm_spec = pl.BlockSpec((rows, bk), lambda h, jj: (jj, h))
    tm_rev = pl.BlockSpec((rows, bk), lambda h, jj: (nt - 1 - jj, h))
    tm_shape = jax.ShapeDtypeStruct((seq * batch, r), F32)
    state = [pltpu.VMEM((rows, bk), F32), pltpu.VMEM((rows, bk), F32), pltpu.VMEM((batch, bk), F32)]

    hf, uc = pl.pallas_call(
        functools.partial(_rglru_fwd_kernel, tt=tt),
        grid=(nh, nt),
        in_specs=[pl.BlockSpec((None, slabs, batch * ttp, LANES), lambda h, jj: (jj, h, 0, 0)),
                  pl.BlockSpec((kw, bk), lambda h, jj: (0, h)),
                  pl.BlockSpec((1, bk), lambda h, jj: (0, h))] + gate_specs(0),
        out_specs=[tm_spec, tm_spec],
        out_shape=[tm_shape, tm_shape],
        scratch_shapes=[pltpu.VMEM(((tt + 3) * batch, bk), F32)] + state,
        compiler_params=_cparams("parallel", "arbitrary"),
        name="rglru_fwd",
    )(u_tiles.reshape(nt, nslab, batch * ttp, LANES), conv_w, conv_b, *gates)

    nat_spec = pl.BlockSpec((batch, tt, bk), lambda h, jj: (0, nt - 1 - jj, h))
    z = pl.pallas_call(
        functools.partial(_rglru_rev_kernel, tt=tt),
        grid=(nh, nt),
        in_specs=[tm_rev] + gate_specs(1) + [tm_rev, nat_spec],
        out_specs=nat_spec,
        out_shape=jax.ShapeDtypeStruct((batch, seq, r), BF16),
        scratch_shapes=state + [pltpu.VMEM((slabs, batch * ttp, LANES), F32)],
        compiler_params=_cparams("parallel", "arbitrary"),
        name="rglru_rev",
    )(uc, *gates, hf, yb.reshape(batch, seq, r))
    return z.reshape(-1, r)


def _route(logits):
    lane = lax.broadcasted_iota(I32, logits.shape, 1).astype(F32)
    neg = jnp.float32(-jnp.inf)
    nolane = jnp.float32(LANES)
    gmask = lane < N_GROUPS
    gmax = jnp.max(jnp.where(gmask, logits, neg), axis=1, keepdims=True)
    eg = jnp.where(gmask, jnp.exp(logits - gmax), 0.0)
    gsum = jnp.sum(eg, axis=1, keepdims=True)
    pg = eg / gsum
    pg_top = 1.0 / gsum
    g_top = jnp.min(jnp.where(gmask & (pg == pg_top), lane, nolane), axis=1, keepdims=True)
    lo = EXPERT_LANE0 + g_top * EXPERTS_PER_GROUP
    emask = (lane >= lo) & (lane < lo + EXPERTS_PER_GROUP)
    emax = jnp.max(jnp.where(emask, logits, neg), axis=1, keepdims=True)
    ee = jnp.where(emask, jnp.exp(logits - emax), 0.0)
    esum = jnp.sum(ee, axis=1, keepdims=True)
    pe = jnp.where(emask, ee / esum, -1.0)
    p1 = 1.0 / esum
    i1 = jnp.min(jnp.where(pe == p1, lane, nolane), axis=1, keepdims=True)
    pe2 = jnp.where(lane == i1, -1.0, pe)
    p2 = jnp.max(pe2, axis=1, keepdims=True)
    i2 = jnp.min(jnp.where(pe2 == p2, lane, nolane), axis=1, keepdims=True)
    psum = p1 + p2
    return i1, i2, pg_top * (p1 / psum), pg_top * (p2 / psum)


def _post_mix_kernel(z_ref, wo_ref, h_ref, g_ref, wr_ref, br_ref,
                     h1_ref, xn_ref, meta_ref, meta_t_ref, cnt_ref, carry_ref, logits_s):
    i = pl.program_id(0)
    tm = z_ref.shape[0]

    @pl.when(i == 0)
    def _():
        logits_s[...] = jnp.zeros_like(logits_s)

    @pl.when(i <= 1)
    def _():
        carry_ref[...] = jnp.zeros_like(carry_ref)

    logits = logits_s[...]

    h1 = h_ref[...] + _dot(z_ref[...], wo_ref[...])
    h1_ref[...] = h1
    xn = _rms(h1, g_ref[...])
    xn_ref[...] = _pack_bf16_pairs(xn)
    logits_s[...] = _dot(xn.astype(BF16), wr_ref[...].astype(BF16)) + br_ref[...]

    i1, i2, w0, w1 = _route(logits)

    lane = lax.broadcasted_iota(I32, (tm, LANES), 1).astype(F32)
    oh0 = lane == i1
    oh1 = lane == i2
    both = (oh0 | oh1).astype(BF16)
    ri = lax.broadcasted_iota(I32, (tm, tm), 0)
    ci = lax.broadcasted_iota(I32, (tm, tm), 1)
    before = (ri > ci).astype(BF16)
    cnt_before = _dot(before, both) + carry_ref[...]
    rank0 = jnp.sum(jnp.where(oh0, cnt_before, 0.0), axis=1, keepdims=True)
    rank1 = jnp.sum(jnp.where(oh1, cnt_before, 0.0), axis=1, keepdims=True)
    carry = carry_ref[...] + jnp.sum(both.astype(F32), axis=0, keepdims=True)
    carry_ref[...] = carry
    cnt_ref[...] = carry

    e0 = i1 - EXPERT_LANE0
    e1 = i2 - EXPERT_LANE0
    meta = jnp.zeros((tm, LANES), F32)
    for ln, val in ((M_E0, e0), (M_E1, e1), (M_R0, rank0), (M_R1, rank1), (M_W0, w0), (M_W1, w1)):
        meta = jnp.where(lane == ln, val, meta)
    meta_ref[...] = meta
    meta_t_ref[...] = meta.T[0:SUBLANES, :]


def _post_mix(z, w_out, h, g, w_router, b_router):
    n, d = h.shape
    k = z.shape[1]
    tm = _tile(n, 512)
    nt = n // tm

    def proj(i):
        return (jnp.minimum(i, nt - 1), 0)

    def routed(i):
        return (jnp.maximum(i - 1, 0), 0)

    return pl.pallas_call(
        _post_mix_kernel,
        grid=(nt + 1,),
        in_specs=[pl.BlockSpec((tm, k), proj),
                  pl.BlockSpec((k, d), lambda i: (0, 0)),
                  pl.BlockSpec((tm, d), proj),
                  pl.BlockSpec((1, d), lambda i: (0, 0)),
                  pl.BlockSpec((d, LANES), lambda i: (0, 0)),
                  pl.BlockSpec((1, LANES), lambda i: (0, 0))],
        out_specs=[pl.BlockSpec((tm, d), proj),
                   pl.BlockSpec((tm, d // 2), proj),
                   pl.BlockSpec((tm, LANES), routed),
                   pl.BlockSpec((SUBLANES, tm), lambda i: (0, jnp.maximum(i - 1, 0))),
                   pl.BlockSpec((1, LANES), lambda i: (0, 0))],
        out_shape=[jax.ShapeDtypeStruct((n, d), F32), jax.ShapeDtypeStruct((n, d // 2), U32),
                   jax.ShapeDtypeStruct((n, LANES), F32), jax.ShapeDtypeStruct((SUBLANES, n), F32),
                   jax.ShapeDtypeStruct((1, LANES), F32)],
        scratch_shapes=[pltpu.VMEM((1, LANES), F32), pltpu.VMEM((tm, LANES), F32)],
        compiler_params=_cparams("arbitrary"),
        name="post_mix_router",
    )(z, w_out, h, g, w_router, b_router)


def _plan_kernel(cnt_ref, meta_t_ref, dest_ref, estart_ref, ecount_ref, nblk_ref, pstart_ref):
    def per_expert(e, acc):
        nb_e = (cnt_ref[e] + (ROUTE_ROWS - 1)) // ROUTE_ROWS
        pstart_ref[e] = acc * ROUTE_ROWS
        estart_ref[e] = acc
        ecount_ref[e] = nb_e
        return acc + nb_e

    nblk_ref[0] = lax.fori_loop(0, N_EXPERTS, per_expert, 0)

    e0 = meta_t_ref[M_E0:M_E0 + 1, :]
    e1 = meta_t_ref[M_E1:M_E1 + 1, :]
    d0 = meta_t_ref[M_R0:M_R0 + 1, :]
    d1 = meta_t_ref[M_R1:M_R1 + 1, :]
    for e in range(N_EXPERTS):
        ps = pstart_ref[e].astype(F32)
        d0 = d0 + jnp.where(e0 == e, ps, 0.0)
        d1 = d1 + jnp.where(e1 == e, ps, 0.0)
    dest_ref[...] = jnp.zeros_like(dest_ref)
    dest_ref[0:1, :] = d0.astype(I32)
    dest_ref[1:2, :] = d1.astype(I32)


def _plan(cnt_i32, meta_t):
    n = meta_t.shape[1]
    smem = pl.BlockSpec(memory_space=pltpu.SMEM)
    return pl.pallas_call(
        _plan_kernel,
        in_specs=[smem, pl.BlockSpec((SUBLANES, n), lambda: (0, 0))],
        out_specs=[pl.BlockSpec((SUBLANES, n), lambda: (0, 0)), smem, smem, smem],
        out_shape=[jax.ShapeDtypeStruct((SUBLANES, n), I32),
                   jax.ShapeDtypeStruct((N_EXPERTS,), I32),
                   jax.ShapeDtypeStruct((N_EXPERTS,), I32),
                   jax.ShapeDtypeStruct((1,), I32)],
        scratch_shapes=[pltpu.SMEM((N_EXPERTS,), I32)],
        name="dispatch_plan",
    )(cnt_i32, meta_t)


def _sc_mesh():
    return plsc.VectorSubcoreMesh(core_axis_name="c", subcore_axis_name="s",
                                  num_cores=SC_CORES, num_subcores=SC_SUBCORES)


def _sc_scatter_rows(x, idx0, idx1, p_rows):
    m, k = x.shape
    per_w = m // (SC_CORES * SC_SUBCORES)
    assert per_w % SC_CHUNK == 0 and per_w * SC_CORES * SC_SUBCORES == m

    nchunk = per_w // SC_CHUNK
    buf = [pltpu.VMEM((SC_CHUNK,), I32), pltpu.VMEM((SC_CHUNK,), I32),
           pltpu.VMEM((SC_CHUNK, k), x.dtype),
           pltpu.SemaphoreType.DMA, pltpu.SemaphoreType.DMA, pltpu.SemaphoreType.DMA]

    @functools.partial(
        pl.kernel, mesh=_sc_mesh(), out_type=jax.ShapeDtypeStruct((p_rows, k), x.dtype),
        scratch_types=buf + buf, name="sc_scatter_rows")
    def scatter(x_hbm, i0_hbm, i1_hbm, out_hbm, *scratch):
        base = (lax.axis_index("s") * SC_CORES + lax.axis_index("c")) * per_w
        slots = (scratch[:6], scratch[6:])

        def loads(c):
            i0_v, i1_v, rows_v, lsem, _, _ = slots[c % 2]
            off = pl.multiple_of(base + c * SC_CHUNK, SC_CHUNK)
            return [pltpu.make_async_copy(i0_hbm.at[pl.ds(off, SC_CHUNK)], i0_v, lsem),
                    pltpu.make_async_copy(i1_hbm.at[pl.ds(off, SC_CHUNK)], i1_v, lsem),
                    pltpu.make_async_copy(x_hbm.at[pl.ds(off, SC_CHUNK)], rows_v, lsem)]

        def puts(c):
            i0_v, i1_v, rows_v, _, psem0, psem1 = slots[c % 2]
            return [pltpu.make_async_copy(rows_v, out_hbm.at[i0_v], psem0),
                    pltpu.make_async_copy(rows_v, out_hbm.at[i1_v], psem1)]

        for d in loads(0):
            d.start()
        for c in range(nchunk):
            for d in loads(c):
                d.wait()
            for d in puts(c):
                d.start()
            if c >= 1:
                for d in puts(c - 1):
                    d.wait()
            if c + 1 < nchunk:
                for d in loads(c + 1):
                    d.start()
        for d in puts(nchunk - 1):
            d.wait()

    return scatter(x, idx0, idx1)


def _expert_kernel(es_ref, ec_ref, nblk_ref, cnt_ref, xs_ref, wg_ref, wu_ref, wd_ref, ys_ref,
                   x_s, y_s, wg_s, wu_s, wd_s, sem):
    e = pl.program_id(0)
    nb = ec_ref[e]
    b0 = es_ref[e]
    k = x_s.shape[2]
    nb_max = ys_ref.shape[0] // ROUTE_ROWS

    def rows(b):
        return pl.ds(pl.multiple_of(b * ROUTE_ROWS, ROUTE_ROWS), ROUTE_ROWS)

    nblk = nblk_ref[0]

    def copy_in(b):
        slot = lax.rem(b, IN_SLOTS)
        return pltpu.make_async_copy(xs_ref.at[rows(b)], x_s.at[slot], sem.at[0, slot])

    def copy_out(b):
        slot = lax.rem(b, OUT_SLOTS)
        return pltpu.make_async_copy(y_s.at[slot], ys_ref.at[rows(b)], sem.at[1, slot])

    @pl.when(e == 0)
    def _():
        for b in range(IN_AHEAD):
            @pl.when(b < nblk)
            def _(b=b):
                copy_in(b).start()

    @pl.when(nb > 0)
    def _():
        wg_s[...] = wg_ref[...].astype(BF16)
        wu_s[...] = wu_ref[...].astype(BF16)
        wd_s[...] = wd_ref[...].astype(BF16)

        def mlp(b, j, width):
            row = lax.broadcasted_iota(I32, (ROUTE_ROWS, 1), 0)
            xp = jnp.concatenate(
                [jnp.where(row < cnt_ref[e] - (j + i) * ROUTE_ROWS,
                           x_s[lax.rem(b + i, IN_SLOTS)], jnp.uint32(0)) for i in range(width)], axis=0)
            x_hi, x_lo = _unpack_bf16_pairs(xp)
            x_hi = x_hi.astype(BF16)
            x_lo = x_lo.astype(BF16)
            g = _dot(x_hi, wg_s[0:k, :]) + _dot(x_lo, wg_s[k:2 * k, :])
            u = _dot(x_hi, wu_s[0:k, :]) + _dot(x_lo, wu_s[k:2 * k, :])
            hb = (jax.nn.silu(g) * u).astype(BF16)
            y = _pack_bf16_pairs(_dot(hb, wd_s[...]))
            return [y[i * ROUTE_ROWS:(i + 1) * ROUTE_ROWS] for i in range(width)]

        def unit(j, width):
            b = b0 + j
            for i in range(width):
                copy_in(b + i).wait()
            for i in range(width):
                @pl.when(b + i + IN_AHEAD < nblk)
                def _(i=i):
                    copy_in(b + i + IN_AHEAD).start()
            ys = mlp(b, j, width)
            for i in range(width):
                @pl.when(b + i >= OUT_SLOTS)
                def _(i=i):
                    copy_out(b + i - OUT_SLOTS).wait()
            for i in range(width):
                y_s[lax.rem(b + i, OUT_SLOTS)] = ys[i]
                copy_out(b + i).start()

        def pair(jj, c):
            unit(2 * jj, 2)
            return c

        lax.fori_loop(0, nb // 2, pair, 0)

        @pl.when(nb % 2 == 1)
        def _():
            unit(nb - 1, 1)

    @pl.when(e == pl.num_programs(0) - 1)
    def _():
        for back in range(OUT_SLOTS, 0, -1):
            @pl.when(nblk >= back)
            def _(back=back):
                copy_out(nblk - back).wait()

    @pl.when(e == pl.num_programs(0) - 1)
    def _():
        y_s[0] = jnp.zeros_like(y_s[0])

        def zero(b):
            return pltpu.make_async_copy(y_s.at[0], ys_ref.at[rows(b)], sem.at[1, 0])

        def start_zero(b, c):
            zero(b).start()
            return c

        def wait_zero(b, c):
            zero(b).wait()
            return c

        lax.fori_loop(nblk_ref[0], nb_max, start_zero, 0)
        lax.fori_loop(nblk_ref[0], nb_max, wait_zero, 0)


def _experts(estart, ecount, nblk, cnt, xs, w_gate, w_up, w_down, layer):
    p, k = xs.shape
    d, f = w_gate.shape[2], w_gate.shape[3]

    def w_map(e, es, ec, nb, cn):
        return (layer, e, 0, 0)

    grid_spec = pltpu.PrefetchScalarGridSpec(
        num_scalar_prefetch=4,
        grid=(N_EXPERTS,),
        in_specs=[pl.BlockSpec(memory_space=pl.ANY),
                  pl.BlockSpec((None, None, d, f), w_map),
                  pl.BlockSpec((None, None, d, f), w_map),
                  pl.BlockSpec((None, None, f, d), w_map)],
        out_specs=pl.BlockSpec(memory_space=pl.ANY),
        scratch_shapes=[pltpu.VMEM((IN_SLOTS, ROUTE_ROWS, k), U32),
                        pltpu.VMEM((OUT_SLOTS, ROUTE_ROWS, k), U32),
                        pltpu.VMEM((d, f), BF16), pltpu.VMEM((d, f), BF16), pltpu.VMEM((f, d), BF16),
                        pltpu.SemaphoreType.DMA((2, max(IN_SLOTS, OUT_SLOTS)))],
    )
    return pl.pallas_call(
        _expert_kernel,
        grid_spec=grid_spec,
        out_shape=jax.ShapeDtypeStruct((p, k), U32),
        compiler_params=_cparams("arbitrary"),
        name="expert_mlp",
    )(estart, ecount, nblk, cnt, xs, w_gate, w_up, w_down)


def _sc_gather_pair(table, idx0, idx1, start, count):
    k = table.shape[1]
    per_w = count // (SC_CORES * SC_SUBCORES)
    assert per_w % SC_CHUNK == 0 and per_w * SC_CORES * SC_SUBCORES == count
    assert start % SC_CHUNK == 0
    rows = jax.ShapeDtypeStruct((count, k), table.dtype)

    @functools.partial(
        pl.kernel, mesh=_sc_mesh(), out_type=(rows, rows),
        scratch_types=[pltpu.VMEM((SC_CHUNK,), I32), pltpu.VMEM((SC_CHUNK,), I32),
                       pltpu.VMEM((SC_CHUNK, k), table.dtype), pltpu.VMEM((SC_CHUNK, k), table.dtype),
                       pltpu.SemaphoreType.DMA, pltpu.SemaphoreType.DMA,
                       pltpu.SemaphoreType.DMA, pltpu.SemaphoreType.DMA],
        name="sc_gather_pair")
    def gather(table_hbm, i0_hbm, i1_hbm, o0_hbm, o1_hbm, i0_v, i1_v, r0_v, r1_v,
               sem0, sem1, sem2, sem3):
        base = (lax.axis_index("s") * SC_CORES + lax.axis_index("c")) * per_w

        @pl.loop(0, per_w // SC_CHUNK)
        def _(c):
            dst = pl.multiple_of(base + c * SC_CHUNK, SC_CHUNK)
            src = pl.multiple_of(start + dst, SC_CHUNK)
            pltpu.sync_copy(i0_hbm.at[pl.ds(src, SC_CHUNK)], i0_v)
            pltpu.sync_copy(i1_hbm.at[pl.ds(src, SC_CHUNK)], i1_v)
            get0 = pltpu.async_copy(table_hbm.at[i0_v], r0_v, sem0)
            get1 = pltpu.async_copy(table_hbm.at[i1_v], r1_v, sem1)
            get0.wait()
            put0 = pltpu.async_copy(r0_v, o0_hbm.at[pl.ds(dst, SC_CHUNK)], sem2)
            get1.wait()
            put1 = pltpu.async_copy(r1_v, o1_hbm.at[pl.ds(dst, SC_CHUNK)], sem3)
            put0.wait()
            put1.wait()

    return gather(table, idx0, idx1)


def _combine_kernel(*refs, final):
    (ya_ref, yb_ref, h1_ref, meta_ref, p_ref, gp_ref, wpg_ref, bpg_ref, wp_ref,
     gf_ref) = refs[:10]
    out_ref = refs[-1]
    meta = meta_ref[...]
    w0 = meta[:, M_W0:M_W0 + 1]
    w1 = meta[:, M_W1:M_W1 + 1]
    ya_hi, ya_lo = _unpack_bf16_pairs(ya_ref[...])
    yb_hi, yb_lo = _unpack_bf16_pairs(yb_ref[...])
    y = jnp.concatenate([ya_hi * w0 + yb_hi * w1, ya_lo * w0 + yb_lo * w1], axis=1)
    h2 = h1_ref[...] + y
    xn = _rms(h2, gp_ref[...]).astype(BF16)
    gate = jax.nn.sigmoid(_dot(xn, wpg_ref[...]) + bpg_ref[...])
    pp = _dot(p_ref[...].astype(BF16), wp_ref[...])
    h3 = h2 + gate * pp
    if final:
        h3 = _rms(h3, gf_ref[...])
    out_ref[...] = h3


def _combine(dest, ys, h1, meta, p, layer, g_ple, w_ple_gate, b_ple_gate, w_ple, g_final, final):
    n, d = h1.shape
    pd = p.shape[2]
    k = ys.shape[1]
    n_part = n // COMBINE_PARTS
    tmc = _tile(n_part, 512)
    tiles = n_part // tmc
    vec = pl.BlockSpec((1, d), lambda i: (0, 0))
    out = None
    for q in range(COMBINE_PARTS):
        ya, yb = _sc_gather_pair(ys, dest[0], dest[1], q * n_part, n_part)

        def tok(i, q=q):
            return (q * tiles + i, 0)

        in_specs = [pl.BlockSpec((tmc, k), lambda i: (i, 0)),
                    pl.BlockSpec((tmc, k), lambda i: (i, 0)),
                    pl.BlockSpec((tmc, d), tok),
                    pl.BlockSpec((tmc, LANES), tok),
                    pl.BlockSpec((None, tmc, pd), lambda i, q=q: (layer, q * tiles + i, 0)),
                    vec,
                    pl.BlockSpec((d, d), lambda i: (0, 0)),
                    vec,
                    pl.BlockSpec((pd, d), lambda i: (0, 0)),
                    vec]
        args = [ya, yb, h1, meta, p, g_ple, w_ple_gate, b_ple_gate, w_ple, g_final]
        aliases = {}
        if out is not None:
            in_specs.append(pl.BlockSpec(memory_space=pl.ANY))
            args.append(out)
            aliases = {len(args) - 1: 0}
        out = pl.pallas_call(
            functools.partial(_combine_kernel, final=final),
            grid=(tiles,),
            in_specs=in_specs,
            out_specs=pl.BlockSpec((tmc, d), tok),
            out_shape=jax.ShapeDtypeStruct((n, d), F32),
            input_output_aliases=aliases,
            compiler_params=_cparams("parallel"),
            name="combine_ple",
        )(*args)
    return out


def _row(v):
    return v.reshape(1, -1)


def _moe_and_ple(z, w_out, h, i, final, norm_ffn, w_router_group, b_router_group, w_router_expert,
                 b_router_expert, w_exp_gate, w_exp_up, w_exp_down, norm_ple, w_ple, w_ple_gate,
                 b_ple_gate, norm_final, p):
    n, d = h.shape
    pad = LANES - N_GROUPS - N_EXPERTS
    w_router = jnp.concatenate(
        [w_router_group[i], w_router_expert[i], jnp.zeros((d, pad), F32)], axis=1)
    b_router = jnp.concatenate(
        [b_router_group[i], b_router_expert[i], jnp.zeros((pad,), F32)]).reshape(1, LANES)
    h1, xn, meta, meta_t, cnt = _post_mix(z, w_out, h, _row(norm_ffn[i]), w_router, b_router)
    cnt_i32 = cnt[0, EXPERT_LANE0:EXPERT_LANE0 + N_EXPERTS].astype(I32)
    nb_max = (2 * n) // ROUTE_ROWS + N_EXPERTS
    dest, estart, ecount, nblk = _plan(cnt_i32, meta_t)
    xs = _sc_scatter_rows(xn, dest[0], dest[1], nb_max * ROUTE_ROWS)
    ys = _experts(estart, ecount, nblk, cnt_i32, xs, w_exp_gate, w_exp_up, w_exp_down, i)
    return _combine(dest, ys, h1, meta, p.reshape(p.shape[0], n, -1), i, _row(norm_ple[i]),
                    w_ple_gate[i].astype(BF16), _row(b_ple_gate[i]), w_ple[i].astype(BF16),
                    _row(norm_final), final)


def kernel(x, p, norm_mix, w_in_a, conv_a, w_out_a, w_in_b, conv_b, conv_bias_b, w_rgate_b, b_rgate_b, w_igate_b, b_igate_b, lam_b, w_out_b, norm_ffn, w_router_group, b_router_group, w_router_expert, b_router_expert, w_exp_gate, w_exp_up, w_exp_down, norm_ple, w_ple, w_ple_gate, b_ple_gate, norm_final):
    batch, seq, d = x.shape
    depth = p.shape[0]
    n = batch * seq
    h = x.reshape(n, d)
    for i in range(depth):
        j = i // 2
        if i % 2 == 0:
            bg, v = _mix_a_in(h, _row(norm_mix[i]), w_in_a[j].astype(BF16))
            z = _mix_a_conv(v, bg, conv_a[j], seq)
            w_out = w_out_a[j]
        else:
            tt = _tile(seq, 256)
            yb, u_tiles = _mix_b_in(h, _row(norm_mix[i]), w_in_b[j].astype(BF16), batch, seq, tt)
            z = _rglru(u_tiles, yb, conv_b[j], _row(conv_bias_b[j]),
                       (0.5 * w_rgate_b[j]).astype(BF16), b_rgate_b[j],
                       (0.5 * w_igate_b[j]).astype(BF16),
                       b_igate_b[j], lam_b[j], seq, tt)
            w_out = w_out_b[j]
        h = _moe_and_ple(z, w_out.astype(BF16), h, i, i == depth - 1, norm_ffn, w_router_group,
                         b_router_group, w_router_expert, b_router_expert, w_exp_gate, w_exp_up,
                         w_exp_down, norm_ple, w_ple, w_ple_gate, b_ple_gate, norm_final, p)
    return h.reshape(batch, seq, d)
```

```python
import functools

import jax
import jax.numpy as jnp
from jax import lax
from jax.experimental import pallas as pl
from jax.experimental.pallas import tpu as pltpu
from jax.experimental.pallas import tpu_sc as plsc

F32 = jnp.float32
BF16 = jnp.bfloat16
I32 = jnp.int32
U32 = jnp.uint32

EPS = 1e-6
N_GROUPS = 4
EXPERTS_PER_GROUP = 8
N_EXPERTS = N_GROUPS * EXPERTS_PER_GROUP
RG_C = 8.0

LANES = 128
SUBLANES = 8
BF16_ROWS = 16
EXPERT_LANE0 = N_GROUPS
ROUTE_ROWS = 256
IN_AHEAD = 4
IN_SLOTS = IN_AHEAD + 2
OUT_SLOTS = 4
SC_CORES = 2
SC_SUBCORES = 16
SC_CHUNK = 64
COMBINE_PARTS = 4
M_E0, M_E1, M_R0, M_R1, M_W0, M_W1 = 0, 1, 2, 3, 4, 5
VMEM_LIMIT = 48 * 1024 * 1024


def _cparams(*sem):
    return pltpu.CompilerParams(dimension_semantics=sem, vmem_limit_bytes=VMEM_LIMIT)


def _rms(x, g):
    return x * lax.rsqrt(jnp.mean(x * x, axis=-1, keepdims=True) + EPS) * g


def _dot(a, b):
    return jnp.dot(a, b, preferred_element_type=F32)


def _pack_bf16_pairs(x):
    k = x.shape[1] // 2
    hi = lax.bitcast_convert_type(x[:, :k].astype(BF16).astype(F32), U32)
    lo = lax.bitcast_convert_type(x[:, k:].astype(BF16).astype(F32), U32)
    return hi | (lo >> 16)


def _unpack_bf16_pairs(p):
    hi = lax.bitcast_convert_type(p & jnp.uint32(0xFFFF0000), F32)
    lo = lax.bitcast_convert_type(p << 16, F32)
    return hi, lo


def _tile(n, want):
    t = min(n, want)
    assert n % t == 0, (n, want)
    return t


def _mix_a_in_kernel(h_ref, g_ref, w_ref, bg_ref, v_ref):
    d = h_ref.shape[1]
    xn = _rms(h_ref[...], g_ref[...]).astype(BF16)
    bg = _dot(xn, w_ref[:, 0:d])
    cg = _dot(xn, w_ref[:, d:2 * d])
    hh = _dot(xn, w_ref[:, 2 * d:3 * d])
    bg_ref[...] = bg.astype(BF16)
    v_ref[...] = (cg * hh).astype(BF16)


def _mix_a_in(h, g, w_in):
    n, d = h.shape
    tm = _tile(n, 512)
    return pl.pallas_call(
        _mix_a_in_kernel,
        grid=(n // tm,),
        in_specs=[pl.BlockSpec((tm, d), lambda i: (i, 0)),
                  pl.BlockSpec((1, d), lambda i: (0, 0)),
                  pl.BlockSpec((d, 3 * d), lambda i: (0, 0))],
        out_specs=[pl.BlockSpec((tm, d), lambda i: (i, 0)),
                   pl.BlockSpec((tm, d), lambda i: (i, 0))],
        out_shape=[jax.ShapeDtypeStruct((n, d), BF16), jax.ShapeDtypeStruct((n, d), BF16)],
        compiler_params=_cparams("parallel"),
        name="mix_a_in",
    )(h, g, w_in)


def _mix_a_conv_kernel(v_ref, vp_ref, vn_ref, bg_ref, cw_ref, z_ref, *, tiles_per_seq):
    i = pl.program_id(0)
    tm = v_ref.shape[0]
    v = v_ref[...].astype(F32)
    at_start = (i % tiles_per_seq) == 0
    at_end = (i % tiles_per_seq) == tiles_per_seq - 1
    prev_row = vp_ref[...].astype(F32)[BF16_ROWS - 1:BF16_ROWS, :]
    next_row = vn_ref[...].astype(F32)[0:1, :]
    prev_row = jnp.where(at_start, 0.0, prev_row)
    next_row = jnp.where(at_end, 0.0, next_row)
    row = lax.broadcasted_iota(I32, (tm, 1), 0)
    v_dn = jnp.where(row == 0, prev_row, pltpu.roll(v, 1, 0))
    v_up = jnp.where(row == tm - 1, next_row, pltpu.roll(v, tm - 1, 0))
    cw = cw_ref[...]
    u = cw[0:1, :] * v_dn + cw[1:2, :] * v + cw[2:3, :] * v_up
    z_ref[...] = (bg_ref[...].astype(F32) * u).astype(BF16)


def _mix_a_conv(v, bg, conv_w, seq):
    n, d = v.shape
    tm = _tile(seq, 512)
    hb = tm // BF16_ROWS
    nhalo = n // BF16_ROWS
    return pl.pallas_call(
        functools.partial(_mix_a_conv_kernel, tiles_per_seq=seq // tm),
        grid=(n // tm,),
        in_specs=[pl.BlockSpec((tm, d), lambda i: (i, 0)),
                  pl.BlockSpec((BF16_ROWS, d), lambda i: (jnp.maximum(i * hb - 1, 0), 0)),
                  pl.BlockSpec((BF16_ROWS, d), lambda i: (jnp.minimum((i + 1) * hb, nhalo - 1), 0)),
                  pl.BlockSpec((tm, d), lambda i: (i, 0)),
                  pl.BlockSpec(conv_w.shape, lambda i: (0, 0))],
        out_specs=pl.BlockSpec((tm, d), lambda i: (i, 0)),
        out_shape=jax.ShapeDtypeStruct((n, d), BF16),
        compiler_params=_cparams("parallel"),
        name="mix_a_conv",
    )(v, v, v, bg, conv_w)


def _mix_b_in_kernel(h_ref, hp_ref, hn_ref, g_ref, w_ref, yb_ref, u_ref, *, steps_per_seq, tt):
    i = pl.program_id(0)
    r = yb_ref.shape[1]
    tiles = h_ref.shape[0] // tt
    g = g_ref[...]
    xn = _rms(h_ref[...], g).astype(BF16)
    yb_ref[...] = jax.nn.gelu(_dot(xn, w_ref[:, 0:r])).astype(BF16)
    tm = h_ref.shape[0]
    xh = _rms(jnp.concatenate([hp_ref[...], hn_ref[...]], axis=0), g).astype(BF16)
    u_all = _dot(jnp.concatenate([xn, xh], axis=0), w_ref[:, r:2 * r])
    u = u_all[0:tm]
    uh = u_all[tm:tm + 2 * SUBLANES]
    at_start = (i % steps_per_seq) == 0
    at_end = (i % steps_per_seq) == steps_per_seq - 1
    prev2 = jnp.where(at_start, 0.0, uh[SUBLANES - 2:SUBLANES, :])
    next1 = jnp.where(at_end, 0.0, uh[SUBLANES:SUBLANES + 1, :])
    row = lax.broadcasted_iota(I32, (SUBLANES, 1), 0)
    for k in range(tiles):
        p2 = prev2 if k == 0 else u[k * tt - 2:k * tt]
        n1 = next1 if k == tiles - 1 else u[(k + 1) * tt:(k + 1) * tt + 1]
        pad = jnp.where(row == 0, p2[0:1], jnp.where(row == 1, p2[1:2], jnp.where(row == 2, n1, 0.0)))
        for s in range(r // LANES):
            lanes = slice(s * LANES, (s + 1) * LANES)
            u_ref[k, s, 0:tt, :] = u[k * tt:(k + 1) * tt, lanes]
            u_ref[k, s, tt:tt + SUBLANES, :] = pad[:, lanes]


def _mix_b_in(h, g, w_in, batch, seq, tt):
    n, d = h.shape
    r = w_in.shape[1] // 2
    tm = _tile(seq, 512)
    steps_per_seq = seq // tm
    tiles = tm // tt
    hb = tm // SUBLANES
    nhalo = n // SUBLANES
    return pl.pallas_call(
        functools.partial(_mix_b_in_kernel, steps_per_seq=steps_per_seq, tt=tt),
        grid=(n // tm,),
        in_specs=[pl.BlockSpec((tm, d), lambda i: (i, 0)),
                  pl.BlockSpec((SUBLANES, d), lambda i: (jnp.maximum(i * hb - 1, 0), 0)),
                  pl.BlockSpec((SUBLANES, d), lambda i: (jnp.minimum((i + 1) * hb, nhalo - 1), 0)),
                  pl.BlockSpec((1, d), lambda i: (0, 0)),
                  pl.BlockSpec((d, 2 * r), lambda i: (0, 0))],
        out_specs=[pl.BlockSpec((tm, r), lambda i: (i, 0)),
                   pl.BlockSpec((tiles, r // LANES, None, tt + SUBLANES, LANES),
                                lambda i: (i % steps_per_seq, 0, i // steps_per_seq, 0, 0))],
        out_shape=[jax.ShapeDtypeStruct((n, r), BF16),
                   jax.ShapeDtypeStruct((seq // tt, r // LANES, batch, tt + SUBLANES, LANES), F32)],
        compiler_params=_cparams("parallel"),
        name="mix_b_in",
    )(h, h, h, g, w_in)


def _rglru_gates(u, wr_ref, br_ref, wi_ref, bi_ref, lam_ref, a_s, g_s):
    ub = u.astype(BF16)
    t_r = jnp.tanh(_dot(ub, wr_ref[...]) + 0.5 * br_ref[...])
    t_i = jnp.tanh(_dot(ub, wi_ref[...]) + 0.5 * bi_ref[...])
    c = (-0.25 * RG_C) * jax.nn.softplus(-lam_ref[...])
    th = jnp.tanh(c + c * t_r)
    q = 1.0 / (1.0 - th)
    a_s[...] = (1.0 + th) * q
    nth = -th
    root = jnp.where(nth > 0.0, nth * lax.rsqrt(nth), 0.0)
    g_s[...] = (q * root) * (u + u * t_i)


def _rglru_fwd_kernel(u_ref, cw_ref, cb_ref, wr_ref, br_ref, wi_ref, bi_ref, lam_ref,
                      hf_ref, uc_ref, x_s, a_s, g_s, carry_s, *, tt):
    nb = SUBLANES
    ttp = tt + SUBLANES
    rows = tt * nb
    slabs = u_ref.shape[0]

    @pl.when(pl.program_id(1) == 0)
    def _():
        carry_s[...] = jnp.zeros_like(carry_s)

    def gather_t(tl, dst):
        for s in range(slabs):
            x_s[pl.ds(dst, nb), s * LANES:(s + 1) * LANES] = u_ref[s, pl.ds(tl, nb, stride=ttp), :]

    def load_t(tl, c):
        gather_t(tl, pl.multiple_of((tl + 2) * nb, nb))
        return c

    lax.fori_loop(0, tt, load_t, 0, unroll=8)
    gather_t(tt, 0)
    gather_t(tt + 1, nb)
    gather_t(tt + 2, (tt + 2) * nb)

    cw = cw_ref[...]
    u = (cw[0:1, :] * x_s[0:rows, :] + cw[1:2, :] * x_s[nb:nb + rows, :]
         + cw[2:3, :] * x_s[2 * nb:2 * nb + rows, :] + cw[3:4, :] * x_s[3 * nb:3 * nb + rows, :]
         + cb_ref[...])
    uc_ref[...] = u
    _rglru_gates(u, wr_ref, br_ref, wi_ref, bi_ref, lam_ref, a_s, g_s)

    def step(tl, h):
        base = pl.multiple_of(tl * nb, nb)
        h = a_s[pl.ds(base, nb), :] * h + g_s[pl.ds(base, nb), :]
        hf_ref[pl.ds(base, nb), :] = h
        return h

    carry_s[...] = lax.fori_loop(0, tt, step, carry_s[...], unroll=8)


def _rglru_rev_kernel(uc_ref, wr_ref, br_ref, wi_ref, bi_ref, lam_ref, hf_ref, yb_ref,
                      z_ref, a_s, g_s, carry_s, nat_s, *, tt):
    nb = SUBLANES
    ttp = tt + SUBLANES
    slabs = nat_s.shape[0]

    @pl.when(pl.program_id(1) == 0)
    def _():
        carry_s[...] = jnp.zeros_like(carry_s)

    _rglru_gates(uc_ref[...], wr_ref, br_ref, wi_ref, bi_ref, lam_ref, a_s, g_s)

    def step(k, h):
        tl = tt - 1 - k
        base = pl.multiple_of(tl * nb, nb)
        h = a_s[pl.ds(base, nb), :] * h + g_s[pl.ds(base, nb), :]
        hs = h + hf_ref[pl.ds(base, nb), :]
        for s in range(slabs):
            nat_s[s, pl.ds(tl, nb, stride=ttp), :] = hs[:, s * LANES:(s + 1) * LANES]
        return h

    carry_s[...] = lax.fori_loop(0, tt, step, carry_s[...], unroll=8)
    for b in range(nb):
        for s in range(slabs):
            lanes = slice(s * LANES, (s + 1) * LANES)
            hb = nat_s[s, b * ttp:b * ttp + tt, :]
            z_ref[b, :, lanes] = (yb_ref[b, :, lanes].astype(F32) * hb).astype(BF16)


def _rglru(u_tiles, yb, conv_w, conv_b, w_r, b_r, w_i, b_i, lam, seq, tt):
    nt, nslab, batch, ttp, _ = u_tiles.shape
    r = nslab * LANES
    assert batch == SUBLANES, "time-major tile layout puts the batch on the 8 sublanes"
    nh, bk = w_r.shape[1], w_r.shape[2]
    slabs = bk // LANES
    kw = conv_w.shape[0]
    rows = tt * batch
    gates = [w_r, b_r.reshape(2, 1, r), w_i, b_i.reshape(2, 1, r), lam.reshape(2, 1, r)]

    def gate_specs(d):
        vec = pl.BlockSpec((None, 1, bk), lambda h, jj: (d, 0, h))
        mat = pl.BlockSpec((None, None, bk, bk), lambda h, jj: (d, h, 0, 0))
        return [mat, vec, mat, vec, vec]

    tm_spec = pl.BlockSpec((rows, bk), lambda h, jj: (jj, h))
    tm_rev = pl.BlockSpec((rows, bk), lambda h, jj: (nt - 1 - jj, h))
    tm_shape = jax.ShapeDtypeStruct((seq * batch, r), F32)
    state = [pltpu.VMEM((rows, bk), F32), pltpu.VMEM((rows, bk), F32), pltpu.VMEM((batch, bk), F32)]

    hf, uc = pl.pallas_call(
        functools.partial(_rglru_fwd_kernel, tt=tt),
        grid=(nh, nt),
        in_specs=[pl.BlockSpec((None, slabs, batch * ttp, LANES), lambda h, jj: (jj, h, 0, 0)),
                  pl.BlockSpec((kw, bk), lambda h, jj: (0, h)),
                  pl.BlockSpec((1, bk), lambda h, jj: (0, h))] + gate_specs(0),
        out_specs=[tm_spec, tm_spec],
        out_shape=[tm_shape, tm_shape],
        scratch_shapes=[pltpu.VMEM(((tt + 3) * batch, bk), F32)] + state,
        compiler_params=_cparams("parallel", "arbitrary"),
        name="rglru_fwd",
    )(u_tiles.reshape(nt, nslab, batch * ttp, LANES), conv_w, conv_b, *gates)

    nat_spec = pl.BlockSpec((batch, tt, bk), lambda h, jj: (0, nt - 1 - jj, h))
    z = pl.pallas_call(
        functools.partial(_rglru_rev_kernel, tt=tt),
        grid=(nh, nt),
        in_specs=[tm_rev] + gate_specs(1) + [tm_rev, nat_spec],
        out_specs=nat_spec,
        out_shape=jax.ShapeDtypeStruct((batch, seq, r), BF16),
        scratch_shapes=state + [pltpu.VMEM((slabs, batch * ttp, LANES), F32)],
        compiler_params=_cparams("parallel", "arbitrary"),
        name="rglru_rev",
    )(uc, *gates, hf, yb.reshape(batch, seq, r))
    return z.reshape(-1, r)


def _route(logits):
    lane = lax.broadcasted_iota(I32, logits.shape, 1).astype(F32)
    neg = jnp.float32(-jnp.inf)
    nolane = jnp.float32(LANES)
    gmask = lane < N_GROUPS
    gmax = jnp.max(jnp.where(gmask, logits, neg), axis=1, keepdims=True)
    eg = jnp.where(gmask, jnp.exp(logits - gmax), 0.0)
    gsum = jnp.sum(eg, axis=1, keepdims=True)
    pg = eg / gsum
    pg_top = 1.0 / gsum
    g_top = jnp.min(jnp.where(gmask & (pg == pg_top), lane, nolane), axis=1, keepdims=True)
    lo = EXPERT_LANE0 + g_top * EXPERTS_PER_GROUP
    emask = (lane >= lo) & (lane < lo + EXPERTS_PER_GROUP)
    emax = jnp.max(jnp.where(emask, logits, neg), axis=1, keepdims=True)
    ee = jnp.where(emask, jnp.exp(logits - emax), 0.0)
    esum = jnp.sum(ee, axis=1, keepdims=True)
    pe = jnp.where(emask, ee / esum, -1.0)
    p1 = 1.0 / esum
    i1 = jnp.min(jnp.where(pe == p1, lane, nolane), axis=1, keepdims=True)
    pe2 = jnp.where(lane == i1, -1.0, pe)
    p2 = jnp.max(pe2, axis=1, keepdims=True)
    i2 = jnp.min(jnp.where(pe2 == p2, lane, nolane), axis=1, keepdims=True)
    psum = p1 + p2
    return i1, i2, pg_top * (p1 / psum), pg_top * (p2 / psum)


def _post_mix_kernel(z_ref, wo_ref, h_ref, g_ref, wr_ref, br_ref,
                     h1_ref, xn_ref, meta_ref, meta_t_ref, cnt_ref, carry_ref, logits_s):
    i = pl.program_id(0)
    tm = z_ref.shape[0]

    @pl.when(i == 0)
    def _():
        logits_s[...] = jnp.zeros_like(logits_s)

    @pl.when(i <= 1)
    def _():
        carry_ref[...] = jnp.zeros_like(carry_ref)

    logits = logits_s[...]

    h1 = h_ref[...] + _dot(z_ref[...], wo_ref[...])
    h1_ref[...] = h1
    xn = _rms(h1, g_ref[...])
    xn_ref[...] = _pack_bf16_pairs(xn)
    logits_s[...] = _dot(xn.astype(BF16), wr_ref[...].astype(BF16)) + br_ref[...]

    i1, i2, w0, w1 = _route(logits)

    lane = lax.broadcasted_iota(I32, (tm, LANES), 1).astype(F32)
    oh0 = lane == i1
    oh1 = lane == i2
    both = (oh0 | oh1).astype(BF16)
    ri = lax.broadcasted_iota(I32, (tm, tm), 0)
    ci = lax.broadcasted_iota(I32, (tm, tm), 1)
    before = (ri > ci).astype(BF16)
    cnt_before = _dot(before, both) + carry_ref[...]
    rank0 = jnp.sum(jnp.where(oh0, cnt_before, 0.0), axis=1, keepdims=True)
    rank1 = jnp.sum(jnp.where(oh1, cnt_before, 0.0), axis=1, keepdims=True)
    carry = carry_ref[...] + jnp.sum(both.astype(F32), axis=0, keepdims=True)
    carry_ref[...] = carry
    cnt_ref[...] = carry

    e0 = i1 - EXPERT_LANE0
    e1 = i2 - EXPERT_LANE0
    meta = jnp.zeros((tm, LANES), F32)
    for ln, val in ((M_E0, e0), (M_E1, e1), (M_R0, rank0), (M_R1, rank1), (M_W0, w0), (M_W1, w1)):
        meta = jnp.where(lane == ln, val, meta)
    meta_ref[...] = meta
    meta_t_ref[...] = meta.T[0:SUBLANES, :]


def _post_mix(z, w_out, h, g, w_router, b_router):
    n, d = h.shape
    k = z.shape[1]
    tm = _tile(n, 512)
    nt = n // tm

    def proj(i):
        return (jnp.minimum(i, nt - 1), 0)

    def routed(i):
        return (jnp.maximum(i - 1, 0), 0)

    return pl.pallas_call(
        _post_mix_kernel,
        grid=(nt + 1,),
        in_specs=[pl.BlockSpec((tm, k), proj),
                  pl.BlockSpec((k, d), lambda i: (0, 0)),
                  pl.BlockSpec((tm, d), proj),
                  pl.BlockSpec((1, d), lambda i: (0, 0)),
                  pl.BlockSpec((d, LANES), lambda i: (0, 0)),
                  pl.BlockSpec((1, LANES), lambda i: (0, 0))],
        out_specs=[pl.BlockSpec((tm, d), proj),
                   pl.BlockSpec((tm, d // 2), proj),
                   pl.BlockSpec((tm, LANES), routed),
                   pl.BlockSpec((SUBLANES, tm), lambda i: (0, jnp.maximum(i - 1, 0))),
                   pl.BlockSpec((1, LANES), lambda i: (0, 0))],
        out_shape=[jax.ShapeDtypeStruct((n, d), F32), jax.ShapeDtypeStruct((n, d // 2), U32),
                   jax.ShapeDtypeStruct((n, LANES), F32), jax.ShapeDtypeStruct((SUBLANES, n), F32),
                   jax.ShapeDtypeStruct((1, LANES), F32)],
        scratch_shapes=[pltpu.VMEM((1, LANES), F32), pltpu.VMEM((tm, LANES), F32)],
        compiler_params=_cparams("arbitrary"),
        name="post_mix_router",
    )(z, w_out, h, g, w_router, b_router)


def _plan_kernel(cnt_ref, meta_t_ref, dest_ref, estart_ref, ecount_ref, nblk_ref, pstart_ref):
    def per_expert(e, acc):
        nb_e = (cnt_ref[e] + (ROUTE_ROWS - 1)) // ROUTE_ROWS
        pstart_ref[e] = acc * ROUTE_ROWS
        estart_ref[e] = acc
        ecount_ref[e] = nb_e
        return acc + nb_e

    nblk_ref[0] = lax.fori_loop(0, N_EXPERTS, per_expert, 0)

    e0 = meta_t_ref[M_E0:M_E0 + 1, :]
    e1 = meta_t_ref[M_E1:M_E1 + 1, :]
    d0 = meta_t_ref[M_R0:M_R0 + 1, :]
    d1 = meta_t_ref[M_R1:M_R1 + 1, :]
    for e in range(N_EXPERTS):
        ps = pstart_ref[e].astype(F32)
        d0 = d0 + jnp.where(e0 == e, ps, 0.0)
        d1 = d1 + jnp.where(e1 == e, ps, 0.0)
    dest_ref[...] = jnp.zeros_like(dest_ref)
    dest_ref[0:1, :] = d0.astype(I32)
    dest_ref[1:2, :] = d1.astype(I32)


def _plan(cnt_i32, meta_t):
    n = meta_t.shape[1]
    smem = pl.BlockSpec(memory_space=pltpu.SMEM)
    return pl.pallas_call(
        _plan_kernel,
        in_specs=[smem, pl.BlockSpec((SUBLANES, n), lambda: (0, 0))],
        out_specs=[pl.BlockSpec((SUBLANES, n), lambda: (0, 0)), smem, smem, smem],
        out_shape=[jax.ShapeDtypeStruct((SUBLANES, n), I32),
                   jax.ShapeDtypeStruct((N_EXPERTS,), I32),
                   jax.ShapeDtypeStruct((N_EXPERTS,), I32),
                   jax.ShapeDtypeStruct((1,), I32)],
        scratch_shapes=[pltpu.SMEM((N_EXPERTS,), I32)],
        name="dispatch_plan",
    )(cnt_i32, meta_t)


def _sc_mesh():
    return plsc.VectorSubcoreMesh(core_axis_name="c", subcore_axis_name="s",
                                  num_cores=SC_CORES, num_subcores=SC_SUBCORES)


def _sc_scatter_rows(x, idx0, idx1, p_rows):
    m, k = x.shape
    per_w = m // (SC_CORES * SC_SUBCORES)
    assert per_w % SC_CHUNK == 0 and per_w * SC_CORES * SC_SUBCORES == m

    nchunk = per_w // SC_CHUNK
    buf = [pltpu.VMEM((SC_CHUNK,), I32), pltpu.VMEM((SC_CHUNK,), I32),
           pltpu.VMEM((SC_CHUNK, k), x.dtype),
           pltpu.SemaphoreType.DMA, pltpu.SemaphoreType.DMA, pltpu.SemaphoreType.DMA]

    @functools.partial(
        pl.kernel, mesh=_sc_mesh(), out_type=jax.ShapeDtypeStruct((p_rows, k), x.dtype),
        scratch_types=buf + buf, name="sc_scatter_rows")
    def scatter(x_hbm, i0_hbm, i1_hbm, out_hbm, *scratch):
        base = (lax.axis_index("s") * SC_CORES + lax.axis_index("c")) * per_w
        slots = (scratch[:6], scratch[6:])

        def loads(c):
            i0_v, i1_v, rows_v, lsem, _, _ = slots[c % 2]
            off = pl.multiple_of(base + c * SC_CHUNK, SC_CHUNK)
            return [pltpu.make_async_copy(i0_hbm.at[pl.ds(off, SC_CHUNK)], i0_v, lsem),
                    pltpu.make_async_copy(i1_hbm.at[pl.ds(off, SC_CHUNK)], i1_v, lsem),
                    pltpu.make_async_copy(x_hbm.at[pl.ds(off, SC_CHUNK)], rows_v, lsem)]

        def puts(c):
            i0_v, i1_v, rows_v, _, psem0, psem1 = slots[c % 2]
            return [pltpu.make_async_copy(rows_v, out_hbm.at[i0_v], psem0),
                    pltpu.make_async_copy(rows_v, out_hbm.at[i1_v], psem1)]

        for d in loads(0):
            d.start()
        for c in range(nchunk):
            for d in loads(c):
                d.wait()
            for d in puts(c):
                d.start()
            if c >= 1:
                for d in puts(c - 1):
                    d.wait()
            if c + 1 < nchunk:
                for d in loads(c + 1):
                    d.start()
        for d in puts(nchunk - 1):
            d.wait()

    return scatter(x, idx0, idx1)


def _expert_kernel(es_ref, ec_ref, nblk_ref, cnt_ref, xs_ref, wg_ref, wu_ref, wd_ref, ys_ref,
                   x_s, y_s, wg_s, wu_s, wd_s, sem):
    e = pl.program_id(0)
    nb = ec_ref[e]
    b0 = es_ref[e]
    k = x_s.shape[2]
    nb_max = ys_ref.shape[0] // ROUTE_ROWS

    def rows(b):
        return pl.ds(pl.multiple_of(b * ROUTE_ROWS, ROUTE_ROWS), ROUTE_ROWS)

    nblk = nblk_ref[0]

    def copy_in(b):
        slot = lax.rem(b, IN_SLOTS)
        return pltpu.make_async_copy(xs_ref.at[rows(b)], x_s.at[slot], sem.at[0, slot])

    def copy_out(b):
        slot = lax.rem(b, OUT_SLOTS)
        return pltpu.make_async_copy(y_s.at[slot], ys_ref.at[rows(b)], sem.at[1, slot])

    @pl.when(e == 0)
    def _():
        for b in range(IN_AHEAD):
            @pl.when(b < nblk)
            def _(b=b):
                copy_in(b).start()

    @pl.when(nb > 0)
    def _():
        wg_s[...] = wg_ref[...].astype(BF16)
        wu_s[...] = wu_ref[...].astype(BF16)
        wd_s[...] = wd_ref[...].astype(BF16)

        def mlp(b, j, width):
            row = lax.broadcasted_iota(I32, (ROUTE_ROWS, 1), 0)
            xp = jnp.concatenate(
                [jnp.where(row < cnt_ref[e] - (j + i) * ROUTE_ROWS,
                           x_s[lax.rem(b + i, IN_SLOTS)], jnp.uint32(0)) for i in range(width)], axis=0)
            x_hi, x_lo = _unpack_bf16_pairs(xp)
            x_hi = x_hi.astype(BF16)
            x_lo = x_lo.astype(BF16)
            g = _dot(x_hi, wg_s[0:k, :]) + _dot(x_lo, wg_s[k:2 * k, :])
            u = _dot(x_hi, wu_s[0:k, :]) + _dot(x_lo, wu_s[k:2 * k, :])
            hb = (jax.nn.silu(g) * u).astype(BF16)
            y = _pack_bf16_pairs(_dot(hb, wd_s[...]))
            return [y[i * ROUTE_ROWS:(i + 1) * ROUTE_ROWS] for i in range(width)]

        def unit(j, width):
            b = b0 + j
            for i in range(width):
                copy_in(b + i).wait()
            for i in range(width):
                @pl.when(b + i + IN_AHEAD < nblk)
                def _(i=i):
                    copy_in(b + i + IN_AHEAD).start()
            ys = mlp(b, j, width)
            for i in range(width):
                @pl.when(b + i >= OUT_SLOTS)
                def _(i=i):
                    copy_out(b + i - OUT_SLOTS).wait()
            for i in range(width):
                y_s[lax.rem(b + i, OUT_SLOTS)] = ys[i]
                copy_out(b + i).start()

        def pair(jj, c):
            unit(2 * jj, 2)
            return c

        lax.fori_loop(0, nb // 2, pair, 0)

        @pl.when(nb % 2 == 1)
        def _():
            unit(nb - 1, 1)

    @pl.when(e == pl.num_programs(0) - 1)
    def _():
        for back in range(OUT_SLOTS, 0, -1):
            @pl.when(nblk >= back)
            def _(back=back):
                copy_out(nblk - back).wait()

    @pl.when(e == pl.num_programs(0) - 1)
    def _():
        y_s[0] = jnp.zeros_like(y_s[0])

        def zero(b):
            return pltpu.make_async_copy(y_s.at[0], ys_ref.at[rows(b)], sem.at[1, 0])

        def start_zero(b, c):
            zero(b).start()
            return c

        def wait_zero(b, c):
            zero(b).wait()
            return c

        lax.fori_loop(nblk_ref[0], nb_max, start_zero, 0)
        lax.fori_loop(nblk_ref[0], nb_max, wait_zero, 0)


def _experts(estart, ecount, nblk, cnt, xs, w_gate, w_up, w_down, layer):
    p, k = xs.shape
    d, f = w_gate.shape[2], w_gate.shape[3]

    def w_map(e, es, ec, nb, cn):
        return (layer, e, 0, 0)

    grid_spec = pltpu.PrefetchScalarGridSpec(
        num_scalar_prefetch=4,
        grid=(N_EXPERTS,),
        in_specs=[pl.BlockSpec(memory_space=pl.ANY),
                  pl.BlockSpec((None, None, d, f), w_map),
                  pl.BlockSpec((None, None, d, f), w_map),
                  pl.BlockSpec((None, None, f, d), w_map)],
        out_specs=pl.BlockSpec(memory_space=pl.ANY),
        scratch_shapes=[pltpu.VMEM((IN_SLOTS, ROUTE_ROWS, k), U32),
                        pltpu.VMEM((OUT_SLOTS, ROUTE_ROWS, k), U32),
                        pltpu.VMEM((d, f), BF16), pltpu.VMEM((d, f), BF16), pltpu.VMEM((f, d), BF16),
                        pltpu.SemaphoreType.DMA((2, max(IN_SLOTS, OUT_SLOTS)))],
    )
    return pl.pallas_call(
        _expert_kernel,
        grid_spec=grid_spec,
        out_shape=jax.ShapeDtypeStruct((p, k), U32),
        compiler_params=_cparams("arbitrary"),
        name="expert_mlp",
    )(estart, ecount, nblk, cnt, xs, w_gate, w_up, w_down)


def _sc_gather_pair(table, idx0, idx1, start, count):
    k = table.shape[1]
    per_w = count // (SC_CORES * SC_SUBCORES)
    assert per_w % SC_CHUNK == 0 and per_w * SC_CORES * SC_SUBCORES == count
    assert start % SC_CHUNK == 0
    rows = jax.ShapeDtypeStruct((count, k), table.dtype)

    @functools.partial(
        pl.kernel, mesh=_sc_mesh(), out_type=(rows, rows),
        scratch_types=[pltpu.VMEM((SC_CHUNK,), I32), pltpu.VMEM((SC_CHUNK,), I32),
                       pltpu.VMEM((SC_CHUNK, k), table.dtype), pltpu.VMEM((SC_CHUNK, k), table.dtype),
                       pltpu.SemaphoreType.DMA, pltpu.SemaphoreType.DMA,
                       pltpu.SemaphoreType.DMA, pltpu.SemaphoreType.DMA],
        name="sc_gather_pair")
    def gather(table_hbm, i0_hbm, i1_hbm, o0_hbm, o1_hbm, i0_v, i1_v, r0_v, r1_v,
               sem0, sem1, sem2, sem3):
        base = (lax.axis_index("s") * SC_CORES + lax.axis_index("c")) * per_w

        @pl.loop(0, per_w // SC_CHUNK)
        def _(c):
            dst = pl.multiple_of(base + c * SC_CHUNK, SC_CHUNK)
            src = pl.multiple_of(start + dst, SC_CHUNK)
            pltpu.sync_copy(i0_hbm.at[pl.ds(src, SC_CHUNK)], i0_v)
            pltpu.sync_copy(i1_hbm.at[pl.ds(src, SC_CHUNK)], i1_v)
            get0 = pltpu.async_copy(table_hbm.at[i0_v], r0_v, sem0)
            get1 = pltpu.async_copy(table_hbm.at[i1_v], r1_v, sem1)
            get0.wait()
            put0 = pltpu.async_copy(r0_v, o0_hbm.at[pl.ds(dst, SC_CHUNK)], sem2)
            get1.wait()
            put1 = pltpu.async_copy(r1_v, o1_hbm.at[pl.ds(dst, SC_CHUNK)], sem3)
            put0.wait()
            put1.wait()

    return gather(table, idx0, idx1)


def _combine_kernel(*refs, final):
    (ya_ref, yb_ref, h1_ref, meta_ref, p_ref, gp_ref, wpg_ref, bpg_ref, wp_ref,
     gf_ref) = refs[:10]
    out_ref = refs[-1]
    meta = meta_ref[...]
    w0 = meta[:, M_W0:M_W0 + 1]
    w1 = meta[:, M_W1:M_W1 + 1]
    ya_hi, ya_lo = _unpack_bf16_pairs(ya_ref[...])
    yb_hi, yb_lo = _unpack_bf16_pairs(yb_ref[...])
    y = jnp.concatenate([ya_hi * w0 + yb_hi * w1, ya_lo * w0 + yb_lo * w1], axis=1)
    h2 = h1_ref[...] + y
    xn = _rms(h2, gp_ref[...]).astype(BF16)
    gate = jax.nn.sigmoid(_dot(xn, wpg_ref[...]) + bpg_ref[...])
    pp = _dot(p_ref[...].astype(BF16), wp_ref[...])
    h3 = h2 + gate * pp
    if final:
        h3 = _rms(h3, gf_ref[...])
    out_ref[...] = h3


def _combine(dest, ys, h1, meta, p, layer, g_ple, w_ple_gate, b_ple_gate, w_ple, g_final, final):
    n, d = h1.shape
    pd = p.shape[2]
    k = ys.shape[1]
    n_part = n // COMBINE_PARTS
    tmc = _tile(n_part, 512)
    tiles = n_part // tmc
    vec = pl.BlockSpec((1, d), lambda i: (0, 0))
    out = None
    for q in range(COMBINE_PARTS):
        ya, yb = _sc_gather_pair(ys, dest[0], dest[1], q * n_part, n_part)

        def tok(i, q=q):
            return (q * tiles + i, 0)

        in_specs = [pl.BlockSpec((tmc, k), lambda i: (i, 0)),
                    pl.BlockSpec((tmc, k), lambda i: (i, 0)),
                    pl.BlockSpec((tmc, d), tok),
                    pl.BlockSpec((tmc, LANES), tok),
                    pl.BlockSpec((None, tmc, pd), lambda i, q=q: (layer, q * tiles + i, 0)),
                    vec,
                    pl.BlockSpec((d, d), lambda i: (0, 0)),
                    vec,
                    pl.BlockSpec((pd, d), lambda i: (0, 0)),
                    vec]
        args = [ya, yb, h1, meta, p, g_ple, w_ple_gate, b_ple_gate, w_ple, g_final]
        aliases = {}
        if out is not None:
            in_specs.append(pl.BlockSpec(memory_space=pl.ANY))
            args.append(out)
            aliases = {len(args) - 1: 0}
        out = pl.pallas_call(
            functools.partial(_combine_kernel, final=final),
            grid=(tiles,),
            in_specs=in_specs,
            out_specs=pl.BlockSpec((tmc, d), tok),
            out_shape=jax.ShapeDtypeStruct((n, d), F32),
            input_output_aliases=aliases,
            compiler_params=_cparams("parallel"),
            name="combine_ple",
        )(*args)
    return out


def _row(v):
    return v.reshape(1, -1)


def _moe_and_ple(z, w_out, h, i, final, norm_ffn, w_router_group, b_router_group, w_router_expert,
                 b_router_expert, w_exp_gate, w_exp_up, w_exp_down, norm_ple, w_ple, w_ple_gate,
                 b_ple_gate, norm_final, p):
    n, d = h.shape
    pad = LANES - N_GROUPS - N_EXPERTS
    w_router = jnp.concatenate(
        [w_router_group[i], w_router_expert[i], jnp.zeros((d, pad), F32)], axis=1)
    b_router = jnp.concatenate(
        [b_router_group[i], b_router_expert[i], jnp.zeros((pad,), F32)]).reshape(1, LANES)
    h1, xn, meta, meta_t, cnt = _post_mix(z, w_out, h, _row(norm_ffn[i]), w_router, b_router)
    cnt_i32 = cnt[0, EXPERT_LANE0:EXPERT_LANE0 + N_EXPERTS].astype(I32)
    nb_max = (2 * n) // ROUTE_ROWS + N_EXPERTS
    dest, estart, ecount, nblk = _plan(cnt_i32, meta_t)
    xs = _sc_scatter_rows(xn, dest[0], dest[1], nb_max * ROUTE_ROWS)
    ys = _experts(estart, ecount, nblk, cnt_i32, xs, w_exp_gate, w_exp_up, w_exp_down, i)
    return _combine(dest, ys, h1, meta, p.reshape(p.shape[0], n, -1), i, _row(norm_ple[i]),
                    w_ple_gate[i].astype(BF16), _row(b_ple_gate[i]), w_ple[i].astype(BF16),
                    _row(norm_final), final)


def kernel(x, p, norm_mix, w_in_a, conv_a, w_out_a, w_in_b, conv_b, conv_bias_b, w_rgate_b, b_rgate_b, w_igate_b, b_igate_b, lam_b, w_out_b, norm_ffn, w_router_group, b_router_group, w_router_expert, b_router_expert, w_exp_gate, w_exp_up, w_exp_down, norm_ple, w_ple, w_ple_gate, b_ple_gate, norm_final):
    batch, seq, d = x.shape
    depth = p.shape[0]
    n = batch * seq
    h = x.reshape(n, d)
    for i in range(depth):
        j = i // 2
        if i % 2 == 0:
            bg, v = _mix_a_in(h, _row(norm_mix[i]), w_in_a[j].astype(BF16))
            z = _mix_a_conv(v, bg, conv_a[j], seq)
            w_out = w_out_a[j]
        else:
            tt = _tile(seq, 256)
            yb, u_tiles = _mix_b_in(h, _row(norm_mix[i]), w_in_b[j].astype(BF16), batch, seq, tt)
            z = _rglru(u_tiles, yb, conv_b[j], _row(conv_bias_b[j]),
                       (0.5 * w_rgate_b[j]).astype(BF16), b_rgate_b[j],
                       (0.5 * w_igate_b[j]).astype(BF16),
                       b_igate_b[j], lam_b[j], seq, tt)
            w_out = w_out_b[j]
        h = _moe_and_ple(z, w_out.astype(BF16), h, i, i == depth - 1, norm_ffn, w_router_group,
                         b_router_group, w_router_expert, b_router_expert, w_exp_gate, w_exp_up,
                         w_exp_down, norm_ple, w_ple, w_ple_gate, b_ple_gate, norm_final, p)
    return h.reshape(batch, seq, d)
```

```python
import functools

import jax
import jax.numpy as jnp
from jax import lax
from jax.experimental import pallas as pl
from jax.experimental.pallas import tpu as pltpu
from jax.experimental.pallas import tpu_sc as plsc

F32 = jnp.float32
BF16 = jnp.bfloat16
I32 = jnp.int32
U32 = jnp.uint32

EPS = 1e-6
N_GROUPS = 4
EXPERTS_PER_GROUP = 8
N_EXPERTS = N_GROUPS * EXPERTS_PER_GROUP
RG_C = 8.0

LANES = 128
SUBLANES = 8
BF16_ROWS = 16
EXPERT_LANE0 = N_GROUPS
ROUTE_ROWS = 256
IN_AHEAD = 4
IN_SLOTS = IN_AHEAD + 2
OUT_SLOTS = 4
SC_CORES = 2
SC_SUBCORES = 16
SC_CHUNK = 64
COMBINE_PARTS = 4
M_E0, M_E1, M_R0, M_R1, M_W0, M_W1 = 0, 1, 2, 3, 4, 5
VMEM_LIMIT = 48 * 1024 * 1024


def _cparams(*sem):
    return pltpu.CompilerParams(dimension_semantics=sem, vmem_limit_bytes=VMEM_LIMIT)


def _rms(x, g):
    return x * lax.rsqrt(jnp.mean(x * x, axis=-1, keepdims=True) + EPS) * g


def _dot(a, b):
    return jnp.dot(a, b, preferred_element_type=F32)


def _pack_bf16_pairs(x):
    k = x.shape[1] // 2
    hi = lax.bitcast_convert_type(x[:, :k].astype(BF16).astype(F32), U32)
    lo = lax.bitcast_convert_type(x[:, k:].astype(BF16).astype(F32), U32)
    return hi | (lo >> 16)


def _unpack_bf16_pairs(p):
    hi = lax.bitcast_convert_type(p & jnp.uint32(0xFFFF0000), F32)
    lo = lax.bitcast_convert_type(p << 16, F32)
    return hi, lo


def _tile(n, want):
    t = min(n, want)
    assert n % t == 0, (n, want)
    return t


def _mix_a_in_kernel(h_ref, g_ref, w_ref, bg_ref, v_ref):
    d = h_ref.shape[1]
    xn = _rms(h_ref[...], g_ref[...]).astype(BF16)
    bg = _dot(xn, w_ref[:, 0:d])
    cg = _dot(xn, w_ref[:, d:2 * d])
    hh = _dot(xn, w_ref[:, 2 * d:3 * d])
    bg_ref[...] = bg.astype(BF16)
    v_ref[...] = (cg * hh).astype(BF16)


def _mix_a_in(h, g, w_in):
    n, d = h.shape
    tm = _tile(n, 1024)
    return pl.pallas_call(
        _mix_a_in_kernel,
        grid=(n // tm,),
        in_specs=[pl.BlockSpec((tm, d), lambda i: (i, 0)),
                  pl.BlockSpec((1, d), lambda i: (0, 0)),
                  pl.BlockSpec((d, 3 * d), lambda i: (0, 0), pipeline_mode=pl.Buffered(1))],
        out_specs=[pl.BlockSpec((tm, d), lambda i: (i, 0)),
                   pl.BlockSpec((tm, d), lambda i: (i, 0))],
        out_shape=[jax.ShapeDtypeStruct((n, d), BF16), jax.ShapeDtypeStruct((n, d), BF16)],
        compiler_params=_cparams("parallel"),
        name="mix_a_in",
    )(h, g, w_in)


def _mix_a_conv_kernel(v_ref, vp_ref, vn_ref, bg_ref, cw_ref, z_ref, *, tiles_per_seq):
    i = pl.program_id(0)
    tm = v_ref.shape[0]
    v = v_ref[...].astype(F32)
    at_start = (i % tiles_per_seq) == 0
    at_end = (i % tiles_per_seq) == tiles_per_seq - 1
    prev_row = vp_ref[...].astype(F32)[BF16_ROWS - 1:BF16_ROWS, :]
    next_row = vn_ref[...].astype(F32)[0:1, :]
    prev_row = jnp.where(at_start, 0.0, prev_row)
    next_row = jnp.where(at_end, 0.0, next_row)
    row = lax.broadcasted_iota(I32, (tm, 1), 0)
    v_dn = jnp.where(row == 0, prev_row, pltpu.roll(v, 1, 0))
    v_up = jnp.where(row == tm - 1, next_row, pltpu.roll(v, tm - 1, 0))
    cw = cw_ref[...]
    u = cw[0:1, :] * v_dn + cw[1:2, :] * v + cw[2:3, :] * v_up
    z_ref[...] = (bg_ref[...].astype(F32) * u).astype(BF16)


def _mix_a_conv(v, bg, conv_w, seq):
    n, d = v.shape
    tm = _tile(seq, 512)
    hb = tm // BF16_ROWS
    nhalo = n // BF16_ROWS
    return pl.pallas_call(
        functools.partial(_mix_a_conv_kernel, tiles_per_seq=seq // tm),
        grid=(n // tm,),
        in_specs=[pl.BlockSpec((tm, d), lambda i: (i, 0)),
                  pl.BlockSpec((BF16_ROWS, d), lambda i: (jnp.maximum(i * hb - 1, 0), 0)),
                  pl.BlockSpec((BF16_ROWS, d), lambda i: (jnp.minimum((i + 1) * hb, nhalo - 1), 0)),
                  pl.BlockSpec((tm, d), lambda i: (i, 0)),
                  pl.BlockSpec(conv_w.shape, lambda i: (0, 0))],
        out_specs=pl.BlockSpec((tm, d), lambda i: (i, 0)),
        out_shape=jax.ShapeDtypeStruct((n, d), BF16),
        compiler_params=_cparams("parallel"),
        name="mix_a_conv",
    )(v, v, v, bg, conv_w)


def _mix_b_in_kernel(h_ref, hp_ref, hn_ref, g_ref, w_ref, yb_ref, u_ref, *, steps_per_seq, tt):
    i = pl.program_id(0)
    r = yb_ref.shape[1]
    tiles = h_ref.shape[0] // tt
    g = g_ref[...]
    xn = _rms(h_ref[...], g).astype(BF16)
    yb_ref[...] = jax.nn.gelu(_dot(xn, w_ref[:, 0:r])).astype(BF16)
    tm = h_ref.shape[0]
    xh = _rms(jnp.concatenate([hp_ref[...], hn_ref[...]], axis=0), g).astype(BF16)
    u_all = _dot(jnp.concatenate([xn, xh], axis=0), w_ref[:, r:2 * r])
    u = u_all[0:tm]
    uh = u_all[tm:tm + 2 * SUBLANES]
    at_start = (i % steps_per_seq) == 0
    at_end = (i % steps_per_seq) == steps_per_seq - 1
    prev2 = jnp.where(at_start, 0.0, uh[SUBLANES - 2:SUBLANES, :])
    next1 = jnp.where(at_end, 0.0, uh[SUBLANES:SUBLANES + 1, :])
    row = lax.broadcasted_iota(I32, (SUBLANES, 1), 0)
    for k in range(tiles):
        p2 = prev2 if k == 0 else u[k * tt - 2:k * tt]
        n1 = next1 if k == tiles - 1 else u[(k + 1) * tt:(k + 1) * tt + 1]
        pad = jnp.where(row == 0, p2[0:1], jnp.where(row == 1, p2[1:2], jnp.where(row == 2, n1, 0.0)))
        for s in range(r // LANES):
            lanes = slice(s * LANES, (s + 1) * LANES)
            u_ref[k, s, 0:tt, :] = u[k * tt:(k + 1) * tt, lanes]
            u_ref[k, s, tt:tt + SUBLANES, :] = pad[:, lanes]


def _mix_b_in(h, g, w_in, batch, seq, tt):
    n, d = h.shape
    r = w_in.shape[1] // 2
    tm = _tile(seq, 1024)
    steps_per_seq = seq // tm
    tiles = tm // tt
    hb = tm // SUBLANES
    nhalo = n // SUBLANES
    return pl.pallas_call(
        functools.partial(_mix_b_in_kernel, steps_per_seq=steps_per_seq, tt=tt),
        grid=(n // tm,),
        in_specs=[pl.BlockSpec((tm, d), lambda i: (i, 0)),
                  pl.BlockSpec((SUBLANES, d), lambda i: (jnp.maximum(i * hb - 1, 0), 0)),
                  pl.BlockSpec((SUBLANES, d), lambda i: (jnp.minimum((i + 1) * hb, nhalo - 1), 0)),
                  pl.BlockSpec((1, d), lambda i: (0, 0)),
                  pl.BlockSpec((d, 2 * r), lambda i: (0, 0), pipeline_mode=pl.Buffered(1))],
        out_specs=[pl.BlockSpec((tm, r), lambda i: (i, 0)),
                   pl.BlockSpec((tiles, r // LANES, None, tt + SUBLANES, LANES),
                                lambda i: (i % steps_per_seq, 0, i // steps_per_seq, 0, 0))],
        out_shape=[jax.ShapeDtypeStruct((n, r), BF16),
                   jax.ShapeDtypeStruct((seq // tt, r // LANES, batch, tt + SUBLANES, LANES), F32)],
        compiler_params=_cparams("parallel"),
        name="mix_b_in",
    )(h, h, h, g, w_in)


def _rglru_gates(u, wr_ref, br_ref, wi_ref, bi_ref, lam_ref, a_s, g_s):
    ub = u.astype(BF16)
    t_r = jnp.tanh(_dot(ub, wr_ref[...]) + 0.5 * br_ref[...])
    t_i = jnp.tanh(_dot(ub, wi_ref[...]) + 0.5 * bi_ref[...])
    c = (-0.25 * RG_C) * jax.nn.softplus(-lam_ref[...])
    th = jnp.tanh(c + c * t_r)
    q = 1.0 / (1.0 - th)
    a_s[...] = (1.0 + th) * q
    nth = -th
    root = jnp.where(nth > 0.0, nth * lax.rsqrt(nth), 0.0)
    g_s[...] = (q * root) * (u + u * t_i)


def _rglru_fwd_kernel(u_ref, cw_ref, cb_ref, wr_ref, br_ref, wi_ref, bi_ref, lam_ref,
                      hf_ref, uc_ref, x_s, a_s, g_s, carry_s, *, tt):
    nb = SUBLANES
    ttp = tt + SUBLANES
    rows = tt * nb
    slabs = u_ref.shape[0]

    @pl.when(pl.program_id(1) == 0)
    def _():
        carry_s[...] = jnp.zeros_like(carry_s)

    def gather_t(tl, dst):
        for s in range(slabs):
            x_s[pl.ds(dst, nb), s * LANES:(s + 1) * LANES] = u_ref[s, pl.ds(tl, nb, stride=ttp), :]

    def load_t(tl, c):
        gather_t(tl, pl.multiple_of((tl + 2) * nb, nb))
        return c

    lax.fori_loop(0, tt, load_t, 0, unroll=8)
    gather_t(tt, 0)
    gather_t(tt + 1, nb)
    gather_t(tt + 2, (tt + 2) * nb)

    cw = cw_ref[...]
    u = (cw[0:1, :] * x_s[0:rows, :] + cw[1:2, :] * x_s[nb:nb + rows, :]
         + cw[2:3, :] * x_s[2 * nb:2 * nb + rows, :] + cw[3:4, :] * x_s[3 * nb:3 * nb + rows, :]
         + cb_ref[...])
    uc_ref[...] = u
    _rglru_gates(u, wr_ref, br_ref, wi_ref, bi_ref, lam_ref, a_s, g_s)

    def step(tl, h):
        base = pl.multiple_of(tl * nb, nb)
        h = a_s[pl.ds(base, nb), :] * h + g_s[pl.ds(base, nb), :]
        hf_ref[pl.ds(base, nb), :] = h
        return h

    carry_s[...] = lax.fori_loop(0, tt, step, carry_s[...], unroll=8)


def _rglru_rev_kernel(uc_ref, wr_ref, br_ref, wi_ref, bi_ref, lam_ref, hf_ref, yb_ref,
                      z_ref, a_s, g_s, carry_s, nat_s, *, tt):
    nb = SUBLANES
    ttp = tt + SUBLANES
    slabs = nat_s.shape[0]

    @pl.when(pl.program_id(1) == 0)
    def _():
        carry_s[...] = jnp.zeros_like(carry_s)

    _rglru_gates(uc_ref[...], wr_ref, br_ref, wi_ref, bi_ref, lam_ref, a_s, g_s)

    def step(k, h):
        tl = tt - 1 - k
        base = pl.multiple_of(tl * nb, nb)
        h = a_s[pl.ds(base, nb), :] * h + g_s[pl.ds(base, nb), :]
        hs = h + hf_ref[pl.ds(base, nb), :]
        for s in range(slabs):
            nat_s[s, pl.ds(tl, nb, stride=ttp), :] = hs[:, s * LANES:(s + 1) * LANES]
        return h

    carry_s[...] = lax.fori_loop(0, tt, step, carry_s[...], unroll=8)
    for b in range(nb):
        for s in range(slabs):
            lanes = slice(s * LANES, (s + 1) * LANES)
            hb = nat_s[s, b * ttp:b * ttp + tt, :]
            z_ref[b, :, lanes] = (yb_ref[b, :, lanes].astype(F32) * hb).astype(BF16)


def _rglru(u_tiles, yb, conv_w, conv_b, w_r, b_r, w_i, b_i, lam, seq, tt):
    nt, nslab, batch, ttp, _ = u_tiles.shape
    r = nslab * LANES
    assert batch == SUBLANES, "time-major tile layout puts the batch on the 8 sublanes"
    nh, bk = w_r.shape[1], w_r.shape[2]
    slabs = bk // LANES
    kw = conv_w.shape[0]
    rows = tt * batch
    gates = [w_r, b_r.reshape(2, 1, r), w_i, b_i.reshape(2, 1, r), lam.reshape(2, 1, r)]

    def gate_specs(d):
        vec = pl.BlockSpec((None, 1, bk), lambda h, jj: (d, 0, h))
        mat = pl.BlockSpec((None, None, bk, bk), lambda h, jj: (d, h, 0, 0))
        return [mat, vec, mat, vec, vec]

    tm_spec = pl.BlockSpec((rows, bk), lambda h, jj: (jj, h))
    tm_rev = pl.BlockSpec((rows, bk), lambda h, jj: (nt - 1 - jj, h))
    tm_shape = jax.ShapeDtypeStruct((seq * batch, r), F32)
    state = [pltpu.VMEM((rows, bk), F32), pltpu.VMEM((rows, bk), F32), pltpu.VMEM((batch, bk), F32)]

    hf, uc = pl.pallas_call(
        functools.partial(_rglru_fwd_kernel, tt=tt),
        grid=(nh, nt),
        in_specs=[pl.BlockSpec((None, slabs, batch * ttp, LANES), lambda h, jj: (jj, h, 0, 0)),
                  pl.BlockSpec((kw, bk), lambda h, jj: (0, h)),
                  pl.BlockSpec((1, bk), lambda h, jj: (0, h))] + gate_specs(0),
        out_specs=[tm_spec, tm_spec],
        out_shape=[tm_shape, tm_shape],
        scratch_shapes=[pltpu.VMEM(((tt + 3) * batch, bk), F32)] + state,
        compiler_params=_cparams("parallel", "arbitrary"),
        name="rglru_fwd",
    )(u_tiles.reshape(nt, nslab, batch * ttp, LANES), conv_w, conv_b, *gates)

    nat_spec = pl.BlockSpec((batch, tt, bk), lambda h, jj: (0, nt - 1 - jj, h))
    z = pl.pallas_call(
        functools.partial(_rglru_rev_kernel, tt=tt),
        grid=(nh, nt),
        in_specs=[tm_rev] + gate_specs(1) + [tm_rev, nat_spec],
        out_specs=nat_spec,
        out_shape=jax.ShapeDtypeStruct((batch, seq, r), BF16),
        scratch_shapes=state + [pltpu.VMEM((slabs, batch * ttp, LANES), F32)],
        compiler_params=_cparams("parallel", "arbitrary"),
        name="rglru_rev",
    )(uc, *gates, hf, yb.reshape(batch, seq, r))
    return z.reshape(-1, r)


def _route(logits):
    lane = lax.broadcasted_iota(I32, logits.shape, 1).astype(F32)
    neg = jnp.float32(-jnp.inf)
    nolane = jnp.float32(LANES)
    gmask = lane < N_GROUPS
    gmax = jnp.max(jnp.where(gmask, logits, neg), axis=1, keepdims=True)
    eg = jnp.where(gmask, jnp.exp(logits - gmax), 0.0)
    gsum = jnp.sum(eg, axis=1, keepdims=True)
    pg = eg / gsum
    pg_top = 1.0 / gsum
    g_top = jnp.min(jnp.where(gmask & (pg == pg_top), lane, nolane), axis=1, keepdims=True)
    lo = EXPERT_LANE0 + g_top * EXPERTS_PER_GROUP
    emask = (lane >= lo) & (lane < lo + EXPERTS_PER_GROUP)
    emax = jnp.max(jnp.where(emask, logits, neg), axis=1, keepdims=True)
    ee = jnp.where(emask, jnp.exp(logits - emax), 0.0)
    esum = jnp.sum(ee, axis=1, keepdims=True)
    pe = jnp.where(emask, ee / esum, -1.0)
    p1 = 1.0 / esum
    i1 = jnp.min(jnp.where(pe == p1, lane, nolane), axis=1, keepdims=True)
    pe2 = jnp.where(lane == i1, -1.0, pe)
    p2 = jnp.max(pe2, axis=1, keepdims=True)
    i2 = jnp.min(jnp.where(pe2 == p2, lane, nolane), axis=1, keepdims=True)
    psum = p1 + p2
    return i1, i2, pg_top * (p1 / psum), pg_top * (p2 / psum)


def _post_mix_kernel(z_ref, wo_ref, h_ref, g_ref, wr_ref, br_ref,
                     h1_ref, xn_ref, meta_ref, meta_t_ref, cnt_ref, carry_ref, logits_s):
    i = pl.program_id(0)
    tm = z_ref.shape[0]

    @pl.when(i == 0)
    def _():
        logits_s[...] = jnp.zeros_like(logits_s)

    @pl.when(i <= 1)
    def _():
        carry_ref[...] = jnp.zeros_like(carry_ref)

    logits = logits_s[...]

    h1 = h_ref[...] + _dot(z_ref[...], wo_ref[...])
    h1_ref[...] = h1
    xn = _rms(h1, g_ref[...])
    xn_ref[...] = _pack_bf16_pairs(xn)
    logits_s[...] = _dot(xn.astype(BF16), wr_ref[...].astype(BF16)) + br_ref[...]

    i1, i2, w0, w1 = _route(logits)

    lane = lax.broadcasted_iota(I32, (tm, LANES), 1).astype(F32)
    oh0 = lane == i1
    oh1 = lane == i2
    both = (oh0 | oh1).astype(BF16)
    ri = lax.broadcasted_iota(I32, (tm, tm), 0)
    ci = lax.broadcasted_iota(I32, (tm, tm), 1)
    before = (ri > ci).astype(BF16)
    cnt_before = _dot(before, both) + carry_ref[...]
    rank0 = jnp.sum(jnp.where(oh0, cnt_before, 0.0), axis=1, keepdims=True)
    rank1 = jnp.sum(jnp.where(oh1, cnt_before, 0.0), axis=1, keepdims=True)
    carry = carry_ref[...] + jnp.sum(both.astype(F32), axis=0, keepdims=True)
    carry_ref[...] = carry
    cnt_ref[...] = carry

    e0 = i1 - EXPERT_LANE0
    e1 = i2 - EXPERT_LANE0
    meta = jnp.zeros((tm, LANES), F32)
    for ln, val in ((M_E0, e0), (M_E1, e1), (M_R0, rank0), (M_R1, rank1), (M_W0, w0), (M_W1, w1)):
        meta = jnp.where(lane == ln, val, meta)
    meta_ref[...] = meta
    meta_t_ref[...] = meta.T[0:SUBLANES, :]


def _post_mix(z, w_out, h, g, w_router, b_router):
    n, d = h.shape
    k = z.shape[1]
    tm = _tile(n, 512)
    nt = n // tm

    def proj(i):
        return (jnp.minimum(i, nt - 1), 0)

    def routed(i):
        return (jnp.maximum(i - 1, 0), 0)

    return pl.pallas_call(
        _post_mix_kernel,
        grid=(nt + 1,),
        in_specs=[pl.BlockSpec((tm, k), proj),
                  pl.BlockSpec((k, d), lambda i: (0, 0)),
                  pl.BlockSpec((tm, d), proj),
                  pl.BlockSpec((1, d), lambda i: (0, 0)),
                  pl.BlockSpec((d, LANES), lambda i: (0, 0)),
                  pl.BlockSpec((1, LANES), lambda i: (0, 0))],
        out_specs=[pl.BlockSpec((tm, d), proj),
                   pl.BlockSpec((tm, d // 2), proj),
                   pl.BlockSpec((tm, LANES), routed),
                   pl.BlockSpec((SUBLANES, tm), lambda i: (0, jnp.maximum(i - 1, 0))),
                   pl.BlockSpec((1, LANES), lambda i: (0, 0))],
        out_shape=[jax.ShapeDtypeStruct((n, d), F32), jax.ShapeDtypeStruct((n, d // 2), U32),
                   jax.ShapeDtypeStruct((n, LANES), F32), jax.ShapeDtypeStruct((SUBLANES, n), F32),
                   jax.ShapeDtypeStruct((1, LANES), F32)],
        scratch_shapes=[pltpu.VMEM((1, LANES), F32), pltpu.VMEM((tm, LANES), F32)],
        compiler_params=_cparams("arbitrary"),
        name="post_mix_router",
    )(z, w_out, h, g, w_router, b_router)


def _plan_kernel(cnt_ref, meta_t_ref, dest_ref, estart_ref, ecount_ref, nblk_ref, pstart_ref):
    def per_expert(e, acc):
        nb_e = (cnt_ref[e] + (ROUTE_ROWS - 1)) // ROUTE_ROWS
        pstart_ref[e] = acc * ROUTE_ROWS
        estart_ref[e] = acc
        ecount_ref[e] = nb_e
        return acc + nb_e

    nblk_ref[0] = lax.fori_loop(0, N_EXPERTS, per_expert, 0)

    e0 = meta_t_ref[M_E0:M_E0 + 1, :]
    e1 = meta_t_ref[M_E1:M_E1 + 1, :]
    d0 = meta_t_ref[M_R0:M_R0 + 1, :]
    d1 = meta_t_ref[M_R1:M_R1 + 1, :]
    for e in range(N_EXPERTS):
        ps = pstart_ref[e].astype(F32)
        d0 = d0 + jnp.where(e0 == e, ps, 0.0)
        d1 = d1 + jnp.where(e1 == e, ps, 0.0)
    dest_ref[...] = jnp.zeros_like(dest_ref)
    dest_ref[0:1, :] = d0.astype(I32)
    dest_ref[1:2, :] = d1.astype(I32)


def _plan(cnt_i32, meta_t):
    n = meta_t.shape[1]
    smem = pl.BlockSpec(memory_space=pltpu.SMEM)
    return pl.pallas_call(
        _plan_kernel,
        in_specs=[smem, pl.BlockSpec((SUBLANES, n), lambda: (0, 0))],
        out_specs=[pl.BlockSpec((SUBLANES, n), lambda: (0, 0)), smem, smem, smem],
        out_shape=[jax.ShapeDtypeStruct((SUBLANES, n), I32),
                   jax.ShapeDtypeStruct((N_EXPERTS,), I32),
                   jax.ShapeDtypeStruct((N_EXPERTS,), I32),
                   jax.ShapeDtypeStruct((1,), I32)],
        scratch_shapes=[pltpu.SMEM((N_EXPERTS,), I32)],
        name="dispatch_plan",
    )(cnt_i32, meta_t)


def _sc_mesh():
    return plsc.VectorSubcoreMesh(core_axis_name="c", subcore_axis_name="s",
                                  num_cores=SC_CORES, num_subcores=SC_SUBCORES)


def _sc_scatter_rows(x, idx0, idx1, p_rows):
    m, k = x.shape
    per_w = m // (SC_CORES * SC_SUBCORES)
    assert per_w % SC_CHUNK == 0 and per_w * SC_CORES * SC_SUBCORES == m

    nchunk = per_w // SC_CHUNK
    buf = [pltpu.VMEM((SC_CHUNK,), I32), pltpu.VMEM((SC_CHUNK,), I32),
           pltpu.VMEM((SC_CHUNK, k), x.dtype),
           pltpu.SemaphoreType.DMA, pltpu.SemaphoreType.DMA, pltpu.SemaphoreType.DMA]

    @functools.partial(
        pl.kernel, mesh=_sc_mesh(), out_type=jax.ShapeDtypeStruct((p_rows, k), x.dtype),
        scratch_types=buf + buf, name="sc_scatter_rows")
    def scatter(x_hbm, i0_hbm, i1_hbm, out_hbm, *scratch):
        base = (lax.axis_index("s") * SC_CORES + lax.axis_index("c")) * per_w
        slots = (scratch[:6], scratch[6:])

        def loads(c):
            i0_v, i1_v, rows_v, lsem, _, _ = slots[c % 2]
            off = pl.multiple_of(base + c * SC_CHUNK, SC_CHUNK)
            return [pltpu.make_async_copy(i0_hbm.at[pl.ds(off, SC_CHUNK)], i0_v, lsem),
                    pltpu.make_async_copy(i1_hbm.at[pl.ds(off, SC_CHUNK)], i1_v, lsem),
                    pltpu.make_async_copy(x_hbm.at[pl.ds(off, SC_CHUNK)], rows_v, lsem)]

        def puts(c):
            i0_v, i1_v, rows_v, _, psem0, psem1 = slots[c % 2]
            return [pltpu.make_async_copy(rows_v, out_hbm.at[i0_v], psem0),
                    pltpu.make_async_copy(rows_v, out_hbm.at[i1_v], psem1)]

        for d in loads(0):
            d.start()
        for c in range(nchunk):
            for d in loads(c):
                d.wait()
            for d in puts(c):
                d.start()
            if c >= 1:
                for d in puts(c - 1):
                    d.wait()
            if c + 1 < nchunk:
                for d in loads(c + 1):
                    d.start()
        for d in puts(nchunk - 1):
            d.wait()

    return scatter(x, idx0, idx1)


def _expert_kernel(es_ref, ec_ref, nblk_ref, cnt_ref, xs_ref, wg_ref, wu_ref, wd_ref, ys_ref,
                   x_s, y_s, wg_s, wu_s, wd_s, sem):
    e = pl.program_id(0)
    nb = ec_ref[e]
    b0 = es_ref[e]
    k = x_s.shape[2]
    nb_max = ys_ref.shape[0] // ROUTE_ROWS

    def rows(b):
        return pl.ds(pl.multiple_of(b * ROUTE_ROWS, ROUTE_ROWS), ROUTE_ROWS)

    nblk = nblk_ref[0]

    def copy_in(b):
        slot = lax.rem(b, IN_SLOTS)
        return pltpu.make_async_copy(xs_ref.at[rows(b)], x_s.at[slot], sem.at[0, slot])

    def copy_out(b):
        slot = lax.rem(b, OUT_SLOTS)
        return pltpu.make_async_copy(y_s.at[slot], ys_ref.at[rows(b)], sem.at[1, slot])

    @pl.when(e == 0)
    def _():
        for b in range(IN_AHEAD):
            @pl.when(b < nblk)
            def _(b=b):
                copy_in(b).start()

    @pl.when(nb > 0)
    def _():
        wg_s[...] = wg_ref[...].astype(BF16)
        wu_s[...] = wu_ref[...].astype(BF16)
        wd_s[...] = wd_ref[...].astype(BF16)

        def mlp(b, j, width):
            row = lax.broadcasted_iota(I32, (ROUTE_ROWS, 1), 0)
            xp = jnp.concatenate(
                [jnp.where(row < cnt_ref[e] - (j + i) * ROUTE_ROWS,
                           x_s[lax.rem(b + i, IN_SLOTS)], jnp.uint32(0)) for i in range(width)], axis=0)
            x_hi, x_lo = _unpack_bf16_pairs(xp)
            x_hi = x_hi.astype(BF16)
            x_lo = x_lo.astype(BF16)
            g = _dot(x_hi, wg_s[0:k, :]) + _dot(x_lo, wg_s[k:2 * k, :])
            u = _dot(x_hi, wu_s[0:k, :]) + _dot(x_lo, wu_s[k:2 * k, :])
            hb = (jax.nn.silu(g) * u).astype(BF16)
            y = _pack_bf16_pairs(_dot(hb, wd_s[...]))
            return [y[i * ROUTE_ROWS:(i + 1) * ROUTE_ROWS] for i in range(width)]

        def unit(j, width):
            b = b0 + j
            for i in range(width):
                copy_in(b + i).wait()
            for i in range(width):
                @pl.when(b + i + IN_AHEAD < nblk)
                def _(i=i):
                    copy_in(b + i + IN_AHEAD).start()
            ys = mlp(b, j, width)
            for i in range(width):
                @pl.when(b + i >= OUT_SLOTS)
                def _(i=i):
                    copy_out(b + i - OUT_SLOTS).wait()
            for i in range(width):
                y_s[lax.rem(b + i, OUT_SLOTS)] = ys[i]
                copy_out(b + i).start()

        def pair(jj, c):
            unit(2 * jj, 2)
            return c

        lax.fori_loop(0, nb // 2, pair, 0)

        @pl.when(nb % 2 == 1)
        def _():
            unit(nb - 1, 1)

    @pl.when(e == pl.num_programs(0) - 1)
    def _():
        for back in range(OUT_SLOTS, 0, -1):
            @pl.when(nblk >= back)
            def _(back=back):
                copy_out(nblk - back).wait()

    @pl.when(e == pl.num_programs(0) - 1)
    def _():
        y_s[0] = jnp.zeros_like(y_s[0])

        def zero(b):
            return pltpu.make_async_copy(y_s.at[0], ys_ref.at[rows(b)], sem.at[1, 0])

        def start_zero(b, c):
            zero(b).start()
            return c

        def wait_zero(b, c):
            zero(b).wait()
            return c

        lax.fori_loop(nblk_ref[0], nb_max, start_zero, 0)
        lax.fori_loop(nblk_ref[0], nb_max, wait_zero, 0)


def _experts(estart, ecount, nblk, cnt, xs, w_gate, w_up, w_down, layer):
    p, k = xs.shape
    d, f = w_gate.shape[2], w_gate.shape[3]

    def w_map(e, es, ec, nb, cn):
        return (layer, e, 0, 0)

    grid_spec = pltpu.PrefetchScalarGridSpec(
        num_scalar_prefetch=4,
        grid=(N_EXPERTS,),
        in_specs=[pl.BlockSpec(memory_space=pl.ANY),
                  pl.BlockSpec((None, None, d, f), w_map),
                  pl.BlockSpec((None, None, d, f), w_map),
                  pl.BlockSpec((None, None, f, d), w_map)],
        out_specs=pl.BlockSpec(memory_space=pl.ANY),
        scratch_shapes=[pltpu.VMEM((IN_SLOTS, ROUTE_ROWS, k), U32),
                        pltpu.VMEM((OUT_SLOTS, ROUTE_ROWS, k), U32),
                        pltpu.VMEM((d, f), BF16), pltpu.VMEM((d, f), BF16), pltpu.VMEM((f, d), BF16),
                        pltpu.SemaphoreType.DMA((2, max(IN_SLOTS, OUT_SLOTS)))],
    )
    return pl.pallas_call(
        _expert_kernel,
        grid_spec=grid_spec,
        out_shape=jax.ShapeDtypeStruct((p, k), U32),
        compiler_params=_cparams("arbitrary"),
        name="expert_mlp",
    )(estart, ecount, nblk, cnt, xs, w_gate, w_up, w_down)


def _sc_gather_pair(table, idx0, idx1, start, count):
    k = table.shape[1]
    per_w = count // (SC_CORES * SC_SUBCORES)
    assert per_w % SC_CHUNK == 0 and per_w * SC_CORES * SC_SUBCORES == count
    assert start % SC_CHUNK == 0
    rows = jax.ShapeDtypeStruct((count, k), table.dtype)

    @functools.partial(
        pl.kernel, mesh=_sc_mesh(), out_type=(rows, rows),
        scratch_types=[pltpu.VMEM((SC_CHUNK,), I32), pltpu.VMEM((SC_CHUNK,), I32),
                       pltpu.VMEM((SC_CHUNK, k), table.dtype), pltpu.VMEM((SC_CHUNK, k), table.dtype),
                       pltpu.SemaphoreType.DMA, pltpu.SemaphoreType.DMA,
                       pltpu.SemaphoreType.DMA, pltpu.SemaphoreType.DMA],
        name="sc_gather_pair")
    def gather(table_hbm, i0_hbm, i1_hbm, o0_hbm, o1_hbm, i0_v, i1_v, r0_v, r1_v,
               sem0, sem1, sem2, sem3):
        base = (lax.axis_index("s") * SC_CORES + lax.axis_index("c")) * per_w

        @pl.loop(0, per_w // SC_CHUNK)
        def _(c):
            dst = pl.multiple_of(base + c * SC_CHUNK, SC_CHUNK)
            src = pl.multiple_of(start + dst, SC_CHUNK)
            pltpu.sync_copy(i0_hbm.at[pl.ds(src, SC_CHUNK)], i0_v)
            pltpu.sync_copy(i1_hbm.at[pl.ds(src, SC_CHUNK)], i1_v)
            get0 = pltpu.async_copy(table_hbm.at[i0_v], r0_v, sem0)
            get1 = pltpu.async_copy(table_hbm.at[i1_v], r1_v, sem1)
            get0.wait()
            put0 = pltpu.async_copy(r0_v, o0_hbm.at[pl.ds(dst, SC_CHUNK)], sem2)
            get1.wait()
            put1 = pltpu.async_copy(r1_v, o1_hbm.at[pl.ds(dst, SC_CHUNK)], sem3)
            put0.wait()
            put1.wait()

    return gather(table, idx0, idx1)


def _combine_kernel(*refs, final):
    (ya_ref, yb_ref, h1_ref, meta_ref, p_ref, gp_ref, wpg_ref, bpg_ref, wp_ref,
     gf_ref) = refs[:10]
    out_ref = refs[-1]
    meta = meta_ref[...]
    w0 = meta[:, M_W0:M_W0 + 1]
    w1 = meta[:, M_W1:M_W1 + 1]
    ya_hi, ya_lo = _unpack_bf16_pairs(ya_ref[...])
    yb_hi, yb_lo = _unpack_bf16_pairs(yb_ref[...])
    y = jnp.concatenate([ya_hi * w0 + yb_hi * w1, ya_lo * w0 + yb_lo * w1], axis=1)
    h2 = h1_ref[...] + y
    xn = _rms(h2, gp_ref[...]).astype(BF16)
    gate = jax.nn.sigmoid(_dot(xn, wpg_ref[...]) + bpg_ref[...])
    pp = _dot(p_ref[...].astype(BF16), wp_ref[...])
    h3 = h2 + gate * pp
    if final:
        h3 = _rms(h3, gf_ref[...])
    out_ref[...] = h3


def _combine(dest, ys, h1, meta, p, layer, g_ple, w_ple_gate, b_ple_gate, w_ple, g_final, final):
    n, d = h1.shape
    pd = p.shape[2]
    k = ys.shape[1]
    n_part = n // COMBINE_PARTS
    tmc = _tile(n_part, 512)
    tiles = n_part // tmc
    vec = pl.BlockSpec((1, d), lambda i: (0, 0))
    out = None
    for q in range(COMBINE_PARTS):
        ya, yb = _sc_gather_pair(ys, dest[0], dest[1], q * n_part, n_part)

        def tok(i, q=q):
            return (q * tiles + i, 0)

        in_specs = [pl.BlockSpec((tmc, k), lambda i: (i, 0)),
                    pl.BlockSpec((tmc, k), lambda i: (i, 0)),
                    pl.BlockSpec((tmc, d), tok),
                    pl.BlockSpec((tmc, LANES), tok),
                    pl.BlockSpec((None, tmc, pd), lambda i, q=q: (layer, q * tiles + i, 0)),
                    vec,
                    pl.BlockSpec((d, d), lambda i: (0, 0)),
                    vec,
                    pl.BlockSpec((pd, d), lambda i: (0, 0)),
                    vec]
        args = [ya, yb, h1, meta, p, g_ple, w_ple_gate, b_ple_gate, w_ple, g_final]
        aliases = {}
        if out is not None:
            in_specs.append(pl.BlockSpec(memory_space=pl.ANY))
            args.append(out)
            aliases = {len(args) - 1: 0}
        out = pl.pallas_call(
            functools.partial(_combine_kernel, final=final),
            grid=(tiles,),
            in_specs=in_specs,
            out_specs=pl.BlockSpec((tmc, d), tok),
            out_shape=jax.ShapeDtypeStruct((n, d), F32),
            input_output_aliases=aliases,
            compiler_params=_cparams("parallel"),
            name="combine_ple",
        )(*args)
    return out


def _row(v):
    return v.reshape(1, -1)


def _moe_and_ple(z, w_out, h, i, final, norm_ffn, w_router_group, b_router_group, w_router_expert,
                 b_router_expert, w_exp_gate, w_exp_up, w_exp_down, norm_ple, w_ple, w_ple_gate,
                 b_ple_gate, norm_final, p):
    n, d = h.shape
    pad = LANES - N_GROUPS - N_EXPERTS
    w_router = jnp.concatenate(
        [w_router_group[i], w_router_expert[i], jnp.zeros((d, pad), F32)], axis=1)
    b_router = jnp.concatenate(
        [b_router_group[i], b_router_expert[i], jnp.zeros((pad,), F32)]).reshape(1, LANES)
    h1, xn, meta, meta_t, cnt = _post_mix(z, w_out, h, _row(norm_ffn[i]), w_router, b_router)
    cnt_i32 = cnt[0, EXPERT_LANE0:EXPERT_LANE0 + N_EXPERTS].astype(I32)
    nb_max = (2 * n) // ROUTE_ROWS + N_EXPERTS
    dest, estart, ecount, nblk = _plan(cnt_i32, meta_t)
    xs = _sc_scatter_rows(xn, dest[0], dest[1], nb_max * ROUTE_ROWS)
    ys = _experts(estart, ecount, nblk, cnt_i32, xs, w_exp_gate, w_exp_up, w_exp_down, i)
    return _combine(dest, ys, h1, meta, p.reshape(p.shape[0], n, -1), i, _row(norm_ple[i]),
                    w_ple_gate[i].astype(BF16), _row(b_ple_gate[i]), w_ple[i].astype(BF16),
                    _row(norm_final), final)


def kernel(x, p, norm_mix, w_in_a, conv_a, w_out_a, w_in_b, conv_b, conv_bias_b, w_rgate_b, b_rgate_b, w_igate_b, b_igate_b, lam_b, w_out_b, norm_ffn, w_router_group, b_router_group, w_router_expert, b_router_expert, w_exp_gate, w_exp_up, w_exp_down, norm_ple, w_ple, w_ple_gate, b_ple_gate, norm_final):
    batch, seq, d = x.shape
    depth = p.shape[0]
    n = batch * seq
    h = x.reshape(n, d)
    for i in range(depth):
        j = i // 2
        if i % 2 == 0:
            bg, v = _mix_a_in(h, _row(norm_mix[i]), w_in_a[j].astype(BF16))
            z = _mix_a_conv(v, bg, conv_a[j], seq)
            w_out = w_out_a[j]
        else:
            tt = _tile(seq, 256)
            yb, u_tiles = _mix_b_in(h, _row(norm_mix[i]), w_in_b[j].astype(BF16), batch, seq, tt)
            z = _rglru(u_tiles, yb, conv_b[j], _row(conv_bias_b[j]),
                       (0.5 * w_rgate_b[j]).astype(BF16), b_rgate_b[j],
                       (0.5 * w_igate_b[j]).astype(BF16),
                       b_igate_b[j], lam_b[j], seq, tt)
            w_out = w_out_b[j]
        h = _moe_and_ple(z, w_out.astype(BF16), h, i, i == depth - 1, norm_ffn, w_router_group,
                         b_router_group, w_router_expert, b_router_expert, w_exp_gate, w_exp_up,
                         w_exp_down, norm_ple, w_ple, w_ple_gate, b_ple_gate, norm_final, p)
    return h.reshape(batch, seq, d)
```

```python
import functools

import jax
import jax.numpy as jnp
from jax import lax
from jax.experimental import pallas as pl
from jax.experimental.pallas import tpu as pltpu
from jax.experimental.pallas import tpu_sc as plsc

F32 = jnp.float32
BF16 = jnp.bfloat16
I32 = jnp.int32
U32 = jnp.uint32

EPS = 1e-6
N_GROUPS = 4
EXPERTS_PER_GROUP = 8
N_EXPERTS = N_GROUPS * EXPERTS_PER_GROUP
RG_C = 8.0

LANES = 128
SUBLANES = 8
BF16_ROWS = 16
EXPERT_LANE0 = N_GROUPS
ROUTE_ROWS = 256
IN_AHEAD = 4
EXPERT_UNIT = 4
IN_SLOTS = IN_AHEAD + EXPERT_UNIT
OUT_SLOTS = 4
SC_CORES = 2
SC_SUBCORES = 16
SC_CHUNK = 64
COMBINE_PARTS = 4
M_E0, M_E1, M_R0, M_R1, M_W0, M_W1 = 0, 1, 2, 3, 4, 5
VMEM_LIMIT = 48 * 1024 * 1024


def _cparams(*sem):
    return pltpu.CompilerParams(dimension_semantics=sem, vmem_limit_bytes=VMEM_LIMIT)


def _rms(x, g):
    return x * lax.rsqrt(jnp.mean(x * x, axis=-1, keepdims=True) + EPS) * g


def _dot(a, b):
    return jnp.dot(a, b, preferred_element_type=F32)


def _pack_bf16_pairs(x):
    k = x.shape[1] // 2
    hi = lax.bitcast_convert_type(x[:, :k].astype(BF16).astype(F32), U32)
    lo = lax.bitcast_convert_type(x[:, k:].astype(BF16).astype(F32), U32)
    return hi | (lo >> 16)


def _unpack_bf16_pairs(p):
    hi = lax.bitcast_convert_type(p & jnp.uint32(0xFFFF0000), F32)
    lo = lax.bitcast_convert_type(p << 16, F32)
    return hi, lo


def _tile(n, want):
    t = min(n, want)
    assert n % t == 0, (n, want)
    return t


def _mix_a_in_kernel(h_ref, g_ref, w_ref, bg_ref, v_ref):
    d = h_ref.shape[1]
    xn = _rms(h_ref[...], g_ref[...]).astype(BF16)
    bg = _dot(xn, w_ref[:, 0:d])
    cg = _dot(xn, w_ref[:, d:2 * d])
    hh = _dot(xn, w_ref[:, 2 * d:3 * d])
    bg_ref[...] = bg.astype(BF16)
    v_ref[...] = (cg * hh).astype(BF16)


def _mix_a_in(h, g, w_in):
    n, d = h.shape
    tm = _tile(n, 1024)
    return pl.pallas_call(
        _mix_a_in_kernel,
        grid=(n // tm,),
        in_specs=[pl.BlockSpec((tm, d), lambda i: (i, 0)),
                  pl.BlockSpec((1, d), lambda i: (0, 0)),
                  pl.BlockSpec((d, 3 * d), lambda i: (0, 0), pipeline_mode=pl.Buffered(1))],
        out_specs=[pl.BlockSpec((tm, d), lambda i: (i, 0)),
                   pl.BlockSpec((tm, d), lambda i: (i, 0))],
        out_shape=[jax.ShapeDtypeStruct((n, d), BF16), jax.ShapeDtypeStruct((n, d), BF16)],
        compiler_params=_cparams("parallel"),
        name="mix_a_in",
    )(h, g, w_in)


def _mix_a_conv_kernel(v_ref, vp_ref, vn_ref, bg_ref, cw_ref, z_ref, *, tiles_per_seq):
    i = pl.program_id(0)
    tm = v_ref.shape[0]
    v = v_ref[...].astype(F32)
    at_start = (i % tiles_per_seq) == 0
    at_end = (i % tiles_per_seq) == tiles_per_seq - 1
    prev_row = vp_ref[...].astype(F32)[BF16_ROWS - 1:BF16_ROWS, :]
    next_row = vn_ref[...].astype(F32)[0:1, :]
    prev_row = jnp.where(at_start, 0.0, prev_row)
    next_row = jnp.where(at_end, 0.0, next_row)
    row = lax.broadcasted_iota(I32, (tm, 1), 0)
    v_dn = jnp.where(row == 0, prev_row, pltpu.roll(v, 1, 0))
    v_up = jnp.where(row == tm - 1, next_row, pltpu.roll(v, tm - 1, 0))
    cw = cw_ref[...]
    u = cw[0:1, :] * v_dn + cw[1:2, :] * v + cw[2:3, :] * v_up
    z_ref[...] = (bg_ref[...].astype(F32) * u).astype(BF16)


def _mix_a_conv(v, bg, conv_w, seq):
    n, d = v.shape
    tm = _tile(seq, 512)
    hb = tm // BF16_ROWS
    nhalo = n // BF16_ROWS
    return pl.pallas_call(
        functools.partial(_mix_a_conv_kernel, tiles_per_seq=seq // tm),
        grid=(n // tm,),
        in_specs=[pl.BlockSpec((tm, d), lambda i: (i, 0)),
                  pl.BlockSpec((BF16_ROWS, d), lambda i: (jnp.maximum(i * hb - 1, 0), 0)),
                  pl.BlockSpec((BF16_ROWS, d), lambda i: (jnp.minimum((i + 1) * hb, nhalo - 1), 0)),
                  pl.BlockSpec((tm, d), lambda i: (i, 0)),
                  pl.BlockSpec(conv_w.shape, lambda i: (0, 0))],
        out_specs=pl.BlockSpec((tm, d), lambda i: (i, 0)),
        out_shape=jax.ShapeDtypeStruct((n, d), BF16),
        compiler_params=_cparams("parallel"),
        name="mix_a_conv",
    )(v, v, v, bg, conv_w)


def _mix_b_in_kernel(h_ref, hp_ref, hn_ref, g_ref, w_ref, yb_ref, u_ref, *, steps_per_seq, tt):
    i = pl.program_id(0)
    r = yb_ref.shape[1]
    tiles = h_ref.shape[0] // tt
    g = g_ref[...]
    xn = _rms(h_ref[...], g).astype(BF16)
    yb_ref[...] = jax.nn.gelu(_dot(xn, w_ref[:, 0:r])).astype(BF16)
    tm = h_ref.shape[0]
    xh = _rms(jnp.concatenate([hp_ref[...], hn_ref[...]], axis=0), g).astype(BF16)
    u_all = _dot(jnp.concatenate([xn, xh], axis=0), w_ref[:, r:2 * r])
    u = u_all[0:tm]
    uh = u_all[tm:tm + 2 * SUBLANES]
    at_start = (i % steps_per_seq) == 0
    at_end = (i % steps_per_seq) == steps_per_seq - 1
    prev2 = jnp.where(at_start, 0.0, uh[SUBLANES - 2:SUBLANES, :])
    next1 = jnp.where(at_end, 0.0, uh[SUBLANES:SUBLANES + 1, :])
    row = lax.broadcasted_iota(I32, (SUBLANES, 1), 0)
    for k in range(tiles):
        p2 = prev2 if k == 0 else u[k * tt - 2:k * tt]
        n1 = next1 if k == tiles - 1 else u[(k + 1) * tt:(k + 1) * tt + 1]
        pad = jnp.where(row == 0, p2[0:1], jnp.where(row == 1, p2[1:2], jnp.where(row == 2, n1, 0.0)))
        for s in range(r // LANES):
            lanes = slice(s * LANES, (s + 1) * LANES)
            u_ref[k, s, 0:tt, :] = u[k * tt:(k + 1) * tt, lanes]
            u_ref[k, s, tt:tt + SUBLANES, :] = pad[:, lanes]


def _mix_b_in(h, g, w_in, batch, seq, tt):
    n, d = h.shape
    r = w_in.shape[1] // 2
    tm = _tile(seq, 1024)
    steps_per_seq = seq // tm
    tiles = tm // tt
    hb = tm // SUBLANES
    nhalo = n // SUBLANES
    return pl.pallas_call(
        functools.partial(_mix_b_in_kernel, steps_per_seq=steps_per_seq, tt=tt),
        grid=(n // tm,),
        in_specs=[pl.BlockSpec((tm, d), lambda i: (i, 0)),
                  pl.BlockSpec((SUBLANES, d), lambda i: (jnp.maximum(i * hb - 1, 0), 0)),
                  pl.BlockSpec((SUBLANES, d), lambda i: (jnp.minimum((i + 1) * hb, nhalo - 1), 0)),
                  pl.BlockSpec((1, d), lambda i: (0, 0)),
                  pl.BlockSpec((d, 2 * r), lambda i: (0, 0), pipeline_mode=pl.Buffered(1))],
        out_specs=[pl.BlockSpec((tm, r), lambda i: (i, 0)),
                   pl.BlockSpec((tiles, r // LANES, None, tt + SUBLANES, LANES),
                                lambda i: (i % steps_per_seq, 0, i // steps_per_seq, 0, 0))],
        out_shape=[jax.ShapeDtypeStruct((n, r), BF16),
                   jax.ShapeDtypeStruct((seq // tt, r // LANES, batch, tt + SUBLANES, LANES), F32)],
        compiler_params=_cparams("parallel"),
        name="mix_b_in",
    )(h, h, h, g, w_in)


def _rglru_gates(u, wr_ref, br_ref, wi_ref, bi_ref, lam_ref, a_s, g_s):
    ub = u.astype(BF16)
    t_r = jnp.tanh(_dot(ub, wr_ref[...]) + 0.5 * br_ref[...])
    t_i = jnp.tanh(_dot(ub, wi_ref[...]) + 0.5 * bi_ref[...])
    c = (-0.25 * RG_C) * jax.nn.softplus(-lam_ref[...])
    th = jnp.tanh(c + c * t_r)
    q = 1.0 / (1.0 - th)
    a_s[...] = (1.0 + th) * q
    nth = -th
    root = jnp.where(nth > 0.0, nth * lax.rsqrt(nth), 0.0)
    g_s[...] = (q * root) * (u + u * t_i)


def _rglru_fwd_kernel(u_ref, cw_ref, cb_ref, wr_ref, br_ref, wi_ref, bi_ref, lam_ref,
                      hf_ref, uc_ref, x_s, a_s, g_s, carry_s, *, tt):
    nb = SUBLANES
    ttp = tt + SUBLANES
    rows = tt * nb
    slabs = u_ref.shape[0]

    @pl.when(pl.program_id(1) == 0)
    def _():
        carry_s[...] = jnp.zeros_like(carry_s)

    def gather_t(tl, dst):
        for s in range(slabs):
            x_s[pl.ds(dst, nb), s * LANES:(s + 1) * LANES] = u_ref[s, pl.ds(tl, nb, stride=ttp), :]

    def load_t(tl, c):
        gather_t(tl, pl.multiple_of((tl + 2) * nb, nb))
        return c

    lax.fori_loop(0, tt, load_t, 0, unroll=8)
    gather_t(tt, 0)
    gather_t(tt + 1, nb)
    gather_t(tt + 2, (tt + 2) * nb)

    cw = cw_ref[...]
    u = (cw[0:1, :] * x_s[0:rows, :] + cw[1:2, :] * x_s[nb:nb + rows, :]
         + cw[2:3, :] * x_s[2 * nb:2 * nb + rows, :] + cw[3:4, :] * x_s[3 * nb:3 * nb + rows, :]
         + cb_ref[...])
    uc_ref[...] = u
    _rglru_gates(u, wr_ref, br_ref, wi_ref, bi_ref, lam_ref, a_s, g_s)

    def step(tl, h):
        base = pl.multiple_of(tl * nb, nb)
        h = a_s[pl.ds(base, nb), :] * h + g_s[pl.ds(base, nb), :]
        hf_ref[pl.ds(base, nb), :] = h
        return h

    carry_s[...] = lax.fori_loop(0, tt, step, carry_s[...], unroll=8)


def _rglru_rev_kernel(uc_ref, wr_ref, br_ref, wi_ref, bi_ref, lam_ref, hf_ref, yb_ref,
                      z_ref, a_s, g_s, carry_s, nat_s, *, tt):
    nb = SUBLANES
    ttp = tt + SUBLANES
    slabs = nat_s.shape[0]

    @pl.when(pl.program_id(1) == 0)
    def _():
        carry_s[...] = jnp.zeros_like(carry_s)

    _rglru_gates(uc_ref[...], wr_ref, br_ref, wi_ref, bi_ref, lam_ref, a_s, g_s)

    def step(k, h):
        tl = tt - 1 - k
        base = pl.multiple_of(tl * nb, nb)
        h = a_s[pl.ds(base, nb), :] * h + g_s[pl.ds(base, nb), :]
        hs = h + hf_ref[pl.ds(base, nb), :]
        for s in range(slabs):
            nat_s[s, pl.ds(tl, nb, stride=ttp), :] = hs[:, s * LANES:(s + 1) * LANES]
        return h

    carry_s[...] = lax.fori_loop(0, tt, step, carry_s[...], unroll=8)
    for b in range(nb):
        for s in range(slabs):
            lanes = slice(s * LANES, (s + 1) * LANES)
            hb = nat_s[s, b * ttp:b * ttp + tt, :]
            z_ref[b, :, lanes] = (yb_ref[b, :, lanes].astype(F32) * hb).astype(BF16)


def _rglru(u_tiles, yb, conv_w, conv_b, w_r, b_r, w_i, b_i, lam, seq, tt):
    nt, nslab, batch, ttp, _ = u_tiles.shape
    r = nslab * LANES
    assert batch == SUBLANES, "time-major tile layout puts the batch on the 8 sublanes"
    nh, bk = w_r.shape[1], w_r.shape[2]
    slabs = bk // LANES
    kw = conv_w.shape[0]
    rows = tt * batch
    gates = [w_r, b_r.reshape(2, 1, r), w_i, b_i.reshape(2, 1, r), lam.reshape(2, 1, r)]

    def gate_specs(d):
        vec = pl.BlockSpec((None, 1, bk), lambda h, jj: (d, 0, h))
        mat = pl.BlockSpec((None, None, bk, bk), lambda h, jj: (d, h, 0, 0))
        return [mat, vec, mat, vec, vec]

    tm_spec = pl.BlockSpec((rows, bk), lambda h, jj: (jj, h))
    tm_rev = pl.BlockSpec((rows, bk), lambda h, jj: (nt - 1 - jj, h))
    tm_shape = jax.ShapeDtypeStruct((seq * batch, r), F32)
    state = [pltpu.VMEM((rows, bk), F32), pltpu.VMEM((rows, bk), F32), pltpu.VMEM((batch, bk), F32)]

    hf, uc = pl.pallas_call(
        functools.partial(_rglru_fwd_kernel, tt=tt),
        grid=(nh, nt),
        in_specs=[pl.BlockSpec((None, slabs, batch * ttp, LANES), lambda h, jj: (jj, h, 0, 0)),
                  pl.BlockSpec((kw, bk), lambda h, jj: (0, h)),
                  pl.BlockSpec((1, bk), lambda h, jj: (0, h))] + gate_specs(0),
        out_specs=[tm_spec, tm_spec],
        out_shape=[tm_shape, tm_shape],
        scratch_shapes=[pltpu.VMEM(((tt + 3) * batch, bk), F32)] + state,
        compiler_params=_cparams("parallel", "arbitrary"),
        name="rglru_fwd",
    )(u_tiles.reshape(nt, nslab, batch * ttp, LANES), conv_w, conv_b, *gates)

    nat_spec = pl.BlockSpec((batch, tt, bk), lambda h, jj: (0, nt - 1 - jj, h))
    z = pl.pallas_call(
        functools.partial(_rglru_rev_kernel, tt=tt),
        grid=(nh, nt),
        in_specs=[tm_rev] + gate_specs(1) + [tm_rev, nat_spec],
        out_specs=nat_spec,
        out_shape=jax.ShapeDtypeStruct((batch, seq, r), BF16),
        scratch_shapes=state + [pltpu.VMEM((slabs, batch * ttp, LANES), F32)],
        compiler_params=_cparams("parallel", "arbitrary"),
        name="rglru_rev",
    )(uc, *gates, hf, yb.reshape(batch, seq, r))
    return z.reshape(-1, r)


def _route(logits):
    lane = lax.broadcasted_iota(I32, logits.shape, 1).astype(F32)
    neg = jnp.float32(-jnp.inf)
    nolane = jnp.float32(LANES)
    gmask = lane < N_GROUPS
    gmax = jnp.max(jnp.where(gmask, logits, neg), axis=1, keepdims=True)
    eg = jnp.where(gmask, jnp.exp(logits - gmax), 0.0)
    gsum = jnp.sum(eg, axis=1, keepdims=True)
    pg = eg / gsum
    pg_top = 1.0 / gsum
    g_top = jnp.min(jnp.where(gmask & (pg == pg_top), lane, nolane), axis=1, keepdims=True)
    lo = EXPERT_LANE0 + g_top * EXPERTS_PER_GROUP
    emask = (lane >= lo) & (lane < lo + EXPERTS_PER_GROUP)
    emax = jnp.max(jnp.where(emask, logits, neg), axis=1, keepdims=True)
    ee = jnp.where(emask, jnp.exp(logits - emax), 0.0)
    esum = jnp.sum(ee, axis=1, keepdims=True)
    pe = jnp.where(emask, ee / esum, -1.0)
    p1 = 1.0 / esum
    i1 = jnp.min(jnp.where(pe == p1, lane, nolane), axis=1, keepdims=True)
    pe2 = jnp.where(lane == i1, -1.0, pe)
    p2 = jnp.max(pe2, axis=1, keepdims=True)
    i2 = jnp.min(jnp.where(pe2 == p2, lane, nolane), axis=1, keepdims=True)
    psum = p1 + p2
    return i1, i2, pg_top * (p1 / psum), pg_top * (p2 / psum)


def _post_mix_kernel(z_ref, wo_ref, h_ref, g_ref, wr_ref, br_ref,
                     h1_ref, xn_ref, meta_ref, meta_t_ref, cnt_ref, carry_ref, logits_s):
    i = pl.program_id(0)
    tm = z_ref.shape[0]

    @pl.when(i == 0)
    def _():
        logits_s[...] = jnp.zeros_like(logits_s)

    @pl.when(i <= 1)
    def _():
        carry_ref[...] = jnp.zeros_like(carry_ref)

    logits = logits_s[...]

    h1 = h_ref[...] + _dot(z_ref[...], wo_ref[...])
    h1_ref[...] = h1
    xn = _rms(h1, g_ref[...])
    xn_ref[...] = _pack_bf16_pairs(xn)
    logits_s[...] = _dot(xn.astype(BF16), wr_ref[...].astype(BF16)) + br_ref[...]

    i1, i2, w0, w1 = _route(logits)

    lane = lax.broadcasted_iota(I32, (tm, LANES), 1).astype(F32)
    oh0 = lane == i1
    oh1 = lane == i2
    both = (oh0 | oh1).astype(BF16)
    ri = lax.broadcasted_iota(I32, (tm, tm), 0)
    ci = lax.broadcasted_iota(I32, (tm, tm), 1)
    before = (ri > ci).astype(BF16)
    cnt_before = _dot(before, both) + carry_ref[...]
    rank0 = jnp.sum(jnp.where(oh0, cnt_before, 0.0), axis=1, keepdims=True)
    rank1 = jnp.sum(jnp.where(oh1, cnt_before, 0.0), axis=1, keepdims=True)
    carry = carry_ref[...] + jnp.sum(both.astype(F32), axis=0, keepdims=True)
    carry_ref[...] = carry
    cnt_ref[...] = carry

    e0 = i1 - EXPERT_LANE0
    e1 = i2 - EXPERT_LANE0
    meta = jnp.zeros((tm, LANES), F32)
    for ln, val in ((M_E0, e0), (M_E1, e1), (M_R0, rank0), (M_R1, rank1), (M_W0, w0), (M_W1, w1)):
        meta = jnp.where(lane == ln, val, meta)
    meta_ref[...] = meta
    meta_t_ref[...] = meta.T[0:SUBLANES, :]


def _post_mix(z, w_out, h, g, w_router, b_router):
    n, d = h.shape
    k = z.shape[1]
    tm = _tile(n, 512)
    nt = n // tm

    def proj(i):
        return (jnp.minimum(i, nt - 1), 0)

    def routed(i):
        return (jnp.maximum(i - 1, 0), 0)

    return pl.pallas_call(
        _post_mix_kernel,
        grid=(nt + 1,),
        in_specs=[pl.BlockSpec((tm, k), proj),
                  pl.BlockSpec((k, d), lambda i: (0, 0)),
                  pl.BlockSpec((tm, d), proj),
                  pl.BlockSpec((1, d), lambda i: (0, 0)),
                  pl.BlockSpec((d, LANES), lambda i: (0, 0)),
                  pl.BlockSpec((1, LANES), lambda i: (0, 0))],
        out_specs=[pl.BlockSpec((tm, d), proj),
                   pl.BlockSpec((tm, d // 2), proj),
                   pl.BlockSpec((tm, LANES), routed),
                   pl.BlockSpec((SUBLANES, tm), lambda i: (0, jnp.maximum(i - 1, 0))),
                   pl.BlockSpec((1, LANES), lambda i: (0, 0))],
        out_shape=[jax.ShapeDtypeStruct((n, d), F32), jax.ShapeDtypeStruct((n, d // 2), U32),
                   jax.ShapeDtypeStruct((n, LANES), F32), jax.ShapeDtypeStruct((SUBLANES, n), F32),
                   jax.ShapeDtypeStruct((1, LANES), F32)],
        scratch_shapes=[pltpu.VMEM((1, LANES), F32), pltpu.VMEM((tm, LANES), F32)],
        compiler_params=_cparams("arbitrary"),
        name="post_mix_router",
    )(z, w_out, h, g, w_router, b_router)


def _plan_kernel(cnt_ref, meta_t_ref, dest_ref, estart_ref, ecount_ref, nblk_ref, pstart_ref):
    def per_expert(e, acc):
        nb_e = (cnt_ref[e] + (ROUTE_ROWS - 1)) // ROUTE_ROWS
        pstart_ref[e] = acc * ROUTE_ROWS
        estart_ref[e] = acc
        ecount_ref[e] = nb_e
        return acc + nb_e

    nblk_ref[0] = lax.fori_loop(0, N_EXPERTS, per_expert, 0)

    e0 = meta_t_ref[M_E0:M_E0 + 1, :]
    e1 = meta_t_ref[M_E1:M_E1 + 1, :]
    d0 = meta_t_ref[M_R0:M_R0 + 1, :]
    d1 = meta_t_ref[M_R1:M_R1 + 1, :]
    for e in range(N_EXPERTS):
        ps = pstart_ref[e].astype(F32)
        d0 = d0 + jnp.where(e0 == e, ps, 0.0)
        d1 = d1 + jnp.where(e1 == e, ps, 0.0)
    dest_ref[...] = jnp.zeros_like(dest_ref)
    dest_ref[0:1, :] = d0.astype(I32)
    dest_ref[1:2, :] = d1.astype(I32)


def _plan(cnt_i32, meta_t):
    n = meta_t.shape[1]
    smem = pl.BlockSpec(memory_space=pltpu.SMEM)
    return pl.pallas_call(
        _plan_kernel,
        in_specs=[smem, pl.BlockSpec((SUBLANES, n), lambda: (0, 0))],
        out_specs=[pl.BlockSpec((SUBLANES, n), lambda: (0, 0)), smem, smem, smem],
        out_shape=[jax.ShapeDtypeStruct((SUBLANES, n), I32),
                   jax.ShapeDtypeStruct((N_EXPERTS,), I32),
                   jax.ShapeDtypeStruct((N_EXPERTS,), I32),
                   jax.ShapeDtypeStruct((1,), I32)],
        scratch_shapes=[pltpu.SMEM((N_EXPERTS,), I32)],
        name="dispatch_plan",
    )(cnt_i32, meta_t)


def _sc_mesh():
    return plsc.VectorSubcoreMesh(core_axis_name="c", subcore_axis_name="s",
                                  num_cores=SC_CORES, num_subcores=SC_SUBCORES)


def _sc_scatter_rows(x, idx0, idx1, p_rows):
    m, k = x.shape
    per_w = m // (SC_CORES * SC_SUBCORES)
    assert per_w % SC_CHUNK == 0 and per_w * SC_CORES * SC_SUBCORES == m

    nchunk = per_w // SC_CHUNK
    buf = [pltpu.VMEM((SC_CHUNK,), I32), pltpu.VMEM((SC_CHUNK,), I32),
           pltpu.VMEM((SC_CHUNK, k), x.dtype),
           pltpu.SemaphoreType.DMA, pltpu.SemaphoreType.DMA, pltpu.SemaphoreType.DMA]

    @functools.partial(
        pl.kernel, mesh=_sc_mesh(), out_type=jax.ShapeDtypeStruct((p_rows, k), x.dtype),
        scratch_types=buf + buf, name="sc_scatter_rows")
    def scatter(x_hbm, i0_hbm, i1_hbm, out_hbm, *scratch):
        base = (lax.axis_index("s") * SC_CORES + lax.axis_index("c")) * per_w
        slots = (scratch[:6], scratch[6:])

        def loads(c):
            i0_v, i1_v, rows_v, lsem, _, _ = slots[c % 2]
            off = pl.multiple_of(base + c * SC_CHUNK, SC_CHUNK)
            return [pltpu.make_async_copy(i0_hbm.at[pl.ds(off, SC_CHUNK)], i0_v, lsem),
                    pltpu.make_async_copy(i1_hbm.at[pl.ds(off, SC_CHUNK)], i1_v, lsem),
                    pltpu.make_async_copy(x_hbm.at[pl.ds(off, SC_CHUNK)], rows_v, lsem)]

        def puts(c):
            i0_v, i1_v, rows_v, _, psem0, psem1 = slots[c % 2]
            return [pltpu.make_async_copy(rows_v, out_hbm.at[i0_v], psem0),
                    pltpu.make_async_copy(rows_v, out_hbm.at[i1_v], psem1)]

        for d in loads(0):
            d.start()
        for c in range(nchunk):
            for d in loads(c):
                d.wait()
            for d in puts(c):
                d.start()
            if c >= 1:
                for d in puts(c - 1):
                    d.wait()
            if c + 1 < nchunk:
                for d in loads(c + 1):
                    d.start()
        for d in puts(nchunk - 1):
            d.wait()

    return scatter(x, idx0, idx1)


def _expert_kernel(es_ref, ec_ref, nblk_ref, cnt_ref, xs_ref, wg_ref, wu_ref, wd_ref, ys_ref,
                   x_s, y_s, wg_s, wu_s, wd_s, sem):
    e = pl.program_id(0)
    nb = ec_ref[e]
    b0 = es_ref[e]
    k = x_s.shape[2]
    nb_max = ys_ref.shape[0] // ROUTE_ROWS

    def rows(b):
        return pl.ds(pl.multiple_of(b * ROUTE_ROWS, ROUTE_ROWS), ROUTE_ROWS)

    nblk = nblk_ref[0]

    def copy_in(b):
        slot = lax.rem(b, IN_SLOTS)
        return pltpu.make_async_copy(xs_ref.at[rows(b)], x_s.at[slot], sem.at[0, slot])

    def copy_out(b):
        slot = lax.rem(b, OUT_SLOTS)
        return pltpu.make_async_copy(y_s.at[slot], ys_ref.at[rows(b)], sem.at[1, slot])

    @pl.when(e == 0)
    def _():
        for b in range(IN_AHEAD):
            @pl.when(b < nblk)
            def _(b=b):
                copy_in(b).start()

    @pl.when(nb > 0)
    def _():
        wg_s[...] = wg_ref[...].astype(BF16)
        wu_s[...] = wu_ref[...].astype(BF16)
        wd_s[...] = wd_ref[...].astype(BF16)

        def mlp(b, j, width):
            row = lax.broadcasted_iota(I32, (ROUTE_ROWS, 1), 0)
            xp = jnp.concatenate(
                [jnp.where(row < cnt_ref[e] - (j + i) * ROUTE_ROWS,
                           x_s[lax.rem(b + i, IN_SLOTS)], jnp.uint32(0)) for i in range(width)], axis=0)
            x_hi, x_lo = _unpack_bf16_pairs(xp)
            x_hi = x_hi.astype(BF16)
            x_lo = x_lo.astype(BF16)
            g = _dot(x_hi, wg_s[0:k, :]) + _dot(x_lo, wg_s[k:2 * k, :])
            u = _dot(x_hi, wu_s[0:k, :]) + _dot(x_lo, wu_s[k:2 * k, :])
            hb = (jax.nn.silu(g) * u).astype(BF16)
            y = _pack_bf16_pairs(_dot(hb, wd_s[...]))
            return [y[i * ROUTE_ROWS:(i + 1) * ROUTE_ROWS] for i in range(width)]

        def unit(j, width):
            b = b0 + j
            for i in range(width):
                copy_in(b + i).wait()
            for i in range(width):
                @pl.when(b + i + IN_AHEAD < nblk)
                def _(i=i):
                    copy_in(b + i + IN_AHEAD).start()
            ys = mlp(b, j, width)
            for i in range(width):
                @pl.when(b + i >= OUT_SLOTS)
                def _(i=i):
                    copy_out(b + i - OUT_SLOTS).wait()
            for i in range(width):
                y_s[lax.rem(b + i, OUT_SLOTS)] = ys[i]
                copy_out(b + i).start()

        def full(jj, c):
            unit(EXPERT_UNIT * jj, EXPERT_UNIT)
            return c

        lax.fori_loop(0, nb // EXPERT_UNIT, full, 0)
        done = (nb // EXPERT_UNIT) * EXPERT_UNIT
        width = EXPERT_UNIT // 2
        while width >= 1:
            @pl.when((nb - done) & width != 0)
            def _(width=width, done=done):
                unit(done, width)

            done = done + ((nb - done) & width)
            width //= 2

    @pl.when(e == pl.num_programs(0) - 1)
    def _():
        for back in range(OUT_SLOTS, 0, -1):
            @pl.when(nblk >= back)
            def _(back=back):
                copy_out(nblk - back).wait()

    @pl.when(e == pl.num_programs(0) - 1)
    def _():
        y_s[0] = jnp.zeros_like(y_s[0])

        def zero(b):
            return pltpu.make_async_copy(y_s.at[0], ys_ref.at[rows(b)], sem.at[1, 0])

        def start_zero(b, c):
            zero(b).start()
            return c

        def wait_zero(b, c):
            zero(b).wait()
            return c

        lax.fori_loop(nblk_ref[0], nb_max, start_zero, 0)
        lax.fori_loop(nblk_ref[0], nb_max, wait_zero, 0)


def _experts(estart, ecount, nblk, cnt, xs, w_gate, w_up, w_down, layer):
    p, k = xs.shape
    d, f = w_gate.shape[2], w_gate.shape[3]

    def w_map(e, es, ec, nb, cn):
        return (layer, e, 0, 0)

    grid_spec = pltpu.PrefetchScalarGridSpec(
        num_scalar_prefetch=4,
        grid=(N_EXPERTS,),
        in_specs=[pl.BlockSpec(memory_space=pl.ANY),
                  pl.BlockSpec((None, None, d, f), w_map),
                  pl.BlockSpec((None, None, d, f), w_map),
                  pl.BlockSpec((None, None, f, d), w_map)],
        out_specs=pl.BlockSpec(memory_space=pl.ANY),
        scratch_shapes=[pltpu.VMEM((IN_SLOTS, ROUTE_ROWS, k), U32),
                        pltpu.VMEM((OUT_SLOTS, ROUTE_ROWS, k), U32),
                        pltpu.VMEM((d, f), BF16), pltpu.VMEM((d, f), BF16), pltpu.VMEM((f, d), BF16),
                        pltpu.SemaphoreType.DMA((2, max(IN_SLOTS, OUT_SLOTS)))],
    )
    return pl.pallas_call(
        _expert_kernel,
        grid_spec=grid_spec,
        out_shape=jax.ShapeDtypeStruct((p, k), U32),
        compiler_params=_cparams("arbitrary"),
        name="expert_mlp",
    )(estart, ecount, nblk, cnt, xs, w_gate, w_up, w_down)


def _sc_gather_pair(table, idx0, idx1, start, count):
    k = table.shape[1]
    per_w = count // (SC_CORES * SC_SUBCORES)
    assert per_w % SC_CHUNK == 0 and per_w * SC_CORES * SC_SUBCORES == count
    assert start % SC_CHUNK == 0
    rows = jax.ShapeDtypeStruct((count, k), table.dtype)

    @functools.partial(
        pl.kernel, mesh=_sc_mesh(), out_type=(rows, rows),
        scratch_types=[pltpu.VMEM((SC_CHUNK,), I32), pltpu.VMEM((SC_CHUNK,), I32),
                       pltpu.VMEM((SC_CHUNK, k), table.dtype), pltpu.VMEM((SC_CHUNK, k), table.dtype),
                       pltpu.SemaphoreType.DMA, pltpu.SemaphoreType.DMA,
                       pltpu.SemaphoreType.DMA, pltpu.SemaphoreType.DMA],
        name="sc_gather_pair")
    def gather(table_hbm, i0_hbm, i1_hbm, o0_hbm, o1_hbm, i0_v, i1_v, r0_v, r1_v,
               sem0, sem1, sem2, sem3):
        base = (lax.axis_index("s") * SC_CORES + lax.axis_index("c")) * per_w

        @pl.loop(0, per_w // SC_CHUNK)
        def _(c):
            dst = pl.multiple_of(base + c * SC_CHUNK, SC_CHUNK)
            src = pl.multiple_of(start + dst, SC_CHUNK)
            pltpu.sync_copy(i0_hbm.at[pl.ds(src, SC_CHUNK)], i0_v)
            pltpu.sync_copy(i1_hbm.at[pl.ds(src, SC_CHUNK)], i1_v)
            get0 = pltpu.async_copy(table_hbm.at[i0_v], r0_v, sem0)
            get1 = pltpu.async_copy(table_hbm.at[i1_v], r1_v, sem1)
            get0.wait()
            put0 = pltpu.async_copy(r0_v, o0_hbm.at[pl.ds(dst, SC_CHUNK)], sem2)
            get1.wait()
            put1 = pltpu.async_copy(r1_v, o1_hbm.at[pl.ds(dst, SC_CHUNK)], sem3)
            put0.wait()
            put1.wait()

    return gather(table, idx0, idx1)


def _combine_kernel(*refs, final):
    (ya_ref, yb_ref, h1_ref, meta_ref, p_ref, gp_ref, wpg_ref, bpg_ref, wp_ref,
     gf_ref) = refs[:10]
    out_ref = refs[-1]
    meta = meta_ref[...]
    w0 = meta[:, M_W0:M_W0 + 1]
    w1 = meta[:, M_W1:M_W1 + 1]
    ya_hi, ya_lo = _unpack_bf16_pairs(ya_ref[...])
    yb_hi, yb_lo = _unpack_bf16_pairs(yb_ref[...])
    y = jnp.concatenate([ya_hi * w0 + yb_hi * w1, ya_lo * w0 + yb_lo * w1], axis=1)
    h2 = h1_ref[...] + y
    xn = _rms(h2, gp_ref[...]).astype(BF16)
    gate = jax.nn.sigmoid(_dot(xn, wpg_ref[...]) + bpg_ref[...])
    pp = _dot(p_ref[...].astype(BF16), wp_ref[...])
    h3 = h2 + gate * pp
    if final:
        h3 = _rms(h3, gf_ref[...])
    out_ref[...] = h3


def _combine(dest, ys, h1, meta, p, layer, g_ple, w_ple_gate, b_ple_gate, w_ple, g_final, final):
    n, d = h1.shape
    pd = p.shape[2]
    k = ys.shape[1]
    n_part = n // COMBINE_PARTS
    tmc = _tile(n_part, 512)
    tiles = n_part // tmc
    vec = pl.BlockSpec((1, d), lambda i: (0, 0))
    out = None
    for q in range(COMBINE_PARTS):
        ya, yb = _sc_gather_pair(ys, dest[0], dest[1], q * n_part, n_part)

        def tok(i, q=q):
            return (q * tiles + i, 0)

        in_specs = [pl.BlockSpec((tmc, k), lambda i: (i, 0)),
                    pl.BlockSpec((tmc, k), lambda i: (i, 0)),
                    pl.BlockSpec((tmc, d), tok),
                    pl.BlockSpec((tmc, LANES), tok),
                    pl.BlockSpec((None, tmc, pd), lambda i, q=q: (layer, q * tiles + i, 0)),
                    vec,
                    pl.BlockSpec((d, d), lambda i: (0, 0)),
                    vec,
                    pl.BlockSpec((pd, d), lambda i: (0, 0)),
                    vec]
        args = [ya, yb, h1, meta, p, g_ple, w_ple_gate, b_ple_gate, w_ple, g_final]
        aliases = {}
        if out is not None:
            in_specs.append(pl.BlockSpec(memory_space=pl.ANY))
            args.append(out)
            aliases = {len(args) - 1: 0}
        out = pl.pallas_call(
            functools.partial(_combine_kernel, final=final),
            grid=(tiles,),
            in_specs=in_specs,
            out_specs=pl.BlockSpec((tmc, d), tok),
            out_shape=jax.ShapeDtypeStruct((n, d), F32),
            input_output_aliases=aliases,
            compiler_params=_cparams("parallel"),
            name="combine_ple",
        )(*args)
    return out


def _row(v):
    return v.reshape(1, -1)


def _moe_and_ple(z, w_out, h, i, final, norm_ffn, w_router_group, b_router_group, w_router_expert,
                 b_router_expert, w_exp_gate, w_exp_up, w_exp_down, norm_ple, w_ple, w_ple_gate,
                 b_ple_gate, norm_final, p):
    n, d = h.shape
    pad = LANES - N_GROUPS - N_EXPERTS
    w_router = jnp.concatenate(
        [w_router_group[i], w_router_expert[i], jnp.zeros((d, pad), F32)], axis=1)
    b_router = jnp.concatenate(
        [b_router_group[i], b_router_expert[i], jnp.zeros((pad,), F32)]).reshape(1, LANES)
    h1, xn, meta, meta_t, cnt = _post_mix(z, w_out, h, _row(norm_ffn[i]), w_router, b_router)
    cnt_i32 = cnt[0, EXPERT_LANE0:EXPERT_LANE0 + N_EXPERTS].astype(I32)
    nb_max = (2 * n) // ROUTE_ROWS + N_EXPERTS
    dest, estart, ecount, nblk = _plan(cnt_i32, meta_t)
    xs = _sc_scatter_rows(xn, dest[0], dest[1], nb_max * ROUTE_ROWS)
    ys = _experts(estart, ecount, nblk, cnt_i32, xs, w_exp_gate, w_exp_up, w_exp_down, i)
    return _combine(dest, ys, h1, meta, p.reshape(p.shape[0], n, -1), i, _row(norm_ple[i]),
                    w_ple_gate[i].astype(BF16), _row(b_ple_gate[i]), w_ple[i].astype(BF16),
                    _row(norm_final), final)


def kernel(x, p, norm_mix, w_in_a, conv_a, w_out_a, w_in_b, conv_b, conv_bias_b, w_rgate_b, b_rgate_b, w_igate_b, b_igate_b, lam_b, w_out_b, norm_ffn, w_router_group, b_router_group, w_router_expert, b_router_expert, w_exp_gate, w_exp_up, w_exp_down, norm_ple, w_ple, w_ple_gate, b_ple_gate, norm_final):
    batch, seq, d = x.shape
    depth = p.shape[0]
    n = batch * seq
    h = x.reshape(n, d)
    for i in range(depth):
        j = i // 2
        if i % 2 == 0:
            bg, v = _mix_a_in(h, _row(norm_mix[i]), w_in_a[j].astype(BF16))
            z = _mix_a_conv(v, bg, conv_a[j], seq)
            w_out = w_out_a[j]
        else:
            tt = _tile(seq, 256)
            yb, u_tiles = _mix_b_in(h, _row(norm_mix[i]), w_in_b[j].astype(BF16), batch, seq, tt)
            z = _rglru(u_tiles, yb, conv_b[j], _row(conv_bias_b[j]),
                       (0.5 * w_rgate_b[j]).astype(BF16), b_rgate_b[j],
                       (0.5 * w_igate_b[j]).astype(BF16),
                       b_igate_b[j], lam_b[j], seq, tt)
            w_out = w_out_b[j]
        h = _moe_and_ple(z, w_out.astype(BF16), h, i, i == depth - 1, norm_ffn, w_router_group,
                         b_router_group, w_router_expert, b_router_expert, w_exp_gate, w_exp_up,
                         w_exp_down, norm_ple, w_ple, w_ple_gate, b_ple_gate, norm_final, p)
    return h.reshape(batch, seq, d)
```

```python
import functools

import jax
import jax.numpy as jnp
from jax import lax
from jax.experimental import pallas as pl
from jax.experimental.pallas import tpu as pltpu
from jax.experimental.pallas import tpu_sc as plsc

F32 = jnp.float32
BF16 = jnp.bfloat16
I32 = jnp.int32
U32 = jnp.uint32

EPS = 1e-6
N_GROUPS = 4
EXPERTS_PER_GROUP = 8
N_EXPERTS = N_GROUPS * EXPERTS_PER_GROUP
RG_C = 8.0

LANES = 128
SUBLANES = 8
BF16_ROWS = 16
EXPERT_LANE0 = N_GROUPS
ROUTE_ROWS = 256
IN_AHEAD = 4
IN_SLOTS = IN_AHEAD + 2
OUT_SLOTS = 4
SC_CORES = 2
SC_SUBCORES = 16
SC_CHUNK = 64
COMBINE_PARTS = 4
M_E0, M_E1, M_R0, M_R1, M_W0, M_W1 = 0, 1, 2, 3, 4, 5
VMEM_LIMIT = 48 * 1024 * 1024


def _cparams(*sem):
    return pltpu.CompilerParams(dimension_semantics=sem, vmem_limit_bytes=VMEM_LIMIT)


def _rms(x, g):
    return x * lax.rsqrt(jnp.mean(x * x, axis=-1, keepdims=True) + EPS) * g


def _dot(a, b):
    return jnp.dot(a, b, preferred_element_type=F32)


def _pack_bf16_pairs(x):
    k = x.shape[1] // 2
    hi = lax.bitcast_convert_type(x[:, :k].astype(BF16).astype(F32), U32)
    lo = lax.bitcast_convert_type(x[:, k:].astype(BF16).astype(F32), U32)
    return hi | (lo >> 16)


def _unpack_bf16_pairs(p):
    hi = lax.bitcast_convert_type(p & jnp.uint32(0xFFFF0000), F32)
    lo = lax.bitcast_convert_type(p << 16, F32)
    return hi, lo


def _tile(n, want):
    t = min(n, want)
    assert n % t == 0, (n, want)
    return t


def _mix_a_in_kernel(h_ref, g_ref, w_ref, bg_ref, v_ref):
    d = h_ref.shape[1]
    xn = _rms(h_ref[...], g_ref[...]).astype(BF16)
    bg = _dot(xn, w_ref[:, 0:d])
    cg = _dot(xn, w_ref[:, d:2 * d])
    hh = _dot(xn, w_ref[:, 2 * d:3 * d])
    bg_ref[...] = bg.astype(BF16)
    v_ref[...] = (cg * hh).astype(BF16)


def _mix_a_in(h, g, w_in):
    n, d = h.shape
    tm = _tile(n, 1024)
    return pl.pallas_call(
        _mix_a_in_kernel,
        grid=(n // tm,),
        in_specs=[pl.BlockSpec((tm, d), lambda i: (i, 0)),
                  pl.BlockSpec((1, d), lambda i: (0, 0)),
                  pl.BlockSpec((d, 3 * d), lambda i: (0, 0), pipeline_mode=pl.Buffered(1))],
        out_specs=[pl.BlockSpec((tm, d), lambda i: (i, 0)),
                   pl.BlockSpec((tm, d), lambda i: (i, 0))],
        out_shape=[jax.ShapeDtypeStruct((n, d), BF16), jax.ShapeDtypeStruct((n, d), BF16)],
        compiler_params=_cparams("parallel"),
        name="mix_a_in",
    )(h, g, w_in)


def _mix_a_conv_kernel(v_ref, vp_ref, vn_ref, bg_ref, cw_ref, z_ref, *, tiles_per_seq):
    i = pl.program_id(0)
    tm = v_ref.shape[0]
    v = v_ref[...].astype(F32)
    at_start = (i % tiles_per_seq) == 0
    at_end = (i % tiles_per_seq) == tiles_per_seq - 1
    prev_row = vp_ref[...].astype(F32)[BF16_ROWS - 1:BF16_ROWS, :]
    next_row = vn_ref[...].astype(F32)[0:1, :]
    prev_row = jnp.where(at_start, 0.0, prev_row)
    next_row = jnp.where(at_end, 0.0, next_row)
    row = lax.broadcasted_iota(I32, (tm, 1), 0)
    v_dn = jnp.where(row == 0, prev_row, pltpu.roll(v, 1, 0))
    v_up = jnp.where(row == tm - 1, next_row, pltpu.roll(v, tm - 1, 0))
    cw = cw_ref[...]
    u = cw[0:1, :] * v_dn + cw[1:2, :] * v + cw[2:3, :] * v_up
    z_ref[...] = (bg_ref[...].astype(F32) * u).astype(BF16)


def _mix_a_conv(v, bg, conv_w, seq):
    n, d = v.shape
    tm = _tile(seq, 512)
    hb = tm // BF16_ROWS
    nhalo = n // BF16_ROWS
    return pl.pallas_call(
        functools.partial(_mix_a_conv_kernel, tiles_per_seq=seq // tm),
        grid=(n // tm,),
        in_specs=[pl.BlockSpec((tm, d), lambda i: (i, 0)),
                  pl.BlockSpec((BF16_ROWS, d), lambda i: (jnp.maximum(i * hb - 1, 0), 0)),
                  pl.BlockSpec((BF16_ROWS, d), lambda i: (jnp.minimum((i + 1) * hb, nhalo - 1), 0)),
                  pl.BlockSpec((tm, d), lambda i: (i, 0)),
                  pl.BlockSpec(conv_w.shape, lambda i: (0, 0))],
        out_specs=pl.BlockSpec((tm, d), lambda i: (i, 0)),
        out_shape=jax.ShapeDtypeStruct((n, d), BF16),
        compiler_params=_cparams("parallel"),
        name="mix_a_conv",
    )(v, v, v, bg, conv_w)


def _mix_b_in_kernel(h_ref, hp_ref, hn_ref, g_ref, w_ref, yb_ref, u_ref, *, steps_per_seq, tt):
    i = pl.program_id(0)
    r = yb_ref.shape[1]
    tiles = h_ref.shape[0] // tt
    g = g_ref[...]
    xn = _rms(h_ref[...], g).astype(BF16)
    yb_ref[...] = jax.nn.gelu(_dot(xn, w_ref[:, 0:r])).astype(BF16)
    tm = h_ref.shape[0]
    xh = _rms(jnp.concatenate([hp_ref[...], hn_ref[...]], axis=0), g).astype(BF16)
    u_all = _dot(jnp.concatenate([xn, xh], axis=0), w_ref[:, r:2 * r])
    u = u_all[0:tm]
    uh = u_all[tm:tm + 2 * SUBLANES]
    at_start = (i % steps_per_seq) == 0
    at_end = (i % steps_per_seq) == steps_per_seq - 1
    prev2 = jnp.where(at_start, 0.0, uh[SUBLANES - 2:SUBLANES, :])
    next1 = jnp.where(at_end, 0.0, uh[SUBLANES:SUBLANES + 1, :])
    row = lax.broadcasted_iota(I32, (SUBLANES, 1), 0)
    for k in range(tiles):
        p2 = prev2 if k == 0 else u[k * tt - 2:k * tt]
        n1 = next1 if k == tiles - 1 else u[(k + 1) * tt:(k + 1) * tt + 1]
        pad = jnp.where(row == 0, p2[0:1], jnp.where(row == 1, p2[1:2], jnp.where(row == 2, n1, 0.0)))
        for s in range(r // LANES):
            lanes = slice(s * LANES, (s + 1) * LANES)
            u_ref[k, s, 0:tt, :] = u[k * tt:(k + 1) * tt, lanes]
            u_ref[k, s, tt:tt + SUBLANES, :] = pad[:, lanes]


def _mix_b_in(h, g, w_in, batch, seq, tt):
    n, d = h.shape
    r = w_in.shape[1] // 2
    tm = _tile(seq, 1024)
    steps_per_seq = seq // tm
    tiles = tm // tt
    hb = tm // SUBLANES
    nhalo = n // SUBLANES
    return pl.pallas_call(
        functools.partial(_mix_b_in_kernel, steps_per_seq=steps_per_seq, tt=tt),
        grid=(n // tm,),
        in_specs=[pl.BlockSpec((tm, d), lambda i: (i, 0)),
                  pl.BlockSpec((SUBLANES, d), lambda i: (jnp.maximum(i * hb - 1, 0), 0)),
                  pl.BlockSpec((SUBLANES, d), lambda i: (jnp.minimum((i + 1) * hb, nhalo - 1), 0)),
                  pl.BlockSpec((1, d), lambda i: (0, 0)),
                  pl.BlockSpec((d, 2 * r), lambda i: (0, 0), pipeline_mode=pl.Buffered(1))],
        out_specs=[pl.BlockSpec((tm, r), lambda i: (i, 0)),
                   pl.BlockSpec((tiles, r // LANES, None, tt + SUBLANES, LANES),
                                lambda i: (i % steps_per_seq, 0, i // steps_per_seq, 0, 0))],
        out_shape=[jax.ShapeDtypeStruct((n, r), BF16),
                   jax.ShapeDtypeStruct((seq // tt, r // LANES, batch, tt + SUBLANES, LANES), F32)],
        compiler_params=_cparams("parallel"),
        name="mix_b_in",
    )(h, h, h, g, w_in)


def _rglru_gates(u, wr_ref, br_ref, wi_ref, bi_ref, lam_ref):
    ub = u.astype(BF16)
    t_r = jnp.tanh(_dot(ub, wr_ref[...]) + 0.5 * br_ref[...])
    t_i = jnp.tanh(_dot(ub, wi_ref[...]) + 0.5 * bi_ref[...])
    c = (-0.25 * RG_C) * jax.nn.softplus(-lam_ref[...])
    th = jnp.tanh(c + c * t_r)
    q = 1.0 / (1.0 - th)
    nth = -th
    root = jnp.where(nth > 0.0, nth * lax.rsqrt(nth), 0.0)
    return (1.0 + th) * q, (q * root) * (u + u * t_i)


def _rglru_fwd_kernel(u_ref, cw_ref, cb_ref, wr_ref, br_ref, wi_ref, bi_ref, lam_ref,
                      hf_ref, uc_ref, x_s, carry_s, *, tt):
    nb = SUBLANES
    ttp = tt + SUBLANES
    rows = tt * nb
    slabs = u_ref.shape[0]

    @pl.when(pl.program_id(1) == 0)
    def _():
        carry_s[...] = jnp.zeros_like(carry_s)

    def gather_t(tl, dst):
        for s in range(slabs):
            x_s[pl.ds(dst, nb), s * LANES:(s + 1) * LANES] = u_ref[s, pl.ds(tl, nb, stride=ttp), :]

    for tl in range(tt):
        gather_t(tl, (tl + 2) * nb)
    gather_t(tt, 0)
    gather_t(tt + 1, nb)
    gather_t(tt + 2, (tt + 2) * nb)

    cw = cw_ref[...]
    u = (cw[0:1, :] * x_s[0:rows, :] + cw[1:2, :] * x_s[nb:nb + rows, :]
         + cw[2:3, :] * x_s[2 * nb:2 * nb + rows, :] + cw[3:4, :] * x_s[3 * nb:3 * nb + rows, :]
         + cb_ref[...])
    uc_ref[...] = u
    a, g = _rglru_gates(u, wr_ref, br_ref, wi_ref, bi_ref, lam_ref)
    h = carry_s[...]
    for tl in range(tt):
        r = slice(tl * nb, (tl + 1) * nb)
        h = a[r] * h + g[r]
        hf_ref[r, :] = h
    carry_s[...] = h


def _rglru_rev_kernel(uc_ref, wr_ref, br_ref, wi_ref, bi_ref, lam_ref, hf_ref, yb_ref,
                      z_ref, carry_s, nat_s, a_s, g_s, *, tt):
    nb = SUBLANES
    ttp = tt + SUBLANES
    slabs = nat_s.shape[0]

    @pl.when(pl.program_id(1) == 0)
    def _():
        carry_s[...] = jnp.zeros_like(carry_s)

    a_s[...], g_s[...] = _rglru_gates(uc_ref[...], wr_ref, br_ref, wi_ref, bi_ref, lam_ref)

    def step(k, h):
        tl = tt - 1 - k
        base = pl.multiple_of(tl * nb, nb)
        h = a_s[pl.ds(base, nb), :] * h + g_s[pl.ds(base, nb), :]
        hs = h + hf_ref[pl.ds(base, nb), :]
        for s in range(slabs):
            nat_s[s, pl.ds(tl, nb, stride=ttp), :] = hs[:, s * LANES:(s + 1) * LANES]
        return h

    carry_s[...] = lax.fori_loop(0, tt, step, carry_s[...], unroll=8)
    for b in range(nb):
        for s in range(slabs):
            lanes = slice(s * LANES, (s + 1) * LANES)
            hb = nat_s[s, b * ttp:b * ttp + tt, :]
            z_ref[b, :, lanes] = (yb_ref[b, :, lanes].astype(F32) * hb).astype(BF16)


def _rglru(u_tiles, yb, conv_w, conv_b, w_r, b_r, w_i, b_i, lam, seq, tt):
    nt, nslab, batch, ttp, _ = u_tiles.shape
    r = nslab * LANES
    assert batch == SUBLANES, "time-major tile layout puts the batch on the 8 sublanes"
    nh, bk = w_r.shape[1], w_r.shape[2]
    slabs = bk // LANES
    kw = conv_w.shape[0]
    rows = tt * batch
    gates = [w_r, b_r.reshape(2, 1, r), w_i, b_i.reshape(2, 1, r), lam.reshape(2, 1, r)]

    def gate_specs(d):
        vec = pl.BlockSpec((None, 1, bk), lambda h, jj: (d, 0, h))
        mat = pl.BlockSpec((None, None, bk, bk), lambda h, jj: (d, h, 0, 0))
        return [mat, vec, mat, vec, vec]

    tm_spec = pl.BlockSpec((rows, bk), lambda h, jj: (jj, h))
    tm_rev = pl.BlockSpec((rows, bk), lambda h, jj: (nt - 1 - jj, h))
    tm_shape = jax.ShapeDtypeStruct((seq * batch, r), F32)
    state = [pltpu.VMEM((batch, bk), F32)]

    hf, uc = pl.pallas_call(
        functools.partial(_rglru_fwd_kernel, tt=tt),
        grid=(nh, nt),
        in_specs=[pl.BlockSpec((None, slabs, batch * ttp, LANES), lambda h, jj: (jj, h, 0, 0)),
                  pl.BlockSpec((kw, bk), lambda h, jj: (0, h)),
                  pl.BlockSpec((1, bk), lambda h, jj: (0, h))] + gate_specs(0),
        out_specs=[tm_spec, tm_spec],
        out_shape=[tm_shape, tm_shape],
        scratch_shapes=[pltpu.VMEM(((tt + 3) * batch, bk), F32)] + state,
        compiler_params=_cparams("parallel", "arbitrary"),
        name="rglru_fwd",
    )(u_tiles.reshape(nt, nslab, batch * ttp, LANES), conv_w, conv_b, *gates)

    nat_spec = pl.BlockSpec((batch, tt, bk), lambda h, jj: (0, nt - 1 - jj, h))
    z = pl.pallas_call(
        functools.partial(_rglru_rev_kernel, tt=tt),
        grid=(nh, nt),
        in_specs=[tm_rev] + gate_specs(1) + [tm_rev, nat_spec],
        out_specs=nat_spec,
        out_shape=jax.ShapeDtypeStruct((batch, seq, r), BF16),
        scratch_shapes=state + [pltpu.VMEM((slabs, batch * ttp, LANES), F32),
                                pltpu.VMEM((rows, bk), F32), pltpu.VMEM((rows, bk), F32)],
        compiler_params=_cparams("parallel", "arbitrary"),
        name="rglru_rev",
    )(uc, *gates, hf, yb.reshape(batch, seq, r))
    return z.reshape(-1, r)


def _route(logits):
    lane = lax.broadcasted_iota(I32, logits.shape, 1).astype(F32)
    neg = jnp.float32(-jnp.inf)
    nolane = jnp.float32(LANES)
    gmask = lane < N_GROUPS
    gmax = jnp.max(jnp.where(gmask, logits, neg), axis=1, keepdims=True)
    eg = jnp.where(gmask, jnp.exp(logits - gmax), 0.0)
    gsum = jnp.sum(eg, axis=1, keepdims=True)
    pg = eg / gsum
    pg_top = 1.0 / gsum
    g_top = jnp.min(jnp.where(gmask & (pg == pg_top), lane, nolane), axis=1, keepdims=True)
    lo = EXPERT_LANE0 + g_top * EXPERTS_PER_GROUP
    emask = (lane >= lo) & (lane < lo + EXPERTS_PER_GROUP)
    emax = jnp.max(jnp.where(emask, logits, neg), axis=1, keepdims=True)
    ee = jnp.where(emask, jnp.exp(logits - emax), 0.0)
    esum = jnp.sum(ee, axis=1, keepdims=True)
    pe = jnp.where(emask, ee / esum, -1.0)
    p1 = 1.0 / esum
    i1 = jnp.min(jnp.where(pe == p1, lane, nolane), axis=1, keepdims=True)
    pe2 = jnp.where(lane == i1, -1.0, pe)
    p2 = jnp.max(pe2, axis=1, keepdims=True)
    i2 = jnp.min(jnp.where(pe2 == p2, lane, nolane), axis=1, keepdims=True)
    psum = p1 + p2
    return i1, i2, pg_top * (p1 / psum), pg_top * (p2 / psum)


def _post_mix_kernel(z_ref, wo_ref, h_ref, g_ref, wr_ref, br_ref,
                     h1_ref, xn_ref, meta_ref, meta_t_ref, cnt_ref, carry_ref, logits_s):
    i = pl.program_id(0)
    tm = z_ref.shape[0]

    @pl.when(i == 0)
    def _():
        logits_s[...] = jnp.zeros_like(logits_s)

    @pl.when(i <= 1)
    def _():
        carry_ref[...] = jnp.zeros_like(carry_ref)

    logits = logits_s[...]

    h1 = h_ref[...] + _dot(z_ref[...], wo_ref[...])
    h1_ref[...] = h1
    xn = _rms(h1, g_ref[...])
    xn_ref[...] = _pack_bf16_pairs(xn)
    logits_s[...] = _dot(xn.astype(BF16), wr_ref[...].astype(BF16)) + br_ref[...]

    i1, i2, w0, w1 = _route(logits)

    lane = lax.broadcasted_iota(I32, (tm, LANES), 1).astype(F32)
    oh0 = lane == i1
    oh1 = lane == i2
    both = (oh0 | oh1).astype(BF16)
    ri = lax.broadcasted_iota(I32, (tm, tm), 0)
    ci = lax.broadcasted_iota(I32, (tm, tm), 1)
    before = (ri > ci).astype(BF16)
    cnt_before = _dot(before, both) + carry_ref[...]
    rank0 = jnp.sum(jnp.where(oh0, cnt_before, 0.0), axis=1, keepdims=True)
    rank1 = jnp.sum(jnp.where(oh1, cnt_before, 0.0), axis=1, keepdims=True)
    carry = carry_ref[...] + jnp.sum(both.astype(F32), axis=0, keepdims=True)
    carry_ref[...] = carry
    cnt_ref[...] = carry

    e0 = i1 - EXPERT_LANE0
    e1 = i2 - EXPERT_LANE0
    meta = jnp.zeros((tm, LANES), F32)
    for ln, val in ((M_E0, e0), (M_E1, e1), (M_R0, rank0), (M_R1, rank1), (M_W0, w0), (M_W1, w1)):
        meta = jnp.where(lane == ln, val, meta)
    meta_ref[...] = meta
    meta_t_ref[...] = meta.T[0:SUBLANES, :]


def _post_mix(z, w_out, h, g, w_router, b_router):
    n, d = h.shape
    k = z.shape[1]
    tm = _tile(n, 512)
    nt = n // tm

    def proj(i):
        return (jnp.minimum(i, nt - 1), 0)

    def routed(i):
        return (jnp.maximum(i - 1, 0), 0)

    return pl.pallas_call(
        _post_mix_kernel,
        grid=(nt + 1,),
        in_specs=[pl.BlockSpec((tm, k), proj),
                  pl.BlockSpec((k, d), lambda i: (0, 0)),
                  pl.BlockSpec((tm, d), proj),
                  pl.BlockSpec((1, d), lambda i: (0, 0)),
                  pl.BlockSpec((d, LANES), lambda i: (0, 0)),
                  pl.BlockSpec((1, LANES), lambda i: (0, 0))],
        out_specs=[pl.BlockSpec((tm, d), proj),
                   pl.BlockSpec((tm, d // 2), proj),
                   pl.BlockSpec((tm, LANES), routed),
                   pl.BlockSpec((SUBLANES, tm), lambda i: (0, jnp.maximum(i - 1, 0))),
                   pl.BlockSpec((1, LANES), lambda i: (0, 0))],
        out_shape=[jax.ShapeDtypeStruct((n, d), F32), jax.ShapeDtypeStruct((n, d // 2), U32),
                   jax.ShapeDtypeStruct((n, LANES), F32), jax.ShapeDtypeStruct((SUBLANES, n), F32),
                   jax.ShapeDtypeStruct((1, LANES), F32)],
        scratch_shapes=[pltpu.VMEM((1, LANES), F32), pltpu.VMEM((tm, LANES), F32)],
        compiler_params=_cparams("arbitrary"),
        name="post_mix_router",
    )(z, w_out, h, g, w_router, b_router)


def _plan_kernel(cnt_ref, meta_t_ref, dest_ref, estart_ref, ecount_ref, nblk_ref, pstart_ref):
    def per_expert(e, acc):
        nb_e = (cnt_ref[e] + (ROUTE_ROWS - 1)) // ROUTE_ROWS
        pstart_ref[e] = acc * ROUTE_ROWS
        estart_ref[e] = acc
        ecount_ref[e] = nb_e
        return acc + nb_e

    nblk_ref[0] = lax.fori_loop(0, N_EXPERTS, per_expert, 0)

    e0 = meta_t_ref[M_E0:M_E0 + 1, :]
    e1 = meta_t_ref[M_E1:M_E1 + 1, :]
    d0 = meta_t_ref[M_R0:M_R0 + 1, :]
    d1 = meta_t_ref[M_R1:M_R1 + 1, :]
    for e in range(N_EXPERTS):
        ps = pstart_ref[e].astype(F32)
        d0 = d0 + jnp.where(e0 == e, ps, 0.0)
        d1 = d1 + jnp.where(e1 == e, ps, 0.0)
    dest_ref[...] = jnp.zeros_like(dest_ref)
    dest_ref[0:1, :] = d0.astype(I32)
    dest_ref[1:2, :] = d1.astype(I32)


def _plan(cnt_i32, meta_t):
    n = meta_t.shape[1]
    smem = pl.BlockSpec(memory_space=pltpu.SMEM)
    return pl.pallas_call(
        _plan_kernel,
        in_specs=[smem, pl.BlockSpec((SUBLANES, n), lambda: (0, 0))],
        out_specs=[pl.BlockSpec((SUBLANES, n), lambda: (0, 0)), smem, smem, smem],
        out_shape=[jax.ShapeDtypeStruct((SUBLANES, n), I32),
                   jax.ShapeDtypeStruct((N_EXPERTS,), I32),
                   jax.ShapeDtypeStruct((N_EXPERTS,), I32),
                   jax.ShapeDtypeStruct((1,), I32)],
        scratch_shapes=[pltpu.SMEM((N_EXPERTS,), I32)],
        name="dispatch_plan",
    )(cnt_i32, meta_t)


def _sc_mesh():
    return plsc.VectorSubcoreMesh(core_axis_name="c", subcore_axis_name="s",
                                  num_cores=SC_CORES, num_subcores=SC_SUBCORES)


def _sc_scatter_rows(x, idx0, idx1, p_rows):
    m, k = x.shape
    per_w = m // (SC_CORES * SC_SUBCORES)
    assert per_w % SC_CHUNK == 0 and per_w * SC_CORES * SC_SUBCORES == m

    nchunk = per_w // SC_CHUNK
    buf = [pltpu.VMEM((SC_CHUNK,), I32), pltpu.VMEM((SC_CHUNK,), I32),
           pltpu.VMEM((SC_CHUNK, k), x.dtype),
           pltpu.SemaphoreType.DMA, pltpu.SemaphoreType.DMA, pltpu.SemaphoreType.DMA]

    @functools.partial(
        pl.kernel, mesh=_sc_mesh(), out_type=jax.ShapeDtypeStruct((p_rows, k), x.dtype),
        scratch_types=buf + buf, name="sc_scatter_rows")
    def scatter(x_hbm, i0_hbm, i1_hbm, out_hbm, *scratch):
        base = (lax.axis_index("s") * SC_CORES + lax.axis_index("c")) * per_w
        slots = (scratch[:6], scratch[6:])

        def loads(c):
            i0_v, i1_v, rows_v, lsem, _, _ = slots[c % 2]
            off = pl.multiple_of(base + c * SC_CHUNK, SC_CHUNK)
            return [pltpu.make_async_copy(i0_hbm.at[pl.ds(off, SC_CHUNK)], i0_v, lsem),
                    pltpu.make_async_copy(i1_hbm.at[pl.ds(off, SC_CHUNK)], i1_v, lsem),
                    pltpu.make_async_copy(x_hbm.at[pl.ds(off, SC_CHUNK)], rows_v, lsem)]

        def puts(c):
            i0_v, i1_v, rows_v, _, psem0, psem1 = slots[c % 2]
            return [pltpu.make_async_copy(rows_v, out_hbm.at[i0_v], psem0),
                    pltpu.make_async_copy(rows_v, out_hbm.at[i1_v], psem1)]

        for d in loads(0):
            d.start()
        for c in range(nchunk):
            for d in loads(c):
                d.wait()
            for d in puts(c):
                d.start()
            if c >= 1:
                for d in puts(c - 1):
                    d.wait()
            if c + 1 < nchunk:
                for d in loads(c + 1):
                    d.start()
        for d in puts(nchunk - 1):
            d.wait()

    return scatter(x, idx0, idx1)


def _expert_kernel(es_ref, ec_ref, nblk_ref, cnt_ref, xs_ref, wg_ref, wu_ref, wd_ref, ys_ref,
                   x_s, y_s, wg_s, wu_s, wd_s, sem):
    e = pl.program_id(0)
    nb = ec_ref[e]
    b0 = es_ref[e]
    k = x_s.shape[2]
    nb_max = ys_ref.shape[0] // ROUTE_ROWS

    def rows(b):
        return pl.ds(pl.multiple_of(b * ROUTE_ROWS, ROUTE_ROWS), ROUTE_ROWS)

    nblk = nblk_ref[0]

    def copy_in(b):
        slot = lax.rem(b, IN_SLOTS)
        return pltpu.make_async_copy(xs_ref.at[rows(b)], x_s.at[slot], sem.at[0, slot])

    def copy_out(b):
        slot = lax.rem(b, OUT_SLOTS)
        return pltpu.make_async_copy(y_s.at[slot], ys_ref.at[rows(b)], sem.at[1, slot])

    @pl.when(e == 0)
    def _():
        for b in range(IN_AHEAD):
            @pl.when(b < nblk)
            def _(b=b):
                copy_in(b).start()

    @pl.when(nb > 0)
    def _():
        wg_s[...] = wg_ref[...].astype(BF16)
        wu_s[...] = wu_ref[...].astype(BF16)
        wd_s[...] = wd_ref[...].astype(BF16)

        def mlp(b, j, width):
            row = lax.broadcasted_iota(I32, (ROUTE_ROWS, 1), 0)
            xp = jnp.concatenate(
                [jnp.where(row < cnt_ref[e] - (j + i) * ROUTE_ROWS,
                           x_s[lax.rem(b + i, IN_SLOTS)], jnp.uint32(0)) for i in range(width)], axis=0)
            x_hi, x_lo = _unpack_bf16_pairs(xp)
            x_hi = x_hi.astype(BF16)
            x_lo = x_lo.astype(BF16)
            g = _dot(x_hi, wg_s[0:k, :]) + _dot(x_lo, wg_s[k:2 * k, :])
            u = _dot(x_hi, wu_s[0:k, :]) + _dot(x_lo, wu_s[k:2 * k, :])
            hb = (jax.nn.silu(g) * u).astype(BF16)
            y = _pack_bf16_pairs(_dot(hb, wd_s[...]))
            return [y[i * ROUTE_ROWS:(i + 1) * ROUTE_ROWS] for i in range(width)]

        def unit(j, width):
            b = b0 + j
            for i in range(width):
                copy_in(b + i).wait()
            for i in range(width):
                @pl.when(b + i + IN_AHEAD < nblk)
                def _(i=i):
                    copy_in(b + i + IN_AHEAD).start()
            ys = mlp(b, j, width)
            for i in range(width):
                @pl.when(b + i >= OUT_SLOTS)
                def _(i=i):
                    copy_out(b + i - OUT_SLOTS).wait()
            for i in range(width):
                y_s[lax.rem(b + i, OUT_SLOTS)] = ys[i]
                copy_out(b + i).start()

        def pair(jj, c):
            unit(2 * jj, 2)
            return c

        lax.fori_loop(0, nb // 2, pair, 0)

        @pl.when(nb % 2 == 1)
        def _():
            unit(nb - 1, 1)

    @pl.when(e == pl.num_programs(0) - 1)
    def _():
        for back in range(OUT_SLOTS, 0, -1):
            @pl.when(nblk >= back)
            def _(back=back):
                copy_out(nblk - back).wait()

    @pl.when(e == pl.num_programs(0) - 1)
    def _():
        y_s[0] = jnp.zeros_like(y_s[0])

        def zero(b):
            return pltpu.make_async_copy(y_s.at[0], ys_ref.at[rows(b)], sem.at[1, 0])

        def start_zero(b, c):
            zero(b).start()
            return c

        def wait_zero(b, c):
            zero(b).wait()
            return c

        lax.fori_loop(nblk_ref[0], nb_max, start_zero, 0)
        lax.fori_loop(nblk_ref[0], nb_max, wait_zero, 0)


def _experts(estart, ecount, nblk, cnt, xs, w_gate, w_up, w_down, layer):
    p, k = xs.shape
    d, f = w_gate.shape[2], w_gate.shape[3]

    def w_map(e, es, ec, nb, cn):
        return (layer, e, 0, 0)

    grid_spec = pltpu.PrefetchScalarGridSpec(
        num_scalar_prefetch=4,
        grid=(N_EXPERTS,),
        in_specs=[pl.BlockSpec(memory_space=pl.ANY),
                  pl.BlockSpec((None, None, d, f), w_map),
                  pl.BlockSpec((None, None, d, f), w_map),
                  pl.BlockSpec((None, None, f, d), w_map)],
        out_specs=pl.BlockSpec(memory_space=pl.ANY),
        scratch_shapes=[pltpu.VMEM((IN_SLOTS, ROUTE_ROWS, k), U32),
                        pltpu.VMEM((OUT_SLOTS, ROUTE_ROWS, k), U32),
                        pltpu.VMEM((d, f), BF16), pltpu.VMEM((d, f), BF16), pltpu.VMEM((f, d), BF16),
                        pltpu.SemaphoreType.DMA((2, max(IN_SLOTS, OUT_SLOTS)))],
    )
    return pl.pallas_call(
        _expert_kernel,
        grid_spec=grid_spec,
        out_shape=jax.ShapeDtypeStruct((p, k), U32),
        compiler_params=_cparams("arbitrary"),
        name="expert_mlp",
    )(estart, ecount, nblk, cnt, xs, w_gate, w_up, w_down)


def _sc_gather_pair(table, idx0, idx1, start, count):
    k = table.shape[1]
    per_w = count // (SC_CORES * SC_SUBCORES)
    assert per_w % SC_CHUNK == 0 and per_w * SC_CORES * SC_SUBCORES == count
    assert start % SC_CHUNK == 0
    rows = jax.ShapeDtypeStruct((count, k), table.dtype)

    @functools.partial(
        pl.kernel, mesh=_sc_mesh(), out_type=(rows, rows),
        scratch_types=[pltpu.VMEM((SC_CHUNK,), I32), pltpu.VMEM((SC_CHUNK,), I32),
                       pltpu.VMEM((SC_CHUNK, k), table.dtype), pltpu.VMEM((SC_CHUNK, k), table.dtype),
                       pltpu.SemaphoreType.DMA, pltpu.SemaphoreType.DMA,
                       pltpu.SemaphoreType.DMA, pltpu.SemaphoreType.DMA],
        name="sc_gather_pair")
    def gather(table_hbm, i0_hbm, i1_hbm, o0_hbm, o1_hbm, i0_v, i1_v, r0_v, r1_v,
               sem0, sem1, sem2, sem3):
        base = (lax.axis_index("s") * SC_CORES + lax.axis_index("c")) * per_w

        @pl.loop(0, per_w // SC_CHUNK)
        def _(c):
            dst = pl.multiple_of(base + c * SC_CHUNK, SC_CHUNK)
            src = pl.multiple_of(start + dst, SC_CHUNK)
            pltpu.sync_copy(i0_hbm.at[pl.ds(src, SC_CHUNK)], i0_v)
            pltpu.sync_copy(i1_hbm.at[pl.ds(src, SC_CHUNK)], i1_v)
            get0 = pltpu.async_copy(table_hbm.at[i0_v], r0_v, sem0)
            get1 = pltpu.async_copy(table_hbm.at[i1_v], r1_v, sem1)
            get0.wait()
            put0 = pltpu.async_copy(r0_v, o0_hbm.at[pl.ds(dst, SC_CHUNK)], sem2)
            get1.wait()
            put1 = pltpu.async_copy(r1_v, o1_hbm.at[pl.ds(dst, SC_CHUNK)], sem3)
            put0.wait()
            put1.wait()

    return gather(table, idx0, idx1)


def _combine_kernel(*refs, final):
    (ya_ref, yb_ref, h1_ref, meta_ref, p_ref, gp_ref, wpg_ref, bpg_ref, wp_ref,
     gf_ref) = refs[:10]
    out_ref = refs[-1]
    meta = meta_ref[...]
    w0 = meta[:, M_W0:M_W0 + 1]
    w1 = meta[:, M_W1:M_W1 + 1]
    ya_hi, ya_lo = _unpack_bf16_pairs(ya_ref[...])
    yb_hi, yb_lo = _unpack_bf16_pairs(yb_ref[...])
    y = jnp.concatenate([ya_hi * w0 + yb_hi * w1, ya_lo * w0 + yb_lo * w1], axis=1)
    h2 = h1_ref[...] + y
    xn = _rms(h2, gp_ref[...]).astype(BF16)
    gate = jax.nn.sigmoid(_dot(xn, wpg_ref[...]) + bpg_ref[...])
    pp = _dot(p_ref[...].astype(BF16), wp_ref[...])
    h3 = h2 + gate * pp
    if final:
        h3 = _rms(h3, gf_ref[...])
    out_ref[...] = h3


def _combine(dest, ys, h1, meta, p, layer, g_ple, w_ple_gate, b_ple_gate, w_ple, g_final, final):
    n, d = h1.shape
    pd = p.shape[2]
    k = ys.shape[1]
    n_part = n // COMBINE_PARTS
    tmc = _tile(n_part, 512)
    tiles = n_part // tmc
    vec = pl.BlockSpec((1, d), lambda i: (0, 0))
    out = None
    for q in range(COMBINE_PARTS):
        ya, yb = _sc_gather_pair(ys, dest[0], dest[1], q * n_part, n_part)

        def tok(i, q=q):
            return (q * tiles + i, 0)

        in_specs = [pl.BlockSpec((tmc, k), lambda i: (i, 0)),
                    pl.BlockSpec((tmc, k), lambda i: (i, 0)),
                    pl.BlockSpec((tmc, d), tok),
                    pl.BlockSpec((tmc, LANES), tok),
                    pl.BlockSpec((None, tmc, pd), lambda i, q=q: (layer, q * tiles + i, 0)),
                    vec,
                    pl.BlockSpec((d, d), lambda i: (0, 0)),
                    vec,
                    pl.BlockSpec((pd, d), lambda i: (0, 0)),
                    vec]
        args = [ya, yb, h1, meta, p, g_ple, w_ple_gate, b_ple_gate, w_ple, g_final]
        aliases = {}
        if out is not None:
            in_specs.append(pl.BlockSpec(memory_space=pl.ANY))
            args.append(out)
            aliases = {len(args) - 1: 0}
        out = pl.pallas_call(
            functools.partial(_combine_kernel, final=final),
            grid=(tiles,),
            in_specs=in_specs,
            out_specs=pl.BlockSpec((tmc, d), tok),
            out_shape=jax.ShapeDtypeStruct((n, d), F32),
            input_output_aliases=aliases,
            compiler_params=_cparams("parallel"),
            name="combine_ple",
        )(*args)
    return out


def _row(v):
    return v.reshape(1, -1)


def _moe_and_ple(z, w_out, h, i, final, norm_ffn, w_router_group, b_router_group, w_router_expert,
                 b_router_expert, w_exp_gate, w_exp_up, w_exp_down, norm_ple, w_ple, w_ple_gate,
                 b_ple_gate, norm_final, p):
    n, d = h.shape
    pad = LANES - N_GROUPS - N_EXPERTS
    w_router = jnp.concatenate(
        [w_router_group[i], w_router_expert[i], jnp.zeros((d, pad), F32)], axis=1)
    b_router = jnp.concatenate(
        [b_router_group[i], b_router_expert[i], jnp.zeros((pad,), F32)]).reshape(1, LANES)
    h1, xn, meta, meta_t, cnt = _post_mix(z, w_out, h, _row(norm_ffn[i]), w_router, b_router)
    cnt_i32 = cnt[0, EXPERT_LANE0:EXPERT_LANE0 + N_EXPERTS].astype(I32)
    nb_max = (2 * n) // ROUTE_ROWS + N_EXPERTS
    dest, estart, ecount, nblk = _plan(cnt_i32, meta_t)
    xs = _sc_scatter_rows(xn, dest[0], dest[1], nb_max * ROUTE_ROWS)
    ys = _experts(estart, ecount, nblk, cnt_i32, xs, w_exp_gate, w_exp_up, w_exp_down, i)
    return _combine(dest, ys, h1, meta, p.reshape(p.shape[0], n, -1), i, _row(norm_ple[i]),
                    w_ple_gate[i].astype(BF16), _row(b_ple_gate[i]), w_ple[i].astype(BF16),
                    _row(norm_final), final)


def kernel(x, p, norm_mix, w_in_a, conv_a, w_out_a, w_in_b, conv_b, conv_bias_b, w_rgate_b, b_rgate_b, w_igate_b, b_igate_b, lam_b, w_out_b, norm_ffn, w_router_group, b_router_group, w_router_expert, b_router_expert, w_exp_gate, w_exp_up, w_exp_down, norm_ple, w_ple, w_ple_gate, b_ple_gate, norm_final):
    batch, seq, d = x.shape
    depth = p.shape[0]
    n = batch * seq
    h = x.reshape(n, d)
    for i in range(depth):
        j = i // 2
        if i % 2 == 0:
            bg, v = _mix_a_in(h, _row(norm_mix[i]), w_in_a[j].astype(BF16))
            z = _mix_a_conv(v, bg, conv_a[j], seq)
            w_out = w_out_a[j]
        else:
            tt = _tile(seq, 256)
            yb, u_tiles = _mix_b_in(h, _row(norm_mix[i]), w_in_b[j].astype(BF16), batch, seq, tt)
            z = _rglru(u_tiles, yb, conv_b[j], _row(conv_bias_b[j]),
                       (0.5 * w_rgate_b[j]).astype(BF16), b_rgate_b[j],
                       (0.5 * w_igate_b[j]).astype(BF16),
                       b_igate_b[j], lam_b[j], seq, tt)
            w_out = w_out_b[j]
        h = _moe_and_ple(z, w_out.astype(BF16), h, i, i == depth - 1, norm_ffn, w_router_group,
                         b_router_group, w_router_expert, b_router_expert, w_exp_gate, w_exp_up,
                         w_exp_down, norm_ple, w_ple, w_ple_gate, b_ple_gate, norm_final, p)
    return h.reshape(batch, seq, d)
```

```python
import functools

import jax
import jax.numpy as jnp
from jax import lax
from jax.experimental import pallas as pl
from jax.experimental.pallas import tpu as pltpu
from jax.experimental.pallas import tpu_sc as plsc

F32 = jnp.float32
BF16 = jnp.bfloat16
I32 = jnp.int32
U32 = jnp.uint32

EPS = 1e-6
N_GROUPS = 4
EXPERTS_PER_GROUP = 8
N_EXPERTS = N_GROUPS * EXPERTS_PER_GROUP
RG_C = 8.0

LANES = 128
SUBLANES = 8
BF16_ROWS = 16
EXPERT_LANE0 = N_GROUPS
ROUTE_ROWS = 256
IN_AHEAD = 4
IN_SLOTS = IN_AHEAD + 2
OUT_SLOTS = 4
SC_CORES = 2
SC_SUBCORES = 16
SC_CHUNK = 64
COMBINE_PARTS = 4
M_E0, M_E1, M_R0, M_R1, M_W0, M_W1 = 0, 1, 2, 3, 4, 5
VMEM_LIMIT = 48 * 1024 * 1024


def _cparams(*sem):
    return pltpu.CompilerParams(dimension_semantics=sem, vmem_limit_bytes=VMEM_LIMIT)


def _rms(x, g):
    return x * lax.rsqrt(jnp.mean(x * x, axis=-1, keepdims=True) + EPS) * g


def _dot(a, b):
    return jnp.dot(a, b, preferred_element_type=F32)


def _pack_bf16_pairs(x):
    k = x.shape[1] // 2
    hi = lax.bitcast_convert_type(x[:, :k].astype(BF16).astype(F32), U32)
    lo = lax.bitcast_convert_type(x[:, k:].astype(BF16).astype(F32), U32)
    return hi | (lo >> 16)


def _unpack_bf16_pairs(p):
    hi = lax.bitcast_convert_type(p & jnp.uint32(0xFFFF0000), F32)
    lo = lax.bitcast_convert_type(p << 16, F32)
    return hi, lo


def _tile(n, want):
    t = min(n, want)
    assert n % t == 0, (n, want)
    return t


def _mix_a_in_kernel(h_ref, g_ref, w_ref, bg_ref, v_ref):
    d = h_ref.shape[1]
    xn = _rms(h_ref[...], g_ref[...]).astype(BF16)
    bg = _dot(xn, w_ref[:, 0:d])
    cg = _dot(xn, w_ref[:, d:2 * d])
    hh = _dot(xn, w_ref[:, 2 * d:3 * d])
    bg_ref[...] = bg.astype(BF16)
    v_ref[...] = (cg * hh).astype(BF16)


def _mix_a_in(h, g, w_in):
    n, d = h.shape
    tm = _tile(n, 1024)
    return pl.pallas_call(
        _mix_a_in_kernel,
        grid=(n // tm,),
        in_specs=[pl.BlockSpec((tm, d), lambda i: (i, 0)),
                  pl.BlockSpec((1, d), lambda i: (0, 0)),
                  pl.BlockSpec((d, 3 * d), lambda i: (0, 0), pipeline_mode=pl.Buffered(1))],
        out_specs=[pl.BlockSpec((tm, d), lambda i: (i, 0)),
                   pl.BlockSpec((tm, d), lambda i: (i, 0))],
        out_shape=[jax.ShapeDtypeStruct((n, d), BF16), jax.ShapeDtypeStruct((n, d), BF16)],
        compiler_params=_cparams("parallel"),
        name="mix_a_in",
    )(h, g, w_in)


def _mix_a_conv_kernel(v_ref, vp_ref, vn_ref, bg_ref, cw_ref, z_ref, *, tiles_per_seq):
    i = pl.program_id(0)
    tm = v_ref.shape[0]
    v = v_ref[...].astype(F32)
    at_start = (i % tiles_per_seq) == 0
    at_end = (i % tiles_per_seq) == tiles_per_seq - 1
    prev_row = vp_ref[...].astype(F32)[BF16_ROWS - 1:BF16_ROWS, :]
    next_row = vn_ref[...].astype(F32)[0:1, :]
    prev_row = jnp.where(at_start, 0.0, prev_row)
    next_row = jnp.where(at_end, 0.0, next_row)
    row = lax.broadcasted_iota(I32, (tm, 1), 0)
    v_dn = jnp.where(row == 0, prev_row, pltpu.roll(v, 1, 0))
    v_up = jnp.where(row == tm - 1, next_row, pltpu.roll(v, tm - 1, 0))
    cw = cw_ref[...]
    u = cw[0:1, :] * v_dn + cw[1:2, :] * v + cw[2:3, :] * v_up
    z_ref[...] = (bg_ref[...].astype(F32) * u).astype(BF16)


def _mix_a_conv(v, bg, conv_w, seq):
    n, d = v.shape
    tm = _tile(seq, 512)
    hb = tm // BF16_ROWS
    nhalo = n // BF16_ROWS
    return pl.pallas_call(
        functools.partial(_mix_a_conv_kernel, tiles_per_seq=seq // tm),
        grid=(n // tm,),
        in_specs=[pl.BlockSpec((tm, d), lambda i: (i, 0)),
                  pl.BlockSpec((BF16_ROWS, d), lambda i: (jnp.maximum(i * hb - 1, 0), 0)),
                  pl.BlockSpec((BF16_ROWS, d), lambda i: (jnp.minimum((i + 1) * hb, nhalo - 1), 0)),
                  pl.BlockSpec((tm, d), lambda i: (i, 0)),
                  pl.BlockSpec(conv_w.shape, lambda i: (0, 0))],
        out_specs=pl.BlockSpec((tm, d), lambda i: (i, 0)),
        out_shape=jax.ShapeDtypeStruct((n, d), BF16),
        compiler_params=_cparams("parallel"),
        name="mix_a_conv",
    )(v, v, v, bg, conv_w)


def _mix_b_in_kernel(h_ref, hp_ref, hn_ref, g_ref, w_ref, yb_ref, u_ref, *, steps_per_seq, tt):
    i = pl.program_id(0)
    r = yb_ref.shape[1]
    tiles = h_ref.shape[0] // tt
    g = g_ref[...]
    xn = _rms(h_ref[...], g).astype(BF16)
    yb_ref[...] = jax.nn.gelu(_dot(xn, w_ref[:, 0:r])).astype(BF16)
    tm = h_ref.shape[0]
    xh = _rms(jnp.concatenate([hp_ref[...], hn_ref[...]], axis=0), g).astype(BF16)
    u_all = _dot(jnp.concatenate([xn, xh], axis=0), w_ref[:, r:2 * r])
    u = u_all[0:tm]
    uh = u_all[tm:tm + 2 * SUBLANES]
    at_start = (i % steps_per_seq) == 0
    at_end = (i % steps_per_seq) == steps_per_seq - 1
    prev2 = jnp.where(at_start, 0.0, uh[SUBLANES - 2:SUBLANES, :])
    next1 = jnp.where(at_end, 0.0, uh[SUBLANES:SUBLANES + 1, :])
    row = lax.broadcasted_iota(I32, (SUBLANES, 1), 0)
    for k in range(tiles):
        p2 = prev2 if k == 0 else u[k * tt - 2:k * tt]
        n1 = next1 if k == tiles - 1 else u[(k + 1) * tt:(k + 1) * tt + 1]
        pad = jnp.where(row == 0, p2[0:1], jnp.where(row == 1, p2[1:2], jnp.where(row == 2, n1, 0.0)))
        for s in range(r // LANES):
            lanes = slice(s * LANES, (s + 1) * LANES)
            u_ref[k, s, 0:tt, :] = u[k * tt:(k + 1) * tt, lanes]
            u_ref[k, s, tt:tt + SUBLANES, :] = pad[:, lanes]


def _mix_b_in(h, g, w_in, batch, seq, tt):
    n, d = h.shape
    r = w_in.shape[1] // 2
    tm = _tile(seq, 1024)
    steps_per_seq = seq // tm
    tiles = tm // tt
    hb = tm // SUBLANES
    nhalo = n // SUBLANES
    return pl.pallas_call(
        functools.partial(_mix_b_in_kernel, steps_per_seq=steps_per_seq, tt=tt),
        grid=(n // tm,),
        in_specs=[pl.BlockSpec((tm, d), lambda i: (i, 0)),
                  pl.BlockSpec((SUBLANES, d), lambda i: (jnp.maximum(i * hb - 1, 0), 0)),
                  pl.BlockSpec((SUBLANES, d), lambda i: (jnp.minimum((i + 1) * hb, nhalo - 1), 0)),
                  pl.BlockSpec((1, d), lambda i: (0, 0)),
                  pl.BlockSpec((d, 2 * r), lambda i: (0, 0), pipeline_mode=pl.Buffered(1))],
        out_specs=[pl.BlockSpec((tm, r), lambda i: (i, 0)),
                   pl.BlockSpec((tiles, r // LANES, None, tt + SUBLANES, LANES),
                                lambda i: (i % steps_per_seq, 0, i // steps_per_seq, 0, 0))],
        out_shape=[jax.ShapeDtypeStruct((n, r), BF16),
                   jax.ShapeDtypeStruct((seq // tt, r // LANES, batch, tt + SUBLANES, LANES), F32)],
        compiler_params=_cparams("parallel"),
        name="mix_b_in",
    )(h, h, h, g, w_in)


def _rglru_gates(u, wr_ref, br_ref, wi_ref, bi_ref, lam_ref):
    ub = u.astype(BF16)
    t_r = jnp.tanh(_dot(ub, wr_ref[...]) + 0.5 * br_ref[...])
    t_i = jnp.tanh(_dot(ub, wi_ref[...]) + 0.5 * bi_ref[...])
    c = (-0.25 * RG_C) * jax.nn.softplus(-lam_ref[...])
    th = jnp.tanh(c + c * t_r)
    q = 1.0 / (1.0 - th)
    nth = -th
    root = jnp.where(nth > 0.0, nth * lax.rsqrt(nth), 0.0)
    return (1.0 + th) * q, (q * root) * (u + u * t_i)


def _rglru_fwd_kernel(u_ref, cw_ref, cb_ref, wr_ref, br_ref, wi_ref, bi_ref, lam_ref,
                      hf_ref, uc_ref, x_s, carry_s, *, tt):
    nb = SUBLANES
    ttp = tt + SUBLANES
    rows = tt * nb
    slabs = u_ref.shape[0]

    @pl.when(pl.program_id(1) == 0)
    def _():
        carry_s[...] = jnp.zeros_like(carry_s)

    def gather_t(tl, dst):
        for s in range(slabs):
            x_s[pl.ds(dst, nb), s * LANES:(s + 1) * LANES] = u_ref[s, pl.ds(tl, nb, stride=ttp), :]

    for tl in range(tt):
        gather_t(tl, (tl + 2) * nb)
    gather_t(tt, 0)
    gather_t(tt + 1, nb)
    gather_t(tt + 2, (tt + 2) * nb)

    cw = cw_ref[...]
    u = (cw[0:1, :] * x_s[0:rows, :] + cw[1:2, :] * x_s[nb:nb + rows, :]
         + cw[2:3, :] * x_s[2 * nb:2 * nb + rows, :] + cw[3:4, :] * x_s[3 * nb:3 * nb + rows, :]
         + cb_ref[...])
    a, g = _rglru_gates(u, wr_ref, br_ref, wi_ref, bi_ref, lam_ref)
    h = carry_s[...]
    for tl in range(tt):
        r = slice(tl * nb, (tl + 1) * nb)
        back = slice((tt - 1 - tl) * nb, (tt - tl) * nb)
        h = a[r] * h + g[r]
        hf_ref[back, :] = h
        uc_ref[back, :] = u[r]
    carry_s[...] = h


def _rglru_rev_kernel(uc_ref, wr_ref, br_ref, wi_ref, bi_ref, lam_ref, hf_ref, yb_ref,
                      z_ref, carry_s, nat_s, *, tt):
    nb = SUBLANES
    ttp = tt + SUBLANES
    slabs = nat_s.shape[0]

    @pl.when(pl.program_id(1) == 0)
    def _():
        carry_s[...] = jnp.zeros_like(carry_s)

    a, g = _rglru_gates(uc_ref[...], wr_ref, br_ref, wi_ref, bi_ref, lam_ref)
    h = carry_s[...]
    for k in range(tt):
        r = slice(k * nb, (k + 1) * nb)
        h = a[r] * h + g[r]
        hs = h + hf_ref[r, :]
        for s in range(slabs):
            nat_s[s, pl.ds(tt - 1 - k, nb, stride=ttp), :] = hs[:, s * LANES:(s + 1) * LANES]
    carry_s[...] = h
    for b in range(nb):
        for s in range(slabs):
            lanes = slice(s * LANES, (s + 1) * LANES)
            hb = nat_s[s, b * ttp:b * ttp + tt, :]
            z_ref[b, :, lanes] = (yb_ref[b, :, lanes].astype(F32) * hb).astype(BF16)


def _rglru(u_tiles, yb, conv_w, conv_b, w_r, b_r, w_i, b_i, lam, seq, tt):
    nt, nslab, batch, ttp, _ = u_tiles.shape
    r = nslab * LANES
    assert batch == SUBLANES, "time-major tile layout puts the batch on the 8 sublanes"
    nh, bk = w_r.shape[1], w_r.shape[2]
    slabs = bk // LANES
    kw = conv_w.shape[0]
    rows = tt * batch
    gates = [w_r, b_r.reshape(2, 1, r), w_i, b_i.reshape(2, 1, r), lam.reshape(2, 1, r)]

    def gate_specs(d):
        vec = pl.BlockSpec((None, 1, bk), lambda h, jj: (d, 0, h))
        mat = pl.BlockSpec((None, None, bk, bk), lambda h, jj: (d, h, 0, 0))
        return [mat, vec, mat, vec, vec]

    tm_spec = pl.BlockSpec((rows, bk), lambda h, jj: (jj, h))
    tm_rev = pl.BlockSpec((rows, bk), lambda h, jj: (nt - 1 - jj, h))
    tm_shape = jax.ShapeDtypeStruct((seq * batch, r), F32)
    state = [pltpu.VMEM((batch, bk), F32)]

    hf, uc = pl.pallas_call(
        functools.partial(_rglru_fwd_kernel, tt=tt),
        grid=(nh, nt),
        in_specs=[pl.BlockSpec((None, slabs, batch * ttp, LANES), lambda h, jj: (jj, h, 0, 0)),
                  pl.BlockSpec((kw, bk), lambda h, jj: (0, h)),
                  pl.BlockSpec((1, bk), lambda h, jj: (0, h))] + gate_specs(0),
        out_specs=[tm_spec, tm_spec],
        out_shape=[tm_shape, tm_shape],
        scratch_shapes=[pltpu.VMEM(((tt + 3) * batch, bk), F32)] + state,
        compiler_params=_cparams("parallel", "arbitrary"),
        name="rglru_fwd",
    )(u_tiles.reshape(nt, nslab, batch * ttp, LANES), conv_w, conv_b, *gates)

    nat_spec = pl.BlockSpec((batch, tt, bk), lambda h, jj: (0, nt - 1 - jj, h))
    z = pl.pallas_call(
        functools.partial(_rglru_rev_kernel, tt=tt),
        grid=(nh, nt),
        in_specs=[tm_rev] + gate_specs(1) + [tm_rev, nat_spec],
        out_specs=nat_spec,
        out_shape=jax.ShapeDtypeStruct((batch, seq, r), BF16),
        scratch_shapes=state + [pltpu.VMEM((slabs, batch * ttp, LANES), F32)],
        compiler_params=_cparams("parallel", "arbitrary"),
        name="rglru_rev",
    )(uc, *gates, hf, yb.reshape(batch, seq, r))
    return z.reshape(-1, r)


def _route(logits):
    lane = lax.broadcasted_iota(I32, logits.shape, 1).astype(F32)
    neg = jnp.float32(-jnp.inf)
    nolane = jnp.float32(LANES)
    gmask = lane < N_GROUPS
    gmax = jnp.max(jnp.where(gmask, logits, neg), axis=1, keepdims=True)
    eg = jnp.where(gmask, jnp.exp(logits - gmax), 0.0)
    gsum = jnp.sum(eg, axis=1, keepdims=True)
    pg = eg / gsum
    pg_top = 1.0 / gsum
    g_top = jnp.min(jnp.where(gmask & (pg == pg_top), lane, nolane), axis=1, keepdims=True)
    lo = EXPERT_LANE0 + g_top * EXPERTS_PER_GROUP
    emask = (lane >= lo) & (lane < lo + EXPERTS_PER_GROUP)
    emax = jnp.max(jnp.where(emask, logits, neg), axis=1, keepdims=True)
    ee = jnp.where(emask, jnp.exp(logits - emax), 0.0)
    esum = jnp.sum(ee, axis=1, keepdims=True)
    pe = jnp.where(emask, ee / esum, -1.0)
    p1 = 1.0 / esum
    i1 = jnp.min(jnp.where(pe == p1, lane, nolane), axis=1, keepdims=True)
    pe2 = jnp.where(lane == i1, -1.0, pe)
    p2 = jnp.max(pe2, axis=1, keepdims=True)
    i2 = jnp.min(jnp.where(pe2 == p2, lane, nolane), axis=1, keepdims=True)
    psum = p1 + p2
    return i1, i2, pg_top * (p1 / psum), pg_top * (p2 / psum)


def _post_mix_kernel(z_ref, wo_ref, h_ref, g_ref, wr_ref, br_ref,
                     h1_ref, xn_ref, meta_ref, meta_t_ref, cnt_ref, carry_ref, logits_s):
    i = pl.program_id(0)
    tm = z_ref.shape[0]

    @pl.when(i == 0)
    def _():
        logits_s[...] = jnp.zeros_like(logits_s)

    @pl.when(i <= 1)
    def _():
        carry_ref[...] = jnp.zeros_like(carry_ref)

    logits = logits_s[...]

    h1 = h_ref[...] + _dot(z_ref[...], wo_ref[...])
    h1_ref[...] = h1
    xn = _rms(h1, g_ref[...])
    xn_ref[...] = _pack_bf16_pairs(xn)
    logits_s[...] = _dot(xn.astype(BF16), wr_ref[...].astype(BF16)) + br_ref[...]

    i1, i2, w0, w1 = _route(logits)

    lane = lax.broadcasted_iota(I32, (tm, LANES), 1).astype(F32)
    oh0 = lane == i1
    oh1 = lane == i2
    both = (oh0 | oh1).astype(BF16)
    ri = lax.broadcasted_iota(I32, (tm, tm), 0)
    ci = lax.broadcasted_iota(I32, (tm, tm), 1)
    before = (ri > ci).astype(BF16)
    cnt_before = _dot(before, both) + carry_ref[...]
    rank0 = jnp.sum(jnp.where(oh0, cnt_before, 0.0), axis=1, keepdims=True)
    rank1 = jnp.sum(jnp.where(oh1, cnt_before, 0.0), axis=1, keepdims=True)
    carry = carry_ref[...] + jnp.sum(both.astype(F32), axis=0, keepdims=True)
    carry_ref[...] = carry
    cnt_ref[...] = carry

    e0 = i1 - EXPERT_LANE0
    e1 = i2 - EXPERT_LANE0
    meta = jnp.zeros((tm, LANES), F32)
    for ln, val in ((M_E0, e0), (M_E1, e1), (M_R0, rank0), (M_R1, rank1), (M_W0, w0), (M_W1, w1)):
        meta = jnp.where(lane == ln, val, meta)
    meta_ref[...] = meta
    meta_t_ref[...] = meta.T[0:SUBLANES, :]


def _post_mix(z, w_out, h, g, w_router, b_router):
    n, d = h.shape
    k = z.shape[1]
    tm = _tile(n, 512)
    nt = n // tm

    def proj(i):
        return (jnp.minimum(i, nt - 1), 0)

    def routed(i):
        return (jnp.maximum(i - 1, 0), 0)

    return pl.pallas_call(
        _post_mix_kernel,
        grid=(nt + 1,),
        in_specs=[pl.BlockSpec((tm, k), proj),
                  pl.BlockSpec((k, d), lambda i: (0, 0)),
                  pl.BlockSpec((tm, d), proj),
                  pl.BlockSpec((1, d), lambda i: (0, 0)),
                  pl.BlockSpec((d, LANES), lambda i: (0, 0)),
                  pl.BlockSpec((1, LANES), lambda i: (0, 0))],
        out_specs=[pl.BlockSpec((tm, d), proj),
                   pl.BlockSpec((tm, d // 2), proj),
                   pl.BlockSpec((tm, LANES), routed),
                   pl.BlockSpec((SUBLANES, tm), lambda i: (0, jnp.maximum(i - 1, 0))),
                   pl.BlockSpec((1, LANES), lambda i: (0, 0))],
        out_shape=[jax.ShapeDtypeStruct((n, d), F32), jax.ShapeDtypeStruct((n, d // 2), U32),
                   jax.ShapeDtypeStruct((n, LANES), F32), jax.ShapeDtypeStruct((SUBLANES, n), F32),
                   jax.ShapeDtypeStruct((1, LANES), F32)],
        scratch_shapes=[pltpu.VMEM((1, LANES), F32), pltpu.VMEM((tm, LANES), F32)],
        compiler_params=_cparams("arbitrary"),
        name="post_mix_router",
    )(z, w_out, h, g, w_router, b_router)


def _plan_kernel(cnt_ref, meta_t_ref, dest_ref, estart_ref, ecount_ref, nblk_ref, pstart_ref):
    def per_expert(e, acc):
        nb_e = (cnt_ref[e] + (ROUTE_ROWS - 1)) // ROUTE_ROWS
        pstart_ref[e] = acc * ROUTE_ROWS
        estart_ref[e] = acc
        ecount_ref[e] = nb_e
        return acc + nb_e

    nblk_ref[0] = lax.fori_loop(0, N_EXPERTS, per_expert, 0)

    e0 = meta_t_ref[M_E0:M_E0 + 1, :]
    e1 = meta_t_ref[M_E1:M_E1 + 1, :]
    d0 = meta_t_ref[M_R0:M_R0 + 1, :]
    d1 = meta_t_ref[M_R1:M_R1 + 1, :]
    for e in range(N_EXPERTS):
        ps = pstart_ref[e].astype(F32)
        d0 = d0 + jnp.where(e0 == e, ps, 0.0)
        d1 = d1 + jnp.where(e1 == e, ps, 0.0)
    dest_ref[...] = jnp.zeros_like(dest_ref)
    dest_ref[0:1, :] = d0.astype(I32)
    dest_ref[1:2, :] = d1.astype(I32)


def _plan(cnt_i32, meta_t):
    n = meta_t.shape[1]
    smem = pl.BlockSpec(memory_space=pltpu.SMEM)
    return pl.pallas_call(
        _plan_kernel,
        in_specs=[smem, pl.BlockSpec((SUBLANES, n), lambda: (0, 0))],
        out_specs=[pl.BlockSpec((SUBLANES, n), lambda: (0, 0)), smem, smem, smem],
        out_shape=[jax.ShapeDtypeStruct((SUBLANES, n), I32),
                   jax.ShapeDtypeStruct((N_EXPERTS,), I32),
                   jax.ShapeDtypeStruct((N_EXPERTS,), I32),
                   jax.ShapeDtypeStruct((1,), I32)],
        scratch_shapes=[pltpu.SMEM((N_EXPERTS,), I32)],
        name="dispatch_plan",
    )(cnt_i32, meta_t)


def _sc_mesh():
    return plsc.VectorSubcoreMesh(core_axis_name="c", subcore_axis_name="s",
                                  num_cores=SC_CORES, num_subcores=SC_SUBCORES)


def _sc_scatter_rows(x, idx0, idx1, p_rows):
    m, k = x.shape
    per_w = m // (SC_CORES * SC_SUBCORES)
    assert per_w % SC_CHUNK == 0 and per_w * SC_CORES * SC_SUBCORES == m

    nchunk = per_w // SC_CHUNK
    buf = [pltpu.VMEM((SC_CHUNK,), I32), pltpu.VMEM((SC_CHUNK,), I32),
           pltpu.VMEM((SC_CHUNK, k), x.dtype),
           pltpu.SemaphoreType.DMA, pltpu.SemaphoreType.DMA, pltpu.SemaphoreType.DMA]

    @functools.partial(
        pl.kernel, mesh=_sc_mesh(), out_type=jax.ShapeDtypeStruct((p_rows, k), x.dtype),
        scratch_types=buf + buf, name="sc_scatter_rows")
    def scatter(x_hbm, i0_hbm, i1_hbm, out_hbm, *scratch):
        base = (lax.axis_index("s") * SC_CORES + lax.axis_index("c")) * per_w
        slots = (scratch[:6], scratch[6:])

        def loads(c):
            i0_v, i1_v, rows_v, lsem, _, _ = slots[c % 2]
            off = pl.multiple_of(base + c * SC_CHUNK, SC_CHUNK)
            return [pltpu.make_async_copy(i0_hbm.at[pl.ds(off, SC_CHUNK)], i0_v, lsem),
                    pltpu.make_async_copy(i1_hbm.at[pl.ds(off, SC_CHUNK)], i1_v, lsem),
                    pltpu.make_async_copy(x_hbm.at[pl.ds(off, SC_CHUNK)], rows_v, lsem)]

        def puts(c):
            i0_v, i1_v, rows_v, _, psem0, psem1 = slots[c % 2]
            return [pltpu.make_async_copy(rows_v, out_hbm.at[i0_v], psem0),
                    pltpu.make_async_copy(rows_v, out_hbm.at[i1_v], psem1)]

        for d in loads(0):
            d.start()
        for c in range(nchunk):
            for d in loads(c):
                d.wait()
            for d in puts(c):
                d.start()
            if c >= 1:
                for d in puts(c - 1):
                    d.wait()
            if c + 1 < nchunk:
                for d in loads(c + 1):
                    d.start()
        for d in puts(nchunk - 1):
            d.wait()

    return scatter(x, idx0, idx1)


def _expert_kernel(es_ref, ec_ref, nblk_ref, cnt_ref, xs_ref, wg_ref, wu_ref, wd_ref, ys_ref,
                   x_s, y_s, wg_s, wu_s, wd_s, sem):
    e = pl.program_id(0)
    nb = ec_ref[e]
    b0 = es_ref[e]
    k = x_s.shape[2]
    nb_max = ys_ref.shape[0] // ROUTE_ROWS

    def rows(b):
        return pl.ds(pl.multiple_of(b * ROUTE_ROWS, ROUTE_ROWS), ROUTE_ROWS)

    nblk = nblk_ref[0]

    def copy_in(b):
        slot = lax.rem(b, IN_SLOTS)
        return pltpu.make_async_copy(xs_ref.at[rows(b)], x_s.at[slot], sem.at[0, slot])

    def copy_out(b):
        slot = lax.rem(b, OUT_SLOTS)
        return pltpu.make_async_copy(y_s.at[slot], ys_ref.at[rows(b)], sem.at[1, slot])

    @pl.when(e == 0)
    def _():
        for b in range(IN_AHEAD):
            @pl.when(b < nblk)
            def _(b=b):
                copy_in(b).start()

    @pl.when(nb > 0)
    def _():
        wg_s[...] = wg_ref[...].astype(BF16)
        wu_s[...] = wu_ref[...].astype(BF16)
        wd_s[...] = wd_ref[...].astype(BF16)

        def mlp(b, j, width):
            row = lax.broadcasted_iota(I32, (ROUTE_ROWS, 1), 0)
            xp = jnp.concatenate(
                [jnp.where(row < cnt_ref[e] - (j + i) * ROUTE_ROWS,
                           x_s[lax.rem(b + i, IN_SLOTS)], jnp.uint32(0)) for i in range(width)], axis=0)
            x_hi, x_lo = _unpack_bf16_pairs(xp)
            x_hi = x_hi.astype(BF16)
            x_lo = x_lo.astype(BF16)
            g = _dot(x_hi, wg_s[0:k, :]) + _dot(x_lo, wg_s[k:2 * k, :])
            u = _dot(x_hi, wu_s[0:k, :]) + _dot(x_lo, wu_s[k:2 * k, :])
            hb = (jax.nn.silu(g) * u).astype(BF16)
            y = _pack_bf16_pairs(_dot(hb, wd_s[...]))
            return [y[i * ROUTE_ROWS:(i + 1) * ROUTE_ROWS] for i in range(width)]

        def unit(j, width):
            b = b0 + j
            for i in range(width):
                copy_in(b + i).wait()
            for i in range(width):
                @pl.when(b + i + IN_AHEAD < nblk)
                def _(i=i):
                    copy_in(b + i + IN_AHEAD).start()
            ys = mlp(b, j, width)
            for i in range(width):
                @pl.when(b + i >= OUT_SLOTS)
                def _(i=i):
                    copy_out(b + i - OUT_SLOTS).wait()
            for i in range(width):
                y_s[lax.rem(b + i, OUT_SLOTS)] = ys[i]
                copy_out(b + i).start()

        def pair(jj, c):
            unit(2 * jj, 2)
            return c

        lax.fori_loop(0, nb // 2, pair, 0)

        @pl.when(nb % 2 == 1)
        def _():
            unit(nb - 1, 1)

    @pl.when(e == pl.num_programs(0) - 1)
    def _():
        for back in range(OUT_SLOTS, 0, -1):
            @pl.when(nblk >= back)
            def _(back=back):
                copy_out(nblk - back).wait()

    @pl.when(e == pl.num_programs(0) - 1)
    def _():
        y_s[0] = jnp.zeros_like(y_s[0])

        def zero(b):
            return pltpu.make_async_copy(y_s.at[0], ys_ref.at[rows(b)], sem.at[1, 0])

        def start_zero(b, c):
            zero(b).start()
            return c

        def wait_zero(b, c):
            zero(b).wait()
            return c

        lax.fori_loop(nblk_ref[0], nb_max, start_zero, 0)
        lax.fori_loop(nblk_ref[0], nb_max, wait_zero, 0)


def _experts(estart, ecount, nblk, cnt, xs, w_gate, w_up, w_down, layer):
    p, k = xs.shape
    d, f = w_gate.shape[2], w_gate.shape[3]

    def w_map(e, es, ec, nb, cn):
        return (layer, e, 0, 0)

    grid_spec = pltpu.PrefetchScalarGridSpec(
        num_scalar_prefetch=4,
        grid=(N_EXPERTS,),
        in_specs=[pl.BlockSpec(memory_space=pl.ANY),
                  pl.BlockSpec((None, None, d, f), w_map),
                  pl.BlockSpec((None, None, d, f), w_map),
                  pl.BlockSpec((None, None, f, d), w_map)],
        out_specs=pl.BlockSpec(memory_space=pl.ANY),
        scratch_shapes=[pltpu.VMEM((IN_SLOTS, ROUTE_ROWS, k), U32),
                        pltpu.VMEM((OUT_SLOTS, ROUTE_ROWS, k), U32),
                        pltpu.VMEM((d, f), BF16), pltpu.VMEM((d, f), BF16), pltpu.VMEM((f, d), BF16),
                        pltpu.SemaphoreType.DMA((2, max(IN_SLOTS, OUT_SLOTS)))],
    )
    return pl.pallas_call(
        _expert_kernel,
        grid_spec=grid_spec,
        out_shape=jax.ShapeDtypeStruct((p, k), U32),
        compiler_params=_cparams("arbitrary"),
        name="expert_mlp",
    )(estart, ecount, nblk, cnt, xs, w_gate, w_up, w_down)


def _sc_gather_pair(table, idx0, idx1, start, count):
    k = table.shape[1]
    per_w = count // (SC_CORES * SC_SUBCORES)
    assert per_w % SC_CHUNK == 0 and per_w * SC_CORES * SC_SUBCORES == count
    assert start % SC_CHUNK == 0
    rows = jax.ShapeDtypeStruct((count, k), table.dtype)

    @functools.partial(
        pl.kernel, mesh=_sc_mesh(), out_type=(rows, rows),
        scratch_types=[pltpu.VMEM((SC_CHUNK,), I32), pltpu.VMEM((SC_CHUNK,), I32),
                       pltpu.VMEM((SC_CHUNK, k), table.dtype), pltpu.VMEM((SC_CHUNK, k), table.dtype),
                       pltpu.SemaphoreType.DMA, pltpu.SemaphoreType.DMA,
                       pltpu.SemaphoreType.DMA, pltpu.SemaphoreType.DMA],
        name="sc_gather_pair")
    def gather(table_hbm, i0_hbm, i1_hbm, o0_hbm, o1_hbm, i0_v, i1_v, r0_v, r1_v,
               sem0, sem1, sem2, sem3):
        base = (lax.axis_index("s") * SC_CORES + lax.axis_index("c")) * per_w

        @pl.loop(0, per_w // SC_CHUNK)
        def _(c):
            dst = pl.multiple_of(base + c * SC_CHUNK, SC_CHUNK)
            src = pl.multiple_of(start + dst, SC_CHUNK)
            pltpu.sync_copy(i0_hbm.at[pl.ds(src, SC_CHUNK)], i0_v)
            pltpu.sync_copy(i1_hbm.at[pl.ds(src, SC_CHUNK)], i1_v)
            get0 = pltpu.async_copy(table_hbm.at[i0_v], r0_v, sem0)
            get1 = pltpu.async_copy(table_hbm.at[i1_v], r1_v, sem1)
            get0.wait()
            put0 = pltpu.async_copy(r0_v, o0_hbm.at[pl.ds(dst, SC_CHUNK)], sem2)
            get1.wait()
            put1 = pltpu.async_copy(r1_v, o1_hbm.at[pl.ds(dst, SC_CHUNK)], sem3)
            put0.wait()
            put1.wait()

    return gather(table, idx0, idx1)


def _combine_kernel(*refs, final):
    (ya_ref, yb_ref, h1_ref, meta_ref, p_ref, gp_ref, wpg_ref, bpg_ref, wp_ref,
     gf_ref) = refs[:10]
    out_ref = refs[-1]
    meta = meta_ref[...]
    w0 = meta[:, M_W0:M_W0 + 1]
    w1 = meta[:, M_W1:M_W1 + 1]
    ya_hi, ya_lo = _unpack_bf16_pairs(ya_ref[...])
    yb_hi, yb_lo = _unpack_bf16_pairs(yb_ref[...])
    y = jnp.concatenate([ya_hi * w0 + yb_hi * w1, ya_lo * w0 + yb_lo * w1], axis=1)
    h2 = h1_ref[...] + y
    xn = _rms(h2, gp_ref[...]).astype(BF16)
    gate = jax.nn.sigmoid(_dot(xn, wpg_ref[...]) + bpg_ref[...])
    pp = _dot(p_ref[...].astype(BF16), wp_ref[...])
    h3 = h2 + gate * pp
    if final:
        h3 = _rms(h3, gf_ref[...])
    out_ref[...] = h3


def _combine(dest, ys, h1, meta, p, layer, g_ple, w_ple_gate, b_ple_gate, w_ple, g_final, final):
    n, d = h1.shape
    pd = p.shape[2]
    k = ys.shape[1]
    n_part = n // COMBINE_PARTS
    tmc = _tile(n_part, 512)
    tiles = n_part // tmc
    vec = pl.BlockSpec((1, d), lambda i: (0, 0))
    out = None
    for q in range(COMBINE_PARTS):
        ya, yb = _sc_gather_pair(ys, dest[0], dest[1], q * n_part, n_part)

        def tok(i, q=q):
            return (q * tiles + i, 0)

        in_specs = [pl.BlockSpec((tmc, k), lambda i: (i, 0)),
                    pl.BlockSpec((tmc, k), lambda i: (i, 0)),
                    pl.BlockSpec((tmc, d), tok),
                    pl.BlockSpec((tmc, LANES), tok),
                    pl.BlockSpec((None, tmc, pd), lambda i, q=q: (layer, q * tiles + i, 0)),
                    vec,
                    pl.BlockSpec((d, d), lambda i: (0, 0)),
                    vec,
                    pl.BlockSpec((pd, d), lambda i: (0, 0)),
                    vec]
        args = [ya, yb, h1, meta, p, g_ple, w_ple_gate, b_ple_gate, w_ple, g_final]
        aliases = {}
        if out is not None:
            in_specs.append(pl.BlockSpec(memory_space=pl.ANY))
            args.append(out)
            aliases = {len(args) - 1: 0}
        out = pl.pallas_call(
            functools.partial(_combine_kernel, final=final),
            grid=(tiles,),
            in_specs=in_specs,
            out_specs=pl.BlockSpec((tmc, d), tok),
            out_shape=jax.ShapeDtypeStruct((n, d), F32),
            input_output_aliases=aliases,
            compiler_params=_cparams("parallel"),
            name="combine_ple",
        )(*args)
    return out


def _row(v):
    return v.reshape(1, -1)


def _moe_and_ple(z, w_out, h, i, final, norm_ffn, w_router_group, b_router_group, w_router_expert,
                 b_router_expert, w_exp_gate, w_exp_up, w_exp_down, norm_ple, w_ple, w_ple_gate,
                 b_ple_gate, norm_final, p):
    n, d = h.shape
    pad = LANES - N_GROUPS - N_EXPERTS
    w_router = jnp.concatenate(
        [w_router_group[i], w_router_expert[i], jnp.zeros((d, pad), F32)], axis=1)
    b_router = jnp.concatenate(
        [b_router_group[i], b_router_expert[i], jnp.zeros((pad,), F32)]).reshape(1, LANES)
    h1, xn, meta, meta_t, cnt = _post_mix(z, w_out, h, _row(norm_ffn[i]), w_router, b_router)
    cnt_i32 = cnt[0, EXPERT_LANE0:EXPERT_LANE0 + N_EXPERTS].astype(I32)
    nb_max = (2 * n) // ROUTE_ROWS + N_EXPERTS
    dest, estart, ecount, nblk = _plan(cnt_i32, meta_t)
    xs = _sc_scatter_rows(xn, dest[0], dest[1], nb_max * ROUTE_ROWS)
    ys = _experts(estart, ecount, nblk, cnt_i32, xs, w_exp_gate, w_exp_up, w_exp_down, i)
    return _combine(dest, ys, h1, meta, p.reshape(p.shape[0], n, -1), i, _row(norm_ple[i]),
                    w_ple_gate[i].astype(BF16), _row(b_ple_gate[i]), w_ple[i].astype(BF16),
                    _row(norm_final), final)


def kernel(x, p, norm_mix, w_in_a, conv_a, w_out_a, w_in_b, conv_b, conv_bias_b, w_rgate_b, b_rgate_b, w_igate_b, b_igate_b, lam_b, w_out_b, norm_ffn, w_router_group, b_router_group, w_router_expert, b_router_expert, w_exp_gate, w_exp_up, w_exp_down, norm_ple, w_ple, w_ple_gate, b_ple_gate, norm_final):
    batch, seq, d = x.shape
    depth = p.shape[0]
    n = batch * seq
    h = x.reshape(n, d)
    for i in range(depth):
        j = i // 2
        if i % 2 == 0:
            bg, v = _mix_a_in(h, _row(norm_mix[i]), w_in_a[j].astype(BF16))
            z = _mix_a_conv(v, bg, conv_a[j], seq)
            w_out = w_out_a[j]
        else:
            tt = _tile(seq, 256)
            yb, u_tiles = _mix_b_in(h, _row(norm_mix[i]), w_in_b[j].astype(BF16), batch, seq, tt)
            z = _rglru(u_tiles, yb, conv_b[j], _row(conv_bias_b[j]),
                       (0.5 * w_rgate_b[j]).astype(BF16), b_rgate_b[j],
                       (0.5 * w_igate_b[j]).astype(BF16),
                       b_igate_b[j], lam_b[j], seq, tt)
            w_out = w_out_b[j]
        h = _moe_and_ple(z, w_out.astype(BF16), h, i, i == depth - 1, norm_ffn, w_router_group,
                         b_router_group, w_router_expert, b_router_expert, w_exp_gate, w_exp_up,
                         w_exp_down, norm_ple, w_ple, w_ple_gate, b_ple_gate, norm_final, p)
    return h.reshape(batch, seq, d)
```

```python
import functools

import jax
import jax.numpy as jnp
from jax import lax
from jax.experimental import pallas as pl
from jax.experimental.pallas import tpu as pltpu
from jax.experimental.pallas import tpu_sc as plsc

F32 = jnp.float32
BF16 = jnp.bfloat16
I32 = jnp.int32
U32 = jnp.uint32

EPS = 1e-6
N_GROUPS = 4
EXPERTS_PER_GROUP = 8
N_EXPERTS = N_GROUPS * EXPERTS_PER_GROUP
RG_C = 8.0

LANES = 128
SUBLANES = 8
BF16_ROWS = 16
EXPERT_LANE0 = N_GROUPS
ROUTE_ROWS = 256
IN_AHEAD = 4
IN_SLOTS = IN_AHEAD + 2
OUT_SLOTS = 4
SC_CORES = 2
SC_SUBCORES = 16
SC_CHUNK = 64
COMBINE_PARTS = 4
M_E0, M_E1, M_R0, M_R1, M_W0, M_W1 = 0, 1, 2, 3, 4, 5
VMEM_LIMIT = 48 * 1024 * 1024


def _cparams(*sem):
    return pltpu.CompilerParams(dimension_semantics=sem, vmem_limit_bytes=VMEM_LIMIT)


def _rms(x, g):
    return x * lax.rsqrt(jnp.mean(x * x, axis=-1, keepdims=True) + EPS) * g


def _dot(a, b):
    return jnp.dot(a, b, preferred_element_type=F32)


def _pack_bf16_pairs(x):
    k = x.shape[1] // 2
    hi = lax.bitcast_convert_type(x[:, :k].astype(BF16).astype(F32), U32)
    lo = lax.bitcast_convert_type(x[:, k:].astype(BF16).astype(F32), U32)
    return hi | (lo >> 16)


def _unpack_bf16_pairs(p):
    hi = lax.bitcast_convert_type(p & jnp.uint32(0xFFFF0000), F32)
    lo = lax.bitcast_convert_type(p << 16, F32)
    return hi, lo


def _tile(n, want):
    t = min(n, want)
    assert n % t == 0, (n, want)
    return t


def _mix_a_in_kernel(h_ref, g_ref, w_ref, bg_ref, v_ref):
    d = h_ref.shape[1]
    xn = _rms(h_ref[...], g_ref[...]).astype(BF16)
    bg = _dot(xn, w_ref[:, 0:d])
    cg = _dot(xn, w_ref[:, d:2 * d])
    hh = _dot(xn, w_ref[:, 2 * d:3 * d])
    bg_ref[...] = bg.astype(BF16)
    v_ref[...] = (cg * hh).astype(BF16)


def _mix_a_in(h, g, w_in):
    n, d = h.shape
    tm = _tile(n, 1024)
    return pl.pallas_call(
        _mix_a_in_kernel,
        grid=(n // tm,),
        in_specs=[pl.BlockSpec((tm, d), lambda i: (i, 0)),
                  pl.BlockSpec((1, d), lambda i: (0, 0)),
                  pl.BlockSpec((d, 3 * d), lambda i: (0, 0), pipeline_mode=pl.Buffered(1))],
        out_specs=[pl.BlockSpec((tm, d), lambda i: (i, 0)),
                   pl.BlockSpec((tm, d), lambda i: (i, 0))],
        out_shape=[jax.ShapeDtypeStruct((n, d), BF16), jax.ShapeDtypeStruct((n, d), BF16)],
        compiler_params=_cparams("parallel"),
        name="mix_a_in",
    )(h, g, w_in)


def _mix_a_conv_kernel(v_ref, vp_ref, vn_ref, bg_ref, cw_ref, z_ref, *, tiles_per_seq):
    i = pl.program_id(0)
    tm = v_ref.shape[0]
    v = v_ref[...].astype(F32)
    at_start = (i % tiles_per_seq) == 0
    at_end = (i % tiles_per_seq) == tiles_per_seq - 1
    prev_row = vp_ref[...].astype(F32)[BF16_ROWS - 1:BF16_ROWS, :]
    next_row = vn_ref[...].astype(F32)[0:1, :]
    prev_row = jnp.where(at_start, 0.0, prev_row)
    next_row = jnp.where(at_end, 0.0, next_row)
    row = lax.broadcasted_iota(I32, (tm, 1), 0)
    v_dn = jnp.where(row == 0, prev_row, pltpu.roll(v, 1, 0))
    v_up = jnp.where(row == tm - 1, next_row, pltpu.roll(v, tm - 1, 0))
    cw = cw_ref[...]
    u = cw[0:1, :] * v_dn + cw[1:2, :] * v + cw[2:3, :] * v_up
    z_ref[...] = (bg_ref[...].astype(F32) * u).astype(BF16)


def _mix_a_conv(v, bg, conv_w, seq):
    n, d = v.shape
    tm = _tile(seq, 512)
    hb = tm // BF16_ROWS
    nhalo = n // BF16_ROWS
    return pl.pallas_call(
        functools.partial(_mix_a_conv_kernel, tiles_per_seq=seq // tm),
        grid=(n // tm,),
        in_specs=[pl.BlockSpec((tm, d), lambda i: (i, 0)),
                  pl.BlockSpec((BF16_ROWS, d), lambda i: (jnp.maximum(i * hb - 1, 0), 0)),
                  pl.BlockSpec((BF16_ROWS, d), lambda i: (jnp.minimum((i + 1) * hb, nhalo - 1), 0)),
                  pl.BlockSpec((tm, d), lambda i: (i, 0)),
                  pl.BlockSpec(conv_w.shape, lambda i: (0, 0))],
        out_specs=pl.BlockSpec((tm, d), lambda i: (i, 0)),
        out_shape=jax.ShapeDtypeStruct((n, d), BF16),
        compiler_params=_cparams("parallel"),
        name="mix_a_conv",
    )(v, v, v, bg, conv_w)


def _mix_b_in_kernel(h_ref, hp_ref, hn_ref, g_ref, w_ref, yb_ref, u_ref, *, steps_per_seq, tt):
    i = pl.program_id(0)
    r = yb_ref.shape[1]
    tiles = h_ref.shape[0] // tt
    g = g_ref[...]
    xn = _rms(h_ref[...], g).astype(BF16)
    yb_ref[...] = jax.nn.gelu(_dot(xn, w_ref[:, 0:r])).astype(BF16)
    tm = h_ref.shape[0]
    xh = _rms(jnp.concatenate([hp_ref[...], hn_ref[...]], axis=0), g).astype(BF16)
    u_all = _dot(jnp.concatenate([xn, xh], axis=0), w_ref[:, r:2 * r])
    u = u_all[0:tm]
    uh = u_all[tm:tm + 2 * SUBLANES]
    at_start = (i % steps_per_seq) == 0
    at_end = (i % steps_per_seq) == steps_per_seq - 1
    prev2 = jnp.where(at_start, 0.0, uh[SUBLANES - 2:SUBLANES, :])
    next1 = jnp.where(at_end, 0.0, uh[SUBLANES:SUBLANES + 1, :])
    row = lax.broadcasted_iota(I32, (SUBLANES, 1), 0)
    for k in range(tiles):
        p2 = prev2 if k == 0 else u[k * tt - 2:k * tt]
        n1 = next1 if k == tiles - 1 else u[(k + 1) * tt:(k + 1) * tt + 1]
        pad = jnp.where(row == 0, p2[0:1], jnp.where(row == 1, p2[1:2], jnp.where(row == 2, n1, 0.0)))
        for s in range(r // LANES):
            lanes = slice(s * LANES, (s + 1) * LANES)
            u_ref[k, s, 0:tt, :] = u[k * tt:(k + 1) * tt, lanes]
            u_ref[k, s, tt:tt + SUBLANES, :] = pad[:, lanes]


def _mix_b_in(h, g, w_in, batch, seq, tt):
    n, d = h.shape
    r = w_in.shape[1] // 2
    tm = _tile(seq, 1024)
    steps_per_seq = seq // tm
    tiles = tm // tt
    hb = tm // SUBLANES
    nhalo = n // SUBLANES
    return pl.pallas_call(
        functools.partial(_mix_b_in_kernel, steps_per_seq=steps_per_seq, tt=tt),
        grid=(n // tm,),
        in_specs=[pl.BlockSpec((tm, d), lambda i: (i, 0)),
                  pl.BlockSpec((SUBLANES, d), lambda i: (jnp.maximum(i * hb - 1, 0), 0)),
                  pl.BlockSpec((SUBLANES, d), lambda i: (jnp.minimum((i + 1) * hb, nhalo - 1), 0)),
                  pl.BlockSpec((1, d), lambda i: (0, 0)),
                  pl.BlockSpec((d, 2 * r), lambda i: (0, 0), pipeline_mode=pl.Buffered(1))],
        out_specs=[pl.BlockSpec((tm, r), lambda i: (i, 0)),
                   pl.BlockSpec((tiles, r // LANES, None, tt + SUBLANES, LANES),
                                lambda i: (i % steps_per_seq, 0, i // steps_per_seq, 0, 0))],
        out_shape=[jax.ShapeDtypeStruct((n, r), BF16),
                   jax.ShapeDtypeStruct((seq // tt, r // LANES, batch, tt + SUBLANES, LANES), F32)],
        compiler_params=_cparams("parallel"),
        name="mix_b_in",
    )(h, h, h, g, w_in)


def _rglru_gates(u, wr_ref, br_ref, wi_ref, bi_ref, lam_ref):
    ub = u.astype(BF16)
    t_r = jnp.tanh(_dot(ub, wr_ref[...]) + 0.5 * br_ref[...])
    t_i = jnp.tanh(_dot(ub, wi_ref[...]) + 0.5 * bi_ref[...])
    c = (-0.25 * RG_C) * jax.nn.softplus(-lam_ref[...])
    th = jnp.tanh(c + c * t_r)
    q = 1.0 / (1.0 - th)
    nth = -th
    root = jnp.where(nth > 0.0, nth * lax.rsqrt(nth), 0.0)
    return (1.0 + th) * q, (q * root) * (u + u * t_i)


def _rglru_fwd_kernel(u_ref, cw_ref, cb_ref, wr_ref, br_ref, wi_ref, bi_ref, lam_ref,
                      hf_ref, uc_ref, x_s, carry_s, *, tt):
    nb = SUBLANES
    ttp = tt + SUBLANES
    rows = tt * nb
    slabs = u_ref.shape[0]

    @pl.when(pl.program_id(1) == 0)
    def _():
        carry_s[...] = jnp.zeros_like(carry_s)

    def gather_t(tl, dst):
        for s in range(slabs):
            x_s[pl.ds(dst, nb), s * LANES:(s + 1) * LANES] = u_ref[s, pl.ds(tl, nb, stride=ttp), :]

    for tl in range(tt):
        gather_t(tl, (tl + 2) * nb)
    gather_t(tt, 0)
    gather_t(tt + 1, nb)
    gather_t(tt + 2, (tt + 2) * nb)

    cw = cw_ref[...]
    u = (cw[0:1, :] * x_s[0:rows, :] + cw[1:2, :] * x_s[nb:nb + rows, :]
         + cw[2:3, :] * x_s[2 * nb:2 * nb + rows, :] + cw[3:4, :] * x_s[3 * nb:3 * nb + rows, :]
         + cb_ref[...])
    a, g = _rglru_gates(u, wr_ref, br_ref, wi_ref, bi_ref, lam_ref)
    h = carry_s[...]
    for tl in range(tt):
        r = slice(tl * nb, (tl + 1) * nb)
        back = slice((tt - 1 - tl) * nb, (tt - tl) * nb)
        h = a[r] * h + g[r]
        hf_ref[back, :] = h.astype(BF16)
        uc_ref[back, :] = u[r].astype(BF16)
    carry_s[...] = h


def _rglru_rev_kernel(uc_ref, wr_ref, br_ref, wi_ref, bi_ref, lam_ref, hf_ref, yb_ref,
                      z_ref, carry_s, nat_s, *, tt):
    nb = SUBLANES
    ttp = tt + SUBLANES
    slabs = nat_s.shape[0]

    @pl.when(pl.program_id(1) == 0)
    def _():
        carry_s[...] = jnp.zeros_like(carry_s)

    a, g = _rglru_gates(uc_ref[...].astype(F32), wr_ref, br_ref, wi_ref, bi_ref, lam_ref)
    h = carry_s[...]
    for k in range(tt):
        r = slice(k * nb, (k + 1) * nb)
        h = a[r] * h + g[r]
        hs = h + hf_ref[r, :].astype(F32)
        for s in range(slabs):
            nat_s[s, pl.ds(tt - 1 - k, nb, stride=ttp), :] = hs[:, s * LANES:(s + 1) * LANES]
    carry_s[...] = h
    for b in range(nb):
        for s in range(slabs):
            lanes = slice(s * LANES, (s + 1) * LANES)
            hb = nat_s[s, b * ttp:b * ttp + tt, :]
            z_ref[b, :, lanes] = (yb_ref[b, :, lanes].astype(F32) * hb).astype(BF16)


def _rglru(u_tiles, yb, conv_w, conv_b, w_r, b_r, w_i, b_i, lam, seq, tt):
    nt, nslab, batch, ttp, _ = u_tiles.shape
    r = nslab * LANES
    assert batch == SUBLANES, "time-major tile layout puts the batch on the 8 sublanes"
    nh, bk = w_r.shape[1], w_r.shape[2]
    slabs = bk // LANES
    kw = conv_w.shape[0]
    rows = tt * batch
    gates = [w_r, b_r.reshape(2, 1, r), w_i, b_i.reshape(2, 1, r), lam.reshape(2, 1, r)]

    def gate_specs(d):
        vec = pl.BlockSpec((None, 1, bk), lambda h, jj: (d, 0, h))
        mat = pl.BlockSpec((None, None, bk, bk), lambda h, jj: (d, h, 0, 0))
        return [mat, vec, mat, vec, vec]

    tm_spec = pl.BlockSpec((rows, bk), lambda h, jj: (jj, h))
    tm_rev = pl.BlockSpec((rows, bk), lambda h, jj: (nt - 1 - jj, h))
    tm_shape = jax.ShapeDtypeStruct((seq * batch, r), BF16)
    state = [pltpu.VMEM((batch, bk), F32)]

    hf, uc = pl.pallas_call(
        functools.partial(_rglru_fwd_kernel, tt=tt),
        grid=(nh, nt),
        in_specs=[pl.BlockSpec((None, slabs, batch * ttp, LANES), lambda h, jj: (jj, h, 0, 0)),
                  pl.BlockSpec((kw, bk), lambda h, jj: (0, h)),
                  pl.BlockSpec((1, bk), lambda h, jj: (0, h))] + gate_specs(0),
        out_specs=[tm_spec, tm_spec],
        out_shape=[tm_shape, tm_shape],
        scratch_shapes=[pltpu.VMEM(((tt + 3) * batch, bk), F32)] + state,
        compiler_params=_cparams("parallel", "arbitrary"),
        name="rglru_fwd",
    )(u_tiles.reshape(nt, nslab, batch * ttp, LANES), conv_w, conv_b, *gates)

    nat_spec = pl.BlockSpec((batch, tt, bk), lambda h, jj: (0, nt - 1 - jj, h))
    z = pl.pallas_call(
        functools.partial(_rglru_rev_kernel, tt=tt),
        grid=(nh, nt),
        in_specs=[tm_rev] + gate_specs(1) + [tm_rev, nat_spec],
        out_specs=nat_spec,
        out_shape=jax.ShapeDtypeStruct((batch, seq, r), BF16),
        scratch_shapes=state + [pltpu.VMEM((slabs, batch * ttp, LANES), F32)],
        compiler_params=_cparams("parallel", "arbitrary"),
        name="rglru_rev",
    )(uc, *gates, hf, yb.reshape(batch, seq, r))
    return z.reshape(-1, r)


def _route(logits):
    lane = lax.broadcasted_iota(I32, logits.shape, 1).astype(F32)
    neg = jnp.float32(-jnp.inf)
    nolane = jnp.float32(LANES)
    gmask = lane < N_GROUPS
    gmax = jnp.max(jnp.where(gmask, logits, neg), axis=1, keepdims=True)
    eg = jnp.where(gmask, jnp.exp(logits - gmax), 0.0)
    gsum = jnp.sum(eg, axis=1, keepdims=True)
    pg = eg / gsum
    pg_top = 1.0 / gsum
    g_top = jnp.min(jnp.where(gmask & (pg == pg_top), lane, nolane), axis=1, keepdims=True)
    lo = EXPERT_LANE0 + g_top * EXPERTS_PER_GROUP
    emask = (lane >= lo) & (lane < lo + EXPERTS_PER_GROUP)
    emax = jnp.max(jnp.where(emask, logits, neg), axis=1, keepdims=True)
    ee = jnp.where(emask, jnp.exp(logits - emax), 0.0)
    esum = jnp.sum(ee, axis=1, keepdims=True)
    pe = jnp.where(emask, ee / esum, -1.0)
    p1 = 1.0 / esum
    i1 = jnp.min(jnp.where(pe == p1, lane, nolane), axis=1, keepdims=True)
    pe2 = jnp.where(lane == i1, -1.0, pe)
    p2 = jnp.max(pe2, axis=1, keepdims=True)
    i2 = jnp.min(jnp.where(pe2 == p2, lane, nolane), axis=1, keepdims=True)
    psum = p1 + p2
    return i1, i2, pg_top * (p1 / psum), pg_top * (p2 / psum)


def _post_mix_kernel(z_ref, wo_ref, h_ref, g_ref, wr_ref, br_ref,
                     h1_ref, xn_ref, meta_ref, meta_t_ref, cnt_ref, carry_ref, logits_s):
    i = pl.program_id(0)
    tm = z_ref.shape[0]

    @pl.when(i == 0)
    def _():
        logits_s[...] = jnp.zeros_like(logits_s)

    @pl.when(i <= 1)
    def _():
        carry_ref[...] = jnp.zeros_like(carry_ref)

    logits = logits_s[...]

    h1 = h_ref[...] + _dot(z_ref[...], wo_ref[...])
    h1_ref[...] = h1
    xn = _rms(h1, g_ref[...])
    xn_ref[...] = _pack_bf16_pairs(xn)
    logits_s[...] = _dot(xn.astype(BF16), wr_ref[...].astype(BF16)) + br_ref[...]

    i1, i2, w0, w1 = _route(logits)

    lane = lax.broadcasted_iota(I32, (tm, LANES), 1).astype(F32)
    oh0 = lane == i1
    oh1 = lane == i2
    both = (oh0 | oh1).astype(BF16)
    ri = lax.broadcasted_iota(I32, (tm, tm), 0)
    ci = lax.broadcasted_iota(I32, (tm, tm), 1)
    before = (ri > ci).astype(BF16)
    cnt_before = _dot(before, both) + carry_ref[...]
    rank0 = jnp.sum(jnp.where(oh0, cnt_before, 0.0), axis=1, keepdims=True)
    rank1 = jnp.sum(jnp.where(oh1, cnt_before, 0.0), axis=1, keepdims=True)
    carry = carry_ref[...] + jnp.sum(both.astype(F32), axis=0, keepdims=True)
    carry_ref[...] = carry
    cnt_ref[...] = carry

    e0 = i1 - EXPERT_LANE0
    e1 = i2 - EXPERT_LANE0
    meta = jnp.zeros((tm, LANES), F32)
    for ln, val in ((M_E0, e0), (M_E1, e1), (M_R0, rank0), (M_R1, rank1), (M_W0, w0), (M_W1, w1)):
        meta = jnp.where(lane == ln, val, meta)
    meta_ref[...] = meta
    meta_t_ref[...] = meta.T[0:SUBLANES, :]


def _post_mix(z, w_out, h, g, w_router, b_router):
    n, d = h.shape
    k = z.shape[1]
    tm = _tile(n, 512)
    nt = n // tm

    def proj(i):
        return (jnp.minimum(i, nt - 1), 0)

    def routed(i):
        return (jnp.maximum(i - 1, 0), 0)

    return pl.pallas_call(
        _post_mix_kernel,
        grid=(nt + 1,),
        in_specs=[pl.BlockSpec((tm, k), proj),
                  pl.BlockSpec((k, d), lambda i: (0, 0)),
                  pl.BlockSpec((tm, d), proj),
                  pl.BlockSpec((1, d), lambda i: (0, 0)),
                  pl.BlockSpec((d, LANES), lambda i: (0, 0)),
                  pl.BlockSpec((1, LANES), lambda i: (0, 0))],
        out_specs=[pl.BlockSpec((tm, d), proj),
                   pl.BlockSpec((tm, d // 2), proj),
                   pl.BlockSpec((tm, LANES), routed),
                   pl.BlockSpec((SUBLANES, tm), lambda i: (0, jnp.maximum(i - 1, 0))),
                   pl.BlockSpec((1, LANES), lambda i: (0, 0))],
        out_shape=[jax.ShapeDtypeStruct((n, d), F32), jax.ShapeDtypeStruct((n, d // 2), U32),
                   jax.ShapeDtypeStruct((n, LANES), F32), jax.ShapeDtypeStruct((SUBLANES, n), F32),
                   jax.ShapeDtypeStruct((1, LANES), F32)],
        scratch_shapes=[pltpu.VMEM((1, LANES), F32), pltpu.VMEM((tm, LANES), F32)],
        compiler_params=_cparams("arbitrary"),
        name="post_mix_router",
    )(z, w_out, h, g, w_router, b_router)


def _plan_kernel(cnt_ref, meta_t_ref, dest_ref, estart_ref, ecount_ref, nblk_ref, pstart_ref):
    def per_expert(e, acc):
        nb_e = (cnt_ref[e] + (ROUTE_ROWS - 1)) // ROUTE_ROWS
        pstart_ref[e] = acc * ROUTE_ROWS
        estart_ref[e] = acc
        ecount_ref[e] = nb_e
        return acc + nb_e

    nblk_ref[0] = lax.fori_loop(0, N_EXPERTS, per_expert, 0)

    e0 = meta_t_ref[M_E0:M_E0 + 1, :]
    e1 = meta_t_ref[M_E1:M_E1 + 1, :]
    d0 = meta_t_ref[M_R0:M_R0 + 1, :]
    d1 = meta_t_ref[M_R1:M_R1 + 1, :]
    for e in range(N_EXPERTS):
        ps = pstart_ref[e].astype(F32)
        d0 = d0 + jnp.where(e0 == e, ps, 0.0)
        d1 = d1 + jnp.where(e1 == e, ps, 0.0)
    dest_ref[...] = jnp.zeros_like(dest_ref)
    dest_ref[0:1, :] = d0.astype(I32)
    dest_ref[1:2, :] = d1.astype(I32)


def _plan(cnt_i32, meta_t):
    n = meta_t.shape[1]
    smem = pl.BlockSpec(memory_space=pltpu.SMEM)
    return pl.pallas_call(
        _plan_kernel,
        in_specs=[smem, pl.BlockSpec((SUBLANES, n), lambda: (0, 0))],
        out_specs=[pl.BlockSpec((SUBLANES, n), lambda: (0, 0)), smem, smem, smem],
        out_shape=[jax.ShapeDtypeStruct((SUBLANES, n), I32),
                   jax.ShapeDtypeStruct((N_EXPERTS,), I32),
                   jax.ShapeDtypeStruct((N_EXPERTS,), I32),
                   jax.ShapeDtypeStruct((1,), I32)],
        scratch_shapes=[pltpu.SMEM((N_EXPERTS,), I32)],
        name="dispatch_plan",
    )(cnt_i32, meta_t)


def _sc_mesh():
    return plsc.VectorSubcoreMesh(core_axis_name="c", subcore_axis_name="s",
                                  num_cores=SC_CORES, num_subcores=SC_SUBCORES)


def _sc_scatter_rows(x, idx0, idx1, p_rows):
    m, k = x.shape
    per_w = m // (SC_CORES * SC_SUBCORES)
    assert per_w % SC_CHUNK == 0 and per_w * SC_CORES * SC_SUBCORES == m

    nchunk = per_w // SC_CHUNK
    buf = [pltpu.VMEM((SC_CHUNK,), I32), pltpu.VMEM((SC_CHUNK,), I32),
           pltpu.VMEM((SC_CHUNK, k), x.dtype),
           pltpu.SemaphoreType.DMA, pltpu.SemaphoreType.DMA, pltpu.SemaphoreType.DMA]

    @functools.partial(
        pl.kernel, mesh=_sc_mesh(), out_type=jax.ShapeDtypeStruct((p_rows, k), x.dtype),
        scratch_types=buf + buf, name="sc_scatter_rows")
    def scatter(x_hbm, i0_hbm, i1_hbm, out_hbm, *scratch):
        base = (lax.axis_index("s") * SC_CORES + lax.axis_index("c")) * per_w
        slots = (scratch[:6], scratch[6:])

        def loads(c):
            i0_v, i1_v, rows_v, lsem, _, _ = slots[c % 2]
            off = pl.multiple_of(base + c * SC_CHUNK, SC_CHUNK)
            return [pltpu.make_async_copy(i0_hbm.at[pl.ds(off, SC_CHUNK)], i0_v, lsem),
                    pltpu.make_async_copy(i1_hbm.at[pl.ds(off, SC_CHUNK)], i1_v, lsem),
                    pltpu.make_async_copy(x_hbm.at[pl.ds(off, SC_CHUNK)], rows_v, lsem)]

        def puts(c):
            i0_v, i1_v, rows_v, _, psem0, psem1 = slots[c % 2]
            return [pltpu.make_async_copy(rows_v, out_hbm.at[i0_v], psem0),
                    pltpu.make_async_copy(rows_v, out_hbm.at[i1_v], psem1)]

        for d in loads(0):
            d.start()
        for c in range(nchunk):
            for d in loads(c):
                d.wait()
            for d in puts(c):
                d.start()
            if c >= 1:
                for d in puts(c - 1):
                    d.wait()
            if c + 1 < nchunk:
                for d in loads(c + 1):
                    d.start()
        for d in puts(nchunk - 1):
            d.wait()

    return scatter(x, idx0, idx1)


def _expert_kernel(es_ref, ec_ref, nblk_ref, cnt_ref, xs_ref, wg_ref, wu_ref, wd_ref, ys_ref,
                   x_s, y_s, wg_s, wu_s, wd_s, sem):
    e = pl.program_id(0)
    nb = ec_ref[e]
    b0 = es_ref[e]
    k = x_s.shape[2]
    nb_max = ys_ref.shape[0] // ROUTE_ROWS

    def rows(b):
        return pl.ds(pl.multiple_of(b * ROUTE_ROWS, ROUTE_ROWS), ROUTE_ROWS)

    nblk = nblk_ref[0]

    def copy_in(b):
        slot = lax.rem(b, IN_SLOTS)
        return pltpu.make_async_copy(xs_ref.at[rows(b)], x_s.at[slot], sem.at[0, slot])

    def copy_out(b):
        slot = lax.rem(b, OUT_SLOTS)
        return pltpu.make_async_copy(y_s.at[slot], ys_ref.at[rows(b)], sem.at[1, slot])

    @pl.when(e == 0)
    def _():
        for b in range(IN_AHEAD):
            @pl.when(b < nblk)
            def _(b=b):
                copy_in(b).start()

    @pl.when(nb > 0)
    def _():
        wg_s[...] = wg_ref[...].astype(BF16)
        wu_s[...] = wu_ref[...].astype(BF16)
        wd_s[...] = wd_ref[...].astype(BF16)

        def mlp(b, j, width):
            row = lax.broadcasted_iota(I32, (ROUTE_ROWS, 1), 0)
            xp = jnp.concatenate(
                [jnp.where(row < cnt_ref[e] - (j + i) * ROUTE_ROWS,
                           x_s[lax.rem(b + i, IN_SLOTS)], jnp.uint32(0)) for i in range(width)], axis=0)
            x_hi, x_lo = _unpack_bf16_pairs(xp)
            x_hi = x_hi.astype(BF16)
            x_lo = x_lo.astype(BF16)
            g = _dot(x_hi, wg_s[0:k, :]) + _dot(x_lo, wg_s[k:2 * k, :])
            u = _dot(x_hi, wu_s[0:k, :]) + _dot(x_lo, wu_s[k:2 * k, :])
            hb = (jax.nn.silu(g) * u).astype(BF16)
            y = _pack_bf16_pairs(_dot(hb, wd_s[...]))
            return [y[i * ROUTE_ROWS:(i + 1) * ROUTE_ROWS] for i in range(width)]

        def unit(j, width):
            b = b0 + j
            for i in range(width):
                copy_in(b + i).wait()
            for i in range(width):
                @pl.when(b + i + IN_AHEAD < nblk)
                def _(i=i):
                    copy_in(b + i + IN_AHEAD).start()
            ys = mlp(b, j, width)
            for i in range(width):
                @pl.when(b + i >= OUT_SLOTS)
                def _(i=i):
                    copy_out(b + i - OUT_SLOTS).wait()
            for i in range(width):
                y_s[lax.rem(b + i, OUT_SLOTS)] = ys[i]
                copy_out(b + i).start()

        def pair(jj, c):
            unit(2 * jj, 2)
            return c

        lax.fori_loop(0, nb // 2, pair, 0)

        @pl.when(nb % 2 == 1)
        def _():
            unit(nb - 1, 1)

    @pl.when(e == pl.num_programs(0) - 1)
    def _():
        for back in range(OUT_SLOTS, 0, -1):
            @pl.when(nblk >= back)
            def _(back=back):
                copy_out(nblk - back).wait()

    @pl.when(e == pl.num_programs(0) - 1)
    def _():
        y_s[0] = jnp.zeros_like(y_s[0])

        def zero(b):
            return pltpu.make_async_copy(y_s.at[0], ys_ref.at[rows(b)], sem.at[1, 0])

        def start_zero(b, c):
            zero(b).start()
            return c

        def wait_zero(b, c):
            zero(b).wait()
            return c

        lax.fori_loop(nblk_ref[0], nb_max, start_zero, 0)
        lax.fori_loop(nblk_ref[0], nb_max, wait_zero, 0)


def _experts(estart, ecount, nblk, cnt, xs, w_gate, w_up, w_down, layer):
    p, k = xs.shape
    d, f = w_gate.shape[2], w_gate.shape[3]

    def w_map(e, es, ec, nb, cn):
        return (layer, e, 0, 0)

    grid_spec = pltpu.PrefetchScalarGridSpec(
        num_scalar_prefetch=4,
        grid=(N_EXPERTS,),
        in_specs=[pl.BlockSpec(memory_space=pl.ANY),
                  pl.BlockSpec((None, None, d, f), w_map),
                  pl.BlockSpec((None, None, d, f), w_map),
                  pl.BlockSpec((None, None, f, d), w_map)],
        out_specs=pl.BlockSpec(memory_space=pl.ANY),
        scratch_shapes=[pltpu.VMEM((IN_SLOTS, ROUTE_ROWS, k), U32),
                        pltpu.VMEM((OUT_SLOTS, ROUTE_ROWS, k), U32),
                        pltpu.VMEM((d, f), BF16), pltpu.VMEM((d, f), BF16), pltpu.VMEM((f, d), BF16),
                        pltpu.SemaphoreType.DMA((2, max(IN_SLOTS, OUT_SLOTS)))],
    )
    return pl.pallas_call(
        _expert_kernel,
        grid_spec=grid_spec,
        out_shape=jax.ShapeDtypeStruct((p, k), U32),
        compiler_params=_cparams("arbitrary"),
        name="expert_mlp",
    )(estart, ecount, nblk, cnt, xs, w_gate, w_up, w_down)


def _sc_gather_pair(table, idx0, idx1, start, count):
    k = table.shape[1]
    per_w = count // (SC_CORES * SC_SUBCORES)
    assert per_w % SC_CHUNK == 0 and per_w * SC_CORES * SC_SUBCORES == count
    assert start % SC_CHUNK == 0
    rows = jax.ShapeDtypeStruct((count, k), table.dtype)

    @functools.partial(
        pl.kernel, mesh=_sc_mesh(), out_type=(rows, rows),
        scratch_types=[pltpu.VMEM((SC_CHUNK,), I32), pltpu.VMEM((SC_CHUNK,), I32),
                       pltpu.VMEM((SC_CHUNK, k), table.dtype), pltpu.VMEM((SC_CHUNK, k), table.dtype),
                       pltpu.SemaphoreType.DMA, pltpu.SemaphoreType.DMA,
                       pltpu.SemaphoreType.DMA, pltpu.SemaphoreType.DMA],
        name="sc_gather_pair")
    def gather(table_hbm, i0_hbm, i1_hbm, o0_hbm, o1_hbm, i0_v, i1_v, r0_v, r1_v,
               sem0, sem1, sem2, sem3):
        base = (lax.axis_index("s") * SC_CORES + lax.axis_index("c")) * per_w

        @pl.loop(0, per_w // SC_CHUNK)
        def _(c):
            dst = pl.multiple_of(base + c * SC_CHUNK, SC_CHUNK)
            src = pl.multiple_of(start + dst, SC_CHUNK)
            pltpu.sync_copy(i0_hbm.at[pl.ds(src, SC_CHUNK)], i0_v)
            pltpu.sync_copy(i1_hbm.at[pl.ds(src, SC_CHUNK)], i1_v)
            get0 = pltpu.async_copy(table_hbm.at[i0_v], r0_v, sem0)
            get1 = pltpu.async_copy(table_hbm.at[i1_v], r1_v, sem1)
            get0.wait()
            put0 = pltpu.async_copy(r0_v, o0_hbm.at[pl.ds(dst, SC_CHUNK)], sem2)
            get1.wait()
            put1 = pltpu.async_copy(r1_v, o1_hbm.at[pl.ds(dst, SC_CHUNK)], sem3)
            put0.wait()
            put1.wait()

    return gather(table, idx0, idx1)


def _combine_kernel(*refs, final):
    (ya_ref, yb_ref, h1_ref, meta_ref, p_ref, gp_ref, wpg_ref, bpg_ref, wp_ref,
     gf_ref) = refs[:10]
    out_ref = refs[-1]
    meta = meta_ref[...]
    w0 = meta[:, M_W0:M_W0 + 1]
    w1 = meta[:, M_W1:M_W1 + 1]
    ya_hi, ya_lo = _unpack_bf16_pairs(ya_ref[...])
    yb_hi, yb_lo = _unpack_bf16_pairs(yb_ref[...])
    y = jnp.concatenate([ya_hi * w0 + yb_hi * w1, ya_lo * w0 + yb_lo * w1], axis=1)
    h2 = h1_ref[...] + y
    xn = _rms(h2, gp_ref[...]).astype(BF16)
    gate = jax.nn.sigmoid(_dot(xn, wpg_ref[...]) + bpg_ref[...])
    pp = _dot(p_ref[...].astype(BF16), wp_ref[...])
    h3 = h2 + gate * pp
    if final:
        h3 = _rms(h3, gf_ref[...])
    out_ref[...] = h3


def _combine(dest, ys, h1, meta, p, layer, g_ple, w_ple_gate, b_ple_gate, w_ple, g_final, final):
    n, d = h1.shape
    pd = p.shape[2]
    k = ys.shape[1]
    n_part = n // COMBINE_PARTS
    tmc = _tile(n_part, 512)
    tiles = n_part // tmc
    vec = pl.BlockSpec((1, d), lambda i: (0, 0))
    out = None
    for q in range(COMBINE_PARTS):
        ya, yb = _sc_gather_pair(ys, dest[0], dest[1], q * n_part, n_part)

        def tok(i, q=q):
            return (q * tiles + i, 0)

        in_specs = [pl.BlockSpec((tmc, k), lambda i: (i, 0)),
                    pl.BlockSpec((tmc, k), lambda i: (i, 0)),
                    pl.BlockSpec((tmc, d), tok),
                    pl.BlockSpec((tmc, LANES), tok),
                    pl.BlockSpec((None, tmc, pd), lambda i, q=q: (layer, q * tiles + i, 0)),
                    vec,
                    pl.BlockSpec((d, d), lambda i: (0, 0)),
                    vec,
                    pl.BlockSpec((pd, d), lambda i: (0, 0)),
                    vec]
        args = [ya, yb, h1, meta, p, g_ple, w_ple_gate, b_ple_gate, w_ple, g_final]
        aliases = {}
        if out is not None:
            in_specs.append(pl.BlockSpec(memory_space=pl.ANY))
            args.append(out)
            aliases = {len(args) - 1: 0}
        out = pl.pallas_call(
            functools.partial(_combine_kernel, final=final),
            grid=(tiles,),
            in_specs=in_specs,
            out_specs=pl.BlockSpec((tmc, d), tok),
            out_shape=jax.ShapeDtypeStruct((n, d), F32),
            input_output_aliases=aliases,
            compiler_params=_cparams("parallel"),
            name="combine_ple",
        )(*args)
    return out


def _row(v):
    return v.reshape(1, -1)


def _moe_and_ple(z, w_out, h, i, final, norm_ffn, w_router_group, b_router_group, w_router_expert,
                 b_router_expert, w_exp_gate, w_exp_up, w_exp_down, norm_ple, w_ple, w_ple_gate,
                 b_ple_gate, norm_final, p):
    n, d = h.shape
    pad = LANES - N_GROUPS - N_EXPERTS
    w_router = jnp.concatenate(
        [w_router_group[i], w_router_expert[i], jnp.zeros((d, pad), F32)], axis=1)
    b_router = jnp.concatenate(
        [b_router_group[i], b_router_expert[i], jnp.zeros((pad,), F32)]).reshape(1, LANES)
    h1, xn, meta, meta_t, cnt = _post_mix(z, w_out, h, _row(norm_ffn[i]), w_router, b_router)
    cnt_i32 = cnt[0, EXPERT_LANE0:EXPERT_LANE0 + N_EXPERTS].astype(I32)
    nb_max = (2 * n) // ROUTE_ROWS + N_EXPERTS
    dest, estart, ecount, nblk = _plan(cnt_i32, meta_t)
    xs = _sc_scatter_rows(xn, dest[0], dest[1], nb_max * ROUTE_ROWS)
    ys = _experts(estart, ecount, nblk, cnt_i32, xs, w_exp_gate, w_exp_up, w_exp_down, i)
    return _combine(dest, ys, h1, meta, p.reshape(p.shape[0], n, -1), i, _row(norm_ple[i]),
                    w_ple_gate[i].astype(BF16), _row(b_ple_gate[i]), w_ple[i].astype(BF16),
                    _row(norm_final), final)


def kernel(x, p, norm_mix, w_in_a, conv_a, w_out_a, w_in_b, conv_b, conv_bias_b, w_rgate_b, b_rgate_b, w_igate_b, b_igate_b, lam_b, w_out_b, norm_ffn, w_router_group, b_router_group, w_router_expert, b_router_expert, w_exp_gate, w_exp_up, w_exp_down, norm_ple, w_ple, w_ple_gate, b_ple_gate, norm_final):
    batch, seq, d = x.shape
    depth = p.shape[0]
    n = batch * seq
    h = x.reshape(n, d)
    for i in range(depth):
        j = i // 2
        if i % 2 == 0:
            bg, v = _mix_a_in(h, _row(norm_mix[i]), w_in_a[j].astype(BF16))
            z = _mix_a_conv(v, bg, conv_a[j], seq)
            w_out = w_out_a[j]
        else:
            tt = _tile(seq, 256)
            yb, u_tiles = _mix_b_in(h, _row(norm_mix[i]), w_in_b[j].astype(BF16), batch, seq, tt)
            z = _rglru(u_tiles, yb, conv_b[j], _row(conv_bias_b[j]),
                       (0.5 * w_rgate_b[j]).astype(BF16), b_rgate_b[j],
                       (0.5 * w_igate_b[j]).astype(BF16),
                       b_igate_b[j], lam_b[j], seq, tt)
            w_out = w_out_b[j]
        h = _moe_and_ple(z, w_out.astype(BF16), h, i, i == depth - 1, norm_ffn, w_router_group,
                         b_router_group, w_router_expert, b_router_expert, w_exp_gate, w_exp_up,
                         w_exp_down, norm_ple, w_ple, w_ple_gate, b_ple_gate, norm_final, p)
    return h.reshape(batch, seq, d)
```

```python
import functools

import jax
import jax.numpy as jnp
from jax import lax
from jax.experimental import pallas as pl
from jax.experimental.pallas import tpu as pltpu
from jax.experimental.pallas import tpu_sc as plsc

F32 = jnp.float32
BF16 = jnp.bfloat16
I32 = jnp.int32
U32 = jnp.uint32

EPS = 1e-6
N_GROUPS = 4
EXPERTS_PER_GROUP = 8
N_EXPERTS = N_GROUPS * EXPERTS_PER_GROUP
RG_C = 8.0

LANES = 128
SUBLANES = 8
BF16_ROWS = 16
EXPERT_LANE0 = N_GROUPS
ROUTE_ROWS = 256
IN_AHEAD = 4
IN_SLOTS = IN_AHEAD + 2
OUT_SLOTS = 4
SC_CORES = 2
SC_SUBCORES = 16
SC_CHUNK = 64
COMBINE_PARTS = 4
M_E0, M_E1, M_R0, M_R1, M_W0, M_W1 = 0, 1, 2, 3, 4, 5
VMEM_LIMIT = 48 * 1024 * 1024


def _cparams(*sem):
    return pltpu.CompilerParams(dimension_semantics=sem, vmem_limit_bytes=VMEM_LIMIT)


def _rms(x, g):
    return x * lax.rsqrt(jnp.mean(x * x, axis=-1, keepdims=True) + EPS) * g


def _dot(a, b):
    return jnp.dot(a, b, preferred_element_type=F32)


def _pack_bf16_pairs(x):
    k = x.shape[1] // 2
    hi = lax.bitcast_convert_type(x[:, :k].astype(BF16).astype(F32), U32)
    lo = lax.bitcast_convert_type(x[:, k:].astype(BF16).astype(F32), U32)
    return hi | (lo >> 16)


def _unpack_bf16_pairs(p):
    hi = lax.bitcast_convert_type(p & jnp.uint32(0xFFFF0000), F32)
    lo = lax.bitcast_convert_type(p << 16, F32)
    return hi, lo


def _tile(n, want):
    t = min(n, want)
    assert n % t == 0, (n, want)
    return t


def _mix_a_kernel(h_ref, g_ref, w_ref, cw_ref, z_ref, v_s, bg_s, last_s, *, tiles_per_seq):
    i = pl.program_id(0)
    tm, d = h_ref.shape

    @pl.when(i == 0)
    def _():
        v_s[...] = jnp.zeros_like(v_s)
        bg_s[...] = jnp.zeros_like(bg_s)
        last_s[...] = jnp.zeros_like(last_s)

    v = v_s[...].astype(F32)
    bg_old = bg_s[...]
    before = last_s[0:1, :]

    xn = _rms(h_ref[...], g_ref[...]).astype(BF16)
    bg = _dot(xn, w_ref[:, 0:d])
    cg = _dot(xn, w_ref[:, d:2 * d])
    hh = _dot(xn, w_ref[:, 2 * d:3 * d])
    v_new = (cg * hh).astype(BF16)

    old = i - 1
    at_start = (old % tiles_per_seq) == 0
    at_end = (old % tiles_per_seq) == tiles_per_seq - 1
    prev_row = jnp.where(at_start, 0.0, before)
    next_row = jnp.where(at_end, 0.0, v_new[0:1, :].astype(F32))
    row = lax.broadcasted_iota(I32, (tm, 1), 0)
    v_dn = jnp.where(row == 0, prev_row, pltpu.roll(v, 1, 0))
    v_up = jnp.where(row == tm - 1, next_row, pltpu.roll(v, tm - 1, 0))
    cw = cw_ref[...]
    u = cw[0:1, :] * v_dn + cw[1:2, :] * v + cw[2:3, :] * v_up
    z_ref[...] = (bg_old.astype(F32) * u).astype(BF16)

    last_s[0:1, :] = v[tm - 1:tm, :]
    v_s[...] = v_new
    bg_s[...] = bg.astype(BF16)


def _mix_a(h, g, w_in, conv_w, seq):
    n, d = h.shape
    tm = _tile(seq, 512)
    nt = n // tm
    return pl.pallas_call(
        functools.partial(_mix_a_kernel, tiles_per_seq=seq // tm),
        grid=(nt + 1,),
        in_specs=[pl.BlockSpec((tm, d), lambda i: (jnp.minimum(i, nt - 1), 0)),
                  pl.BlockSpec((1, d), lambda i: (0, 0)),
                  pl.BlockSpec((d, 3 * d), lambda i: (0, 0), pipeline_mode=pl.Buffered(1)),
                  pl.BlockSpec(conv_w.shape, lambda i: (0, 0))],
        out_specs=pl.BlockSpec((tm, d), lambda i: (jnp.maximum(i - 1, 0), 0)),
        out_shape=jax.ShapeDtypeStruct((n, d), BF16),
        scratch_shapes=[pltpu.VMEM((tm, d), BF16), pltpu.VMEM((tm, d), BF16),
                        pltpu.VMEM((SUBLANES, d), F32)],
        compiler_params=_cparams("arbitrary"),
        name="mix_a",
    )(h, g, w_in, conv_w)


def _mix_b_in_kernel(h_ref, hp_ref, hn_ref, g_ref, w_ref, yb_ref, u_ref, *, steps_per_seq, tt):
    i = pl.program_id(0)
    r = yb_ref.shape[1]
    tiles = h_ref.shape[0] // tt
    g = g_ref[...]
    xn = _rms(h_ref[...], g).astype(BF16)
    yb_ref[...] = jax.nn.gelu(_dot(xn, w_ref[:, 0:r])).astype(BF16)
    tm = h_ref.shape[0]
    xh = _rms(jnp.concatenate([hp_ref[...], hn_ref[...]], axis=0), g).astype(BF16)
    u_all = _dot(jnp.concatenate([xn, xh], axis=0), w_ref[:, r:2 * r])
    u = u_all[0:tm]
    uh = u_all[tm:tm + 2 * SUBLANES]
    at_start = (i % steps_per_seq) == 0
    at_end = (i % steps_per_seq) == steps_per_seq - 1
    prev2 = jnp.where(at_start, 0.0, uh[SUBLANES - 2:SUBLANES, :])
    next1 = jnp.where(at_end, 0.0, uh[SUBLANES:SUBLANES + 1, :])
    row = lax.broadcasted_iota(I32, (SUBLANES, 1), 0)
    for k in range(tiles):
        p2 = prev2 if k == 0 else u[k * tt - 2:k * tt]
        n1 = next1 if k == tiles - 1 else u[(k + 1) * tt:(k + 1) * tt + 1]
        pad = jnp.where(row == 0, p2[0:1], jnp.where(row == 1, p2[1:2], jnp.where(row == 2, n1, 0.0)))
        for s in range(r // LANES):
            lanes = slice(s * LANES, (s + 1) * LANES)
            u_ref[k, s, 0:tt, :] = u[k * tt:(k + 1) * tt, lanes]
            u_ref[k, s, tt:tt + SUBLANES, :] = pad[:, lanes]


def _mix_b_in(h, g, w_in, batch, seq, tt):
    n, d = h.shape
    r = w_in.shape[1] // 2
    tm = _tile(seq, 1024)
    steps_per_seq = seq // tm
    tiles = tm // tt
    hb = tm // SUBLANES
    nhalo = n // SUBLANES
    return pl.pallas_call(
        functools.partial(_mix_b_in_kernel, steps_per_seq=steps_per_seq, tt=tt),
        grid=(n // tm,),
        in_specs=[pl.BlockSpec((tm, d), lambda i: (i, 0)),
                  pl.BlockSpec((SUBLANES, d), lambda i: (jnp.maximum(i * hb - 1, 0), 0)),
                  pl.BlockSpec((SUBLANES, d), lambda i: (jnp.minimum((i + 1) * hb, nhalo - 1), 0)),
                  pl.BlockSpec((1, d), lambda i: (0, 0)),
                  pl.BlockSpec((d, 2 * r), lambda i: (0, 0), pipeline_mode=pl.Buffered(1))],
        out_specs=[pl.BlockSpec((tm, r), lambda i: (i, 0)),
                   pl.BlockSpec((tiles, r // LANES, None, tt + SUBLANES, LANES),
                                lambda i: (i % steps_per_seq, 0, i // steps_per_seq, 0, 0))],
        out_shape=[jax.ShapeDtypeStruct((n, r), BF16),
                   jax.ShapeDtypeStruct((seq // tt, r // LANES, batch, tt + SUBLANES, LANES), F32)],
        compiler_params=_cparams("parallel"),
        name="mix_b_in",
    )(h, h, h, g, w_in)


def _rglru_gates(u, wr_ref, br_ref, wi_ref, bi_ref, lam_ref):
    ub = u.astype(BF16)
    t_r = jnp.tanh(_dot(ub, wr_ref[...]) + 0.5 * br_ref[...])
    t_i = jnp.tanh(_dot(ub, wi_ref[...]) + 0.5 * bi_ref[...])
    c = (-0.25 * RG_C) * jax.nn.softplus(-lam_ref[...])
    th = jnp.tanh(c + c * t_r)
    q = 1.0 / (1.0 - th)
    nth = -th
    root = jnp.where(nth > 0.0, nth * lax.rsqrt(nth), 0.0)
    return (1.0 + th) * q, (q * root) * (u + u * t_i)


def _rglru_fwd_kernel(u_ref, cw_ref, cb_ref, wr_ref, br_ref, wi_ref, bi_ref, lam_ref,
                      hf_ref, uc_ref, x_s, carry_s, *, tt):
    nb = SUBLANES
    ttp = tt + SUBLANES
    rows = tt * nb
    slabs = u_ref.shape[0]

    @pl.when(pl.program_id(1) == 0)
    def _():
        carry_s[...] = jnp.zeros_like(carry_s)

    def gather_t(tl, dst):
        for s in range(slabs):
            x_s[pl.ds(dst, nb), s * LANES:(s + 1) * LANES] = u_ref[s, pl.ds(tl, nb, stride=ttp), :]

    for tl in range(tt):
        gather_t(tl, (tl + 2) * nb)
    gather_t(tt, 0)
    gather_t(tt + 1, nb)
    gather_t(tt + 2, (tt + 2) * nb)

    cw = cw_ref[...]
    u = (cw[0:1, :] * x_s[0:rows, :] + cw[1:2, :] * x_s[nb:nb + rows, :]
         + cw[2:3, :] * x_s[2 * nb:2 * nb + rows, :] + cw[3:4, :] * x_s[3 * nb:3 * nb + rows, :]
         + cb_ref[...])
    a, g = _rglru_gates(u, wr_ref, br_ref, wi_ref, bi_ref, lam_ref)
    h = carry_s[...]
    for tl in range(tt):
        r = slice(tl * nb, (tl + 1) * nb)
        back = slice((tt - 1 - tl) * nb, (tt - tl) * nb)
        h = a[r] * h + g[r]
        hf_ref[back, :] = h.astype(BF16)
        uc_ref[back, :] = u[r].astype(BF16)
    carry_s[...] = h


def _rglru_rev_kernel(uc_ref, wr_ref, br_ref, wi_ref, bi_ref, lam_ref, hf_ref, yb_ref,
                      z_ref, carry_s, nat_s, *, tt):
    nb = SUBLANES
    ttp = tt + SUBLANES
    slabs = nat_s.shape[0]

    @pl.when(pl.program_id(1) == 0)
    def _():
        carry_s[...] = jnp.zeros_like(carry_s)

    a, g = _rglru_gates(uc_ref[...].astype(F32), wr_ref, br_ref, wi_ref, bi_ref, lam_ref)
    h = carry_s[...]
    for k in range(tt):
        r = slice(k * nb, (k + 1) * nb)
        h = a[r] * h + g[r]
        hs = h + hf_ref[r, :].astype(F32)
        for s in range(slabs):
            nat_s[s, pl.ds(tt - 1 - k, nb, stride=ttp), :] = hs[:, s * LANES:(s + 1) * LANES]
    carry_s[...] = h
    for b in range(nb):
        for s in range(slabs):
            lanes = slice(s * LANES, (s + 1) * LANES)
            hb = nat_s[s, b * ttp:b * ttp + tt, :]
            z_ref[b, :, lanes] = (yb_ref[b, :, lanes].astype(F32) * hb).astype(BF16)


def _rglru(u_tiles, yb, conv_w, conv_b, w_r, b_r, w_i, b_i, lam, seq, tt):
    nt, nslab, batch, ttp, _ = u_tiles.shape
    r = nslab * LANES
    assert batch == SUBLANES, "time-major tile layout puts the batch on the 8 sublanes"
    nh, bk = w_r.shape[1], w_r.shape[2]
    slabs = bk // LANES
    kw = conv_w.shape[0]
    rows = tt * batch
    gates = [w_r, b_r.reshape(2, 1, r), w_i, b_i.reshape(2, 1, r), lam.reshape(2, 1, r)]

    def gate_specs(d):
        vec = pl.BlockSpec((None, 1, bk), lambda h, jj: (d, 0, h))
        mat = pl.BlockSpec((None, None, bk, bk), lambda h, jj: (d, h, 0, 0))
        return [mat, vec, mat, vec, vec]

    tm_spec = pl.BlockSpec((rows, bk), lambda h, jj: (jj, h))
    tm_rev = pl.BlockSpec((rows, bk), lambda h, jj: (nt - 1 - jj, h))
    tm_shape = jax.ShapeDtypeStruct((seq * batch, r), BF16)
    state = [pltpu.VMEM((batch, bk), F32)]

    hf, uc = pl.pallas_call(
        functools.partial(_rglru_fwd_kernel, tt=tt),
        grid=(nh, nt),
        in_specs=[pl.BlockSpec((None, slabs, batch * ttp, LANES), lambda h, jj: (jj, h, 0, 0)),
                  pl.BlockSpec((kw, bk), lambda h, jj: (0, h)),
                  pl.BlockSpec((1, bk), lambda h, jj: (0, h))] + gate_specs(0),
        out_specs=[tm_spec, tm_spec],
        out_shape=[tm_shape, tm_shape],
        scratch_shapes=[pltpu.VMEM(((tt + 3) * batch, bk), F32)] + state,
        compiler_params=_cparams("parallel", "arbitrary"),
        name="rglru_fwd",
    )(u_tiles.reshape(nt, nslab, batch * ttp, LANES), conv_w, conv_b, *gates)

    nat_spec = pl.BlockSpec((batch, tt, bk), lambda h, jj: (0, nt - 1 - jj, h))
    z = pl.pallas_call(
        functools.partial(_rglru_rev_kernel, tt=tt),
        grid=(nh, nt),
        in_specs=[tm_rev] + gate_specs(1) + [tm_rev, nat_spec],
        out_specs=nat_spec,
        out_shape=jax.ShapeDtypeStruct((batch, seq, r), BF16),
        scratch_shapes=state + [pltpu.VMEM((slabs, batch * ttp, LANES), F32)],
        compiler_params=_cparams("parallel", "arbitrary"),
        name="rglru_rev",
    )(uc, *gates, hf, yb.reshape(batch, seq, r))
    return z.reshape(-1, r)


def _route(logits):
    lane = lax.broadcasted_iota(I32, logits.shape, 1).astype(F32)
    neg = jnp.float32(-jnp.inf)
    nolane = jnp.float32(LANES)
    gmask = lane < N_GROUPS
    gmax = jnp.max(jnp.where(gmask, logits, neg), axis=1, keepdims=True)
    eg = jnp.where(gmask, jnp.exp(logits - gmax), 0.0)
    gsum = jnp.sum(eg, axis=1, keepdims=True)
    pg = eg / gsum
    pg_top = 1.0 / gsum
    g_top = jnp.min(jnp.where(gmask & (pg == pg_top), lane, nolane), axis=1, keepdims=True)
    lo = EXPERT_LANE0 + g_top * EXPERTS_PER_GROUP
    emask = (lane >= lo) & (lane < lo + EXPERTS_PER_GROUP)
    emax = jnp.max(jnp.where(emask, logits, neg), axis=1, keepdims=True)
    ee = jnp.where(emask, jnp.exp(logits - emax), 0.0)
    esum = jnp.sum(ee, axis=1, keepdims=True)
    pe = jnp.where(emask, ee / esum, -1.0)
    p1 = 1.0 / esum
    i1 = jnp.min(jnp.where(pe == p1, lane, nolane), axis=1, keepdims=True)
    pe2 = jnp.where(lane == i1, -1.0, pe)
    p2 = jnp.max(pe2, axis=1, keepdims=True)
    i2 = jnp.min(jnp.where(pe2 == p2, lane, nolane), axis=1, keepdims=True)
    psum = p1 + p2
    return i1, i2, pg_top * (p1 / psum), pg_top * (p2 / psum)


def _post_mix_kernel(z_ref, wo_ref, h_ref, g_ref, wr_ref, br_ref,
                     h1_ref, xn_ref, meta_ref, meta_t_ref, cnt_ref, carry_ref, logits_s):
    i = pl.program_id(0)
    tm = z_ref.shape[0]

    @pl.when(i == 0)
    def _():
        logits_s[...] = jnp.zeros_like(logits_s)

    @pl.when(i <= 1)
    def _():
        carry_ref[...] = jnp.zeros_like(carry_ref)

    logits = logits_s[...]

    h1 = h_ref[...] + _dot(z_ref[...], wo_ref[...])
    h1_ref[...] = h1
    xn = _rms(h1, g_ref[...])
    xn_ref[...] = _pack_bf16_pairs(xn)
    logits_s[...] = _dot(xn.astype(BF16), wr_ref[...].astype(BF16)) + br_ref[...]

    i1, i2, w0, w1 = _route(logits)

    lane = lax.broadcasted_iota(I32, (tm, LANES), 1).astype(F32)
    oh0 = lane == i1
    oh1 = lane == i2
    both = (oh0 | oh1).astype(BF16)
    ri = lax.broadcasted_iota(I32, (tm, tm), 0)
    ci = lax.broadcasted_iota(I32, (tm, tm), 1)
    before = (ri > ci).astype(BF16)
    cnt_before = _dot(before, both) + carry_ref[...]
    rank0 = jnp.sum(jnp.where(oh0, cnt_before, 0.0), axis=1, keepdims=True)
    rank1 = jnp.sum(jnp.where(oh1, cnt_before, 0.0), axis=1, keepdims=True)
    carry = carry_ref[...] + jnp.sum(both.astype(F32), axis=0, keepdims=True)
    carry_ref[...] = carry
    cnt_ref[...] = carry

    e0 = i1 - EXPERT_LANE0
    e1 = i2 - EXPERT_LANE0
    meta = jnp.zeros((tm, LANES), F32)
    for ln, val in ((M_E0, e0), (M_E1, e1), (M_R0, rank0), (M_R1, rank1), (M_W0, w0), (M_W1, w1)):
        meta = jnp.where(lane == ln, val, meta)
    meta_ref[...] = meta
    meta_t_ref[...] = meta.T[0:SUBLANES, :]


def _post_mix(z, w_out, h, g, w_router, b_router):
    n, d = h.shape
    k = z.shape[1]
    tm = _tile(n, 512)
    nt = n // tm

    def proj(i):
        return (jnp.minimum(i, nt - 1), 0)

    def routed(i):
        return (jnp.maximum(i - 1, 0), 0)

    return pl.pallas_call(
        _post_mix_kernel,
        grid=(nt + 1,),
        in_specs=[pl.BlockSpec((tm, k), proj),
                  pl.BlockSpec((k, d), lambda i: (0, 0)),
                  pl.BlockSpec((tm, d), proj),
                  pl.BlockSpec((1, d), lambda i: (0, 0)),
                  pl.BlockSpec((d, LANES), lambda i: (0, 0)),
                  pl.BlockSpec((1, LANES), lambda i: (0, 0))],
        out_specs=[pl.BlockSpec((tm, d), proj),
                   pl.BlockSpec((tm, d // 2), proj),
                   pl.BlockSpec((tm, LANES), routed),
                   pl.BlockSpec((SUBLANES, tm), lambda i: (0, jnp.maximum(i - 1, 0))),
                   pl.BlockSpec((1, LANES), lambda i: (0, 0))],
        out_shape=[jax.ShapeDtypeStruct((n, d), F32), jax.ShapeDtypeStruct((n, d // 2), U32),
                   jax.ShapeDtypeStruct((n, LANES), F32), jax.ShapeDtypeStruct((SUBLANES, n), F32),
                   jax.ShapeDtypeStruct((1, LANES), F32)],
        scratch_shapes=[pltpu.VMEM((1, LANES), F32), pltpu.VMEM((tm, LANES), F32)],
        compiler_params=_cparams("arbitrary"),
        name="post_mix_router",
    )(z, w_out, h, g, w_router, b_router)


def _plan_kernel(cnt_ref, meta_t_ref, dest_ref, estart_ref, ecount_ref, nblk_ref, pstart_ref):
    def per_expert(e, acc):
        nb_e = (cnt_ref[e] + (ROUTE_ROWS - 1)) // ROUTE_ROWS
        pstart_ref[e] = acc * ROUTE_ROWS
        estart_ref[e] = acc
        ecount_ref[e] = nb_e
        return acc + nb_e

    nblk_ref[0] = lax.fori_loop(0, N_EXPERTS, per_expert, 0)

    e0 = meta_t_ref[M_E0:M_E0 + 1, :]
    e1 = meta_t_ref[M_E1:M_E1 + 1, :]
    d0 = meta_t_ref[M_R0:M_R0 + 1, :]
    d1 = meta_t_ref[M_R1:M_R1 + 1, :]
    for e in range(N_EXPERTS):
        ps = pstart_ref[e].astype(F32)
        d0 = d0 + jnp.where(e0 == e, ps, 0.0)
        d1 = d1 + jnp.where(e1 == e, ps, 0.0)
    dest_ref[...] = jnp.zeros_like(dest_ref)
    dest_ref[0:1, :] = d0.astype(I32)
    dest_ref[1:2, :] = d1.astype(I32)


def _plan(cnt_i32, meta_t):
    n = meta_t.shape[1]
    smem = pl.BlockSpec(memory_space=pltpu.SMEM)
    return pl.pallas_call(
        _plan_kernel,
        in_specs=[smem, pl.BlockSpec((SUBLANES, n), lambda: (0, 0))],
        out_specs=[pl.BlockSpec((SUBLANES, n), lambda: (0, 0)), smem, smem, smem],
        out_shape=[jax.ShapeDtypeStruct((SUBLANES, n), I32),
                   jax.ShapeDtypeStruct((N_EXPERTS,), I32),
                   jax.ShapeDtypeStruct((N_EXPERTS,), I32),
                   jax.ShapeDtypeStruct((1,), I32)],
        scratch_shapes=[pltpu.SMEM((N_EXPERTS,), I32)],
        name="dispatch_plan",
    )(cnt_i32, meta_t)


def _sc_mesh():
    return plsc.VectorSubcoreMesh(core_axis_name="c", subcore_axis_name="s",
                                  num_cores=SC_CORES, num_subcores=SC_SUBCORES)


def _sc_scatter_rows(x, idx0, idx1, p_rows):
    m, k = x.shape
    per_w = m // (SC_CORES * SC_SUBCORES)
    assert per_w % SC_CHUNK == 0 and per_w * SC_CORES * SC_SUBCORES == m

    nchunk = per_w // SC_CHUNK
    buf = [pltpu.VMEM((SC_CHUNK,), I32), pltpu.VMEM((SC_CHUNK,), I32),
           pltpu.VMEM((SC_CHUNK, k), x.dtype),
           pltpu.SemaphoreType.DMA, pltpu.SemaphoreType.DMA, pltpu.SemaphoreType.DMA]

    @functools.partial(
        pl.kernel, mesh=_sc_mesh(), out_type=jax.ShapeDtypeStruct((p_rows, k), x.dtype),
        scratch_types=buf + buf, name="sc_scatter_rows")
    def scatter(x_hbm, i0_hbm, i1_hbm, out_hbm, *scratch):
        base = (lax.axis_index("s") * SC_CORES + lax.axis_index("c")) * per_w
        slots = (scratch[:6], scratch[6:])

        def loads(c):
            i0_v, i1_v, rows_v, lsem, _, _ = slots[c % 2]
            off = pl.multiple_of(base + c * SC_CHUNK, SC_CHUNK)
            return [pltpu.make_async_copy(i0_hbm.at[pl.ds(off, SC_CHUNK)], i0_v, lsem),
                    pltpu.make_async_copy(i1_hbm.at[pl.ds(off, SC_CHUNK)], i1_v, lsem),
                    pltpu.make_async_copy(x_hbm.at[pl.ds(off, SC_CHUNK)], rows_v, lsem)]

        def puts(c):
            i0_v, i1_v, rows_v, _, psem0, psem1 = slots[c % 2]
            return [pltpu.make_async_copy(rows_v, out_hbm.at[i0_v], psem0),
                    pltpu.make_async_copy(rows_v, out_hbm.at[i1_v], psem1)]

        for d in loads(0):
            d.start()
        for c in range(nchunk):
            for d in loads(c):
                d.wait()
            for d in puts(c):
                d.start()
            if c >= 1:
                for d in puts(c - 1):
                    d.wait()
            if c + 1 < nchunk:
                for d in loads(c + 1):
                    d.start()
        for d in puts(nchunk - 1):
            d.wait()

    return scatter(x, idx0, idx1)


def _expert_kernel(es_ref, ec_ref, nblk_ref, cnt_ref, xs_ref, wg_ref, wu_ref, wd_ref, ys_ref,
                   x_s, y_s, wg_s, wu_s, wd_s, sem):
    e = pl.program_id(0)
    nb = ec_ref[e]
    b0 = es_ref[e]
    k = x_s.shape[2]
    nb_max = ys_ref.shape[0] // ROUTE_ROWS

    def rows(b):
        return pl.ds(pl.multiple_of(b * ROUTE_ROWS, ROUTE_ROWS), ROUTE_ROWS)

    nblk = nblk_ref[0]

    def copy_in(b):
        slot = lax.rem(b, IN_SLOTS)
        return pltpu.make_async_copy(xs_ref.at[rows(b)], x_s.at[slot], sem.at[0, slot])

    def copy_out(b):
        slot = lax.rem(b, OUT_SLOTS)
        return pltpu.make_async_copy(y_s.at[slot], ys_ref.at[rows(b)], sem.at[1, slot])

    @pl.when(e == 0)
    def _():
        for b in range(IN_AHEAD):
            @pl.when(b < nblk)
            def _(b=b):
                copy_in(b).start()

    @pl.when(nb > 0)
    def _():
        wg_s[...] = wg_ref[...].astype(BF16)
        wu_s[...] = wu_ref[...].astype(BF16)
        wd_s[...] = wd_ref[...].astype(BF16)

        def mlp(b, j, width):
            row = lax.broadcasted_iota(I32, (ROUTE_ROWS, 1), 0)
            xp = jnp.concatenate(
                [jnp.where(row < cnt_ref[e] - (j + i) * ROUTE_ROWS,
                           x_s[lax.rem(b + i, IN_SLOTS)], jnp.uint32(0)) for i in range(width)], axis=0)
            x_hi, x_lo = _unpack_bf16_pairs(xp)
            x_hi = x_hi.astype(BF16)
            x_lo = x_lo.astype(BF16)
            g = _dot(x_hi, wg_s[0:k, :]) + _dot(x_lo, wg_s[k:2 * k, :])
            u = _dot(x_hi, wu_s[0:k, :]) + _dot(x_lo, wu_s[k:2 * k, :])
            hb = (jax.nn.silu(g) * u).astype(BF16)
            y = _pack_bf16_pairs(_dot(hb, wd_s[...]))
            return [y[i * ROUTE_ROWS:(i + 1) * ROUTE_ROWS] for i in range(width)]

        def unit(j, width):
            b = b0 + j
            for i in range(width):
                copy_in(b + i).wait()
            for i in range(width):
                @pl.when(b + i + IN_AHEAD < nblk)
                def _(i=i):
                    copy_in(b + i + IN_AHEAD).start()
            ys = mlp(b, j, width)
            for i in range(width):
                @pl.when(b + i >= OUT_SLOTS)
                def _(i=i):
                    copy_out(b + i - OUT_SLOTS).wait()
            for i in range(width):
                y_s[lax.rem(b + i, OUT_SLOTS)] = ys[i]
                copy_out(b + i).start()

        def pair(jj, c):
            unit(2 * jj, 2)
            return c

        lax.fori_loop(0, nb // 2, pair, 0)

        @pl.when(nb % 2 == 1)
        def _():
            unit(nb - 1, 1)

    @pl.when(e == pl.num_programs(0) - 1)
    def _():
        for back in range(OUT_SLOTS, 0, -1):
            @pl.when(nblk >= back)
            def _(back=back):
                copy_out(nblk - back).wait()

    @pl.when(e == pl.num_programs(0) - 1)
    def _():
        y_s[0] = jnp.zeros_like(y_s[0])

        def zero(b):
            return pltpu.make_async_copy(y_s.at[0], ys_ref.at[rows(b)], sem.at[1, 0])

        def start_zero(b, c):
            zero(b).start()
            return c

        def wait_zero(b, c):
            zero(b).wait()
            return c

        lax.fori_loop(nblk_ref[0], nb_max, start_zero, 0)
        lax.fori_loop(nblk_ref[0], nb_max, wait_zero, 0)


def _experts(estart, ecount, nblk, cnt, xs, w_gate, w_up, w_down, layer):
    p, k = xs.shape
    d, f = w_gate.shape[2], w_gate.shape[3]

    def w_map(e, es, ec, nb, cn):
        return (layer, e, 0, 0)

    grid_spec = pltpu.PrefetchScalarGridSpec(
        num_scalar_prefetch=4,
        grid=(N_EXPERTS,),
        in_specs=[pl.BlockSpec(memory_space=pl.ANY),
                  pl.BlockSpec((None, None, d, f), w_map),
                  pl.BlockSpec((None, None, d, f), w_map),
                  pl.BlockSpec((None, None, f, d), w_map)],
        out_specs=pl.BlockSpec(memory_space=pl.ANY),
        scratch_shapes=[pltpu.VMEM((IN_SLOTS, ROUTE_ROWS, k), U32),
                        pltpu.VMEM((OUT_SLOTS, ROUTE_ROWS, k), U32),
                        pltpu.VMEM((d, f), BF16), pltpu.VMEM((d, f), BF16), pltpu.VMEM((f, d), BF16),
                        pltpu.SemaphoreType.DMA((2, max(IN_SLOTS, OUT_SLOTS)))],
    )
    return pl.pallas_call(
        _expert_kernel,
        grid_spec=grid_spec,
        out_shape=jax.ShapeDtypeStruct((p, k), U32),
        compiler_params=_cparams("arbitrary"),
        name="expert_mlp",
    )(estart, ecount, nblk, cnt, xs, w_gate, w_up, w_down)


def _sc_gather_pair(table, idx0, idx1, start, count):
    k = table.shape[1]
    per_w = count // (SC_CORES * SC_SUBCORES)
    assert per_w % SC_CHUNK == 0 and per_w * SC_CORES * SC_SUBCORES == count
    assert start % SC_CHUNK == 0
    rows = jax.ShapeDtypeStruct((count, k), table.dtype)

    @functools.partial(
        pl.kernel, mesh=_sc_mesh(), out_type=(rows, rows),
        scratch_types=[pltpu.VMEM((SC_CHUNK,), I32), pltpu.VMEM((SC_CHUNK,), I32),
                       pltpu.VMEM((SC_CHUNK, k), table.dtype), pltpu.VMEM((SC_CHUNK, k), table.dtype),
                       pltpu.SemaphoreType.DMA, pltpu.SemaphoreType.DMA,
                       pltpu.SemaphoreType.DMA, pltpu.SemaphoreType.DMA],
        name="sc_gather_pair")
    def gather(table_hbm, i0_hbm, i1_hbm, o0_hbm, o1_hbm, i0_v, i1_v, r0_v, r1_v,
               sem0, sem1, sem2, sem3):
        base = (lax.axis_index("s") * SC_CORES + lax.axis_index("c")) * per_w

        @pl.loop(0, per_w // SC_CHUNK)
        def _(c):
            dst = pl.multiple_of(base + c * SC_CHUNK, SC_CHUNK)
            src = pl.multiple_of(start + dst, SC_CHUNK)
            pltpu.sync_copy(i0_hbm.at[pl.ds(src, SC_CHUNK)], i0_v)
            pltpu.sync_copy(i1_hbm.at[pl.ds(src, SC_CHUNK)], i1_v)
            get0 = pltpu.async_copy(table_hbm.at[i0_v], r0_v, sem0)
            get1 = pltpu.async_copy(table_hbm.at[i1_v], r1_v, sem1)
            get0.wait()
            put0 = pltpu.async_copy(r0_v, o0_hbm.at[pl.ds(dst, SC_CHUNK)], sem2)
            get1.wait()
            put1 = pltpu.async_copy(r1_v, o1_hbm.at[pl.ds(dst, SC_CHUNK)], sem3)
            put0.wait()
            put1.wait()

    return gather(table, idx0, idx1)


def _combine_kernel(*refs, final):
    (ya_ref, yb_ref, h1_ref, meta_ref, p_ref, gp_ref, wpg_ref, bpg_ref, wp_ref,
     gf_ref) = refs[:10]
    out_ref = refs[-1]
    meta = meta_ref[...]
    w0 = meta[:, M_W0:M_W0 + 1]
    w1 = meta[:, M_W1:M_W1 + 1]
    ya_hi, ya_lo = _unpack_bf16_pairs(ya_ref[...])
    yb_hi, yb_lo = _unpack_bf16_pairs(yb_ref[...])
    y = jnp.concatenate([ya_hi * w0 + yb_hi * w1, ya_lo * w0 + yb_lo * w1], axis=1)
    h2 = h1_ref[...] + y
    xn = _rms(h2, gp_ref[...]).astype(BF16)
    gate = jax.nn.sigmoid(_dot(xn, wpg_ref[...]) + bpg_ref[...])
    pp = _dot(p_ref[...].astype(BF16), wp_ref[...])
    h3 = h2 + gate * pp
    if final:
        h3 = _rms(h3, gf_ref[...])
    out_ref[...] = h3


def _combine(dest, ys, h1, meta, p, layer, g_ple, w_ple_gate, b_ple_gate, w_ple, g_final, final):
    n, d = h1.shape
    pd = p.shape[2]
    k = ys.shape[1]
    n_part = n // COMBINE_PARTS
    tmc = _tile(n_part, 512)
    tiles = n_part // tmc
    vec = pl.BlockSpec((1, d), lambda i: (0, 0))
    out = None
    for q in range(COMBINE_PARTS):
        ya, yb = _sc_gather_pair(ys, dest[0], dest[1], q * n_part, n_part)

        def tok(i, q=q):
            return (q * tiles + i, 0)

        in_specs = [pl.BlockSpec((tmc, k), lambda i: (i, 0)),
                    pl.BlockSpec((tmc, k), lambda i: (i, 0)),
                    pl.BlockSpec((tmc, d), tok),
                    pl.BlockSpec((tmc, LANES), tok),
                    pl.BlockSpec((None, tmc, pd), lambda i, q=q: (layer, q * tiles + i, 0)),
                    vec,
                    pl.BlockSpec((d, d), lambda i: (0, 0)),
                    vec,
                    pl.BlockSpec((pd, d), lambda i: (0, 0)),
                    vec]
        args = [ya, yb, h1, meta, p, g_ple, w_ple_gate, b_ple_gate, w_ple, g_final]
        aliases = {}
        if out is not None:
            in_specs.append(pl.BlockSpec(memory_space=pl.ANY))
            args.append(out)
            aliases = {len(args) - 1: 0}
        out = pl.pallas_call(
            functools.partial(_combine_kernel, final=final),
            grid=(tiles,),
            in_specs=in_specs,
            out_specs=pl.BlockSpec((tmc, d), tok),
            out_shape=jax.ShapeDtypeStruct((n, d), F32),
            input_output_aliases=aliases,
            compiler_params=_cparams("parallel"),
            name="combine_ple",
        )(*args)
    return out


def _row(v):
    return v.reshape(1, -1)


def _moe_and_ple(z, w_out, h, i, final, norm_ffn, w_router_group, b_router_group, w_router_expert,
                 b_router_expert, w_exp_gate, w_exp_up, w_exp_down, norm_ple, w_ple, w_ple_gate,
                 b_ple_gate, norm_final, p):
    n, d = h.shape
    pad = LANES - N_GROUPS - N_EXPERTS
    w_router = jnp.concatenate(
        [w_router_group[i], w_router_expert[i], jnp.zeros((d, pad), F32)], axis=1)
    b_router = jnp.concatenate(
        [b_router_group[i], b_router_expert[i], jnp.zeros((pad,), F32)]).reshape(1, LANES)
    h1, xn, meta, meta_t, cnt = _post_mix(z, w_out, h, _row(norm_ffn[i]), w_router, b_router)
    cnt_i32 = cnt[0, EXPERT_LANE0:EXPERT_LANE0 + N_EXPERTS].astype(I32)
    nb_max = (2 * n) // ROUTE_ROWS + N_EXPERTS
    dest, estart, ecount, nblk = _plan(cnt_i32, meta_t)
    xs = _sc_scatter_rows(xn, dest[0], dest[1], nb_max * ROUTE_ROWS)
    ys = _experts(estart, ecount, nblk, cnt_i32, xs, w_exp_gate, w_exp_up, w_exp_down, i)
    return _combine(dest, ys, h1, meta, p.reshape(p.shape[0], n, -1), i, _row(norm_ple[i]),
                    w_ple_gate[i].astype(BF16), _row(b_ple_gate[i]), w_ple[i].astype(BF16),
                    _row(norm_final), final)


def kernel(x, p, norm_mix, w_in_a, conv_a, w_out_a, w_in_b, conv_b, conv_bias_b, w_rgate_b, b_rgate_b, w_igate_b, b_igate_b, lam_b, w_out_b, norm_ffn, w_router_group, b_router_group, w_router_expert, b_router_expert, w_exp_gate, w_exp_up, w_exp_down, norm_ple, w_ple, w_ple_gate, b_ple_gate, norm_final):
    batch, seq, d = x.shape
    depth = p.shape[0]
    n = batch * seq
    h = x.reshape(n, d)
    for i in range(depth):
        j = i // 2
        if i % 2 == 0:
            z = _mix_a(h, _row(norm_mix[i]), w_in_a[j].astype(BF16), conv_a[j], seq)
            w_out = w_out_a[j]
        else:
            tt = _tile(seq, 256)
            yb, u_tiles = _mix_b_in(h, _row(norm_mix[i]), w_in_b[j].astype(BF16), batch, seq, tt)
            z = _rglru(u_tiles, yb, conv_b[j], _row(conv_bias_b[j]),
                       (0.5 * w_rgate_b[j]).astype(BF16), b_rgate_b[j],
                       (0.5 * w_igate_b[j]).astype(BF16),
                       b_igate_b[j], lam_b[j], seq, tt)
            w_out = w_out_b[j]
        h = _moe_and_ple(z, w_out.astype(BF16), h, i, i == depth - 1, norm_ffn, w_router_group,
                         b_router_group, w_router_expert, b_router_expert, w_exp_gate, w_exp_up,
                         w_exp_down, norm_ple, w_ple, w_ple_gate, b_ple_gate, norm_final, p)
    return h.reshape(batch, seq, d)
```

```python
import functools

import jax
import jax.numpy as jnp
from jax import lax
from jax.experimental import pallas as pl
from jax.experimental.pallas import tpu as pltpu
from jax.experimental.pallas import tpu_sc as plsc

F32 = jnp.float32
BF16 = jnp.bfloat16
I32 = jnp.int32
U32 = jnp.uint32

EPS = 1e-6
N_GROUPS = 4
EXPERTS_PER_GROUP = 8
N_EXPERTS = N_GROUPS * EXPERTS_PER_GROUP
RG_C = 8.0

LANES = 128
SUBLANES = 8
EXPERT_LANE0 = N_GROUPS
ROUTE_ROWS = 256
IN_AHEAD = 4
IN_SLOTS = IN_AHEAD + 2
OUT_SLOTS = 4
SC_CORES = 2
SC_SUBCORES = 16
SC_CHUNK = 64
COMBINE_PARTS = 4
M_E0, M_E1, M_R0, M_R1, M_W0, M_W1 = 0, 1, 2, 3, 4, 5
VMEM_LIMIT = 48 * 1024 * 1024


def _cparams(*sem):
    return pltpu.CompilerParams(dimension_semantics=sem, vmem_limit_bytes=VMEM_LIMIT)


def _rms(x, g):
    return x * lax.rsqrt(jnp.mean(x * x, axis=-1, keepdims=True) + EPS) * g


def _dot(a, b):
    return jnp.dot(a, b, preferred_element_type=F32)


def _pack_bf16_pairs(x):
    k = x.shape[1] // 2
    hi = lax.bitcast_convert_type(x[:, :k].astype(BF16).astype(F32), U32)
    lo = lax.bitcast_convert_type(x[:, k:].astype(BF16).astype(F32), U32)
    return hi | (lo >> 16)


def _unpack_bf16_pairs(p):
    hi = lax.bitcast_convert_type(p & jnp.uint32(0xFFFF0000), F32)
    lo = lax.bitcast_convert_type(p << 16, F32)
    return hi, lo


def _tile(n, want):
    t = min(n, want)
    assert n % t == 0, (n, want)
    return t


def _mix_a_kernel(h_ref, g_ref, w_ref, cw_ref, z_ref, v_s, bg_s, last_s, *, tiles_per_seq):
    i = pl.program_id(0)
    tm, d = h_ref.shape

    @pl.when(i == 0)
    def _():
        v_s[...] = jnp.zeros_like(v_s)
        bg_s[...] = jnp.zeros_like(bg_s)
        last_s[...] = jnp.zeros_like(last_s)

    v = v_s[...].astype(F32)
    bg_old = bg_s[...]
    before = last_s[0:1, :]

    xn = _rms(h_ref[...], g_ref[...]).astype(BF16)
    cg = _dot(xn, w_ref[:, d:2 * d])
    hh = _dot(xn, w_ref[:, 2 * d:3 * d])
    v_new = (cg * hh).astype(BF16)
    bg = _dot(xn, w_ref[:, 0:d])

    old = i - 1
    at_start = (old % tiles_per_seq) == 0
    at_end = (old % tiles_per_seq) == tiles_per_seq - 1
    prev_row = jnp.where(at_start, 0.0, before)
    next_row = jnp.where(at_end, 0.0, v_new[0:1, :].astype(F32))
    row = lax.broadcasted_iota(I32, (tm, 1), 0)
    v_dn = jnp.where(row == 0, prev_row, pltpu.roll(v, 1, 0))
    v_up = jnp.where(row == tm - 1, next_row, pltpu.roll(v, tm - 1, 0))
    cw = cw_ref[...]
    u = cw[0:1, :] * v_dn + cw[1:2, :] * v + cw[2:3, :] * v_up
    z_ref[...] = (bg_old.astype(F32) * u).astype(BF16)

    last_s[0:1, :] = v[tm - 1:tm, :]
    v_s[...] = v_new
    bg_s[...] = bg.astype(BF16)


def _mix_a(h, g, w_in, conv_w, seq):
    n, d = h.shape
    tm = _tile(seq, 512)
    nt = n // tm
    return pl.pallas_call(
        functools.partial(_mix_a_kernel, tiles_per_seq=seq // tm),
        grid=(nt + 1,),
        in_specs=[pl.BlockSpec((tm, d), lambda i: (jnp.minimum(i, nt - 1), 0)),
                  pl.BlockSpec((1, d), lambda i: (0, 0)),
                  pl.BlockSpec((d, 3 * d), lambda i: (0, 0), pipeline_mode=pl.Buffered(1)),
                  pl.BlockSpec(conv_w.shape, lambda i: (0, 0))],
        out_specs=pl.BlockSpec((tm, d), lambda i: (jnp.maximum(i - 1, 0), 0)),
        out_shape=jax.ShapeDtypeStruct((n, d), BF16),
        scratch_shapes=[pltpu.VMEM((tm, d), BF16), pltpu.VMEM((tm, d), BF16),
                        pltpu.VMEM((SUBLANES, d), F32)],
        compiler_params=_cparams("arbitrary"),
        name="mix_a",
    )(h, g, w_in, conv_w)


def _mix_b_in_kernel(h_ref, hp_ref, hn_ref, g_ref, w_ref, yb_ref, u_ref, *, steps_per_seq, tt):
    i = pl.program_id(0)
    r = yb_ref.shape[1]
    tiles = h_ref.shape[0] // tt
    g = g_ref[...]
    xn = _rms(h_ref[...], g).astype(BF16)
    yb_ref[...] = jax.nn.gelu(_dot(xn, w_ref[:, 0:r])).astype(BF16)
    tm = h_ref.shape[0]
    xh = _rms(jnp.concatenate([hp_ref[...], hn_ref[...]], axis=0), g).astype(BF16)
    u_all = _dot(jnp.concatenate([xn, xh], axis=0), w_ref[:, r:2 * r])
    u = u_all[0:tm]
    uh = u_all[tm:tm + 2 * SUBLANES]
    at_start = (i % steps_per_seq) == 0
    at_end = (i % steps_per_seq) == steps_per_seq - 1
    prev2 = jnp.where(at_start, 0.0, uh[SUBLANES - 2:SUBLANES, :])
    next1 = jnp.where(at_end, 0.0, uh[SUBLANES:SUBLANES + 1, :])
    row = lax.broadcasted_iota(I32, (SUBLANES, 1), 0)
    for k in range(tiles):
        p2 = prev2 if k == 0 else u[k * tt - 2:k * tt]
        n1 = next1 if k == tiles - 1 else u[(k + 1) * tt:(k + 1) * tt + 1]
        pad = jnp.where(row == 0, p2[0:1], jnp.where(row == 1, p2[1:2], jnp.where(row == 2, n1, 0.0)))
        for s in range(r // LANES):
            lanes = slice(s * LANES, (s + 1) * LANES)
            u_ref[k, s, 0:tt, :] = u[k * tt:(k + 1) * tt, lanes]
            u_ref[k, s, tt:tt + SUBLANES, :] = pad[:, lanes]


def _mix_b_in(h, g, w_in, batch, seq, tt):
    n, d = h.shape
    r = w_in.shape[1] // 2
    tm = _tile(seq, 1024)
    steps_per_seq = seq // tm
    tiles = tm // tt
    hb = tm // SUBLANES
    nhalo = n // SUBLANES
    return pl.pallas_call(
        functools.partial(_mix_b_in_kernel, steps_per_seq=steps_per_seq, tt=tt),
        grid=(n // tm,),
        in_specs=[pl.BlockSpec((tm, d), lambda i: (i, 0)),
                  pl.BlockSpec((SUBLANES, d), lambda i: (jnp.maximum(i * hb - 1, 0), 0)),
                  pl.BlockSpec((SUBLANES, d), lambda i: (jnp.minimum((i + 1) * hb, nhalo - 1), 0)),
                  pl.BlockSpec((1, d), lambda i: (0, 0)),
                  pl.BlockSpec((d, 2 * r), lambda i: (0, 0), pipeline_mode=pl.Buffered(1))],
        out_specs=[pl.BlockSpec((tm, r), lambda i: (i, 0)),
                   pl.BlockSpec((tiles, r // LANES, None, tt + SUBLANES, LANES),
                                lambda i: (i % steps_per_seq, 0, i // steps_per_seq, 0, 0))],
        out_shape=[jax.ShapeDtypeStruct((n, r), BF16),
                   jax.ShapeDtypeStruct((seq // tt, r // LANES, batch, tt + SUBLANES, LANES), F32)],
        compiler_params=_cparams("parallel"),
        name="mix_b_in",
    )(h, h, h, g, w_in)


def _rglru_gates(u, wr_ref, br_ref, wi_ref, bi_ref, lam_ref):
    ub = u.astype(BF16)
    t_r = jnp.tanh(_dot(ub, wr_ref[...]) + 0.5 * br_ref[...])
    t_i = jnp.tanh(_dot(ub, wi_ref[...]) + 0.5 * bi_ref[...])
    c = (-0.25 * RG_C) * jax.nn.softplus(-lam_ref[...])
    th = jnp.tanh(c + c * t_r)
    q = 1.0 / (1.0 - th)
    nth = -th
    root = jnp.where(nth > 0.0, nth * lax.rsqrt(nth), 0.0)
    return (1.0 + th) * q, (q * root) * (u + u * t_i)


def _rglru_fwd_kernel(u_ref, cw_ref, cb_ref, wr_ref, br_ref, wi_ref, bi_ref, lam_ref,
                      hf_ref, uc_ref, x_s, carry_s, *, tt):
    nb = SUBLANES
    ttp = tt + SUBLANES
    rows = tt * nb
    slabs = u_ref.shape[0]

    @pl.when(pl.program_id(1) == 0)
    def _():
        carry_s[...] = jnp.zeros_like(carry_s)

    def gather_t(tl, dst):
        for s in range(slabs):
            x_s[pl.ds(dst, nb), s * LANES:(s + 1) * LANES] = u_ref[s, pl.ds(tl, nb, stride=ttp), :]

    for tl in range(tt):
        gather_t(tl, (tl + 2) * nb)
    gather_t(tt, 0)
    gather_t(tt + 1, nb)
    gather_t(tt + 2, (tt + 2) * nb)

    cw = cw_ref[...]
    u = (cw[0:1, :] * x_s[0:rows, :] + cw[1:2, :] * x_s[nb:nb + rows, :]
         + cw[2:3, :] * x_s[2 * nb:2 * nb + rows, :] + cw[3:4, :] * x_s[3 * nb:3 * nb + rows, :]
         + cb_ref[...])
    a, g = _rglru_gates(u, wr_ref, br_ref, wi_ref, bi_ref, lam_ref)
    h = carry_s[...]
    for tl in range(tt):
        r = slice(tl * nb, (tl + 1) * nb)
        back = slice((tt - 1 - tl) * nb, (tt - tl) * nb)
        h = a[r] * h + g[r]
        hf_ref[back, :] = h.astype(BF16)
        uc_ref[back, :] = u[r].astype(BF16)
    carry_s[...] = h


def _rglru_rev_kernel(uc_ref, wr_ref, br_ref, wi_ref, bi_ref, lam_ref, hf_ref, yb_ref,
                      z_ref, carry_s, nat_s, *, tt):
    nb = SUBLANES
    ttp = tt + SUBLANES
    slabs = nat_s.shape[0]

    @pl.when(pl.program_id(1) == 0)
    def _():
        carry_s[...] = jnp.zeros_like(carry_s)

    a, g = _rglru_gates(uc_ref[...].astype(F32), wr_ref, br_ref, wi_ref, bi_ref, lam_ref)
    h = carry_s[...]
    for k in range(tt):
        r = slice(k * nb, (k + 1) * nb)
        h = a[r] * h + g[r]
        hs = h + hf_ref[r, :].astype(F32)
        for s in range(slabs):
            nat_s[s, pl.ds(tt - 1 - k, nb, stride=ttp), :] = hs[:, s * LANES:(s + 1) * LANES]
    carry_s[...] = h
    for b in range(nb):
        for s in range(slabs):
            lanes = slice(s * LANES, (s + 1) * LANES)
            hb = nat_s[s, b * ttp:b * ttp + tt, :]
            z_ref[b, :, lanes] = (yb_ref[b, :, lanes].astype(F32) * hb).astype(BF16)


def _rglru(u_tiles, yb, conv_w, conv_b, w_r, b_r, w_i, b_i, lam, seq, tt):
    nt, nslab, batch, ttp, _ = u_tiles.shape
    r = nslab * LANES
    assert batch == SUBLANES, "time-major tile layout puts the batch on the 8 sublanes"
    nh, bk = w_r.shape[1], w_r.shape[2]
    slabs = bk // LANES
    kw = conv_w.shape[0]
    rows = tt * batch
    gates = [w_r, b_r.reshape(2, 1, r), w_i, b_i.reshape(2, 1, r), lam.reshape(2, 1, r)]

    def gate_specs(d):
        vec = pl.BlockSpec((None, 1, bk), lambda h, jj: (d, 0, h))
        mat = pl.BlockSpec((None, None, bk, bk), lambda h, jj: (d, h, 0, 0))
        return [mat, vec, mat, vec, vec]

    tm_spec = pl.BlockSpec((rows, bk), lambda h, jj: (jj, h))
    tm_rev = pl.BlockSpec((rows, bk), lambda h, jj: (nt - 1 - jj, h))
    tm_shape = jax.ShapeDtypeStruct((seq * batch, r), BF16)
    state = [pltpu.VMEM((batch, bk), F32)]

    hf, uc = pl.pallas_call(
        functools.partial(_rglru_fwd_kernel, tt=tt),
        grid=(nh, nt),
        in_specs=[pl.BlockSpec((None, slabs, batch * ttp, LANES), lambda h, jj: (jj, h, 0, 0)),
                  pl.BlockSpec((kw, bk), lambda h, jj: (0, h)),
                  pl.BlockSpec((1, bk), lambda h, jj: (0, h))] + gate_specs(0),
        out_specs=[tm_spec, tm_spec],
        out_shape=[tm_shape, tm_shape],
        scratch_shapes=[pltpu.VMEM(((tt + 3) * batch, bk), F32)] + state,
        compiler_params=_cparams("parallel", "arbitrary"),
        name="rglru_fwd",
    )(u_tiles.reshape(nt, nslab, batch * ttp, LANES), conv_w, conv_b, *gates)

    nat_spec = pl.BlockSpec((batch, tt, bk), lambda h, jj: (0, nt - 1 - jj, h))
    z = pl.pallas_call(
        functools.partial(_rglru_rev_kernel, tt=tt),
        grid=(nh, nt),
        in_specs=[tm_rev] + gate_specs(1) + [tm_rev, nat_spec],
        out_specs=nat_spec,
        out_shape=jax.ShapeDtypeStruct((batch, seq, r), BF16),
        scratch_shapes=state + [pltpu.VMEM((slabs, batch * ttp, LANES), F32)],
        compiler_params=_cparams("parallel", "arbitrary"),
        name="rglru_rev",
    )(uc, *gates, hf, yb.reshape(batch, seq, r))
    return z.reshape(-1, r)


def _route(logits):
    lane = lax.broadcasted_iota(I32, logits.shape, 1).astype(F32)
    neg = jnp.float32(-jnp.inf)
    nolane = jnp.float32(LANES)
    gmask = lane < N_GROUPS
    gmax = jnp.max(jnp.where(gmask, logits, neg), axis=1, keepdims=True)
    eg = jnp.where(gmask, jnp.exp(logits - gmax), 0.0)
    gsum = jnp.sum(eg, axis=1, keepdims=True)
    pg = eg / gsum
    pg_top = 1.0 / gsum
    g_top = jnp.min(jnp.where(gmask & (pg == pg_top), lane, nolane), axis=1, keepdims=True)
    lo = EXPERT_LANE0 + g_top * EXPERTS_PER_GROUP
    emask = (lane >= lo) & (lane < lo + EXPERTS_PER_GROUP)
    emax = jnp.max(jnp.where(emask, logits, neg), axis=1, keepdims=True)
    ee = jnp.where(emask, jnp.exp(logits - emax), 0.0)
    esum = jnp.sum(ee, axis=1, keepdims=True)
    pe = jnp.where(emask, ee / esum, -1.0)
    p1 = 1.0 / esum
    i1 = jnp.min(jnp.where(pe == p1, lane, nolane), axis=1, keepdims=True)
    pe2 = jnp.where(lane == i1, -1.0, pe)
    p2 = jnp.max(pe2, axis=1, keepdims=True)
    i2 = jnp.min(jnp.where(pe2 == p2, lane, nolane), axis=1, keepdims=True)
    psum = p1 + p2
    return i1, i2, pg_top * (p1 / psum), pg_top * (p2 / psum)


def _post_mix_kernel(z_ref, wo_ref, h_ref, g_ref, wr_ref, br_ref,
                     h1_ref, xn_ref, meta_t_ref, cnt_ref, carry_ref, logits_s):
    i = pl.program_id(0)
    tm = z_ref.shape[0]

    @pl.when(i == 0)
    def _():
        logits_s[...] = jnp.zeros_like(logits_s)

    @pl.when(i <= 1)
    def _():
        carry_ref[...] = jnp.zeros_like(carry_ref)

    logits = logits_s[...]

    h1 = h_ref[...] + _dot(z_ref[...], wo_ref[...])
    h1_ref[...] = h1
    xn = _rms(h1, g_ref[...])
    xn_ref[...] = _pack_bf16_pairs(xn)
    logits_s[...] = _dot(xn.astype(BF16), wr_ref[...].astype(BF16)) + br_ref[...]

    i1, i2, w0, w1 = _route(logits)

    lane = lax.broadcasted_iota(I32, (tm, LANES), 1).astype(F32)
    oh0 = lane == i1
    oh1 = lane == i2
    both = (oh0 | oh1).astype(BF16)
    ri = lax.broadcasted_iota(I32, (tm, tm), 0)
    ci = lax.broadcasted_iota(I32, (tm, tm), 1)
    before = (ri > ci).astype(BF16)
    cnt_before = _dot(before, both) + carry_ref[...]
    rank0 = jnp.sum(jnp.where(oh0, cnt_before, 0.0), axis=1, keepdims=True)
    rank1 = jnp.sum(jnp.where(oh1, cnt_before, 0.0), axis=1, keepdims=True)
    carry = carry_ref[...] + jnp.sum(both.astype(F32), axis=0, keepdims=True)
    carry_ref[...] = carry
    cnt_ref[...] = carry

    e0 = i1 - EXPERT_LANE0
    e1 = i2 - EXPERT_LANE0
    meta = jnp.zeros((tm, LANES), F32)
    for ln, val in ((M_E0, e0), (M_E1, e1), (M_R0, rank0), (M_R1, rank1), (M_W0, w0), (M_W1, w1)):
        meta = jnp.where(lane == ln, val, meta)
    meta_t_ref[...] = meta.T[0:SUBLANES, :]


def _post_mix(z, w_out, h, g, w_router, b_router):
    n, d = h.shape
    k = z.shape[1]
    tm = _tile(n, 512)
    nt = n // tm

    def proj(i):
        return (jnp.minimum(i, nt - 1), 0)

    def routed(i):
        return (0, jnp.maximum(i - 1, 0))

    return pl.pallas_call(
        _post_mix_kernel,
        grid=(nt + 1,),
        in_specs=[pl.BlockSpec((tm, k), proj),
                  pl.BlockSpec((k, d), lambda i: (0, 0)),
                  pl.BlockSpec((tm, d), proj),
                  pl.BlockSpec((1, d), lambda i: (0, 0)),
                  pl.BlockSpec((d, LANES), lambda i: (0, 0)),
                  pl.BlockSpec((1, LANES), lambda i: (0, 0))],
        out_specs=[pl.BlockSpec((tm, d), proj),
                   pl.BlockSpec((tm, d // 2), proj),
                   pl.BlockSpec((SUBLANES, tm), routed),
                   pl.BlockSpec((1, LANES), lambda i: (0, 0))],
        out_shape=[jax.ShapeDtypeStruct((n, d), F32), jax.ShapeDtypeStruct((n, d // 2), U32),
                   jax.ShapeDtypeStruct((SUBLANES, n), F32),
                   jax.ShapeDtypeStruct((1, LANES), F32)],
        scratch_shapes=[pltpu.VMEM((1, LANES), F32), pltpu.VMEM((tm, LANES), F32)],
        compiler_params=_cparams("arbitrary"),
        name="post_mix_router",
    )(z, w_out, h, g, w_router, b_router)


def _plan_kernel(cnt_ref, meta_t_ref, dest_ref, estart_ref, ecount_ref, nblk_ref, pstart_ref):
    def per_expert(e, acc):
        nb_e = (cnt_ref[e] + (ROUTE_ROWS - 1)) // ROUTE_ROWS
        pstart_ref[e] = acc * ROUTE_ROWS
        estart_ref[e] = acc
        ecount_ref[e] = nb_e
        return acc + nb_e

    nblk_ref[0] = lax.fori_loop(0, N_EXPERTS, per_expert, 0)

    e0 = meta_t_ref[M_E0:M_E0 + 1, :]
    e1 = meta_t_ref[M_E1:M_E1 + 1, :]
    d0 = meta_t_ref[M_R0:M_R0 + 1, :]
    d1 = meta_t_ref[M_R1:M_R1 + 1, :]
    for e in range(N_EXPERTS):
        ps = pstart_ref[e].astype(F32)
        d0 = d0 + jnp.where(e0 == e, ps, 0.0)
        d1 = d1 + jnp.where(e1 == e, ps, 0.0)
    dest_ref[...] = jnp.zeros_like(dest_ref)
    dest_ref[0:1, :] = d0.astype(I32)
    dest_ref[1:2, :] = d1.astype(I32)


def _plan(cnt_i32, meta_t):
    n = meta_t.shape[1]
    smem = pl.BlockSpec(memory_space=pltpu.SMEM)
    return pl.pallas_call(
        _plan_kernel,
        in_specs=[smem, pl.BlockSpec((SUBLANES, n), lambda: (0, 0))],
        out_specs=[pl.BlockSpec((SUBLANES, n), lambda: (0, 0)), smem, smem, smem],
        out_shape=[jax.ShapeDtypeStruct((SUBLANES, n), I32),
                   jax.ShapeDtypeStruct((N_EXPERTS,), I32),
                   jax.ShapeDtypeStruct((N_EXPERTS,), I32),
                   jax.ShapeDtypeStruct((1,), I32)],
        scratch_shapes=[pltpu.SMEM((N_EXPERTS,), I32)],
        name="dispatch_plan",
    )(cnt_i32, meta_t)


def _sc_mesh():
    return plsc.VectorSubcoreMesh(core_axis_name="c", subcore_axis_name="s",
                                  num_cores=SC_CORES, num_subcores=SC_SUBCORES)


def _sc_scatter_rows(x, idx0, idx1, p_rows):
    m, k = x.shape
    per_w = m // (SC_CORES * SC_SUBCORES)
    assert per_w % SC_CHUNK == 0 and per_w * SC_CORES * SC_SUBCORES == m

    nchunk = per_w // SC_CHUNK
    buf = [pltpu.VMEM((SC_CHUNK,), I32), pltpu.VMEM((SC_CHUNK,), I32),
           pltpu.VMEM((SC_CHUNK, k), x.dtype),
           pltpu.SemaphoreType.DMA, pltpu.SemaphoreType.DMA, pltpu.SemaphoreType.DMA]

    @functools.partial(
        pl.kernel, mesh=_sc_mesh(), out_type=jax.ShapeDtypeStruct((p_rows, k), x.dtype),
        scratch_types=buf + buf, name="sc_scatter_rows")
    def scatter(x_hbm, i0_hbm, i1_hbm, out_hbm, *scratch):
        base = (lax.axis_index("s") * SC_CORES + lax.axis_index("c")) * per_w
        slots = (scratch[:6], scratch[6:])

        def loads(c):
            i0_v, i1_v, rows_v, lsem, _, _ = slots[c % 2]
            off = pl.multiple_of(base + c * SC_CHUNK, SC_CHUNK)
            return [pltpu.make_async_copy(i0_hbm.at[pl.ds(off, SC_CHUNK)], i0_v, lsem),
                    pltpu.make_async_copy(i1_hbm.at[pl.ds(off, SC_CHUNK)], i1_v, lsem),
                    pltpu.make_async_copy(x_hbm.at[pl.ds(off, SC_CHUNK)], rows_v, lsem)]

        def puts(c):
            i0_v, i1_v, rows_v, _, psem0, psem1 = slots[c % 2]
            return [pltpu.make_async_copy(rows_v, out_hbm.at[i0_v], psem0),
                    pltpu.make_async_copy(rows_v, out_hbm.at[i1_v], psem1)]

        for d in loads(0):
            d.start()
        for c in range(nchunk):
            for d in loads(c):
                d.wait()
            for d in puts(c):
                d.start()
            if c >= 1:
                for d in puts(c - 1):
                    d.wait()
            if c + 1 < nchunk:
                for d in loads(c + 1):
                    d.start()
        for d in puts(nchunk - 1):
            d.wait()

    return scatter(x, idx0, idx1)


def _expert_kernel(es_ref, ec_ref, nblk_ref, cnt_ref, xs_ref, wg_ref, wu_ref, wd_ref, ys_ref,
                   x_s, y_s, wg_s, wu_s, wd_s, sem):
    e = pl.program_id(0)
    nb = ec_ref[e]
    b0 = es_ref[e]
    k = x_s.shape[2]
    nb_max = ys_ref.shape[0] // ROUTE_ROWS

    def rows(b):
        return pl.ds(pl.multiple_of(b * ROUTE_ROWS, ROUTE_ROWS), ROUTE_ROWS)

    nblk = nblk_ref[0]

    def copy_in(b):
        slot = lax.rem(b, IN_SLOTS)
        return pltpu.make_async_copy(xs_ref.at[rows(b)], x_s.at[slot], sem.at[0, slot])

    def copy_out(b):
        slot = lax.rem(b, OUT_SLOTS)
        return pltpu.make_async_copy(y_s.at[slot], ys_ref.at[rows(b)], sem.at[1, slot])

    @pl.when(e == 0)
    def _():
        for b in range(IN_AHEAD):
            @pl.when(b < nblk)
            def _(b=b):
                copy_in(b).start()

    @pl.when(nb > 0)
    def _():
        wg_s[...] = wg_ref[...].astype(BF16)
        wu_s[...] = wu_ref[...].astype(BF16)
        wd_s[...] = wd_ref[...].astype(BF16)

        def mlp(b, j, width):
            row = lax.broadcasted_iota(I32, (ROUTE_ROWS, 1), 0)
            xp = jnp.concatenate(
                [jnp.where(row < cnt_ref[e] - (j + i) * ROUTE_ROWS,
                           x_s[lax.rem(b + i, IN_SLOTS)], jnp.uint32(0)) for i in range(width)], axis=0)
            x_hi, x_lo = _unpack_bf16_pairs(xp)
            x_hi = x_hi.astype(BF16)
            x_lo = x_lo.astype(BF16)
            g = _dot(x_hi, wg_s[0:k, :]) + _dot(x_lo, wg_s[k:2 * k, :])
            u = _dot(x_hi, wu_s[0:k, :]) + _dot(x_lo, wu_s[k:2 * k, :])
            hb = (jax.nn.silu(g) * u).astype(BF16)
            y = _pack_bf16_pairs(_dot(hb, wd_s[...]))
            return [y[i * ROUTE_ROWS:(i + 1) * ROUTE_ROWS] for i in range(width)]

        def unit(j, width):
            b = b0 + j
            for i in range(width):
                copy_in(b + i).wait()
            for i in range(width):
                @pl.when(b + i + IN_AHEAD < nblk)
                def _(i=i):
                    copy_in(b + i + IN_AHEAD).start()
            ys = mlp(b, j, width)
            for i in range(width):
                @pl.when(b + i >= OUT_SLOTS)
                def _(i=i):
                    copy_out(b + i - OUT_SLOTS).wait()
            for i in range(width):
                y_s[lax.rem(b + i, OUT_SLOTS)] = ys[i]
                copy_out(b + i).start()

        def pair(jj, c):
            unit(2 * jj, 2)
            return c

        lax.fori_loop(0, nb // 2, pair, 0)

        @pl.when(nb % 2 == 1)
        def _():
            unit(nb - 1, 1)

    @pl.when(e == pl.num_programs(0) - 1)
    def _():
        for back in range(OUT_SLOTS, 0, -1):
            @pl.when(nblk >= back)
            def _(back=back):
                copy_out(nblk - back).wait()

    @pl.when(e == pl.num_programs(0) - 1)
    def _():
        y_s[0] = jnp.zeros_like(y_s[0])

        def zero(b):
            return pltpu.make_async_copy(y_s.at[0], ys_ref.at[rows(b)], sem.at[1, 0])

        def start_zero(b, c):
            zero(b).start()
            return c

        def wait_zero(b, c):
            zero(b).wait()
            return c

        lax.fori_loop(nblk_ref[0], nb_max, start_zero, 0)
        lax.fori_loop(nblk_ref[0], nb_max, wait_zero, 0)


def _experts(estart, ecount, nblk, cnt, xs, w_gate, w_up, w_down, layer):
    p, k = xs.shape
    d, f = w_gate.shape[2], w_gate.shape[3]

    def w_map(e, es, ec, nb, cn):
        return (layer, e, 0, 0)

    grid_spec = pltpu.PrefetchScalarGridSpec(
        num_scalar_prefetch=4,
        grid=(N_EXPERTS,),
        in_specs=[pl.BlockSpec(memory_space=pl.ANY),
                  pl.BlockSpec((None, None, d, f), w_map),
                  pl.BlockSpec((None, None, d, f), w_map),
                  pl.BlockSpec((None, None, f, d), w_map)],
        out_specs=pl.BlockSpec(memory_space=pl.ANY),
        scratch_shapes=[pltpu.VMEM((IN_SLOTS, ROUTE_ROWS, k), U32),
                        pltpu.VMEM((OUT_SLOTS, ROUTE_ROWS, k), U32),
                        pltpu.VMEM((d, f), BF16), pltpu.VMEM((d, f), BF16), pltpu.VMEM((f, d), BF16),
                        pltpu.SemaphoreType.DMA((2, max(IN_SLOTS, OUT_SLOTS)))],
    )
    return pl.pallas_call(
        _expert_kernel,
        grid_spec=grid_spec,
        out_shape=jax.ShapeDtypeStruct((p, k), U32),
        compiler_params=_cparams("arbitrary"),
        name="expert_mlp",
    )(estart, ecount, nblk, cnt, xs, w_gate, w_up, w_down)


def _sc_gather_pair(table, idx0, idx1, start, count):
    k = table.shape[1]
    per_w = count // (SC_CORES * SC_SUBCORES)
    assert per_w % SC_CHUNK == 0 and per_w * SC_CORES * SC_SUBCORES == count
    assert start % SC_CHUNK == 0
    rows = jax.ShapeDtypeStruct((count, k), table.dtype)

    @functools.partial(
        pl.kernel, mesh=_sc_mesh(), out_type=(rows, rows),
        scratch_types=[pltpu.VMEM((SC_CHUNK,), I32), pltpu.VMEM((SC_CHUNK,), I32),
                       pltpu.VMEM((SC_CHUNK, k), table.dtype), pltpu.VMEM((SC_CHUNK, k), table.dtype),
                       pltpu.SemaphoreType.DMA, pltpu.SemaphoreType.DMA,
                       pltpu.SemaphoreType.DMA, pltpu.SemaphoreType.DMA],
        name="sc_gather_pair")
    def gather(table_hbm, i0_hbm, i1_hbm, o0_hbm, o1_hbm, i0_v, i1_v, r0_v, r1_v,
               sem0, sem1, sem2, sem3):
        base = (lax.axis_index("s") * SC_CORES + lax.axis_index("c")) * per_w

        @pl.loop(0, per_w // SC_CHUNK)
        def _(c):
            dst = pl.multiple_of(base + c * SC_CHUNK, SC_CHUNK)
            src = pl.multiple_of(start + dst, SC_CHUNK)
            pltpu.sync_copy(i0_hbm.at[pl.ds(src, SC_CHUNK)], i0_v)
            pltpu.sync_copy(i1_hbm.at[pl.ds(src, SC_CHUNK)], i1_v)
            get0 = pltpu.async_copy(table_hbm.at[i0_v], r0_v, sem0)
            get1 = pltpu.async_copy(table_hbm.at[i1_v], r1_v, sem1)
            get0.wait()
            put0 = pltpu.async_copy(r0_v, o0_hbm.at[pl.ds(dst, SC_CHUNK)], sem2)
            get1.wait()
            put1 = pltpu.async_copy(r1_v, o1_hbm.at[pl.ds(dst, SC_CHUNK)], sem3)
            put0.wait()
            put1.wait()

    return gather(table, idx0, idx1)


def _combine_kernel(*refs, final):
    (ya_ref, yb_ref, h1_ref, meta_ref, p_ref, gp_ref, wpg_ref, bpg_ref, wp_ref,
     gf_ref) = refs[:10]
    out_ref = refs[-1]
    meta_t = meta_ref[...]
    pad = jnp.zeros((LANES - SUBLANES, meta_t.shape[1]), F32)
    meta = jnp.concatenate([meta_t, pad], axis=0).T
    w0 = meta[:, M_W0:M_W0 + 1]
    w1 = meta[:, M_W1:M_W1 + 1]
    ya_hi, ya_lo = _unpack_bf16_pairs(ya_ref[...])
    yb_hi, yb_lo = _unpack_bf16_pairs(yb_ref[...])
    y = jnp.concatenate([ya_hi * w0 + yb_hi * w1, ya_lo * w0 + yb_lo * w1], axis=1)
    h2 = h1_ref[...] + y
    xn = _rms(h2, gp_ref[...]).astype(BF16)
    gate = jax.nn.sigmoid(_dot(xn, wpg_ref[...]) + bpg_ref[...])
    pp = _dot(p_ref[...].astype(BF16), wp_ref[...])
    h3 = h2 + gate * pp
    if final:
        h3 = _rms(h3, gf_ref[...])
    out_ref[...] = h3


def _combine(dest, ys, h1, meta_t, p, layer, g_ple, w_ple_gate, b_ple_gate, w_ple, g_final, final):
    n, d = h1.shape
    pd = p.shape[2]
    k = ys.shape[1]
    n_part = n // COMBINE_PARTS
    tmc = _tile(n_part, 512)
    tiles = n_part // tmc
    vec = pl.BlockSpec((1, d), lambda i: (0, 0))
    out = None
    for q in range(COMBINE_PARTS):
        ya, yb = _sc_gather_pair(ys, dest[0], dest[1], q * n_part, n_part)

        def tok(i, q=q):
            return (q * tiles + i, 0)

        in_specs = [pl.BlockSpec((tmc, k), lambda i: (i, 0)),
                    pl.BlockSpec((tmc, k), lambda i: (i, 0)),
                    pl.BlockSpec((tmc, d), tok),
                    pl.BlockSpec((SUBLANES, tmc), lambda i, q=q: (0, q * tiles + i)),
                    pl.BlockSpec((None, tmc, pd), lambda i, q=q: (layer, q * tiles + i, 0)),
                    vec,
                    pl.BlockSpec((d, d), lambda i: (0, 0)),
                    vec,
                    pl.BlockSpec((pd, d), lambda i: (0, 0)),
                    vec]
        args = [ya, yb, h1, meta_t, p, g_ple, w_ple_gate, b_ple_gate, w_ple, g_final]
        aliases = {}
        if out is not None:
            in_specs.append(pl.BlockSpec(memory_space=pl.ANY))
            args.append(out)
            aliases = {len(args) - 1: 0}
        out = pl.pallas_call(
            functools.partial(_combine_kernel, final=final),
            grid=(tiles,),
            in_specs=in_specs,
            out_specs=pl.BlockSpec((tmc, d), tok),
            out_shape=jax.ShapeDtypeStruct((n, d), F32),
            input_output_aliases=aliases,
            compiler_params=_cparams("parallel"),
            name="combine_ple",
        )(*args)
    return out


def _row(v):
    return v.reshape(1, -1)


def _moe_and_ple(z, w_out, h, i, final, norm_ffn, w_router_group, b_router_group, w_router_expert,
                 b_router_expert, w_exp_gate, w_exp_up, w_exp_down, norm_ple, w_ple, w_ple_gate,
                 b_ple_gate, norm_final, p):
    n, d = h.shape
    pad = LANES - N_GROUPS - N_EXPERTS
    w_router = jnp.concatenate(
        [w_router_group[i], w_router_expert[i], jnp.zeros((d, pad), F32)], axis=1)
    b_router = jnp.concatenate(
        [b_router_group[i], b_router_expert[i], jnp.zeros((pad,), F32)]).reshape(1, LANES)
    h1, xn, meta_t, cnt = _post_mix(z, w_out, h, _row(norm_ffn[i]), w_router, b_router)
    cnt_i32 = cnt[0, EXPERT_LANE0:EXPERT_LANE0 + N_EXPERTS].astype(I32)
    nb_max = (2 * n) // ROUTE_ROWS + N_EXPERTS
    dest, estart, ecount, nblk = _plan(cnt_i32, meta_t)
    xs = _sc_scatter_rows(xn, dest[0], dest[1], nb_max * ROUTE_ROWS)
    ys = _experts(estart, ecount, nblk, cnt_i32, xs, w_exp_gate, w_exp_up, w_exp_down, i)
    return _combine(dest, ys, h1, meta_t, p.reshape(p.shape[0], n, -1), i, _row(norm_ple[i]),
                    w_ple_gate[i].astype(BF16), _row(b_ple_gate[i]), w_ple[i].astype(BF16),
                    _row(norm_final), final)


def kernel(x, p, norm_mix, w_in_a, conv_a, w_out_a, w_in_b, conv_b, conv_bias_b, w_rgate_b, b_rgate_b, w_igate_b, b_igate_b, lam_b, w_out_b, norm_ffn, w_router_group, b_router_group, w_router_expert, b_router_expert, w_exp_gate, w_exp_up, w_exp_down, norm_ple, w_ple, w_ple_gate, b_ple_gate, norm_final):
    batch, seq, d = x.shape
    depth = p.shape[0]
    n = batch * seq
    h = x.reshape(n, d)
    for i in range(depth):
        j = i // 2
        if i % 2 == 0:
            z = _mix_a(h, _row(norm_mix[i]), w_in_a[j].astype(BF16), conv_a[j], seq)
            w_out = w_out_a[j]
        else:
            tt = _tile(seq, 512)
            yb, u_tiles = _mix_b_in(h, _row(norm_mix[i]), w_in_b[j].astype(BF16), batch, seq, tt)
            z = _rglru(u_tiles, yb, conv_b[j], _row(conv_bias_b[j]),
                       (0.5 * w_rgate_b[j]).astype(BF16), b_rgate_b[j],
                       (0.5 * w_igate_b[j]).astype(BF16),
                       b_igate_b[j], lam_b[j], seq, tt)
            w_out = w_out_b[j]
        h = _moe_and_ple(z, w_out.astype(BF16), h, i, i == depth - 1, norm_ffn, w_router_group,
                         b_router_group, w_router_expert, b_router_expert, w_exp_gate, w_exp_up,
                         w_exp_down, norm_ple, w_ple, w_ple_gate, b_ple_gate, norm_final, p)
    return h.reshape(batch, seq, d)
```

```python
import functools

import jax
import jax.numpy as jnp
from jax import lax
from jax.experimental import pallas as pl
from jax.experimental.pallas import tpu as pltpu
from jax.experimental.pallas import tpu_sc as plsc

F32 = jnp.float32
BF16 = jnp.bfloat16
I32 = jnp.int32
U32 = jnp.uint32

EPS = 1e-6
N_GROUPS = 4
EXPERTS_PER_GROUP = 8
N_EXPERTS = N_GROUPS * EXPERTS_PER_GROUP
RG_C = 8.0

LANES = 128
SUBLANES = 8
EXPERT_LANE0 = N_GROUPS
ROUTE_ROWS = 256
IN_AHEAD = 4
IN_SLOTS = IN_AHEAD + 2
OUT_SLOTS = 4
SC_CORES = 2
SC_SUBCORES = 16
SC_CHUNK = 64
POST_SLOTS = 3
COMBINE_PARTS = 4
M_E0, M_E1, M_R0, M_R1, M_W0, M_W1 = 0, 1, 2, 3, 4, 5
VMEM_LIMIT = 48 * 1024 * 1024


def _cparams(*sem):
    return pltpu.CompilerParams(dimension_semantics=sem, vmem_limit_bytes=VMEM_LIMIT)


def _rms(x, g):
    return x * lax.rsqrt(jnp.mean(x * x, axis=-1, keepdims=True) + EPS) * g


def _dot(a, b):
    return jnp.dot(a, b, preferred_element_type=F32)


def _pack_bf16_pairs(x):
    k = x.shape[1] // 2
    hi = lax.bitcast_convert_type(x[:, :k].astype(BF16).astype(F32), U32)
    lo = lax.bitcast_convert_type(x[:, k:].astype(BF16).astype(F32), U32)
    return hi | (lo >> 16)


def _unpack_bf16_pairs(p):
    hi = lax.bitcast_convert_type(p & jnp.uint32(0xFFFF0000), F32)
    lo = lax.bitcast_convert_type(p << 16, F32)
    return hi, lo


def _tile(n, want):
    t = min(n, want)
    assert n % t == 0, (n, want)
    return t


def _mix_a_kernel(h_ref, g_ref, w_ref, cw_ref, z_ref, v_s, bg_s, last_s, *, tiles_per_seq):
    i = pl.program_id(0)
    tm, d = h_ref.shape

    @pl.when(i == 0)
    def _():
        v_s[...] = jnp.zeros_like(v_s)
        bg_s[...] = jnp.zeros_like(bg_s)
        last_s[...] = jnp.zeros_like(last_s)

    v = v_s[...].astype(F32)
    bg_old = bg_s[...]
    before = last_s[0:1, :]

    xn = _rms(h_ref[...], g_ref[...]).astype(BF16)
    cg = _dot(xn, w_ref[:, d:2 * d])
    hh = _dot(xn, w_ref[:, 2 * d:3 * d])
    v_new = (cg * hh).astype(BF16)
    bg = _dot(xn, w_ref[:, 0:d])

    old = i - 1
    at_start = (old % tiles_per_seq) == 0
    at_end = (old % tiles_per_seq) == tiles_per_seq - 1
    prev_row = jnp.where(at_start, 0.0, before)
    next_row = jnp.where(at_end, 0.0, v_new[0:1, :].astype(F32))
    row = lax.broadcasted_iota(I32, (tm, 1), 0)
    v_dn = jnp.where(row == 0, prev_row, pltpu.roll(v, 1, 0))
    v_up = jnp.where(row == tm - 1, next_row, pltpu.roll(v, tm - 1, 0))
    cw = cw_ref[...]
    u = cw[0:1, :] * v_dn + cw[1:2, :] * v + cw[2:3, :] * v_up
    z_ref[...] = (bg_old.astype(F32) * u).astype(BF16)

    last_s[0:1, :] = v[tm - 1:tm, :]
    v_s[...] = v_new
    bg_s[...] = bg.astype(BF16)


def _mix_a(h, g, w_in, conv_w, seq):
    n, d = h.shape
    tm = _tile(seq, 512)
    nt = n // tm
    return pl.pallas_call(
        functools.partial(_mix_a_kernel, tiles_per_seq=seq // tm),
        grid=(nt + 1,),
        in_specs=[pl.BlockSpec((tm, d), lambda i: (jnp.minimum(i, nt - 1), 0)),
                  pl.BlockSpec((1, d), lambda i: (0, 0)),
                  pl.BlockSpec((d, 3 * d), lambda i: (0, 0), pipeline_mode=pl.Buffered(1)),
                  pl.BlockSpec(conv_w.shape, lambda i: (0, 0))],
        out_specs=pl.BlockSpec((tm, d), lambda i: (jnp.maximum(i - 1, 0), 0)),
        out_shape=jax.ShapeDtypeStruct((n, d), BF16),
        scratch_shapes=[pltpu.VMEM((tm, d), BF16), pltpu.VMEM((tm, d), BF16),
                        pltpu.VMEM((SUBLANES, d), F32)],
        compiler_params=_cparams("arbitrary"),
        name="mix_a",
    )(h, g, w_in, conv_w)


def _mix_b_in_kernel(h_ref, hp_ref, hn_ref, g_ref, w_ref, yb_ref, u_ref, *, steps_per_seq, tt):
    i = pl.program_id(0)
    r = yb_ref.shape[1]
    tiles = h_ref.shape[0] // tt
    g = g_ref[...]
    xn = _rms(h_ref[...], g).astype(BF16)
    yb_ref[...] = jax.nn.gelu(_dot(xn, w_ref[:, 0:r])).astype(BF16)
    tm = h_ref.shape[0]
    xh = _rms(jnp.concatenate([hp_ref[...], hn_ref[...]], axis=0), g).astype(BF16)
    u_all = _dot(jnp.concatenate([xn, xh], axis=0), w_ref[:, r:2 * r])
    u = u_all[0:tm]
    uh = u_all[tm:tm + 2 * SUBLANES]
    at_start = (i % steps_per_seq) == 0
    at_end = (i % steps_per_seq) == steps_per_seq - 1
    prev2 = jnp.where(at_start, 0.0, uh[SUBLANES - 2:SUBLANES, :])
    next1 = jnp.where(at_end, 0.0, uh[SUBLANES:SUBLANES + 1, :])
    row = lax.broadcasted_iota(I32, (SUBLANES, 1), 0)
    for k in range(tiles):
        p2 = prev2 if k == 0 else u[k * tt - 2:k * tt]
        n1 = next1 if k == tiles - 1 else u[(k + 1) * tt:(k + 1) * tt + 1]
        pad = jnp.where(row == 0, p2[0:1], jnp.where(row == 1, p2[1:2], jnp.where(row == 2, n1, 0.0)))
        for s in range(r // LANES):
            lanes = slice(s * LANES, (s + 1) * LANES)
            u_ref[k, s, 0:tt, :] = u[k * tt:(k + 1) * tt, lanes]
            u_ref[k, s, tt:tt + SUBLANES, :] = pad[:, lanes]


def _mix_b_in(h, g, w_in, batch, seq, tt):
    n, d = h.shape
    r = w_in.shape[1] // 2
    tm = _tile(seq, 1024)
    steps_per_seq = seq // tm
    tiles = tm // tt
    hb = tm // SUBLANES
    nhalo = n // SUBLANES
    return pl.pallas_call(
        functools.partial(_mix_b_in_kernel, steps_per_seq=steps_per_seq, tt=tt),
        grid=(n // tm,),
        in_specs=[pl.BlockSpec((tm, d), lambda i: (i, 0)),
                  pl.BlockSpec((SUBLANES, d), lambda i: (jnp.maximum(i * hb - 1, 0), 0)),
                  pl.BlockSpec((SUBLANES, d), lambda i: (jnp.minimum((i + 1) * hb, nhalo - 1), 0)),
                  pl.BlockSpec((1, d), lambda i: (0, 0)),
                  pl.BlockSpec((d, 2 * r), lambda i: (0, 0), pipeline_mode=pl.Buffered(1))],
        out_specs=[pl.BlockSpec((tm, r), lambda i: (i, 0)),
                   pl.BlockSpec((tiles, r // LANES, None, tt + SUBLANES, LANES),
                                lambda i: (i % steps_per_seq, 0, i // steps_per_seq, 0, 0))],
        out_shape=[jax.ShapeDtypeStruct((n, r), BF16),
                   jax.ShapeDtypeStruct((seq // tt, r // LANES, batch, tt + SUBLANES, LANES), F32)],
        compiler_params=_cparams("parallel"),
        name="mix_b_in",
    )(h, h, h, g, w_in)


def _rglru_gates(u, wr_ref, br_ref, wi_ref, bi_ref, lam_ref):
    ub = u.astype(BF16)
    t_r = jnp.tanh(_dot(ub, wr_ref[...]) + 0.5 * br_ref[...])
    t_i = jnp.tanh(_dot(ub, wi_ref[...]) + 0.5 * bi_ref[...])
    c = (-0.25 * RG_C) * jax.nn.softplus(-lam_ref[...])
    th = jnp.tanh(c + c * t_r)
    q = 1.0 / (1.0 - th)
    nth = -th
    root = jnp.where(nth > 0.0, nth * lax.rsqrt(nth), 0.0)
    return (1.0 + th) * q, (q * root) * (u + u * t_i)


def _rglru_fwd_kernel(u_ref, cw_ref, cb_ref, wr_ref, br_ref, wi_ref, bi_ref, lam_ref,
                      hf_ref, uc_ref, x_s, carry_s, *, tt):
    nb = SUBLANES
    ttp = tt + SUBLANES
    rows = tt * nb
    slabs = u_ref.shape[0]

    @pl.when(pl.program_id(1) == 0)
    def _():
        carry_s[...] = jnp.zeros_like(carry_s)

    def gather_t(tl, dst):
        for s in range(slabs):
            x_s[pl.ds(dst, nb), s * LANES:(s + 1) * LANES] = u_ref[s, pl.ds(tl, nb, stride=ttp), :]

    for tl in range(tt):
        gather_t(tl, (tl + 2) * nb)
    gather_t(tt, 0)
    gather_t(tt + 1, nb)
    gather_t(tt + 2, (tt + 2) * nb)

    cw = cw_ref[...]
    u = (cw[0:1, :] * x_s[0:rows, :] + cw[1:2, :] * x_s[nb:nb + rows, :]
         + cw[2:3, :] * x_s[2 * nb:2 * nb + rows, :] + cw[3:4, :] * x_s[3 * nb:3 * nb + rows, :]
         + cb_ref[...])
    a, g = _rglru_gates(u, wr_ref, br_ref, wi_ref, bi_ref, lam_ref)
    h = carry_s[...]
    for tl in range(tt):
        r = slice(tl * nb, (tl + 1) * nb)
        back = slice((tt - 1 - tl) * nb, (tt - tl) * nb)
        h = a[r] * h + g[r]
        hf_ref[back, :] = h.astype(BF16)
        uc_ref[back, :] = u[r].astype(BF16)
    carry_s[...] = h


def _rglru_rev_kernel(uc_ref, wr_ref, br_ref, wi_ref, bi_ref, lam_ref, hf_ref, yb_ref,
                      z_ref, carry_s, nat_s, *, tt):
    nb = SUBLANES
    ttp = tt + SUBLANES
    slabs = nat_s.shape[0]

    @pl.when(pl.program_id(1) == 0)
    def _():
        carry_s[...] = jnp.zeros_like(carry_s)

    a, g = _rglru_gates(uc_ref[...].astype(F32), wr_ref, br_ref, wi_ref, bi_ref, lam_ref)
    h = carry_s[...]
    for k in range(tt):
        r = slice(k * nb, (k + 1) * nb)
        h = a[r] * h + g[r]
        hs = h + hf_ref[r, :].astype(F32)
        for s in range(slabs):
            nat_s[s, pl.ds(tt - 1 - k, nb, stride=ttp), :] = hs[:, s * LANES:(s + 1) * LANES]
    carry_s[...] = h
    for b in range(nb):
        for s in range(slabs):
            lanes = slice(s * LANES, (s + 1) * LANES)
            hb = nat_s[s, b * ttp:b * ttp + tt, :]
            z_ref[b, :, lanes] = (yb_ref[b, :, lanes].astype(F32) * hb).astype(BF16)


def _rglru(u_tiles, yb, conv_w, conv_b, w_r, b_r, w_i, b_i, lam, seq, tt):
    nt, nslab, batch, ttp, _ = u_tiles.shape
    r = nslab * LANES
    assert batch == SUBLANES, "time-major tile layout puts the batch on the 8 sublanes"
    nh, bk = w_r.shape[1], w_r.shape[2]
    slabs = bk // LANES
    kw = conv_w.shape[0]
    rows = tt * batch
    gates = [w_r, b_r.reshape(2, 1, r), w_i, b_i.reshape(2, 1, r), lam.reshape(2, 1, r)]

    def gate_specs(d):
        vec = pl.BlockSpec((None, 1, bk), lambda h, jj: (d, 0, h))
        mat = pl.BlockSpec((None, None, bk, bk), lambda h, jj: (d, h, 0, 0))
        return [mat, vec, mat, vec, vec]

    tm_spec = pl.BlockSpec((rows, bk), lambda h, jj: (jj, h))
    tm_rev = pl.BlockSpec((rows, bk), lambda h, jj: (nt - 1 - jj, h))
    tm_shape = jax.ShapeDtypeStruct((seq * batch, r), BF16)
    state = [pltpu.VMEM((batch, bk), F32)]

    hf, uc = pl.pallas_call(
        functools.partial(_rglru_fwd_kernel, tt=tt),
        grid=(nh, nt),
        in_specs=[pl.BlockSpec((None, slabs, batch * ttp, LANES), lambda h, jj: (jj, h, 0, 0)),
                  pl.BlockSpec((kw, bk), lambda h, jj: (0, h)),
                  pl.BlockSpec((1, bk), lambda h, jj: (0, h))] + gate_specs(0),
        out_specs=[tm_spec, tm_spec],
        out_shape=[tm_shape, tm_shape],
        scratch_shapes=[pltpu.VMEM(((tt + 3) * batch, bk), F32)] + state,
        compiler_params=_cparams("parallel", "arbitrary"),
        name="rglru_fwd",
    )(u_tiles.reshape(nt, nslab, batch * ttp, LANES), conv_w, conv_b, *gates)

    nat_spec = pl.BlockSpec((batch, tt, bk), lambda h, jj: (0, nt - 1 - jj, h))
    z = pl.pallas_call(
        functools.partial(_rglru_rev_kernel, tt=tt),
        grid=(nh, nt),
        in_specs=[tm_rev] + gate_specs(1) + [tm_rev, nat_spec],
        out_specs=nat_spec,
        out_shape=jax.ShapeDtypeStruct((batch, seq, r), BF16),
        scratch_shapes=state + [pltpu.VMEM((slabs, batch * ttp, LANES), F32)],
        compiler_params=_cparams("parallel", "arbitrary"),
        name="rglru_rev",
    )(uc, *gates, hf, yb.reshape(batch, seq, r))
    return z.reshape(-1, r)


def _route(logits):
    lane = lax.broadcasted_iota(I32, logits.shape, 1).astype(F32)
    neg = jnp.float32(-jnp.inf)
    nolane = jnp.float32(LANES)
    gmask = lane < N_GROUPS
    gmax = jnp.max(jnp.where(gmask, logits, neg), axis=1, keepdims=True)
    eg = jnp.where(gmask, jnp.exp(logits - gmax), 0.0)
    gsum = jnp.sum(eg, axis=1, keepdims=True)
    pg = eg / gsum
    pg_top = 1.0 / gsum
    g_top = jnp.min(jnp.where(gmask & (pg == pg_top), lane, nolane), axis=1, keepdims=True)
    lo = EXPERT_LANE0 + g_top * EXPERTS_PER_GROUP
    emask = (lane >= lo) & (lane < lo + EXPERTS_PER_GROUP)
    emax = jnp.max(jnp.where(emask, logits, neg), axis=1, keepdims=True)
    ee = jnp.where(emask, jnp.exp(logits - emax), 0.0)
    esum = jnp.sum(ee, axis=1, keepdims=True)
    pe = jnp.where(emask, ee / esum, -1.0)
    p1 = 1.0 / esum
    i1 = jnp.min(jnp.where(pe == p1, lane, nolane), axis=1, keepdims=True)
    pe2 = jnp.where(lane == i1, -1.0, pe)
    p2 = jnp.max(pe2, axis=1, keepdims=True)
    i2 = jnp.min(jnp.where(pe2 == p2, lane, nolane), axis=1, keepdims=True)
    psum = p1 + p2
    return i1, i2, pg_top * (p1 / psum), pg_top * (p2 / psum)


def _post_mix_kernel(z_hbm, wo_ref, h_hbm, g_ref, wr_ref, br_ref,
                     h1_ref, xn_ref, meta_t_ref, cnt_ref, carry_ref, logits_s,
                     z_ring, h_ring, sem, *, nt):
    i = pl.program_id(0)
    tm = z_ring.shape[1]

    def tile_copies(t):
        slot = t % POST_SLOTS
        rows = pl.ds(pl.multiple_of(t * tm, tm), tm)
        return (pltpu.make_async_copy(z_hbm.at[rows], z_ring.at[slot], sem.at[0, slot]),
                pltpu.make_async_copy(h_hbm.at[rows], h_ring.at[slot], sem.at[1, slot]))

    def start_tile(t):
        for c in tile_copies(t):
            c.start()

    @pl.when(i == 0)
    def _():
        logits_s[...] = jnp.zeros_like(logits_s)
        for t in range(min(POST_SLOTS - 1, nt)):
            start_tile(t)

    @pl.when(i + POST_SLOTS - 1 < nt)
    def _():
        start_tile(i + POST_SLOTS - 1)

    @pl.when(i < nt)
    def _():
        for c in tile_copies(i):
            c.wait()

    slot = jnp.minimum(i, nt - 1) % POST_SLOTS
    z_ref = z_ring.at[slot]
    h_ref = h_ring.at[slot]

    @pl.when(i <= 1)
    def _():
        carry_ref[...] = jnp.zeros_like(carry_ref)

    logits = logits_s[...]

    h1 = h_ref[...] + _dot(z_ref[...], wo_ref[...])
    h1_ref[...] = h1
    xn = _rms(h1, g_ref[...])
    xn_ref[...] = _pack_bf16_pairs(xn)
    logits_s[...] = _dot(xn.astype(BF16), wr_ref[...].astype(BF16)) + br_ref[...]

    i1, i2, w0, w1 = _route(logits)

    lane = lax.broadcasted_iota(I32, (tm, LANES), 1).astype(F32)
    oh0 = lane == i1
    oh1 = lane == i2
    both = (oh0 | oh1).astype(BF16)
    ri = lax.broadcasted_iota(I32, (tm, tm), 0)
    ci = lax.broadcasted_iota(I32, (tm, tm), 1)
    before = (ri > ci).astype(BF16)
    cnt_before = _dot(before, both) + carry_ref[...]
    rank0 = jnp.sum(jnp.where(oh0, cnt_before, 0.0), axis=1, keepdims=True)
    rank1 = jnp.sum(jnp.where(oh1, cnt_before, 0.0), axis=1, keepdims=True)
    carry = carry_ref[...] + jnp.sum(both.astype(F32), axis=0, keepdims=True)
    carry_ref[...] = carry
    cnt_ref[...] = carry

    e0 = i1 - EXPERT_LANE0
    e1 = i2 - EXPERT_LANE0
    meta = jnp.zeros((tm, LANES), F32)
    for ln, val in ((M_E0, e0), (M_E1, e1), (M_R0, rank0), (M_R1, rank1), (M_W0, w0), (M_W1, w1)):
        meta = jnp.where(lane == ln, val, meta)
    meta_t_ref[...] = meta.T[0:SUBLANES, :]


def _post_mix(z, w_out, h, g, w_router, b_router):
    n, d = h.shape
    k = z.shape[1]
    tm = _tile(n, 512)
    nt = n // tm

    def proj(i):
        return (jnp.minimum(i, nt - 1), 0)

    def routed(i):
        return (0, jnp.maximum(i - 1, 0))

    return pl.pallas_call(
        functools.partial(_post_mix_kernel, nt=nt),
        grid=(nt + 1,),
        in_specs=[pl.BlockSpec(memory_space=pl.ANY),
                  pl.BlockSpec((k, d), lambda i: (0, 0)),
                  pl.BlockSpec(memory_space=pl.ANY),
                  pl.BlockSpec((1, d), lambda i: (0, 0)),
                  pl.BlockSpec((d, LANES), lambda i: (0, 0)),
                  pl.BlockSpec((1, LANES), lambda i: (0, 0))],
        out_specs=[pl.BlockSpec((tm, d), proj),
                   pl.BlockSpec((tm, d // 2), proj),
                   pl.BlockSpec((SUBLANES, tm), routed),
                   pl.BlockSpec((1, LANES), lambda i: (0, 0))],
        out_shape=[jax.ShapeDtypeStruct((n, d), F32), jax.ShapeDtypeStruct((n, d // 2), U32),
                   jax.ShapeDtypeStruct((SUBLANES, n), F32),
                   jax.ShapeDtypeStruct((1, LANES), F32)],
        scratch_shapes=[pltpu.VMEM((1, LANES), F32), pltpu.VMEM((tm, LANES), F32),
                        pltpu.VMEM((POST_SLOTS, tm, k), z.dtype),
                        pltpu.VMEM((POST_SLOTS, tm, d), F32),
                        pltpu.SemaphoreType.DMA((2, POST_SLOTS))],
        compiler_params=_cparams("arbitrary"),
        name="post_mix_router",
    )(z, w_out, h, g, w_router, b_router)


def _plan_kernel(cnt_ref, meta_t_ref, dest_ref, estart_ref, ecount_ref, nblk_ref, pstart_ref):
    def per_expert(e, acc):
        nb_e = (cnt_ref[e] + (ROUTE_ROWS - 1)) // ROUTE_ROWS
        pstart_ref[e] = acc * ROUTE_ROWS
        estart_ref[e] = acc
        ecount_ref[e] = nb_e
        return acc + nb_e

    nblk_ref[0] = lax.fori_loop(0, N_EXPERTS, per_expert, 0)

    e0 = meta_t_ref[M_E0:M_E0 + 1, :]
    e1 = meta_t_ref[M_E1:M_E1 + 1, :]
    d0 = meta_t_ref[M_R0:M_R0 + 1, :]
    d1 = meta_t_ref[M_R1:M_R1 + 1, :]
    for e in range(N_EXPERTS):
        ps = pstart_ref[e].astype(F32)
        d0 = d0 + jnp.where(e0 == e, ps, 0.0)
        d1 = d1 + jnp.where(e1 == e, ps, 0.0)
    dest_ref[...] = jnp.zeros_like(dest_ref)
    dest_ref[0:1, :] = d0.astype(I32)
    dest_ref[1:2, :] = d1.astype(I32)


def _plan(cnt_i32, meta_t):
    n = meta_t.shape[1]
    smem = pl.BlockSpec(memory_space=pltpu.SMEM)
    return pl.pallas_call(
        _plan_kernel,
        in_specs=[smem, pl.BlockSpec((SUBLANES, n), lambda: (0, 0))],
        out_specs=[pl.BlockSpec((SUBLANES, n), lambda: (0, 0)), smem, smem, smem],
        out_shape=[jax.ShapeDtypeStruct((SUBLANES, n), I32),
                   jax.ShapeDtypeStruct((N_EXPERTS,), I32),
                   jax.ShapeDtypeStruct((N_EXPERTS,), I32),
                   jax.ShapeDtypeStruct((1,), I32)],
        scratch_shapes=[pltpu.SMEM((N_EXPERTS,), I32)],
        name="dispatch_plan",
    )(cnt_i32, meta_t)


def _sc_mesh():
    return plsc.VectorSubcoreMesh(core_axis_name="c", subcore_axis_name="s",
                                  num_cores=SC_CORES, num_subcores=SC_SUBCORES)


def _sc_scatter_rows(x, idx0, idx1, p_rows):
    m, k = x.shape
    per_w = m // (SC_CORES * SC_SUBCORES)
    assert per_w % SC_CHUNK == 0 and per_w * SC_CORES * SC_SUBCORES == m

    nchunk = per_w // SC_CHUNK
    buf = [pltpu.VMEM((SC_CHUNK,), I32), pltpu.VMEM((SC_CHUNK,), I32),
           pltpu.VMEM((SC_CHUNK, k), x.dtype),
           pltpu.SemaphoreType.DMA, pltpu.SemaphoreType.DMA, pltpu.SemaphoreType.DMA]

    @functools.partial(
        pl.kernel, mesh=_sc_mesh(), out_type=jax.ShapeDtypeStruct((p_rows, k), x.dtype),
        scratch_types=buf + buf, name="sc_scatter_rows")
    def scatter(x_hbm, i0_hbm, i1_hbm, out_hbm, *scratch):
        base = (lax.axis_index("s") * SC_CORES + lax.axis_index("c")) * per_w
        slots = (scratch[:6], scratch[6:])

        def loads(c):
            i0_v, i1_v, rows_v, lsem, _, _ = slots[c % 2]
            off = pl.multiple_of(base + c * SC_CHUNK, SC_CHUNK)
            return [pltpu.make_async_copy(i0_hbm.at[pl.ds(off, SC_CHUNK)], i0_v, lsem),
                    pltpu.make_async_copy(i1_hbm.at[pl.ds(off, SC_CHUNK)], i1_v, lsem),
                    pltpu.make_async_copy(x_hbm.at[pl.ds(off, SC_CHUNK)], rows_v, lsem)]

        def puts(c):
            i0_v, i1_v, rows_v, _, psem0, psem1 = slots[c % 2]
            return [pltpu.make_async_copy(rows_v, out_hbm.at[i0_v], psem0),
                    pltpu.make_async_copy(rows_v, out_hbm.at[i1_v], psem1)]

        for d in loads(0):
            d.start()
        for c in range(nchunk):
            for d in loads(c):
                d.wait()
            for d in puts(c):
                d.start()
            if c >= 1:
                for d in puts(c - 1):
                    d.wait()
            if c + 1 < nchunk:
                for d in loads(c + 1):
                    d.start()
        for d in puts(nchunk - 1):
            d.wait()

    return scatter(x, idx0, idx1)


def _expert_kernel(es_ref, ec_ref, nblk_ref, cnt_ref, xs_ref, wg_ref, wu_ref, wd_ref, ys_ref,
                   x_s, y_s, wg_s, wu_s, wd_s, sem):
    e = pl.program_id(0)
    nb = ec_ref[e]
    b0 = es_ref[e]
    k = x_s.shape[2]
    nb_max = ys_ref.shape[0] // ROUTE_ROWS

    def rows(b):
        return pl.ds(pl.multiple_of(b * ROUTE_ROWS, ROUTE_ROWS), ROUTE_ROWS)

    nblk = nblk_ref[0]

    def copy_in(b):
        slot = lax.rem(b, IN_SLOTS)
        return pltpu.make_async_copy(xs_ref.at[rows(b)], x_s.at[slot], sem.at[0, slot])

    def copy_out(b):
        slot = lax.rem(b, OUT_SLOTS)
        return pltpu.make_async_copy(y_s.at[slot], ys_ref.at[rows(b)], sem.at[1, slot])

    @pl.when(e == 0)
    def _():
        for b in range(IN_AHEAD):
            @pl.when(b < nblk)
            def _(b=b):
                copy_in(b).start()

    @pl.when(nb > 0)
    def _():
        wg_s[...] = wg_ref[...].astype(BF16)
        wu_s[...] = wu_ref[...].astype(BF16)
        wd_s[...] = wd_ref[...].astype(BF16)

        def mlp(b, j, width):
            row = lax.broadcasted_iota(I32, (ROUTE_ROWS, 1), 0)
            xp = jnp.concatenate(
                [jnp.where(row < cnt_ref[e] - (j + i) * ROUTE_ROWS,
                           x_s[lax.rem(b + i, IN_SLOTS)], jnp.uint32(0)) for i in range(width)], axis=0)
            x_hi, x_lo = _unpack_bf16_pairs(xp)
            x_hi = x_hi.astype(BF16)
            x_lo = x_lo.astype(BF16)
            g = _dot(x_hi, wg_s[0:k, :]) + _dot(x_lo, wg_s[k:2 * k, :])
            u = _dot(x_hi, wu_s[0:k, :]) + _dot(x_lo, wu_s[k:2 * k, :])
            hb = (jax.nn.silu(g) * u).astype(BF16)
            y = _pack_bf16_pairs(_dot(hb, wd_s[...]))
            return [y[i * ROUTE_ROWS:(i + 1) * ROUTE_ROWS] for i in range(width)]

        def unit(j, width):
            b = b0 + j
            for i in range(width):
                copy_in(b + i).wait()
            for i in range(width):
                @pl.when(b + i + IN_AHEAD < nblk)
                def _(i=i):
                    copy_in(b + i + IN_AHEAD).start()
            ys = mlp(b, j, width)
            for i in range(width):
                @pl.when(b + i >= OUT_SLOTS)
                def _(i=i):
                    copy_out(b + i - OUT_SLOTS).wait()
            for i in range(width):
                y_s[lax.rem(b + i, OUT_SLOTS)] = ys[i]
                copy_out(b + i).start()

        def pair(jj, c):
            unit(2 * jj, 2)
            return c

        lax.fori_loop(0, nb // 2, pair, 0)

        @pl.when(nb % 2 == 1)
        def _():
            unit(nb - 1, 1)

    @pl.when(e == pl.num_programs(0) - 1)
    def _():
        for back in range(OUT_SLOTS, 0, -1):
            @pl.when(nblk >= back)
            def _(back=back):
                copy_out(nblk - back).wait()

    @pl.when(e == pl.num_programs(0) - 1)
    def _():
        y_s[0] = jnp.zeros_like(y_s[0])

        def zero(b):
            return pltpu.make_async_copy(y_s.at[0], ys_ref.at[rows(b)], sem.at[1, 0])

        def start_zero(b, c):
            zero(b).start()
            return c

        def wait_zero(b, c):
            zero(b).wait()
            return c

        lax.fori_loop(nblk_ref[0], nb_max, start_zero, 0)
        lax.fori_loop(nblk_ref[0], nb_max, wait_zero, 0)


def _experts(estart, ecount, nblk, cnt, xs, w_gate, w_up, w_down, layer):
    p, k = xs.shape
    d, f = w_gate.shape[2], w_gate.shape[3]

    def w_map(e, es, ec, nb, cn):
        return (layer, e, 0, 0)

    grid_spec = pltpu.PrefetchScalarGridSpec(
        num_scalar_prefetch=4,
        grid=(N_EXPERTS,),
        in_specs=[pl.BlockSpec(memory_space=pl.ANY),
                  pl.BlockSpec((None, None, d, f), w_map),
                  pl.BlockSpec((None, None, d, f), w_map),
                  pl.BlockSpec((None, None, f, d), w_map)],
        out_specs=pl.BlockSpec(memory_space=pl.ANY),
        scratch_shapes=[pltpu.VMEM((IN_SLOTS, ROUTE_ROWS, k), U32),
                        pltpu.VMEM((OUT_SLOTS, ROUTE_ROWS, k), U32),
                        pltpu.VMEM((d, f), BF16), pltpu.VMEM((d, f), BF16), pltpu.VMEM((f, d), BF16),
                        pltpu.SemaphoreType.DMA((2, max(IN_SLOTS, OUT_SLOTS)))],
    )
    return pl.pallas_call(
        _expert_kernel,
        grid_spec=grid_spec,
        out_shape=jax.ShapeDtypeStruct((p, k), U32),
        compiler_params=_cparams("arbitrary"),
        name="expert_mlp",
    )(estart, ecount, nblk, cnt, xs, w_gate, w_up, w_down)


def _sc_gather_pair(table, idx0, idx1, start, count):
    k = table.shape[1]
    per_w = count // (SC_CORES * SC_SUBCORES)
    assert per_w % SC_CHUNK == 0 and per_w * SC_CORES * SC_SUBCORES == count
    assert start % SC_CHUNK == 0
    rows = jax.ShapeDtypeStruct((count, k), table.dtype)

    @functools.partial(
        pl.kernel, mesh=_sc_mesh(), out_type=(rows, rows),
        scratch_types=[pltpu.VMEM((SC_CHUNK,), I32), pltpu.VMEM((SC_CHUNK,), I32),
                       pltpu.VMEM((SC_CHUNK, k), table.dtype), pltpu.VMEM((SC_CHUNK, k), table.dtype),
                       pltpu.SemaphoreType.DMA, pltpu.SemaphoreType.DMA,
                       pltpu.SemaphoreType.DMA, pltpu.SemaphoreType.DMA],
        name="sc_gather_pair")
    def gather(table_hbm, i0_hbm, i1_hbm, o0_hbm, o1_hbm, i0_v, i1_v, r0_v, r1_v,
               sem0, sem1, sem2, sem3):
        base = (lax.axis_index("s") * SC_CORES + lax.axis_index("c")) * per_w

        @pl.loop(0, per_w // SC_CHUNK)
        def _(c):
            dst = pl.multiple_of(base + c * SC_CHUNK, SC_CHUNK)
            src = pl.multiple_of(start + dst, SC_CHUNK)
            pltpu.sync_copy(i0_hbm.at[pl.ds(src, SC_CHUNK)], i0_v)
            pltpu.sync_copy(i1_hbm.at[pl.ds(src, SC_CHUNK)], i1_v)
            get0 = pltpu.async_copy(table_hbm.at[i0_v], r0_v, sem0)
            get1 = pltpu.async_copy(table_hbm.at[i1_v], r1_v, sem1)
            get0.wait()
            put0 = pltpu.async_copy(r0_v, o0_hbm.at[pl.ds(dst, SC_CHUNK)], sem2)
            get1.wait()
            put1 = pltpu.async_copy(r1_v, o1_hbm.at[pl.ds(dst, SC_CHUNK)], sem3)
            put0.wait()
            put1.wait()

    return gather(table, idx0, idx1)


def _combine_kernel(*refs, final):
    (ya_ref, yb_ref, h1_ref, meta_ref, p_ref, gp_ref, wpg_ref, bpg_ref, wp_ref,
     gf_ref) = refs[:10]
    out_ref = refs[-1]
    meta_t = meta_ref[...]
    pad = jnp.zeros((LANES - SUBLANES, meta_t.shape[1]), F32)
    meta = jnp.concatenate([meta_t, pad], axis=0).T
    w0 = meta[:, M_W0:M_W0 + 1]
    w1 = meta[:, M_W1:M_W1 + 1]
    ya_hi, ya_lo = _unpack_bf16_pairs(ya_ref[...])
    yb_hi, yb_lo = _unpack_bf16_pairs(yb_ref[...])
    y = jnp.concatenate([ya_hi * w0 + yb_hi * w1, ya_lo * w0 + yb_lo * w1], axis=1)
    h2 = h1_ref[...] + y
    xn = _rms(h2, gp_ref[...]).astype(BF16)
    gate = jax.nn.sigmoid(_dot(xn, wpg_ref[...]) + bpg_ref[...])
    pp = _dot(p_ref[...].astype(BF16), wp_ref[...])
    h3 = h2 + gate * pp
    if final:
        h3 = _rms(h3, gf_ref[...])
    out_ref[...] = h3


def _combine(dest, ys, h1, meta_t, p, layer, g_ple, w_ple_gate, b_ple_gate, w_ple, g_final, final):
    n, d = h1.shape
    pd = p.shape[2]
    k = ys.shape[1]
    n_part = n // COMBINE_PARTS
    tmc = _tile(n_part, 512)
    tiles = n_part // tmc
    vec = pl.BlockSpec((1, d), lambda i: (0, 0))
    out = None
    for q in range(COMBINE_PARTS):
        ya, yb = _sc_gather_pair(ys, dest[0], dest[1], q * n_part, n_part)

        def tok(i, q=q):
            return (q * tiles + i, 0)

        in_specs = [pl.BlockSpec((tmc, k), lambda i: (i, 0)),
                    pl.BlockSpec((tmc, k), lambda i: (i, 0)),
                    pl.BlockSpec((tmc, d), tok),
                    pl.BlockSpec((SUBLANES, tmc), lambda i, q=q: (0, q * tiles + i)),
                    pl.BlockSpec((None, tmc, pd), lambda i, q=q: (layer, q * tiles + i, 0)),
                    vec,
                    pl.BlockSpec((d, d), lambda i: (0, 0)),
                    vec,
                    pl.BlockSpec((pd, d), lambda i: (0, 0)),
                    vec]
        args = [ya, yb, h1, meta_t, p, g_ple, w_ple_gate, b_ple_gate, w_ple, g_final]
        aliases = {}
        if out is not None:
            in_specs.append(pl.BlockSpec(memory_space=pl.ANY))
            args.append(out)
            aliases = {len(args) - 1: 0}
        out = pl.pallas_call(
            functools.partial(_combine_kernel, final=final),
            grid=(tiles,),
            in_specs=in_specs,
            out_specs=pl.BlockSpec((tmc, d), tok),
            out_shape=jax.ShapeDtypeStruct((n, d), F32),
            input_output_aliases=aliases,
            compiler_params=_cparams("parallel"),
            name="combine_ple",
        )(*args)
    return out


def _row(v):
    return v.reshape(1, -1)


def _moe_and_ple(z, w_out, h, i, final, norm_ffn, w_router_group, b_router_group, w_router_expert,
                 b_router_expert, w_exp_gate, w_exp_up, w_exp_down, norm_ple, w_ple, w_ple_gate,
                 b_ple_gate, norm_final, p):
    n, d = h.shape
    pad = LANES - N_GROUPS - N_EXPERTS
    w_router = jnp.concatenate(
        [w_router_group[i], w_router_expert[i], jnp.zeros((d, pad), F32)], axis=1)
    b_router = jnp.concatenate(
        [b_router_group[i], b_router_expert[i], jnp.zeros((pad,), F32)]).reshape(1, LANES)
    h1, xn, meta_t, cnt = _post_mix(z, w_out, h, _row(norm_ffn[i]), w_router, b_router)
    cnt_i32 = cnt[0, EXPERT_LANE0:EXPERT_LANE0 + N_EXPERTS].astype(I32)
    nb_max = (2 * n) // ROUTE_ROWS + N_EXPERTS
    dest, estart, ecount, nblk = _plan(cnt_i32, meta_t)
    xs = _sc_scatter_rows(xn, dest[0], dest[1], nb_max * ROUTE_ROWS)
    ys = _experts(estart, ecount, nblk, cnt_i32, xs, w_exp_gate, w_exp_up, w_exp_down, i)
    return _combine(dest, ys, h1, meta_t, p.reshape(p.shape[0], n, -1), i, _row(norm_ple[i]),
                    w_ple_gate[i].astype(BF16), _row(b_ple_gate[i]), w_ple[i].astype(BF16),
                    _row(norm_final), final)


def kernel(x, p, norm_mix, w_in_a, conv_a, w_out_a, w_in_b, conv_b, conv_bias_b, w_rgate_b, b_rgate_b, w_igate_b, b_igate_b, lam_b, w_out_b, norm_ffn, w_router_group, b_router_group, w_router_expert, b_router_expert, w_exp_gate, w_exp_up, w_exp_down, norm_ple, w_ple, w_ple_gate, b_ple_gate, norm_final):
    batch, seq, d = x.shape
    depth = p.shape[0]
    n = batch * seq
    h = x.reshape(n, d)
    for i in range(depth):
        j = i // 2
        if i % 2 == 0:
            z = _mix_a(h, _row(norm_mix[i]), w_in_a[j].astype(BF16), conv_a[j], seq)
            w_out = w_out_a[j]
        else:
            tt = _tile(seq, 512)
            yb, u_tiles = _mix_b_in(h, _row(norm_mix[i]), w_in_b[j].astype(BF16), batch, seq, tt)
            z = _rglru(u_tiles, yb, conv_b[j], _row(conv_bias_b[j]),
                       (0.5 * w_rgate_b[j]).astype(BF16), b_rgate_b[j],
                       (0.5 * w_igate_b[j]).astype(BF16),
                       b_igate_b[j], lam_b[j], seq, tt)
            w_out = w_out_b[j]
        h = _moe_and_ple(z, w_out.astype(BF16), h, i, i == depth - 1, norm_ffn, w_router_group,
                         b_router_group, w_router_expert, b_router_expert, w_exp_gate, w_exp_up,
                         w_exp_down, norm_ple, w_ple, w_ple_gate, b_ple_gate, norm_final, p)
    return h.reshape(batch, seq, d)
```

```python
import functools

import jax
import jax.numpy as jnp
from jax import lax
from jax.experimental import pallas as pl
from jax.experimental.pallas import tpu as pltpu
from jax.experimental.pallas import tpu_sc as plsc

F32 = jnp.float32
BF16 = jnp.bfloat16
I32 = jnp.int32
U32 = jnp.uint32

EPS = 1e-6
N_GROUPS = 4
EXPERTS_PER_GROUP = 8
N_EXPERTS = N_GROUPS * EXPERTS_PER_GROUP
RG_C = 8.0

LANES = 128
SUBLANES = 8
EXPERT_LANE0 = N_GROUPS
ROUTE_ROWS = 256
IN_AHEAD = 4
IN_SLOTS = IN_AHEAD + 2
OUT_SLOTS = 4
SC_CORES = 2
SC_SUBCORES = 16
SC_CHUNK = 64
RING_SLOTS = 3
COMBINE_PARTS = 4
M_E0, M_E1, M_R0, M_R1, M_W0, M_W1 = 0, 1, 2, 3, 4, 5
VMEM_LIMIT = 48 * 1024 * 1024


def _cparams(*sem):
    return pltpu.CompilerParams(dimension_semantics=sem, vmem_limit_bytes=VMEM_LIMIT)


def _rms(x, g):
    return x * lax.rsqrt(jnp.mean(x * x, axis=-1, keepdims=True) + EPS) * g


def _dot(a, b):
    return jnp.dot(a, b, preferred_element_type=F32)


def _pack_bf16_pairs(x):
    k = x.shape[1] // 2
    hi = lax.bitcast_convert_type(x[:, :k].astype(BF16).astype(F32), U32)
    lo = lax.bitcast_convert_type(x[:, k:].astype(BF16).astype(F32), U32)
    return hi | (lo >> 16)


def _unpack_bf16_pairs(p):
    hi = lax.bitcast_convert_type(p & jnp.uint32(0xFFFF0000), F32)
    lo = lax.bitcast_convert_type(p << 16, F32)
    return hi, lo


def _tile(n, want):
    t = min(n, want)
    assert n % t == 0, (n, want)
    return t


def _mix_a_kernel(h_ref, g_ref, w_ref, cw_ref, z_ref, v_s, bg_s, last_s, *, tiles_per_seq):
    i = pl.program_id(0)
    tm, d = h_ref.shape

    @pl.when(i == 0)
    def _():
        v_s[...] = jnp.zeros_like(v_s)
        bg_s[...] = jnp.zeros_like(bg_s)
        last_s[...] = jnp.zeros_like(last_s)

    v = v_s[...].astype(F32)
    bg_old = bg_s[...]
    before = last_s[0:1, :]

    xn = _rms(h_ref[...], g_ref[...]).astype(BF16)
    cg = _dot(xn, w_ref[:, d:2 * d])
    hh = _dot(xn, w_ref[:, 2 * d:3 * d])
    v_new = (cg * hh).astype(BF16)
    bg = _dot(xn, w_ref[:, 0:d])

    old = i - 1
    at_start = (old % tiles_per_seq) == 0
    at_end = (old % tiles_per_seq) == tiles_per_seq - 1
    prev_row = jnp.where(at_start, 0.0, before)
    next_row = jnp.where(at_end, 0.0, v_new[0:1, :].astype(F32))
    row = lax.broadcasted_iota(I32, (tm, 1), 0)
    v_dn = jnp.where(row == 0, prev_row, pltpu.roll(v, 1, 0))
    v_up = jnp.where(row == tm - 1, next_row, pltpu.roll(v, tm - 1, 0))
    cw = cw_ref[...]
    u = cw[0:1, :] * v_dn + cw[1:2, :] * v + cw[2:3, :] * v_up
    z_ref[...] = (bg_old.astype(F32) * u).astype(BF16)

    last_s[0:1, :] = v[tm - 1:tm, :]
    v_s[...] = v_new
    bg_s[...] = bg.astype(BF16)


def _mix_a(h, g, w_in, conv_w, seq):
    n, d = h.shape
    tm = _tile(seq, 512)
    nt = n // tm
    return pl.pallas_call(
        functools.partial(_mix_a_kernel, tiles_per_seq=seq // tm),
        grid=(nt + 1,),
        in_specs=[pl.BlockSpec((tm, d), lambda i: (jnp.minimum(i, nt - 1), 0)),
                  pl.BlockSpec((1, d), lambda i: (0, 0)),
                  pl.BlockSpec((d, 3 * d), lambda i: (0, 0), pipeline_mode=pl.Buffered(1)),
                  pl.BlockSpec(conv_w.shape, lambda i: (0, 0))],
        out_specs=pl.BlockSpec((tm, d), lambda i: (jnp.maximum(i - 1, 0), 0)),
        out_shape=jax.ShapeDtypeStruct((n, d), BF16),
        scratch_shapes=[pltpu.VMEM((tm, d), BF16), pltpu.VMEM((tm, d), BF16),
                        pltpu.VMEM((SUBLANES, d), F32)],
        compiler_params=_cparams("arbitrary"),
        name="mix_a",
    )(h, g, w_in, conv_w)


def _mix_b_in_kernel(h_ref, hp_ref, hn_ref, g_ref, w_ref, yb_ref, u_ref, *, steps_per_seq, tt):
    i = pl.program_id(0)
    r = yb_ref.shape[1]
    tiles = h_ref.shape[0] // tt
    g = g_ref[...]
    xn = _rms(h_ref[...], g).astype(BF16)
    yb_ref[...] = jax.nn.gelu(_dot(xn, w_ref[:, 0:r])).astype(BF16)
    tm = h_ref.shape[0]
    xh = _rms(jnp.concatenate([hp_ref[...], hn_ref[...]], axis=0), g).astype(BF16)
    u_all = _dot(jnp.concatenate([xn, xh], axis=0), w_ref[:, r:2 * r])
    u = u_all[0:tm]
    uh = u_all[tm:tm + 2 * SUBLANES]
    at_start = (i % steps_per_seq) == 0
    at_end = (i % steps_per_seq) == steps_per_seq - 1
    prev2 = jnp.where(at_start, 0.0, uh[SUBLANES - 2:SUBLANES, :])
    next1 = jnp.where(at_end, 0.0, uh[SUBLANES:SUBLANES + 1, :])
    row = lax.broadcasted_iota(I32, (SUBLANES, 1), 0)
    for k in range(tiles):
        p2 = prev2 if k == 0 else u[k * tt - 2:k * tt]
        n1 = next1 if k == tiles - 1 else u[(k + 1) * tt:(k + 1) * tt + 1]
        pad = jnp.where(row == 0, p2[0:1], jnp.where(row == 1, p2[1:2], jnp.where(row == 2, n1, 0.0)))
        for s in range(r // LANES):
            lanes = slice(s * LANES, (s + 1) * LANES)
            u_ref[k, s, 0:tt, :] = u[k * tt:(k + 1) * tt, lanes]
            u_ref[k, s, tt:tt + SUBLANES, :] = pad[:, lanes]


def _mix_b_in(h, g, w_in, batch, seq, tt):
    n, d = h.shape
    r = w_in.shape[1] // 2
    tm = _tile(seq, 1024)
    steps_per_seq = seq // tm
    tiles = tm // tt
    hb = tm // SUBLANES
    nhalo = n // SUBLANES
    return pl.pallas_call(
        functools.partial(_mix_b_in_kernel, steps_per_seq=steps_per_seq, tt=tt),
        grid=(n // tm,),
        in_specs=[pl.BlockSpec((tm, d), lambda i: (i, 0)),
                  pl.BlockSpec((SUBLANES, d), lambda i: (jnp.maximum(i * hb - 1, 0), 0)),
                  pl.BlockSpec((SUBLANES, d), lambda i: (jnp.minimum((i + 1) * hb, nhalo - 1), 0)),
                  pl.BlockSpec((1, d), lambda i: (0, 0)),
                  pl.BlockSpec((d, 2 * r), lambda i: (0, 0), pipeline_mode=pl.Buffered(1))],
        out_specs=[pl.BlockSpec((tm, r), lambda i: (i, 0)),
                   pl.BlockSpec((tiles, r // LANES, None, tt + SUBLANES, LANES),
                                lambda i: (i % steps_per_seq, 0, i // steps_per_seq, 0, 0))],
        out_shape=[jax.ShapeDtypeStruct((n, r), BF16),
                   jax.ShapeDtypeStruct((seq // tt, r // LANES, batch, tt + SUBLANES, LANES), F32)],
        compiler_params=_cparams("parallel"),
        name="mix_b_in",
    )(h, h, h, g, w_in)


def _rglru_gates(u, wr_ref, br_ref, wi_ref, bi_ref, lam_ref):
    ub = u.astype(BF16)
    t_r = jnp.tanh(_dot(ub, wr_ref[...]) + 0.5 * br_ref[...])
    t_i = jnp.tanh(_dot(ub, wi_ref[...]) + 0.5 * bi_ref[...])
    c = (-0.25 * RG_C) * jax.nn.softplus(-lam_ref[...])
    th = jnp.tanh(c + c * t_r)
    q = 1.0 / (1.0 - th)
    nth = -th
    root = jnp.where(nth > 0.0, nth * lax.rsqrt(nth), 0.0)
    return (1.0 + th) * q, (q * root) * (u + u * t_i)


def _rglru_fwd_kernel(u_ref, cw_ref, cb_ref, wr_ref, br_ref, wi_ref, bi_ref, lam_ref,
                      hf_ref, uc_ref, x_s, carry_s, *, tt):
    nb = SUBLANES
    ttp = tt + SUBLANES
    rows = tt * nb
    slabs = u_ref.shape[0]

    @pl.when(pl.program_id(1) == 0)
    def _():
        carry_s[...] = jnp.zeros_like(carry_s)

    def gather_t(tl, dst):
        for s in range(slabs):
            x_s[pl.ds(dst, nb), s * LANES:(s + 1) * LANES] = u_ref[s, pl.ds(tl, nb, stride=ttp), :]

    for tl in range(tt):
        gather_t(tl, (tl + 2) * nb)
    gather_t(tt, 0)
    gather_t(tt + 1, nb)
    gather_t(tt + 2, (tt + 2) * nb)

    cw = cw_ref[...]
    u = (cw[0:1, :] * x_s[0:rows, :] + cw[1:2, :] * x_s[nb:nb + rows, :]
         + cw[2:3, :] * x_s[2 * nb:2 * nb + rows, :] + cw[3:4, :] * x_s[3 * nb:3 * nb + rows, :]
         + cb_ref[...])
    a, g = _rglru_gates(u, wr_ref, br_ref, wi_ref, bi_ref, lam_ref)
    h = carry_s[...]
    for tl in range(tt):
        r = slice(tl * nb, (tl + 1) * nb)
        back = slice((tt - 1 - tl) * nb, (tt - tl) * nb)
        h = a[r] * h + g[r]
        hf_ref[back, :] = h.astype(BF16)
        uc_ref[back, :] = u[r].astype(BF16)
    carry_s[...] = h


def _rglru_rev_kernel(uc_ref, wr_ref, br_ref, wi_ref, bi_ref, lam_ref, hf_ref, yb_ref,
                      z_ref, carry_s, nat_s, *, tt):
    nb = SUBLANES
    ttp = tt + SUBLANES
    slabs = nat_s.shape[0]

    @pl.when(pl.program_id(1) == 0)
    def _():
        carry_s[...] = jnp.zeros_like(carry_s)

    a, g = _rglru_gates(uc_ref[...].astype(F32), wr_ref, br_ref, wi_ref, bi_ref, lam_ref)
    h = carry_s[...]
    for k in range(tt):
        r = slice(k * nb, (k + 1) * nb)
        h = a[r] * h + g[r]
        hs = h + hf_ref[r, :].astype(F32)
        for s in range(slabs):
            nat_s[s, pl.ds(tt - 1 - k, nb, stride=ttp), :] = hs[:, s * LANES:(s + 1) * LANES]
    carry_s[...] = h
    for b in range(nb):
        for s in range(slabs):
            lanes = slice(s * LANES, (s + 1) * LANES)
            hb = nat_s[s, b * ttp:b * ttp + tt, :]
            z_ref[b, :, lanes] = (yb_ref[b, :, lanes].astype(F32) * hb).astype(BF16)


def _rglru(u_tiles, yb, conv_w, conv_b, w_r, b_r, w_i, b_i, lam, seq, tt):
    nt, nslab, batch, ttp, _ = u_tiles.shape
    r = nslab * LANES
    assert batch == SUBLANES, "time-major tile layout puts the batch on the 8 sublanes"
    nh, bk = w_r.shape[1], w_r.shape[2]
    slabs = bk // LANES
    kw = conv_w.shape[0]
    rows = tt * batch
    gates = [w_r, b_r.reshape(2, 1, r), w_i, b_i.reshape(2, 1, r), lam.reshape(2, 1, r)]

    def gate_specs(d):
        vec = pl.BlockSpec((None, 1, bk), lambda h, jj: (d, 0, h))
        mat = pl.BlockSpec((None, None, bk, bk), lambda h, jj: (d, h, 0, 0))
        return [mat, vec, mat, vec, vec]

    tm_spec = pl.BlockSpec((rows, bk), lambda h, jj: (jj, h))
    tm_rev = pl.BlockSpec((rows, bk), lambda h, jj: (nt - 1 - jj, h))
    tm_shape = jax.ShapeDtypeStruct((seq * batch, r), BF16)
    state = [pltpu.VMEM((batch, bk), F32)]

    hf, uc = pl.pallas_call(
        functools.partial(_rglru_fwd_kernel, tt=tt),
        grid=(nh, nt),
        in_specs=[pl.BlockSpec((None, slabs, batch * ttp, LANES), lambda h, jj: (jj, h, 0, 0)),
                  pl.BlockSpec((kw, bk), lambda h, jj: (0, h)),
                  pl.BlockSpec((1, bk), lambda h, jj: (0, h))] + gate_specs(0),
        out_specs=[tm_spec, tm_spec],
        out_shape=[tm_shape, tm_shape],
        scratch_shapes=[pltpu.VMEM(((tt + 3) * batch, bk), F32)] + state,
        compiler_params=_cparams("parallel", "arbitrary"),
        name="rglru_fwd",
    )(u_tiles.reshape(nt, nslab, batch * ttp, LANES), conv_w, conv_b, *gates)

    nat_spec = pl.BlockSpec((batch, tt, bk), lambda h, jj: (0, nt - 1 - jj, h))
    z = pl.pallas_call(
        functools.partial(_rglru_rev_kernel, tt=tt),
        grid=(nh, nt),
        in_specs=[tm_rev] + gate_specs(1) + [tm_rev, nat_spec],
        out_specs=nat_spec,
        out_shape=jax.ShapeDtypeStruct((batch, seq, r), BF16),
        scratch_shapes=state + [pltpu.VMEM((slabs, batch * ttp, LANES), F32)],
        compiler_params=_cparams("parallel", "arbitrary"),
        name="rglru_rev",
    )(uc, *gates, hf, yb.reshape(batch, seq, r))
    return z.reshape(-1, r)


def _route(logits):
    lane = lax.broadcasted_iota(I32, logits.shape, 1).astype(F32)
    neg = jnp.float32(-jnp.inf)
    nolane = jnp.float32(LANES)
    gmask = lane < N_GROUPS
    gmax = jnp.max(jnp.where(gmask, logits, neg), axis=1, keepdims=True)
    eg = jnp.where(gmask, jnp.exp(logits - gmax), 0.0)
    gsum = jnp.sum(eg, axis=1, keepdims=True)
    pg = eg / gsum
    pg_top = 1.0 / gsum
    g_top = jnp.min(jnp.where(gmask & (pg == pg_top), lane, nolane), axis=1, keepdims=True)
    lo = EXPERT_LANE0 + g_top * EXPERTS_PER_GROUP
    emask = (lane >= lo) & (lane < lo + EXPERTS_PER_GROUP)
    emax = jnp.max(jnp.where(emask, logits, neg), axis=1, keepdims=True)
    ee = jnp.where(emask, jnp.exp(logits - emax), 0.0)
    esum = jnp.sum(ee, axis=1, keepdims=True)
    pe = jnp.where(emask, ee / esum, -1.0)
    p1 = 1.0 / esum
    i1 = jnp.min(jnp.where(pe == p1, lane, nolane), axis=1, keepdims=True)
    pe2 = jnp.where(lane == i1, -1.0, pe)
    p2 = jnp.max(pe2, axis=1, keepdims=True)
    i2 = jnp.min(jnp.where(pe2 == p2, lane, nolane), axis=1, keepdims=True)
    psum = p1 + p2
    return i1, i2, pg_top * (p1 / psum), pg_top * (p2 / psum)


def _post_mix_kernel(z_hbm, wo_ref, h_hbm, g_ref, wr_ref, br_ref,
                     h1_ref, xn_ref, meta_t_ref, cnt_ref, carry_ref, logits_s,
                     z_ring, h_ring, sem, *, nt):
    i = pl.program_id(0)
    tm = z_ring.shape[1]

    def tile_copies(t):
        slot = t % RING_SLOTS
        rows = pl.ds(pl.multiple_of(t * tm, tm), tm)
        return (pltpu.make_async_copy(z_hbm.at[rows], z_ring.at[slot], sem.at[0, slot]),
                pltpu.make_async_copy(h_hbm.at[rows], h_ring.at[slot], sem.at[1, slot]))

    def start_tile(t):
        for c in tile_copies(t):
            c.start()

    @pl.when(i == 0)
    def _():
        logits_s[...] = jnp.zeros_like(logits_s)
        for t in range(min(RING_SLOTS - 1, nt)):
            start_tile(t)

    @pl.when(i + RING_SLOTS - 1 < nt)
    def _():
        start_tile(i + RING_SLOTS - 1)

    @pl.when(i < nt)
    def _():
        for c in tile_copies(i):
            c.wait()

    slot = jnp.minimum(i, nt - 1) % RING_SLOTS
    z_ref = z_ring.at[slot]
    h_ref = h_ring.at[slot]

    @pl.when(i <= 1)
    def _():
        carry_ref[...] = jnp.zeros_like(carry_ref)

    logits = logits_s[...]

    h1 = h_ref[...] + _dot(z_ref[...], wo_ref[...])
    h1_ref[...] = h1
    xn = _rms(h1, g_ref[...])
    xn_ref[...] = _pack_bf16_pairs(xn)
    logits_s[...] = _dot(xn.astype(BF16), wr_ref[...].astype(BF16)) + br_ref[...]

    i1, i2, w0, w1 = _route(logits)

    lane = lax.broadcasted_iota(I32, (tm, LANES), 1).astype(F32)
    oh0 = lane == i1
    oh1 = lane == i2
    both = (oh0 | oh1).astype(BF16)
    ri = lax.broadcasted_iota(I32, (tm, tm), 0)
    ci = lax.broadcasted_iota(I32, (tm, tm), 1)
    before = (ri > ci).astype(BF16)
    cnt_before = _dot(before, both) + carry_ref[...]
    rank0 = jnp.sum(jnp.where(oh0, cnt_before, 0.0), axis=1, keepdims=True)
    rank1 = jnp.sum(jnp.where(oh1, cnt_before, 0.0), axis=1, keepdims=True)
    carry = carry_ref[...] + jnp.sum(both.astype(F32), axis=0, keepdims=True)
    carry_ref[...] = carry
    cnt_ref[...] = carry

    e0 = i1 - EXPERT_LANE0
    e1 = i2 - EXPERT_LANE0
    meta = jnp.zeros((tm, LANES), F32)
    for ln, val in ((M_E0, e0), (M_E1, e1), (M_R0, rank0), (M_R1, rank1), (M_W0, w0), (M_W1, w1)):
        meta = jnp.where(lane == ln, val, meta)
    meta_t_ref[...] = meta.T[0:SUBLANES, :]


def _post_mix(z, w_out, h, g, w_router, b_router):
    n, d = h.shape
    k = z.shape[1]
    tm = _tile(n, 512)
    nt = n // tm

    def proj(i):
        return (jnp.minimum(i, nt - 1), 0)

    def routed(i):
        return (0, jnp.maximum(i - 1, 0))

    return pl.pallas_call(
        functools.partial(_post_mix_kernel, nt=nt),
        grid=(nt + 1,),
        in_specs=[pl.BlockSpec(memory_space=pl.ANY),
                  pl.BlockSpec((k, d), lambda i: (0, 0)),
                  pl.BlockSpec(memory_space=pl.ANY),
                  pl.BlockSpec((1, d), lambda i: (0, 0)),
                  pl.BlockSpec((d, LANES), lambda i: (0, 0)),
                  pl.BlockSpec((1, LANES), lambda i: (0, 0))],
        out_specs=[pl.BlockSpec((tm, d), proj),
                   pl.BlockSpec((tm, d // 2), proj),
                   pl.BlockSpec((SUBLANES, tm), routed),
                   pl.BlockSpec((1, LANES), lambda i: (0, 0))],
        out_shape=[jax.ShapeDtypeStruct((n, d), F32), jax.ShapeDtypeStruct((n, d // 2), U32),
                   jax.ShapeDtypeStruct((SUBLANES, n), F32),
                   jax.ShapeDtypeStruct((1, LANES), F32)],
        scratch_shapes=[pltpu.VMEM((1, LANES), F32), pltpu.VMEM((tm, LANES), F32),
                        pltpu.VMEM((RING_SLOTS, tm, k), z.dtype),
                        pltpu.VMEM((RING_SLOTS, tm, d), F32),
                        pltpu.SemaphoreType.DMA((2, RING_SLOTS))],
        compiler_params=_cparams("arbitrary"),
        name="post_mix_router",
    )(z, w_out, h, g, w_router, b_router)


def _plan_kernel(cnt_ref, meta_t_ref, dest_ref, estart_ref, ecount_ref, nblk_ref, pstart_ref):
    def per_expert(e, acc):
        nb_e = (cnt_ref[e] + (ROUTE_ROWS - 1)) // ROUTE_ROWS
        pstart_ref[e] = acc * ROUTE_ROWS
        estart_ref[e] = acc
        ecount_ref[e] = nb_e
        return acc + nb_e

    nblk_ref[0] = lax.fori_loop(0, N_EXPERTS, per_expert, 0)

    e0 = meta_t_ref[M_E0:M_E0 + 1, :]
    e1 = meta_t_ref[M_E1:M_E1 + 1, :]
    d0 = meta_t_ref[M_R0:M_R0 + 1, :]
    d1 = meta_t_ref[M_R1:M_R1 + 1, :]
    for e in range(N_EXPERTS):
        ps = pstart_ref[e].astype(F32)
        d0 = d0 + jnp.where(e0 == e, ps, 0.0)
        d1 = d1 + jnp.where(e1 == e, ps, 0.0)
    dest_ref[...] = jnp.zeros_like(dest_ref)
    dest_ref[0:1, :] = d0.astype(I32)
    dest_ref[1:2, :] = d1.astype(I32)


def _plan(cnt_i32, meta_t):
    n = meta_t.shape[1]
    smem = pl.BlockSpec(memory_space=pltpu.SMEM)
    return pl.pallas_call(
        _plan_kernel,
        in_specs=[smem, pl.BlockSpec((SUBLANES, n), lambda: (0, 0))],
        out_specs=[pl.BlockSpec((SUBLANES, n), lambda: (0, 0)), smem, smem, smem],
        out_shape=[jax.ShapeDtypeStruct((SUBLANES, n), I32),
                   jax.ShapeDtypeStruct((N_EXPERTS,), I32),
                   jax.ShapeDtypeStruct((N_EXPERTS,), I32),
                   jax.ShapeDtypeStruct((1,), I32)],
        scratch_shapes=[pltpu.SMEM((N_EXPERTS,), I32)],
        name="dispatch_plan",
    )(cnt_i32, meta_t)


def _sc_mesh():
    return plsc.VectorSubcoreMesh(core_axis_name="c", subcore_axis_name="s",
                                  num_cores=SC_CORES, num_subcores=SC_SUBCORES)


def _sc_scatter_rows(x, idx0, idx1, p_rows):
    m, k = x.shape
    per_w = m // (SC_CORES * SC_SUBCORES)
    assert per_w % SC_CHUNK == 0 and per_w * SC_CORES * SC_SUBCORES == m

    nchunk = per_w // SC_CHUNK
    buf = [pltpu.VMEM((SC_CHUNK,), I32), pltpu.VMEM((SC_CHUNK,), I32),
           pltpu.VMEM((SC_CHUNK, k), x.dtype),
           pltpu.SemaphoreType.DMA, pltpu.SemaphoreType.DMA, pltpu.SemaphoreType.DMA]

    @functools.partial(
        pl.kernel, mesh=_sc_mesh(), out_type=jax.ShapeDtypeStruct((p_rows, k), x.dtype),
        scratch_types=buf + buf, name="sc_scatter_rows")
    def scatter(x_hbm, i0_hbm, i1_hbm, out_hbm, *scratch):
        base = (lax.axis_index("s") * SC_CORES + lax.axis_index("c")) * per_w
        slots = (scratch[:6], scratch[6:])

        def loads(c):
            i0_v, i1_v, rows_v, lsem, _, _ = slots[c % 2]
            off = pl.multiple_of(base + c * SC_CHUNK, SC_CHUNK)
            return [pltpu.make_async_copy(i0_hbm.at[pl.ds(off, SC_CHUNK)], i0_v, lsem),
                    pltpu.make_async_copy(i1_hbm.at[pl.ds(off, SC_CHUNK)], i1_v, lsem),
                    pltpu.make_async_copy(x_hbm.at[pl.ds(off, SC_CHUNK)], rows_v, lsem)]

        def puts(c):
            i0_v, i1_v, rows_v, _, psem0, psem1 = slots[c % 2]
            return [pltpu.make_async_copy(rows_v, out_hbm.at[i0_v], psem0),
                    pltpu.make_async_copy(rows_v, out_hbm.at[i1_v], psem1)]

        for d in loads(0):
            d.start()
        for c in range(nchunk):
            for d in loads(c):
                d.wait()
            for d in puts(c):
                d.start()
            if c >= 1:
                for d in puts(c - 1):
                    d.wait()
            if c + 1 < nchunk:
                for d in loads(c + 1):
                    d.start()
        for d in puts(nchunk - 1):
            d.wait()

    return scatter(x, idx0, idx1)


def _expert_kernel(es_ref, ec_ref, nblk_ref, cnt_ref, xs_ref, wg_ref, wu_ref, wd_ref, ys_ref,
                   x_s, y_s, wg_s, wu_s, wd_s, sem):
    e = pl.program_id(0)
    nb = ec_ref[e]
    b0 = es_ref[e]
    k = x_s.shape[2]
    nb_max = ys_ref.shape[0] // ROUTE_ROWS

    def rows(b):
        return pl.ds(pl.multiple_of(b * ROUTE_ROWS, ROUTE_ROWS), ROUTE_ROWS)

    nblk = nblk_ref[0]

    def copy_in(b):
        slot = lax.rem(b, IN_SLOTS)
        return pltpu.make_async_copy(xs_ref.at[rows(b)], x_s.at[slot], sem.at[0, slot])

    def copy_out(b):
        slot = lax.rem(b, OUT_SLOTS)
        return pltpu.make_async_copy(y_s.at[slot], ys_ref.at[rows(b)], sem.at[1, slot])

    @pl.when(e == 0)
    def _():
        for b in range(IN_AHEAD):
            @pl.when(b < nblk)
            def _(b=b):
                copy_in(b).start()

    @pl.when(nb > 0)
    def _():
        wg_s[...] = wg_ref[...].astype(BF16)
        wu_s[...] = wu_ref[...].astype(BF16)
        wd_s[...] = wd_ref[...].astype(BF16)

        def mlp(b, j, width):
            row = lax.broadcasted_iota(I32, (ROUTE_ROWS, 1), 0)
            xp = jnp.concatenate(
                [jnp.where(row < cnt_ref[e] - (j + i) * ROUTE_ROWS,
                           x_s[lax.rem(b + i, IN_SLOTS)], jnp.uint32(0)) for i in range(width)], axis=0)
            x_hi, x_lo = _unpack_bf16_pairs(xp)
            x_hi = x_hi.astype(BF16)
            x_lo = x_lo.astype(BF16)
            g = _dot(x_hi, wg_s[0:k, :]) + _dot(x_lo, wg_s[k:2 * k, :])
            u = _dot(x_hi, wu_s[0:k, :]) + _dot(x_lo, wu_s[k:2 * k, :])
            hb = (jax.nn.silu(g) * u).astype(BF16)
            y = _pack_bf16_pairs(_dot(hb, wd_s[...]))
            return [y[i * ROUTE_ROWS:(i + 1) * ROUTE_ROWS] for i in range(width)]

        def unit(j, width):
            b = b0 + j
            for i in range(width):
                copy_in(b + i).wait()
            for i in range(width):
                @pl.when(b + i + IN_AHEAD < nblk)
                def _(i=i):
                    copy_in(b + i + IN_AHEAD).start()
            ys = mlp(b, j, width)
            for i in range(width):
                @pl.when(b + i >= OUT_SLOTS)
                def _(i=i):
                    copy_out(b + i - OUT_SLOTS).wait()
            for i in range(width):
                y_s[lax.rem(b + i, OUT_SLOTS)] = ys[i]
                copy_out(b + i).start()

        def pair(jj, c):
            unit(2 * jj, 2)
            return c

        lax.fori_loop(0, nb // 2, pair, 0)

        @pl.when(nb % 2 == 1)
        def _():
            unit(nb - 1, 1)

    @pl.when(e == pl.num_programs(0) - 1)
    def _():
        for back in range(OUT_SLOTS, 0, -1):
            @pl.when(nblk >= back)
            def _(back=back):
                copy_out(nblk - back).wait()

    @pl.when(e == pl.num_programs(0) - 1)
    def _():
        y_s[0] = jnp.zeros_like(y_s[0])

        def zero(b):
            return pltpu.make_async_copy(y_s.at[0], ys_ref.at[rows(b)], sem.at[1, 0])

        def start_zero(b, c):
            zero(b).start()
            return c

        def wait_zero(b, c):
            zero(b).wait()
            return c

        lax.fori_loop(nblk_ref[0], nb_max, start_zero, 0)
        lax.fori_loop(nblk_ref[0], nb_max, wait_zero, 0)


def _experts(estart, ecount, nblk, cnt, xs, w_gate, w_up, w_down, layer):
    p, k = xs.shape
    d, f = w_gate.shape[2], w_gate.shape[3]

    def w_map(e, es, ec, nb, cn):
        return (layer, e, 0, 0)

    grid_spec = pltpu.PrefetchScalarGridSpec(
        num_scalar_prefetch=4,
        grid=(N_EXPERTS,),
        in_specs=[pl.BlockSpec(memory_space=pl.ANY),
                  pl.BlockSpec((None, None, d, f), w_map),
                  pl.BlockSpec((None, None, d, f), w_map),
                  pl.BlockSpec((None, None, f, d), w_map)],
        out_specs=pl.BlockSpec(memory_space=pl.ANY),
        scratch_shapes=[pltpu.VMEM((IN_SLOTS, ROUTE_ROWS, k), U32),
                        pltpu.VMEM((OUT_SLOTS, ROUTE_ROWS, k), U32),
                        pltpu.VMEM((d, f), BF16), pltpu.VMEM((d, f), BF16), pltpu.VMEM((f, d), BF16),
                        pltpu.SemaphoreType.DMA((2, max(IN_SLOTS, OUT_SLOTS)))],
    )
    return pl.pallas_call(
        _expert_kernel,
        grid_spec=grid_spec,
        out_shape=jax.ShapeDtypeStruct((p, k), U32),
        compiler_params=_cparams("arbitrary"),
        name="expert_mlp",
    )(estart, ecount, nblk, cnt, xs, w_gate, w_up, w_down)


def _sc_gather_pair(table, idx0, idx1, start, count):
    k = table.shape[1]
    per_w = count // (SC_CORES * SC_SUBCORES)
    assert per_w % SC_CHUNK == 0 and per_w * SC_CORES * SC_SUBCORES == count
    assert start % SC_CHUNK == 0
    rows = jax.ShapeDtypeStruct((count, k), table.dtype)

    @functools.partial(
        pl.kernel, mesh=_sc_mesh(), out_type=(rows, rows),
        scratch_types=[pltpu.VMEM((SC_CHUNK,), I32), pltpu.VMEM((SC_CHUNK,), I32),
                       pltpu.VMEM((SC_CHUNK, k), table.dtype), pltpu.VMEM((SC_CHUNK, k), table.dtype),
                       pltpu.SemaphoreType.DMA, pltpu.SemaphoreType.DMA,
                       pltpu.SemaphoreType.DMA, pltpu.SemaphoreType.DMA],
        name="sc_gather_pair")
    def gather(table_hbm, i0_hbm, i1_hbm, o0_hbm, o1_hbm, i0_v, i1_v, r0_v, r1_v,
               sem0, sem1, sem2, sem3):
        base = (lax.axis_index("s") * SC_CORES + lax.axis_index("c")) * per_w

        @pl.loop(0, per_w // SC_CHUNK)
        def _(c):
            dst = pl.multiple_of(base + c * SC_CHUNK, SC_CHUNK)
            src = pl.multiple_of(start + dst, SC_CHUNK)
            pltpu.sync_copy(i0_hbm.at[pl.ds(src, SC_CHUNK)], i0_v)
            pltpu.sync_copy(i1_hbm.at[pl.ds(src, SC_CHUNK)], i1_v)
            get0 = pltpu.async_copy(table_hbm.at[i0_v], r0_v, sem0)
            get1 = pltpu.async_copy(table_hbm.at[i1_v], r1_v, sem1)
            get0.wait()
            put0 = pltpu.async_copy(r0_v, o0_hbm.at[pl.ds(dst, SC_CHUNK)], sem2)
            get1.wait()
            put1 = pltpu.async_copy(r1_v, o1_hbm.at[pl.ds(dst, SC_CHUNK)], sem3)
            put0.wait()
            put1.wait()

    return gather(table, idx0, idx1)


def _combine_kernel(*refs, final, first, tiles):
    (ya_hbm, yb_hbm, h1_hbm, meta_ref, p_ref, gp_ref, wpg_ref, bpg_ref, wp_ref,
     gf_ref) = refs[:10]
    out_ref, ya_ring, yb_ring, h1_ring, sem = refs[-5:]
    i = pl.program_id(0)
    tmc = h1_ring.shape[1]

    def tile_copies(t):
        slot = t % RING_SLOTS
        part_rows = pl.ds(pl.multiple_of(t * tmc, tmc), tmc)
        rows = pl.ds(pl.multiple_of((first + t) * tmc, tmc), tmc)
        return (pltpu.make_async_copy(ya_hbm.at[part_rows], ya_ring.at[slot], sem.at[0, slot]),
                pltpu.make_async_copy(yb_hbm.at[part_rows], yb_ring.at[slot], sem.at[1, slot]),
                pltpu.make_async_copy(h1_hbm.at[rows], h1_ring.at[slot], sem.at[2, slot]))

    @pl.when(i == 0)
    def _():
        for t in range(min(RING_SLOTS - 1, tiles)):
            for c in tile_copies(t):
                c.start()

    @pl.when(i + RING_SLOTS - 1 < tiles)
    def _():
        for c in tile_copies(i + RING_SLOTS - 1):
            c.start()

    for c in tile_copies(i):
        c.wait()
    slot = i % RING_SLOTS
    ya_ref, yb_ref, h1_ref = ya_ring.at[slot], yb_ring.at[slot], h1_ring.at[slot]
    meta_t = meta_ref[...]
    pad = jnp.zeros((LANES - SUBLANES, meta_t.shape[1]), F32)
    meta = jnp.concatenate([meta_t, pad], axis=0).T
    w0 = meta[:, M_W0:M_W0 + 1]
    w1 = meta[:, M_W1:M_W1 + 1]
    ya_hi, ya_lo = _unpack_bf16_pairs(ya_ref[...])
    yb_hi, yb_lo = _unpack_bf16_pairs(yb_ref[...])
    y = jnp.concatenate([ya_hi * w0 + yb_hi * w1, ya_lo * w0 + yb_lo * w1], axis=1)
    h2 = h1_ref[...] + y
    xn = _rms(h2, gp_ref[...]).astype(BF16)
    gate = jax.nn.sigmoid(_dot(xn, wpg_ref[...]) + bpg_ref[...])
    pp = _dot(p_ref[...].astype(BF16), wp_ref[...])
    h3 = h2 + gate * pp
    if final:
        h3 = _rms(h3, gf_ref[...])
    out_ref[...] = h3


def _combine(dest, ys, h1, meta_t, p, layer, g_ple, w_ple_gate, b_ple_gate, w_ple, g_final, final):
    n, d = h1.shape
    pd = p.shape[2]
    k = ys.shape[1]
    n_part = n // COMBINE_PARTS
    tmc = _tile(n_part, 512)
    tiles = n_part // tmc
    vec = pl.BlockSpec((1, d), lambda i: (0, 0))
    out = None
    for q in range(COMBINE_PARTS):
        ya, yb = _sc_gather_pair(ys, dest[0], dest[1], q * n_part, n_part)

        def tok(i, q=q):
            return (q * tiles + i, 0)

        in_specs = [pl.BlockSpec(memory_space=pl.ANY),
                    pl.BlockSpec(memory_space=pl.ANY),
                    pl.BlockSpec(memory_space=pl.ANY),
                    pl.BlockSpec((SUBLANES, tmc), lambda i, q=q: (0, q * tiles + i)),
                    pl.BlockSpec((None, tmc, pd), lambda i, q=q: (layer, q * tiles + i, 0)),
                    vec,
                    pl.BlockSpec((d, d), lambda i: (0, 0)),
                    vec,
                    pl.BlockSpec((pd, d), lambda i: (0, 0)),
                    vec]
        args = [ya, yb, h1, meta_t, p, g_ple, w_ple_gate, b_ple_gate, w_ple, g_final]
        aliases = {}
        if out is not None:
            in_specs.append(pl.BlockSpec(memory_space=pl.ANY))
            args.append(out)
            aliases = {len(args) - 1: 0}
        out = pl.pallas_call(
            functools.partial(_combine_kernel, final=final, first=q * tiles, tiles=tiles),
            grid=(tiles,),
            in_specs=in_specs,
            out_specs=pl.BlockSpec((tmc, d), tok),
            out_shape=jax.ShapeDtypeStruct((n, d), F32),
            input_output_aliases=aliases,
            scratch_shapes=[pltpu.VMEM((RING_SLOTS, tmc, k), ys.dtype),
                            pltpu.VMEM((RING_SLOTS, tmc, k), ys.dtype),
                            pltpu.VMEM((RING_SLOTS, tmc, d), F32),
                            pltpu.SemaphoreType.DMA((3, RING_SLOTS))],
            compiler_params=_cparams("arbitrary"),
            name="combine_ple",
        )(*args)
    return out


def _row(v):
    return v.reshape(1, -1)


def _moe_and_ple(z, w_out, h, i, final, norm_ffn, w_router_group, b_router_group, w_router_expert,
                 b_router_expert, w_exp_gate, w_exp_up, w_exp_down, norm_ple, w_ple, w_ple_gate,
                 b_ple_gate, norm_final, p):
    n, d = h.shape
    pad = LANES - N_GROUPS - N_EXPERTS
    w_router = jnp.concatenate(
        [w_router_group[i], w_router_expert[i], jnp.zeros((d, pad), F32)], axis=1)
    b_router = jnp.concatenate(
        [b_router_group[i], b_router_expert[i], jnp.zeros((pad,), F32)]).reshape(1, LANES)
    h1, xn, meta_t, cnt = _post_mix(z, w_out, h, _row(norm_ffn[i]), w_router, b_router)
    cnt_i32 = cnt[0, EXPERT_LANE0:EXPERT_LANE0 + N_EXPERTS].astype(I32)
    nb_max = (2 * n) // ROUTE_ROWS + N_EXPERTS
    dest, estart, ecount, nblk = _plan(cnt_i32, meta_t)
    xs = _sc_scatter_rows(xn, dest[0], dest[1], nb_max * ROUTE_ROWS)
    ys = _experts(estart, ecount, nblk, cnt_i32, xs, w_exp_gate, w_exp_up, w_exp_down, i)
    return _combine(dest, ys, h1, meta_t, p.reshape(p.shape[0], n, -1), i, _row(norm_ple[i]),
                    w_ple_gate[i].astype(BF16), _row(b_ple_gate[i]), w_ple[i].astype(BF16),
                    _row(norm_final), final)


def kernel(x, p, norm_mix, w_in_a, conv_a, w_out_a, w_in_b, conv_b, conv_bias_b, w_rgate_b, b_rgate_b, w_igate_b, b_igate_b, lam_b, w_out_b, norm_ffn, w_router_group, b_router_group, w_router_expert, b_router_expert, w_exp_gate, w_exp_up, w_exp_down, norm_ple, w_ple, w_ple_gate, b_ple_gate, norm_final):
    batch, seq, d = x.shape
    depth = p.shape[0]
    n = batch * seq
    h = x.reshape(n, d)
    for i in range(depth):
        j = i // 2
        if i % 2 == 0:
            z = _mix_a(h, _row(norm_mix[i]), w_in_a[j].astype(BF16), conv_a[j], seq)
            w_out = w_out_a[j]
        else:
            tt = _tile(seq, 512)
            yb, u_tiles = _mix_b_in(h, _row(norm_mix[i]), w_in_b[j].astype(BF16), batch, seq, tt)
            z = _rglru(u_tiles, yb, conv_b[j], _row(conv_bias_b[j]),
                       (0.5 * w_rgate_b[j]).astype(BF16), b_rgate_b[j],
                       (0.5 * w_igate_b[j]).astype(BF16),
                       b_igate_b[j], lam_b[j], seq, tt)
            w_out = w_out_b[j]
        h = _moe_and_ple(z, w_out.astype(BF16), h, i, i == depth - 1, norm_ffn, w_router_group,
                         b_router_group, w_router_expert, b_router_expert, w_exp_gate, w_exp_up,
                         w_exp_down, norm_ple, w_ple, w_ple_gate, b_ple_gate, norm_final, p)
    return h.reshape(batch, seq, d)
```

```python
import functools

import jax
import jax.numpy as jnp
from jax import lax
from jax.experimental import pallas as pl
from jax.experimental.pallas import tpu as pltpu
from jax.experimental.pallas import tpu_sc as plsc

F32 = jnp.float32
BF16 = jnp.bfloat16
I32 = jnp.int32
U32 = jnp.uint32

EPS = 1e-6
N_GROUPS = 4
EXPERTS_PER_GROUP = 8
N_EXPERTS = N_GROUPS * EXPERTS_PER_GROUP
RG_C = 8.0

LANES = 128
SUBLANES = 8
EXPERT_LANE0 = N_GROUPS
ROUTE_ROWS = 256
IN_AHEAD = 4
IN_SLOTS = IN_AHEAD + 2
OUT_SLOTS = 4
SC_CORES = 2
SC_SUBCORES = 16
SC_CHUNK = 64
RING_SLOTS = 4
COMBINE_PARTS = 4
M_E0, M_E1, M_R0, M_R1, M_W0, M_W1 = 0, 1, 2, 3, 4, 5
VMEM_LIMIT = 48 * 1024 * 1024


def _cparams(*sem):
    return pltpu.CompilerParams(dimension_semantics=sem, vmem_limit_bytes=VMEM_LIMIT)


def _rms(x, g):
    return x * lax.rsqrt(jnp.mean(x * x, axis=-1, keepdims=True) + EPS) * g


def _dot(a, b):
    return jnp.dot(a, b, preferred_element_type=F32)


def _pack_bf16_pairs(x):
    k = x.shape[1] // 2
    hi = lax.bitcast_convert_type(x[:, :k].astype(BF16).astype(F32), U32)
    lo = lax.bitcast_convert_type(x[:, k:].astype(BF16).astype(F32), U32)
    return hi | (lo >> 16)


def _unpack_bf16_pairs(p):
    hi = lax.bitcast_convert_type(p & jnp.uint32(0xFFFF0000), F32)
    lo = lax.bitcast_convert_type(p << 16, F32)
    return hi, lo


def _tile(n, want):
    t = min(n, want)
    assert n % t == 0, (n, want)
    return t


def _mix_a_kernel(h_ref, g_ref, w_ref, cw_ref, z_ref, v_s, bg_s, last_s, *, tiles_per_seq):
    i = pl.program_id(0)
    tm, d = h_ref.shape

    @pl.when(i == 0)
    def _():
        v_s[...] = jnp.zeros_like(v_s)
        bg_s[...] = jnp.zeros_like(bg_s)
        last_s[...] = jnp.zeros_like(last_s)

    v = v_s[...].astype(F32)
    bg_old = bg_s[...]
    before = last_s[0:1, :]

    xn = _rms(h_ref[...], g_ref[...]).astype(BF16)
    cg = _dot(xn, w_ref[:, d:2 * d])
    hh = _dot(xn, w_ref[:, 2 * d:3 * d])
    v_new = (cg * hh).astype(BF16)
    bg = _dot(xn, w_ref[:, 0:d])

    old = i - 1
    at_start = (old % tiles_per_seq) == 0
    at_end = (old % tiles_per_seq) == tiles_per_seq - 1
    prev_row = jnp.where(at_start, 0.0, before)
    next_row = jnp.where(at_end, 0.0, v_new[0:1, :].astype(F32))
    row = lax.broadcasted_iota(I32, (tm, 1), 0)
    v_dn = jnp.where(row == 0, prev_row, pltpu.roll(v, 1, 0))
    v_up = jnp.where(row == tm - 1, next_row, pltpu.roll(v, tm - 1, 0))
    cw = cw_ref[...]
    u = cw[0:1, :] * v_dn + cw[1:2, :] * v + cw[2:3, :] * v_up
    z_ref[...] = (bg_old.astype(F32) * u).astype(BF16)

    last_s[0:1, :] = v[tm - 1:tm, :]
    v_s[...] = v_new
    bg_s[...] = bg.astype(BF16)


def _mix_a(h, g, w_in, conv_w, seq):
    n, d = h.shape
    tm = _tile(seq, 512)
    nt = n // tm
    return pl.pallas_call(
        functools.partial(_mix_a_kernel, tiles_per_seq=seq // tm),
        grid=(nt + 1,),
        in_specs=[pl.BlockSpec((tm, d), lambda i: (jnp.minimum(i, nt - 1), 0)),
                  pl.BlockSpec((1, d), lambda i: (0, 0)),
                  pl.BlockSpec((d, 3 * d), lambda i: (0, 0), pipeline_mode=pl.Buffered(1)),
                  pl.BlockSpec(conv_w.shape, lambda i: (0, 0))],
        out_specs=pl.BlockSpec((tm, d), lambda i: (jnp.maximum(i - 1, 0), 0)),
        out_shape=jax.ShapeDtypeStruct((n, d), BF16),
        scratch_shapes=[pltpu.VMEM((tm, d), BF16), pltpu.VMEM((tm, d), BF16),
                        pltpu.VMEM((SUBLANES, d), F32)],
        compiler_params=_cparams("arbitrary"),
        name="mix_a",
    )(h, g, w_in, conv_w)


def _mix_b_in_kernel(h_ref, hp_ref, hn_ref, g_ref, w_ref, yb_ref, u_ref, *, steps_per_seq, tt):
    i = pl.program_id(0)
    r = yb_ref.shape[1]
    tiles = h_ref.shape[0] // tt
    g = g_ref[...]
    xn = _rms(h_ref[...], g).astype(BF16)
    yb_ref[...] = jax.nn.gelu(_dot(xn, w_ref[:, 0:r])).astype(BF16)
    tm = h_ref.shape[0]
    xh = _rms(jnp.concatenate([hp_ref[...], hn_ref[...]], axis=0), g).astype(BF16)
    u_all = _dot(jnp.concatenate([xn, xh], axis=0), w_ref[:, r:2 * r])
    u = u_all[0:tm]
    uh = u_all[tm:tm + 2 * SUBLANES]
    at_start = (i % steps_per_seq) == 0
    at_end = (i % steps_per_seq) == steps_per_seq - 1
    prev2 = jnp.where(at_start, 0.0, uh[SUBLANES - 2:SUBLANES, :])
    next1 = jnp.where(at_end, 0.0, uh[SUBLANES:SUBLANES + 1, :])
    row = lax.broadcasted_iota(I32, (SUBLANES, 1), 0)
    for k in range(tiles):
        p2 = prev2 if k == 0 else u[k * tt - 2:k * tt]
        n1 = next1 if k == tiles - 1 else u[(k + 1) * tt:(k + 1) * tt + 1]
        pad = jnp.where(row == 0, p2[0:1], jnp.where(row == 1, p2[1:2], jnp.where(row == 2, n1, 0.0)))
        for s in range(r // LANES):
            lanes = slice(s * LANES, (s + 1) * LANES)
            u_ref[k, s, 0:tt, :] = u[k * tt:(k + 1) * tt, lanes]
            u_ref[k, s, tt:tt + SUBLANES, :] = pad[:, lanes]


def _mix_b_in(h, g, w_in, batch, seq, tt):
    n, d = h.shape
    r = w_in.shape[1] // 2
    tm = _tile(seq, 1024)
    steps_per_seq = seq // tm
    tiles = tm // tt
    hb = tm // SUBLANES
    nhalo = n // SUBLANES
    return pl.pallas_call(
        functools.partial(_mix_b_in_kernel, steps_per_seq=steps_per_seq, tt=tt),
        grid=(n // tm,),
        in_specs=[pl.BlockSpec((tm, d), lambda i: (i, 0)),
                  pl.BlockSpec((SUBLANES, d), lambda i: (jnp.maximum(i * hb - 1, 0), 0)),
                  pl.BlockSpec((SUBLANES, d), lambda i: (jnp.minimum((i + 1) * hb, nhalo - 1), 0)),
                  pl.BlockSpec((1, d), lambda i: (0, 0)),
                  pl.BlockSpec((d, 2 * r), lambda i: (0, 0), pipeline_mode=pl.Buffered(1))],
        out_specs=[pl.BlockSpec((tm, r), lambda i: (i, 0)),
                   pl.BlockSpec((tiles, r // LANES, None, tt + SUBLANES, LANES),
                                lambda i: (i % steps_per_seq, 0, i // steps_per_seq, 0, 0))],
        out_shape=[jax.ShapeDtypeStruct((n, r), BF16),
                   jax.ShapeDtypeStruct((seq // tt, r // LANES, batch, tt + SUBLANES, LANES), F32)],
        compiler_params=_cparams("parallel"),
        name="mix_b_in",
    )(h, h, h, g, w_in)


def _rglru_gates(u, wr_ref, br_ref, wi_ref, bi_ref, lam_ref):
    ub = u.astype(BF16)
    t_r = jnp.tanh(_dot(ub, wr_ref[...]) + 0.5 * br_ref[...])
    t_i = jnp.tanh(_dot(ub, wi_ref[...]) + 0.5 * bi_ref[...])
    c = (-0.25 * RG_C) * jax.nn.softplus(-lam_ref[...])
    th = jnp.tanh(c + c * t_r)
    q = 1.0 / (1.0 - th)
    nth = -th
    root = jnp.where(nth > 0.0, nth * lax.rsqrt(nth), 0.0)
    return (1.0 + th) * q, (q * root) * (u + u * t_i)


def _rglru_fwd_kernel(u_ref, cw_ref, cb_ref, wr_ref, br_ref, wi_ref, bi_ref, lam_ref,
                      hf_ref, uc_ref, x_s, carry_s, *, tt):
    nb = SUBLANES
    ttp = tt + SUBLANES
    rows = tt * nb
    slabs = u_ref.shape[0]

    @pl.when(pl.program_id(1) == 0)
    def _():
        carry_s[...] = jnp.zeros_like(carry_s)

    def gather_t(tl, dst):
        for s in range(slabs):
            x_s[pl.ds(dst, nb), s * LANES:(s + 1) * LANES] = u_ref[s, pl.ds(tl, nb, stride=ttp), :]

    for tl in range(tt):
        gather_t(tl, (tl + 2) * nb)
    gather_t(tt, 0)
    gather_t(tt + 1, nb)
    gather_t(tt + 2, (tt + 2) * nb)

    cw = cw_ref[...]
    u = (cw[0:1, :] * x_s[0:rows, :] + cw[1:2, :] * x_s[nb:nb + rows, :]
         + cw[2:3, :] * x_s[2 * nb:2 * nb + rows, :] + cw[3:4, :] * x_s[3 * nb:3 * nb + rows, :]
         + cb_ref[...])
    a, g = _rglru_gates(u, wr_ref, br_ref, wi_ref, bi_ref, lam_ref)
    h = carry_s[...]
    for tl in range(tt):
        r = slice(tl * nb, (tl + 1) * nb)
        back = slice((tt - 1 - tl) * nb, (tt - tl) * nb)
        h = a[r] * h + g[r]
        hf_ref[back, :] = h.astype(BF16)
        uc_ref[back, :] = u[r].astype(BF16)
    carry_s[...] = h


def _rglru_rev_kernel(uc_ref, wr_ref, br_ref, wi_ref, bi_ref, lam_ref, hf_ref, yb_ref,
                      z_ref, carry_s, nat_s, *, tt):
    nb = SUBLANES
    ttp = tt + SUBLANES
    slabs = nat_s.shape[0]

    @pl.when(pl.program_id(1) == 0)
    def _():
        carry_s[...] = jnp.zeros_like(carry_s)

    a, g = _rglru_gates(uc_ref[...].astype(F32), wr_ref, br_ref, wi_ref, bi_ref, lam_ref)
    h = carry_s[...]
    for k in range(tt):
        r = slice(k * nb, (k + 1) * nb)
        h = a[r] * h + g[r]
        hs = h + hf_ref[r, :].astype(F32)
        for s in range(slabs):
            nat_s[s, pl.ds(tt - 1 - k, nb, stride=ttp), :] = hs[:, s * LANES:(s + 1) * LANES]
    carry_s[...] = h
    for b in range(nb):
        for s in range(slabs):
            lanes = slice(s * LANES, (s + 1) * LANES)
            hb = nat_s[s, b * ttp:b * ttp + tt, :]
            z_ref[b, :, lanes] = (yb_ref[b, :, lanes].astype(F32) * hb).astype(BF16)


def _rglru(u_tiles, yb, conv_w, conv_b, w_r, b_r, w_i, b_i, lam, seq, tt):
    nt, nslab, batch, ttp, _ = u_tiles.shape
    r = nslab * LANES
    assert batch == SUBLANES, "time-major tile layout puts the batch on the 8 sublanes"
    nh, bk = w_r.shape[1], w_r.shape[2]
    slabs = bk // LANES
    kw = conv_w.shape[0]
    rows = tt * batch
    gates = [w_r, b_r.reshape(2, 1, r), w_i, b_i.reshape(2, 1, r), lam.reshape(2, 1, r)]

    def gate_specs(d):
        vec = pl.BlockSpec((None, 1, bk), lambda h, jj: (d, 0, h))
        mat = pl.BlockSpec((None, None, bk, bk), lambda h, jj: (d, h, 0, 0))
        return [mat, vec, mat, vec, vec]

    tm_spec = pl.BlockSpec((rows, bk), lambda h, jj: (jj, h))
    tm_rev = pl.BlockSpec((rows, bk), lambda h, jj: (nt - 1 - jj, h))
    tm_shape = jax.ShapeDtypeStruct((seq * batch, r), BF16)
    state = [pltpu.VMEM((batch, bk), F32)]

    hf, uc = pl.pallas_call(
        functools.partial(_rglru_fwd_kernel, tt=tt),
        grid=(nh, nt),
        in_specs=[pl.BlockSpec((None, slabs, batch * ttp, LANES), lambda h, jj: (jj, h, 0, 0)),
                  pl.BlockSpec((kw, bk), lambda h, jj: (0, h)),
                  pl.BlockSpec((1, bk), lambda h, jj: (0, h))] + gate_specs(0),
        out_specs=[tm_spec, tm_spec],
        out_shape=[tm_shape, tm_shape],
        scratch_shapes=[pltpu.VMEM(((tt + 3) * batch, bk), F32)] + state,
        compiler_params=_cparams("parallel", "arbitrary"),
        name="rglru_fwd",
    )(u_tiles.reshape(nt, nslab, batch * ttp, LANES), conv_w, conv_b, *gates)

    nat_spec = pl.BlockSpec((batch, tt, bk), lambda h, jj: (0, nt - 1 - jj, h))
    z = pl.pallas_call(
        functools.partial(_rglru_rev_kernel, tt=tt),
        grid=(nh, nt),
        in_specs=[tm_rev] + gate_specs(1) + [tm_rev, nat_spec],
        out_specs=nat_spec,
        out_shape=jax.ShapeDtypeStruct((batch, seq, r), BF16),
        scratch_shapes=state + [pltpu.VMEM((slabs, batch * ttp, LANES), F32)],
        compiler_params=_cparams("parallel", "arbitrary"),
        name="rglru_rev",
    )(uc, *gates, hf, yb.reshape(batch, seq, r))
    return z.reshape(-1, r)


def _route(logits):
    lane = lax.broadcasted_iota(I32, logits.shape, 1).astype(F32)
    neg = jnp.float32(-jnp.inf)
    nolane = jnp.float32(LANES)
    gmask = lane < N_GROUPS
    gmax = jnp.max(jnp.where(gmask, logits, neg), axis=1, keepdims=True)
    eg = jnp.where(gmask, jnp.exp(logits - gmax), 0.0)
    gsum = jnp.sum(eg, axis=1, keepdims=True)
    pg = eg / gsum
    pg_top = 1.0 / gsum
    g_top = jnp.min(jnp.where(gmask & (pg == pg_top), lane, nolane), axis=1, keepdims=True)
    lo = EXPERT_LANE0 + g_top * EXPERTS_PER_GROUP
    emask = (lane >= lo) & (lane < lo + EXPERTS_PER_GROUP)
    emax = jnp.max(jnp.where(emask, logits, neg), axis=1, keepdims=True)
    ee = jnp.where(emask, jnp.exp(logits - emax), 0.0)
    esum = jnp.sum(ee, axis=1, keepdims=True)
    pe = jnp.where(emask, ee / esum, -1.0)
    p1 = 1.0 / esum
    i1 = jnp.min(jnp.where(pe == p1, lane, nolane), axis=1, keepdims=True)
    pe2 = jnp.where(lane == i1, -1.0, pe)
    p2 = jnp.max(pe2, axis=1, keepdims=True)
    i2 = jnp.min(jnp.where(pe2 == p2, lane, nolane), axis=1, keepdims=True)
    psum = p1 + p2
    return i1, i2, pg_top * (p1 / psum), pg_top * (p2 / psum)


def _post_mix_kernel(z_hbm, wo_ref, h_hbm, g_ref, wr_ref, br_ref,
                     h1_ref, xn_ref, meta_t_ref, cnt_ref, carry_ref, logits_s,
                     z_ring, h_ring, sem, *, nt):
    i = pl.program_id(0)
    tm = z_ring.shape[1]

    def tile_copies(t):
        slot = t % RING_SLOTS
        rows = pl.ds(pl.multiple_of(t * tm, tm), tm)
        return (pltpu.make_async_copy(z_hbm.at[rows], z_ring.at[slot], sem.at[0, slot]),
                pltpu.make_async_copy(h_hbm.at[rows], h_ring.at[slot], sem.at[1, slot]))

    def start_tile(t):
        for c in tile_copies(t):
            c.start()

    @pl.when(i == 0)
    def _():
        logits_s[...] = jnp.zeros_like(logits_s)
        for t in range(min(RING_SLOTS - 1, nt)):
            start_tile(t)

    @pl.when(i + RING_SLOTS - 1 < nt)
    def _():
        start_tile(i + RING_SLOTS - 1)

    @pl.when(i < nt)
    def _():
        for c in tile_copies(i):
            c.wait()

    slot = jnp.minimum(i, nt - 1) % RING_SLOTS
    z_ref = z_ring.at[slot]
    h_ref = h_ring.at[slot]

    @pl.when(i <= 1)
    def _():
        carry_ref[...] = jnp.zeros_like(carry_ref)

    logits = logits_s[...]

    h1 = h_ref[...] + _dot(z_ref[...], wo_ref[...])
    h1_ref[...] = h1
    xn = _rms(h1, g_ref[...])
    xn_ref[...] = _pack_bf16_pairs(xn)
    logits_s[...] = _dot(xn.astype(BF16), wr_ref[...].astype(BF16)) + br_ref[...]

    i1, i2, w0, w1 = _route(logits)

    lane = lax.broadcasted_iota(I32, (tm, LANES), 1).astype(F32)
    oh0 = lane == i1
    oh1 = lane == i2
    both = (oh0 | oh1).astype(BF16)
    ri = lax.broadcasted_iota(I32, (tm, tm), 0)
    ci = lax.broadcasted_iota(I32, (tm, tm), 1)
    before = (ri > ci).astype(BF16)
    cnt_before = _dot(before, both) + carry_ref[...]
    rank0 = jnp.sum(jnp.where(oh0, cnt_before, 0.0), axis=1, keepdims=True)
    rank1 = jnp.sum(jnp.where(oh1, cnt_before, 0.0), axis=1, keepdims=True)
    carry = carry_ref[...] + jnp.sum(both.astype(F32), axis=0, keepdims=True)
    carry_ref[...] = carry
    cnt_ref[...] = carry

    e0 = i1 - EXPERT_LANE0
    e1 = i2 - EXPERT_LANE0
    meta = jnp.zeros((tm, LANES), F32)
    for ln, val in ((M_E0, e0), (M_E1, e1), (M_R0, rank0), (M_R1, rank1), (M_W0, w0), (M_W1, w1)):
        meta = jnp.where(lane == ln, val, meta)
    meta_t_ref[...] = meta.T[0:SUBLANES, :]


def _post_mix(z, w_out, h, g, w_router, b_router):
    n, d = h.shape
    k = z.shape[1]
    tm = _tile(n, 512)
    nt = n // tm

    def proj(i):
        return (jnp.minimum(i, nt - 1), 0)

    def routed(i):
        return (0, jnp.maximum(i - 1, 0))

    return pl.pallas_call(
        functools.partial(_post_mix_kernel, nt=nt),
        grid=(nt + 1,),
        in_specs=[pl.BlockSpec(memory_space=pl.ANY),
                  pl.BlockSpec((k, d), lambda i: (0, 0)),
                  pl.BlockSpec(memory_space=pl.ANY),
                  pl.BlockSpec((1, d), lambda i: (0, 0)),
                  pl.BlockSpec((d, LANES), lambda i: (0, 0)),
                  pl.BlockSpec((1, LANES), lambda i: (0, 0))],
        out_specs=[pl.BlockSpec((tm, d), proj),
                   pl.BlockSpec((tm, d // 2), proj),
                   pl.BlockSpec((SUBLANES, tm), routed),
                   pl.BlockSpec((1, LANES), lambda i: (0, 0))],
        out_shape=[jax.ShapeDtypeStruct((n, d), F32), jax.ShapeDtypeStruct((n, d // 2), U32),
                   jax.ShapeDtypeStruct((SUBLANES, n), F32),
                   jax.ShapeDtypeStruct((1, LANES), F32)],
        scratch_shapes=[pltpu.VMEM((1, LANES), F32), pltpu.VMEM((tm, LANES), F32),
                        pltpu.VMEM((RING_SLOTS, tm, k), z.dtype),
                        pltpu.VMEM((RING_SLOTS, tm, d), F32),
                        pltpu.SemaphoreType.DMA((2, RING_SLOTS))],
        compiler_params=_cparams("arbitrary"),
        name="post_mix_router",
    )(z, w_out, h, g, w_router, b_router)


def _plan_kernel(cnt_ref, meta_t_ref, dest_ref, estart_ref, ecount_ref, nblk_ref, pstart_ref):
    def per_expert(e, acc):
        nb_e = (cnt_ref[e] + (ROUTE_ROWS - 1)) // ROUTE_ROWS
        pstart_ref[e] = acc * ROUTE_ROWS
        estart_ref[e] = acc
        ecount_ref[e] = nb_e
        return acc + nb_e

    nblk_ref[0] = lax.fori_loop(0, N_EXPERTS, per_expert, 0)

    e0 = meta_t_ref[M_E0:M_E0 + 1, :]
    e1 = meta_t_ref[M_E1:M_E1 + 1, :]
    d0 = meta_t_ref[M_R0:M_R0 + 1, :]
    d1 = meta_t_ref[M_R1:M_R1 + 1, :]
    for e in range(N_EXPERTS):
        ps = pstart_ref[e].astype(F32)
        d0 = d0 + jnp.where(e0 == e, ps, 0.0)
        d1 = d1 + jnp.where(e1 == e, ps, 0.0)
    dest_ref[...] = jnp.zeros_like(dest_ref)
    dest_ref[0:1, :] = d0.astype(I32)
    dest_ref[1:2, :] = d1.astype(I32)


def _plan(cnt_i32, meta_t):
    n = meta_t.shape[1]
    smem = pl.BlockSpec(memory_space=pltpu.SMEM)
    return pl.pallas_call(
        _plan_kernel,
        in_specs=[smem, pl.BlockSpec((SUBLANES, n), lambda: (0, 0))],
        out_specs=[pl.BlockSpec((SUBLANES, n), lambda: (0, 0)), smem, smem, smem],
        out_shape=[jax.ShapeDtypeStruct((SUBLANES, n), I32),
                   jax.ShapeDtypeStruct((N_EXPERTS,), I32),
                   jax.ShapeDtypeStruct((N_EXPERTS,), I32),
                   jax.ShapeDtypeStruct((1,), I32)],
        scratch_shapes=[pltpu.SMEM((N_EXPERTS,), I32)],
        name="dispatch_plan",
    )(cnt_i32, meta_t)


def _sc_mesh():
    return plsc.VectorSubcoreMesh(core_axis_name="c", subcore_axis_name="s",
                                  num_cores=SC_CORES, num_subcores=SC_SUBCORES)


def _sc_scatter_rows(x, idx0, idx1, p_rows):
    m, k = x.shape
    per_w = m // (SC_CORES * SC_SUBCORES)
    assert per_w % SC_CHUNK == 0 and per_w * SC_CORES * SC_SUBCORES == m

    nchunk = per_w // SC_CHUNK
    buf = [pltpu.VMEM((SC_CHUNK,), I32), pltpu.VMEM((SC_CHUNK,), I32),
           pltpu.VMEM((SC_CHUNK, k), x.dtype),
           pltpu.SemaphoreType.DMA, pltpu.SemaphoreType.DMA, pltpu.SemaphoreType.DMA]

    @functools.partial(
        pl.kernel, mesh=_sc_mesh(), out_type=jax.ShapeDtypeStruct((p_rows, k), x.dtype),
        scratch_types=buf + buf, name="sc_scatter_rows")
    def scatter(x_hbm, i0_hbm, i1_hbm, out_hbm, *scratch):
        base = (lax.axis_index("s") * SC_CORES + lax.axis_index("c")) * per_w
        slots = (scratch[:6], scratch[6:])

        def loads(c):
            i0_v, i1_v, rows_v, lsem, _, _ = slots[c % 2]
            off = pl.multiple_of(base + c * SC_CHUNK, SC_CHUNK)
            return [pltpu.make_async_copy(i0_hbm.at[pl.ds(off, SC_CHUNK)], i0_v, lsem),
                    pltpu.make_async_copy(i1_hbm.at[pl.ds(off, SC_CHUNK)], i1_v, lsem),
                    pltpu.make_async_copy(x_hbm.at[pl.ds(off, SC_CHUNK)], rows_v, lsem)]

        def puts(c):
            i0_v, i1_v, rows_v, _, psem0, psem1 = slots[c % 2]
            return [pltpu.make_async_copy(rows_v, out_hbm.at[i0_v], psem0),
                    pltpu.make_async_copy(rows_v, out_hbm.at[i1_v], psem1)]

        for d in loads(0):
            d.start()
        for c in range(nchunk):
            for d in loads(c):
                d.wait()
            for d in puts(c):
                d.start()
            if c >= 1:
                for d in puts(c - 1):
                    d.wait()
            if c + 1 < nchunk:
                for d in loads(c + 1):
                    d.start()
        for d in puts(nchunk - 1):
            d.wait()

    return scatter(x, idx0, idx1)


def _expert_kernel(es_ref, ec_ref, nblk_ref, cnt_ref, xs_ref, wg_ref, wu_ref, wd_ref, ys_ref,
                   x_s, y_s, wg_s, wu_s, wd_s, sem):
    e = pl.program_id(0)
    nb = ec_ref[e]
    b0 = es_ref[e]
    k = x_s.shape[2]
    nb_max = ys_ref.shape[0] // ROUTE_ROWS

    def rows(b):
        return pl.ds(pl.multiple_of(b * ROUTE_ROWS, ROUTE_ROWS), ROUTE_ROWS)

    nblk = nblk_ref[0]

    def copy_in(b):
        slot = lax.rem(b, IN_SLOTS)
        return pltpu.make_async_copy(xs_ref.at[rows(b)], x_s.at[slot], sem.at[0, slot])

    def copy_out(b):
        slot = lax.rem(b, OUT_SLOTS)
        return pltpu.make_async_copy(y_s.at[slot], ys_ref.at[rows(b)], sem.at[1, slot])

    @pl.when(e == 0)
    def _():
        for b in range(IN_AHEAD):
            @pl.when(b < nblk)
            def _(b=b):
                copy_in(b).start()

    @pl.when(nb > 0)
    def _():
        wg_s[...] = wg_ref[...].astype(BF16)
        wu_s[...] = wu_ref[...].astype(BF16)
        wd_s[...] = wd_ref[...].astype(BF16)

        def mlp(b, j, width):
            row = lax.broadcasted_iota(I32, (ROUTE_ROWS, 1), 0)
            xp = jnp.concatenate(
                [jnp.where(row < cnt_ref[e] - (j + i) * ROUTE_ROWS,
                           x_s[lax.rem(b + i, IN_SLOTS)], jnp.uint32(0)) for i in range(width)], axis=0)
            x_hi, x_lo = _unpack_bf16_pairs(xp)
            x_hi = x_hi.astype(BF16)
            x_lo = x_lo.astype(BF16)
            g = _dot(x_hi, wg_s[0:k, :]) + _dot(x_lo, wg_s[k:2 * k, :])
            u = _dot(x_hi, wu_s[0:k, :]) + _dot(x_lo, wu_s[k:2 * k, :])
            hb = (jax.nn.silu(g) * u).astype(BF16)
            y = _pack_bf16_pairs(_dot(hb, wd_s[...]))
            return [y[i * ROUTE_ROWS:(i + 1) * ROUTE_ROWS] for i in range(width)]

        def unit(j, width):
            b = b0 + j
            for i in range(width):
                copy_in(b + i).wait()
            for i in range(width):
                @pl.when(b + i + IN_AHEAD < nblk)
                def _(i=i):
                    copy_in(b + i + IN_AHEAD).start()
            ys = mlp(b, j, width)
            for i in range(width):
                @pl.when(b + i >= OUT_SLOTS)
                def _(i=i):
                    copy_out(b + i - OUT_SLOTS).wait()
            for i in range(width):
                y_s[lax.rem(b + i, OUT_SLOTS)] = ys[i]
                copy_out(b + i).start()

        def pair(jj, c):
            unit(2 * jj, 2)
            return c

        lax.fori_loop(0, nb // 2, pair, 0)

        @pl.when(nb % 2 == 1)
        def _():
            unit(nb - 1, 1)

    @pl.when(e == pl.num_programs(0) - 1)
    def _():
        for back in range(OUT_SLOTS, 0, -1):
            @pl.when(nblk >= back)
            def _(back=back):
                copy_out(nblk - back).wait()

    @pl.when(e == pl.num_programs(0) - 1)
    def _():
        y_s[0] = jnp.zeros_like(y_s[0])

        def zero(b):
            return pltpu.make_async_copy(y_s.at[0], ys_ref.at[rows(b)], sem.at[1, 0])

        def start_zero(b, c):
            zero(b).start()
            return c

        def wait_zero(b, c):
            zero(b).wait()
            return c

        lax.fori_loop(nblk_ref[0], nb_max, start_zero, 0)
        lax.fori_loop(nblk_ref[0], nb_max, wait_zero, 0)


def _experts(estart, ecount, nblk, cnt, xs, w_gate, w_up, w_down, layer):
    p, k = xs.shape
    d, f = w_gate.shape[2], w_gate.shape[3]

    def w_map(e, es, ec, nb, cn):
        return (layer, e, 0, 0)

    grid_spec = pltpu.PrefetchScalarGridSpec(
        num_scalar_prefetch=4,
        grid=(N_EXPERTS,),
        in_specs=[pl.BlockSpec(memory_space=pl.ANY),
                  pl.BlockSpec((None, None, d, f), w_map),
                  pl.BlockSpec((None, None, d, f), w_map),
                  pl.BlockSpec((None, None, f, d), w_map)],
        out_specs=pl.BlockSpec(memory_space=pl.ANY),
        scratch_shapes=[pltpu.VMEM((IN_SLOTS, ROUTE_ROWS, k), U32),
                        pltpu.VMEM((OUT_SLOTS, ROUTE_ROWS, k), U32),
                        pltpu.VMEM((d, f), BF16), pltpu.VMEM((d, f), BF16), pltpu.VMEM((f, d), BF16),
                        pltpu.SemaphoreType.DMA((2, max(IN_SLOTS, OUT_SLOTS)))],
    )
    return pl.pallas_call(
        _expert_kernel,
        grid_spec=grid_spec,
        out_shape=jax.ShapeDtypeStruct((p, k), U32),
        compiler_params=_cparams("arbitrary"),
        name="expert_mlp",
    )(estart, ecount, nblk, cnt, xs, w_gate, w_up, w_down)


def _sc_gather_pair(table, idx0, idx1, start, count):
    k = table.shape[1]
    per_w = count // (SC_CORES * SC_SUBCORES)
    assert per_w % SC_CHUNK == 0 and per_w * SC_CORES * SC_SUBCORES == count
    assert start % SC_CHUNK == 0
    rows = jax.ShapeDtypeStruct((count, k), table.dtype)

    @functools.partial(
        pl.kernel, mesh=_sc_mesh(), out_type=(rows, rows),
        scratch_types=[pltpu.VMEM((SC_CHUNK,), I32), pltpu.VMEM((SC_CHUNK,), I32),
                       pltpu.VMEM((SC_CHUNK, k), table.dtype), pltpu.VMEM((SC_CHUNK, k), table.dtype),
                       pltpu.SemaphoreType.DMA, pltpu.SemaphoreType.DMA,
                       pltpu.SemaphoreType.DMA, pltpu.SemaphoreType.DMA],
        name="sc_gather_pair")
    def gather(table_hbm, i0_hbm, i1_hbm, o0_hbm, o1_hbm, i0_v, i1_v, r0_v, r1_v,
               sem0, sem1, sem2, sem3):
        base = (lax.axis_index("s") * SC_CORES + lax.axis_index("c")) * per_w

        @pl.loop(0, per_w // SC_CHUNK)
        def _(c):
            dst = pl.multiple_of(base + c * SC_CHUNK, SC_CHUNK)
            src = pl.multiple_of(start + dst, SC_CHUNK)
            pltpu.sync_copy(i0_hbm.at[pl.ds(src, SC_CHUNK)], i0_v)
            pltpu.sync_copy(i1_hbm.at[pl.ds(src, SC_CHUNK)], i1_v)
            get0 = pltpu.async_copy(table_hbm.at[i0_v], r0_v, sem0)
            get1 = pltpu.async_copy(table_hbm.at[i1_v], r1_v, sem1)
            get0.wait()
            put0 = pltpu.async_copy(r0_v, o0_hbm.at[pl.ds(dst, SC_CHUNK)], sem2)
            get1.wait()
            put1 = pltpu.async_copy(r1_v, o1_hbm.at[pl.ds(dst, SC_CHUNK)], sem3)
            put0.wait()
            put1.wait()

    return gather(table, idx0, idx1)


def _combine_kernel(*refs, final, first, tiles):
    (ya_hbm, yb_hbm, h1_hbm, meta_ref, p_ref, gp_ref, wpg_ref, bpg_ref, wp_ref,
     gf_ref) = refs[:10]
    out_ref, ya_ring, yb_ring, h1_ring, sem = refs[-5:]
    i = pl.program_id(0)
    tmc = h1_ring.shape[1]

    def tile_copies(t):
        slot = t % RING_SLOTS
        part_rows = pl.ds(pl.multiple_of(t * tmc, tmc), tmc)
        rows = pl.ds(pl.multiple_of((first + t) * tmc, tmc), tmc)
        return (pltpu.make_async_copy(ya_hbm.at[part_rows], ya_ring.at[slot], sem.at[0, slot]),
                pltpu.make_async_copy(yb_hbm.at[part_rows], yb_ring.at[slot], sem.at[1, slot]),
                pltpu.make_async_copy(h1_hbm.at[rows], h1_ring.at[slot], sem.at[2, slot]))

    @pl.when(i == 0)
    def _():
        for t in range(min(RING_SLOTS - 1, tiles)):
            for c in tile_copies(t):
                c.start()

    @pl.when(i + RING_SLOTS - 1 < tiles)
    def _():
        for c in tile_copies(i + RING_SLOTS - 1):
            c.start()

    for c in tile_copies(i):
        c.wait()
    slot = i % RING_SLOTS
    ya_ref, yb_ref, h1_ref = ya_ring.at[slot], yb_ring.at[slot], h1_ring.at[slot]
    meta_t = meta_ref[...]
    pad = jnp.zeros((LANES - SUBLANES, meta_t.shape[1]), F32)
    meta = jnp.concatenate([meta_t, pad], axis=0).T
    w0 = meta[:, M_W0:M_W0 + 1]
    w1 = meta[:, M_W1:M_W1 + 1]
    ya_hi, ya_lo = _unpack_bf16_pairs(ya_ref[...])
    yb_hi, yb_lo = _unpack_bf16_pairs(yb_ref[...])
    y = jnp.concatenate([ya_hi * w0 + yb_hi * w1, ya_lo * w0 + yb_lo * w1], axis=1)
    h2 = h1_ref[...] + y
    xn = _rms(h2, gp_ref[...]).astype(BF16)
    gate = jax.nn.sigmoid(_dot(xn, wpg_ref[...]) + bpg_ref[...])
    pp = _dot(p_ref[...].astype(BF16), wp_ref[...])
    h3 = h2 + gate * pp
    if final:
        h3 = _rms(h3, gf_ref[...])
    out_ref[...] = h3


def _combine(dest, ys, h1, meta_t, p, layer, g_ple, w_ple_gate, b_ple_gate, w_ple, g_final, final):
    n, d = h1.shape
    pd = p.shape[2]
    k = ys.shape[1]
    n_part = n // COMBINE_PARTS
    tmc = _tile(n_part, 512)
    tiles = n_part // tmc
    vec = pl.BlockSpec((1, d), lambda i: (0, 0))
    out = None
    for q in range(COMBINE_PARTS):
        ya, yb = _sc_gather_pair(ys, dest[0], dest[1], q * n_part, n_part)

        def tok(i, q=q):
            return (q * tiles + i, 0)

        in_specs = [pl.BlockSpec(memory_space=pl.ANY),
                    pl.BlockSpec(memory_space=pl.ANY),
                    pl.BlockSpec(memory_space=pl.ANY),
                    pl.BlockSpec((SUBLANES, tmc), lambda i, q=q: (0, q * tiles + i)),
                    pl.BlockSpec((None, tmc, pd), lambda i, q=q: (layer, q * tiles + i, 0)),
                    vec,
                    pl.BlockSpec((d, d), lambda i: (0, 0)),
                    vec,
                    pl.BlockSpec((pd, d), lambda i: (0, 0)),
                    vec]
        args = [ya, yb, h1, meta_t, p, g_ple, w_ple_gate, b_ple_gate, w_ple, g_final]
        aliases = {}
        if out is not None:
            in_specs.append(pl.BlockSpec(memory_space=pl.ANY))
            args.append(out)
            aliases = {len(args) - 1: 0}
        out = pl.pallas_call(
            functools.partial(_combine_kernel, final=final, first=q * tiles, tiles=tiles),
            grid=(tiles,),
            in_specs=in_specs,
            out_specs=pl.BlockSpec((tmc, d), tok),
            out_shape=jax.ShapeDtypeStruct((n, d), F32),
            input_output_aliases=aliases,
            scratch_shapes=[pltpu.VMEM((RING_SLOTS, tmc, k), ys.dtype),
                            pltpu.VMEM((RING_SLOTS, tmc, k), ys.dtype),
                            pltpu.VMEM((RING_SLOTS, tmc, d), F32),
                            pltpu.SemaphoreType.DMA((3, RING_SLOTS))],
            compiler_params=_cparams("arbitrary"),
            name="combine_ple",
        )(*args)
    return out


def _row(v):
    return v.reshape(1, -1)


def _moe_and_ple(z, w_out, h, i, final, norm_ffn, w_router_group, b_router_group, w_router_expert,
                 b_router_expert, w_exp_gate, w_exp_up, w_exp_down, norm_ple, w_ple, w_ple_gate,
                 b_ple_gate, norm_final, p):
    n, d = h.shape
    pad = LANES - N_GROUPS - N_EXPERTS
    w_router = jnp.concatenate(
        [w_router_group[i], w_router_expert[i], jnp.zeros((d, pad), F32)], axis=1)
    b_router = jnp.concatenate(
        [b_router_group[i], b_router_expert[i], jnp.zeros((pad,), F32)]).reshape(1, LANES)
    h1, xn, meta_t, cnt = _post_mix(z, w_out, h, _row(norm_ffn[i]), w_router, b_router)
    cnt_i32 = cnt[0, EXPERT_LANE0:EXPERT_LANE0 + N_EXPERTS].astype(I32)
    nb_max = (2 * n) // ROUTE_ROWS + N_EXPERTS
    dest, estart, ecount, nblk = _plan(cnt_i32, meta_t)
    xs = _sc_scatter_rows(xn, dest[0], dest[1], nb_max * ROUTE_ROWS)
    ys = _experts(estart, ecount, nblk, cnt_i32, xs, w_exp_gate, w_exp_up, w_exp_down, i)
    return _combine(dest, ys, h1, meta_t, p.reshape(p.shape[0], n, -1), i, _row(norm_ple[i]),
                    w_ple_gate[i].astype(BF16), _row(b_ple_gate[i]), w_ple[i].astype(BF16),
                    _row(norm_final), final)


def kernel(x, p, norm_mix, w_in_a, conv_a, w_out_a, w_in_b, conv_b, conv_bias_b, w_rgate_b, b_rgate_b, w_igate_b, b_igate_b, lam_b, w_out_b, norm_ffn, w_router_group, b_router_group, w_router_expert, b_router_expert, w_exp_gate, w_exp_up, w_exp_down, norm_ple, w_ple, w_ple_gate, b_ple_gate, norm_final):
    batch, seq, d = x.shape
    depth = p.shape[0]
    n = batch * seq
    h = x.reshape(n, d)
    for i in range(depth):
        j = i // 2
        if i % 2 == 0:
            z = _mix_a(h, _row(norm_mix[i]), w_in_a[j].astype(BF16), conv_a[j], seq)
            w_out = w_out_a[j]
        else:
            tt = _tile(seq, 512)
            yb, u_tiles = _mix_b_in(h, _row(norm_mix[i]), w_in_b[j].astype(BF16), batch, seq, tt)
            z = _rglru(u_tiles, yb, conv_b[j], _row(conv_bias_b[j]),
                       (0.5 * w_rgate_b[j]).astype(BF16), b_rgate_b[j],
                       (0.5 * w_igate_b[j]).astype(BF16),
                       b_igate_b[j], lam_b[j], seq, tt)
            w_out = w_out_b[j]
        h = _moe_and_ple(z, w_out.astype(BF16), h, i, i == depth - 1, norm_ffn, w_router_group,
                         b_router_group, w_router_expert, b_router_expert, w_exp_gate, w_exp_up,
                         w_exp_down, norm_ple, w_ple, w_ple_gate, b_ple_gate, norm_final, p)
    return h.reshape(batch, seq, d)
```
